```python
import jax, jax.numpy as jnp
from jax import lax
import numpy as np

D_MODEL = 1024
BATCH = 8
SEQ = 8192
DEPTH = 2

MIX_WIDTH = D_MODEL
GM_WIDTH = MIX_WIDTH // 4
GM_HEADS = 4
GM_HEAD_DIM = GM_WIDTH // GM_HEADS
GM_CHUNK = 128
RET_WIDTH = MIX_WIDTH // 2
RET_HEADS = 4
RET_HEAD_DIM = RET_WIDTH // RET_HEADS
RET_CHUNK = 128
CONV_WIDTH = MIX_WIDTH - GM_WIDTH - RET_WIDTH
CONV_KERNEL = 31
IN_WIDTH = 2 * GM_WIDTH + 4 * RET_WIDTH + 2 * CONV_WIDTH
FFN_HIDDEN = ((8 * D_MODEL // 3 + 255) // 256) * 256
ROPE_BASE = 10000.0
EPS = 1e-6

kernel_name = 'hybrid_gmlp_retention_conformer_encoder'


def _rmsnorm(x, g):
    x32 = x.astype(jnp.float32)
    y = x32 * lax.rsqrt(jnp.mean(x32 * x32, axis=-1, keepdims=True) + EPS)
    return (y * g.astype(jnp.float32)).astype(x.dtype)


def _standardize(x):
    x32 = x.astype(jnp.float32)
    mu = jnp.mean(x32, axis=-1, keepdims=True)
    var = jnp.mean(jnp.square(x32 - mu), axis=-1, keepdims=True)
    return (x32 - mu) * lax.rsqrt(var + EPS)


def _layernorm(x, g, b):
    return (_standardize(x) * g.astype(jnp.float32) + b.astype(jnp.float32)).astype(x.dtype)


def _spatial_gating(u, v, ln_g, ln_b, w_s, b_s):
    bsz, seq, _ = v.shape
    n = seq // GM_CHUNK
    v = _layernorm(v, ln_g, ln_b).reshape(bsz, n, GM_CHUNK, GM_HEADS, GM_HEAD_DIM)
    mixed = jnp.einsum('hpq,bnqhd->bnphd', w_s, v) + b_s.T[:, :, None]
    return (u.reshape(mixed.shape) * mixed).reshape(bsz, seq, GM_WIDTH)


def _rotary(t, cos, sin):
    t1, t2 = jnp.split(t, 2, axis=-1)
    return jnp.concatenate([t1 * cos - t2 * sin, t1 * sin + t2 * cos], axis=-1)


def _retention_direction(q, k, v, gamma, include_diag):
    dt = q.dtype
    idx = jnp.arange(RET_CHUNK, dtype=jnp.float32)
    log_g = jnp.log(gamma)[:, None]
    diff = idx[:, None] - idx[None, :]
    mask = (diff >= 0) if include_diag else (diff > 0)
    d_intra = jnp.where(mask, jnp.exp(log_g[:, :, None] * jnp.where(mask, diff, 0.0)), 0.0).astype(dt)
    zeta = jnp.exp(log_g * (RET_CHUNK - 1 - idx)).astype(dt)
    xi = jnp.exp(log_g * (idx + 1)).astype(dt)
    gamma_c = jnp.exp(log_g * RET_CHUNK).astype(dt)[:, :, None]
    scores = jnp.einsum('bhncd,bhnmd->bhncm', q, k) * d_intra[:, None]
    intra = jnp.einsum('bhncm,bhnme->bhnce', scores, v)
    kv = jnp.einsum('bhnmd,bhnme,hm->nbhde', k, v, zeta)

    def step(state, kv_n):
        return gamma_c * state + kv_n, state

    _, prev = lax.scan(step, jnp.zeros_like(kv[0]), kv)
    cross = jnp.einsum('bhncd,nbhde,hc->bhnce', q, prev, xi)
    return intra + cross


def _retention(q, k, v, g, cos, sin):
    bsz, seq, _ = q.shape
    n = seq // RET_CHUNK
    shp = (bsz, seq, RET_HEADS, RET_HEAD_DIM)
    q = _rotary(q.reshape(shp), cos, sin)
    k = _rotary(k.reshape(shp), cos, sin) * (RET_HEAD_DIM ** -0.5)
    v = v.reshape(shp)

    def chunk(t):
        return t.reshape(bsz, n, RET_CHUNK, RET_HEADS, RET_HEAD_DIM).transpose(0, 3, 1, 2, 4)

    def unchunk(t):
        return t.transpose(0, 2, 3, 1, 4).reshape(shp)

    def rev(t):
        return jnp.flip(t, axis=1)

    gamma_fwd = 1.0 - jnp.exp2(-5.0 - jnp.arange(RET_HEADS, dtype=jnp.float32))
    gamma_bwd = gamma_fwd[::-1]
    fwd = unchunk(_retention_direction(chunk(q), chunk(k), chunk(v), gamma_fwd, True))
    bwd = rev(unchunk(_retention_direction(chunk(rev(q)), chunk(rev(k)), chunk(rev(v)), gamma_bwd, False)))
    o = _standardize(fwd + bwd).astype(q.dtype)
    return o.reshape(bsz, seq, RET_WIDTH) * jax.nn.silu(g)


def _conformer_conv(a, gate, dw_w, dw_b, ln_g, ln_b):
    h = a * jax.nn.sigmoid(gate)
    pad = CONV_KERNEL // 2
    h = lax.conv_general_dilated(h, dw_w[:, None, :], window_strides=(1,), padding=[(pad, pad)],
                                 dimension_numbers=('NWC', 'WIO', 'NWC'),
                                 feature_group_count=CONV_WIDTH) + dw_b
    return jax.nn.silu(_layernorm(h, ln_g, ln_b))


def _fwd_setup_inputs(seed: int = 0) -> dict:
    key = jax.random.key(seed)
    ks = jax.random.split(key, 16)
    f32 = jnp.float32

    def nrm(k, shape, scale):
        return jax.random.normal(k, shape, f32) * scale

    return {
        'x': nrm(ks[0], (BATCH, SEQ, D_MODEL), 1.0),
        'norm1_g': 1.0 + nrm(ks[1], (DEPTH, D_MODEL), 0.02),
        'w_in': nrm(ks[2], (DEPTH, D_MODEL, IN_WIDTH), D_MODEL ** -0.5),
        'gm_ln_g': 1.0 + nrm(ks[3], (DEPTH, GM_WIDTH), 0.02),
        'gm_ln_b': nrm(ks[4], (DEPTH, GM_WIDTH), 0.02),
        'gm_ws': nrm(ks[5], (DEPTH, GM_HEADS, GM_CHUNK, GM_CHUNK), GM_CHUNK ** -0.5),
        'gm_bs': 1.0 + nrm(ks[6], (DEPTH, GM_HEADS, GM_CHUNK), 0.02),
        'conv_w': nrm(ks[7], (DEPTH, CONV_KERNEL, CONV_WIDTH), CONV_KERNEL ** -0.5),
        'conv_b': nrm(ks[8], (DEPTH, CONV_WIDTH), 0.02),
        'conv_ln_g': 1.0 + nrm(ks[9], (DEPTH, CONV_WIDTH), 0.02),
        'conv_ln_b': nrm(ks[10], (DEPTH, CONV_WIDTH), 0.02),
        'w_out': nrm(ks[11], (DEPTH, MIX_WIDTH, D_MODEL), MIX_WIDTH ** -0.5),
        'norm2_g': 1.0 + nrm(ks[12], (DEPTH, D_MODEL), 0.02),
        'w_ffn_in': nrm(ks[13], (DEPTH, D_MODEL, 2 * FFN_HIDDEN), D_MODEL ** -0.5),
        'w_ffn_out': nrm(ks[14], (DEPTH, FFN_HIDDEN, D_MODEL), FFN_HIDDEN ** -0.5),
        'final_g': 1.0 + nrm(ks[15], (D_MODEL,), 0.02),
    }


def _fwd_reference(x, norm1_g, w_in, gm_ln_g, gm_ln_b, gm_ws, gm_bs, conv_w, conv_b, conv_ln_g,
              conv_ln_b, w_out, norm2_g, w_ffn_in, w_ffn_out, final_g):
    seq = x.shape[1]
    pos = jnp.arange(seq, dtype=jnp.float32)
    half = RET_HEAD_DIM // 2
    inv_freq = ROPE_BASE ** (-jnp.arange(half, dtype=jnp.float32) / half)
    ang = pos[:, None] * inv_freq[None, :]
    cos = jnp.cos(ang)[:, None, :].astype(x.dtype)
    sin = jnp.sin(ang)[:, None, :].astype(x.dtype)
    splits = np.cumsum([GM_WIDTH, GM_WIDTH, RET_WIDTH, RET_WIDTH, RET_WIDTH, RET_WIDTH, CONV_WIDTH])
    for l in range(DEPTH):
        h = _rmsnorm(x, norm1_g[l])
        proj = h @ w_in[l]
        gm_u, gm_v, q, k, v, g, cv_a, cv_gate = jnp.split(proj, splits, axis=-1)
        y_gm = _spatial_gating(jax.nn.gelu(gm_u, approximate=False), jax.nn.gelu(gm_v, approximate=False),
                               gm_ln_g[l], gm_ln_b[l], gm_ws[l], gm_bs[l])
        y_ret = _retention(q, k, v, g, cos, sin)
        y_cv = _conformer_conv(cv_a, cv_gate, conv_w[l], conv_b[l], conv_ln_g[l], conv_ln_b[l])
        x = x + jnp.concatenate([y_gm, y_ret, y_cv], axis=-1) @ w_out[l]
        h = _rmsnorm(x, norm2_g[l])
        gate, up = jnp.split(h @ w_ffn_in[l], 2, axis=-1)
        x = x + (jax.nn.silu(gate) * up) @ w_ffn_out[l]
    return _rmsnorm(x, final_g)


import jax as _jax
import jax.numpy as _jnp

TWIN_FORMAT = 'train_step'
FWD_PARAMS = ['x', 'norm1_g', 'w_in', 'gm_ln_g', 'gm_ln_b', 'gm_ws', 'gm_bs', 'conv_w', 'conv_b', 'conv_ln_g', 'conv_ln_b', 'w_out', 'norm2_g', 'w_ffn_in', 'w_ffn_out', 'final_g']
TWIN_WEIGHTS = ['norm1_g', 'w_in', 'gm_ln_g', 'gm_ln_b', 'gm_ws', 'gm_bs', 'conv_w', 'conv_b', 'conv_ln_g', 'conv_ln_b', 'w_out', 'norm2_g', 'w_ffn_in', 'w_ffn_out', 'final_g']
TWIN_DIFF_INPUT = 'x'
TWIN_INPUTS = ['x', 'norm1_g', 'w_in', 'gm_ln_g', 'gm_ln_b', 'gm_ws', 'gm_bs', 'conv_w', 'conv_b', 'conv_ln_g', 'conv_ln_b', 'w_out', 'norm2_g', 'w_ffn_in', 'w_ffn_out', 'final_g', 'loss_target', 'm_norm1_g', 'm_w_in', 'm_gm_ln_g', 'm_gm_ln_b', 'm_gm_ws', 'm_gm_bs', 'm_conv_w', 'm_conv_b', 'm_conv_ln_g', 'm_conv_ln_b', 'm_w_out', 'm_norm2_g', 'm_w_ffn_in', 'm_w_ffn_out', 'm_final_g', 'v_norm1_g', 'v_w_in', 'v_gm_ln_g', 'v_gm_ln_b', 'v_gm_ws', 'v_gm_bs', 'v_conv_w', 'v_conv_b', 'v_conv_ln_g', 'v_conv_ln_b', 'v_w_out', 'v_norm2_g', 'v_w_ffn_in', 'v_w_ffn_out', 'v_final_g']
TWIN_OUTPUTS = ['loss', 'grad_x', 'grad_norm1_g', 'grad_w_in', 'grad_gm_ln_g', 'grad_gm_ln_b', 'grad_gm_ws', 'grad_gm_bs', 'grad_conv_w', 'grad_conv_b', 'grad_conv_ln_g', 'grad_conv_ln_b', 'grad_w_out', 'grad_norm2_g', 'grad_w_ffn_in', 'grad_w_ffn_out', 'grad_final_g', 'delta_norm1_g', 'delta_w_in', 'delta_gm_ln_g', 'delta_gm_ln_b', 'delta_gm_ws', 'delta_gm_bs', 'delta_conv_w', 'delta_conv_b', 'delta_conv_ln_g', 'delta_conv_ln_b', 'delta_w_out', 'delta_norm2_g', 'delta_w_ffn_in', 'delta_w_ffn_out', 'delta_final_g', 'new_m_norm1_g', 'new_m_w_in', 'new_m_gm_ln_g', 'new_m_gm_ln_b', 'new_m_gm_ws', 'new_m_gm_bs', 'new_m_conv_w', 'new_m_conv_b', 'new_m_conv_ln_g', 'new_m_conv_ln_b', 'new_m_w_out', 'new_m_norm2_g', 'new_m_w_ffn_in', 'new_m_w_ffn_out', 'new_m_final_g', 'new_v_norm1_g', 'new_v_w_in', 'new_v_gm_ln_g', 'new_v_gm_ln_b', 'new_v_gm_ws', 'new_v_gm_bs', 'new_v_conv_w', 'new_v_conv_b', 'new_v_conv_ln_g', 'new_v_conv_ln_b', 'new_v_w_out', 'new_v_norm2_g', 'new_v_w_ffn_in', 'new_v_w_ffn_out', 'new_v_final_g']
TWIN_LEAF_KINDS = {'loss': 'loss', 'grad_x': 'grad_x', 'grad_norm1_g': 'grad_w', 'grad_w_in': 'grad_w', 'grad_gm_ln_g': 'grad_w', 'grad_gm_ln_b': 'grad_w', 'grad_gm_ws': 'grad_w', 'grad_gm_bs': 'grad_w', 'grad_conv_w': 'grad_w', 'grad_conv_b': 'grad_w', 'grad_conv_ln_g': 'grad_w', 'grad_conv_ln_b': 'grad_w', 'grad_w_out': 'grad_w', 'grad_norm2_g': 'grad_w', 'grad_w_ffn_in': 'grad_w', 'grad_w_ffn_out': 'grad_w', 'grad_final_g': 'grad_w', 'delta_norm1_g': 'delta_w', 'delta_w_in': 'delta_w', 'delta_gm_ln_g': 'delta_w', 'delta_gm_ln_b': 'delta_w', 'delta_gm_ws': 'delta_w', 'delta_gm_bs': 'delta_w', 'delta_conv_w': 'delta_w', 'delta_conv_b': 'delta_w', 'delta_conv_ln_g': 'delta_w', 'delta_conv_ln_b': 'delta_w', 'delta_w_out': 'delta_w', 'delta_norm2_g': 'delta_w', 'delta_w_ffn_in': 'delta_w', 'delta_w_ffn_out': 'delta_w', 'delta_final_g': 'delta_w', 'new_m_norm1_g': 'new_m', 'new_m_w_in': 'new_m', 'new_m_gm_ln_g': 'new_m', 'new_m_gm_ln_b': 'new_m', 'new_m_gm_ws': 'new_m', 'new_m_gm_bs': 'new_m', 'new_m_conv_w': 'new_m', 'new_m_conv_b': 'new_m', 'new_m_conv_ln_g': 'new_m', 'new_m_conv_ln_b': 'new_m', 'new_m_w_out': 'new_m', 'new_m_norm2_g': 'new_m', 'new_m_w_ffn_in': 'new_m', 'new_m_w_ffn_out': 'new_m', 'new_m_final_g': 'new_m', 'new_v_norm1_g': 'new_v', 'new_v_w_in': 'new_v', 'new_v_gm_ln_g': 'new_v', 'new_v_gm_ln_b': 'new_v', 'new_v_gm_ws': 'new_v', 'new_v_gm_bs': 'new_v', 'new_v_conv_w': 'new_v', 'new_v_conv_b': 'new_v', 'new_v_conv_ln_g': 'new_v', 'new_v_conv_ln_b': 'new_v', 'new_v_w_out': 'new_v', 'new_v_norm2_g': 'new_v', 'new_v_w_ffn_in': 'new_v', 'new_v_w_ffn_out': 'new_v', 'new_v_final_g': 'new_v'}


def _forward(args):
    return _fwd_reference(*[args[k] for k in FWD_PARAMS])


def _output_shape():
    def fwd():
        inp = _fwd_setup_inputs(0)
        return _fwd_reference(*[inp[k] for k in FWD_PARAMS])
    out = _jax.eval_shape(fwd)
    return out.shape, out.dtype

N_MICROBATCH = 1
ADAM_LR = 0.001
ADAM_B1 = 0.9
ADAM_B2 = 0.999
ADAM_EPS = 1e-08
ADAM_WD = 0.01
ADAM_STEP = 10
PER_EXAMPLE_BATCH_AXIS = {'x': 0, 'loss_target': 0}
SHARED_INPUTS = []
_WEIGHT_DTYPES = {'norm1_g': _jnp.float32, 'w_in': _jnp.float32, 'gm_ln_g': _jnp.float32, 'gm_ln_b': _jnp.float32, 'gm_ws': _jnp.float32, 'gm_bs': _jnp.float32, 'conv_w': _jnp.float32, 'conv_b': _jnp.float32, 'conv_ln_g': _jnp.float32, 'conv_ln_b': _jnp.float32, 'w_out': _jnp.float32, 'norm2_g': _jnp.float32, 'w_ffn_in': _jnp.float32, 'w_ffn_out': _jnp.float32, 'final_g': _jnp.float32}
MOMENT_SCALE = {'norm1_g': 2.572148e-01, 'w_in': 1.465426e-01, 'gm_ln_g': 1.459025e-01, 'gm_ln_b': 1.554575e-01, 'gm_ws': 1.057328e-01, 'gm_bs': 1.112807e-01, 'conv_w': 1.453541e-01, 'conv_b': 4.002708e-01, 'conv_ln_g': 2.225617e-01, 'conv_ln_b': 2.186196e-01, 'w_out': 1.636347e-01, 'norm2_g': 1.788234e-01, 'w_ffn_in': 7.040322e-02, 'w_ffn_out': 1.151697e-01, 'final_g': 6.400505e+01}


def _to_microbatches(a, axis):
    t = _jnp.moveaxis(a, axis, 0)
    t = t.reshape((N_MICROBATCH, t.shape[0] // N_MICROBATCH) + t.shape[1:])
    return _jnp.moveaxis(t, 1, axis + 1)


def setup_inputs(seed: int = 0) -> dict:
    inp = _fwd_setup_inputs(seed)
    key = _jax.random.fold_in(_jax.random.key(seed), 7919)
    shape, _ = _output_shape()
    out = dict(inp)
    out["loss_target"] = _jax.random.normal(_jax.random.fold_in(key, 0), shape, _jnp.float32)
    for i, name in enumerate(TWIN_WEIGHTS):
        w = inp[name].astype(_jnp.float32)
        if MOMENT_SCALE is None:
            s = _jnp.sqrt(_jnp.mean(_jnp.square(w)) + 1e-30)
        else:
            s = MOMENT_SCALE[name]
        km, kv = _jax.random.split(_jax.random.fold_in(key, i + 1))
        out[name] = w
        out["m_" + name] = s * _jax.random.normal(km, w.shape, _jnp.float32)
        out["v_" + name] = (s * s) * _jax.random.uniform(kv, w.shape, _jnp.float32, 0.5, 1.5)
    if N_MICROBATCH > 1:
        for name, axis in PER_EXAMPLE_BATCH_AXIS.items():
            out[name] = _to_microbatches(out[name], axis)
    return {'x': out['x'], 'norm1_g': out['norm1_g'], 'w_in': out['w_in'], 'gm_ln_g': out['gm_ln_g'], 'gm_ln_b': out['gm_ln_b'], 'gm_ws': out['gm_ws'], 'gm_bs': out['gm_bs'], 'conv_w': out['conv_w'], 'conv_b': out['conv_b'], 'conv_ln_g': out['conv_ln_g'], 'conv_ln_b': out['conv_ln_b'], 'w_out': out['w_out'], 'norm2_g': out['norm2_g'], 'w_ffn_in': out['w_ffn_in'], 'w_ffn_out': out['w_ffn_out'], 'final_g': out['final_g'], 'loss_target': out['loss_target'], 'm_norm1_g': out['m_norm1_g'], 'm_w_in': out['m_w_in'], 'm_gm_ln_g': out['m_gm_ln_g'], 'm_gm_ln_b': out['m_gm_ln_b'], 'm_gm_ws': out['m_gm_ws'], 'm_gm_bs': out['m_gm_bs'], 'm_conv_w': out['m_conv_w'], 'm_conv_b': out['m_conv_b'], 'm_conv_ln_g': out['m_conv_ln_g'], 'm_conv_ln_b': out['m_conv_ln_b'], 'm_w_out': out['m_w_out'], 'm_norm2_g': out['m_norm2_g'], 'm_w_ffn_in': out['m_w_ffn_in'], 'm_w_ffn_out': out['m_w_ffn_out'], 'm_final_g': out['m_final_g'], 'v_norm1_g': out['v_norm1_g'], 'v_w_in': out['v_w_in'], 'v_gm_ln_g': out['v_gm_ln_g'], 'v_gm_ln_b': out['v_gm_ln_b'], 'v_gm_ws': out['v_gm_ws'], 'v_gm_bs': out['v_gm_bs'], 'v_conv_w': out['v_conv_w'], 'v_conv_b': out['v_conv_b'], 'v_conv_ln_g': out['v_conv_ln_g'], 'v_conv_ln_b': out['v_conv_ln_b'], 'v_w_out': out['v_w_out'], 'v_norm2_g': out['v_norm2_g'], 'v_w_ffn_in': out['v_w_ffn_in'], 'v_w_ffn_out': out['v_w_ffn_out'], 'v_final_g': out['v_final_g']}


def _loss(weights, diff, rest, loss_target):
    with _jax.named_scope("forward"):
        args = {**rest, TWIN_DIFF_INPUT: diff, **{k: w.astype(_WEIGHT_DTYPES[k]) for k, w in weights.items()}}
        y = _forward(args)
    with _jax.named_scope("loss_head"):
        err = _jnp.square(y.astype(_jnp.float32) - loss_target)
        return 0.5 * _jnp.sum(_jnp.mean(err, axis=-1)) if err.ndim else 0.5 * err


def _adamw(w, g, m, v):
    m = ADAM_B1 * m + (1.0 - ADAM_B1) * g
    v = ADAM_B2 * v + (1.0 - ADAM_B2) * _jnp.square(g)
    m_hat = m / (1.0 - ADAM_B1 ** ADAM_STEP)
    v_hat = v / (1.0 - ADAM_B2 ** ADAM_STEP)
    delta = -ADAM_LR * (m_hat / (_jnp.sqrt(v_hat) + ADAM_EPS) + ADAM_WD * w)
    return delta, m, v


def reference(x, norm1_g, w_in, gm_ln_g, gm_ln_b, gm_ws, gm_bs, conv_w, conv_b, conv_ln_g, conv_ln_b, w_out, norm2_g, w_ffn_in, w_ffn_out, final_g, loss_target, m_norm1_g, m_w_in, m_gm_ln_g, m_gm_ln_b, m_gm_ws, m_gm_bs, m_conv_w, m_conv_b, m_conv_ln_g, m_conv_ln_b, m_w_out, m_norm2_g, m_w_ffn_in, m_w_ffn_out, m_final_g, v_norm1_g, v_w_in, v_gm_ln_g, v_gm_ln_b, v_gm_ws, v_gm_bs, v_conv_w, v_conv_b, v_conv_ln_g, v_conv_ln_b, v_w_out, v_norm2_g, v_w_ffn_in, v_w_ffn_out, v_final_g):
    given = dict(x=x, norm1_g=norm1_g, w_in=w_in, gm_ln_g=gm_ln_g, gm_ln_b=gm_ln_b, gm_ws=gm_ws, gm_bs=gm_bs, conv_w=conv_w, conv_b=conv_b, conv_ln_g=conv_ln_g, conv_ln_b=conv_ln_b, w_out=w_out, norm2_g=norm2_g, w_ffn_in=w_ffn_in, w_ffn_out=w_ffn_out, final_g=final_g, loss_target=loss_target, m_norm1_g=m_norm1_g, m_w_in=m_w_in, m_gm_ln_g=m_gm_ln_g, m_gm_ln_b=m_gm_ln_b, m_gm_ws=m_gm_ws, m_gm_bs=m_gm_bs, m_conv_w=m_conv_w, m_conv_b=m_conv_b, m_conv_ln_g=m_conv_ln_g, m_conv_ln_b=m_conv_ln_b, m_w_out=m_w_out, m_norm2_g=m_norm2_g, m_w_ffn_in=m_w_ffn_in, m_w_ffn_out=m_w_ffn_out, m_final_g=m_final_g, v_norm1_g=v_norm1_g, v_w_in=v_w_in, v_gm_ln_g=v_gm_ln_g, v_gm_ln_b=v_gm_ln_b, v_gm_ws=v_gm_ws, v_gm_bs=v_gm_bs, v_conv_w=v_conv_w, v_conv_b=v_conv_b, v_conv_ln_g=v_conv_ln_g, v_conv_ln_b=v_conv_ln_b, v_w_out=v_w_out, v_norm2_g=v_norm2_g, v_w_ffn_in=v_w_ffn_in, v_w_ffn_out=v_w_ffn_out, v_final_g=v_final_g)
    weights = {n: given[n] for n in TWIN_WEIGHTS}
    shared = {n: given[n] for n in SHARED_INPUTS}
    per_example = {n: given[n] for n in ['x']}
    grad_fn = _jax.value_and_grad(_loss, argnums=(0, 1))

    def one_microbatch(ex, loss_target):
        ex = dict(ex)
        diff = ex.pop(TWIN_DIFF_INPUT)
        return grad_fn(weights, diff, {**shared, **ex}, loss_target)

    if N_MICROBATCH == 1:
        loss, (grad_w, grad_x) = one_microbatch(per_example, given["loss_target"])
    else:
        def body(carry, xs):
            loss_sum, grad_sum = carry
            l_k, (gw_k, gx_k) = one_microbatch(xs[0], xs[1])
            with _jax.named_scope("update"):
                return (loss_sum + l_k, _jax.tree.map(_jnp.add, grad_sum, gw_k)), gx_k

        init = (_jnp.zeros((), _jnp.float32), _jax.tree.map(_jnp.zeros_like, weights))
        (loss, grad_w), grad_x = _jax.lax.scan(body, init, (per_example, given["loss_target"]))
    with _jax.named_scope("update"):
        delta_w, new_m, new_v = {}, {}, {}
        for n in TWIN_WEIGHTS:
            delta_w[n], new_m[n], new_v[n] = _adamw(weights[n], grad_w[n], given["m_" + n], given["v_" + n])
    return (loss, grad_x, *[grad_w[n] for n in TWIN_WEIGHTS], *[delta_w[n] for n in TWIN_WEIGHTS],
            *[new_m[n] for n in TWIN_WEIGHTS], *[new_v[n] for n in TWIN_WEIGHTS])
```

```python
import functools
import math

import numpy as np
import jax
import jax.numpy as jnp
from jax import lax
from jax.experimental import pallas as pl
from jax.experimental.pallas import tpu as pltpu

F32 = jnp.float32
BF16 = jnp.bfloat16
MXU_DTYPE = BF16
S = jax.ShapeDtypeStruct

D = 1024
DEPTH = 2
GM_W = 256
GM_HEADS = 4
RET_W = 512
HEADS = 4
HD = 128
CV_W = 256
KCONV = 31
IN_W = 2 * GM_W + 4 * RET_W + 2 * CV_W
FFN_H = 2816
CH = 128
ROPE_BASE = 10000.0
EPS = 1e-6
N_CHIPS = 4
N_DEV = 8
HALO = 16

ADAM_LR = 0.001
ADAM_B1 = 0.9
ADAM_B2 = 0.999
ADAM_EPS = 1e-08
ADAM_WD = 0.01
ADAM_STEP = 10

VMEM_LIMIT = 52 * 1024 * 1024
MESH = pl.DeviceIdType.MESH


def _cp(*sem, vmem=VMEM_LIMIT):
    return pltpu.CompilerParams(dimension_semantics=tuple(sem), vmem_limit_bytes=vmem)


def _mx(a):
    return a.astype(MXU_DTYPE)


def _dot(a, b):
    return jnp.dot(_mx(a), _mx(b), preferred_element_type=F32)


def _dot_nt(a, b):
    return lax.dot_general(_mx(a), _mx(b), (((1,), (1,)), ((), ())), preferred_element_type=F32)


def _dot_tn(a, b):
    return lax.dot_general(_mx(a), _mx(b), (((0,), (0,)), ((), ())), preferred_element_type=F32)


def _sigmoid(x):
    return 1.0 / (1.0 + jnp.exp(-x))


def _gelu(x):
    return 0.5 * x * (1.0 + lax.erf(x * (1.0 / math.sqrt(2.0))))


def _gelu_grad(x):
    return 0.5 * (1.0 + lax.erf(x * (1.0 / math.sqrt(2.0)))) + x * jnp.exp(-0.5 * x * x) * (1.0 / math.sqrt(2.0 * math.pi))


def _rms_r(x):
    return lax.rsqrt(jnp.mean(x * x, axis=-1, keepdims=True) + EPS)


def _rms_bwd(dh, x, r, g):
    u = dh * g
    dx = r * u - x * (r * r * r) * jnp.mean(u * x, axis=-1, keepdims=True)
    return dx, dh * x * r


def _standardize(a):
    mu = jnp.mean(a, axis=-1, keepdims=True)
    d = a - mu
    r = lax.rsqrt(jnp.mean(d * d, axis=-1, keepdims=True) + EPS)
    return d * r, r


def _standardize_bwd(do, o, r):
    return r * (do - jnp.mean(do, axis=-1, keepdims=True) - o * jnp.mean(do * o, axis=-1, keepdims=True))


def _acc_out(ref, val, first):
    @pl.when(first)
    def _():
        ref[...] = val

    @pl.when(jnp.logical_not(first))
    def _():
        ref[...] += val


def _row_tile(t, pref):
    tm = min(t, pref)
    assert t % tm == 0, (t, tm)
    return tm


def _segments(part_widths, shard_w):
    bounds = {0}
    off = 0
    for w in part_widths:
        off += w
        bounds.add(off)
    total = off
    for j in range(1, total // shard_w + 1):
        bounds.add(j * shard_w)
    bounds = sorted(bounds)
    starts = np.cumsum([0] + list(part_widths))
    segs = []
    for a, b in zip(bounds[:-1], bounds[1:]):
        p = int(np.searchsorted(starts, a, side="right") - 1)
        segs.append((p, a - int(starts[p]), a // shard_w, a % shard_w, b - a))
    return segs


def _wcol_spec(w, l):
    return pl.BlockSpec((N_CHIPS, None) + w.shape[2:], lambda *_: (0, l, 0, 0))


def _wrow_spec(w, l):
    return pl.BlockSpec((None,) + w.shape[1:], lambda *_: (l, 0, 0))


def _norm_mm(x, g, w, l, name, tm_pref):
    t = x.shape[0]
    nc = w.shape[3]
    tm = _row_tile(t, tm_pref)

    def body(x_ref, g_ref, w_ref, o_ref):
        xv = x_ref[...]
        h = _mx(xv * _rms_r(xv) * g_ref[...])
        for j in range(N_CHIPS):
            o_ref[:, j * nc:(j + 1) * nc] = jnp.dot(h, w_ref[j], preferred_element_type=F32)

    return pl.pallas_call(
        body, grid=(t // tm,),
        in_specs=[pl.BlockSpec((tm, D), lambda i: (i, 0)), pl.BlockSpec((1, D), lambda i: (0, 0)),
                  _wcol_spec(w, l)],
        out_specs=pl.BlockSpec((tm, N_CHIPS * nc), lambda i: (i, 0)),
        out_shape=S((t, N_CHIPS * nc), F32), name=name, compiler_params=_cp("parallel"))(x, g, w)


def _parts_mm_res(parts, w, l, res, name):
    t = res.shape[0]
    tm = _row_tile(t, 512)
    widths = [p.shape[1] for p in parts]
    offs = np.cumsum([0] + widths)
    n = len(parts)

    def body(*refs):
        p_refs, w_ref, r_ref, o_ref = refs[:n], refs[n], refs[n + 1], refs[n + 2]
        acc = r_ref[...]
        for p in range(n):
            acc = acc + _dot(p_refs[p][...], w_ref[int(offs[p]):int(offs[p + 1]), :])
        o_ref[...] = acc

    return pl.pallas_call(
        body, grid=(t // tm,),
        in_specs=[pl.BlockSpec((tm, wd), lambda i: (i, 0)) for wd in widths]
        + [_wrow_spec(w, l), pl.BlockSpec((tm, D), lambda i: (i, 0))],
        out_specs=pl.BlockSpec((tm, D), lambda i: (i, 0)),
        out_shape=S((t, D), F32), name=name, compiler_params=_cp("parallel"))(*parts, w, res)


def _swiglu(ff):
    gate = ff[:, :FFN_H]
    up = ff[:, FFN_H:]
    return gate * _sigmoid(gate) * up


def _swiglu_mm_res(ff, w, l, res, name):
    t = res.shape[0]
    tm = _row_tile(t, 256)

    def body(f_ref, w_ref, r_ref, o_ref):
        o_ref[...] = r_ref[...] + _dot(_swiglu(f_ref[...]), w_ref[...])

    return pl.pallas_call(
        body, grid=(t // tm,),
        in_specs=[pl.BlockSpec((tm, 2 * FFN_H), lambda i: (i, 0)), _wrow_spec(w, l),
                  pl.BlockSpec((tm, D), lambda i: (i, 0))],
        out_specs=pl.BlockSpec((tm, D), lambda i: (i, 0)),
        out_shape=S((t, D), F32), name=name, compiler_params=_cp("parallel"))(ff, w, res)


def _dx_norm(dparts, w, l, x, g, dres, name, tm_pref):
    t = x.shape[0]
    nc = w.shape[3]
    tm = _row_tile(t, tm_pref)
    widths = [p.shape[1] for p in dparts]
    segs = _segments(widths, nc)
    n = len(dparts)

    def body(*refs):
        d_refs = refs[:n]
        w_ref, x_ref, g_ref, r_ref, dx_ref, dg_ref = refs[n:]
        dh = jnp.zeros((tm, D), F32)
        for (p, po, j, jo, wd) in segs:
            dh = dh + _dot_nt(d_refs[p][:, po:po + wd], w_ref[j, :, jo:jo + wd])
        xv = x_ref[...]
        dx, dgrow = _rms_bwd(dh, xv, _rms_r(xv), g_ref[...])
        dx_ref[...] = r_ref[...] + dx
        _acc_out(dg_ref, jnp.sum(dgrow, axis=0, keepdims=True), pl.program_id(0) == 0)

    return pl.pallas_call(
        body, grid=(t // tm,),
        in_specs=[pl.BlockSpec((tm, wd), lambda i: (i, 0)) for wd in widths]
        + [_wcol_spec(w, l), pl.BlockSpec((tm, D), lambda i: (i, 0)),
           pl.BlockSpec((1, D), lambda i: (0, 0)), pl.BlockSpec((tm, D), lambda i: (i, 0))],
        out_specs=[pl.BlockSpec((tm, D), lambda i: (i, 0)), pl.BlockSpec((1, D), lambda i: (0, 0))],
        out_shape=[S((t, D), F32), S((1, D), F32)], name=name,
        compiler_params=_cp("arbitrary"))(*dparts, w, x, g, dres)


def _dx_parts(dy, w, l, widths, name):
    t = dy.shape[0]
    tm = _row_tile(t, 512)
    offs = np.cumsum([0] + list(widths))
    n = len(widths)

    def body(dy_ref, w_ref, *o_refs):
        dyv = _mx(dy_ref[...])
        for p in range(n):
            o_refs[p][...] = _dot_nt(dyv, w_ref[int(offs[p]):int(offs[p + 1]), :])

    return pl.pallas_call(
        body, grid=(t // tm,),
        in_specs=[pl.BlockSpec((tm, D), lambda i: (i, 0)), _wrow_spec(w, l)],
        out_specs=[pl.BlockSpec((tm, wd), lambda i: (i, 0)) for wd in widths],
        out_shape=[S((t, wd), F32) for wd in widths], name=name, compiler_params=_cp("parallel"))(dy, w)


def _dx_swiglu(dy, w, l, ff, name):
    t = dy.shape[0]
    tm = _row_tile(t, 256)

    def body(dy_ref, w_ref, f_ref, o_ref):
        dact = _dot_nt(dy_ref[...], w_ref[...])
        gate = f_ref[:, :FFN_H]
        up = f_ref[:, FFN_H:]
        s = _sigmoid(gate)
        o_ref[:, :FFN_H] = dact * up * (s * (1.0 + gate * (1.0 - s)))
        o_ref[:, FFN_H:] = dact * (gate * s)

    return pl.pallas_call(
        body, grid=(t // tm,),
        in_specs=[pl.BlockSpec((tm, D), lambda i: (i, 0)), _wrow_spec(w, l),
                  pl.BlockSpec((tm, 2 * FFN_H), lambda i: (i, 0))],
        out_specs=pl.BlockSpec((tm, 2 * FFN_H), lambda i: (i, 0)),
        out_shape=S((t, 2 * FFN_H), F32), name=name, compiler_params=_cp("parallel"))(dy, w, ff)


def _dw_norm_parts(x, g, dparts, nc, name):
    t = x.shape[0]
    tk = _row_tile(t, 512)
    widths = [p.shape[1] for p in dparts]
    segs = _segments(widths, nc)
    n = len(dparts)
    nk = t // tk

    def body(*refs):
        x_ref, g_ref = refs[0], refs[1]
        d_refs = refs[2:2 + n]
        o_ref, acc_ref = refs[2 + n], refs[3 + n]
        k = pl.program_id(0)
        xv = x_ref[...]
        h = _mx(xv * _rms_r(xv) * g_ref[...])

        @pl.when(k == 0)
        def _():
            acc_ref[...] = jnp.zeros_like(acc_ref)

        for (p, po, j, jo, wd) in segs:
            acc_ref[j, :, jo:jo + wd] += _dot_tn(h, d_refs[p][:, po:po + wd])

        @pl.when(k == nk - 1)
        def _():
            o_ref[...] = acc_ref[...].astype(o_ref.dtype)

    return pl.pallas_call(
        body, grid=(nk,),
        in_specs=[pl.BlockSpec((tk, D), lambda k: (k, 0)), pl.BlockSpec((1, D), lambda k: (0, 0))]
        + [pl.BlockSpec((tk, wd), lambda k: (k, 0)) for wd in widths],
        out_specs=pl.BlockSpec((N_CHIPS, D, nc), lambda k: (0, 0, 0)),
        out_shape=S((N_CHIPS, D, nc), MXU_DTYPE), name=name,
        scratch_shapes=[pltpu.VMEM((N_CHIPS, D, nc), F32)],
        compiler_params=_cp("arbitrary"))(x, g, *dparts)


def _dw_norm_cols(x, g, dy, nc, name):
    t = x.shape[0]
    tk = _row_tile(t, 512)
    nk = t // tk

    def body(x_ref, g_ref, dy_ref, o_ref, acc_ref):
        k = pl.program_id(1)
        xv = x_ref[...]
        h = _mx(xv * _rms_r(xv) * g_ref[...])

        @pl.when(k == 0)
        def _():
            acc_ref[...] = jnp.zeros_like(acc_ref)

        acc_ref[...] += _dot_tn(h, dy_ref[...])

        @pl.when(k == nk - 1)
        def _():
            o_ref[...] = acc_ref[...].astype(o_ref.dtype)

    return pl.pallas_call(
        body, grid=(N_CHIPS, nk),
        in_specs=[pl.BlockSpec((tk, D), lambda j, k: (k, 0)), pl.BlockSpec((1, D), lambda j, k: (0, 0)),
                  pl.BlockSpec((tk, nc), lambda j, k: (k, j))],
        out_specs=pl.BlockSpec((None, D, nc), lambda j, k: (j, 0, 0)),
        out_shape=S((N_CHIPS, D, nc), MXU_DTYPE), name=name,
        scratch_shapes=[pltpu.VMEM((D, nc), F32)],
        compiler_params=_cp("parallel", "arbitrary"))(x, g, dy)


def _dw_parts(parts, dy, name):
    t = dy.shape[0]
    tk = _row_tile(t, 512)
    widths = [p.shape[1] for p in parts]
    offs = np.cumsum([0] + widths)
    ktot = int(offs[-1])
    n = len(parts)
    nk = t // tk

    def body(*refs):
        p_refs, dy_ref, o_ref, acc_ref = refs[:n], refs[n], refs[n + 1], refs[n + 2]
        k = pl.program_id(0)

        @pl.when(k == 0)
        def _():
            acc_ref[...] = jnp.zeros_like(acc_ref)

        dyv = _mx(dy_ref[...])
        for p in range(n):
            acc_ref[int(offs[p]):int(offs[p + 1]), :] += _dot_tn(p_refs[p][...], dyv)

        @pl.when(k == nk - 1)
        def _():
            o_ref[...] = acc_ref[...].astype(o_ref.dtype)

    return pl.pallas_call(
        body, grid=(nk,),
        in_specs=[pl.BlockSpec((tk, wd), lambda k: (k, 0)) for wd in widths]
        + [pl.BlockSpec((tk, D), lambda k: (k, 0))],
        out_specs=pl.BlockSpec((ktot, D), lambda k: (0, 0)),
        out_shape=S((ktot, D), MXU_DTYPE), name=name,
        scratch_shapes=[pltpu.VMEM((ktot, D), F32)],
        compiler_params=_cp("arbitrary"))(*parts, dy)


def _dw_swiglu(ff, dy, name):
    t = dy.shape[0]
    tk = _row_tile(t, 256)
    nk = t // tk

    def body(f_ref, dy_ref, o_ref, acc_ref):
        k = pl.program_id(0)

        @pl.when(k == 0)
        def _():
            acc_ref[...] = jnp.zeros_like(acc_ref)

        acc_ref[...] += _dot_tn(_swiglu(f_ref[...]), dy_ref[...])

        @pl.when(k == nk - 1)
        def _():
            o_ref[...] = acc_ref[...].astype(o_ref.dtype)

    return pl.pallas_call(
        body, grid=(nk,),
        in_specs=[pl.BlockSpec((tk, 2 * FFN_H), lambda k: (k, 0)), pl.BlockSpec((tk, D), lambda k: (k, 0))],
        out_specs=pl.BlockSpec((FFN_H, D), lambda k: (0, 0)),
        out_shape=S((FFN_H, D), MXU_DTYPE), name=name,
        scratch_shapes=[pltpu.VMEM((FFN_H, D), F32)],
        compiler_params=_cp("arbitrary"))(ff, dy)


def _tables(t):
    pos = jnp.arange(t, dtype=F32)
    half = HD // 2
    inv_freq = ROPE_BASE ** (-jnp.arange(half, dtype=F32) / half)
    ang = pos[:, None] * inv_freq[None, :]
    cos, sin = jnp.cos(ang), jnp.sin(ang)
    tb = {"cos2": jnp.concatenate([cos, cos], axis=1), "sin2": jnp.concatenate([-sin, sin], axis=1)}
    gf = 1.0 - jnp.exp2(-5.0 - jnp.arange(HEADS, dtype=F32))
    lgf = jnp.log(gf)[:, None]
    lgb = jnp.log(gf[::-1])[:, None]
    idx = jnp.arange(CH, dtype=F32)
    diff = idx[:, None] - idx[None, :]
    dfwd = jnp.where(diff >= 0, jnp.exp(lgf[:, :, None] * jnp.where(diff >= 0, diff, 0.0)), 0.0)
    dbwd = jnp.where(diff < 0, jnp.exp(lgb[:, :, None] * jnp.where(diff < 0, -diff, 0.0)), 0.0)
    tb["dm"] = dfwd + dbwd
    tb["dmt"] = jnp.swapaxes(tb["dm"], 1, 2)

    def lanes(a):
        return jnp.repeat(a.T, HD, axis=1)

    tb["xif"] = lanes(jnp.exp(lgf * (idx + 1)))
    tb["zf"] = lanes(jnp.exp(lgf * (CH - 1 - idx)))
    tb["xib"] = lanes(jnp.exp(lgb * (CH - idx)))
    tb["zb"] = lanes(jnp.exp(lgb * idx))
    tb["gcf"] = jnp.repeat(jnp.exp(lgf * CH), HD, axis=0).reshape(1, HEADS * HD)
    tb["gcb"] = jnp.repeat(jnp.exp(lgb * CH), HD, axis=0).reshape(1, HEADS * HD)
    return tb


def _full(shape):
    nd = len(shape)
    return pl.BlockSpec(shape, lambda *_: (0,) * nd)


def _gm_mixed(vn, ws_ref, bias):
    lane = lax.broadcasted_iota(jnp.int32, (CH, 128), 1)
    halves = []
    for hf in range(2):
        vh = _mx(vn[:, hf * 128:(hf + 1) * 128])
        r0 = jnp.dot(_mx(ws_ref[2 * hf]), vh, preferred_element_type=F32)
        r1 = jnp.dot(_mx(ws_ref[2 * hf + 1]), vh, preferred_element_type=F32)
        halves.append(jnp.where(lane < 64, r0, r1))
    return jnp.concatenate(halves, axis=1) + bias


def _gm_fwd(proj, ln_g, ln_b, ws, bias, name):
    t = proj.shape[0]
    tm = _row_tile(t, 512)

    def body(pu_ref, pv_ref, g_ref, b_ref, ws_ref, bias_ref, o_ref):
        for c in range(tm // CH):
            rows = slice(c * CH, (c + 1) * CH)
            u = _gelu(pu_ref[rows, :])
            o, _ = _standardize(_gelu(pv_ref[rows, :]))
            vn = o * g_ref[...] + b_ref[...]
            o_ref[rows, :] = u * _gm_mixed(vn, ws_ref, bias_ref[...])

    return pl.pallas_call(
        body, grid=(t // tm,),
        in_specs=[pl.BlockSpec((tm, GM_W), lambda i: (i, 0)), pl.BlockSpec((tm, GM_W), lambda i: (i, 1)),
                  _full((1, GM_W)), _full((1, GM_W)), _full((GM_HEADS, CH, CH)), _full((CH, GM_W))],
        out_specs=pl.BlockSpec((tm, GM_W), lambda i: (i, 0)),
        out_shape=S((t, GM_W), F32), name=name, compiler_params=_cp("parallel"))(proj, proj, ln_g, ln_b, ws, bias)


def _gm_bwd(proj, dy, ln_g, ln_b, ws, wst, bias, name):
    t = proj.shape[0]
    tm = _row_tile(t, 512)
    nb = t // tm

    def body(pu_ref, pv_ref, dy_ref, g_ref, b_ref, ws_ref, wst_ref, bias_ref,
             d_ref, dws_ref, dbs_ref, dg_ref, db_ref, dbias_ref):
        first = pl.program_id(0) == 0
        lane = lax.broadcasted_iota(jnp.int32, (CH, 128), 1)
        dws = [jnp.zeros((CH, CH), F32) for _ in range(GM_HEADS)]
        dbias = jnp.zeros((CH, GM_W), F32)
        dg = jnp.zeros((1, GM_W), F32)
        db = jnp.zeros((1, GM_W), F32)
        for c in range(tm // CH):
            rows = slice(c * CH, (c + 1) * CH)
            pu = pu_ref[rows, :]
            pv = pv_ref[rows, :]
            u = _gelu(pu)
            o, r = _standardize(_gelu(pv))
            vn = o * g_ref[...] + b_ref[...]
            mixed = _gm_mixed(vn, ws_ref, bias_ref[...])
            dyv = dy_ref[rows, :]
            d_ref[rows, :GM_W] = dyv * mixed * _gelu_grad(pu)
            dmixed = dyv * u
            dbias = dbias + dmixed
            dvn_halves = []
            for hf in range(2):
                dm = dmixed[:, hf * 128:(hf + 1) * 128]
                vh = vn[:, hf * 128:(hf + 1) * 128]
                dm0 = jnp.where(lane < 64, dm, 0.0)
                dm1 = dm - dm0
                dws[2 * hf] = dws[2 * hf] + _dot_nt(dm0, vh)
                dws[2 * hf + 1] = dws[2 * hf + 1] + _dot_nt(dm1, vh)
                t0 = _dot(wst_ref[2 * hf], dm)
                t1 = _dot(wst_ref[2 * hf + 1], dm)
                dvn_halves.append(jnp.where(lane < 64, t0, t1))
            dvn = jnp.concatenate(dvn_halves, axis=1)
            dg = dg + jnp.sum(dvn * o, axis=0, keepdims=True)
            db = db + jnp.sum(dvn, axis=0, keepdims=True)
            dv = _standardize_bwd(dvn * g_ref[...], o, r)
            d_ref[rows, GM_W:] = dv * _gelu_grad(pv)
        for h in range(GM_HEADS):
            _acc_out(dws_ref.at[h], dws[h], first)
        _acc_out(dbias_ref, dbias, first)
        _acc_out(dg_ref, dg, first)
        _acc_out(db_ref, db, first)

        @pl.when(pl.program_id(0) == nb - 1)
        def _():
            tot = dbias_ref[...]
            head = lax.broadcasted_iota(jnp.int32, (CH, GM_W), 1) // (GM_W // GM_HEADS)
            out = jnp.zeros((CH, 128), F32)
            for h in range(GM_HEADS):
                s = jnp.sum(jnp.where(head == h, tot, 0.0), axis=1, keepdims=True)
                out = jnp.where(lane == h, s, out)
            dbs_ref[...] = out

    return pl.pallas_call(
        body, grid=(nb,),
        in_specs=[pl.BlockSpec((tm, GM_W), lambda i: (i, 0)), pl.BlockSpec((tm, GM_W), lambda i: (i, 1)),
                  pl.BlockSpec((tm, GM_W), lambda i: (i, 0)),
                  _full((1, GM_W)), _full((1, GM_W)), _full((GM_HEADS, CH, CH)), _full((GM_HEADS, CH, CH)),
                  _full((CH, GM_W))],
        out_specs=[pl.BlockSpec((tm, 2 * GM_W), lambda i: (i, 0)), _full((GM_HEADS, CH, CH)), _full((CH, 128)),
                   _full((1, GM_W)), _full((1, GM_W))],
        out_shape=[S((t, 2 * GM_W), F32), S((GM_HEADS, CH, CH), F32), S((CH, 128), F32),
                   S((1, GM_W), F32), S((1, GM_W), F32)],
        scratch_shapes=[pltpu.VMEM((CH, GM_W), F32)],
        name=name, compiler_params=_cp("arbitrary"))(proj, proj, dy, ln_g, ln_b, ws, wst, bias)


def _rot(x, cos2, sin2):
    return x * cos2 + pltpu.roll(x, HD // 2, 1) * sin2


def _rot_bwd(dx, cos2, sin2):
    return dx * cos2 + pltpu.roll(dx * sin2, HD // 2, 1)


def _rotary(proj, cos2, sin2, name):
    t = proj.shape[0]
    tm = _row_tile(t, 512)
    scale = HD ** -0.5

    def body(q_ref, k_ref, c_ref, s_ref, rq_ref, rk_ref):
        c, s = c_ref[...], s_ref[...]
        for h in range(HEADS):
            cols = slice(h * HD, (h + 1) * HD)
            rq_ref[:, cols] = _rot(q_ref[:, cols], c, s)
            rk_ref[:, cols] = _rot(k_ref[:, cols], c, s) * scale

    return pl.pallas_call(
        body, grid=(t // tm,),
        in_specs=[pl.BlockSpec((tm, RET_W), lambda i: (i, 1)), pl.BlockSpec((tm, RET_W), lambda i: (i, 2)),
                  pl.BlockSpec((tm, HD), lambda i: (i, 0)), pl.BlockSpec((tm, HD), lambda i: (i, 0))],
        out_specs=[pl.BlockSpec((tm, RET_W), lambda i: (i, 0))] * 2,
        out_shape=[S((t, RET_W), F32)] * 2, name=name, compiler_params=_cp("parallel"))(proj, proj, cos2, sin2)


def _ret_scan(lhs, rhs, rhs_col, lp, ls, gp, gs, name):
    t = lhs.shape[0]
    n = t // CH
    r = 4 if n % 4 == 0 else 1
    ns = n // r

    def body(lp_ref, ls_ref, gp_ref, gs_ref, l1_ref, r1_ref, l2_ref, r2_ref, pre_ref, suf_ref, sp_ref, ss_ref):
        @pl.when(pl.program_id(0) == 0)
        def _():
            sp_ref[...] = jnp.zeros_like(sp_ref)
            ss_ref[...] = jnp.zeros_like(ss_ref)

        def kv(l_ref, r_ref, scale, rows):
            lv = l_ref[rows, :] * scale
            rv = r_ref[rows, :]
            return jnp.concatenate([_dot_tn(lv[:, h * HD:(h + 1) * HD], rv[:, h * HD:(h + 1) * HD])
                                    for h in range(HEADS)], axis=1)

        for j in range(r):
            pre_ref[j] = sp_ref[...]
            sp_ref[...] = sp_ref[...] * gp_ref[...] + kv(l1_ref, r1_ref, lp_ref[...], slice(j * CH, (j + 1) * CH))
        for j in reversed(range(r)):
            suf_ref[j] = ss_ref[...]
            ss_ref[...] = ss_ref[...] * gs_ref[...] + kv(l2_ref, r2_ref, ls_ref[...], slice(j * CH, (j + 1) * CH))

    w = HEADS * HD
    return pl.pallas_call(
        body, grid=(ns,),
        in_specs=[_full((CH, w)), _full((CH, w)), _full((1, w)), _full((1, w)),
                  pl.BlockSpec((r * CH, w), lambda s: (s, 0)), pl.BlockSpec((r * CH, w), lambda s: (s, rhs_col)),
                  pl.BlockSpec((r * CH, w), lambda s: (ns - 1 - s, 0)),
                  pl.BlockSpec((r * CH, w), lambda s: (ns - 1 - s, rhs_col))],
        out_specs=[pl.BlockSpec((r, HD, w), lambda s: (s, 0, 0)), pl.BlockSpec((r, HD, w), lambda s: (ns - 1 - s, 0, 0))],
        out_shape=[S((n, HD, w), F32)] * 2, name=name,
        scratch_shapes=[pltpu.VMEM((HD, w), F32), pltpu.VMEM((HD, w), F32)],
        compiler_params=_cp("arbitrary"))(lp, ls, gp, gs, lhs, rhs, lhs, rhs)


def _ret_out(rq, rk, proj, sf, sb, tb, name):
    t = rq.shape[0]
    r = 2 if (t // CH) % 2 == 0 else 1
    tm = r * CH
    w = HEADS * HD

    def body(rq_ref, rk_ref, v_ref, g_ref, sf_ref, sb_ref, dm_ref, xif_ref, xib_ref, a_ref, y_ref):
        for c in range(r):
            rows = slice(c * CH, (c + 1) * CH)
            for h in range(HEADS):
                cols = slice(h * HD, (h + 1) * HD)
                q = rq_ref[rows, cols]
                p = _dot_nt(q, rk_ref[rows, cols]) * dm_ref[h]
                a = (_dot(p, v_ref[rows, cols]) + _dot(q * xif_ref[:, cols], sf_ref[c, :, cols])
                     + _dot(q * xib_ref[:, cols], sb_ref[c, :, cols]))
                a_ref[rows, cols] = a
                o, _ = _standardize(a)
                gv = g_ref[rows, cols]
                y_ref[rows, cols] = o * (gv * _sigmoid(gv))

    return pl.pallas_call(
        body, grid=(t // tm,),
        in_specs=[pl.BlockSpec((tm, w), lambda i: (i, 0)), pl.BlockSpec((tm, w), lambda i: (i, 0)),
                  pl.BlockSpec((tm, w), lambda i: (i, 3)), pl.BlockSpec((tm, w), lambda i: (i, 4)),
                  pl.BlockSpec((r, HD, w), lambda i: (i, 0, 0)), pl.BlockSpec((r, HD, w), lambda i: (i, 0, 0)),
                  _full((HEADS, CH, CH)), _full((CH, w)), _full((CH, w))],
        out_specs=[pl.BlockSpec((tm, w), lambda i: (i, 0))] * 2,
        out_shape=[S((t, w), F32)] * 2, name=name,
        compiler_params=_cp("parallel"))(rq, rk, proj, proj, sf, sb, tb["dm"], tb["xif"], tb["xib"])


def _ret_bwd_pre(dy, a, proj, name):
    t = dy.shape[0]
    tm = _row_tile(t, 512)
    w = HEADS * HD

    def body(dy_ref, a_ref, g_ref, da_ref, dg_ref):
        for h in range(HEADS):
            cols = slice(h * HD, (h + 1) * HD)
            o, r = _standardize(a_ref[:, cols])
            gv = g_ref[:, cols]
            s = _sigmoid(gv)
            dyv = dy_ref[:, cols]
            dg_ref[:, cols] = dyv * o * (s * (1.0 + gv * (1.0 - s)))
            da_ref[:, cols] = _standardize_bwd(dyv * (gv * s), o, r)

    return pl.pallas_call(
        body, grid=(t // tm,),
        in_specs=[pl.BlockSpec((tm, w), lambda i: (i, 0)), pl.BlockSpec((tm, w), lambda i: (i, 0)),
                  pl.BlockSpec((tm, w), lambda i: (i, 4))],
        out_specs=[pl.BlockSpec((tm, w), lambda i: (i, 0))] * 2,
        out_shape=[S((t, w), F32)] * 2, name=name, compiler_params=_cp("parallel"))(dy, a, proj)


def _ret_bwd_main(rq, rk, proj, da, sf, sb, gf, gb, tb, name):
    t = rq.shape[0]
    r = 2 if (t // CH) % 2 == 0 else 1
    tm = r * CH
    w = HEADS * HD
    scale = HD ** -0.5

    def body(rq_ref, rk_ref, v_ref, da_ref, sf_ref, sb_ref, gf_ref, gb_ref, dm_ref, dmt_ref,
             xif_ref, xib_ref, zf_ref, zb_ref, c_ref, s_ref, o_ref):
        for c in range(r):
            rows = slice(c * CH, (c + 1) * CH)
            cos2, sin2 = c_ref[rows, :], s_ref[rows, :]
            for h in range(HEADS):
                cols = slice(h * HD, (h + 1) * HD)
                q, k, v, dav = rq_ref[rows, cols], rk_ref[rows, cols], v_ref[rows, cols], da_ref[rows, cols]
                qm, km, vm, dam = _mx(q), _mx(k), _mx(v), _mx(dav)
                dm, dmt = dm_ref[h], dmt_ref[h]
                pt = _dot_nt(km, qm) * dmt
                dp = _dot_nt(dam, vm) * dm
                dpt = _dot_nt(vm, dam) * dmt
                sfh, sbh, gfh, gbh = sf_ref[c, :, cols], sb_ref[c, :, cols], gf_ref[c, :, cols], gb_ref[c, :, cols]
                zf, zb = zf_ref[:, cols], zb_ref[:, cols]
                dv = _dot(pt, dam) + zf * _dot(km, gfh) + zb * _dot(km, gbh)
                drq = _dot(dp, km) + xif_ref[:, cols] * _dot_nt(dam, sfh) + xib_ref[:, cols] * _dot_nt(dam, sbh)
                drk = _dot(dpt, qm) + _dot_nt(zf * v, gfh) + _dot_nt(zb * v, gbh)
                o_ref[rows, h * HD:(h + 1) * HD] = _rot_bwd(drq, cos2, sin2)
                o_ref[rows, w + h * HD:w + (h + 1) * HD] = _rot_bwd(drk, cos2, sin2) * scale
                o_ref[rows, 2 * w + h * HD:2 * w + (h + 1) * HD] = dv

    st = pl.BlockSpec((r, HD, w), lambda i: (i, 0, 0))
    return pl.pallas_call(
        body, grid=(t // tm,),
        in_specs=[pl.BlockSpec((tm, w), lambda i: (i, 0)), pl.BlockSpec((tm, w), lambda i: (i, 0)),
                  pl.BlockSpec((tm, w), lambda i: (i, 3)), pl.BlockSpec((tm, w), lambda i: (i, 0)),
                  st, st, st, st, _full((HEADS, CH, CH)), _full((HEADS, CH, CH)),
                  _full((CH, w)), _full((CH, w)), _full((CH, w)), _full((CH, w)),
                  pl.BlockSpec((tm, HD), lambda i: (i, 0)), pl.BlockSpec((tm, HD), lambda i: (i, 0))],
        out_specs=pl.BlockSpec((tm, 3 * w), lambda i: (i, 0)),
        out_shape=S((t, 3 * w), F32), name=name,
        compiler_params=_cp("parallel"))(rq, rk, proj, da, sf, sb, gf, gb, tb["dm"], tb["dmt"],
                                         tb["xif"], tb["xib"], tb["zf"], tb["zb"], tb["cos2"], tb["sin2"])


CONV_TM = 256
CONV_SUB = 64
A_COL = (2 * GM_W + 4 * RET_W) // CV_W
G_COL = A_COL + 1


def _halo_specs(t, tm, col):
    nb16 = t // HALO
    per = tm // HALO
    return [pl.BlockSpec((tm, CV_W), lambda i: (i, col)),
            pl.BlockSpec((HALO, CV_W), lambda i: (jnp.maximum(i * per - 1, 0), col)),
            pl.BlockSpec((HALO, CV_W), lambda i: (jnp.minimum((i + 1) * per, nb16 - 1), col))]


def _fill_padded(dst_ref, prev, main, nxt, tm, i, nb):
    dst_ref[0:HALO, :] = jnp.where(i > 0, prev, 0.0)
    dst_ref[HALO:HALO + tm, :] = main
    dst_ref[HALO + tm:2 * HALO + tm, :] = jnp.where(i < nb - 1, nxt, 0.0)


def _conv_fwd(proj, cw, cb, ln_g, ln_b, name):
    t = proj.shape[0]
    tm = _row_tile(t, CONV_TM)
    nb = t // tm

    def body(a_ref, ap_ref, an_ref, g_ref, gp_ref, gn_ref, w_ref, b_ref, lg_ref, lb_ref, y_ref, hc_ref, hp_ref):
        i = pl.program_id(0)
        _fill_padded(hp_ref, ap_ref[...] * _sigmoid(gp_ref[...]), a_ref[...] * _sigmoid(g_ref[...]),
                     an_ref[...] * _sigmoid(gn_ref[...]), tm, i, nb)
        for sb in range(tm // CONV_SUB):
            acc = jnp.zeros((CONV_SUB, CV_W), F32) + b_ref[...]
            for k in range(KCONV):
                acc = acc + w_ref[k:k + 1, :] * hp_ref[pl.ds(sb * CONV_SUB + k + 1, CONV_SUB), :]
            rows = slice(sb * CONV_SUB, (sb + 1) * CONV_SUB)
            hc_ref[rows, :] = acc
            o, _ = _standardize(acc)
            z = o * lg_ref[...] + lb_ref[...]
            y_ref[rows, :] = z * _sigmoid(z)

    return pl.pallas_call(
        body, grid=(nb,),
        in_specs=_halo_specs(t, tm, A_COL) + _halo_specs(t, tm, G_COL)
        + [_full((32, CV_W)), _full((1, CV_W)), _full((1, CV_W)), _full((1, CV_W))],
        out_specs=[pl.BlockSpec((tm, CV_W), lambda i: (i, 0))] * 2,
        out_shape=[S((t, CV_W), F32)] * 2, name=name,
        scratch_shapes=[pltpu.VMEM((tm + 2 * HALO, CV_W), F32)],
        compiler_params=_cp("parallel"))(proj, proj, proj, proj, proj, proj, cw, cb, ln_g, ln_b)


def _conv_bwd(proj, dy, hc, cw, ln_g, ln_b, name):
    t = proj.shape[0]
    tm = _row_tile(t, CONV_TM)
    nb = t // tm

    def body(a_ref, ap_ref, an_ref, g_ref, gp_ref, gn_ref, dy_ref, dyp_ref, dyn_ref, hc_ref, hcp_ref, hcn_ref,
             w_ref, lg_ref, lb_ref, d_ref, dw_ref, dcb_ref, dlg_ref, dlb_ref, hp_ref, dhp_ref, dwacc_ref):
        i = pl.program_id(0)
        first = i == 0

        def dhc_of(dyv, hcv):
            o, r = _standardize(hcv)
            z = o * lg_ref[...] + lb_ref[...]
            s = _sigmoid(z)
            dz = dyv * (s * (1.0 + z * (1.0 - s)))
            return _standardize_bwd(dz * lg_ref[...], o, r), dz, o

        dhc, dz, o = dhc_of(dy_ref[...], hc_ref[...])
        _acc_out(dlg_ref, jnp.sum(dz * o, axis=0, keepdims=True), first)
        _acc_out(dlb_ref, jnp.sum(dz, axis=0, keepdims=True), first)
        _acc_out(dcb_ref, jnp.sum(dhc, axis=0, keepdims=True), first)
        _fill_padded(dhp_ref, dhc_of(dyp_ref[...], hcp_ref[...])[0], dhc, dhc_of(dyn_ref[...], hcn_ref[...])[0],
                     tm, i, nb)
        _fill_padded(hp_ref, ap_ref[...] * _sigmoid(gp_ref[...]), a_ref[...] * _sigmoid(g_ref[...]),
                     an_ref[...] * _sigmoid(gn_ref[...]), tm, i, nb)

        @pl.when(first)
        def _():
            dwacc_ref[...] = jnp.zeros_like(dwacc_ref)

        for sb in range(tm // CONV_SUB):
            base = sb * CONV_SUB
            dmain = dhp_ref[pl.ds(HALO + base, CONV_SUB), :]
            dh = jnp.zeros((CONV_SUB, CV_W), F32)
            for k in range(KCONV):
                dh = dh + w_ref[k:k + 1, :] * dhp_ref[pl.ds(base + 2 * HALO - 1 - k, CONV_SUB), :]
                prod = dmain * hp_ref[pl.ds(base + k + 1, CONV_SUB), :]
                dwacc_ref[k * 8:(k + 1) * 8, :] += jnp.sum(prod.reshape(CONV_SUB // 8, 8, CV_W), axis=0)
            rows = slice(base, base + CONV_SUB)
            s = _sigmoid(g_ref[rows, :])
            d_ref[rows, :CV_W] = dh * s
            d_ref[rows, CV_W:] = dh * a_ref[rows, :] * (s * (1.0 - s))

        @pl.when(i == nb - 1)
        def _():
            for k in range(KCONV):
                dw_ref[k:k + 1, :] = jnp.sum(dwacc_ref[k * 8:(k + 1) * 8, :], axis=0, keepdims=True)
            dw_ref[KCONV:32, :] = jnp.zeros((32 - KCONV, CV_W), F32)

    hs = [pl.BlockSpec((tm, CV_W), lambda i: (i, 0)),
          pl.BlockSpec((HALO, CV_W), lambda i: (jnp.maximum(i * (tm // HALO) - 1, 0), 0)),
          pl.BlockSpec((HALO, CV_W), lambda i: (jnp.minimum((i + 1) * (tm // HALO), t // HALO - 1), 0))]
    return pl.pallas_call(
        body, grid=(nb,),
        in_specs=_halo_specs(t, tm, A_COL) + _halo_specs(t, tm, G_COL) + hs + hs
        + [_full((32, CV_W)), _full((1, CV_W)), _full((1, CV_W))],
        out_specs=[pl.BlockSpec((tm, 2 * CV_W), lambda i: (i, 0)), _full((32, CV_W)), _full((1, CV_W)),
                   _full((1, CV_W)), _full((1, CV_W))],
        out_shape=[S((t, 2 * CV_W), F32), S((32, CV_W), F32), S((1, CV_W), F32), S((1, CV_W), F32), S((1, CV_W), F32)],
        name=name,
        scratch_shapes=[pltpu.VMEM((tm + 2 * HALO, CV_W), F32), pltpu.VMEM((tm + 2 * HALO, CV_W), F32),
                        pltpu.VMEM((32 * 8, CV_W), F32)],
        compiler_params=_cp("arbitrary"))(proj, proj, proj, proj, proj, proj, dy, dy, dy, hc, hc, hc, cw, ln_g, ln_b)


def _loss_head(x, g, target, name):
    t = x.shape[0]
    tm = _row_tile(t, 512)

    def body(x_ref, g_ref, t_ref, dx_ref, dg_ref, l_ref):
        first = pl.program_id(0) == 0
        xv = x_ref[...]
        r = _rms_r(xv)
        e = xv * r * g_ref[...] - t_ref[...]
        dx, dgrow = _rms_bwd(e * (1.0 / D), xv, r, g_ref[...])
        dx_ref[...] = dx
        _acc_out(dg_ref, jnp.sum(dgrow, axis=0, keepdims=True), first)
        part = 0.5 * jnp.sum(jnp.mean(e * e, axis=-1, keepdims=True), axis=0, keepdims=True)
        _acc_out(l_ref, jnp.broadcast_to(part, (8, 128)), first)

    return pl.pallas_call(
        body, grid=(t // tm,),
        in_specs=[pl.BlockSpec((tm, D), lambda i: (i, 0)), _full((1, D)), pl.BlockSpec((tm, D), lambda i: (i, 0))],
        out_specs=[pl.BlockSpec((tm, D), lambda i: (i, 0)), _full((1, D)), _full((8, 128))],
        out_shape=[S((t, D), F32), S((1, D), F32), S((8, 128), F32)], name=name,
        compiler_params=_cp("arbitrary"))(x, g, target)


def _as2d(a):
    return a.reshape(-1, a.shape[-1])


def _ew_tile(rows, cols, n_arrays):
    budget = VMEM_LIMIT // 2
    tr = rows
    while tr * cols * 4 * n_arrays * 2 > budget and tr % 16 == 0:
        tr //= 2
    assert rows % tr == 0
    return tr


def _sum_cast(arrs, dtype, name):
    shape = arrs[0].shape
    a2 = [_as2d(a) for a in arrs]
    rows, cols = a2[0].shape
    tr = _ew_tile(rows, cols, len(arrs) + 1)
    n = len(arrs)

    def body(*refs):
        acc = refs[0][...].astype(F32)
        for q in range(1, n):
            acc = acc + refs[q][...].astype(F32)
        refs[n][...] = acc.astype(dtype)

    spec = pl.BlockSpec((tr, cols), lambda i: (i, 0))
    out = pl.pallas_call(body, grid=(rows // tr,), in_specs=[spec] * n, out_specs=spec,
                         out_shape=S((rows, cols), dtype), name=name, compiler_params=_cp("parallel"))(*a2)
    return out.reshape(shape)


def _adamw(w, g, m, v, name):
    shape = w.shape
    w2, g2, m2, v2 = _as2d(w), _as2d(g), _as2d(m), _as2d(v)
    rows, cols = w2.shape
    tr = _ew_tile(rows, cols, 7)

    def body(w_ref, g_ref, m_ref, v_ref, d_ref, nm_ref, nv_ref):
        gv = g_ref[...]
        nm = ADAM_B1 * m_ref[...] + (1.0 - ADAM_B1) * gv
        nv = ADAM_B2 * v_ref[...] + (1.0 - ADAM_B2) * (gv * gv)
        m_hat = nm / (1.0 - ADAM_B1 ** ADAM_STEP)
        v_hat = nv / (1.0 - ADAM_B2 ** ADAM_STEP)
        d_ref[...] = -ADAM_LR * (m_hat / (jnp.sqrt(v_hat) + ADAM_EPS) + ADAM_WD * w_ref[...])
        nm_ref[...] = nm
        nv_ref[...] = nv

    spec = pl.BlockSpec((tr, cols), lambda i: (i, 0))
    outs = pl.pallas_call(body, grid=(rows // tr,), in_specs=[spec] * 4, out_specs=[spec] * 3,
                          out_shape=[S((rows, cols), F32)] * 3, name=name,
                          compiler_params=_cp("parallel"))(w2, g2, m2, v2)
    return tuple(o.reshape(shape) for o in outs)


BIG = (("w_in", "col"), ("w_out", "row"), ("w_ffn_in", "col"), ("w_ffn_out", "row"))
NBIG = len(BIG)
ANY = pl.BlockSpec(memory_space=pl.ANY)


def _shard_view(a):
    l, r, c = a.shape
    return a.reshape(l, 2, r // 2, c)


def _full_view(kind, l, r, c):
    return (N_CHIPS, l, 2, r // 2, c) if kind == "col" else (l, N_CHIPS, 2, r // 2, c)


def _region(kind, ref5, j, h):
    return ref5.at[j, :, h] if kind == "col" else ref5.at[:, j, h]


def _chip_region(kind, ref5, j):
    return ref5.at[j] if kind == "col" else ref5.at[:, j]


def _mesh_pos():
    x, y, c = lax.axis_index("x"), lax.axis_index("y"), lax.axis_index("c")
    chips = [(1 - x, y), (x, 1 - y), (1 - x, 1 - y)]
    return x, y, c, 2 * x + y, chips, [2 * cx + cy for cx, cy in chips]


def _all_gather_big(shards):
    views = [_shard_view(a) for a in shards]
    kinds = [k for _, k in BIG]
    out_shapes = [S(_full_view(k, *a.shape), a.dtype) for k, a in zip(kinds, shards)]

    def body(*refs):
        s_refs, f_refs = refs[:NBIG], refs[NBIG:2 * NBIG]
        loc, isend, irecv, dsend, drecv = refs[2 * NBIG:]
        x, y, c, me, chips, cj = _mesh_pos()
        sib = (x, y, 1 - c)
        locals_, sends = [], []
        for w in range(NBIG):
            cp = pltpu.make_async_copy(s_refs[w], _chip_region(kinds[w], f_refs[w], me), loc.at[w])
            cp.start()
            locals_.append(cp)
            for k in range(3):
                cp = pltpu.make_async_remote_copy(
                    src_ref=s_refs[w].at[:, c], dst_ref=_region(kinds[w], f_refs[w], me, c),
                    send_sem=isend.at[w, k], recv_sem=irecv.at[w, k],
                    device_id=(*chips[k], c), device_id_type=MESH)
                cp.start()
                sends.append(cp)
        for k in range(3):
            for w in range(NBIG):
                reg = _region(kinds[w], f_refs[w], cj[k], c)
                pltpu.make_async_remote_copy(
                    src_ref=reg, dst_ref=reg, send_sem=isend.at[w, k], recv_sem=irecv.at[w, k],
                    device_id=(*chips[k], c), device_id_type=MESH).wait_recv()
                cp = pltpu.make_async_remote_copy(
                    src_ref=reg, dst_ref=reg, send_sem=dsend.at[w, k], recv_sem=drecv.at[w, k],
                    device_id=sib, device_id_type=MESH)
                cp.start()
                sends.append(cp)
        for k in range(3):
            for w in range(NBIG):
                reg = _region(kinds[w], f_refs[w], cj[k], 1 - c)
                pltpu.make_async_remote_copy(
                    src_ref=reg, dst_ref=reg, send_sem=dsend.at[w, k], recv_sem=drecv.at[w, k],
                    device_id=sib, device_id_type=MESH).wait_recv()
        for cp in sends:
            cp.wait_send()
        for cp in locals_:
            cp.wait()

    return pl.pallas_call(
        body, in_specs=[ANY] * NBIG, out_specs=[ANY] * NBIG, out_shape=out_shapes, name="all_gather_weights",
        scratch_shapes=[pltpu.SemaphoreType.DMA((NBIG,))] + [pltpu.SemaphoreType.DMA((NBIG, 3))] * 4)(*views)


def _pair_exchange(grads5):
    def half_shape(a):
        return a.shape[:2] + a.shape[3:]

    out_shapes = [S(half_shape(a), a.dtype) for a in grads5] * 2

    def body(*refs):
        g_refs, mine, theirs = refs[:NBIG], refs[NBIG:2 * NBIG], refs[2 * NBIG:3 * NBIG]
        loc, send, recv = refs[3 * NBIG:]
        x, y, c, *_ = _mesh_pos()
        cps = []
        for w in range(NBIG):
            cp = pltpu.make_async_remote_copy(
                src_ref=g_refs[w].at[:, :, 1 - c], dst_ref=theirs[w], send_sem=send.at[w], recv_sem=recv.at[w],
                device_id=(x, y, 1 - c), device_id_type=MESH)
            cp.start()
            cps.append(cp)
            cp = pltpu.make_async_copy(g_refs[w].at[:, :, c], mine[w], loc.at[w])
            cp.start()
            cps.append(cp)
        for cp in cps:
            cp.wait()

    outs = pl.pallas_call(
        body, in_specs=[ANY] * NBIG, out_specs=[ANY] * (2 * NBIG), out_shape=out_shapes, name="grad_pair_exchange",
        scratch_shapes=[pltpu.SemaphoreType.DMA((NBIG,))] * 3)(*grads5)
    return outs[:NBIG], outs[NBIG:]


def _chip_scatter(qs):
    kinds = [k for _, k in BIG]

    def piece(kind, ref, j):
        return ref.at[j] if kind == "col" else ref.at[:, j]

    def piece_shape(kind, a):
        return a.shape[1:] if kind == "col" else (a.shape[0],) + a.shape[2:]

    own_shapes = [S(piece_shape(k, a), a.dtype) for k, a in zip(kinds, qs)]
    got_shapes = [S((3,) + s.shape, s.dtype) for s in own_shapes]

    def body(*refs):
        q_refs, own, got = refs[:NBIG], refs[NBIG:2 * NBIG], refs[2 * NBIG:3 * NBIG]
        loc, send, recv = refs[3 * NBIG:]
        x, y, c, me, chips, cj = _mesh_pos()
        cps = []
        for w in range(NBIG):
            cp = pltpu.make_async_copy(piece(kinds[w], q_refs[w], me), own[w], loc.at[w])
            cp.start()
            cps.append(cp)
            for k in range(3):
                cp = pltpu.make_async_remote_copy(
                    src_ref=piece(kinds[w], q_refs[w], cj[k]), dst_ref=got[w].at[k],
                    send_sem=send.at[w, k], recv_sem=recv.at[w, k],
                    device_id=(*chips[k], c), device_id_type=MESH)
                cp.start()
                cps.append(cp)
        for cp in cps:
            cp.wait()

    outs = pl.pallas_call(
        body, in_specs=[ANY] * NBIG, out_specs=[ANY] * (2 * NBIG), out_shape=own_shapes + got_shapes,
        name="grad_chip_scatter",
        scratch_shapes=[pltpu.SemaphoreType.DMA((NBIG,))] + [pltpu.SemaphoreType.DMA((NBIG, 3))] * 2)(*qs)
    return outs[:NBIG], outs[NBIG:]


def _pair_gather(halves):
    out_shapes = [S((a.shape[0], 2) + a.shape[1:], a.dtype) for a in halves]

    def body(*refs):
        h_refs, o_refs = refs[:NBIG], refs[NBIG:2 * NBIG]
        loc, send, recv = refs[2 * NBIG:]
        x, y, c, *_ = _mesh_pos()
        cps = []
        for w in range(NBIG):
            cp = pltpu.make_async_remote_copy(
                src_ref=h_refs[w], dst_ref=o_refs[w].at[:, c], send_sem=send.at[w], recv_sem=recv.at[w],
                device_id=(x, y, 1 - c), device_id_type=MESH)
            cp.start()
            cps.append(cp)
            cp = pltpu.make_async_copy(h_refs[w], o_refs[w].at[:, c], loc.at[w])
            cp.start()
            cps.append(cp)
        for cp in cps:
            cp.wait()

    outs = pl.pallas_call(
        body, in_specs=[ANY] * NBIG, out_specs=[ANY] * NBIG, out_shape=out_shapes, name="grad_pair_gather",
        scratch_shapes=[pltpu.SemaphoreType.DMA((NBIG,))] * 3)(*halves)
    return [o.reshape(o.shape[0], 2 * o.shape[2], o.shape[3]) for o in outs]


def _all_reduce_small(p, name):
    rows = p.shape[0]

    def body(p_ref, o_ref, gath, send, recv):
        x, y, c = lax.axis_index("x"), lax.axis_index("y"), lax.axis_index("c")
        my_id = 4 * x + 2 * y + c
        gath[my_id] = p_ref[...]
        cps = []
        for r in range(1, N_DEV):
            bx, by, bc = (r >> 2) & 1, (r >> 1) & 1, r & 1
            tx, ty, tc = (1 - x if bx else x), (1 - y if by else y), (1 - c if bc else c)
            cp = pltpu.make_async_remote_copy(
                src_ref=p_ref, dst_ref=gath.at[my_id], send_sem=send.at[r - 1], recv_sem=recv.at[r - 1],
                device_id=(tx, ty, tc), device_id_type=MESH)
            cp.start()
            cps.append((cp, 4 * tx + 2 * ty + tc))
        for r, (cp, peer) in enumerate(cps):
            pltpu.make_async_remote_copy(
                src_ref=p_ref, dst_ref=gath.at[peer], send_sem=send.at[r], recv_sem=recv.at[r],
                device_id=(x, y, c), device_id_type=MESH).wait_recv()
        for cp, _ in cps:
            cp.wait_send()
        acc = gath[0]
        for s in range(1, N_DEV):
            acc = acc + gath[s]
        o_ref[...] = acc

    vm = pl.BlockSpec(memory_space=pltpu.VMEM)
    return pl.pallas_call(
        body, in_specs=[vm], out_specs=vm, out_shape=S((rows, 128), F32), name=name,
        scratch_shapes=[pltpu.VMEM((N_DEV, rows, 128), F32), pltpu.SemaphoreType.DMA((N_DEV - 1,)),
                        pltpu.SemaphoreType.DMA((N_DEV - 1,))],
        compiler_params=pltpu.CompilerParams(vmem_limit_bytes=VMEM_LIMIT))(p)


def _reduce_pieces(own, got, name):
    shape = own.shape
    o2 = _as2d(own)
    rows, cols = o2.shape
    g3 = got.reshape(3, rows, cols)
    tr = _ew_tile(rows, cols, 5)

    def body(o_ref, g0_ref, g1_ref, g2_ref, out_ref):
        acc = o_ref[...].astype(F32)
        for r in (g0_ref, g1_ref, g2_ref):
            acc = acc + r[...].astype(F32)
        out_ref[...] = acc

    spec = pl.BlockSpec((tr, cols), lambda i: (i, 0))
    gspecs = [pl.BlockSpec((None, tr, cols), functools.partial(lambda k, i: (k, i, 0), k)) for k in range(3)]
    out = pl.pallas_call(body, grid=(rows // tr,), in_specs=[spec] + gspecs, out_specs=spec,
                         out_shape=S((rows, cols), F32), name=name, compiler_params=_cp("parallel"))(o2, g3, g3, g3)
    return out.reshape(shape)


PACK_UNIT = 8 * 128


def _pack(arrs):
    parts = []
    for a in arrs:
        flat = a.reshape(-1)
        pad = (-flat.shape[0]) % PACK_UNIT
        parts.append(jnp.pad(flat, (0, pad)).reshape(-1, 128))
    return jnp.concatenate(parts, axis=0)


def _unpack(buf, shapes):
    outs, row = [], 0
    for shp in shapes:
        n = int(np.prod(shp))
        rows = -(-n // PACK_UNIT) * 8
        outs.append(buf[row:row + rows].reshape(-1)[:n].reshape(shp))
        row += rows
    return outs


SMALL = ("norm1_g", "gm_ln_g", "gm_ln_b", "gm_ws", "gm_bs", "conv_w", "conv_b", "conv_ln_g", "conv_ln_b",
         "norm2_g", "final_g")
WEIGHTS = ("norm1_g", "w_in", "gm_ln_g", "gm_ln_b", "gm_ws", "gm_bs", "conv_w", "conv_b", "conv_ln_g",
           "conv_ln_b", "w_out", "norm2_g", "w_ffn_in", "w_ffn_out", "final_g")


def kernel(x, norm1_g, w_in, gm_ln_g, gm_ln_b, gm_ws, gm_bs, conv_w, conv_b, conv_ln_g, conv_ln_b, w_out, norm2_g, w_ffn_in, w_ffn_out, final_g, loss_target, m_norm1_g, m_w_in, m_gm_ln_g, m_gm_ln_b, m_gm_ws, m_gm_bs, m_conv_w, m_conv_b, m_conv_ln_g, m_conv_ln_b, m_w_out, m_norm2_g, m_w_ffn_in, m_w_ffn_out, m_final_g, v_norm1_g, v_w_in, v_gm_ln_g, v_gm_ln_b, v_gm_ws, v_gm_bs, v_conv_w, v_conv_b, v_conv_ln_g, v_conv_ln_b, v_w_out, v_norm2_g, v_w_ffn_in, v_w_ffn_out, v_final_g):
    given = dict(locals())
    t = x.shape[1]
    xc = x.reshape(t, D)
    target = loss_target.reshape(t, D)
    me = 2 * lax.axis_index("x") + lax.axis_index("y")
    core = lax.axis_index("c")
    tb = _tables(t)

    shards = [_sum_cast([given[n]], MXU_DTYPE, f"cast_{n}") for n, _ in BIG]
    gathered = _all_gather_big(shards)
    wfull = {}
    for (n, kind), g5 in zip(BIG, gathered):
        if kind == "col":
            wfull[n] = g5.reshape(g5.shape[0], g5.shape[1], 2 * g5.shape[3], g5.shape[4])
        else:
            wfull[n] = g5.reshape(g5.shape[0], N_CHIPS * 2 * g5.shape[3], g5.shape[4])
    cshard = CV_W // N_CHIPS
    placed = lax.dynamic_update_slice(jnp.zeros((DEPTH, KCONV, CV_W), F32),
                                      conv_w * (core == 0).astype(F32), (0, 0, me * cshard))
    conv_w_full = _unpack(_all_reduce_small(_pack([placed]), "gather_conv_w"), [(DEPTH, KCONV, CV_W)])[0]
    cw32 = jnp.pad(conv_w_full, ((0, 0), (0, 32 - KCONV), (0, 0)))

    def row(a, l):
        return a[l].reshape(1, -1)

    saved = []
    for l in range(DEPTH):
        sv = {"x": xc}
        bias = jnp.repeat(gm_bs[l].T, GM_W // GM_HEADS, axis=1)
        proj = _norm_mm(xc, row(norm1_g, l), wfull["w_in"], l, f"in_proj{l}", 512)
        y_gm = _gm_fwd(proj, row(gm_ln_g, l), row(gm_ln_b, l), gm_ws[l], bias, f"gm_fwd{l}")
        rq, rk = _rotary(proj, tb["cos2"], tb["sin2"], f"rotary{l}")
        sf, sb = _ret_scan(rk, proj, 3, tb["zf"], tb["zb"], tb["gcf"], tb["gcb"], f"ret_state{l}")
        a, y_ret = _ret_out(rq, rk, proj, sf, sb, tb, f"ret_out{l}")
        y_cv, hc = _conv_fwd(proj, cw32[l], row(conv_b, l), row(conv_ln_g, l), row(conv_ln_b, l), f"conv_fwd{l}")
        x_mid = _parts_mm_res([y_gm, y_ret, y_cv], wfull["w_out"], l, xc, f"out_proj{l}")
        ff = _norm_mm(x_mid, row(norm2_g, l), wfull["w_ffn_in"], l, f"ffn_in{l}", 256)
        xc = _swiglu_mm_res(ff, wfull["w_ffn_out"], l, x_mid, f"ffn_out{l}")
        sv.update(bias=bias, proj=proj, y_gm=y_gm, rq=rq, rk=rk, sf=sf, sb=sb, a=a, y_ret=y_ret, y_cv=y_cv,
                  hc=hc, x_mid=x_mid, ff=ff)
        saved.append(sv)

    dx, d_final_g, lpart = _loss_head(xc, final_g.reshape(1, D), target, "loss_head")
    loss = lax.psum(lpart[0, 0], ("x", "y", "c"))

    small_g = {n: [None] * DEPTH for n in SMALL}
    big_g = {n: [None] * DEPTH for n, _ in BIG}
    for l in reversed(range(DEPTH)):
        sv = saved[l]
        proj = sv["proj"]
        dff = _dx_swiglu(dx, wfull["w_ffn_out"], l, sv["ff"], f"ffn_out_dx{l}")
        big_g["w_ffn_out"][l] = _dw_swiglu(sv["ff"], dx, f"ffn_out_dw{l}")
        dx_mid, dg2 = _dx_norm([dff], wfull["w_ffn_in"], l, sv["x_mid"], row(norm2_g, l), dx, f"ffn_in_dx{l}", 256)
        big_g["w_ffn_in"][l] = _dw_norm_cols(sv["x_mid"], row(norm2_g, l), dff, w_ffn_in.shape[2], f"ffn_in_dw{l}")
        dy_gm, dy_ret, dy_cv = _dx_parts(dx_mid, wfull["w_out"], l, [GM_W, RET_W, CV_W], f"out_proj_dx{l}")
        big_g["w_out"][l] = _dw_parts([sv["y_gm"], sv["y_ret"], sv["y_cv"]], dx_mid, f"out_proj_dw{l}")
        d_cv, dcw, dcb, dclg, dclb = _conv_bwd(proj, dy_cv, sv["hc"], cw32[l], row(conv_ln_g, l),
                                               row(conv_ln_b, l), f"conv_bwd{l}")
        da, d_g = _ret_bwd_pre(dy_ret, sv["a"], proj, f"ret_bwd_pre{l}")
        gb_, gf_ = _ret_scan(sv["rq"], da, 0, tb["xib"], tb["xif"], tb["gcb"], tb["gcf"], f"ret_bwd_state{l}")
        d_qkv = _ret_bwd_main(sv["rq"], sv["rk"], proj, da, sv["sf"], sv["sb"], gf_, gb_, tb, f"ret_bwd_main{l}")
        d_gm, dws, dbs, dglg, dglb = _gm_bwd(proj, dy_gm, row(gm_ln_g, l), row(gm_ln_b, l), gm_ws[l],
                                             jnp.swapaxes(gm_ws[l], 1, 2), sv["bias"], f"gm_bwd{l}")
        dparts = [d_gm, d_qkv, d_g, d_cv]
        dx, dg1 = _dx_norm(dparts, wfull["w_in"], l, sv["x"], row(norm1_g, l), dx_mid, f"in_proj_dx{l}", 512)
        big_g["w_in"][l] = _dw_norm_parts(sv["x"], row(norm1_g, l), dparts, w_in.shape[2], f"in_proj_dw{l}")
        for n, val in (("norm1_g", dg1[0]), ("gm_ln_g", dglg[0]), ("gm_ln_b", dglb[0]), ("gm_ws", dws),
                       ("gm_bs", dbs[:, :GM_HEADS].T), ("conv_w", dcw[:KCONV]), ("conv_b", dcb[0]),
                       ("conv_ln_g", dclg[0]), ("conv_ln_b", dclb[0]), ("norm2_g", dg2[0])):
            small_g[n][l] = val

    small_shapes = [given[n].shape if n != "conv_w" else (DEPTH, KCONV, CV_W) for n in SMALL]
    partials = [d_final_g[0] if n == "final_g" else jnp.stack(small_g[n]) for n in SMALL]
    reduced = dict(zip(SMALL, _unpack(_all_reduce_small(_pack(partials), "all_reduce_small_grads"), small_shapes)))
    reduced["conv_w"] = lax.dynamic_slice(reduced["conv_w"], (0, 0, me * cshard), (DEPTH, KCONV, cshard))

    grads5 = []
    for n, kind in BIG:
        l_, r_, c_ = given[n].shape
        g = jnp.stack(big_g[n], axis=1 if kind == "col" else 0)
        grads5.append(g.reshape(_full_view(kind, l_, r_, c_)))
    mine, theirs = _pair_exchange(grads5)
    qs = [_sum_cast([a, b], MXU_DTYPE, f"pair_sum_{n}") for (n, _), a, b in zip(BIG, mine, theirs)]
    own, got = _chip_scatter(qs)
    halves = [_reduce_pieces(o, g, f"chip_sum_{n}") for (n, _), o, g in zip(BIG, own, got)]
    grads = dict(zip([n for n, _ in BIG], _pair_gather(halves)))
    grads.update(reduced)

    delta, new_m, new_v = {}, {}, {}
    for n, _ in BIG:
        delta[n], new_m[n], new_v[n] = _adamw(given[n], grads[n], given["m_" + n], given["v_" + n], f"adamw_{n}")
    shapes = [given[n].shape for n in SMALL]
    packed = [_pack([src[n] if src is grads else src[p + n] for n in SMALL])
              for src, p in ((given, ""), (grads, ""), (given, "m_"), (given, "v_"))]
    outs = _adamw(*packed, "adamw_small")
    for dst, buf in zip((delta, new_m, new_v), outs):
        dst.update(zip(SMALL, _unpack(buf, shapes)))

    return (loss, dx.reshape(1, t, D), *[grads[n] for n in WEIGHTS], *[delta[n] for n in WEIGHTS],
            *[new_m[n] for n in WEIGHTS], *[new_v[n] for n in WEIGHTS])
```

```python
import functools
import math

import numpy as np
import jax
import jax.numpy as jnp
from jax import lax
from jax.experimental import pallas as pl
from jax.experimental.pallas import tpu as pltpu

F32 = jnp.float32
BF16 = jnp.bfloat16
MXU_DTYPE = BF16
S = jax.ShapeDtypeStruct

D = 1024
DEPTH = 2
GM_W = 256
GM_HEADS = 4
RET_W = 512
HEADS = 4
HD = 128
CV_W = 256
KCONV = 31
IN_W = 2 * GM_W + 4 * RET_W + 2 * CV_W
FFN_H = 2816
CH = 128
ROPE_BASE = 10000.0
EPS = 1e-6
N_CHIPS = 4
N_DEV = 8
HALO = 16

ADAM_LR = 0.001
ADAM_B1 = 0.9
ADAM_B2 = 0.999
ADAM_EPS = 1e-08
ADAM_WD = 0.01
ADAM_STEP = 10

VMEM_LIMIT = 52 * 1024 * 1024
MESH = pl.DeviceIdType.MESH


def _cp(*sem, vmem=VMEM_LIMIT):
    return pltpu.CompilerParams(dimension_semantics=tuple(sem), vmem_limit_bytes=vmem)


def _mx(a):
    return a.astype(MXU_DTYPE)


def _dot(a, b):
    return jnp.dot(_mx(a), _mx(b), preferred_element_type=F32)


def _dot_nt(a, b):
    return lax.dot_general(_mx(a), _mx(b), (((1,), (1,)), ((), ())), preferred_element_type=F32)


def _dot_tn(a, b):
    return lax.dot_general(_mx(a), _mx(b), (((0,), (0,)), ((), ())), preferred_element_type=F32)


def _sigmoid(x):
    return 1.0 / (1.0 + jnp.exp(-x))


def _gelu(x):
    return 0.5 * x * (1.0 + lax.erf(x * (1.0 / math.sqrt(2.0))))


def _gelu_grad(x):
    return 0.5 * (1.0 + lax.erf(x * (1.0 / math.sqrt(2.0)))) + x * jnp.exp(-0.5 * x * x) * (1.0 / math.sqrt(2.0 * math.pi))


def _rms_r(x):
    return lax.rsqrt(jnp.mean(x * x, axis=-1, keepdims=True) + EPS)


def _rms_bwd(dh, x, r, g):
    u = dh * g
    dx = r * u - x * (r * r * r) * jnp.mean(u * x, axis=-1, keepdims=True)
    return dx, dh * x * r


def _standardize(a):
    mu = jnp.mean(a, axis=-1, keepdims=True)
    d = a - mu
    r = lax.rsqrt(jnp.mean(d * d, axis=-1, keepdims=True) + EPS)
    return d * r, r


def _standardize_bwd(do, o, r):
    return r * (do - jnp.mean(do, axis=-1, keepdims=True) - o * jnp.mean(do * o, axis=-1, keepdims=True))


def _acc_out(ref, val, first):
    @pl.when(first)
    def _():
        ref[...] = val

    @pl.when(jnp.logical_not(first))
    def _():
        ref[...] += val


def _row_tile(t, pref):
    tm = min(t, pref)
    assert t % tm == 0, (t, tm)
    return tm


def _segments(part_widths, shard_w):
    bounds = {0}
    off = 0
    for w in part_widths:
        off += w
        bounds.add(off)
    total = off
    for j in range(1, total // shard_w + 1):
        bounds.add(j * shard_w)
    bounds = sorted(bounds)
    starts = np.cumsum([0] + list(part_widths))
    segs = []
    for a, b in zip(bounds[:-1], bounds[1:]):
        p = int(np.searchsorted(starts, a, side="right") - 1)
        segs.append((p, a - int(starts[p]), a // shard_w, a % shard_w, b - a))
    return segs


def _wcol_spec(w, l):
    return pl.BlockSpec((N_CHIPS, None) + w.shape[2:], lambda *_: (0, l, 0, 0))


def _wrow_spec(w, l):
    return pl.BlockSpec((None,) + w.shape[1:], lambda *_: (l, 0, 0))


def _norm_mm(x, g, w, l, name, tm_pref):
    t = x.shape[0]
    nc = w.shape[3]
    tm = _row_tile(t, tm_pref)

    def body(x_ref, g_ref, w_ref, o_ref):
        xv = x_ref[...]
        h = _mx(xv * _rms_r(xv) * g_ref[...])
        for j in range(N_CHIPS):
            o_ref[:, j * nc:(j + 1) * nc] = jnp.dot(h, w_ref[j], preferred_element_type=F32)

    return pl.pallas_call(
        body, grid=(t // tm,),
        in_specs=[pl.BlockSpec((tm, D), lambda i: (i, 0)), pl.BlockSpec((1, D), lambda i: (0, 0)),
                  _wcol_spec(w, l)],
        out_specs=pl.BlockSpec((tm, N_CHIPS * nc), lambda i: (i, 0)),
        out_shape=S((t, N_CHIPS * nc), F32), name=name, compiler_params=_cp("parallel"))(x, g, w)


def _parts_mm_res(parts, w, l, res, name):
    t = res.shape[0]
    tm = _row_tile(t, 512)
    widths = [p.shape[1] for p in parts]
    offs = np.cumsum([0] + widths)
    n = len(parts)

    def body(*refs):
        p_refs, w_ref, r_ref, o_ref = refs[:n], refs[n], refs[n + 1], refs[n + 2]
        acc = r_ref[...]
        for p in range(n):
            acc = acc + _dot(p_refs[p][...], w_ref[int(offs[p]):int(offs[p + 1]), :])
        o_ref[...] = acc

    return pl.pallas_call(
        body, grid=(t // tm,),
        in_specs=[pl.BlockSpec((tm, wd), lambda i: (i, 0)) for wd in widths]
        + [_wrow_spec(w, l), pl.BlockSpec((tm, D), lambda i: (i, 0))],
        out_specs=pl.BlockSpec((tm, D), lambda i: (i, 0)),
        out_shape=S((t, D), F32), name=name, compiler_params=_cp("parallel"))(*parts, w, res)


def _swiglu(ff):
    gate = ff[:, :FFN_H]
    up = ff[:, FFN_H:]
    return gate * _sigmoid(gate) * up


def _swiglu_mm_res(ff, w, l, res, name):
    t = res.shape[0]
    tm = _row_tile(t, 256)

    def body(f_ref, w_ref, r_ref, o_ref):
        o_ref[...] = r_ref[...] + _dot(_swiglu(f_ref[...]), w_ref[...])

    return pl.pallas_call(
        body, grid=(t // tm,),
        in_specs=[pl.BlockSpec((tm, 2 * FFN_H), lambda i: (i, 0)), _wrow_spec(w, l),
                  pl.BlockSpec((tm, D), lambda i: (i, 0))],
        out_specs=pl.BlockSpec((tm, D), lambda i: (i, 0)),
        out_shape=S((t, D), F32), name=name, compiler_params=_cp("parallel"))(ff, w, res)


def _dx_norm(dparts, w, l, x, g, dres, name, tm_pref):
    t = x.shape[0]
    nc = w.shape[3]
    tm = _row_tile(t, tm_pref)
    widths = [p.shape[1] for p in dparts]
    segs = _segments(widths, nc)
    n = len(dparts)

    def body(*refs):
        d_refs = refs[:n]
        w_ref, x_ref, g_ref, r_ref, dx_ref, dg_ref = refs[n:]
        dh = jnp.zeros((tm, D), F32)
        for (p, po, j, jo, wd) in segs:
            dh = dh + _dot_nt(d_refs[p][:, po:po + wd], w_ref[j, :, jo:jo + wd])
        xv = x_ref[...]
        dx, dgrow = _rms_bwd(dh, xv, _rms_r(xv), g_ref[...])
        dx_ref[...] = r_ref[...] + dx
        _acc_out(dg_ref, jnp.sum(dgrow, axis=0, keepdims=True), pl.program_id(0) == 0)

    return pl.pallas_call(
        body, grid=(t // tm,),
        in_specs=[pl.BlockSpec((tm, wd), lambda i: (i, 0)) for wd in widths]
        + [_wcol_spec(w, l), pl.BlockSpec((tm, D), lambda i: (i, 0)),
           pl.BlockSpec((1, D), lambda i: (0, 0)), pl.BlockSpec((tm, D), lambda i: (i, 0))],
        out_specs=[pl.BlockSpec((tm, D), lambda i: (i, 0)), pl.BlockSpec((1, D), lambda i: (0, 0))],
        out_shape=[S((t, D), F32), S((1, D), F32)], name=name,
        compiler_params=_cp("arbitrary"))(*dparts, w, x, g, dres)


def _dx_parts(dy, w, l, widths, name):
    t = dy.shape[0]
    tm = _row_tile(t, 512)
    offs = np.cumsum([0] + list(widths))
    n = len(widths)

    def body(dy_ref, w_ref, *o_refs):
        dyv = _mx(dy_ref[...])
        for p in range(n):
            o_refs[p][...] = _dot_nt(dyv, w_ref[int(offs[p]):int(offs[p + 1]), :])

    return pl.pallas_call(
        body, grid=(t // tm,),
        in_specs=[pl.BlockSpec((tm, D), lambda i: (i, 0)), _wrow_spec(w, l)],
        out_specs=[pl.BlockSpec((tm, wd), lambda i: (i, 0)) for wd in widths],
        out_shape=[S((t, wd), F32) for wd in widths], name=name, compiler_params=_cp("parallel"))(dy, w)


def _dx_swiglu(dy, w, l, ff, name):
    t = dy.shape[0]
    tm = _row_tile(t, 256)

    def body(dy_ref, w_ref, f_ref, o_ref):
        dact = _dot_nt(dy_ref[...], w_ref[...])
        gate = f_ref[:, :FFN_H]
        up = f_ref[:, FFN_H:]
        s = _sigmoid(gate)
        o_ref[:, :FFN_H] = dact * up * (s * (1.0 + gate * (1.0 - s)))
        o_ref[:, FFN_H:] = dact * (gate * s)

    return pl.pallas_call(
        body, grid=(t // tm,),
        in_specs=[pl.BlockSpec((tm, D), lambda i: (i, 0)), _wrow_spec(w, l),
                  pl.BlockSpec((tm, 2 * FFN_H), lambda i: (i, 0))],
        out_specs=pl.BlockSpec((tm, 2 * FFN_H), lambda i: (i, 0)),
        out_shape=S((t, 2 * FFN_H), F32), name=name, compiler_params=_cp("parallel"))(dy, w, ff)


def _call_into(body, into, in_specs, args, **kw):
    if into is None:
        return pl.pallas_call(body, in_specs=in_specs, **kw)(*args)
    n_in = len(args)

    def wrapped(*refs):
        return body(*refs[:n_in], *refs[n_in + 1:])

    return pl.pallas_call(wrapped, in_specs=list(in_specs) + [pl.BlockSpec(memory_space=pl.ANY)],
                          input_output_aliases={n_in: 0}, **kw)(*args, into)


def _dw_norm_parts(x, g, dparts, nc, l, into, name):
    t = x.shape[0]
    tk = _row_tile(t, 512)
    widths = [p.shape[1] for p in dparts]
    segs = _segments(widths, nc)
    n = len(dparts)
    nk = t // tk

    def body(*refs):
        x_ref, g_ref = refs[0], refs[1]
        d_refs = refs[2:2 + n]
        o_ref, acc_ref = refs[2 + n], refs[3 + n]
        k = pl.program_id(0)
        xv = x_ref[...]
        h = _mx(xv * _rms_r(xv) * g_ref[...])

        @pl.when(k == 0)
        def _():
            acc_ref[...] = jnp.zeros_like(acc_ref)

        for (p, po, j, jo, wd) in segs:
            acc_ref[j, :, jo:jo + wd] += _dot_tn(h, d_refs[p][:, po:po + wd])

        @pl.when(k == nk - 1)
        def _():
            o_ref[...] = acc_ref[...].astype(o_ref.dtype)

    return _call_into(
        body, into,
        [pl.BlockSpec((tk, D), lambda k: (k, 0)), pl.BlockSpec((1, D), lambda k: (0, 0))]
        + [pl.BlockSpec((tk, wd), lambda k: (k, 0)) for wd in widths], [x, g, *dparts],
        grid=(nk,), out_specs=pl.BlockSpec((N_CHIPS, None, D, nc), lambda k: (0, l, 0, 0)),
        out_shape=S((N_CHIPS, DEPTH, D, nc), MXU_DTYPE), name=name,
        scratch_shapes=[pltpu.VMEM((N_CHIPS, D, nc), F32)], compiler_params=_cp("arbitrary"))


def _dw_norm_cols(x, g, dy, nc, l, into, name):
    t = x.shape[0]
    tk = _row_tile(t, 512)
    nk = t // tk

    def body(x_ref, g_ref, dy_ref, o_ref, acc_ref):
        k = pl.program_id(1)
        xv = x_ref[...]
        h = _mx(xv * _rms_r(xv) * g_ref[...])

        @pl.when(k == 0)
        def _():
            acc_ref[...] = jnp.zeros_like(acc_ref)

        acc_ref[...] += _dot_tn(h, dy_ref[...])

        @pl.when(k == nk - 1)
        def _():
            o_ref[...] = acc_ref[...].astype(o_ref.dtype)

    return _call_into(
        body, into,
        [pl.BlockSpec((tk, D), lambda j, k: (k, 0)), pl.BlockSpec((1, D), lambda j, k: (0, 0)),
         pl.BlockSpec((tk, nc), lambda j, k: (k, j))], [x, g, dy],
        grid=(N_CHIPS, nk), out_specs=pl.BlockSpec((None, None, D, nc), lambda j, k: (j, l, 0, 0)),
        out_shape=S((N_CHIPS, DEPTH, D, nc), MXU_DTYPE), name=name,
        scratch_shapes=[pltpu.VMEM((D, nc), F32)], compiler_params=_cp("parallel", "arbitrary"))


def _dw_parts(parts, dy, l, into, name):
    t = dy.shape[0]
    tk = _row_tile(t, 512)
    widths = [p.shape[1] for p in parts]
    offs = np.cumsum([0] + widths)
    ktot = int(offs[-1])
    n = len(parts)
    nk = t // tk

    def body(*refs):
        p_refs, dy_ref, o_ref, acc_ref = refs[:n], refs[n], refs[n + 1], refs[n + 2]
        k = pl.program_id(0)

        @pl.when(k == 0)
        def _():
            acc_ref[...] = jnp.zeros_like(acc_ref)

        dyv = _mx(dy_ref[...])
        for p in range(n):
            acc_ref[int(offs[p]):int(offs[p + 1]), :] += _dot_tn(p_refs[p][...], dyv)

        @pl.when(k == nk - 1)
        def _():
            o_ref[...] = acc_ref[...].astype(o_ref.dtype)

    return _call_into(
        body, into,
        [pl.BlockSpec((tk, wd), lambda k: (k, 0)) for wd in widths] + [pl.BlockSpec((tk, D), lambda k: (k, 0))],
        [*parts, dy],
        grid=(nk,), out_specs=pl.BlockSpec((None, ktot, D), lambda k: (l, 0, 0)),
        out_shape=S((DEPTH, ktot, D), MXU_DTYPE), name=name,
        scratch_shapes=[pltpu.VMEM((ktot, D), F32)], compiler_params=_cp("arbitrary"))


def _dw_swiglu(ff, dy, l, into, name):
    t = dy.shape[0]
    tk = _row_tile(t, 256)
    nk = t // tk

    def body(f_ref, dy_ref, o_ref, acc_ref):
        k = pl.program_id(0)

        @pl.when(k == 0)
        def _():
            acc_ref[...] = jnp.zeros_like(acc_ref)

        acc_ref[...] += _dot_tn(_swiglu(f_ref[...]), dy_ref[...])

        @pl.when(k == nk - 1)
        def _():
            o_ref[...] = acc_ref[...].astype(o_ref.dtype)

    return _call_into(
        body, into,
        [pl.BlockSpec((tk, 2 * FFN_H), lambda k: (k, 0)), pl.BlockSpec((tk, D), lambda k: (k, 0))], [ff, dy],
        grid=(nk,), out_specs=pl.BlockSpec((None, FFN_H, D), lambda k: (l, 0, 0)),
        out_shape=S((DEPTH, FFN_H, D), MXU_DTYPE), name=name,
        scratch_shapes=[pltpu.VMEM((FFN_H, D), F32)], compiler_params=_cp("arbitrary"))


def _tables(t):
    pos = jnp.arange(t, dtype=F32)
    half = HD // 2
    inv_freq = ROPE_BASE ** (-jnp.arange(half, dtype=F32) / half)
    ang = pos[:, None] * inv_freq[None, :]
    cos, sin = jnp.cos(ang), jnp.sin(ang)
    tb = {"cos2": jnp.concatenate([cos, cos], axis=1), "sin2": jnp.concatenate([-sin, sin], axis=1)}
    gf = 1.0 - jnp.exp2(-5.0 - jnp.arange(HEADS, dtype=F32))
    lgf = jnp.log(gf)[:, None]
    lgb = jnp.log(gf[::-1])[:, None]
    idx = jnp.arange(CH, dtype=F32)
    diff = idx[:, None] - idx[None, :]
    dfwd = jnp.where(diff >= 0, jnp.exp(lgf[:, :, None] * jnp.where(diff >= 0, diff, 0.0)), 0.0)
    dbwd = jnp.where(diff < 0, jnp.exp(lgb[:, :, None] * jnp.where(diff < 0, -diff, 0.0)), 0.0)
    tb["dm"] = dfwd + dbwd
    tb["dmt"] = jnp.swapaxes(tb["dm"], 1, 2)

    def lanes(a):
        return jnp.repeat(a.T, HD, axis=1)

    tb["xif"] = lanes(jnp.exp(lgf * (idx + 1)))
    tb["zf"] = lanes(jnp.exp(lgf * (CH - 1 - idx)))
    tb["xib"] = lanes(jnp.exp(lgb * (CH - idx)))
    tb["zb"] = lanes(jnp.exp(lgb * idx))
    tb["gcf"] = jnp.repeat(jnp.exp(lgf * CH), HD, axis=0).reshape(1, HEADS * HD)
    tb["gcb"] = jnp.repeat(jnp.exp(lgb * CH), HD, axis=0).reshape(1, HEADS * HD)
    return tb


def _full(shape):
    nd = len(shape)
    return pl.BlockSpec(shape, lambda *_: (0,) * nd)


def _gm_mixed(vn, ws_ref, bias):
    lane = lax.broadcasted_iota(jnp.int32, (CH, 128), 1)
    halves = []
    for hf in range(2):
        vh = _mx(vn[:, hf * 128:(hf + 1) * 128])
        r0 = jnp.dot(_mx(ws_ref[2 * hf]), vh, preferred_element_type=F32)
        r1 = jnp.dot(_mx(ws_ref[2 * hf + 1]), vh, preferred_element_type=F32)
        halves.append(jnp.where(lane < 64, r0, r1))
    return jnp.concatenate(halves, axis=1) + bias


def _gm_fwd(proj, ln_g, ln_b, ws, bias, name):
    t = proj.shape[0]
    tm = _row_tile(t, 512)

    def body(pu_ref, pv_ref, g_ref, b_ref, ws_ref, bias_ref, o_ref):
        for c in range(tm // CH):
            rows = slice(c * CH, (c + 1) * CH)
            u = _gelu(pu_ref[rows, :])
            o, _ = _standardize(_gelu(pv_ref[rows, :]))
            vn = o * g_ref[...] + b_ref[...]
            o_ref[rows, :] = u * _gm_mixed(vn, ws_ref, bias_ref[...])

    return pl.pallas_call(
        body, grid=(t // tm,),
        in_specs=[pl.BlockSpec((tm, GM_W), lambda i: (i, 0)), pl.BlockSpec((tm, GM_W), lambda i: (i, 1)),
                  _full((1, GM_W)), _full((1, GM_W)), _full((GM_HEADS, CH, CH)), _full((CH, GM_W))],
        out_specs=pl.BlockSpec((tm, GM_W), lambda i: (i, 0)),
        out_shape=S((t, GM_W), F32), name=name, compiler_params=_cp("parallel"))(proj, proj, ln_g, ln_b, ws, bias)


def _gm_bwd(proj, dy, ln_g, ln_b, ws, wst, bias, name):
    t = proj.shape[0]
    tm = _row_tile(t, 512)
    nb = t // tm

    def body(pu_ref, pv_ref, dy_ref, g_ref, b_ref, ws_ref, wst_ref, bias_ref,
             d_ref, dws_ref, dbs_ref, dg_ref, db_ref, dbias_ref):
        first = pl.program_id(0) == 0
        lane = lax.broadcasted_iota(jnp.int32, (CH, 128), 1)
        dws = [jnp.zeros((CH, CH), F32) for _ in range(GM_HEADS)]
        dbias = jnp.zeros((CH, GM_W), F32)
        dg = jnp.zeros((1, GM_W), F32)
        db = jnp.zeros((1, GM_W), F32)
        for c in range(tm // CH):
            rows = slice(c * CH, (c + 1) * CH)
            pu = pu_ref[rows, :]
            pv = pv_ref[rows, :]
            u = _gelu(pu)
            o, r = _standardize(_gelu(pv))
            vn = o * g_ref[...] + b_ref[...]
            mixed = _gm_mixed(vn, ws_ref, bias_ref[...])
            dyv = dy_ref[rows, :]
            d_ref[rows, :GM_W] = dyv * mixed * _gelu_grad(pu)
            dmixed = dyv * u
            dbias = dbias + dmixed
            dvn_halves = []
            for hf in range(2):
                dm = dmixed[:, hf * 128:(hf + 1) * 128]
                vh = vn[:, hf * 128:(hf + 1) * 128]
                dm0 = jnp.where(lane < 64, dm, 0.0)
                dm1 = dm - dm0
                dws[2 * hf] = dws[2 * hf] + _dot_nt(dm0, vh)
                dws[2 * hf + 1] = dws[2 * hf + 1] + _dot_nt(dm1, vh)
                t0 = _dot(wst_ref[2 * hf], dm)
                t1 = _dot(wst_ref[2 * hf + 1], dm)
                dvn_halves.append(jnp.where(lane < 64, t0, t1))
            dvn = jnp.concatenate(dvn_halves, axis=1)
            dg = dg + jnp.sum(dvn * o, axis=0, keepdims=True)
            db = db + jnp.sum(dvn, axis=0, keepdims=True)
            dv = _standardize_bwd(dvn * g_ref[...], o, r)
            d_ref[rows, GM_W:] = dv * _gelu_grad(pv)
        for h in range(GM_HEADS):
            _acc_out(dws_ref.at[h], dws[h], first)
        _acc_out(dbias_ref, dbias, first)
        _acc_out(dg_ref, dg, first)
        _acc_out(db_ref, db, first)

        @pl.when(pl.program_id(0) == nb - 1)
        def _():
            tot = dbias_ref[...]
            head = lax.broadcasted_iota(jnp.int32, (CH, GM_W), 1) // (GM_W // GM_HEADS)
            out = jnp.zeros((CH, 128), F32)
            for h in range(GM_HEADS):
                s = jnp.sum(jnp.where(head == h, tot, 0.0), axis=1, keepdims=True)
                out = jnp.where(lane == h, s, out)
            dbs_ref[...] = out

    return pl.pallas_call(
        body, grid=(nb,),
        in_specs=[pl.BlockSpec((tm, GM_W), lambda i: (i, 0)), pl.BlockSpec((tm, GM_W), lambda i: (i, 1)),
                  pl.BlockSpec((tm, GM_W), lambda i: (i, 0)),
                  _full((1, GM_W)), _full((1, GM_W)), _full((GM_HEADS, CH, CH)), _full((GM_HEADS, CH, CH)),
                  _full((CH, GM_W))],
        out_specs=[pl.BlockSpec((tm, 2 * GM_W), lambda i: (i, 0)), _full((GM_HEADS, CH, CH)), _full((CH, 128)),
                   _full((1, GM_W)), _full((1, GM_W))],
        out_shape=[S((t, 2 * GM_W), F32), S((GM_HEADS, CH, CH), F32), S((CH, 128), F32),
                   S((1, GM_W), F32), S((1, GM_W), F32)],
        scratch_shapes=[pltpu.VMEM((CH, GM_W), F32)],
        name=name, compiler_params=_cp("arbitrary"))(proj, proj, dy, ln_g, ln_b, ws, wst, bias)


def _rot(x, cos2, sin2):
    return x * cos2 + pltpu.roll(x, HD // 2, 1) * sin2


def _rot_bwd(dx, cos2, sin2):
    return dx * cos2 + pltpu.roll(dx * sin2, HD // 2, 1)


def _rotary(proj, cos2, sin2, name):
    t = proj.shape[0]
    tm = _row_tile(t, 512)
    scale = HD ** -0.5

    def body(q_ref, k_ref, c_ref, s_ref, rq_ref, rk_ref):
        c, s = c_ref[...], s_ref[...]
        for h in range(HEADS):
            cols = slice(h * HD, (h + 1) * HD)
            rq_ref[:, cols] = _rot(q_ref[:, cols], c, s)
            rk_ref[:, cols] = _rot(k_ref[:, cols], c, s) * scale

    return pl.pallas_call(
        body, grid=(t // tm,),
        in_specs=[pl.BlockSpec((tm, RET_W), lambda i: (i, 1)), pl.BlockSpec((tm, RET_W), lambda i: (i, 2)),
                  pl.BlockSpec((tm, HD), lambda i: (i, 0)), pl.BlockSpec((tm, HD), lambda i: (i, 0))],
        out_specs=[pl.BlockSpec((tm, RET_W), lambda i: (i, 0))] * 2,
        out_shape=[S((t, RET_W), F32)] * 2, name=name, compiler_params=_cp("parallel"))(proj, proj, cos2, sin2)


def _ret_scan(lhs, rhs, rhs_col, lp, ls, gp, gs, name):
    t = lhs.shape[0]
    n = t // CH
    r = 4 if n % 4 == 0 else 1
    ns = n // r

    def body(lp_ref, ls_ref, gp_ref, gs_ref, l1_ref, r1_ref, l2_ref, r2_ref, pre_ref, suf_ref, sp_ref, ss_ref):
        @pl.when(pl.program_id(0) == 0)
        def _():
            sp_ref[...] = jnp.zeros_like(sp_ref)
            ss_ref[...] = jnp.zeros_like(ss_ref)

        def kv(l_ref, r_ref, scale, rows):
            lv = l_ref[rows, :] * scale
            rv = r_ref[rows, :]
            return jnp.concatenate([_dot_tn(lv[:, h * HD:(h + 1) * HD], rv[:, h * HD:(h + 1) * HD])
                                    for h in range(HEADS)], axis=1)

        for j in range(r):
            pre_ref[j] = sp_ref[...]
            sp_ref[...] = sp_ref[...] * gp_ref[...] + kv(l1_ref, r1_ref, lp_ref[...], slice(j * CH, (j + 1) * CH))
        for j in reversed(range(r)):
            suf_ref[j] = ss_ref[...]
            ss_ref[...] = ss_ref[...] * gs_ref[...] + kv(l2_ref, r2_ref, ls_ref[...], slice(j * CH, (j + 1) * CH))

    w = HEADS * HD
    return pl.pallas_call(
        body, grid=(ns,),
        in_specs=[_full((CH, w)), _full((CH, w)), _full((1, w)), _full((1, w)),
                  pl.BlockSpec((r * CH, w), lambda s: (s, 0)), pl.BlockSpec((r * CH, w), lambda s: (s, rhs_col)),
                  pl.BlockSpec((r * CH, w), lambda s: (ns - 1 - s, 0)),
                  pl.BlockSpec((r * CH, w), lambda s: (ns - 1 - s, rhs_col))],
        out_specs=[pl.BlockSpec((r, HD, w), lambda s: (s, 0, 0)), pl.BlockSpec((r, HD, w), lambda s: (ns - 1 - s, 0, 0))],
        out_shape=[S((n, HD, w), F32)] * 2, name=name,
        scratch_shapes=[pltpu.VMEM((HD, w), F32), pltpu.VMEM((HD, w), F32)],
        compiler_params=_cp("arbitrary"))(lp, ls, gp, gs, lhs, rhs, lhs, rhs)


def _ret_out(rq, rk, proj, sf, sb, tb, name):
    t = rq.shape[0]
    r = 2 if (t // CH) % 2 == 0 else 1
    tm = r * CH
    w = HEADS * HD

    def body(rq_ref, rk_ref, v_ref, g_ref, sf_ref, sb_ref, dm_ref, xif_ref, xib_ref, a_ref, y_ref):
        for c in range(r):
            rows = slice(c * CH, (c + 1) * CH)
            for h in range(HEADS):
                cols = slice(h * HD, (h + 1) * HD)
                q = rq_ref[rows, cols]
                p = _dot_nt(q, rk_ref[rows, cols]) * dm_ref[h]
                a = (_dot(p, v_ref[rows, cols]) + _dot(q * xif_ref[:, cols], sf_ref[c, :, cols])
                     + _dot(q * xib_ref[:, cols], sb_ref[c, :, cols]))
                a_ref[rows, cols] = a
                o, _ = _standardize(a)
                gv = g_ref[rows, cols]
                y_ref[rows, cols] = o * (gv * _sigmoid(gv))

    return pl.pallas_call(
        body, grid=(t // tm,),
        in_specs=[pl.BlockSpec((tm, w), lambda i: (i, 0)), pl.BlockSpec((tm, w), lambda i: (i, 0)),
                  pl.BlockSpec((tm, w), lambda i: (i, 3)), pl.BlockSpec((tm, w), lambda i: (i, 4)),
                  pl.BlockSpec((r, HD, w), lambda i: (i, 0, 0)), pl.BlockSpec((r, HD, w), lambda i: (i, 0, 0)),
                  _full((HEADS, CH, CH)), _full((CH, w)), _full((CH, w))],
        out_specs=[pl.BlockSpec((tm, w), lambda i: (i, 0))] * 2,
        out_shape=[S((t, w), F32)] * 2, name=name,
        compiler_params=_cp("parallel"))(rq, rk, proj, proj, sf, sb, tb["dm"], tb["xif"], tb["xib"])


def _ret_bwd_pre(dy, a, proj, name):
    t = dy.shape[0]
    tm = _row_tile(t, 512)
    w = HEADS * HD

    def body(dy_ref, a_ref, g_ref, da_ref, dg_ref):
        for h in range(HEADS):
            cols = slice(h * HD, (h + 1) * HD)
            o, r = _standardize(a_ref[:, cols])
            gv = g_ref[:, cols]
            s = _sigmoid(gv)
            dyv = dy_ref[:, cols]
            dg_ref[:, cols] = dyv * o * (s * (1.0 + gv * (1.0 - s)))
            da_ref[:, cols] = _standardize_bwd(dyv * (gv * s), o, r)

    return pl.pallas_call(
        body, grid=(t // tm,),
        in_specs=[pl.BlockSpec((tm, w), lambda i: (i, 0)), pl.BlockSpec((tm, w), lambda i: (i, 0)),
                  pl.BlockSpec((tm, w), lambda i: (i, 4))],
        out_specs=[pl.BlockSpec((tm, w), lambda i: (i, 0))] * 2,
        out_shape=[S((t, w), F32)] * 2, name=name, compiler_params=_cp("parallel"))(dy, a, proj)


def _ret_bwd_main(rq, rk, proj, da, sf, sb, gf, gb, tb, name):
    t = rq.shape[0]
    r = 2 if (t // CH) % 2 == 0 else 1
    tm = r * CH
    w = HEADS * HD
    scale = HD ** -0.5

    def body(rq_ref, rk_ref, v_ref, da_ref, sf_ref, sb_ref, gf_ref, gb_ref, dm_ref, dmt_ref,
             xif_ref, xib_ref, zf_ref, zb_ref, c_ref, s_ref, o_ref):
        for c in range(r):
            rows = slice(c * CH, (c + 1) * CH)
            cos2, sin2 = c_ref[rows, :], s_ref[rows, :]
            for h in range(HEADS):
                cols = slice(h * HD, (h + 1) * HD)
                q, k, v, dav = rq_ref[rows, cols], rk_ref[rows, cols], v_ref[rows, cols], da_ref[rows, cols]
                qm, km, vm, dam = _mx(q), _mx(k), _mx(v), _mx(dav)
                dm, dmt = dm_ref[h], dmt_ref[h]
                pt = _dot_nt(km, qm) * dmt
                dp = _dot_nt(dam, vm) * dm
                dpt = _dot_nt(vm, dam) * dmt
                sfh, sbh, gfh, gbh = sf_ref[c, :, cols], sb_ref[c, :, cols], gf_ref[c, :, cols], gb_ref[c, :, cols]
                zf, zb = zf_ref[:, cols], zb_ref[:, cols]
                dv = _dot(pt, dam) + zf * _dot(km, gfh) + zb * _dot(km, gbh)
                drq = _dot(dp, km) + xif_ref[:, cols] * _dot_nt(dam, sfh) + xib_ref[:, cols] * _dot_nt(dam, sbh)
                drk = _dot(dpt, qm) + _dot_nt(zf * v, gfh) + _dot_nt(zb * v, gbh)
                o_ref[rows, h * HD:(h + 1) * HD] = _rot_bwd(drq, cos2, sin2)
                o_ref[rows, w + h * HD:w + (h + 1) * HD] = _rot_bwd(drk, cos2, sin2) * scale
                o_ref[rows, 2 * w + h * HD:2 * w + (h + 1) * HD] = dv

    st = pl.BlockSpec((r, HD, w), lambda i: (i, 0, 0))
    return pl.pallas_call(
        body, grid=(t // tm,),
        in_specs=[pl.BlockSpec((tm, w), lambda i: (i, 0)), pl.BlockSpec((tm, w), lambda i: (i, 0)),
                  pl.BlockSpec((tm, w), lambda i: (i, 3)), pl.BlockSpec((tm, w), lambda i: (i, 0)),
                  st, st, st, st, _full((HEADS, CH, CH)), _full((HEADS, CH, CH)),
                  _full((CH, w)), _full((CH, w)), _full((CH, w)), _full((CH, w)),
                  pl.BlockSpec((tm, HD), lambda i: (i, 0)), pl.BlockSpec((tm, HD), lambda i: (i, 0))],
        out_specs=pl.BlockSpec((tm, 3 * w), lambda i: (i, 0)),
        out_shape=S((t, 3 * w), F32), name=name,
        compiler_params=_cp("parallel"))(rq, rk, proj, da, sf, sb, gf, gb, tb["dm"], tb["dmt"],
                                         tb["xif"], tb["xib"], tb["zf"], tb["zb"], tb["cos2"], tb["sin2"])


CONV_TM = 256
CONV_SUB = 64
A_COL = (2 * GM_W + 4 * RET_W) // CV_W
G_COL = A_COL + 1


def _halo_specs(t, tm, col):
    nb16 = t // HALO
    per = tm // HALO
    return [pl.BlockSpec((tm, CV_W), lambda i: (i, col)),
            pl.BlockSpec((HALO, CV_W), lambda i: (jnp.maximum(i * per - 1, 0), col)),
            pl.BlockSpec((HALO, CV_W), lambda i: (jnp.minimum((i + 1) * per, nb16 - 1), col))]


def _fill_padded(dst_ref, prev, main, nxt, tm, i, nb):
    dst_ref[0:HALO, :] = jnp.where(i > 0, prev, 0.0)
    dst_ref[HALO:HALO + tm, :] = main
    dst_ref[HALO + tm:2 * HALO + tm, :] = jnp.where(i < nb - 1, nxt, 0.0)


def _conv_fwd(proj, cw, cb, ln_g, ln_b, name):
    t = proj.shape[0]
    tm = _row_tile(t, CONV_TM)
    nb = t // tm

    def body(a_ref, ap_ref, an_ref, g_ref, gp_ref, gn_ref, w_ref, b_ref, lg_ref, lb_ref, y_ref, hc_ref, hp_ref):
        i = pl.program_id(0)
        _fill_padded(hp_ref, ap_ref[...] * _sigmoid(gp_ref[...]), a_ref[...] * _sigmoid(g_ref[...]),
                     an_ref[...] * _sigmoid(gn_ref[...]), tm, i, nb)
        for sb in range(tm // CONV_SUB):
            acc = jnp.zeros((CONV_SUB, CV_W), F32) + b_ref[...]
            for k in range(KCONV):
                acc = acc + w_ref[k:k + 1, :] * hp_ref[pl.ds(sb * CONV_SUB + k + 1, CONV_SUB), :]
            rows = slice(sb * CONV_SUB, (sb + 1) * CONV_SUB)
            hc_ref[rows, :] = acc
            o, _ = _standardize(acc)
            z = o * lg_ref[...] + lb_ref[...]
            y_ref[rows, :] = z * _sigmoid(z)

    return pl.pallas_call(
        body, grid=(nb,),
        in_specs=_halo_specs(t, tm, A_COL) + _halo_specs(t, tm, G_COL)
        + [_full((32, CV_W)), _full((1, CV_W)), _full((1, CV_W)), _full((1, CV_W))],
        out_specs=[pl.BlockSpec((tm, CV_W), lambda i: (i, 0))] * 2,
        out_shape=[S((t, CV_W), F32)] * 2, name=name,
        scratch_shapes=[pltpu.VMEM((tm + 2 * HALO, CV_W), F32)],
        compiler_params=_cp("parallel"))(proj, proj, proj, proj, proj, proj, cw, cb, ln_g, ln_b)


def _conv_bwd(proj, dy, hc, cw, ln_g, ln_b, name):
    t = proj.shape[0]
    tm = _row_tile(t, CONV_TM)
    nb = t // tm

    def body(a_ref, ap_ref, an_ref, g_ref, gp_ref, gn_ref, dy_ref, dyp_ref, dyn_ref, hc_ref, hcp_ref, hcn_ref,
             w_ref, lg_ref, lb_ref, d_ref, dw_ref, dcb_ref, dlg_ref, dlb_ref, hp_ref, dhp_ref, dwacc_ref):
        i = pl.program_id(0)
        first = i == 0

        def dhc_of(dyv, hcv):
            o, r = _standardize(hcv)
            z = o * lg_ref[...] + lb_ref[...]
            s = _sigmoid(z)
            dz = dyv * (s * (1.0 + z * (1.0 - s)))
            return _standardize_bwd(dz * lg_ref[...], o, r), dz, o

        dhc, dz, o = dhc_of(dy_ref[...], hc_ref[...])
        _acc_out(dlg_ref, jnp.sum(dz * o, axis=0, keepdims=True), first)
        _acc_out(dlb_ref, jnp.sum(dz, axis=0, keepdims=True), first)
        _acc_out(dcb_ref, jnp.sum(dhc, axis=0, keepdims=True), first)
        _fill_padded(dhp_ref, dhc_of(dyp_ref[...], hcp_ref[...])[0], dhc, dhc_of(dyn_ref[...], hcn_ref[...])[0],
                     tm, i, nb)
        _fill_padded(hp_ref, ap_ref[...] * _sigmoid(gp_ref[...]), a_ref[...] * _sigmoid(g_ref[...]),
                     an_ref[...] * _sigmoid(gn_ref[...]), tm, i, nb)

        @pl.when(first)
        def _():
            dwacc_ref[...] = jnp.zeros_like(dwacc_ref)

        for sb in range(tm // CONV_SUB):
            base = sb * CONV_SUB
            dmain = dhp_ref[pl.ds(HALO + base, CONV_SUB), :]
            dh = jnp.zeros((CONV_SUB, CV_W), F32)
            for k in range(KCONV):
                dh = dh + w_ref[k:k + 1, :] * dhp_ref[pl.ds(base + 2 * HALO - 1 - k, CONV_SUB), :]
                prod = dmain * hp_ref[pl.ds(base + k + 1, CONV_SUB), :]
                dwacc_ref[k * 8:(k + 1) * 8, :] += jnp.sum(prod.reshape(CONV_SUB // 8, 8, CV_W), axis=0)
            rows = slice(base, base + CONV_SUB)
            s = _sigmoid(g_ref[rows, :])
            d_ref[rows, :CV_W] = dh * s
            d_ref[rows, CV_W:] = dh * a_ref[rows, :] * (s * (1.0 - s))

        @pl.when(i == nb - 1)
        def _():
            for k in range(KCONV):
                dw_ref[k:k + 1, :] = jnp.sum(dwacc_ref[k * 8:(k + 1) * 8, :], axis=0, keepdims=True)
            dw_ref[KCONV:32, :] = jnp.zeros((32 - KCONV, CV_W), F32)

    hs = [pl.BlockSpec((tm, CV_W), lambda i: (i, 0)),
          pl.BlockSpec((HALO, CV_W), lambda i: (jnp.maximum(i * (tm // HALO) - 1, 0), 0)),
          pl.BlockSpec((HALO, CV_W), lambda i: (jnp.minimum((i + 1) * (tm // HALO), t // HALO - 1), 0))]
    return pl.pallas_call(
        body, grid=(nb,),
        in_specs=_halo_specs(t, tm, A_COL) + _halo_specs(t, tm, G_COL) + hs + hs
        + [_full((32, CV_W)), _full((1, CV_W)), _full((1, CV_W))],
        out_specs=[pl.BlockSpec((tm, 2 * CV_W), lambda i: (i, 0)), _full((32, CV_W)), _full((1, CV_W)),
                   _full((1, CV_W)), _full((1, CV_W))],
        out_shape=[S((t, 2 * CV_W), F32), S((32, CV_W), F32), S((1, CV_W), F32), S((1, CV_W), F32), S((1, CV_W), F32)],
        name=name,
        scratch_shapes=[pltpu.VMEM((tm + 2 * HALO, CV_W), F32), pltpu.VMEM((tm + 2 * HALO, CV_W), F32),
                        pltpu.VMEM((32 * 8, CV_W), F32)],
        compiler_params=_cp("arbitrary"))(proj, proj, proj, proj, proj, proj, dy, dy, dy, hc, hc, hc, cw, ln_g, ln_b)


def _loss_head(x, g, target, name):
    t = x.shape[0]
    tm = _row_tile(t, 512)

    def body(x_ref, g_ref, t_ref, dx_ref, dg_ref, l_ref):
        first = pl.program_id(0) == 0
        xv = x_ref[...]
        r = _rms_r(xv)
        e = xv * r * g_ref[...] - t_ref[...]
        dx, dgrow = _rms_bwd(e * (1.0 / D), xv, r, g_ref[...])
        dx_ref[...] = dx
        _acc_out(dg_ref, jnp.sum(dgrow, axis=0, keepdims=True), first)
        part = 0.5 * jnp.sum(jnp.mean(e * e, axis=-1, keepdims=True), axis=0, keepdims=True)
        _acc_out(l_ref, jnp.broadcast_to(part, (8, 128)), first)

    return pl.pallas_call(
        body, grid=(t // tm,),
        in_specs=[pl.BlockSpec((tm, D), lambda i: (i, 0)), _full((1, D)), pl.BlockSpec((tm, D), lambda i: (i, 0))],
        out_specs=[pl.BlockSpec((tm, D), lambda i: (i, 0)), _full((1, D)), _full((8, 128))],
        out_shape=[S((t, D), F32), S((1, D), F32), S((8, 128), F32)], name=name,
        compiler_params=_cp("arbitrary"))(x, g, target)


def _as2d(a):
    return a.reshape(-1, a.shape[-1])


def _ew_tile(rows, cols, n_arrays):
    budget = VMEM_LIMIT // 2
    tr = rows
    while tr * cols * 4 * n_arrays * 2 > budget and tr % 16 == 0:
        tr //= 2
    assert rows % tr == 0
    return tr


def _adamw(w, g, m, v, name):
    shape = w.shape
    w2, g2, m2, v2 = _as2d(w), _as2d(g), _as2d(m), _as2d(v)
    rows, cols = w2.shape
    tr = _ew_tile(rows, cols, 7)

    def body(w_ref, g_ref, m_ref, v_ref, d_ref, nm_ref, nv_ref):
        gv = g_ref[...]
        nm = ADAM_B1 * m_ref[...] + (1.0 - ADAM_B1) * gv
        nv = ADAM_B2 * v_ref[...] + (1.0 - ADAM_B2) * (gv * gv)
        m_hat = nm / (1.0 - ADAM_B1 ** ADAM_STEP)
        v_hat = nv / (1.0 - ADAM_B2 ** ADAM_STEP)
        d_ref[...] = -ADAM_LR * (m_hat / (jnp.sqrt(v_hat) + ADAM_EPS) + ADAM_WD * w_ref[...])
        nm_ref[...] = nm
        nv_ref[...] = nv

    spec = pl.BlockSpec((tr, cols), lambda i: (i, 0))
    outs = pl.pallas_call(body, grid=(rows // tr,), in_specs=[spec] * 4, out_specs=[spec] * 3,
                          out_shape=[S((rows, cols), F32)] * 3, name=name,
                          compiler_params=_cp("parallel"))(w2, g2, m2, v2)
    return tuple(o.reshape(shape) for o in outs)


BIG = (("w_in", "col"), ("w_out", "row"), ("w_ffn_in", "col"), ("w_ffn_out", "row"))
NBIG = len(BIG)
ANY = pl.BlockSpec(memory_space=pl.ANY)


def _full_view(kind, l, r, c):
    return (N_CHIPS, l, 2, r // 2, c) if kind == "col" else (l, N_CHIPS, 2, r // 2, c)


def _region(kind, ref5, j, h):
    return ref5.at[j, :, h] if kind == "col" else ref5.at[:, j, h]


def _mesh_pos():
    x, y, c = lax.axis_index("x"), lax.axis_index("y"), lax.axis_index("c")
    chips = [(1 - x, y), (x, 1 - y), (1 - x, 1 - y)]
    return x, y, c, 2 * x + y, chips, [2 * cx + cy for cx, cy in chips]


def _cast_to_gathered(w, kind, me, name):
    l_, r_, c_ = w.shape
    tr = _ew_tile(r_, c_, 2)
    col = kind == "col"
    shape = (N_CHIPS, l_, r_, c_) if col else (l_, N_CHIPS, r_, c_)

    def body(me_ref, w_ref, o_ref):
        o_ref[...] = w_ref[...].astype(o_ref.dtype)

    gs = pltpu.PrefetchScalarGridSpec(
        num_scalar_prefetch=1, grid=(l_, r_ // tr),
        in_specs=[pl.BlockSpec((None, tr, c_), lambda l, i, s: (l, i, 0))],
        out_specs=pl.BlockSpec((None, None, tr, c_),
                               (lambda l, i, s: (s[0], l, i, 0)) if col else (lambda l, i, s: (l, s[0], i, 0))))
    out = pl.pallas_call(body, grid_spec=gs, out_shape=S(shape, MXU_DTYPE), name=name,
                         compiler_params=_cp("parallel", "parallel"))(me.reshape(1), w)
    return out.reshape(_full_view(kind, l_, r_, c_))


def _all_gather_big(fulls):
    kinds = [k for _, k in BIG]

    def body(*refs):
        i_refs, f_refs = refs[:NBIG], refs[NBIG:2 * NBIG]
        isend, irecv, dsend, drecv = refs[2 * NBIG:]
        x, y, c, me, chips, cj = _mesh_pos()
        sib = (x, y, 1 - c)
        sends = []
        for w in range(NBIG):
            for k in range(3):
                cp = pltpu.make_async_remote_copy(
                    src_ref=_region(kinds[w], i_refs[w], me, c), dst_ref=_region(kinds[w], f_refs[w], me, c),
                    send_sem=isend.at[w, k], recv_sem=irecv.at[w, k],
                    device_id=(*chips[k], c), device_id_type=MESH)
                cp.start()
                sends.append(cp)
        for k in range(3):
            for w in range(NBIG):
                reg = _region(kinds[w], f_refs[w], cj[k], c)
                pltpu.make_async_remote_copy(
                    src_ref=reg, dst_ref=reg, send_sem=isend.at[w, k], recv_sem=irecv.at[w, k],
                    device_id=(*chips[k], c), device_id_type=MESH).wait_recv()
                cp = pltpu.make_async_remote_copy(
                    src_ref=reg, dst_ref=reg, send_sem=dsend.at[w, k], recv_sem=drecv.at[w, k],
                    device_id=sib, device_id_type=MESH)
                cp.start()
                sends.append(cp)
        for k in range(3):
            for w in range(NBIG):
                reg = _region(kinds[w], f_refs[w], cj[k], 1 - c)
                pltpu.make_async_remote_copy(
                    src_ref=reg, dst_ref=reg, send_sem=dsend.at[w, k], recv_sem=drecv.at[w, k],
                    device_id=sib, device_id_type=MESH).wait_recv()
        for cp in sends:
            cp.wait_send()

    return pl.pallas_call(
        body, in_specs=[ANY] * NBIG, out_specs=[ANY] * NBIG, out_shape=[S(a.shape, a.dtype) for a in fulls],
        input_output_aliases={w: w for w in range(NBIG)}, name="all_gather_weights",
        scratch_shapes=[pltpu.SemaphoreType.DMA((NBIG, 3))] * 4)(*fulls)


def _pair_exchange(grads5):
    out_shapes = [S(a.shape[:2] + a.shape[3:], a.dtype) for a in grads5]

    def body(*refs):
        g_refs, theirs = refs[:NBIG], refs[NBIG:2 * NBIG]
        send, recv = refs[2 * NBIG:]
        x, y, c, *_ = _mesh_pos()
        cps = []
        for w in range(NBIG):
            cp = pltpu.make_async_remote_copy(
                src_ref=g_refs[w].at[:, :, 1 - c], dst_ref=theirs[w], send_sem=send.at[w], recv_sem=recv.at[w],
                device_id=(x, y, 1 - c), device_id_type=MESH)
            cp.start()
            cps.append(cp)
        for cp in cps:
            cp.wait()

    return pl.pallas_call(
        body, in_specs=[ANY] * NBIG, out_specs=[ANY] * NBIG, out_shape=out_shapes, name="grad_pair_exchange",
        scratch_shapes=[pltpu.SemaphoreType.DMA((NBIG,))] * 2)(*grads5)


def _pair_sum(g5, theirs, core, name):
    a_, b_, _, rh, c_ = g5.shape
    tr = _ew_tile(rh, c_, 2)

    def body(s_ref, g_ref, t_ref, o_ref):
        o_ref[...] = (g_ref[...].astype(F32) + t_ref[...].astype(F32)).astype(o_ref.dtype)

    blk = pl.BlockSpec((None, None, tr, c_), lambda a, b, i, s: (a, b, i, 0))
    gs = pltpu.PrefetchScalarGridSpec(
        num_scalar_prefetch=1, grid=(a_, b_, rh // tr),
        in_specs=[pl.BlockSpec((None, None, None, tr, c_), lambda a, b, i, s: (a, b, s[0], i, 0)), blk],
        out_specs=blk)
    return pl.pallas_call(body, grid_spec=gs, out_shape=S(theirs.shape, theirs.dtype), name=name,
                          compiler_params=_cp("parallel", "parallel", "parallel"))(core.reshape(1), g5, theirs)


def _chip_scatter(qs):
    kinds = [k for _, k in BIG]

    def piece(kind, ref, j):
        return ref.at[j] if kind == "col" else ref.at[:, j]

    def piece_shape(kind, a):
        return a.shape[1:] if kind == "col" else (a.shape[0],) + a.shape[2:]

    got_shapes = [S((3,) + piece_shape(k, a), a.dtype) for k, a in zip(kinds, qs)]

    def body(*refs):
        q_refs, got = refs[:NBIG], refs[NBIG:2 * NBIG]
        send, recv = refs[2 * NBIG:]
        x, y, c, me, chips, cj = _mesh_pos()
        cps = []
        for w in range(NBIG):
            for k in range(3):
                cp = pltpu.make_async_remote_copy(
                    src_ref=piece(kinds[w], q_refs[w], cj[k]), dst_ref=got[w].at[k],
                    send_sem=send.at[w, k], recv_sem=recv.at[w, k],
                    device_id=(*chips[k], c), device_id_type=MESH)
                cp.start()
                cps.append(cp)
        for cp in cps:
            cp.wait()

    return pl.pallas_call(
        body, in_specs=[ANY] * NBIG, out_specs=[ANY] * NBIG, out_shape=got_shapes, name="grad_chip_scatter",
        scratch_shapes=[pltpu.SemaphoreType.DMA((NBIG, 3))] * 2)(*qs)


def _chip_sum(q, got, kind, me, core, name):
    _, l_, rh, c_ = got.shape
    tr = _ew_tile(rh, c_, 4)
    col = kind == "col"

    def body(s_ref, q_ref, g0_ref, g1_ref, g2_ref, o_ref):
        acc = q_ref[...].astype(F32)
        for r in (g0_ref, g1_ref, g2_ref):
            acc = acc + r[...].astype(F32)
        o_ref[...] = acc

    gspecs = [pl.BlockSpec((None, None, tr, c_), functools.partial(lambda k, l, i, s: (k, l, i, 0), k))
              for k in range(3)]
    gs = pltpu.PrefetchScalarGridSpec(
        num_scalar_prefetch=1, grid=(l_, rh // tr),
        in_specs=[pl.BlockSpec((None, None, tr, c_),
                               (lambda l, i, s: (s[0], l, i, 0)) if col else (lambda l, i, s: (l, s[0], i, 0)))]
        + gspecs,
        out_specs=pl.BlockSpec((None, None, tr, c_), lambda l, i, s: (l, s[1], i, 0)))
    return pl.pallas_call(body, grid_spec=gs, out_shape=S((l_, 2, rh, c_), F32), name=name,
                          compiler_params=_cp("parallel", "parallel"))(jnp.stack([me, core]), q, got, got, got)


def _pair_gather(gs4):
    def body(*refs):
        i_refs, o_refs = refs[:NBIG], refs[NBIG:2 * NBIG]
        send, recv = refs[2 * NBIG:]
        x, y, c, *_ = _mesh_pos()
        cps = []
        for w in range(NBIG):
            cp = pltpu.make_async_remote_copy(
                src_ref=i_refs[w].at[:, c], dst_ref=o_refs[w].at[:, c], send_sem=send.at[w], recv_sem=recv.at[w],
                device_id=(x, y, 1 - c), device_id_type=MESH)
            cp.start()
            cps.append(cp)
        for cp in cps:
            cp.wait()

    outs = pl.pallas_call(
        body, in_specs=[ANY] * NBIG, out_specs=[ANY] * NBIG, out_shape=[S(a.shape, a.dtype) for a in gs4],
        input_output_aliases={w: w for w in range(NBIG)}, name="grad_pair_gather",
        scratch_shapes=[pltpu.SemaphoreType.DMA((NBIG,))] * 2)(*gs4)
    return [o.reshape(o.shape[0], 2 * o.shape[2], o.shape[3]) for o in outs]


def _all_reduce_small(p, name):
    rows = p.shape[0]

    def body(p_ref, o_ref, gath, send, recv):
        x, y, c = lax.axis_index("x"), lax.axis_index("y"), lax.axis_index("c")
        my_id = 4 * x + 2 * y + c
        gath[my_id] = p_ref[...]
        cps = []
        for r in range(1, N_DEV):
            bx, by, bc = (r >> 2) & 1, (r >> 1) & 1, r & 1
            tx, ty, tc = (1 - x if bx else x), (1 - y if by else y), (1 - c if bc else c)
            cp = pltpu.make_async_remote_copy(
                src_ref=p_ref, dst_ref=gath.at[my_id], send_sem=send.at[r - 1], recv_sem=recv.at[r - 1],
                device_id=(tx, ty, tc), device_id_type=MESH)
            cp.start()
            cps.append((cp, 4 * tx + 2 * ty + tc))
        for r, (cp, peer) in enumerate(cps):
            pltpu.make_async_remote_copy(
                src_ref=p_ref, dst_ref=gath.at[peer], send_sem=send.at[r], recv_sem=recv.at[r],
                device_id=(x, y, c), device_id_type=MESH).wait_recv()
        for cp, _ in cps:
            cp.wait_send()
        acc = gath[0]
        for s in range(1, N_DEV):
            acc = acc + gath[s]
        o_ref[...] = acc

    vm = pl.BlockSpec(memory_space=pltpu.VMEM)
    return pl.pallas_call(
        body, in_specs=[vm], out_specs=vm, out_shape=S((rows, 128), F32), name=name,
        scratch_shapes=[pltpu.VMEM((N_DEV, rows, 128), F32), pltpu.SemaphoreType.DMA((N_DEV - 1,)),
                        pltpu.SemaphoreType.DMA((N_DEV - 1,))],
        compiler_params=pltpu.CompilerParams(vmem_limit_bytes=VMEM_LIMIT))(p)


PACK_UNIT = 8 * 128


def _pack(arrs):
    parts = []
    for a in arrs:
        flat = a.reshape(-1)
        pad = (-flat.shape[0]) % PACK_UNIT
        parts.append(jnp.pad(flat, (0, pad)).reshape(-1, 128))
    return jnp.concatenate(parts, axis=0)


def _unpack(buf, shapes):
    outs, row = [], 0
    for shp in shapes:
        n = int(np.prod(shp))
        rows = -(-n // PACK_UNIT) * 8
        outs.append(buf[row:row + rows].reshape(-1)[:n].reshape(shp))
        row += rows
    return outs


SMALL = ("norm1_g", "gm_ln_g", "gm_ln_b", "gm_ws", "gm_bs", "conv_w", "conv_b", "conv_ln_g", "conv_ln_b",
         "norm2_g", "final_g")
WEIGHTS = ("norm1_g", "w_in", "gm_ln_g", "gm_ln_b", "gm_ws", "gm_bs", "conv_w", "conv_b", "conv_ln_g",
           "conv_ln_b", "w_out", "norm2_g", "w_ffn_in", "w_ffn_out", "final_g")


def kernel(x, norm1_g, w_in, gm_ln_g, gm_ln_b, gm_ws, gm_bs, conv_w, conv_b, conv_ln_g, conv_ln_b, w_out, norm2_g, w_ffn_in, w_ffn_out, final_g, loss_target, m_norm1_g, m_w_in, m_gm_ln_g, m_gm_ln_b, m_gm_ws, m_gm_bs, m_conv_w, m_conv_b, m_conv_ln_g, m_conv_ln_b, m_w_out, m_norm2_g, m_w_ffn_in, m_w_ffn_out, m_final_g, v_norm1_g, v_w_in, v_gm_ln_g, v_gm_ln_b, v_gm_ws, v_gm_bs, v_conv_w, v_conv_b, v_conv_ln_g, v_conv_ln_b, v_w_out, v_norm2_g, v_w_ffn_in, v_w_ffn_out, v_final_g):
    given = dict(locals())
    t = x.shape[1]
    xc = x.reshape(t, D)
    target = loss_target.reshape(t, D)
    me = 2 * lax.axis_index("x") + lax.axis_index("y")
    core = lax.axis_index("c")
    tb = _tables(t)

    me = me.astype(jnp.int32)
    core = core.astype(jnp.int32)
    gathered = _all_gather_big([_cast_to_gathered(given[n], kind, me, f"cast_{n}") for n, kind in BIG])
    wfull = {}
    for (n, kind), g5 in zip(BIG, gathered):
        if kind == "col":
            wfull[n] = g5.reshape(g5.shape[0], g5.shape[1], 2 * g5.shape[3], g5.shape[4])
        else:
            wfull[n] = g5.reshape(g5.shape[0], N_CHIPS * 2 * g5.shape[3], g5.shape[4])
    cshard = CV_W // N_CHIPS
    placed = lax.dynamic_update_slice(jnp.zeros((DEPTH, KCONV, CV_W), F32),
                                      conv_w * (core == 0).astype(F32), (0, 0, me * cshard))
    conv_w_full = _unpack(_all_reduce_small(_pack([placed]), "gather_conv_w"), [(DEPTH, KCONV, CV_W)])[0]
    cw32 = jnp.pad(conv_w_full, ((0, 0), (0, 32 - KCONV), (0, 0)))

    def row(a, l):
        return a[l].reshape(1, -1)

    saved = []
    for l in range(DEPTH):
        sv = {"x": xc}
        bias = jnp.repeat(gm_bs[l].T, GM_W // GM_HEADS, axis=1)
        proj = _norm_mm(xc, row(norm1_g, l), wfull["w_in"], l, f"in_proj{l}", 512)
        y_gm = _gm_fwd(proj, row(gm_ln_g, l), row(gm_ln_b, l), gm_ws[l], bias, f"gm_fwd{l}")
        rq, rk = _rotary(proj, tb["cos2"], tb["sin2"], f"rotary{l}")
        sf, sb = _ret_scan(rk, proj, 3, tb["zf"], tb["zb"], tb["gcf"], tb["gcb"], f"ret_state{l}")
        a, y_ret = _ret_out(rq, rk, proj, sf, sb, tb, f"ret_out{l}")
        y_cv, hc = _conv_fwd(proj, cw32[l], row(conv_b, l), row(conv_ln_g, l), row(conv_ln_b, l), f"conv_fwd{l}")
        x_mid = _parts_mm_res([y_gm, y_ret, y_cv], wfull["w_out"], l, xc, f"out_proj{l}")
        ff = _norm_mm(x_mid, row(norm2_g, l), wfull["w_ffn_in"], l, f"ffn_in{l}", 256)
        xc = _swiglu_mm_res(ff, wfull["w_ffn_out"], l, x_mid, f"ffn_out{l}")
        sv.update(bias=bias, proj=proj, y_gm=y_gm, rq=rq, rk=rk, sf=sf, sb=sb, a=a, y_ret=y_ret, y_cv=y_cv,
                  hc=hc, x_mid=x_mid, ff=ff)
        saved.append(sv)

    dx, d_final_g, lpart = _loss_head(xc, final_g.reshape(1, D), target, "loss_head")
    loss = lax.psum(lpart[0, 0], ("x", "y", "c"))

    small_g = {n: [None] * DEPTH for n in SMALL}
    big_g = {n: None for n, _ in BIG}
    for l in reversed(range(DEPTH)):
        sv = saved[l]
        proj = sv["proj"]
        dff = _dx_swiglu(dx, wfull["w_ffn_out"], l, sv["ff"], f"ffn_out_dx{l}")
        big_g["w_ffn_out"] = _dw_swiglu(sv["ff"], dx, l, big_g["w_ffn_out"], f"ffn_out_dw{l}")
        dx_mid, dg2 = _dx_norm([dff], wfull["w_ffn_in"], l, sv["x_mid"], row(norm2_g, l), dx, f"ffn_in_dx{l}", 256)
        big_g["w_ffn_in"] = _dw_norm_cols(sv["x_mid"], row(norm2_g, l), dff, w_ffn_in.shape[2], l,
                                          big_g["w_ffn_in"], f"ffn_in_dw{l}")
        dy_gm, dy_ret, dy_cv = _dx_parts(dx_mid, wfull["w_out"], l, [GM_W, RET_W, CV_W], f"out_proj_dx{l}")
        big_g["w_out"] = _dw_parts([sv["y_gm"], sv["y_ret"], sv["y_cv"]], dx_mid, l, big_g["w_out"],
                                   f"out_proj_dw{l}")
        d_cv, dcw, dcb, dclg, dclb = _conv_bwd(proj, dy_cv, sv["hc"], cw32[l], row(conv_ln_g, l),
                                               row(conv_ln_b, l), f"conv_bwd{l}")
        da, d_g = _ret_bwd_pre(dy_ret, sv["a"], proj, f"ret_bwd_pre{l}")
        gb_, gf_ = _ret_scan(sv["rq"], da, 0, tb["xib"], tb["xif"], tb["gcb"], tb["gcf"], f"ret_bwd_state{l}")
        d_qkv = _ret_bwd_main(sv["rq"], sv["rk"], proj, da, sv["sf"], sv["sb"], gf_, gb_, tb, f"ret_bwd_main{l}")
        d_gm, dws, dbs, dglg, dglb = _gm_bwd(proj, dy_gm, row(gm_ln_g, l), row(gm_ln_b, l), gm_ws[l],
                                             jnp.swapaxes(gm_ws[l], 1, 2), sv["bias"], f"gm_bwd{l}")
        dparts = [d_gm, d_qkv, d_g, d_cv]
        dx, dg1 = _dx_norm(dparts, wfull["w_in"], l, sv["x"], row(norm1_g, l), dx_mid, f"in_proj_dx{l}", 512)
        big_g["w_in"] = _dw_norm_parts(sv["x"], row(norm1_g, l), dparts, w_in.shape[2], l, big_g["w_in"],
                                       f"in_proj_dw{l}")
        for n, val in (("norm1_g", dg1[0]), ("gm_ln_g", dglg[0]), ("gm_ln_b", dglb[0]), ("gm_ws", dws),
                       ("gm_bs", dbs[:, :GM_HEADS].T), ("conv_w", dcw[:KCONV]), ("conv_b", dcb[0]),
                       ("conv_ln_g", dclg[0]), ("conv_ln_b", dclb[0]), ("norm2_g", dg2[0])):
            small_g[n][l] = val

    small_shapes = [given[n].shape if n != "conv_w" else (DEPTH, KCONV, CV_W) for n in SMALL]
    partials = [d_final_g[0] if n == "final_g" else jnp.stack(small_g[n]) for n in SMALL]
    reduced = dict(zip(SMALL, _unpack(_all_reduce_small(_pack(partials), "all_reduce_small_grads"), small_shapes)))
    reduced["conv_w"] = lax.dynamic_slice(reduced["conv_w"], (0, 0, me * cshard), (DEPTH, KCONV, cshard))

    grads5 = [big_g[n].reshape(_full_view(kind, *given[n].shape)) for n, kind in BIG]
    theirs = _pair_exchange(grads5)
    qs = [_pair_sum(g5, th, core, f"pair_sum_{n}") for (n, _), g5, th in zip(BIG, grads5, theirs)]
    got = _chip_scatter(qs)
    halves = [_chip_sum(q, g, kind, me, core, f"chip_sum_{n}") for (n, kind), q, g in zip(BIG, qs, got)]
    grads = dict(zip([n for n, _ in BIG], _pair_gather(halves)))
    grads.update(reduced)

    delta, new_m, new_v = {}, {}, {}
    for n, _ in BIG:
        delta[n], new_m[n], new_v[n] = _adamw(given[n], grads[n], given["m_" + n], given["v_" + n], f"adamw_{n}")
    shapes = [given[n].shape for n in SMALL]
    packed = [_pack([src[n] if src is grads else src[p + n] for n in SMALL])
              for src, p in ((given, ""), (grads, ""), (given, "m_"), (given, "v_"))]
    outs = _adamw(*packed, "adamw_small")
    for dst, buf in zip((delta, new_m, new_v), outs):
        dst.update(zip(SMALL, _unpack(buf, shapes)))

    return (loss, dx.reshape(1, t, D), *[grads[n] for n in WEIGHTS], *[delta[n] for n in WEIGHTS],
            *[new_m[n] for n in WEIGHTS], *[new_v[n] for n in WEIGHTS])
```

```python
import functools
import math

import numpy as np
import jax
import jax.numpy as jnp
from jax import lax
from jax.experimental import pallas as pl
from jax.experimental.pallas import tpu as pltpu

F32 = jnp.float32
BF16 = jnp.bfloat16
MXU_DTYPE = BF16
ACT_DTYPE = BF16
S = jax.ShapeDtypeStruct

D = 1024
DEPTH = 2
GM_W = 256
GM_HEADS = 4
RET_W = 512
HEADS = 4
HD = 128
CV_W = 256
KCONV = 31
IN_W = 2 * GM_W + 4 * RET_W + 2 * CV_W
FFN_H = 2816
CH = 128
ROPE_BASE = 10000.0
EPS = 1e-6
N_CHIPS = 4
N_DEV = 8
HALO = 16

ADAM_LR = 0.001
ADAM_B1 = 0.9
ADAM_B2 = 0.999
ADAM_EPS = 1e-08
ADAM_WD = 0.01
ADAM_STEP = 10

VMEM_LIMIT = 52 * 1024 * 1024
MESH = pl.DeviceIdType.MESH


def _cp(*sem, vmem=VMEM_LIMIT):
    return pltpu.CompilerParams(dimension_semantics=tuple(sem), vmem_limit_bytes=vmem)


def _mx(a):
    return a.astype(MXU_DTYPE)


def _dot(a, b):
    return jnp.dot(_mx(a), _mx(b), preferred_element_type=F32)


def _dot_nt(a, b):
    return lax.dot_general(_mx(a), _mx(b), (((1,), (1,)), ((), ())), preferred_element_type=F32)


def _dot_tn(a, b):
    return lax.dot_general(_mx(a), _mx(b), (((0,), (0,)), ((), ())), preferred_element_type=F32)


def _sigmoid(x):
    return 1.0 / (1.0 + jnp.exp(-x))


def _gelu(x):
    return 0.5 * x * (1.0 + lax.erf(x * (1.0 / math.sqrt(2.0))))


def _gelu_grad(x):
    return 0.5 * (1.0 + lax.erf(x * (1.0 / math.sqrt(2.0)))) + x * jnp.exp(-0.5 * x * x) * (1.0 / math.sqrt(2.0 * math.pi))


def _rms_r(x):
    return lax.rsqrt(jnp.mean(x * x, axis=-1, keepdims=True) + EPS)


def _rms_bwd(dh, x, r, g):
    u = dh * g
    dx = r * u - x * (r * r * r) * jnp.mean(u * x, axis=-1, keepdims=True)
    return dx, dh * x * r


def _standardize(a):
    mu = jnp.mean(a, axis=-1, keepdims=True)
    d = a - mu
    r = lax.rsqrt(jnp.mean(d * d, axis=-1, keepdims=True) + EPS)
    return d * r, r


def _standardize_bwd(do, o, r):
    return r * (do - jnp.mean(do, axis=-1, keepdims=True) - o * jnp.mean(do * o, axis=-1, keepdims=True))


def _acc_out(ref, val, first):
    @pl.when(first)
    def _():
        ref[...] = val

    @pl.when(jnp.logical_not(first))
    def _():
        ref[...] += val


def _row_tile(t, pref):
    tm = min(t, pref)
    assert t % tm == 0, (t, tm)
    return tm


def _segments(part_widths, shard_w):
    bounds = {0}
    off = 0
    for w in part_widths:
        off += w
        bounds.add(off)
    total = off
    for j in range(1, total // shard_w + 1):
        bounds.add(j * shard_w)
    bounds = sorted(bounds)
    starts = np.cumsum([0] + list(part_widths))
    segs = []
    for a, b in zip(bounds[:-1], bounds[1:]):
        p = int(np.searchsorted(starts, a, side="right") - 1)
        segs.append((p, a - int(starts[p]), a // shard_w, a % shard_w, b - a))
    return segs


def _wcol_spec(w, l):
    return pl.BlockSpec((N_CHIPS, None) + w.shape[2:], lambda *_: (0, l, 0, 0))


def _wrow_spec(w, l):
    return pl.BlockSpec((None,) + w.shape[1:], lambda *_: (l, 0, 0))


def _norm_mm(x, g, w, l, out_dtype, name, tm_pref):
    t = x.shape[0]
    nc = w.shape[3]
    tm = _row_tile(t, tm_pref)

    def body(x_ref, g_ref, w_ref, o_ref):
        xv = x_ref[...]
        h = _mx(xv * _rms_r(xv) * g_ref[...])
        for j in range(N_CHIPS):
            o_ref[:, j * nc:(j + 1) * nc] = jnp.dot(h, w_ref[j], preferred_element_type=F32).astype(o_ref.dtype)

    return pl.pallas_call(
        body, grid=(t // tm,),
        in_specs=[pl.BlockSpec((tm, D), lambda i: (i, 0)), pl.BlockSpec((1, D), lambda i: (0, 0)),
                  _wcol_spec(w, l)],
        out_specs=pl.BlockSpec((tm, N_CHIPS * nc), lambda i: (i, 0)),
        out_shape=S((t, N_CHIPS * nc), out_dtype), name=name, compiler_params=_cp("parallel"))(x, g, w)


def _parts_mm_res(parts, w, l, res, name):
    t = res.shape[0]
    tm = _row_tile(t, 512)
    widths = [p.shape[1] for p in parts]
    offs = np.cumsum([0] + widths)
    n = len(parts)

    def body(*refs):
        p_refs, w_ref, r_ref, o_ref = refs[:n], refs[n], refs[n + 1], refs[n + 2]
        acc = r_ref[...]
        for p in range(n):
            acc = acc + _dot(p_refs[p][...], w_ref[int(offs[p]):int(offs[p + 1]), :])
        o_ref[...] = acc

    return pl.pallas_call(
        body, grid=(t // tm,),
        in_specs=[pl.BlockSpec((tm, wd), lambda i: (i, 0)) for wd in widths]
        + [_wrow_spec(w, l), pl.BlockSpec((tm, D), lambda i: (i, 0))],
        out_specs=pl.BlockSpec((tm, D), lambda i: (i, 0)),
        out_shape=S((t, D), F32), name=name, compiler_params=_cp("parallel"))(*parts, w, res)


def _swiglu(ff):
    gate = ff[:, :FFN_H].astype(F32)
    up = ff[:, FFN_H:].astype(F32)
    return gate * _sigmoid(gate) * up


def _swiglu_mm_res(ff, w, l, res, name):
    t = res.shape[0]
    tm = _row_tile(t, 256)

    def body(f_ref, w_ref, r_ref, o_ref):
        o_ref[...] = r_ref[...] + _dot(_swiglu(f_ref[...]), w_ref[...])

    return pl.pallas_call(
        body, grid=(t // tm,),
        in_specs=[pl.BlockSpec((tm, 2 * FFN_H), lambda i: (i, 0)), _wrow_spec(w, l),
                  pl.BlockSpec((tm, D), lambda i: (i, 0))],
        out_specs=pl.BlockSpec((tm, D), lambda i: (i, 0)),
        out_shape=S((t, D), F32), name=name, compiler_params=_cp("parallel"))(ff, w, res)


def _dx_norm(dparts, w, l, x, g, dres, name, tm_pref):
    t = x.shape[0]
    nc = w.shape[3]
    tm = _row_tile(t, tm_pref)
    widths = [p.shape[1] for p in dparts]
    segs = _segments(widths, nc)
    n = len(dparts)

    def body(*refs):
        d_refs = refs[:n]
        w_ref, x_ref, g_ref, r_ref, dx_ref, dg_ref = refs[n:]
        dh = jnp.zeros((tm, D), F32)
        for (p, po, j, jo, wd) in segs:
            dh = dh + _dot_nt(d_refs[p][:, po:po + wd], w_ref[j, :, jo:jo + wd])
        xv = x_ref[...]
        dx, dgrow = _rms_bwd(dh, xv, _rms_r(xv), g_ref[...])
        dx_ref[...] = r_ref[...] + dx
        _acc_out(dg_ref, jnp.sum(dgrow, axis=0, keepdims=True), pl.program_id(0) == 0)

    return pl.pallas_call(
        body, grid=(t // tm,),
        in_specs=[pl.BlockSpec((tm, wd), lambda i: (i, 0)) for wd in widths]
        + [_wcol_spec(w, l), pl.BlockSpec((tm, D), lambda i: (i, 0)),
           pl.BlockSpec((1, D), lambda i: (0, 0)), pl.BlockSpec((tm, D), lambda i: (i, 0))],
        out_specs=[pl.BlockSpec((tm, D), lambda i: (i, 0)), pl.BlockSpec((1, D), lambda i: (0, 0))],
        out_shape=[S((t, D), F32), S((1, D), F32)], name=name,
        compiler_params=_cp("arbitrary"))(*dparts, w, x, g, dres)


def _dx_parts(dy, w, l, widths, name):
    t = dy.shape[0]
    tm = _row_tile(t, 512)
    offs = np.cumsum([0] + list(widths))
    n = len(widths)

    def body(dy_ref, w_ref, *o_refs):
        dyv = _mx(dy_ref[...])
        for p in range(n):
            o_refs[p][...] = _dot_nt(dyv, w_ref[int(offs[p]):int(offs[p + 1]), :])

    return pl.pallas_call(
        body, grid=(t // tm,),
        in_specs=[pl.BlockSpec((tm, D), lambda i: (i, 0)), _wrow_spec(w, l)],
        out_specs=[pl.BlockSpec((tm, wd), lambda i: (i, 0)) for wd in widths],
        out_shape=[S((t, wd), F32) for wd in widths], name=name, compiler_params=_cp("parallel"))(dy, w)


def _dx_swiglu(dy, w, l, ff, name):
    t = dy.shape[0]
    tm = _row_tile(t, 256)

    def body(dy_ref, w_ref, f_ref, o_ref):
        dact = _dot_nt(dy_ref[...], w_ref[...])
        gate = f_ref[:, :FFN_H].astype(F32)
        up = f_ref[:, FFN_H:].astype(F32)
        s = _sigmoid(gate)
        o_ref[:, :FFN_H] = (dact * up * (s * (1.0 + gate * (1.0 - s)))).astype(o_ref.dtype)
        o_ref[:, FFN_H:] = (dact * (gate * s)).astype(o_ref.dtype)

    return pl.pallas_call(
        body, grid=(t // tm,),
        in_specs=[pl.BlockSpec((tm, D), lambda i: (i, 0)), _wrow_spec(w, l),
                  pl.BlockSpec((tm, 2 * FFN_H), lambda i: (i, 0))],
        out_specs=pl.BlockSpec((tm, 2 * FFN_H), lambda i: (i, 0)),
        out_shape=S((t, 2 * FFN_H), ACT_DTYPE), name=name, compiler_params=_cp("parallel"))(dy, w, ff)


def _call_into(body, into, in_specs, args, **kw):
    if into is None:
        return pl.pallas_call(body, in_specs=in_specs, **kw)(*args)
    n_in = len(args)

    def wrapped(*refs):
        return body(*refs[:n_in], *refs[n_in + 1:])

    return pl.pallas_call(wrapped, in_specs=list(in_specs) + [pl.BlockSpec(memory_space=pl.ANY)],
                          input_output_aliases={n_in: 0}, **kw)(*args, into)


def _dw_norm_parts(x, g, dparts, nc, l, into, name):
    t = x.shape[0]
    tk = _row_tile(t, 512)
    widths = [p.shape[1] for p in dparts]
    segs = _segments(widths, nc)
    n = len(dparts)
    nk = t // tk

    def body(*refs):
        x_ref, g_ref = refs[0], refs[1]
        d_refs = refs[2:2 + n]
        o_ref, acc_ref = refs[2 + n], refs[3 + n]
        k = pl.program_id(0)
        xv = x_ref[...]
        h = _mx(xv * _rms_r(xv) * g_ref[...])

        @pl.when(k == 0)
        def _():
            acc_ref[...] = jnp.zeros_like(acc_ref)

        for (p, po, j, jo, wd) in segs:
            acc_ref[j, :, jo:jo + wd] += _dot_tn(h, d_refs[p][:, po:po + wd])

        @pl.when(k == nk - 1)
        def _():
            o_ref[...] = acc_ref[...].astype(o_ref.dtype)

    return _call_into(
        body, into,
        [pl.BlockSpec((tk, D), lambda k: (k, 0)), pl.BlockSpec((1, D), lambda k: (0, 0))]
        + [pl.BlockSpec((tk, wd), lambda k: (k, 0)) for wd in widths], [x, g, *dparts],
        grid=(nk,), out_specs=pl.BlockSpec((N_CHIPS, None, D, nc), lambda k: (0, l, 0, 0)),
        out_shape=S((N_CHIPS, DEPTH, D, nc), MXU_DTYPE), name=name,
        scratch_shapes=[pltpu.VMEM((N_CHIPS, D, nc), F32)], compiler_params=_cp("arbitrary"))


def _dw_norm_cols(x, g, dy, nc, l, into, name):
    t = x.shape[0]
    tk = _row_tile(t, 512)
    nk = t // tk

    def body(x_ref, g_ref, dy_ref, o_ref, acc_ref):
        k = pl.program_id(1)
        xv = x_ref[...]
        h = _mx(xv * _rms_r(xv) * g_ref[...])

        @pl.when(k == 0)
        def _():
            acc_ref[...] = jnp.zeros_like(acc_ref)

        acc_ref[...] += _dot_tn(h, dy_ref[...])

        @pl.when(k == nk - 1)
        def _():
            o_ref[...] = acc_ref[...].astype(o_ref.dtype)

    return _call_into(
        body, into,
        [pl.BlockSpec((tk, D), lambda j, k: (k, 0)), pl.BlockSpec((1, D), lambda j, k: (0, 0)),
         pl.BlockSpec((tk, nc), lambda j, k: (k, j))], [x, g, dy],
        grid=(N_CHIPS, nk), out_specs=pl.BlockSpec((None, None, D, nc), lambda j, k: (j, l, 0, 0)),
        out_shape=S((N_CHIPS, DEPTH, D, nc), MXU_DTYPE), name=name,
        scratch_shapes=[pltpu.VMEM((D, nc), F32)], compiler_params=_cp("parallel", "arbitrary"))


def _dw_parts(parts, dy, l, into, name):
    t = dy.shape[0]
    tk = _row_tile(t, 512)
    widths = [p.shape[1] for p in parts]
    offs = np.cumsum([0] + widths)
    ktot = int(offs[-1])
    n = len(parts)
    nk = t // tk

    def body(*refs):
        p_refs, dy_ref, o_ref, acc_ref = refs[:n], refs[n], refs[n + 1], refs[n + 2]
        k = pl.program_id(0)

        @pl.when(k == 0)
        def _():
            acc_ref[...] = jnp.zeros_like(acc_ref)

        dyv = _mx(dy_ref[...])
        for p in range(n):
            acc_ref[int(offs[p]):int(offs[p + 1]), :] += _dot_tn(p_refs[p][...], dyv)

        @pl.when(k == nk - 1)
        def _():
            o_ref[...] = acc_ref[...].astype(o_ref.dtype)

    return _call_into(
        body, into,
        [pl.BlockSpec((tk, wd), lambda k: (k, 0)) for wd in widths] + [pl.BlockSpec((tk, D), lambda k: (k, 0))],
        [*parts, dy],
        grid=(nk,), out_specs=pl.BlockSpec((None, ktot, D), lambda k: (l, 0, 0)),
        out_shape=S((DEPTH, ktot, D), MXU_DTYPE), name=name,
        scratch_shapes=[pltpu.VMEM((ktot, D), F32)], compiler_params=_cp("arbitrary"))


def _dw_swiglu(ff, dy, l, into, name):
    t = dy.shape[0]
    tk = _row_tile(t, 256)
    nk = t // tk

    def body(f_ref, dy_ref, o_ref, acc_ref):
        k = pl.program_id(0)

        @pl.when(k == 0)
        def _():
            acc_ref[...] = jnp.zeros_like(acc_ref)

        acc_ref[...] += _dot_tn(_swiglu(f_ref[...]), dy_ref[...])

        @pl.when(k == nk - 1)
        def _():
            o_ref[...] = acc_ref[...].astype(o_ref.dtype)

    return _call_into(
        body, into,
        [pl.BlockSpec((tk, 2 * FFN_H), lambda k: (k, 0)), pl.BlockSpec((tk, D), lambda k: (k, 0))], [ff, dy],
        grid=(nk,), out_specs=pl.BlockSpec((None, FFN_H, D), lambda k: (l, 0, 0)),
        out_shape=S((DEPTH, FFN_H, D), MXU_DTYPE), name=name,
        scratch_shapes=[pltpu.VMEM((FFN_H, D), F32)], compiler_params=_cp("arbitrary"))


def _tables(t):
    pos = jnp.arange(t, dtype=F32)
    half = HD // 2
    inv_freq = ROPE_BASE ** (-jnp.arange(half, dtype=F32) / half)
    ang = pos[:, None] * inv_freq[None, :]
    cos, sin = jnp.cos(ang), jnp.sin(ang)
    tb = {"cos2": jnp.concatenate([cos, cos], axis=1), "sin2": jnp.concatenate([-sin, sin], axis=1)}
    gf = 1.0 - jnp.exp2(-5.0 - jnp.arange(HEADS, dtype=F32))
    lgf = jnp.log(gf)[:, None]
    lgb = jnp.log(gf[::-1])[:, None]
    idx = jnp.arange(CH, dtype=F32)
    diff = idx[:, None] - idx[None, :]
    dfwd = jnp.where(diff >= 0, jnp.exp(lgf[:, :, None] * jnp.where(diff >= 0, diff, 0.0)), 0.0)
    dbwd = jnp.where(diff < 0, jnp.exp(lgb[:, :, None] * jnp.where(diff < 0, -diff, 0.0)), 0.0)
    tb["dm"] = dfwd + dbwd
    tb["dmt"] = jnp.swapaxes(tb["dm"], 1, 2)

    def lanes(a):
        return jnp.repeat(a.T, HD, axis=1)

    tb["xif"] = lanes(jnp.exp(lgf * (idx + 1)))
    tb["zf"] = lanes(jnp.exp(lgf * (CH - 1 - idx)))
    tb["xib"] = lanes(jnp.exp(lgb * (CH - idx)))
    tb["zb"] = lanes(jnp.exp(lgb * idx))
    tb["gcf"] = jnp.repeat(jnp.exp(lgf * CH), HD, axis=0).reshape(1, HEADS * HD)
    tb["gcb"] = jnp.repeat(jnp.exp(lgb * CH), HD, axis=0).reshape(1, HEADS * HD)
    return tb


def _full(shape):
    nd = len(shape)
    return pl.BlockSpec(shape, lambda *_: (0,) * nd)


def _gm_mixed(vn, ws_ref, bias):
    lane = lax.broadcasted_iota(jnp.int32, (CH, 128), 1)
    halves = []
    for hf in range(2):
        vh = _mx(vn[:, hf * 128:(hf + 1) * 128])
        r0 = jnp.dot(_mx(ws_ref[2 * hf]), vh, preferred_element_type=F32)
        r1 = jnp.dot(_mx(ws_ref[2 * hf + 1]), vh, preferred_element_type=F32)
        halves.append(jnp.where(lane < 64, r0, r1))
    return jnp.concatenate(halves, axis=1) + bias


def _gm_fwd(proj, ln_g, ln_b, ws, bias, name):
    t = proj.shape[0]
    tm = _row_tile(t, 512)

    def body(pu_ref, pv_ref, g_ref, b_ref, ws_ref, bias_ref, o_ref):
        for c in range(tm // CH):
            rows = slice(c * CH, (c + 1) * CH)
            u = _gelu(pu_ref[rows, :])
            o, _ = _standardize(_gelu(pv_ref[rows, :]))
            vn = o * g_ref[...] + b_ref[...]
            o_ref[rows, :] = (u * _gm_mixed(vn, ws_ref, bias_ref[...])).astype(o_ref.dtype)

    return pl.pallas_call(
        body, grid=(t // tm,),
        in_specs=[pl.BlockSpec((tm, GM_W), lambda i: (i, 0)), pl.BlockSpec((tm, GM_W), lambda i: (i, 1)),
                  _full((1, GM_W)), _full((1, GM_W)), _full((GM_HEADS, CH, CH)), _full((CH, GM_W))],
        out_specs=pl.BlockSpec((tm, GM_W), lambda i: (i, 0)),
        out_shape=S((t, GM_W), ACT_DTYPE), name=name, compiler_params=_cp("parallel"))(proj, proj, ln_g, ln_b, ws, bias)


def _gm_bwd(proj, dy, ln_g, ln_b, ws, wst, bias, name):
    t = proj.shape[0]
    tm = _row_tile(t, 512)
    nb = t // tm

    def body(pu_ref, pv_ref, dy_ref, g_ref, b_ref, ws_ref, wst_ref, bias_ref,
             d_ref, dws_ref, dbs_ref, dg_ref, db_ref, dbias_ref):
        first = pl.program_id(0) == 0
        lane = lax.broadcasted_iota(jnp.int32, (CH, 128), 1)
        dws = [jnp.zeros((CH, CH), F32) for _ in range(GM_HEADS)]
        dbias = jnp.zeros((CH, GM_W), F32)
        dg = jnp.zeros((1, GM_W), F32)
        db = jnp.zeros((1, GM_W), F32)
        for c in range(tm // CH):
            rows = slice(c * CH, (c + 1) * CH)
            pu = pu_ref[rows, :]
            pv = pv_ref[rows, :]
            u = _gelu(pu)
            o, r = _standardize(_gelu(pv))
            vn = o * g_ref[...] + b_ref[...]
            mixed = _gm_mixed(vn, ws_ref, bias_ref[...])
            dyv = dy_ref[rows, :]
            d_ref[rows, :GM_W] = (dyv * mixed * _gelu_grad(pu)).astype(d_ref.dtype)
            dmixed = dyv * u
            dbias = dbias + dmixed
            dvn_halves = []
            for hf in range(2):
                dm = dmixed[:, hf * 128:(hf + 1) * 128]
                vh = vn[:, hf * 128:(hf + 1) * 128]
                dm0 = jnp.where(lane < 64, dm, 0.0)
                dm1 = dm - dm0
                dws[2 * hf] = dws[2 * hf] + _dot_nt(dm0, vh)
                dws[2 * hf + 1] = dws[2 * hf + 1] + _dot_nt(dm1, vh)
                t0 = _dot(wst_ref[2 * hf], dm)
                t1 = _dot(wst_ref[2 * hf + 1], dm)
                dvn_halves.append(jnp.where(lane < 64, t0, t1))
            dvn = jnp.concatenate(dvn_halves, axis=1)
            dg = dg + jnp.sum(dvn * o, axis=0, keepdims=True)
            db = db + jnp.sum(dvn, axis=0, keepdims=True)
            dv = _standardize_bwd(dvn * g_ref[...], o, r)
            d_ref[rows, GM_W:] = (dv * _gelu_grad(pv)).astype(d_ref.dtype)
        for h in range(GM_HEADS):
            _acc_out(dws_ref.at[h], dws[h], first)
        _acc_out(dbias_ref, dbias, first)
        _acc_out(dg_ref, dg, first)
        _acc_out(db_ref, db, first)

        @pl.when(pl.program_id(0) == nb - 1)
        def _():
            tot = dbias_ref[...]
            head = lax.broadcasted_iota(jnp.int32, (CH, GM_W), 1) // (GM_W // GM_HEADS)
            out = jnp.zeros((CH, 128), F32)
            for h in range(GM_HEADS):
                s = jnp.sum(jnp.where(head == h, tot, 0.0), axis=1, keepdims=True)
                out = jnp.where(lane == h, s, out)
            dbs_ref[...] = out

    return pl.pallas_call(
        body, grid=(nb,),
        in_specs=[pl.BlockSpec((tm, GM_W), lambda i: (i, 0)), pl.BlockSpec((tm, GM_W), lambda i: (i, 1)),
                  pl.BlockSpec((tm, GM_W), lambda i: (i, 0)),
                  _full((1, GM_W)), _full((1, GM_W)), _full((GM_HEADS, CH, CH)), _full((GM_HEADS, CH, CH)),
                  _full((CH, GM_W))],
        out_specs=[pl.BlockSpec((tm, 2 * GM_W), lambda i: (i, 0)), _full((GM_HEADS, CH, CH)), _full((CH, 128)),
                   _full((1, GM_W)), _full((1, GM_W))],
        out_shape=[S((t, 2 * GM_W), ACT_DTYPE), S((GM_HEADS, CH, CH), F32), S((CH, 128), F32),
                   S((1, GM_W), F32), S((1, GM_W), F32)],
        scratch_shapes=[pltpu.VMEM((CH, GM_W), F32)],
        name=name, compiler_params=_cp("arbitrary"))(proj, proj, dy, ln_g, ln_b, ws, wst, bias)


def _rot(x, cos2, sin2):
    return x * cos2 + pltpu.roll(x, HD // 2, 1) * sin2


def _rot_bwd(dx, cos2, sin2):
    return dx * cos2 + pltpu.roll(dx * sin2, HD // 2, 1)


def _rotary(proj, cos2, sin2, name):
    t = proj.shape[0]
    tm = _row_tile(t, 512)
    scale = HD ** -0.5

    def body(q_ref, k_ref, c_ref, s_ref, rq_ref, rk_ref):
        c, s = c_ref[...], s_ref[...]
        for h in range(HEADS):
            cols = slice(h * HD, (h + 1) * HD)
            rq_ref[:, cols] = _rot(q_ref[:, cols], c, s)
            rk_ref[:, cols] = _rot(k_ref[:, cols], c, s) * scale

    return pl.pallas_call(
        body, grid=(t // tm,),
        in_specs=[pl.BlockSpec((tm, RET_W), lambda i: (i, 1)), pl.BlockSpec((tm, RET_W), lambda i: (i, 2)),
                  pl.BlockSpec((tm, HD), lambda i: (i, 0)), pl.BlockSpec((tm, HD), lambda i: (i, 0))],
        out_specs=[pl.BlockSpec((tm, RET_W), lambda i: (i, 0))] * 2,
        out_shape=[S((t, RET_W), F32)] * 2, name=name, compiler_params=_cp("parallel"))(proj, proj, cos2, sin2)


def _ret_scan(lhs, rhs, rhs_col, lp, ls, gp, gs, name):
    t = lhs.shape[0]
    n = t // CH
    r = 4 if n % 4 == 0 else 1
    ns = n // r

    def body(lp_ref, ls_ref, gp_ref, gs_ref, l1_ref, r1_ref, l2_ref, r2_ref, pre_ref, suf_ref, sp_ref, ss_ref):
        @pl.when(pl.program_id(0) == 0)
        def _():
            sp_ref[...] = jnp.zeros_like(sp_ref)
            ss_ref[...] = jnp.zeros_like(ss_ref)

        def kv(l_ref, r_ref, scale, rows):
            lv = l_ref[rows, :] * scale
            rv = r_ref[rows, :]
            return jnp.concatenate([_dot_tn(lv[:, h * HD:(h + 1) * HD], rv[:, h * HD:(h + 1) * HD])
                                    for h in range(HEADS)], axis=1)

        for j in range(r):
            pre_ref[j] = sp_ref[...]
            sp_ref[...] = sp_ref[...] * gp_ref[...] + kv(l1_ref, r1_ref, lp_ref[...], slice(j * CH, (j + 1) * CH))
        for j in reversed(range(r)):
            suf_ref[j] = ss_ref[...]
            ss_ref[...] = ss_ref[...] * gs_ref[...] + kv(l2_ref, r2_ref, ls_ref[...], slice(j * CH, (j + 1) * CH))

    w = HEADS * HD
    return pl.pallas_call(
        body, grid=(ns,),
        in_specs=[_full((CH, w)), _full((CH, w)), _full((1, w)), _full((1, w)),
                  pl.BlockSpec((r * CH, w), lambda s: (s, 0)), pl.BlockSpec((r * CH, w), lambda s: (s, rhs_col)),
                  pl.BlockSpec((r * CH, w), lambda s: (ns - 1 - s, 0)),
                  pl.BlockSpec((r * CH, w), lambda s: (ns - 1 - s, rhs_col))],
        out_specs=[pl.BlockSpec((r, HD, w), lambda s: (s, 0, 0)), pl.BlockSpec((r, HD, w), lambda s: (ns - 1 - s, 0, 0))],
        out_shape=[S((n, HD, w), F32)] * 2, name=name,
        scratch_shapes=[pltpu.VMEM((HD, w), F32), pltpu.VMEM((HD, w), F32)],
        compiler_params=_cp("arbitrary"))(lp, ls, gp, gs, lhs, rhs, lhs, rhs)


def _ret_out(rq, rk, proj, sf, sb, tb, name):
    t = rq.shape[0]
    r = 2 if (t // CH) % 2 == 0 else 1
    tm = r * CH
    w = HEADS * HD

    def body(rq_ref, rk_ref, v_ref, g_ref, sf_ref, sb_ref, dm_ref, xif_ref, xib_ref, a_ref, y_ref):
        for c in range(r):
            rows = slice(c * CH, (c + 1) * CH)
            for h in range(HEADS):
                cols = slice(h * HD, (h + 1) * HD)
                q = rq_ref[rows, cols]
                p = _dot_nt(q, rk_ref[rows, cols]) * dm_ref[h]
                a = (_dot(p, v_ref[rows, cols]) + _dot(q * xif_ref[:, cols], sf_ref[c, :, cols])
                     + _dot(q * xib_ref[:, cols], sb_ref[c, :, cols]))
                a_ref[rows, cols] = a
                o, _ = _standardize(a)
                gv = g_ref[rows, cols]
                y_ref[rows, cols] = (o * (gv * _sigmoid(gv))).astype(y_ref.dtype)

    return pl.pallas_call(
        body, grid=(t // tm,),
        in_specs=[pl.BlockSpec((tm, w), lambda i: (i, 0)), pl.BlockSpec((tm, w), lambda i: (i, 0)),
                  pl.BlockSpec((tm, w), lambda i: (i, 3)), pl.BlockSpec((tm, w), lambda i: (i, 4)),
                  pl.BlockSpec((r, HD, w), lambda i: (i, 0, 0)), pl.BlockSpec((r, HD, w), lambda i: (i, 0, 0)),
                  _full((HEADS, CH, CH)), _full((CH, w)), _full((CH, w))],
        out_specs=[pl.BlockSpec((tm, w), lambda i: (i, 0))] * 2,
        out_shape=[S((t, w), F32), S((t, w), ACT_DTYPE)], name=name,
        compiler_params=_cp("parallel"))(rq, rk, proj, proj, sf, sb, tb["dm"], tb["xif"], tb["xib"])


def _ret_bwd_pre(dy, a, proj, name):
    t = dy.shape[0]
    tm = _row_tile(t, 512)
    w = HEADS * HD

    def body(dy_ref, a_ref, g_ref, da_ref, dg_ref):
        for h in range(HEADS):
            cols = slice(h * HD, (h + 1) * HD)
            o, r = _standardize(a_ref[:, cols])
            gv = g_ref[:, cols]
            s = _sigmoid(gv)
            dyv = dy_ref[:, cols]
            dg_ref[:, cols] = (dyv * o * (s * (1.0 + gv * (1.0 - s)))).astype(dg_ref.dtype)
            da_ref[:, cols] = _standardize_bwd(dyv * (gv * s), o, r).astype(da_ref.dtype)

    return pl.pallas_call(
        body, grid=(t // tm,),
        in_specs=[pl.BlockSpec((tm, w), lambda i: (i, 0)), pl.BlockSpec((tm, w), lambda i: (i, 0)),
                  pl.BlockSpec((tm, w), lambda i: (i, 4))],
        out_specs=[pl.BlockSpec((tm, w), lambda i: (i, 0))] * 2,
        out_shape=[S((t, w), ACT_DTYPE)] * 2, name=name, compiler_params=_cp("parallel"))(dy, a, proj)


def _ret_bwd_main(rq, rk, proj, da, sf, sb, gf, gb, tb, name):
    t = rq.shape[0]
    r = 2 if (t // CH) % 2 == 0 else 1
    tm = r * CH
    w = HEADS * HD
    scale = HD ** -0.5

    def body(rq_ref, rk_ref, v_ref, da_ref, sf_ref, sb_ref, gf_ref, gb_ref, dm_ref, dmt_ref,
             xif_ref, xib_ref, zf_ref, zb_ref, c_ref, s_ref, o_ref):
        for c in range(r):
            rows = slice(c * CH, (c + 1) * CH)
            cos2, sin2 = c_ref[rows, :], s_ref[rows, :]
            for h in range(HEADS):
                cols = slice(h * HD, (h + 1) * HD)
                q, k, v, dav = rq_ref[rows, cols], rk_ref[rows, cols], v_ref[rows, cols], da_ref[rows, cols]
                qm, km, vm, dam = _mx(q), _mx(k), _mx(v), _mx(dav)
                dm, dmt = dm_ref[h], dmt_ref[h]
                pt = _dot_nt(km, qm) * dmt
                dp = _dot_nt(dam, vm) * dm
                dpt = _dot_nt(vm, dam) * dmt
                sfh, sbh, gfh, gbh = sf_ref[c, :, cols], sb_ref[c, :, cols], gf_ref[c, :, cols], gb_ref[c, :, cols]
                zf, zb = zf_ref[:, cols], zb_ref[:, cols]
                dv = _dot(pt, dam) + zf * _dot(km, gfh) + zb * _dot(km, gbh)
                drq = _dot(dp, km) + xif_ref[:, cols] * _dot_nt(dam, sfh) + xib_ref[:, cols] * _dot_nt(dam, sbh)
                drk = _dot(dpt, qm) + _dot_nt(zf * v, gfh) + _dot_nt(zb * v, gbh)
                o_ref[rows, h * HD:(h + 1) * HD] = _rot_bwd(drq, cos2, sin2).astype(o_ref.dtype)
                o_ref[rows, w + h * HD:w + (h + 1) * HD] = (_rot_bwd(drk, cos2, sin2) * scale).astype(o_ref.dtype)
                o_ref[rows, 2 * w + h * HD:2 * w + (h + 1) * HD] = dv.astype(o_ref.dtype)

    st = pl.BlockSpec((r, HD, w), lambda i: (i, 0, 0))
    return pl.pallas_call(
        body, grid=(t // tm,),
        in_specs=[pl.BlockSpec((tm, w), lambda i: (i, 0)), pl.BlockSpec((tm, w), lambda i: (i, 0)),
                  pl.BlockSpec((tm, w), lambda i: (i, 3)), pl.BlockSpec((tm, w), lambda i: (i, 0)),
                  st, st, st, st, _full((HEADS, CH, CH)), _full((HEADS, CH, CH)),
                  _full((CH, w)), _full((CH, w)), _full((CH, w)), _full((CH, w)),
                  pl.BlockSpec((tm, HD), lambda i: (i, 0)), pl.BlockSpec((tm, HD), lambda i: (i, 0))],
        out_specs=pl.BlockSpec((tm, 3 * w), lambda i: (i, 0)),
        out_shape=S((t, 3 * w), ACT_DTYPE), name=name,
        compiler_params=_cp("parallel"))(rq, rk, proj, da, sf, sb, gf, gb, tb["dm"], tb["dmt"],
                                         tb["xif"], tb["xib"], tb["zf"], tb["zb"], tb["cos2"], tb["sin2"])


CONV_TM = 256
CONV_SUB = 64
A_COL = (2 * GM_W + 4 * RET_W) // CV_W
G_COL = A_COL + 1


def _halo_specs(t, tm, col):
    nb16 = t // HALO
    per = tm // HALO
    return [pl.BlockSpec((tm, CV_W), lambda i: (i, col)),
            pl.BlockSpec((HALO, CV_W), lambda i: (jnp.maximum(i * per - 1, 0), col)),
            pl.BlockSpec((HALO, CV_W), lambda i: (jnp.minimum((i + 1) * per, nb16 - 1), col))]


def _fill_padded(dst_ref, prev, main, nxt, tm, i, nb):
    dst_ref[0:HALO, :] = jnp.where(i > 0, prev, 0.0)
    dst_ref[HALO:HALO + tm, :] = main
    dst_ref[HALO + tm:2 * HALO + tm, :] = jnp.where(i < nb - 1, nxt, 0.0)


def _conv_fwd(proj, cw, cb, ln_g, ln_b, name):
    t = proj.shape[0]
    tm = _row_tile(t, CONV_TM)
    nb = t // tm

    def body(a_ref, ap_ref, an_ref, g_ref, gp_ref, gn_ref, w_ref, b_ref, lg_ref, lb_ref, y_ref, hc_ref, hp_ref):
        i = pl.program_id(0)
        _fill_padded(hp_ref, ap_ref[...] * _sigmoid(gp_ref[...]), a_ref[...] * _sigmoid(g_ref[...]),
                     an_ref[...] * _sigmoid(gn_ref[...]), tm, i, nb)
        for sb in range(tm // CONV_SUB):
            acc = jnp.zeros((CONV_SUB, CV_W), F32) + b_ref[...]
            for k in range(KCONV):
                acc = acc + w_ref[k:k + 1, :] * hp_ref[pl.ds(sb * CONV_SUB + k + 1, CONV_SUB), :]
            rows = slice(sb * CONV_SUB, (sb + 1) * CONV_SUB)
            hc_ref[rows, :] = acc
            o, _ = _standardize(acc)
            z = o * lg_ref[...] + lb_ref[...]
            y_ref[rows, :] = (z * _sigmoid(z)).astype(y_ref.dtype)

    return pl.pallas_call(
        body, grid=(nb,),
        in_specs=_halo_specs(t, tm, A_COL) + _halo_specs(t, tm, G_COL)
        + [_full((32, CV_W)), _full((1, CV_W)), _full((1, CV_W)), _full((1, CV_W))],
        out_specs=[pl.BlockSpec((tm, CV_W), lambda i: (i, 0))] * 2,
        out_shape=[S((t, CV_W), ACT_DTYPE), S((t, CV_W), F32)], name=name,
        scratch_shapes=[pltpu.VMEM((tm + 2 * HALO, CV_W), F32)],
        compiler_params=_cp("parallel"))(proj, proj, proj, proj, proj, proj, cw, cb, ln_g, ln_b)


def _conv_bwd(proj, dy, hc, cw, ln_g, ln_b, name):
    t = proj.shape[0]
    tm = _row_tile(t, CONV_TM)
    nb = t // tm

    def body(a_ref, ap_ref, an_ref, g_ref, gp_ref, gn_ref, dy_ref, dyp_ref, dyn_ref, hc_ref, hcp_ref, hcn_ref,
             w_ref, lg_ref, lb_ref, d_ref, dw_ref, dcb_ref, dlg_ref, dlb_ref, hp_ref, dhp_ref, dwacc_ref):
        i = pl.program_id(0)
        first = i == 0

        def dhc_of(dyv, hcv):
            o, r = _standardize(hcv)
            z = o * lg_ref[...] + lb_ref[...]
            s = _sigmoid(z)
            dz = dyv * (s * (1.0 + z * (1.0 - s)))
            return _standardize_bwd(dz * lg_ref[...], o, r), dz, o

        dhc, dz, o = dhc_of(dy_ref[...], hc_ref[...])
        _acc_out(dlg_ref, jnp.sum(dz * o, axis=0, keepdims=True), first)
        _acc_out(dlb_ref, jnp.sum(dz, axis=0, keepdims=True), first)
        _acc_out(dcb_ref, jnp.sum(dhc, axis=0, keepdims=True), first)
        _fill_padded(dhp_ref, dhc_of(dyp_ref[...], hcp_ref[...])[0], dhc, dhc_of(dyn_ref[...], hcn_ref[...])[0],
                     tm, i, nb)
        _fill_padded(hp_ref, ap_ref[...] * _sigmoid(gp_ref[...]), a_ref[...] * _sigmoid(g_ref[...]),
                     an_ref[...] * _sigmoid(gn_ref[...]), tm, i, nb)

        @pl.when(first)
        def _():
            dwacc_ref[...] = jnp.zeros_like(dwacc_ref)

        for sb in range(tm // CONV_SUB):
            base = sb * CONV_SUB
            dmain = dhp_ref[pl.ds(HALO + base, CONV_SUB), :]
            dh = jnp.zeros((CONV_SUB, CV_W), F32)
            for k in range(KCONV):
                dh = dh + w_ref[k:k + 1, :] * dhp_ref[pl.ds(base + 2 * HALO - 1 - k, CONV_SUB), :]
                prod = dmain * hp_ref[pl.ds(base + k + 1, CONV_SUB), :]
                dwacc_ref[k * 8:(k + 1) * 8, :] += jnp.sum(prod.reshape(CONV_SUB // 8, 8, CV_W), axis=0)
            rows = slice(base, base + CONV_SUB)
            s = _sigmoid(g_ref[rows, :])
            d_ref[rows, :CV_W] = (dh * s).astype(d_ref.dtype)
            d_ref[rows, CV_W:] = (dh * a_ref[rows, :] * (s * (1.0 - s))).astype(d_ref.dtype)

        @pl.when(i == nb - 1)
        def _():
            for k in range(KCONV):
                dw_ref[k:k + 1, :] = jnp.sum(dwacc_ref[k * 8:(k + 1) * 8, :], axis=0, keepdims=True)
            dw_ref[KCONV:32, :] = jnp.zeros((32 - KCONV, CV_W), F32)

    hs = [pl.BlockSpec((tm, CV_W), lambda i: (i, 0)),
          pl.BlockSpec((HALO, CV_W), lambda i: (jnp.maximum(i * (tm // HALO) - 1, 0), 0)),
          pl.BlockSpec((HALO, CV_W), lambda i: (jnp.minimum((i + 1) * (tm // HALO), t // HALO - 1), 0))]
    return pl.pallas_call(
        body, grid=(nb,),
        in_specs=_halo_specs(t, tm, A_COL) + _halo_specs(t, tm, G_COL) + hs + hs
        + [_full((32, CV_W)), _full((1, CV_W)), _full((1, CV_W))],
        out_specs=[pl.BlockSpec((tm, 2 * CV_W), lambda i: (i, 0)), _full((32, CV_W)), _full((1, CV_W)),
                   _full((1, CV_W)), _full((1, CV_W))],
        out_shape=[S((t, 2 * CV_W), ACT_DTYPE), S((32, CV_W), F32), S((1, CV_W), F32), S((1, CV_W), F32),
                   S((1, CV_W), F32)],
        name=name,
        scratch_shapes=[pltpu.VMEM((tm + 2 * HALO, CV_W), F32), pltpu.VMEM((tm + 2 * HALO, CV_W), F32),
                        pltpu.VMEM((32 * 8, CV_W), F32)],
        compiler_params=_cp("arbitrary"))(proj, proj, proj, proj, proj, proj, dy, dy, dy, hc, hc, hc, cw, ln_g, ln_b)


def _loss_head(x, g, target, name):
    t = x.shape[0]
    tm = _row_tile(t, 512)

    def body(x_ref, g_ref, t_ref, dx_ref, dg_ref, l_ref):
        first = pl.program_id(0) == 0
        xv = x_ref[...]
        r = _rms_r(xv)
        e = xv * r * g_ref[...] - t_ref[...]
        dx, dgrow = _rms_bwd(e * (1.0 / D), xv, r, g_ref[...])
        dx_ref[...] = dx
        _acc_out(dg_ref, jnp.sum(dgrow, axis=0, keepdims=True), first)
        part = 0.5 * jnp.sum(jnp.mean(e * e, axis=-1, keepdims=True), axis=0, keepdims=True)
        _acc_out(l_ref, jnp.broadcast_to(part, (8, 128)), first)

    return pl.pallas_call(
        body, grid=(t // tm,),
        in_specs=[pl.BlockSpec((tm, D), lambda i: (i, 0)), _full((1, D)), pl.BlockSpec((tm, D), lambda i: (i, 0))],
        out_specs=[pl.BlockSpec((tm, D), lambda i: (i, 0)), _full((1, D)), _full((8, 128))],
        out_shape=[S((t, D), F32), S((1, D), F32), S((8, 128), F32)], name=name,
        compiler_params=_cp("arbitrary"))(x, g, target)


def _as2d(a):
    return a.reshape(-1, a.shape[-1])


def _ew_tile(rows, cols, n_arrays):
    budget = VMEM_LIMIT // 2
    tr = rows
    while tr * cols * 4 * n_arrays * 2 > budget and tr % 16 == 0:
        tr //= 2
    assert rows % tr == 0
    return tr


def _adamw(w, g, m, v, name):
    shape = w.shape
    w2, g2, m2, v2 = _as2d(w), _as2d(g), _as2d(m), _as2d(v)
    rows, cols = w2.shape
    tr = _ew_tile(rows, cols, 7)

    def body(w_ref, g_ref, m_ref, v_ref, d_ref, nm_ref, nv_ref):
        gv = g_ref[...]
        nm = ADAM_B1 * m_ref[...] + (1.0 - ADAM_B1) * gv
        nv = ADAM_B2 * v_ref[...] + (1.0 - ADAM_B2) * (gv * gv)
        m_hat = nm / (1.0 - ADAM_B1 ** ADAM_STEP)
        v_hat = nv / (1.0 - ADAM_B2 ** ADAM_STEP)
        d_ref[...] = -ADAM_LR * (m_hat / (jnp.sqrt(v_hat) + ADAM_EPS) + ADAM_WD * w_ref[...])
        nm_ref[...] = nm
        nv_ref[...] = nv

    spec = pl.BlockSpec((tr, cols), lambda i: (i, 0))
    outs = pl.pallas_call(body, grid=(rows // tr,), in_specs=[spec] * 4, out_specs=[spec] * 3,
                          out_shape=[S((rows, cols), F32)] * 3, name=name,
                          compiler_params=_cp("parallel"))(w2, g2, m2, v2)
    return tuple(o.reshape(shape) for o in outs)


BIG = (("w_in", "col"), ("w_out", "row"), ("w_ffn_in", "col"), ("w_ffn_out", "row"))
NBIG = len(BIG)
ANY = pl.BlockSpec(memory_space=pl.ANY)


def _full_view(kind, l, r, c):
    return (N_CHIPS, l, 2, r // 2, c) if kind == "col" else (l, N_CHIPS, 2, r // 2, c)


def _region(kind, ref5, j, h):
    return ref5.at[j, :, h] if kind == "col" else ref5.at[:, j, h]


def _mesh_pos():
    x, y, c = lax.axis_index("x"), lax.axis_index("y"), lax.axis_index("c")
    chips = [(1 - x, y), (x, 1 - y), (1 - x, 1 - y)]
    return x, y, c, 2 * x + y, chips, [2 * cx + cy for cx, cy in chips]


def _cast_to_gathered(w, kind, me, name):
    l_, r_, c_ = w.shape
    tr = _ew_tile(r_, c_, 2)
    col = kind == "col"
    shape = (N_CHIPS, l_, r_, c_) if col else (l_, N_CHIPS, r_, c_)

    def body(me_ref, w_ref, o_ref):
        o_ref[...] = w_ref[...].astype(o_ref.dtype)

    gs = pltpu.PrefetchScalarGridSpec(
        num_scalar_prefetch=1, grid=(l_, r_ // tr),
        in_specs=[pl.BlockSpec((None, tr, c_), lambda l, i, s: (l, i, 0))],
        out_specs=pl.BlockSpec((None, None, tr, c_),
                               (lambda l, i, s: (s[0], l, i, 0)) if col else (lambda l, i, s: (l, s[0], i, 0))))
    out = pl.pallas_call(body, grid_spec=gs, out_shape=S(shape, MXU_DTYPE), name=name,
                         compiler_params=_cp("parallel", "parallel"))(me.reshape(1), w)
    return out.reshape(_full_view(kind, l_, r_, c_))


def _all_gather_big(fulls):
    kinds = [k for _, k in BIG]

    def body(*refs):
        i_refs, f_refs = refs[:NBIG], refs[NBIG:2 * NBIG]
        isend, irecv, dsend, drecv = refs[2 * NBIG:]
        x, y, c, me, chips, cj = _mesh_pos()
        sib = (x, y, 1 - c)
        sends = []
        for w in range(NBIG):
            for k in range(3):
                cp = pltpu.make_async_remote_copy(
                    src_ref=_region(kinds[w], i_refs[w], me, c), dst_ref=_region(kinds[w], f_refs[w], me, c),
                    send_sem=isend.at[w, k], recv_sem=irecv.at[w, k],
                    device_id=(*chips[k], c), device_id_type=MESH)
                cp.start()
                sends.append(cp)
        for k in range(3):
            for w in range(NBIG):
                reg = _region(kinds[w], f_refs[w], cj[k], c)
                pltpu.make_async_remote_copy(
                    src_ref=reg, dst_ref=reg, send_sem=isend.at[w, k], recv_sem=irecv.at[w, k],
                    device_id=(*chips[k], c), device_id_type=MESH).wait_recv()
                cp = pltpu.make_async_remote_copy(
                    src_ref=reg, dst_ref=reg, send_sem=dsend.at[w, k], recv_sem=drecv.at[w, k],
                    device_id=sib, device_id_type=MESH)
                cp.start()
                sends.append(cp)
        for k in range(3):
            for w in range(NBIG):
                reg = _region(kinds[w], f_refs[w], cj[k], 1 - c)
                pltpu.make_async_remote_copy(
                    src_ref=reg, dst_ref=reg, send_sem=dsend.at[w, k], recv_sem=drecv.at[w, k],
                    device_id=sib, device_id_type=MESH).wait_recv()
        for cp in sends:
            cp.wait_send()

    return pl.pallas_call(
        body, in_specs=[ANY] * NBIG, out_specs=[ANY] * NBIG, out_shape=[S(a.shape, a.dtype) for a in fulls],
        input_output_aliases={w: w for w in range(NBIG)}, name="all_gather_weights",
        scratch_shapes=[pltpu.SemaphoreType.DMA((NBIG, 3))] * 4)(*fulls)


def _pair_exchange(grads5):
    out_shapes = [S(a.shape[:2] + a.shape[3:], a.dtype) for a in grads5]

    def body(*refs):
        g_refs, theirs = refs[:NBIG], refs[NBIG:2 * NBIG]
        send, recv = refs[2 * NBIG:]
        x, y, c, *_ = _mesh_pos()
        cps = []
        for w in range(NBIG):
            cp = pltpu.make_async_remote_copy(
                src_ref=g_refs[w].at[:, :, 1 - c], dst_ref=theirs[w], send_sem=send.at[w], recv_sem=recv.at[w],
                device_id=(x, y, 1 - c), device_id_type=MESH)
            cp.start()
            cps.append(cp)
        for cp in cps:
            cp.wait()

    return pl.pallas_call(
        body, in_specs=[ANY] * NBIG, out_specs=[ANY] * NBIG, out_shape=out_shapes, name="grad_pair_exchange",
        scratch_shapes=[pltpu.SemaphoreType.DMA((NBIG,))] * 2)(*grads5)


def _pair_sum(g5, theirs, core, name):
    a_, b_, _, rh, c_ = g5.shape
    tr = _ew_tile(rh, c_, 2)

    def body(s_ref, g_ref, t_ref, o_ref):
        o_ref[...] = (g_ref[...].astype(F32) + t_ref[...].astype(F32)).astype(o_ref.dtype)

    blk = pl.BlockSpec((None, None, tr, c_), lambda a, b, i, s: (a, b, i, 0))
    gs = pltpu.PrefetchScalarGridSpec(
        num_scalar_prefetch=1, grid=(a_, b_, rh // tr),
        in_specs=[pl.BlockSpec((None, None, None, tr, c_), lambda a, b, i, s: (a, b, s[0], i, 0)), blk],
        out_specs=blk)
    return pl.pallas_call(body, grid_spec=gs, out_shape=S(theirs.shape, theirs.dtype), name=name,
                          compiler_params=_cp("parallel", "parallel", "parallel"))(core.reshape(1), g5, theirs)


def _chip_scatter(qs):
    kinds = [k for _, k in BIG]

    def piece(kind, ref, j):
        return ref.at[j] if kind == "col" else ref.at[:, j]

    def piece_shape(kind, a):
        return a.shape[1:] if kind == "col" else (a.shape[0],) + a.shape[2:]

    got_shapes = [S((3,) + piece_shape(k, a), a.dtype) for k, a in zip(kinds, qs)]

    def body(*refs):
        q_refs, got = refs[:NBIG], refs[NBIG:2 * NBIG]
        send, recv = refs[2 * NBIG:]
        x, y, c, me, chips, cj = _mesh_pos()
        cps = []
        for w in range(NBIG):
            for k in range(3):
                cp = pltpu.make_async_remote_copy(
                    src_ref=piece(kinds[w], q_refs[w], cj[k]), dst_ref=got[w].at[k],
                    send_sem=send.at[w, k], recv_sem=recv.at[w, k],
                    device_id=(*chips[k], c), device_id_type=MESH)
                cp.start()
                cps.append(cp)
        for cp in cps:
            cp.wait()

    return pl.pallas_call(
        body, in_specs=[ANY] * NBIG, out_specs=[ANY] * NBIG, out_shape=got_shapes, name="grad_chip_scatter",
        scratch_shapes=[pltpu.SemaphoreType.DMA((NBIG, 3))] * 2)(*qs)


def _chip_sum(q, got, kind, me, core, name):
    _, l_, rh, c_ = got.shape
    tr = _ew_tile(rh, c_, 4)
    col = kind == "col"

    def body(s_ref, q_ref, g0_ref, g1_ref, g2_ref, o_ref):
        acc = q_ref[...].astype(F32)
        for r in (g0_ref, g1_ref, g2_ref):
            acc = acc + r[...].astype(F32)
        o_ref[...] = acc

    gspecs = [pl.BlockSpec((None, None, tr, c_), functools.partial(lambda k, l, i, s: (k, l, i, 0), k))
              for k in range(3)]
    gs = pltpu.PrefetchScalarGridSpec(
        num_scalar_prefetch=1, grid=(l_, rh // tr),
        in_specs=[pl.BlockSpec((None, None, tr, c_),
                               (lambda l, i, s: (s[0], l, i, 0)) if col else (lambda l, i, s: (l, s[0], i, 0)))]
        + gspecs,
        out_specs=pl.BlockSpec((None, None, tr, c_), lambda l, i, s: (l, s[1], i, 0)))
    return pl.pallas_call(body, grid_spec=gs, out_shape=S((l_, 2, rh, c_), F32), name=name,
                          compiler_params=_cp("parallel", "parallel"))(jnp.stack([me, core]), q, got, got, got)


def _pair_gather(gs4):
    def body(*refs):
        i_refs, o_refs = refs[:NBIG], refs[NBIG:2 * NBIG]
        send, recv = refs[2 * NBIG:]
        x, y, c, *_ = _mesh_pos()
        cps = []
        for w in range(NBIG):
            cp = pltpu.make_async_remote_copy(
                src_ref=i_refs[w].at[:, c], dst_ref=o_refs[w].at[:, c], send_sem=send.at[w], recv_sem=recv.at[w],
                device_id=(x, y, 1 - c), device_id_type=MESH)
            cp.start()
            cps.append(cp)
        for cp in cps:
            cp.wait()

    outs = pl.pallas_call(
        body, in_specs=[ANY] * NBIG, out_specs=[ANY] * NBIG, out_shape=[S(a.shape, a.dtype) for a in gs4],
        input_output_aliases={w: w for w in range(NBIG)}, name="grad_pair_gather",
        scratch_shapes=[pltpu.SemaphoreType.DMA((NBIG,))] * 2)(*gs4)
    return [o.reshape(o.shape[0], 2 * o.shape[2], o.shape[3]) for o in outs]


def _all_reduce_small(p, name):
    rows = p.shape[0]

    def body(p_ref, o_ref, gath, send, recv):
        x, y, c = lax.axis_index("x"), lax.axis_index("y"), lax.axis_index("c")
        my_id = 4 * x + 2 * y + c
        gath[my_id] = p_ref[...]
        cps = []
        for r in range(1, N_DEV):
            bx, by, bc = (r >> 2) & 1, (r >> 1) & 1, r & 1
            tx, ty, tc = (1 - x if bx else x), (1 - y if by else y), (1 - c if bc else c)
            cp = pltpu.make_async_remote_copy(
                src_ref=p_ref, dst_ref=gath.at[my_id], send_sem=send.at[r - 1], recv_sem=recv.at[r - 1],
                device_id=(tx, ty, tc), device_id_type=MESH)
            cp.start()
            cps.append((cp, 4 * tx + 2 * ty + tc))
        for r, (cp, peer) in enumerate(cps):
            pltpu.make_async_remote_copy(
                src_ref=p_ref, dst_ref=gath.at[peer], send_sem=send.at[r], recv_sem=recv.at[r],
                device_id=(x, y, c), device_id_type=MESH).wait_recv()
        for cp, _ in cps:
            cp.wait_send()
        acc = gath[0]
        for s in range(1, N_DEV):
            acc = acc + gath[s]
        o_ref[...] = acc

    vm = pl.BlockSpec(memory_space=pltpu.VMEM)
    return pl.pallas_call(
        body, in_specs=[vm], out_specs=vm, out_shape=S((rows, 128), F32), name=name,
        scratch_shapes=[pltpu.VMEM((N_DEV, rows, 128), F32), pltpu.SemaphoreType.DMA((N_DEV - 1,)),
                        pltpu.SemaphoreType.DMA((N_DEV - 1,))],
        compiler_params=pltpu.CompilerParams(vmem_limit_bytes=VMEM_LIMIT))(p)


PACK_UNIT = 8 * 128


def _pack(arrs):
    parts = []
    for a in arrs:
        flat = a.reshape(-1)
        pad = (-flat.shape[0]) % PACK_UNIT
        parts.append(jnp.pad(flat, (0, pad)).reshape(-1, 128))
    return jnp.concatenate(parts, axis=0)


def _unpack(buf, shapes):
    outs, row = [], 0
    for shp in shapes:
        n = int(np.prod(shp))
        rows = -(-n // PACK_UNIT) * 8
        outs.append(buf[row:row + rows].reshape(-1)[:n].reshape(shp))
        row += rows
    return outs


SMALL = ("norm1_g", "gm_ln_g", "gm_ln_b", "gm_ws", "gm_bs", "conv_w", "conv_b", "conv_ln_g", "conv_ln_b",
         "norm2_g", "final_g")
WEIGHTS = ("norm1_g", "w_in", "gm_ln_g", "gm_ln_b", "gm_ws", "gm_bs", "conv_w", "conv_b", "conv_ln_g",
           "conv_ln_b", "w_out", "norm2_g", "w_ffn_in", "w_ffn_out", "final_g")


def kernel(x, norm1_g, w_in, gm_ln_g, gm_ln_b, gm_ws, gm_bs, conv_w, conv_b, conv_ln_g, conv_ln_b, w_out, norm2_g, w_ffn_in, w_ffn_out, final_g, loss_target, m_norm1_g, m_w_in, m_gm_ln_g, m_gm_ln_b, m_gm_ws, m_gm_bs, m_conv_w, m_conv_b, m_conv_ln_g, m_conv_ln_b, m_w_out, m_norm2_g, m_w_ffn_in, m_w_ffn_out, m_final_g, v_norm1_g, v_w_in, v_gm_ln_g, v_gm_ln_b, v_gm_ws, v_gm_bs, v_conv_w, v_conv_b, v_conv_ln_g, v_conv_ln_b, v_w_out, v_norm2_g, v_w_ffn_in, v_w_ffn_out, v_final_g):
    given = dict(locals())
    t = x.shape[1]
    xc = x.reshape(t, D)
    target = loss_target.reshape(t, D)
    me = 2 * lax.axis_index("x") + lax.axis_index("y")
    core = lax.axis_index("c")
    tb = _tables(t)

    me = me.astype(jnp.int32)
    core = core.astype(jnp.int32)
    gathered = _all_gather_big([_cast_to_gathered(given[n], kind, me, f"cast_{n}") for n, kind in BIG])
    wfull = {}
    for (n, kind), g5 in zip(BIG, gathered):
        if kind == "col":
            wfull[n] = g5.reshape(g5.shape[0], g5.shape[1], 2 * g5.shape[3], g5.shape[4])
        else:
            wfull[n] = g5.reshape(g5.shape[0], N_CHIPS * 2 * g5.shape[3], g5.shape[4])
    cshard = CV_W // N_CHIPS
    placed = lax.dynamic_update_slice(jnp.zeros((DEPTH, KCONV, CV_W), F32),
                                      conv_w * (core == 0).astype(F32), (0, 0, me * cshard))
    conv_w_full = _unpack(_all_reduce_small(_pack([placed]), "gather_conv_w"), [(DEPTH, KCONV, CV_W)])[0]
    cw32 = jnp.pad(conv_w_full, ((0, 0), (0, 32 - KCONV), (0, 0)))

    def row(a, l):
        return a[l].reshape(1, -1)

    saved = []
    for l in range(DEPTH):
        sv = {"x": xc}
        bias = jnp.repeat(gm_bs[l].T, GM_W // GM_HEADS, axis=1)
        proj = _norm_mm(xc, row(norm1_g, l), wfull["w_in"], l, F32, f"in_proj{l}", 512)
        y_gm = _gm_fwd(proj, row(gm_ln_g, l), row(gm_ln_b, l), gm_ws[l], bias, f"gm_fwd{l}")
        rq, rk = _rotary(proj, tb["cos2"], tb["sin2"], f"rotary{l}")
        sf, sb = _ret_scan(rk, proj, 3, tb["zf"], tb["zb"], tb["gcf"], tb["gcb"], f"ret_state{l}")
        a, y_ret = _ret_out(rq, rk, proj, sf, sb, tb, f"ret_out{l}")
        y_cv, hc = _conv_fwd(proj, cw32[l], row(conv_b, l), row(conv_ln_g, l), row(conv_ln_b, l), f"conv_fwd{l}")
        x_mid = _parts_mm_res([y_gm, y_ret, y_cv], wfull["w_out"], l, xc, f"out_proj{l}")
        ff = _norm_mm(x_mid, row(norm2_g, l), wfull["w_ffn_in"], l, ACT_DTYPE, f"ffn_in{l}", 512)
        xc = _swiglu_mm_res(ff, wfull["w_ffn_out"], l, x_mid, f"ffn_out{l}")
        sv.update(bias=bias, proj=proj, y_gm=y_gm, rq=rq, rk=rk, sf=sf, sb=sb, a=a, y_ret=y_ret, y_cv=y_cv,
                  hc=hc, x_mid=x_mid, ff=ff)
        saved.append(sv)

    dx, d_final_g, lpart = _loss_head(xc, final_g.reshape(1, D), target, "loss_head")
    loss = lax.psum(lpart[0, 0], ("x", "y", "c"))

    small_g = {n: [None] * DEPTH for n in SMALL}
    big_g = {n: None for n, _ in BIG}
    for l in reversed(range(DEPTH)):
        sv = saved[l]
        proj = sv["proj"]
        dff = _dx_swiglu(dx, wfull["w_ffn_out"], l, sv["ff"], f"ffn_out_dx{l}")
        big_g["w_ffn_out"] = _dw_swiglu(sv["ff"], dx, l, big_g["w_ffn_out"], f"ffn_out_dw{l}")
        dx_mid, dg2 = _dx_norm([dff], wfull["w_ffn_in"], l, sv["x_mid"], row(norm2_g, l), dx, f"ffn_in_dx{l}", 256)
        big_g["w_ffn_in"] = _dw_norm_cols(sv["x_mid"], row(norm2_g, l), dff, w_ffn_in.shape[2], l,
                                          big_g["w_ffn_in"], f"ffn_in_dw{l}")
        dy_gm, dy_ret, dy_cv = _dx_parts(dx_mid, wfull["w_out"], l, [GM_W, RET_W, CV_W], f"out_proj_dx{l}")
        big_g["w_out"] = _dw_parts([sv["y_gm"], sv["y_ret"], sv["y_cv"]], dx_mid, l, big_g["w_out"],
                                   f"out_proj_dw{l}")
        d_cv, dcw, dcb, dclg, dclb = _conv_bwd(proj, dy_cv, sv["hc"], cw32[l], row(conv_ln_g, l),
                                               row(conv_ln_b, l), f"conv_bwd{l}")
        da, d_g = _ret_bwd_pre(dy_ret, sv["a"], proj, f"ret_bwd_pre{l}")
        gb_, gf_ = _ret_scan(sv["rq"], da, 0, tb["xib"], tb["xif"], tb["gcb"], tb["gcf"], f"ret_bwd_state{l}")
        d_qkv = _ret_bwd_main(sv["rq"], sv["rk"], proj, da, sv["sf"], sv["sb"], gf_, gb_, tb, f"ret_bwd_main{l}")
        d_gm, dws, dbs, dglg, dglb = _gm_bwd(proj, dy_gm, row(gm_ln_g, l), row(gm_ln_b, l), gm_ws[l],
                                             jnp.swapaxes(gm_ws[l], 1, 2), sv["bias"], f"gm_bwd{l}")
        dparts = [d_gm, d_qkv, d_g, d_cv]
        dx, dg1 = _dx_norm(dparts, wfull["w_in"], l, sv["x"], row(norm1_g, l), dx_mid, f"in_proj_dx{l}", 512)
        big_g["w_in"] = _dw_norm_parts(sv["x"], row(norm1_g, l), dparts, w_in.shape[2], l, big_g["w_in"],
                                       f"in_proj_dw{l}")
        for n, val in (("norm1_g", dg1[0]), ("gm_ln_g", dglg[0]), ("gm_ln_b", dglb[0]), ("gm_ws", dws),
                       ("gm_bs", dbs[:, :GM_HEADS].T), ("conv_w", dcw[:KCONV]), ("conv_b", dcb[0]),
                       ("conv_ln_g", dclg[0]), ("conv_ln_b", dclb[0]), ("norm2_g", dg2[0])):
            small_g[n][l] = val

    small_shapes = [given[n].shape if n != "conv_w" else (DEPTH, KCONV, CV_W) for n in SMALL]
    partials = [d_final_g[0] if n == "final_g" else jnp.stack(small_g[n]) for n in SMALL]
    reduced = dict(zip(SMALL, _unpack(_all_reduce_small(_pack(partials), "all_reduce_small_grads"), small_shapes)))
    reduced["conv_w"] = lax.dynamic_slice(reduced["conv_w"], (0, 0, me * cshard), (DEPTH, KCONV, cshard))

    grads5 = [big_g[n].reshape(_full_view(kind, *given[n].shape)) for n, kind in BIG]
    theirs = _pair_exchange(grads5)
    qs = [_pair_sum(g5, th, core, f"pair_sum_{n}") for (n, _), g5, th in zip(BIG, grads5, theirs)]
    got = _chip_scatter(qs)
    halves = [_chip_sum(q, g, kind, me, core, f"chip_sum_{n}") for (n, kind), q, g in zip(BIG, qs, got)]
    grads = dict(zip([n for n, _ in BIG], _pair_gather(halves)))
    grads.update(reduced)

    delta, new_m, new_v = {}, {}, {}
    for n, _ in BIG:
        delta[n], new_m[n], new_v[n] = _adamw(given[n], grads[n], given["m_" + n], given["v_" + n], f"adamw_{n}")
    shapes = [given[n].shape for n in SMALL]
    packed = [_pack([src[n] if src is grads else src[p + n] for n in SMALL])
              for src, p in ((given, ""), (grads, ""), (given, "m_"), (given, "v_"))]
    outs = _adamw(*packed, "adamw_small")
    for dst, buf in zip((delta, new_m, new_v), outs):
        dst.update(zip(SMALL, _unpack(buf, shapes)))

    return (loss, dx.reshape(1, t, D), *[grads[n] for n in WEIGHTS], *[delta[n] for n in WEIGHTS],
            *[new_m[n] for n in WEIGHTS], *[new_v[n] for n in WEIGHTS])
```

```python
import functools
import math

import numpy as np
import jax
import jax.numpy as jnp
from jax import lax
from jax.experimental import pallas as pl
from jax.experimental.pallas import tpu as pltpu

F32 = jnp.float32
BF16 = jnp.bfloat16
MXU_DTYPE = BF16
ACT_DTYPE = BF16
S = jax.ShapeDtypeStruct

D = 1024
DEPTH = 2
GM_W = 256
GM_HEADS = 4
RET_W = 512
HEADS = 4
HD = 128
CV_W = 256
KCONV = 31
IN_W = 2 * GM_W + 4 * RET_W + 2 * CV_W
FFN_H = 2816
CH = 128
ROPE_BASE = 10000.0
EPS = 1e-6
N_CHIPS = 4
N_DEV = 8
HALO = 16

ADAM_LR = 0.001
ADAM_B1 = 0.9
ADAM_B2 = 0.999
ADAM_EPS = 1e-08
ADAM_WD = 0.01
ADAM_STEP = 10

VMEM_LIMIT = 52 * 1024 * 1024
MESH = pl.DeviceIdType.MESH


def _cp(*sem, vmem=VMEM_LIMIT):
    return pltpu.CompilerParams(dimension_semantics=tuple(sem), vmem_limit_bytes=vmem)


def _mx(a):
    return a.astype(MXU_DTYPE)


def _dot(a, b):
    return jnp.dot(_mx(a), _mx(b), preferred_element_type=F32)


def _dot_nt(a, b):
    return lax.dot_general(_mx(a), _mx(b), (((1,), (1,)), ((), ())), preferred_element_type=F32)


def _dot_tn(a, b):
    return lax.dot_general(_mx(a), _mx(b), (((0,), (0,)), ((), ())), preferred_element_type=F32)


def _sigmoid(x):
    return 1.0 / (1.0 + jnp.exp(-x))


def _gelu(x):
    return 0.5 * x * (1.0 + lax.erf(x * (1.0 / math.sqrt(2.0))))


def _gelu_grad(x):
    return 0.5 * (1.0 + lax.erf(x * (1.0 / math.sqrt(2.0)))) + x * jnp.exp(-0.5 * x * x) * (1.0 / math.sqrt(2.0 * math.pi))


def _rms_r(x):
    return lax.rsqrt(jnp.mean(x * x, axis=-1, keepdims=True) + EPS)


def _rms_bwd(dh, x, r, g):
    u = dh * g
    dx = r * u - x * (r * r * r) * jnp.mean(u * x, axis=-1, keepdims=True)
    return dx, dh * x * r


def _standardize(a):
    mu = jnp.mean(a, axis=-1, keepdims=True)
    d = a - mu
    r = lax.rsqrt(jnp.mean(d * d, axis=-1, keepdims=True) + EPS)
    return d * r, r


def _standardize_bwd(do, o, r):
    return r * (do - jnp.mean(do, axis=-1, keepdims=True) - o * jnp.mean(do * o, axis=-1, keepdims=True))


def _acc_out(ref, val, first):
    @pl.when(first)
    def _():
        ref[...] = val

    @pl.when(jnp.logical_not(first))
    def _():
        ref[...] += val


def _row_tile(t, pref):
    tm = min(t, pref)
    assert t % tm == 0, (t, tm)
    return tm


def _segments(part_widths, shard_w):
    bounds = {0}
    off = 0
    for w in part_widths:
        off += w
        bounds.add(off)
    total = off
    for j in range(1, total // shard_w + 1):
        bounds.add(j * shard_w)
    bounds = sorted(bounds)
    starts = np.cumsum([0] + list(part_widths))
    segs = []
    for a, b in zip(bounds[:-1], bounds[1:]):
        p = int(np.searchsorted(starts, a, side="right") - 1)
        segs.append((p, a - int(starts[p]), a // shard_w, a % shard_w, b - a))
    return segs


ANY = pl.BlockSpec(memory_space=pl.ANY)


def _mesh_pos():
    x, y, c = lax.axis_index("x"), lax.axis_index("y"), lax.axis_index("c")
    chips = [(1 - x, y), (x, 1 - y), (1 - x, 1 - y)]
    return x, y, c, 2 * x + y, chips, [2 * cx + cy for cx, cy in chips]


def _rider_copies(kind, i_refs, o_refs, send, recv, pos):
    x, y, c, me, chips, cj = pos
    out = []
    for b, (i_ref, o_ref) in enumerate(zip(i_refs, o_refs)):
        for k in range(3):
            if kind == "ici":
                src, dst, land, dev = i_ref.at[me, c], o_ref.at[me, c], o_ref.at[cj[k], c], (*chips[k], c)
            elif kind == "d2d":
                src, dst, land, dev = i_ref.at[cj[k], c], o_ref.at[cj[k], c], o_ref.at[cj[k], 1 - c], (x, y, 1 - c)
            else:
                src, dst, land, dev = i_ref.at[cj[k]], o_ref.at[k], o_ref.at[k], (*chips[k], c)
            out.append(tuple(pltpu.make_async_remote_copy(
                src_ref=s_, dst_ref=d_, send_sem=send.at[b, k], recv_sem=recv.at[b, k],
                device_id=dev, device_id_type=MESH) for s_, d_ in ((src, dst), (land, land))))
    return out


def _rider_out_shape(kind, a):
    return S((3,) + a.shape[1:], a.dtype) if kind == "scatter" else S(a.shape, a.dtype)


def _pcall(body, args, riders, *, grid, in_specs, out_specs, out_shape, name, sem, scratch_shapes=()):
    outs = list(out_shape)
    if not riders:
        res = pl.pallas_call(body, grid=grid, in_specs=in_specs, out_specs=out_specs, out_shape=outs, name=name,
                             scratch_shapes=list(scratch_shapes), compiler_params=_cp(*sem))(*args)
        return res, []
    r_in = [a for _, bufs in riders for a in bufs]
    r_out = [_rider_out_shape(kind, a) for kind, bufs in riders for a in bufs]
    n_in, n_out, n_scr, n_r = len(args), len(outs), len(scratch_shapes), len(r_in)
    aliases, idx = {}, 0
    for kind, bufs in riders:
        for _ in bufs:
            if kind != "scatter":
                aliases[n_in + idx] = n_out + idx
            idx += 1
    sems = [pltpu.SemaphoreType.DMA((len(bufs), 3)) for _, bufs in riders for _ in range(2)]

    def wrapped(*refs):
        a, ri = refs[:n_in], refs[n_in:n_in + n_r]
        o, ro = refs[n_in + n_r:n_in + n_r + n_out], refs[n_in + n_r + n_out:n_in + 2 * n_r + n_out]
        scr = refs[n_in + 2 * n_r + n_out:n_in + 2 * n_r + n_out + n_scr]
        sm = refs[n_in + 2 * n_r + n_out + n_scr:]
        pos = _mesh_pos()
        copies, off = [], 0
        for r, (kind, bufs) in enumerate(riders):
            copies += _rider_copies(kind, ri[off:off + len(bufs)], ro[off:off + len(bufs)], sm[2 * r], sm[2 * r + 1], pos)
            off += len(bufs)
        ids = [pl.program_id(d) for d in range(len(grid))]
        first = functools.reduce(jnp.logical_and, [i == 0 for i in ids])
        last = functools.reduce(jnp.logical_and, [i == n - 1 for i, n in zip(ids, grid)])

        @pl.when(first)
        def _():
            for cp, _ in copies:
                cp.start()

        body(*a, *o, *scr)

        @pl.when(last)
        def _():
            for cp, land in copies:
                land.wait_recv()
                cp.wait_send()

    res = pl.pallas_call(
        wrapped, grid=grid, in_specs=list(in_specs) + [ANY] * n_r, out_specs=list(out_specs) + [ANY] * n_r,
        out_shape=outs + r_out, input_output_aliases=aliases, name=name,
        scratch_shapes=list(scratch_shapes) + sems, compiler_params=_cp(*(("arbitrary",) * len(grid))))(*args, *r_in)
    return res[:n_out], res[n_out:]


def _wcol_spec(w):
    return pl.BlockSpec(w.shape, lambda *_: (0, 0, 0))


def _wrow_spec(w):
    return pl.BlockSpec(w.shape, lambda *_: (0, 0))


def _norm_mm(x, g, w, out_dtype, name, tm_pref, riders=()):
    t = x.shape[0]
    nc = w.shape[2]
    tm = _row_tile(t, tm_pref)

    def body(x_ref, g_ref, w_ref, o_ref):
        xv = x_ref[...]
        h = _mx(xv * _rms_r(xv) * g_ref[...])
        for j in range(N_CHIPS):
            o_ref[:, j * nc:(j + 1) * nc] = jnp.dot(h, w_ref[j], preferred_element_type=F32).astype(o_ref.dtype)

    (out,), rid = _pcall(
        body, [x, g, w], riders, grid=(t // tm,),
        in_specs=[pl.BlockSpec((tm, D), lambda i: (i, 0)), pl.BlockSpec((1, D), lambda i: (0, 0)), _wcol_spec(w)],
        out_specs=[pl.BlockSpec((tm, N_CHIPS * nc), lambda i: (i, 0))],
        out_shape=[S((t, N_CHIPS * nc), out_dtype)], name=name, sem=("parallel",))
    return out, rid


def _parts_mm_res(parts, w, res, name):
    t = res.shape[0]
    tm = _row_tile(t, 512)
    widths = [p.shape[1] for p in parts]
    offs = np.cumsum([0] + widths)
    n = len(parts)

    def body(*refs):
        p_refs, w_ref, r_ref, o_ref = refs[:n], refs[n], refs[n + 1], refs[n + 2]
        acc = r_ref[...]
        for p in range(n):
            acc = acc + _dot(p_refs[p][...], w_ref[int(offs[p]):int(offs[p + 1]), :])
        o_ref[...] = acc

    return pl.pallas_call(
        body, grid=(t // tm,),
        in_specs=[pl.BlockSpec((tm, wd), lambda i: (i, 0)) for wd in widths]
        + [_wrow_spec(w), pl.BlockSpec((tm, D), lambda i: (i, 0))],
        out_specs=pl.BlockSpec((tm, D), lambda i: (i, 0)),
        out_shape=S((t, D), F32), name=name, compiler_params=_cp("parallel"))(*parts, w, res)


def _swiglu(ff):
    gate = ff[:, :FFN_H].astype(F32)
    up = ff[:, FFN_H:].astype(F32)
    return gate * _sigmoid(gate) * up


def _swiglu_mm_res(ff, w, res, name, riders=()):
    t = res.shape[0]
    tm = _row_tile(t, 256)

    def body(f_ref, w_ref, r_ref, o_ref):
        o_ref[...] = r_ref[...] + _dot(_swiglu(f_ref[...]), w_ref[...])

    (out,), rid = _pcall(
        body, [ff, w, res], riders, grid=(t // tm,),
        in_specs=[pl.BlockSpec((tm, 2 * FFN_H), lambda i: (i, 0)), _wrow_spec(w),
                  pl.BlockSpec((tm, D), lambda i: (i, 0))],
        out_specs=[pl.BlockSpec((tm, D), lambda i: (i, 0))],
        out_shape=[S((t, D), F32)], name=name, sem=("parallel",))
    return out, rid


def _dx_norm(dparts, w, x, g, dres, name, tm_pref):
    t = x.shape[0]
    nc = w.shape[2]
    tm = _row_tile(t, tm_pref)
    widths = [p.shape[1] for p in dparts]
    segs = _segments(widths, nc)
    n = len(dparts)

    def body(*refs):
        d_refs = refs[:n]
        w_ref, x_ref, g_ref, r_ref, dx_ref, dg_ref = refs[n:]
        dh = jnp.zeros((tm, D), F32)
        for (p, po, j, jo, wd) in segs:
            dh = dh + _dot_nt(d_refs[p][:, po:po + wd], w_ref[j, :, jo:jo + wd])
        xv = x_ref[...]
        dx, dgrow = _rms_bwd(dh, xv, _rms_r(xv), g_ref[...])
        dx_ref[...] = r_ref[...] + dx
        _acc_out(dg_ref, jnp.sum(dgrow, axis=0, keepdims=True), pl.program_id(0) == 0)

    return pl.pallas_call(
        body, grid=(t // tm,),
        in_specs=[pl.BlockSpec((tm, wd), lambda i: (i, 0)) for wd in widths]
        + [_wcol_spec(w), pl.BlockSpec((tm, D), lambda i: (i, 0)),
           pl.BlockSpec((1, D), lambda i: (0, 0)), pl.BlockSpec((tm, D), lambda i: (i, 0))],
        out_specs=[pl.BlockSpec((tm, D), lambda i: (i, 0)), pl.BlockSpec((1, D), lambda i: (0, 0))],
        out_shape=[S((t, D), F32), S((1, D), F32)], name=name,
        compiler_params=_cp("arbitrary"))(*dparts, w, x, g, dres)


def _dx_parts(dy, w, widths, name):
    t = dy.shape[0]
    tm = _row_tile(t, 512)
    offs = np.cumsum([0] + list(widths))
    n = len(widths)

    def body(dy_ref, w_ref, *o_refs):
        dyv = _mx(dy_ref[...])
        for p in range(n):
            o_refs[p][...] = _dot_nt(dyv, w_ref[int(offs[p]):int(offs[p + 1]), :])

    return pl.pallas_call(
        body, grid=(t // tm,),
        in_specs=[pl.BlockSpec((tm, D), lambda i: (i, 0)), _wrow_spec(w)],
        out_specs=[pl.BlockSpec((tm, wd), lambda i: (i, 0)) for wd in widths],
        out_shape=[S((t, wd), F32) for wd in widths], name=name, compiler_params=_cp("parallel"))(dy, w)


def _dx_swiglu(dy, w, ff, name):
    t = dy.shape[0]
    tm = _row_tile(t, 256)

    def body(dy_ref, w_ref, f_ref, o_ref):
        dact = _dot_nt(dy_ref[...], w_ref[...])
        gate = f_ref[:, :FFN_H].astype(F32)
        up = f_ref[:, FFN_H:].astype(F32)
        s = _sigmoid(gate)
        o_ref[:, :FFN_H] = (dact * up * (s * (1.0 + gate * (1.0 - s)))).astype(o_ref.dtype)
        o_ref[:, FFN_H:] = (dact * (gate * s)).astype(o_ref.dtype)

    return pl.pallas_call(
        body, grid=(t // tm,),
        in_specs=[pl.BlockSpec((tm, D), lambda i: (i, 0)), _wrow_spec(w),
                  pl.BlockSpec((tm, 2 * FFN_H), lambda i: (i, 0))],
        out_specs=pl.BlockSpec((tm, 2 * FFN_H), lambda i: (i, 0)),
        out_shape=S((t, 2 * FFN_H), ACT_DTYPE), name=name, compiler_params=_cp("parallel"))(dy, w, ff)


def _call_into(body, into, in_specs, args, *, n_prefetch, grid, out_specs, **kw):
    n_in = len(args)
    if into is None:
        gs = pltpu.PrefetchScalarGridSpec(num_scalar_prefetch=n_prefetch, grid=grid, in_specs=in_specs,
                                          out_specs=out_specs)
        return pl.pallas_call(body, grid_spec=gs, **kw)(*args)

    def wrapped(*refs):
        return body(*refs[:n_in], *refs[n_in + 1:])

    gs = pltpu.PrefetchScalarGridSpec(num_scalar_prefetch=n_prefetch, grid=grid,
                                      in_specs=list(in_specs) + [ANY], out_specs=out_specs)
    return pl.pallas_call(wrapped, grid_spec=gs, input_output_aliases={n_in: 0}, **kw)(*args, into)


def _dw_norm_parts(x, g, dparts, nc, name):
    t = x.shape[0]
    tk = _row_tile(t, 512)
    widths = [p.shape[1] for p in dparts]
    segs = _segments(widths, nc)
    n = len(dparts)
    nk = t // tk

    def body(*refs):
        x_ref, g_ref = refs[0], refs[1]
        d_refs = refs[2:2 + n]
        o_ref, acc_ref = refs[2 + n], refs[3 + n]
        k = pl.program_id(0)
        xv = x_ref[...]
        h = _mx(xv * _rms_r(xv) * g_ref[...])

        @pl.when(k == 0)
        def _():
            acc_ref[...] = jnp.zeros_like(acc_ref)

        for (p, po, j, jo, wd) in segs:
            acc_ref[j, :, jo:jo + wd] += _dot_tn(h, d_refs[p][:, po:po + wd])

        @pl.when(k == nk - 1)
        def _():
            o_ref[...] = acc_ref[...].astype(o_ref.dtype)

    return pl.pallas_call(
        body, grid=(nk,),
        in_specs=[pl.BlockSpec((tk, D), lambda k: (k, 0)), pl.BlockSpec((1, D), lambda k: (0, 0))]
        + [pl.BlockSpec((tk, wd), lambda k: (k, 0)) for wd in widths],
        out_specs=pl.BlockSpec((N_CHIPS, D, nc), lambda k: (0, 0, 0)),
        out_shape=S((N_CHIPS, D, nc), MXU_DTYPE), name=name,
        scratch_shapes=[pltpu.VMEM((N_CHIPS, D, nc), F32)], compiler_params=_cp("arbitrary"))(x, g, *dparts)


def _dw_norm_cols(x, g, dy, nc, name, riders=()):
    t = x.shape[0]
    tk = _row_tile(t, 512)
    nk = t // tk

    def body(x_ref, g_ref, dy_ref, o_ref, acc_ref):
        k = pl.program_id(1)
        xv = x_ref[...]
        h = _mx(xv * _rms_r(xv) * g_ref[...])

        @pl.when(k == 0)
        def _():
            acc_ref[...] = jnp.zeros_like(acc_ref)

        acc_ref[...] += _dot_tn(h, dy_ref[...])

        @pl.when(k == nk - 1)
        def _():
            o_ref[...] = acc_ref[...].astype(o_ref.dtype)

    (out,), rid = _pcall(
        body, [x, g, dy], riders, grid=(N_CHIPS, nk),
        in_specs=[pl.BlockSpec((tk, D), lambda j, k: (k, 0)), pl.BlockSpec((1, D), lambda j, k: (0, 0)),
                  pl.BlockSpec((tk, nc), lambda j, k: (k, j))],
        out_specs=[pl.BlockSpec((None, D, nc), lambda j, k: (j, 0, 0))],
        out_shape=[S((N_CHIPS, D, nc), MXU_DTYPE)], name=name, sem=("parallel", "arbitrary"),
        scratch_shapes=[pltpu.VMEM((D, nc), F32)])
    return out, rid


def _dw_parts(parts, dy, name):
    t = dy.shape[0]
    tk = _row_tile(t, 512)
    widths = [p.shape[1] for p in parts]
    offs = np.cumsum([0] + widths)
    ktot = int(offs[-1])
    n = len(parts)
    nk = t // tk

    def body(*refs):
        p_refs, dy_ref, o_ref, acc_ref = refs[:n], refs[n], refs[n + 1], refs[n + 2]
        k = pl.program_id(0)

        @pl.when(k == 0)
        def _():
            acc_ref[...] = jnp.zeros_like(acc_ref)

        dyv = _mx(dy_ref[...])
        for p in range(n):
            acc_ref[int(offs[p]):int(offs[p + 1]), :] += _dot_tn(p_refs[p][...], dyv)

        @pl.when(k == nk - 1)
        def _():
            o_ref[...] = acc_ref[...].astype(o_ref.dtype)

    return pl.pallas_call(
        body, grid=(nk,),
        in_specs=[pl.BlockSpec((tk, wd), lambda k: (k, 0)) for wd in widths]
        + [pl.BlockSpec((tk, D), lambda k: (k, 0))],
        out_specs=pl.BlockSpec((ktot, D), lambda k: (0, 0)),
        out_shape=S((ktot, D), MXU_DTYPE), name=name,
        scratch_shapes=[pltpu.VMEM((ktot, D), F32)], compiler_params=_cp("arbitrary"))(*parts, dy)


def _dw_swiglu(ff, dy, name):
    t = dy.shape[0]
    tk = _row_tile(t, 256)
    nk = t // tk

    def body(f_ref, dy_ref, o_ref, acc_ref):
        k = pl.program_id(0)

        @pl.when(k == 0)
        def _():
            acc_ref[...] = jnp.zeros_like(acc_ref)

        acc_ref[...] += _dot_tn(_swiglu(f_ref[...]), dy_ref[...])

        @pl.when(k == nk - 1)
        def _():
            o_ref[...] = acc_ref[...].astype(o_ref.dtype)

    return pl.pallas_call(
        body, grid=(nk,),
        in_specs=[pl.BlockSpec((tk, 2 * FFN_H), lambda k: (k, 0)), pl.BlockSpec((tk, D), lambda k: (k, 0))],
        out_specs=pl.BlockSpec((FFN_H, D), lambda k: (0, 0)),
        out_shape=S((FFN_H, D), MXU_DTYPE), name=name,
        scratch_shapes=[pltpu.VMEM((FFN_H, D), F32)], compiler_params=_cp("arbitrary"))(ff, dy)


def _tables(t):
    pos = jnp.arange(t, dtype=F32)
    half = HD // 2
    inv_freq = ROPE_BASE ** (-jnp.arange(half, dtype=F32) / half)
    ang = pos[:, None] * inv_freq[None, :]
    cos, sin = jnp.cos(ang), jnp.sin(ang)
    tb = {"cos2": jnp.concatenate([cos, cos], axis=1), "sin2": jnp.concatenate([-sin, sin], axis=1)}
    gf = 1.0 - jnp.exp2(-5.0 - jnp.arange(HEADS, dtype=F32))
    lgf = jnp.log(gf)[:, None]
    lgb = jnp.log(gf[::-1])[:, None]
    idx = jnp.arange(CH, dtype=F32)
    diff = idx[:, None] - idx[None, :]
    dfwd = jnp.where(diff >= 0, jnp.exp(lgf[:, :, None] * jnp.where(diff >= 0, diff, 0.0)), 0.0)
    dbwd = jnp.where(diff < 0, jnp.exp(lgb[:, :, None] * jnp.where(diff < 0, -diff, 0.0)), 0.0)
    tb["dm"] = dfwd + dbwd
    tb["dmt"] = jnp.swapaxes(tb["dm"], 1, 2)

    def lanes(a):
        return jnp.repeat(a.T, HD, axis=1)

    tb["xif"] = lanes(jnp.exp(lgf * (idx + 1)))
    tb["zf"] = lanes(jnp.exp(lgf * (CH - 1 - idx)))
    tb["xib"] = lanes(jnp.exp(lgb * (CH - idx)))
    tb["zb"] = lanes(jnp.exp(lgb * idx))
    tb["gcf"] = jnp.repeat(jnp.exp(lgf * CH), HD, axis=0).reshape(1, HEADS * HD)
    tb["gcb"] = jnp.repeat(jnp.exp(lgb * CH), HD, axis=0).reshape(1, HEADS * HD)
    return tb


def _full(shape):
    nd = len(shape)
    return pl.BlockSpec(shape, lambda *_: (0,) * nd)


def _gm_mixed(vn, ws_ref, bias):
    lane = lax.broadcasted_iota(jnp.int32, (CH, 128), 1)
    halves = []
    for hf in range(2):
        vh = _mx(vn[:, hf * 128:(hf + 1) * 128])
        r0 = jnp.dot(_mx(ws_ref[2 * hf]), vh, preferred_element_type=F32)
        r1 = jnp.dot(_mx(ws_ref[2 * hf + 1]), vh, preferred_element_type=F32)
        halves.append(jnp.where(lane < 64, r0, r1))
    return jnp.concatenate(halves, axis=1) + bias


def _gm_fwd(proj, ln_g, ln_b, ws, bias, name):
    t = proj.shape[0]
    tm = _row_tile(t, 512)

    def body(pu_ref, pv_ref, g_ref, b_ref, ws_ref, bias_ref, o_ref):
        for c in range(tm // CH):
            rows = slice(c * CH, (c + 1) * CH)
            u = _gelu(pu_ref[rows, :])
            o, _ = _standardize(_gelu(pv_ref[rows, :]))
            vn = o * g_ref[...] + b_ref[...]
            o_ref[rows, :] = (u * _gm_mixed(vn, ws_ref, bias_ref[...])).astype(o_ref.dtype)

    return pl.pallas_call(
        body, grid=(t // tm,),
        in_specs=[pl.BlockSpec((tm, GM_W), lambda i: (i, 0)), pl.BlockSpec((tm, GM_W), lambda i: (i, 1)),
                  _full((1, GM_W)), _full((1, GM_W)), _full((GM_HEADS, CH, CH)), _full((CH, GM_W))],
        out_specs=pl.BlockSpec((tm, GM_W), lambda i: (i, 0)),
        out_shape=S((t, GM_W), ACT_DTYPE), name=name, compiler_params=_cp("parallel"))(proj, proj, ln_g, ln_b, ws, bias)


def _gm_bwd(proj, dy, ln_g, ln_b, ws, wst, bias, name):
    t = proj.shape[0]
    tm = _row_tile(t, 512)
    nb = t // tm

    def body(pu_ref, pv_ref, dy_ref, g_ref, b_ref, ws_ref, wst_ref, bias_ref,
             d_ref, dws_ref, dbs_ref, dg_ref, db_ref, dbias_ref):
        first = pl.program_id(0) == 0
        lane = lax.broadcasted_iota(jnp.int32, (CH, 128), 1)
        dws = [jnp.zeros((CH, CH), F32) for _ in range(GM_HEADS)]
        dbias = jnp.zeros((CH, GM_W), F32)
        dg = jnp.zeros((1, GM_W), F32)
        db = jnp.zeros((1, GM_W), F32)
        for c in range(tm // CH):
            rows = slice(c * CH, (c + 1) * CH)
            pu = pu_ref[rows, :]
            pv = pv_ref[rows, :]
            u = _gelu(pu)
            o, r = _standardize(_gelu(pv))
            vn = o * g_ref[...] + b_ref[...]
            mixed = _gm_mixed(vn, ws_ref, bias_ref[...])
            dyv = dy_ref[rows, :]
            d_ref[rows, :GM_W] = (dyv * mixed * _gelu_grad(pu)).astype(d_ref.dtype)
            dmixed = dyv * u
            dbias = dbias + dmixed
            dvn_halves = []
            for hf in range(2):
                dm = dmixed[:, hf * 128:(hf + 1) * 128]
                vh = vn[:, hf * 128:(hf + 1) * 128]
                dm0 = jnp.where(lane < 64, dm, 0.0)
                dm1 = dm - dm0
                dws[2 * hf] = dws[2 * hf] + _dot_nt(dm0, vh)
                dws[2 * hf + 1] = dws[2 * hf + 1] + _dot_nt(dm1, vh)
                t0 = _dot(wst_ref[2 * hf], dm)
                t1 = _dot(wst_ref[2 * hf + 1], dm)
                dvn_halves.append(jnp.where(lane < 64, t0, t1))
            dvn = jnp.concatenate(dvn_halves, axis=1)
            dg = dg + jnp.sum(dvn * o, axis=0, keepdims=True)
            db = db + jnp.sum(dvn, axis=0, keepdims=True)
            dv = _standardize_bwd(dvn * g_ref[...], o, r)
            d_ref[rows, GM_W:] = (dv * _gelu_grad(pv)).astype(d_ref.dtype)
        for h in range(GM_HEADS):
            _acc_out(dws_ref.at[h], dws[h], first)
        _acc_out(dbias_ref, dbias, first)
        _acc_out(dg_ref, dg, first)
        _acc_out(db_ref, db, first)

        @pl.when(pl.program_id(0) == nb - 1)
        def _():
            tot = dbias_ref[...]
            head = lax.broadcasted_iota(jnp.int32, (CH, GM_W), 1) // (GM_W // GM_HEADS)
            out = jnp.zeros((CH, 128), F32)
            for h in range(GM_HEADS):
                s = jnp.sum(jnp.where(head == h, tot, 0.0), axis=1, keepdims=True)
                out = jnp.where(lane == h, s, out)
            dbs_ref[...] = out

    return pl.pallas_call(
        body, grid=(nb,),
        in_specs=[pl.BlockSpec((tm, GM_W), lambda i: (i, 0)), pl.BlockSpec((tm, GM_W), lambda i: (i, 1)),
                  pl.BlockSpec((tm, GM_W), lambda i: (i, 0)),
                  _full((1, GM_W)), _full((1, GM_W)), _full((GM_HEADS, CH, CH)), _full((GM_HEADS, CH, CH)),
                  _full((CH, GM_W))],
        out_specs=[pl.BlockSpec((tm, 2 * GM_W), lambda i: (i, 0)), _full((GM_HEADS, CH, CH)), _full((CH, 128)),
                   _full((1, GM_W)), _full((1, GM_W))],
        out_shape=[S((t, 2 * GM_W), ACT_DTYPE), S((GM_HEADS, CH, CH), F32), S((CH, 128), F32),
                   S((1, GM_W), F32), S((1, GM_W), F32)],
        scratch_shapes=[pltpu.VMEM((CH, GM_W), F32)],
        name=name, compiler_params=_cp("arbitrary"))(proj, proj, dy, ln_g, ln_b, ws, wst, bias)


def _rot(x, cos2, sin2):
    return x * cos2 + pltpu.roll(x, HD // 2, 1) * sin2


def _rot_bwd(dx, cos2, sin2):
    return dx * cos2 + pltpu.roll(dx * sin2, HD // 2, 1)


def _rotary(proj, cos2, sin2, name):
    t = proj.shape[0]
    tm = _row_tile(t, 512)
    scale = HD ** -0.5

    def body(q_ref, k_ref, c_ref, s_ref, rq_ref, rk_ref):
        c, s = c_ref[...], s_ref[...]
        for h in range(HEADS):
            cols = slice(h * HD, (h + 1) * HD)
            rq_ref[:, cols] = _rot(q_ref[:, cols], c, s)
            rk_ref[:, cols] = _rot(k_ref[:, cols], c, s) * scale

    return pl.pallas_call(
        body, grid=(t // tm,),
        in_specs=[pl.BlockSpec((tm, RET_W), lambda i: (i, 1)), pl.BlockSpec((tm, RET_W), lambda i: (i, 2)),
                  pl.BlockSpec((tm, HD), lambda i: (i, 0)), pl.BlockSpec((tm, HD), lambda i: (i, 0))],
        out_specs=[pl.BlockSpec((tm, RET_W), lambda i: (i, 0))] * 2,
        out_shape=[S((t, RET_W), F32)] * 2, name=name, compiler_params=_cp("parallel"))(proj, proj, cos2, sin2)


def _ret_scan(lhs, rhs, rhs_col, lp, ls, gp, gs, name):
    t = lhs.shape[0]
    n = t // CH
    r = 4 if n % 4 == 0 else 1
    ns = n // r

    def body(lp_ref, ls_ref, gp_ref, gs_ref, l1_ref, r1_ref, l2_ref, r2_ref, pre_ref, suf_ref, sp_ref, ss_ref):
        @pl.when(pl.program_id(0) == 0)
        def _():
            sp_ref[...] = jnp.zeros_like(sp_ref)
            ss_ref[...] = jnp.zeros_like(ss_ref)

        def kv(l_ref, r_ref, scale, rows):
            lv = l_ref[rows, :] * scale
            rv = r_ref[rows, :]
            return jnp.concatenate([_dot_tn(lv[:, h * HD:(h + 1) * HD], rv[:, h * HD:(h + 1) * HD])
                                    for h in range(HEADS)], axis=1)

        for j in range(r):
            pre_ref[j] = sp_ref[...]
            sp_ref[...] = sp_ref[...] * gp_ref[...] + kv(l1_ref, r1_ref, lp_ref[...], slice(j * CH, (j + 1) * CH))
        for j in reversed(range(r)):
            suf_ref[j] = ss_ref[...]
            ss_ref[...] = ss_ref[...] * gs_ref[...] + kv(l2_ref, r2_ref, ls_ref[...], slice(j * CH, (j + 1) * CH))

    w = HEADS * HD
    return pl.pallas_call(
        body, grid=(ns,),
        in_specs=[_full((CH, w)), _full((CH, w)), _full((1, w)), _full((1, w)),
                  pl.BlockSpec((r * CH, w), lambda s: (s, 0)), pl.BlockSpec((r * CH, w), lambda s: (s, rhs_col)),
                  pl.BlockSpec((r * CH, w), lambda s: (ns - 1 - s, 0)),
                  pl.BlockSpec((r * CH, w), lambda s: (ns - 1 - s, rhs_col))],
        out_specs=[pl.BlockSpec((r, HD, w), lambda s: (s, 0, 0)), pl.BlockSpec((r, HD, w), lambda s: (ns - 1 - s, 0, 0))],
        out_shape=[S((n, HD, w), F32)] * 2, name=name,
        scratch_shapes=[pltpu.VMEM((HD, w), F32), pltpu.VMEM((HD, w), F32)],
        compiler_params=_cp("arbitrary"))(lp, ls, gp, gs, lhs, rhs, lhs, rhs)


def _ret_out(rq, rk, proj, sf, sb, tb, name):
    t = rq.shape[0]
    r = 2 if (t // CH) % 2 == 0 else 1
    tm = r * CH
    w = HEADS * HD

    def body(rq_ref, rk_ref, v_ref, g_ref, sf_ref, sb_ref, dm_ref, xif_ref, xib_ref, a_ref, y_ref):
        for c in range(r):
            rows = slice(c * CH, (c + 1) * CH)
            for h in range(HEADS):
                cols = slice(h * HD, (h + 1) * HD)
                q = rq_ref[rows, cols]
                p = _dot_nt(q, rk_ref[rows, cols]) * dm_ref[h]
                a = (_dot(p, v_ref[rows, cols]) + _dot(q * xif_ref[:, cols], sf_ref[c, :, cols])
                     + _dot(q * xib_ref[:, cols], sb_ref[c, :, cols]))
                a_ref[rows, cols] = a
                o, _ = _standardize(a)
                gv = g_ref[rows, cols]
                y_ref[rows, cols] = (o * (gv * _sigmoid(gv))).astype(y_ref.dtype)

    return pl.pallas_call(
        body, grid=(t // tm,),
        in_specs=[pl.BlockSpec((tm, w), lambda i: (i, 0)), pl.BlockSpec((tm, w), lambda i: (i, 0)),
                  pl.BlockSpec((tm, w), lambda i: (i, 3)), pl.BlockSpec((tm, w), lambda i: (i, 4)),
                  pl.BlockSpec((r, HD, w), lambda i: (i, 0, 0)), pl.BlockSpec((r, HD, w), lambda i: (i, 0, 0)),
                  _full((HEADS, CH, CH)), _full((CH, w)), _full((CH, w))],
        out_specs=[pl.BlockSpec((tm, w), lambda i: (i, 0))] * 2,
        out_shape=[S((t, w), F32), S((t, w), ACT_DTYPE)], name=name,
        compiler_params=_cp("parallel"))(rq, rk, proj, proj, sf, sb, tb["dm"], tb["xif"], tb["xib"])


def _ret_bwd_pre(dy, a, proj, name):
    t = dy.shape[0]
    tm = _row_tile(t, 512)
    w = HEADS * HD

    def body(dy_ref, a_ref, g_ref, da_ref, dg_ref):
        for h in range(HEADS):
            cols = slice(h * HD, (h + 1) * HD)
            o, r = _standardize(a_ref[:, cols])
            gv = g_ref[:, cols]
            s = _sigmoid(gv)
            dyv = dy_ref[:, cols]
            dg_ref[:, cols] = (dyv * o * (s * (1.0 + gv * (1.0 - s)))).astype(dg_ref.dtype)
            da_ref[:, cols] = _standardize_bwd(dyv * (gv * s), o, r).astype(da_ref.dtype)

    return pl.pallas_call(
        body, grid=(t // tm,),
        in_specs=[pl.BlockSpec((tm, w), lambda i: (i, 0)), pl.BlockSpec((tm, w), lambda i: (i, 0)),
                  pl.BlockSpec((tm, w), lambda i: (i, 4))],
        out_specs=[pl.BlockSpec((tm, w), lambda i: (i, 0))] * 2,
        out_shape=[S((t, w), ACT_DTYPE)] * 2, name=name, compiler_params=_cp("parallel"))(dy, a, proj)


def _ret_bwd_main(rq, rk, proj, da, sf, sb, gf, gb, tb, name):
    t = rq.shape[0]
    r = 2 if (t // CH) % 2 == 0 else 1
    tm = r * CH
    w = HEADS * HD
    scale = HD ** -0.5

    def body(rq_ref, rk_ref, v_ref, da_ref, sf_ref, sb_ref, gf_ref, gb_ref, dm_ref, dmt_ref,
             xif_ref, xib_ref, zf_ref, zb_ref, c_ref, s_ref, o_ref):
        for c in range(r):
            rows = slice(c * CH, (c + 1) * CH)
            cos2, sin2 = c_ref[rows, :], s_ref[rows, :]
            for h in range(HEADS):
                cols = slice(h * HD, (h + 1) * HD)
                q, k, v, dav = rq_ref[rows, cols], rk_ref[rows, cols], v_ref[rows, cols], da_ref[rows, cols]
                qm, km, vm, dam = _mx(q), _mx(k), _mx(v), _mx(dav)
                dm, dmt = dm_ref[h], dmt_ref[h]
                pt = _dot_nt(km, qm) * dmt
                dp = _dot_nt(dam, vm) * dm
                dpt = _dot_nt(vm, dam) * dmt
                sfh, sbh, gfh, gbh = sf_ref[c, :, cols], sb_ref[c, :, cols], gf_ref[c, :, cols], gb_ref[c, :, cols]
                zf, zb = zf_ref[:, cols], zb_ref[:, cols]
                dv = _dot(pt, dam) + zf * _dot(km, gfh) + zb * _dot(km, gbh)
                drq = _dot(dp, km) + xif_ref[:, cols] * _dot_nt(dam, sfh) + xib_ref[:, cols] * _dot_nt(dam, sbh)
                drk = _dot(dpt, qm) + _dot_nt(zf * v, gfh) + _dot_nt(zb * v, gbh)
                o_ref[rows, h * HD:(h + 1) * HD] = _rot_bwd(drq, cos2, sin2).astype(o_ref.dtype)
                o_ref[rows, w + h * HD:w + (h + 1) * HD] = (_rot_bwd(drk, cos2, sin2) * scale).astype(o_ref.dtype)
                o_ref[rows, 2 * w + h * HD:2 * w + (h + 1) * HD] = dv.astype(o_ref.dtype)

    st = pl.BlockSpec((r, HD, w), lambda i: (i, 0, 0))
    return pl.pallas_call(
        body, grid=(t // tm,),
        in_specs=[pl.BlockSpec((tm, w), lambda i: (i, 0)), pl.BlockSpec((tm, w), lambda i: (i, 0)),
                  pl.BlockSpec((tm, w), lambda i: (i, 3)), pl.BlockSpec((tm, w), lambda i: (i, 0)),
                  st, st, st, st, _full((HEADS, CH, CH)), _full((HEADS, CH, CH)),
                  _full((CH, w)), _full((CH, w)), _full((CH, w)), _full((CH, w)),
                  pl.BlockSpec((tm, HD), lambda i: (i, 0)), pl.BlockSpec((tm, HD), lambda i: (i, 0))],
        out_specs=pl.BlockSpec((tm, 3 * w), lambda i: (i, 0)),
        out_shape=S((t, 3 * w), ACT_DTYPE), name=name,
        compiler_params=_cp("parallel"))(rq, rk, proj, da, sf, sb, gf, gb, tb["dm"], tb["dmt"],
                                         tb["xif"], tb["xib"], tb["zf"], tb["zb"], tb["cos2"], tb["sin2"])


CONV_TM = 256
CONV_SUB = 64
A_COL = (2 * GM_W + 4 * RET_W) // CV_W
G_COL = A_COL + 1


def _halo_specs(t, tm, col):
    nb16 = t // HALO
    per = tm // HALO
    return [pl.BlockSpec((tm, CV_W), lambda i: (i, col)),
            pl.BlockSpec((HALO, CV_W), lambda i: (jnp.maximum(i * per - 1, 0), col)),
            pl.BlockSpec((HALO, CV_W), lambda i: (jnp.minimum((i + 1) * per, nb16 - 1), col))]


def _fill_padded(dst_ref, prev, main, nxt, tm, i, nb):
    dst_ref[0:HALO, :] = jnp.where(i > 0, prev, 0.0)
    dst_ref[HALO:HALO + tm, :] = main
    dst_ref[HALO + tm:2 * HALO + tm, :] = jnp.where(i < nb - 1, nxt, 0.0)


def _conv_fwd(proj, cw, cb, ln_g, ln_b, name):
    t = proj.shape[0]
    tm = _row_tile(t, CONV_TM)
    nb = t // tm

    def body(a_ref, ap_ref, an_ref, g_ref, gp_ref, gn_ref, w_ref, b_ref, lg_ref, lb_ref, y_ref, hc_ref, hp_ref):
        i = pl.program_id(0)
        _fill_padded(hp_ref, ap_ref[...] * _sigmoid(gp_ref[...]), a_ref[...] * _sigmoid(g_ref[...]),
                     an_ref[...] * _sigmoid(gn_ref[...]), tm, i, nb)
        for sb in range(tm // CONV_SUB):
            acc = jnp.zeros((CONV_SUB, CV_W), F32) + b_ref[...]
            for k in range(KCONV):
                acc = acc + w_ref[k:k + 1, :] * hp_ref[pl.ds(sb * CONV_SUB + k + 1, CONV_SUB), :]
            rows = slice(sb * CONV_SUB, (sb + 1) * CONV_SUB)
            hc_ref[rows, :] = acc
            o, _ = _standardize(acc)
            z = o * lg_ref[...] + lb_ref[...]
            y_ref[rows, :] = (z * _sigmoid(z)).astype(y_ref.dtype)

    return pl.pallas_call(
        body, grid=(nb,),
        in_specs=_halo_specs(t, tm, A_COL) + _halo_specs(t, tm, G_COL)
        + [_full((32, CV_W)), _full((1, CV_W)), _full((1, CV_W)), _full((1, CV_W))],
        out_specs=[pl.BlockSpec((tm, CV_W), lambda i: (i, 0))] * 2,
        out_shape=[S((t, CV_W), ACT_DTYPE), S((t, CV_W), F32)], name=name,
        scratch_shapes=[pltpu.VMEM((tm + 2 * HALO, CV_W), F32)],
        compiler_params=_cp("parallel"))(proj, proj, proj, proj, proj, proj, cw, cb, ln_g, ln_b)


def _conv_bwd(proj, dy, hc, cw, ln_g, ln_b, name):
    t = proj.shape[0]
    tm = _row_tile(t, CONV_TM)
    nb = t // tm

    def body(a_ref, ap_ref, an_ref, g_ref, gp_ref, gn_ref, dy_ref, dyp_ref, dyn_ref, hc_ref, hcp_ref, hcn_ref,
             w_ref, lg_ref, lb_ref, d_ref, dw_ref, dcb_ref, dlg_ref, dlb_ref, hp_ref, dhp_ref, dwacc_ref):
        i = pl.program_id(0)
        first = i == 0

        def dhc_of(dyv, hcv):
            o, r = _standardize(hcv)
            z = o * lg_ref[...] + lb_ref[...]
            s = _sigmoid(z)
            dz = dyv * (s * (1.0 + z * (1.0 - s)))
            return _standardize_bwd(dz * lg_ref[...], o, r), dz, o

        dhc, dz, o = dhc_of(dy_ref[...], hc_ref[...])
        _acc_out(dlg_ref, jnp.sum(dz * o, axis=0, keepdims=True), first)
        _acc_out(dlb_ref, jnp.sum(dz, axis=0, keepdims=True), first)
        _acc_out(dcb_ref, jnp.sum(dhc, axis=0, keepdims=True), first)
        _fill_padded(dhp_ref, dhc_of(dyp_ref[...], hcp_ref[...])[0], dhc, dhc_of(dyn_ref[...], hcn_ref[...])[0],
                     tm, i, nb)
        _fill_padded(hp_ref, ap_ref[...] * _sigmoid(gp_ref[...]), a_ref[...] * _sigmoid(g_ref[...]),
                     an_ref[...] * _sigmoid(gn_ref[...]), tm, i, nb)

        @pl.when(first)
        def _():
            dwacc_ref[...] = jnp.zeros_like(dwacc_ref)

        for sb in range(tm // CONV_SUB):
            base = sb * CONV_SUB
            dmain = dhp_ref[pl.ds(HALO + base, CONV_SUB), :]
            dh = jnp.zeros((CONV_SUB, CV_W), F32)
            for k in range(KCONV):
                dh = dh + w_ref[k:k + 1, :] * dhp_ref[pl.ds(base + 2 * HALO - 1 - k, CONV_SUB), :]
                prod = dmain * hp_ref[pl.ds(base + k + 1, CONV_SUB), :]
                dwacc_ref[k * 8:(k + 1) * 8, :] += jnp.sum(prod.reshape(CONV_SUB // 8, 8, CV_W), axis=0)
            rows = slice(base, base + CONV_SUB)
            s = _sigmoid(g_ref[rows, :])
            d_ref[rows, :CV_W] = (dh * s).astype(d_ref.dtype)
            d_ref[rows, CV_W:] = (dh * a_ref[rows, :] * (s * (1.0 - s))).astype(d_ref.dtype)

        @pl.when(i == nb - 1)
        def _():
            for k in range(KCONV):
                dw_ref[k:k + 1, :] = jnp.sum(dwacc_ref[k * 8:(k + 1) * 8, :], axis=0, keepdims=True)
            dw_ref[KCONV:32, :] = jnp.zeros((32 - KCONV, CV_W), F32)

    hs = [pl.BlockSpec((tm, CV_W), lambda i: (i, 0)),
          pl.BlockSpec((HALO, CV_W), lambda i: (jnp.maximum(i * (tm // HALO) - 1, 0), 0)),
          pl.BlockSpec((HALO, CV_W), lambda i: (jnp.minimum((i + 1) * (tm // HALO), t // HALO - 1), 0))]
    return pl.pallas_call(
        body, grid=(nb,),
        in_specs=_halo_specs(t, tm, A_COL) + _halo_specs(t, tm, G_COL) + hs + hs
        + [_full((32, CV_W)), _full((1, CV_W)), _full((1, CV_W))],
        out_specs=[pl.BlockSpec((tm, 2 * CV_W), lambda i: (i, 0)), _full((32, CV_W)), _full((1, CV_W)),
                   _full((1, CV_W)), _full((1, CV_W))],
        out_shape=[S((t, 2 * CV_W), ACT_DTYPE), S((32, CV_W), F32), S((1, CV_W), F32), S((1, CV_W), F32),
                   S((1, CV_W), F32)],
        name=name,
        scratch_shapes=[pltpu.VMEM((tm + 2 * HALO, CV_W), F32), pltpu.VMEM((tm + 2 * HALO, CV_W), F32),
                        pltpu.VMEM((32 * 8, CV_W), F32)],
        compiler_params=_cp("arbitrary"))(proj, proj, proj, proj, proj, proj, dy, dy, dy, hc, hc, hc, cw, ln_g, ln_b)


def _loss_head(x, g, target, name):
    t = x.shape[0]
    tm = _row_tile(t, 512)

    def body(x_ref, g_ref, t_ref, dx_ref, dg_ref, l_ref):
        first = pl.program_id(0) == 0
        xv = x_ref[...]
        r = _rms_r(xv)
        e = xv * r * g_ref[...] - t_ref[...]
        dx, dgrow = _rms_bwd(e * (1.0 / D), xv, r, g_ref[...])
        dx_ref[...] = dx
        _acc_out(dg_ref, jnp.sum(dgrow, axis=0, keepdims=True), first)
        part = 0.5 * jnp.sum(jnp.mean(e * e, axis=-1, keepdims=True), axis=0, keepdims=True)
        _acc_out(l_ref, jnp.broadcast_to(part, (8, 128)), first)

    return pl.pallas_call(
        body, grid=(t // tm,),
        in_specs=[pl.BlockSpec((tm, D), lambda i: (i, 0)), _full((1, D)), pl.BlockSpec((tm, D), lambda i: (i, 0))],
        out_specs=[pl.BlockSpec((tm, D), lambda i: (i, 0)), _full((1, D)), _full((8, 128))],
        out_shape=[S((t, D), F32), S((1, D), F32), S((8, 128), F32)], name=name,
        compiler_params=_cp("arbitrary"))(x, g, target)


def _as2d(a):
    return a.reshape(-1, a.shape[-1])


def _ew_tile(rows, cols, n_arrays):
    budget = VMEM_LIMIT // 2
    tr = rows
    while tr * cols * 4 * n_arrays * 2 > budget and tr % 16 == 0:
        tr //= 2
    assert rows % tr == 0
    return tr


def _adamw(w, g, m, v, name):
    shape = w.shape
    w2, g2, m2, v2 = _as2d(w), _as2d(g), _as2d(m), _as2d(v)
    rows, cols = w2.shape
    tr = _ew_tile(rows, cols, 7)

    def body(w_ref, g_ref, m_ref, v_ref, d_ref, nm_ref, nv_ref):
        gv = g_ref[...]
        nm = ADAM_B1 * m_ref[...] + (1.0 - ADAM_B1) * gv
        nv = ADAM_B2 * v_ref[...] + (1.0 - ADAM_B2) * (gv * gv)
        m_hat = nm / (1.0 - ADAM_B1 ** ADAM_STEP)
        v_hat = nv / (1.0 - ADAM_B2 ** ADAM_STEP)
        d_ref[...] = -ADAM_LR * (m_hat / (jnp.sqrt(v_hat) + ADAM_EPS) + ADAM_WD * w_ref[...])
        nm_ref[...] = nm
        nv_ref[...] = nv

    spec = pl.BlockSpec((tr, cols), lambda i: (i, 0))
    outs = pl.pallas_call(body, grid=(rows // tr,), in_specs=[spec] * 4, out_specs=[spec] * 3,
                          out_shape=[S((rows, cols), F32)] * 3, name=name,
                          compiler_params=_cp("parallel"))(w2, g2, m2, v2)
    return tuple(o.reshape(shape) for o in outs)


BIG = (("w_in", "col"), ("w_out", "row"), ("w_ffn_in", "col"), ("w_ffn_out", "row"))
NBIG = len(BIG)


def _cast_to_gathered(w, l, me, name):
    _, r_, c_ = w.shape
    tr = _ew_tile(r_, c_, 2)

    def body(me_ref, w_ref, o_ref):
        o_ref[...] = w_ref[...].astype(o_ref.dtype)

    gs = pltpu.PrefetchScalarGridSpec(
        num_scalar_prefetch=1, grid=(r_ // tr,),
        in_specs=[pl.BlockSpec((None, tr, c_), lambda i, s: (l, i, 0))],
        out_specs=pl.BlockSpec((None, tr, c_), lambda i, s: (s[0], i, 0)))
    out = pl.pallas_call(body, grid_spec=gs, out_shape=S((N_CHIPS, r_, c_), MXU_DTYPE), name=name,
                         compiler_params=_cp("parallel"))(me.reshape(1), w)
    return out.reshape(N_CHIPS, 2, r_ // 2, c_)


def _all_gather(bufs, name):
    n = len(bufs)

    def body(*refs):
        i_refs, o_refs = refs[:n], refs[n:2 * n]
        isend, irecv, dsend, drecv = refs[2 * n:]
        pos = _mesh_pos()
        ici = _rider_copies("ici", i_refs, o_refs, isend, irecv, pos)
        d2d = _rider_copies("d2d", o_refs, o_refs, dsend, drecv, pos)
        for cp, _ in ici:
            cp.start()
        for (_, land), (fwd, _) in zip(ici, d2d):
            land.wait_recv()
            fwd.start()
        for _, land in d2d:
            land.wait_recv()
        for cp, _ in ici + d2d:
            cp.wait_send()

    return pl.pallas_call(
        body, in_specs=[ANY] * n, out_specs=[ANY] * n, out_shape=[S(a.shape, a.dtype) for a in bufs],
        input_output_aliases={w: w for w in range(n)}, name=name,
        scratch_shapes=[pltpu.SemaphoreType.DMA((n, 3))] * 4)(*bufs)


def _pair_exchange(grads, name):
    n = len(grads)

    def body(*refs):
        g_refs, theirs = refs[:n], refs[n:2 * n]
        send, recv = refs[2 * n:]
        x, y, c, *_ = _mesh_pos()
        cps = []
        for w in range(n):
            cp = pltpu.make_async_remote_copy(
                src_ref=g_refs[w].at[:, 1 - c], dst_ref=theirs[w], send_sem=send.at[w], recv_sem=recv.at[w],
                device_id=(x, y, 1 - c), device_id_type=MESH)
            cp.start()
            cps.append(cp)
        for cp in cps:
            cp.wait()

    return pl.pallas_call(
        body, in_specs=[ANY] * n, out_specs=[ANY] * n,
        out_shape=[S(a.shape[:1] + a.shape[2:], a.dtype) for a in grads], name=name,
        scratch_shapes=[pltpu.SemaphoreType.DMA((n,))] * 2)(*grads)


def _pair_sum(g, theirs, core, name):
    _, _, rh, c_ = g.shape
    tr = _ew_tile(rh, c_, 2)

    def body(s_ref, g_ref, t_ref, o_ref):
        o_ref[...] = (g_ref[...].astype(F32) + t_ref[...].astype(F32)).astype(o_ref.dtype)

    blk = pl.BlockSpec((None, tr, c_), lambda j, i, s: (j, i, 0))
    gs = pltpu.PrefetchScalarGridSpec(
        num_scalar_prefetch=1, grid=(N_CHIPS, rh // tr),
        in_specs=[pl.BlockSpec((None, None, tr, c_), lambda j, i, s: (j, s[0], i, 0)), blk], out_specs=blk)
    return pl.pallas_call(body, grid_spec=gs, out_shape=S(theirs.shape, theirs.dtype), name=name,
                          compiler_params=_cp("parallel", "parallel"))(core.reshape(1), g, theirs)


def _chip_scatter(qs, name):
    n = len(qs)

    def body(*refs):
        q_refs, got = refs[:n], refs[n:2 * n]
        send, recv = refs[2 * n:]
        copies = _rider_copies("scatter", q_refs, got, send, recv, _mesh_pos())
        for cp, _ in copies:
            cp.start()
        for cp, land in copies:
            land.wait_recv()
            cp.wait_send()

    return pl.pallas_call(
        body, in_specs=[ANY] * n, out_specs=[ANY] * n, out_shape=[_rider_out_shape("scatter", a) for a in qs],
        name=name, scratch_shapes=[pltpu.SemaphoreType.DMA((n, 3))] * 2)(*qs)


def _chip_sum(q, got, l, me, core, into, name):
    _, rh, c_ = got.shape
    tr = _ew_tile(rh, c_, 4)

    def body(s_ref, q_ref, g0_ref, g1_ref, g2_ref, o_ref):
        acc = q_ref[...].astype(F32)
        for r in (g0_ref, g1_ref, g2_ref):
            acc = acc + r[...].astype(F32)
        o_ref[...] = acc

    in_specs = [pl.BlockSpec((None, tr, c_), lambda i, s: (s[0], i, 0))] + [
        pl.BlockSpec((None, tr, c_), functools.partial(lambda k, i, s: (k, i, 0), k)) for k in range(3)]
    return _call_into(
        body, into, in_specs, [jnp.stack([me, core]), q, got, got, got], n_prefetch=1, grid=(rh // tr,),
        out_specs=pl.BlockSpec((None, None, tr, c_), lambda i, s: (l, s[1], i, 0)),
        out_shape=S((DEPTH, 2, rh, c_), F32), name=name, compiler_params=_cp("parallel"))


def _pair_gather(gs4):
    def body(*refs):
        i_refs, o_refs = refs[:NBIG], refs[NBIG:2 * NBIG]
        send, recv = refs[2 * NBIG:]
        x, y, c, *_ = _mesh_pos()
        cps = []
        for w in range(NBIG):
            cp = pltpu.make_async_remote_copy(
                src_ref=i_refs[w].at[:, c], dst_ref=o_refs[w].at[:, c], send_sem=send.at[w], recv_sem=recv.at[w],
                device_id=(x, y, 1 - c), device_id_type=MESH)
            cp.start()
            cps.append(cp)
        for cp in cps:
            cp.wait()

    outs = pl.pallas_call(
        body, in_specs=[ANY] * NBIG, out_specs=[ANY] * NBIG, out_shape=[S(a.shape, a.dtype) for a in gs4],
        input_output_aliases={w: w for w in range(NBIG)}, name="grad_pair_gather",
        scratch_shapes=[pltpu.SemaphoreType.DMA((NBIG,))] * 2)(*gs4)
    return [o.reshape(o.shape[0], 2 * o.shape[2], o.shape[3]) for o in outs]


def _all_reduce_small(p, name):
    rows = p.shape[0]

    def body(p_ref, o_ref, gath, send, recv):
        x, y, c = lax.axis_index("x"), lax.axis_index("y"), lax.axis_index("c")
        my_id = 4 * x + 2 * y + c
        gath[my_id] = p_ref[...]
        cps = []
        for r in range(1, N_DEV):
            bx, by, bc = (r >> 2) & 1, (r >> 1) & 1, r & 1
            tx, ty, tc = (1 - x if bx else x), (1 - y if by else y), (1 - c if bc else c)
            cp = pltpu.make_async_remote_copy(
                src_ref=p_ref, dst_ref=gath.at[my_id], send_sem=send.at[r - 1], recv_sem=recv.at[r - 1],
                device_id=(tx, ty, tc), device_id_type=MESH)
            cp.start()
            cps.append((cp, 4 * tx + 2 * ty + tc))
        for r, (cp, peer) in enumerate(cps):
            pltpu.make_async_remote_copy(
                src_ref=p_ref, dst_ref=gath.at[peer], send_sem=send.at[r], recv_sem=recv.at[r],
                device_id=(x, y, c), device_id_type=MESH).wait_recv()
        for cp, _ in cps:
            cp.wait_send()
        acc = gath[0]
        for s in range(1, N_DEV):
            acc = acc + gath[s]
        o_ref[...] = acc

    vm = pl.BlockSpec(memory_space=pltpu.VMEM)
    return pl.pallas_call(
        body, in_specs=[vm], out_specs=vm, out_shape=S((rows, 128), F32), name=name,
        scratch_shapes=[pltpu.VMEM((N_DEV, rows, 128), F32), pltpu.SemaphoreType.DMA((N_DEV - 1,)),
                        pltpu.SemaphoreType.DMA((N_DEV - 1,))],
        compiler_params=pltpu.CompilerParams(vmem_limit_bytes=VMEM_LIMIT))(p)


PACK_UNIT = 8 * 128


def _pack(arrs):
    parts = []
    for a in arrs:
        flat = a.reshape(-1)
        pad = (-flat.shape[0]) % PACK_UNIT
        parts.append(jnp.pad(flat, (0, pad)).reshape(-1, 128))
    return jnp.concatenate(parts, axis=0)


def _unpack(buf, shapes):
    outs, row = [], 0
    for shp in shapes:
        n = int(np.prod(shp))
        rows = -(-n // PACK_UNIT) * 8
        outs.append(buf[row:row + rows].reshape(-1)[:n].reshape(shp))
        row += rows
    return outs


SMALL = ("norm1_g", "gm_ln_g", "gm_ln_b", "gm_ws", "gm_bs", "conv_w", "conv_b", "conv_ln_g", "conv_ln_b",
         "norm2_g", "final_g")
WEIGHTS = ("norm1_g", "w_in", "gm_ln_g", "gm_ln_b", "gm_ws", "gm_bs", "conv_w", "conv_b", "conv_ln_g",
           "conv_ln_b", "w_out", "norm2_g", "w_ffn_in", "w_ffn_out", "final_g")


def kernel(x, norm1_g, w_in, gm_ln_g, gm_ln_b, gm_ws, gm_bs, conv_w, conv_b, conv_ln_g, conv_ln_b, w_out, norm2_g, w_ffn_in, w_ffn_out, final_g, loss_target, m_norm1_g, m_w_in, m_gm_ln_g, m_gm_ln_b, m_gm_ws, m_gm_bs, m_conv_w, m_conv_b, m_conv_ln_g, m_conv_ln_b, m_w_out, m_norm2_g, m_w_ffn_in, m_w_ffn_out, m_final_g, v_norm1_g, v_w_in, v_gm_ln_g, v_gm_ln_b, v_gm_ws, v_gm_bs, v_conv_w, v_conv_b, v_conv_ln_g, v_conv_ln_b, v_w_out, v_norm2_g, v_w_ffn_in, v_w_ffn_out, v_final_g):
    given = dict(locals())
    t = x.shape[1]
    xc = x.reshape(t, D)
    target = loss_target.reshape(t, D)
    me = 2 * lax.axis_index("x") + lax.axis_index("y")
    core = lax.axis_index("c")
    tb = _tables(t)

    me = me.astype(jnp.int32)
    core = core.astype(jnp.int32)
    names = [n for n, _ in BIG]
    kinds = dict(BIG)
    gathered = [{n: _cast_to_gathered(given[n], l, me, f"cast_{n}{l}") for n in names} for l in range(DEPTH)]
    gathered[0] = dict(zip(names, _all_gather([gathered[0][n] for n in names], "all_gather_weights0")))

    def weight(l, n):
        b = gathered[l][n]
        r_, c_ = 2 * b.shape[2], b.shape[3]
        return b.reshape(N_CHIPS, r_, c_) if kinds[n] == "col" else b.reshape(N_CHIPS * r_, c_)

    cshard = CV_W // N_CHIPS
    placed = lax.dynamic_update_slice(jnp.zeros((DEPTH, KCONV, CV_W), F32),
                                      conv_w * (core == 0).astype(F32), (0, 0, me * cshard))
    conv_w_full = _unpack(_all_reduce_small(_pack([placed]), "gather_conv_w"), [(DEPTH, KCONV, CV_W)])[0]
    cw32 = jnp.pad(conv_w_full, ((0, 0), (0, 32 - KCONV), (0, 0)))

    def row(a, l):
        return a[l].reshape(1, -1)

    saved = []
    early = ["w_in", "w_out", "w_ffn_in"]
    for l in range(DEPTH):
        nxt = gathered[l + 1] if l + 1 < DEPTH else None
        sv = {"x": xc}
        bias = jnp.repeat(gm_bs[l].T, GM_W // GM_HEADS, axis=1)
        proj, rid = _norm_mm(xc, row(norm1_g, l), weight(l, "w_in"), F32, f"in_proj{l}", 512,
                             [("d2d", [gathered[l]["w_ffn_out"]])] if l > 0 else ())
        if l > 0:
            gathered[l]["w_ffn_out"] = rid[0]
        y_gm = _gm_fwd(proj, row(gm_ln_g, l), row(gm_ln_b, l), gm_ws[l], bias, f"gm_fwd{l}")
        rq, rk = _rotary(proj, tb["cos2"], tb["sin2"], f"rotary{l}")
        sf, sb = _ret_scan(rk, proj, 3, tb["zf"], tb["zb"], tb["gcf"], tb["gcb"], f"ret_state{l}")
        a, y_ret = _ret_out(rq, rk, proj, sf, sb, tb, f"ret_out{l}")
        y_cv, hc = _conv_fwd(proj, cw32[l], row(conv_b, l), row(conv_ln_g, l), row(conv_ln_b, l), f"conv_fwd{l}")
        x_mid = _parts_mm_res([y_gm, y_ret, y_cv], weight(l, "w_out"), xc, f"out_proj{l}")
        ff, rid = _norm_mm(x_mid, row(norm2_g, l), weight(l, "w_ffn_in"), ACT_DTYPE, f"ffn_in{l}", 512,
                           [("ici", [nxt[n] for n in early])] if nxt else ())
        if nxt:
            nxt.update(zip(early, rid))
        xc, rid = _swiglu_mm_res(ff, weight(l, "w_ffn_out"), x_mid, f"ffn_out{l}",
                                 [("d2d", [nxt[n] for n in early]), ("ici", [nxt["w_ffn_out"]])] if nxt else ())
        if nxt:
            nxt.update(zip(early + ["w_ffn_out"], rid))
        sv.update(bias=bias, proj=proj, y_gm=y_gm, rq=rq, rk=rk, sf=sf, sb=sb, a=a, y_ret=y_ret, y_cv=y_cv,
                  hc=hc, x_mid=x_mid, ff=ff)
        saved.append(sv)

    dx, d_final_g, lpart = _loss_head(xc, final_g.reshape(1, D), target, "loss_head")
    loss = lax.psum(lpart[0, 0], ("x", "y", "c"))

    small_g = {n: [None] * DEPTH for n in SMALL}
    qs, got = [None] * DEPTH, [None] * DEPTH
    riding = None
    for l in reversed(range(DEPTH)):
        sv = saved[l]
        proj = sv["proj"]
        big_g = {}
        dff = _dx_swiglu(dx, weight(l, "w_ffn_out"), sv["ff"], f"ffn_out_dx{l}")
        big_g["w_ffn_out"] = _dw_swiglu(sv["ff"], dx, f"ffn_out_dw{l}")
        dx_mid, dg2 = _dx_norm([dff], weight(l, "w_ffn_in"), sv["x_mid"], row(norm2_g, l), dx, f"ffn_in_dx{l}", 256)
        big_g["w_ffn_in"], rid = _dw_norm_cols(sv["x_mid"], row(norm2_g, l), dff, w_ffn_in.shape[2],
                                               f"ffn_in_dw{l}", [("scatter", riding)] if riding else ())
        if riding:
            got[l + 1] = rid
        dy_gm, dy_ret, dy_cv = _dx_parts(dx_mid, weight(l, "w_out"), [GM_W, RET_W, CV_W], f"out_proj_dx{l}")
        big_g["w_out"] = _dw_parts([sv["y_gm"], sv["y_ret"], sv["y_cv"]], dx_mid, f"out_proj_dw{l}")
        d_cv, dcw, dcb, dclg, dclb = _conv_bwd(proj, dy_cv, sv["hc"], cw32[l], row(conv_ln_g, l),
                                               row(conv_ln_b, l), f"conv_bwd{l}")
        da, d_g = _ret_bwd_pre(dy_ret, sv["a"], proj, f"ret_bwd_pre{l}")
        gb_, gf_ = _ret_scan(sv["rq"], da, 0, tb["xib"], tb["xif"], tb["gcb"], tb["gcf"], f"ret_bwd_state{l}")
        d_qkv = _ret_bwd_main(sv["rq"], sv["rk"], proj, da, sv["sf"], sv["sb"], gf_, gb_, tb, f"ret_bwd_main{l}")
        d_gm, dws, dbs, dglg, dglb = _gm_bwd(proj, dy_gm, row(gm_ln_g, l), row(gm_ln_b, l), gm_ws[l],
                                             jnp.swapaxes(gm_ws[l], 1, 2), sv["bias"], f"gm_bwd{l}")
        dparts = [d_gm, d_qkv, d_g, d_cv]
        dx, dg1 = _dx_norm(dparts, weight(l, "w_in"), sv["x"], row(norm1_g, l), dx_mid, f"in_proj_dx{l}", 512)
        big_g["w_in"] = _dw_norm_parts(sv["x"], row(norm1_g, l), dparts, w_in.shape[2], f"in_proj_dw{l}")
        for n, val in (("norm1_g", dg1[0]), ("gm_ln_g", dglg[0]), ("gm_ln_b", dglb[0]), ("gm_ws", dws),
                       ("gm_bs", dbs[:, :GM_HEADS].T), ("conv_w", dcw[:KCONV]), ("conv_b", dcb[0]),
                       ("conv_ln_g", dclg[0]), ("conv_ln_b", dclb[0]), ("norm2_g", dg2[0])):
            small_g[n][l] = val
        g4 = [big_g[n].reshape(N_CHIPS, 2, given[n].shape[1] // 2, given[n].shape[2]) for n in names]
        theirs = _pair_exchange(g4, f"grad_pair_exchange{l}")
        qs[l] = [_pair_sum(g, th, core, f"pair_sum_{n}{l}") for n, g, th in zip(names, g4, theirs)]
        riding = qs[l] if l > 0 else None
    got[0] = _chip_scatter(qs[0], "grad_chip_scatter0")

    small_shapes = [given[n].shape if n != "conv_w" else (DEPTH, KCONV, CV_W) for n in SMALL]
    partials = [d_final_g[0] if n == "final_g" else jnp.stack(small_g[n]) for n in SMALL]
    reduced = dict(zip(SMALL, _unpack(_all_reduce_small(_pack(partials), "all_reduce_small_grads"), small_shapes)))
    reduced["conv_w"] = lax.dynamic_slice(reduced["conv_w"], (0, 0, me * cshard), (DEPTH, KCONV, cshard))

    halves = [None] * NBIG
    for l in reversed(range(DEPTH)):
        halves = [_chip_sum(q, g, l, me, core, h, f"chip_sum_{n}{l}")
                  for n, q, g, h in zip(names, qs[l], got[l], halves)]
    grads = dict(zip(names, _pair_gather(halves)))
    grads.update(reduced)

    delta, new_m, new_v = {}, {}, {}
    for n, _ in BIG:
        delta[n], new_m[n], new_v[n] = _adamw(given[n], grads[n], given["m_" + n], given["v_" + n], f"adamw_{n}")
    shapes = [given[n].shape for n in SMALL]
    packed = [_pack([src[n] if src is grads else src[p + n] for n in SMALL])
              for src, p in ((given, ""), (grads, ""), (given, "m_"), (given, "v_"))]
    outs = _adamw(*packed, "adamw_small")
    for dst, buf in zip((delta, new_m, new_v), outs):
        dst.update(zip(SMALL, _unpack(buf, shapes)))

    return (loss, dx.reshape(1, t, D), *[grads[n] for n in WEIGHTS], *[delta[n] for n in WEIGHTS],
            *[new_m[n] for n in WEIGHTS], *[new_v[n] for n in WEIGHTS])
```

```python
import functools
import math

import numpy as np
import jax
import jax.numpy as jnp
from jax import lax
from jax.experimental import pallas as pl
from jax.experimental.pallas import tpu as pltpu

F32 = jnp.float32
BF16 = jnp.bfloat16
MXU_DTYPE = BF16
ACT_DTYPE = BF16
S = jax.ShapeDtypeStruct

D = 1024
DEPTH = 2
GM_W = 256
GM_HEADS = 4
RET_W = 512
HEADS = 4
HD = 128
CV_W = 256
KCONV = 31
IN_W = 2 * GM_W + 4 * RET_W + 2 * CV_W
FFN_H = 2816
CH = 128
ROPE_BASE = 10000.0
EPS = 1e-6
N_CHIPS = 4
N_DEV = 8
HALO = 16

ADAM_LR = 0.001
ADAM_B1 = 0.9
ADAM_B2 = 0.999
ADAM_EPS = 1e-08
ADAM_WD = 0.01
ADAM_STEP = 10

VMEM_LIMIT = 52 * 1024 * 1024
MESH = pl.DeviceIdType.MESH


def _cp(*sem, vmem=VMEM_LIMIT):
    return pltpu.CompilerParams(dimension_semantics=tuple(sem), vmem_limit_bytes=vmem)


def _mx(a):
    return a.astype(MXU_DTYPE)


def _dot(a, b):
    return jnp.dot(_mx(a), _mx(b), preferred_element_type=F32)


def _dot_nt(a, b):
    return lax.dot_general(_mx(a), _mx(b), (((1,), (1,)), ((), ())), preferred_element_type=F32)


def _dot_tn(a, b):
    return lax.dot_general(_mx(a), _mx(b), (((0,), (0,)), ((), ())), preferred_element_type=F32)


def _sigmoid(x):
    return 1.0 / (1.0 + jnp.exp(-x))


def _gelu(x):
    return 0.5 * x * (1.0 + lax.erf(x * (1.0 / math.sqrt(2.0))))


def _gelu_grad(x):
    return 0.5 * (1.0 + lax.erf(x * (1.0 / math.sqrt(2.0)))) + x * jnp.exp(-0.5 * x * x) * (1.0 / math.sqrt(2.0 * math.pi))


def _rms_r(x):
    return lax.rsqrt(jnp.mean(x * x, axis=-1, keepdims=True) + EPS)


def _rms_bwd(dh, x, r, g):
    u = dh * g
    dx = r * u - x * (r * r * r) * jnp.mean(u * x, axis=-1, keepdims=True)
    return dx, dh * x * r


def _standardize(a):
    mu = jnp.mean(a, axis=-1, keepdims=True)
    d = a - mu
    r = lax.rsqrt(jnp.mean(d * d, axis=-1, keepdims=True) + EPS)
    return d * r, r


def _standardize_bwd(do, o, r):
    return r * (do - jnp.mean(do, axis=-1, keepdims=True) - o * jnp.mean(do * o, axis=-1, keepdims=True))


def _acc_out(ref, val, first):
    @pl.when(first)
    def _():
        ref[...] = val

    @pl.when(jnp.logical_not(first))
    def _():
        ref[...] += val


def _row_tile(t, pref):
    tm = min(t, pref)
    assert t % tm == 0, (t, tm)
    return tm


def _segments(part_widths, shard_w):
    bounds = {0}
    off = 0
    for w in part_widths:
        off += w
        bounds.add(off)
    total = off
    for j in range(1, total // shard_w + 1):
        bounds.add(j * shard_w)
    bounds = sorted(bounds)
    starts = np.cumsum([0] + list(part_widths))
    segs = []
    for a, b in zip(bounds[:-1], bounds[1:]):
        p = int(np.searchsorted(starts, a, side="right") - 1)
        segs.append((p, a - int(starts[p]), a // shard_w, a % shard_w, b - a))
    return segs


ANY = pl.BlockSpec(memory_space=pl.ANY)


def _mesh_pos():
    x, y, c = lax.axis_index("x"), lax.axis_index("y"), lax.axis_index("c")
    chips = [(1 - x, y), (x, 1 - y), (1 - x, 1 - y)]
    return x, y, c, 2 * x + y, chips, [2 * cx + cy for cx, cy in chips]


def _rider_copies(kind, i_refs, o_refs, send, recv, pos):
    x, y, c, me, chips, cj = pos
    out = []
    for b, (i_ref, o_ref) in enumerate(zip(i_refs, o_refs)):
        for k in range(3):
            if kind == "ici":
                src, dst, land, dev = i_ref.at[me, c], o_ref.at[me, c], o_ref.at[cj[k], c], (*chips[k], c)
            elif kind == "d2d":
                src, dst, land, dev = i_ref.at[cj[k], c], o_ref.at[cj[k], c], o_ref.at[cj[k], 1 - c], (x, y, 1 - c)
            else:
                src, dst, land, dev = i_ref.at[cj[k]], o_ref.at[k], o_ref.at[k], (*chips[k], c)
            out.append(tuple(pltpu.make_async_remote_copy(
                src_ref=s_, dst_ref=d_, send_sem=send.at[b, k], recv_sem=recv.at[b, k],
                device_id=dev, device_id_type=MESH) for s_, d_ in ((src, dst), (land, land))))
    return out


def _rider_out_shape(kind, a):
    return S((3,) + a.shape[1:], a.dtype) if kind == "scatter" else S(a.shape, a.dtype)


def _pcall(body, args, riders, *, grid, in_specs, out_specs, out_shape, name, sem, scratch_shapes=()):
    outs = list(out_shape)
    if not riders:
        res = pl.pallas_call(body, grid=grid, in_specs=in_specs, out_specs=out_specs, out_shape=outs, name=name,
                             scratch_shapes=list(scratch_shapes), compiler_params=_cp(*sem))(*args)
        return res, []
    r_in = [a for _, bufs in riders for a in bufs]
    r_out = [_rider_out_shape(kind, a) for kind, bufs in riders for a in bufs]
    n_in, n_out, n_scr, n_r = len(args), len(outs), len(scratch_shapes), len(r_in)
    aliases, idx = {}, 0
    for kind, bufs in riders:
        for _ in bufs:
            if kind != "scatter":
                aliases[n_in + idx] = n_out + idx
            idx += 1
    sems = [pltpu.SemaphoreType.DMA((len(bufs), 3)) for _, bufs in riders for _ in range(2)]

    def wrapped(*refs):
        a, ri = refs[:n_in], refs[n_in:n_in + n_r]
        o, ro = refs[n_in + n_r:n_in + n_r + n_out], refs[n_in + n_r + n_out:n_in + 2 * n_r + n_out]
        scr = refs[n_in + 2 * n_r + n_out:n_in + 2 * n_r + n_out + n_scr]
        sm = refs[n_in + 2 * n_r + n_out + n_scr:]
        pos = _mesh_pos()
        copies, off = [], 0
        for r, (kind, bufs) in enumerate(riders):
            copies += _rider_copies(kind, ri[off:off + len(bufs)], ro[off:off + len(bufs)], sm[2 * r], sm[2 * r + 1], pos)
            off += len(bufs)
        ids = [pl.program_id(d) for d in range(len(grid))]
        first = functools.reduce(jnp.logical_and, [i == 0 for i in ids])
        last = functools.reduce(jnp.logical_and, [i == n - 1 for i, n in zip(ids, grid)])

        @pl.when(first)
        def _():
            for cp, _ in copies:
                cp.start()

        body(*a, *o, *scr)

        @pl.when(last)
        def _():
            for cp, land in copies:
                land.wait_recv()
                cp.wait_send()

    res = pl.pallas_call(
        wrapped, grid=grid, in_specs=list(in_specs) + [ANY] * n_r, out_specs=list(out_specs) + [ANY] * n_r,
        out_shape=outs + r_out, input_output_aliases=aliases, name=name,
        scratch_shapes=list(scratch_shapes) + sems, compiler_params=_cp(*(("arbitrary",) * len(grid))))(*args, *r_in)
    return res[:n_out], res[n_out:]


def _wcol_spec(w):
    return pl.BlockSpec(w.shape, lambda *_: (0, 0, 0))


def _wrow_spec(w):
    return pl.BlockSpec(w.shape, lambda *_: (0, 0))


def _norm_mm(x, g, w, out_dtype, name, tm_pref, riders=()):
    t = x.shape[0]
    nc = w.shape[2]
    tm = _row_tile(t, tm_pref)

    def body(x_ref, g_ref, w_ref, o_ref):
        xv = x_ref[...]
        h = _mx(xv * _rms_r(xv) * g_ref[...])
        for j in range(N_CHIPS):
            o_ref[:, j * nc:(j + 1) * nc] = jnp.dot(h, w_ref[j], preferred_element_type=F32).astype(o_ref.dtype)

    (out,), rid = _pcall(
        body, [x, g, w], riders, grid=(t // tm,),
        in_specs=[pl.BlockSpec((tm, D), lambda i: (i, 0)), pl.BlockSpec((1, D), lambda i: (0, 0)), _wcol_spec(w)],
        out_specs=[pl.BlockSpec((tm, N_CHIPS * nc), lambda i: (i, 0))],
        out_shape=[S((t, N_CHIPS * nc), out_dtype)], name=name, sem=("parallel",))
    return out, rid


def _parts_mm_res(parts, w, res, name):
    t = res.shape[0]
    tm = _row_tile(t, 512)
    widths = [p.shape[1] for p in parts]
    offs = np.cumsum([0] + widths)
    n = len(parts)

    def body(*refs):
        p_refs, w_ref, r_ref, o_ref = refs[:n], refs[n], refs[n + 1], refs[n + 2]
        acc = r_ref[...]
        for p in range(n):
            acc = acc + _dot(p_refs[p][...], w_ref[int(offs[p]):int(offs[p + 1]), :])
        o_ref[...] = acc

    return pl.pallas_call(
        body, grid=(t // tm,),
        in_specs=[pl.BlockSpec((tm, wd), lambda i: (i, 0)) for wd in widths]
        + [_wrow_spec(w), pl.BlockSpec((tm, D), lambda i: (i, 0))],
        out_specs=pl.BlockSpec((tm, D), lambda i: (i, 0)),
        out_shape=S((t, D), F32), name=name, compiler_params=_cp("parallel"))(*parts, w, res)


def _swiglu(ff):
    gate = ff[:, :FFN_H].astype(F32)
    up = ff[:, FFN_H:].astype(F32)
    return gate * _sigmoid(gate) * up


def _swiglu_mm_res(ff, w, res, name, riders=()):
    t = res.shape[0]
    tm = _row_tile(t, 512)

    def body(f_ref, w_ref, r_ref, o_ref):
        o_ref[...] = r_ref[...] + _dot(_swiglu(f_ref[...]), w_ref[...])

    (out,), rid = _pcall(
        body, [ff, w, res], riders, grid=(t // tm,),
        in_specs=[pl.BlockSpec((tm, 2 * FFN_H), lambda i: (i, 0)), _wrow_spec(w),
                  pl.BlockSpec((tm, D), lambda i: (i, 0))],
        out_specs=[pl.BlockSpec((tm, D), lambda i: (i, 0))],
        out_shape=[S((t, D), F32)], name=name, sem=("parallel",))
    return out, rid


def _dx_norm(dparts, w, x, g, dres, name, tm_pref):
    t = x.shape[0]
    nc = w.shape[2]
    tm = _row_tile(t, tm_pref)
    widths = [p.shape[1] for p in dparts]
    segs = _segments(widths, nc)
    n = len(dparts)

    def body(*refs):
        d_refs = refs[:n]
        w_ref, x_ref, g_ref, r_ref, dx_ref, dg_ref = refs[n:]
        dh = jnp.zeros((tm, D), F32)
        for (p, po, j, jo, wd) in segs:
            dh = dh + _dot_nt(d_refs[p][:, po:po + wd], w_ref[j, :, jo:jo + wd])
        xv = x_ref[...]
        dx, dgrow = _rms_bwd(dh, xv, _rms_r(xv), g_ref[...])
        dx_ref[...] = r_ref[...] + dx
        _acc_out(dg_ref, jnp.sum(dgrow, axis=0, keepdims=True), pl.program_id(0) == 0)

    return pl.pallas_call(
        body, grid=(t // tm,),
        in_specs=[pl.BlockSpec((tm, wd), lambda i: (i, 0)) for wd in widths]
        + [_wcol_spec(w), pl.BlockSpec((tm, D), lambda i: (i, 0)),
           pl.BlockSpec((1, D), lambda i: (0, 0)), pl.BlockSpec((tm, D), lambda i: (i, 0))],
        out_specs=[pl.BlockSpec((tm, D), lambda i: (i, 0)), pl.BlockSpec((1, D), lambda i: (0, 0))],
        out_shape=[S((t, D), F32), S((1, D), F32)], name=name,
        compiler_params=_cp("arbitrary"))(*dparts, w, x, g, dres)


def _dx_parts(dy, w, widths, name):
    t = dy.shape[0]
    tm = _row_tile(t, 512)
    offs = np.cumsum([0] + list(widths))
    n = len(widths)

    def body(dy_ref, w_ref, *o_refs):
        dyv = _mx(dy_ref[...])
        for p in range(n):
            o_refs[p][...] = _dot_nt(dyv, w_ref[int(offs[p]):int(offs[p + 1]), :])

    return pl.pallas_call(
        body, grid=(t // tm,),
        in_specs=[pl.BlockSpec((tm, D), lambda i: (i, 0)), _wrow_spec(w)],
        out_specs=[pl.BlockSpec((tm, wd), lambda i: (i, 0)) for wd in widths],
        out_shape=[S((t, wd), F32) for wd in widths], name=name, compiler_params=_cp("parallel"))(dy, w)


def _dx_swiglu(dy, w, ff, name):
    t = dy.shape[0]
    tm = _row_tile(t, 256)

    def body(dy_ref, w_ref, f_ref, o_ref):
        dact = _dot_nt(dy_ref[...], w_ref[...])
        gate = f_ref[:, :FFN_H].astype(F32)
        up = f_ref[:, FFN_H:].astype(F32)
        s = _sigmoid(gate)
        o_ref[:, :FFN_H] = (dact * up * (s * (1.0 + gate * (1.0 - s)))).astype(o_ref.dtype)
        o_ref[:, FFN_H:] = (dact * (gate * s)).astype(o_ref.dtype)

    return pl.pallas_call(
        body, grid=(t // tm,),
        in_specs=[pl.BlockSpec((tm, D), lambda i: (i, 0)), _wrow_spec(w),
                  pl.BlockSpec((tm, 2 * FFN_H), lambda i: (i, 0))],
        out_specs=pl.BlockSpec((tm, 2 * FFN_H), lambda i: (i, 0)),
        out_shape=S((t, 2 * FFN_H), ACT_DTYPE), name=name, compiler_params=_cp("parallel"))(dy, w, ff)


def _call_into(body, into, in_specs, args, *, n_prefetch, grid, out_specs, **kw):
    n_in = len(args)
    if into is None:
        gs = pltpu.PrefetchScalarGridSpec(num_scalar_prefetch=n_prefetch, grid=grid, in_specs=in_specs,
                                          out_specs=out_specs)
        return pl.pallas_call(body, grid_spec=gs, **kw)(*args)

    def wrapped(*refs):
        return body(*refs[:n_in], *refs[n_in + 1:])

    gs = pltpu.PrefetchScalarGridSpec(num_scalar_prefetch=n_prefetch, grid=grid,
                                      in_specs=list(in_specs) + [ANY], out_specs=out_specs)
    return pl.pallas_call(wrapped, grid_spec=gs, input_output_aliases={n_in: 0}, **kw)(*args, into)


def _dw_norm_parts(x, g, dparts, nc, name):
    t = x.shape[0]
    tk = _row_tile(t, 1024)
    widths = [p.shape[1] for p in dparts]
    segs = _segments(widths, nc)
    n = len(dparts)
    nk = t // tk

    def body(*refs):
        x_ref, g_ref = refs[0], refs[1]
        d_refs = refs[2:2 + n]
        o_ref, acc_ref = refs[2 + n], refs[3 + n]
        k = pl.program_id(0)
        xv = x_ref[...]
        h = _mx(xv * _rms_r(xv) * g_ref[...])

        @pl.when(k == 0)
        def _():
            acc_ref[...] = jnp.zeros_like(acc_ref)

        for (p, po, j, jo, wd) in segs:
            acc_ref[j, :, jo:jo + wd] += _dot_tn(h, d_refs[p][:, po:po + wd])

        @pl.when(k == nk - 1)
        def _():
            o_ref[...] = acc_ref[...].astype(o_ref.dtype)

    return pl.pallas_call(
        body, grid=(nk,),
        in_specs=[pl.BlockSpec((tk, D), lambda k: (k, 0)), pl.BlockSpec((1, D), lambda k: (0, 0))]
        + [pl.BlockSpec((tk, wd), lambda k: (k, 0)) for wd in widths],
        out_specs=pl.BlockSpec((N_CHIPS, D, nc), lambda k: (0, 0, 0)),
        out_shape=S((N_CHIPS, D, nc), MXU_DTYPE), name=name,
        scratch_shapes=[pltpu.VMEM((N_CHIPS, D, nc), F32)], compiler_params=_cp("arbitrary"))(x, g, *dparts)


def _dw_norm_cols(x, g, dy, nc, name, riders=()):
    t = x.shape[0]
    tk = _row_tile(t, 1024)
    nk = t // tk

    def body(x_ref, g_ref, dy_ref, o_ref, acc_ref):
        k = pl.program_id(1)
        xv = x_ref[...]
        h = _mx(xv * _rms_r(xv) * g_ref[...])

        @pl.when(k == 0)
        def _():
            acc_ref[...] = jnp.zeros_like(acc_ref)

        acc_ref[...] += _dot_tn(h, dy_ref[...])

        @pl.when(k == nk - 1)
        def _():
            o_ref[...] = acc_ref[...].astype(o_ref.dtype)

    (out,), rid = _pcall(
        body, [x, g, dy], riders, grid=(N_CHIPS, nk),
        in_specs=[pl.BlockSpec((tk, D), lambda j, k: (k, 0)), pl.BlockSpec((1, D), lambda j, k: (0, 0)),
                  pl.BlockSpec((tk, nc), lambda j, k: (k, j))],
        out_specs=[pl.BlockSpec((None, D, nc), lambda j, k: (j, 0, 0))],
        out_shape=[S((N_CHIPS, D, nc), MXU_DTYPE)], name=name, sem=("parallel", "arbitrary"),
        scratch_shapes=[pltpu.VMEM((D, nc), F32)])
    return out, rid


def _dw_parts(parts, dy, name):
    t = dy.shape[0]
    tk = _row_tile(t, 1024)
    widths = [p.shape[1] for p in parts]
    offs = np.cumsum([0] + widths)
    ktot = int(offs[-1])
    n = len(parts)
    nk = t // tk

    def body(*refs):
        p_refs, dy_ref, o_ref, acc_ref = refs[:n], refs[n], refs[n + 1], refs[n + 2]
        k = pl.program_id(0)

        @pl.when(k == 0)
        def _():
            acc_ref[...] = jnp.zeros_like(acc_ref)

        dyv = _mx(dy_ref[...])
        for p in range(n):
            acc_ref[int(offs[p]):int(offs[p + 1]), :] += _dot_tn(p_refs[p][...], dyv)

        @pl.when(k == nk - 1)
        def _():
            o_ref[...] = acc_ref[...].astype(o_ref.dtype)

    return pl.pallas_call(
        body, grid=(nk,),
        in_specs=[pl.BlockSpec((tk, wd), lambda k: (k, 0)) for wd in widths]
        + [pl.BlockSpec((tk, D), lambda k: (k, 0))],
        out_specs=pl.BlockSpec((ktot, D), lambda k: (0, 0)),
        out_shape=S((ktot, D), MXU_DTYPE), name=name,
        scratch_shapes=[pltpu.VMEM((ktot, D), F32)], compiler_params=_cp("arbitrary"))(*parts, dy)


def _dw_swiglu(ff, dy, name):
    t = dy.shape[0]
    tk = _row_tile(t, 512)
    nk = t // tk

    def body(f_ref, dy_ref, o_ref, acc_ref):
        k = pl.program_id(0)

        @pl.when(k == 0)
        def _():
            acc_ref[...] = jnp.zeros_like(acc_ref)

        acc_ref[...] += _dot_tn(_swiglu(f_ref[...]), dy_ref[...])

        @pl.when(k == nk - 1)
        def _():
            o_ref[...] = acc_ref[...].astype(o_ref.dtype)

    return pl.pallas_call(
        body, grid=(nk,),
        in_specs=[pl.BlockSpec((tk, 2 * FFN_H), lambda k: (k, 0)), pl.BlockSpec((tk, D), lambda k: (k, 0))],
        out_specs=pl.BlockSpec((FFN_H, D), lambda k: (0, 0)),
        out_shape=S((FFN_H, D), MXU_DTYPE), name=name,
        scratch_shapes=[pltpu.VMEM((FFN_H, D), F32)], compiler_params=_cp("arbitrary"))(ff, dy)


def _tables(t):
    pos = jnp.arange(t, dtype=F32)
    half = HD // 2
    inv_freq = ROPE_BASE ** (-jnp.arange(half, dtype=F32) / half)
    ang = pos[:, None] * inv_freq[None, :]
    cos, sin = jnp.cos(ang), jnp.sin(ang)
    tb = {"cos2": jnp.concatenate([cos, cos], axis=1), "sin2": jnp.concatenate([-sin, sin], axis=1)}
    gf = 1.0 - jnp.exp2(-5.0 - jnp.arange(HEADS, dtype=F32))
    lgf = jnp.log(gf)[:, None]
    lgb = jnp.log(gf[::-1])[:, None]
    idx = jnp.arange(CH, dtype=F32)
    diff = idx[:, None] - idx[None, :]
    dfwd = jnp.where(diff >= 0, jnp.exp(lgf[:, :, None] * jnp.where(diff >= 0, diff, 0.0)), 0.0)
    dbwd = jnp.where(diff < 0, jnp.exp(lgb[:, :, None] * jnp.where(diff < 0, -diff, 0.0)), 0.0)
    tb["dm"] = dfwd + dbwd
    tb["dmt"] = jnp.swapaxes(tb["dm"], 1, 2)

    def lanes(a):
        return jnp.repeat(a.T, HD, axis=1)

    tb["xif"] = lanes(jnp.exp(lgf * (idx + 1)))
    tb["zf"] = lanes(jnp.exp(lgf * (CH - 1 - idx)))
    tb["xib"] = lanes(jnp.exp(lgb * (CH - idx)))
    tb["zb"] = lanes(jnp.exp(lgb * idx))
    tb["gcf"] = jnp.repeat(jnp.exp(lgf * CH), HD, axis=0).reshape(1, HEADS * HD)
    tb["gcb"] = jnp.repeat(jnp.exp(lgb * CH), HD, axis=0).reshape(1, HEADS * HD)
    return tb


def _full(shape):
    nd = len(shape)
    return pl.BlockSpec(shape, lambda *_: (0,) * nd)


def _gm_mixed(vn, ws_ref, bias):
    lane = lax.broadcasted_iota(jnp.int32, (CH, 128), 1)
    halves = []
    for hf in range(2):
        vh = _mx(vn[:, hf * 128:(hf + 1) * 128])
        r0 = jnp.dot(_mx(ws_ref[2 * hf]), vh, preferred_element_type=F32)
        r1 = jnp.dot(_mx(ws_ref[2 * hf + 1]), vh, preferred_element_type=F32)
        halves.append(jnp.where(lane < 64, r0, r1))
    return jnp.concatenate(halves, axis=1) + bias


def _gm_fwd(proj, ln_g, ln_b, ws, bias, name, riders=()):
    t = proj.shape[0]
    tm = _row_tile(t, 512)

    def body(pu_ref, pv_ref, g_ref, b_ref, ws_ref, bias_ref, o_ref):
        for c in range(tm // CH):
            rows = slice(c * CH, (c + 1) * CH)
            u = _gelu(pu_ref[rows, :])
            o, _ = _standardize(_gelu(pv_ref[rows, :]))
            vn = o * g_ref[...] + b_ref[...]
            o_ref[rows, :] = (u * _gm_mixed(vn, ws_ref, bias_ref[...])).astype(o_ref.dtype)

    (out,), rid = _pcall(
        body, [proj, proj, ln_g, ln_b, ws, bias], riders, grid=(t // tm,),
        in_specs=[pl.BlockSpec((tm, GM_W), lambda i: (i, 0)), pl.BlockSpec((tm, GM_W), lambda i: (i, 1)),
                  _full((1, GM_W)), _full((1, GM_W)), _full((GM_HEADS, CH, CH)), _full((CH, GM_W))],
        out_specs=[pl.BlockSpec((tm, GM_W), lambda i: (i, 0))],
        out_shape=[S((t, GM_W), ACT_DTYPE)], name=name, sem=("parallel",))
    return out, rid


def _gm_bwd(proj, dy, ln_g, ln_b, ws, wst, bias, name):
    t = proj.shape[0]
    tm = _row_tile(t, 512)
    nb = t // tm

    def body(pu_ref, pv_ref, dy_ref, g_ref, b_ref, ws_ref, wst_ref, bias_ref,
             d_ref, dws_ref, dbs_ref, dg_ref, db_ref, dbias_ref):
        first = pl.program_id(0) == 0
        lane = lax.broadcasted_iota(jnp.int32, (CH, 128), 1)
        dws = [jnp.zeros((CH, CH), F32) for _ in range(GM_HEADS)]
        dbias = jnp.zeros((CH, GM_W), F32)
        dg = jnp.zeros((1, GM_W), F32)
        db = jnp.zeros((1, GM_W), F32)
        for c in range(tm // CH):
            rows = slice(c * CH, (c + 1) * CH)
            pu = pu_ref[rows, :]
            pv = pv_ref[rows, :]
            u = _gelu(pu)
            o, r = _standardize(_gelu(pv))
            vn = o * g_ref[...] + b_ref[...]
            mixed = _gm_mixed(vn, ws_ref, bias_ref[...])
            dyv = dy_ref[rows, :]
            d_ref[rows, :GM_W] = (dyv * mixed * _gelu_grad(pu)).astype(d_ref.dtype)
            dmixed = dyv * u
            dbias = dbias + dmixed
            dvn_halves = []
            for hf in range(2):
                dm = dmixed[:, hf * 128:(hf + 1) * 128]
                vh = vn[:, hf * 128:(hf + 1) * 128]
                dm0 = jnp.where(lane < 64, dm, 0.0)
                dm1 = dm - dm0
                dws[2 * hf] = dws[2 * hf] + _dot_nt(dm0, vh)
                dws[2 * hf + 1] = dws[2 * hf + 1] + _dot_nt(dm1, vh)
                t0 = _dot(wst_ref[2 * hf], dm)
                t1 = _dot(wst_ref[2 * hf + 1], dm)
                dvn_halves.append(jnp.where(lane < 64, t0, t1))
            dvn = jnp.concatenate(dvn_halves, axis=1)
            dg = dg + jnp.sum(dvn * o, axis=0, keepdims=True)
            db = db + jnp.sum(dvn, axis=0, keepdims=True)
            dv = _standardize_bwd(dvn * g_ref[...], o, r)
            d_ref[rows, GM_W:] = (dv * _gelu_grad(pv)).astype(d_ref.dtype)
        for h in range(GM_HEADS):
            _acc_out(dws_ref.at[h], dws[h], first)
        _acc_out(dbias_ref, dbias, first)
        _acc_out(dg_ref, dg, first)
        _acc_out(db_ref, db, first)

        @pl.when(pl.program_id(0) == nb - 1)
        def _():
            tot = dbias_ref[...]
            head = lax.broadcasted_iota(jnp.int32, (CH, GM_W), 1) // (GM_W // GM_HEADS)
            out = jnp.zeros((CH, 128), F32)
            for h in range(GM_HEADS):
                s = jnp.sum(jnp.where(head == h, tot, 0.0), axis=1, keepdims=True)
                out = jnp.where(lane == h, s, out)
            dbs_ref[...] = out

    return pl.pallas_call(
        body, grid=(nb,),
        in_specs=[pl.BlockSpec((tm, GM_W), lambda i: (i, 0)), pl.BlockSpec((tm, GM_W), lambda i: (i, 1)),
                  pl.BlockSpec((tm, GM_W), lambda i: (i, 0)),
                  _full((1, GM_W)), _full((1, GM_W)), _full((GM_HEADS, CH, CH)), _full((GM_HEADS, CH, CH)),
                  _full((CH, GM_W))],
        out_specs=[pl.BlockSpec((tm, 2 * GM_W), lambda i: (i, 0)), _full((GM_HEADS, CH, CH)), _full((CH, 128)),
                   _full((1, GM_W)), _full((1, GM_W))],
        out_shape=[S((t, 2 * GM_W), ACT_DTYPE), S((GM_HEADS, CH, CH), F32), S((CH, 128), F32),
                   S((1, GM_W), F32), S((1, GM_W), F32)],
        scratch_shapes=[pltpu.VMEM((CH, GM_W), F32)],
        name=name, compiler_params=_cp("arbitrary"))(proj, proj, dy, ln_g, ln_b, ws, wst, bias)


def _rot(x, cos2, sin2):
    return x * cos2 + pltpu.roll(x, HD // 2, 1) * sin2


def _rot_bwd(dx, cos2, sin2):
    return dx * cos2 + pltpu.roll(dx * sin2, HD // 2, 1)


def _rotary(proj, cos2, sin2, name):
    t = proj.shape[0]
    tm = _row_tile(t, 512)
    scale = HD ** -0.5

    def body(q_ref, k_ref, c_ref, s_ref, rq_ref, rk_ref):
        c, s = c_ref[...], s_ref[...]
        for h in range(HEADS):
            cols = slice(h * HD, (h + 1) * HD)
            rq_ref[:, cols] = _rot(q_ref[:, cols], c, s)
            rk_ref[:, cols] = _rot(k_ref[:, cols], c, s) * scale

    return pl.pallas_call(
        body, grid=(t // tm,),
        in_specs=[pl.BlockSpec((tm, RET_W), lambda i: (i, 1)), pl.BlockSpec((tm, RET_W), lambda i: (i, 2)),
                  pl.BlockSpec((tm, HD), lambda i: (i, 0)), pl.BlockSpec((tm, HD), lambda i: (i, 0))],
        out_specs=[pl.BlockSpec((tm, RET_W), lambda i: (i, 0))] * 2,
        out_shape=[S((t, RET_W), F32)] * 2, name=name, compiler_params=_cp("parallel"))(proj, proj, cos2, sin2)


def _ret_scan(lhs, rhs, rhs_col, lp, ls, gp, gs, name):
    t = lhs.shape[0]
    n = t // CH
    r = 4 if n % 4 == 0 else 1
    ns = n // r

    def body(lp_ref, ls_ref, gp_ref, gs_ref, l1_ref, r1_ref, l2_ref, r2_ref, pre_ref, suf_ref, sp_ref, ss_ref):
        @pl.when(pl.program_id(0) == 0)
        def _():
            sp_ref[...] = jnp.zeros_like(sp_ref)
            ss_ref[...] = jnp.zeros_like(ss_ref)

        def kv(l_ref, r_ref, scale, rows):
            lv = l_ref[rows, :] * scale
            rv = r_ref[rows, :]
            return jnp.concatenate([_dot_tn(lv[:, h * HD:(h + 1) * HD], rv[:, h * HD:(h + 1) * HD])
                                    for h in range(HEADS)], axis=1)

        for j in range(r):
            pre_ref[j] = sp_ref[...]
            sp_ref[...] = sp_ref[...] * gp_ref[...] + kv(l1_ref, r1_ref, lp_ref[...], slice(j * CH, (j + 1) * CH))
        for j in reversed(range(r)):
            suf_ref[j] = ss_ref[...]
            ss_ref[...] = ss_ref[...] * gs_ref[...] + kv(l2_ref, r2_ref, ls_ref[...], slice(j * CH, (j + 1) * CH))

    w = HEADS * HD
    return pl.pallas_call(
        body, grid=(ns,),
        in_specs=[_full((CH, w)), _full((CH, w)), _full((1, w)), _full((1, w)),
                  pl.BlockSpec((r * CH, w), lambda s: (s, 0)), pl.BlockSpec((r * CH, w), lambda s: (s, rhs_col)),
                  pl.BlockSpec((r * CH, w), lambda s: (ns - 1 - s, 0)),
                  pl.BlockSpec((r * CH, w), lambda s: (ns - 1 - s, rhs_col))],
        out_specs=[pl.BlockSpec((r, HD, w), lambda s: (s, 0, 0)), pl.BlockSpec((r, HD, w), lambda s: (ns - 1 - s, 0, 0))],
        out_shape=[S((n, HD, w), F32)] * 2, name=name,
        scratch_shapes=[pltpu.VMEM((HD, w), F32), pltpu.VMEM((HD, w), F32)],
        compiler_params=_cp("arbitrary"))(lp, ls, gp, gs, lhs, rhs, lhs, rhs)


def _ret_out(rq, rk, proj, sf, sb, tb, name, riders=()):
    t = rq.shape[0]
    r = 2 if (t // CH) % 2 == 0 else 1
    tm = r * CH
    w = HEADS * HD

    def body(rq_ref, rk_ref, v_ref, g_ref, sf_ref, sb_ref, dm_ref, xif_ref, xib_ref, a_ref, y_ref):
        for c in range(r):
            rows = slice(c * CH, (c + 1) * CH)
            for h in range(HEADS):
                cols = slice(h * HD, (h + 1) * HD)
                q = rq_ref[rows, cols]
                p = _dot_nt(q, rk_ref[rows, cols]) * dm_ref[h]
                a = (_dot(p, v_ref[rows, cols]) + _dot(q * xif_ref[:, cols], sf_ref[c, :, cols])
                     + _dot(q * xib_ref[:, cols], sb_ref[c, :, cols]))
                a_ref[rows, cols] = a
                o, _ = _standardize(a)
                gv = g_ref[rows, cols]
                y_ref[rows, cols] = (o * (gv * _sigmoid(gv))).astype(y_ref.dtype)

    (a, y), rid = _pcall(
        body, [rq, rk, proj, proj, sf, sb, tb["dm"], tb["xif"], tb["xib"]], riders, grid=(t // tm,),
        in_specs=[pl.BlockSpec((tm, w), lambda i: (i, 0)), pl.BlockSpec((tm, w), lambda i: (i, 0)),
                  pl.BlockSpec((tm, w), lambda i: (i, 3)), pl.BlockSpec((tm, w), lambda i: (i, 4)),
                  pl.BlockSpec((r, HD, w), lambda i: (i, 0, 0)), pl.BlockSpec((r, HD, w), lambda i: (i, 0, 0)),
                  _full((HEADS, CH, CH)), _full((CH, w)), _full((CH, w))],
        out_specs=[pl.BlockSpec((tm, w), lambda i: (i, 0))] * 2,
        out_shape=[S((t, w), F32), S((t, w), ACT_DTYPE)], name=name, sem=("parallel",))
    return a, y, rid


def _ret_bwd_pre(dy, a, proj, name):
    t = dy.shape[0]
    tm = _row_tile(t, 512)
    w = HEADS * HD

    def body(dy_ref, a_ref, g_ref, da_ref, dg_ref):
        for h in range(HEADS):
            cols = slice(h * HD, (h + 1) * HD)
            o, r = _standardize(a_ref[:, cols])
            gv = g_ref[:, cols]
            s = _sigmoid(gv)
            dyv = dy_ref[:, cols]
            dg_ref[:, cols] = (dyv * o * (s * (1.0 + gv * (1.0 - s)))).astype(dg_ref.dtype)
            da_ref[:, cols] = _standardize_bwd(dyv * (gv * s), o, r).astype(da_ref.dtype)

    return pl.pallas_call(
        body, grid=(t // tm,),
        in_specs=[pl.BlockSpec((tm, w), lambda i: (i, 0)), pl.BlockSpec((tm, w), lambda i: (i, 0)),
                  pl.BlockSpec((tm, w), lambda i: (i, 4))],
        out_specs=[pl.BlockSpec((tm, w), lambda i: (i, 0))] * 2,
        out_shape=[S((t, w), ACT_DTYPE)] * 2, name=name, compiler_params=_cp("parallel"))(dy, a, proj)


def _ret_bwd_main(rq, rk, proj, da, sf, sb, gf, gb, tb, name):
    t = rq.shape[0]
    r = 2 if (t // CH) % 2 == 0 else 1
    tm = r * CH
    w = HEADS * HD
    scale = HD ** -0.5

    def body(rq_ref, rk_ref, v_ref, da_ref, sf_ref, sb_ref, gf_ref, gb_ref, dm_ref, dmt_ref,
             xif_ref, xib_ref, zf_ref, zb_ref, c_ref, s_ref, o_ref):
        for c in range(r):
            rows = slice(c * CH, (c + 1) * CH)
            cos2, sin2 = c_ref[rows, :], s_ref[rows, :]
            for h in range(HEADS):
                cols = slice(h * HD, (h + 1) * HD)
                q, k, v, dav = rq_ref[rows, cols], rk_ref[rows, cols], v_ref[rows, cols], da_ref[rows, cols]
                qm, km, vm, dam = _mx(q), _mx(k), _mx(v), _mx(dav)
                dm, dmt = dm_ref[h], dmt_ref[h]
                pt = _dot_nt(km, qm) * dmt
                dp = _dot_nt(dam, vm) * dm
                dpt = _dot_nt(vm, dam) * dmt
                sfh, sbh, gfh, gbh = sf_ref[c, :, cols], sb_ref[c, :, cols], gf_ref[c, :, cols], gb_ref[c, :, cols]
                zf, zb = zf_ref[:, cols], zb_ref[:, cols]
                dv = _dot(pt, dam) + zf * _dot(km, gfh) + zb * _dot(km, gbh)
                drq = _dot(dp, km) + xif_ref[:, cols] * _dot_nt(dam, sfh) + xib_ref[:, cols] * _dot_nt(dam, sbh)
                drk = _dot(dpt, qm) + _dot_nt(zf * v, gfh) + _dot_nt(zb * v, gbh)
                o_ref[rows, h * HD:(h + 1) * HD] = _rot_bwd(drq, cos2, sin2).astype(o_ref.dtype)
                o_ref[rows, w + h * HD:w + (h + 1) * HD] = (_rot_bwd(drk, cos2, sin2) * scale).astype(o_ref.dtype)
                o_ref[rows, 2 * w + h * HD:2 * w + (h + 1) * HD] = dv.astype(o_ref.dtype)

    st = pl.BlockSpec((r, HD, w), lambda i: (i, 0, 0))
    return pl.pallas_call(
        body, grid=(t // tm,),
        in_specs=[pl.BlockSpec((tm, w), lambda i: (i, 0)), pl.BlockSpec((tm, w), lambda i: (i, 0)),
                  pl.BlockSpec((tm, w), lambda i: (i, 3)), pl.BlockSpec((tm, w), lambda i: (i, 0)),
                  st, st, st, st, _full((HEADS, CH, CH)), _full((HEADS, CH, CH)),
                  _full((CH, w)), _full((CH, w)), _full((CH, w)), _full((CH, w)),
                  pl.BlockSpec((tm, HD), lambda i: (i, 0)), pl.BlockSpec((tm, HD), lambda i: (i, 0))],
        out_specs=pl.BlockSpec((tm, 3 * w), lambda i: (i, 0)),
        out_shape=S((t, 3 * w), ACT_DTYPE), name=name,
        compiler_params=_cp("parallel"))(rq, rk, proj, da, sf, sb, gf, gb, tb["dm"], tb["dmt"],
                                         tb["xif"], tb["xib"], tb["zf"], tb["zb"], tb["cos2"], tb["sin2"])


CONV_TM = 256
CONV_SUB = 64
A_COL = (2 * GM_W + 4 * RET_W) // CV_W
G_COL = A_COL + 1


def _halo_specs(t, tm, col):
    nb16 = t // HALO
    per = tm // HALO
    return [pl.BlockSpec((tm, CV_W), lambda i: (i, col)),
            pl.BlockSpec((HALO, CV_W), lambda i: (jnp.maximum(i * per - 1, 0), col)),
            pl.BlockSpec((HALO, CV_W), lambda i: (jnp.minimum((i + 1) * per, nb16 - 1), col))]


def _fill_padded(dst_ref, prev, main, nxt, tm, i, nb):
    dst_ref[0:HALO, :] = jnp.where(i > 0, prev, 0.0)
    dst_ref[HALO:HALO + tm, :] = main
    dst_ref[HALO + tm:2 * HALO + tm, :] = jnp.where(i < nb - 1, nxt, 0.0)


def _conv_fwd(proj, cw, cb, ln_g, ln_b, name, riders=()):
    t = proj.shape[0]
    tm = _row_tile(t, CONV_TM)
    nb = t // tm

    def body(a_ref, ap_ref, an_ref, g_ref, gp_ref, gn_ref, w_ref, b_ref, lg_ref, lb_ref, y_ref, hc_ref, hp_ref):
        i = pl.program_id(0)
        _fill_padded(hp_ref, ap_ref[...] * _sigmoid(gp_ref[...]), a_ref[...] * _sigmoid(g_ref[...]),
                     an_ref[...] * _sigmoid(gn_ref[...]), tm, i, nb)
        for sb in range(tm // CONV_SUB):
            acc = jnp.zeros((CONV_SUB, CV_W), F32) + b_ref[...]
            for k in range(KCONV):
                acc = acc + w_ref[k:k + 1, :] * hp_ref[pl.ds(sb * CONV_SUB + k + 1, CONV_SUB), :]
            rows = slice(sb * CONV_SUB, (sb + 1) * CONV_SUB)
            hc_ref[rows, :] = acc
            o, _ = _standardize(acc)
            z = o * lg_ref[...] + lb_ref[...]
            y_ref[rows, :] = (z * _sigmoid(z)).astype(y_ref.dtype)

    (y, hc), rid = _pcall(
        body, [proj, proj, proj, proj, proj, proj, cw, cb, ln_g, ln_b], riders, grid=(nb,),
        in_specs=_halo_specs(t, tm, A_COL) + _halo_specs(t, tm, G_COL)
        + [_full((32, CV_W)), _full((1, CV_W)), _full((1, CV_W)), _full((1, CV_W))],
        out_specs=[pl.BlockSpec((tm, CV_W), lambda i: (i, 0))] * 2,
        out_shape=[S((t, CV_W), ACT_DTYPE), S((t, CV_W), F32)], name=name, sem=("parallel",),
        scratch_shapes=[pltpu.VMEM((tm + 2 * HALO, CV_W), F32)])
    return y, hc, rid


def _conv_bwd(proj, dy, hc, cw, ln_g, ln_b, name, riders=()):
    t = proj.shape[0]
    tm = _row_tile(t, CONV_TM)
    nb = t // tm

    def body(a_ref, ap_ref, an_ref, g_ref, gp_ref, gn_ref, dy_ref, dyp_ref, dyn_ref, hc_ref, hcp_ref, hcn_ref,
             w_ref, lg_ref, lb_ref, d_ref, dw_ref, dcb_ref, dlg_ref, dlb_ref, hp_ref, dhp_ref, dwacc_ref):
        i = pl.program_id(0)
        first = i == 0

        def dhc_of(dyv, hcv):
            o, r = _standardize(hcv)
            z = o * lg_ref[...] + lb_ref[...]
            s = _sigmoid(z)
            dz = dyv * (s * (1.0 + z * (1.0 - s)))
            return _standardize_bwd(dz * lg_ref[...], o, r), dz, o

        dhc, dz, o = dhc_of(dy_ref[...], hc_ref[...])
        _acc_out(dlg_ref, jnp.sum(dz * o, axis=0, keepdims=True), first)
        _acc_out(dlb_ref, jnp.sum(dz, axis=0, keepdims=True), first)
        _acc_out(dcb_ref, jnp.sum(dhc, axis=0, keepdims=True), first)
        _fill_padded(dhp_ref, dhc_of(dyp_ref[...], hcp_ref[...])[0], dhc, dhc_of(dyn_ref[...], hcn_ref[...])[0],
                     tm, i, nb)
        _fill_padded(hp_ref, ap_ref[...] * _sigmoid(gp_ref[...]), a_ref[...] * _sigmoid(g_ref[...]),
                     an_ref[...] * _sigmoid(gn_ref[...]), tm, i, nb)

        @pl.when(first)
        def _():
            dwacc_ref[...] = jnp.zeros_like(dwacc_ref)

        for sb in range(tm // CONV_SUB):
            base = sb * CONV_SUB
            dmain = dhp_ref[pl.ds(HALO + base, CONV_SUB), :]
            dh = jnp.zeros((CONV_SUB, CV_W), F32)
            for k in range(KCONV):
                dh = dh + w_ref[k:k + 1, :] * dhp_ref[pl.ds(base + 2 * HALO - 1 - k, CONV_SUB), :]
                prod = dmain * hp_ref[pl.ds(base + k + 1, CONV_SUB), :]
                dwacc_ref[k * 8:(k + 1) * 8, :] += jnp.sum(prod.reshape(CONV_SUB // 8, 8, CV_W), axis=0)
            rows = slice(base, base + CONV_SUB)
            s = _sigmoid(g_ref[rows, :])
            d_ref[rows, :CV_W] = (dh * s).astype(d_ref.dtype)
            d_ref[rows, CV_W:] = (dh * a_ref[rows, :] * (s * (1.0 - s))).astype(d_ref.dtype)

        @pl.when(i == nb - 1)
        def _():
            for k in range(KCONV):
                dw_ref[k:k + 1, :] = jnp.sum(dwacc_ref[k * 8:(k + 1) * 8, :], axis=0, keepdims=True)
            dw_ref[KCONV:32, :] = jnp.zeros((32 - KCONV, CV_W), F32)

    hs = [pl.BlockSpec((tm, CV_W), lambda i: (i, 0)),
          pl.BlockSpec((HALO, CV_W), lambda i: (jnp.maximum(i * (tm // HALO) - 1, 0), 0)),
          pl.BlockSpec((HALO, CV_W), lambda i: (jnp.minimum((i + 1) * (tm // HALO), t // HALO - 1), 0))]
    outs, rid = _pcall(
        body, [proj, proj, proj, proj, proj, proj, dy, dy, dy, hc, hc, hc, cw, ln_g, ln_b], riders, grid=(nb,),
        in_specs=_halo_specs(t, tm, A_COL) + _halo_specs(t, tm, G_COL) + hs + hs
        + [_full((32, CV_W)), _full((1, CV_W)), _full((1, CV_W))],
        out_specs=[pl.BlockSpec((tm, 2 * CV_W), lambda i: (i, 0)), _full((32, CV_W)), _full((1, CV_W)),
                   _full((1, CV_W)), _full((1, CV_W))],
        out_shape=[S((t, 2 * CV_W), ACT_DTYPE), S((32, CV_W), F32), S((1, CV_W), F32), S((1, CV_W), F32),
                   S((1, CV_W), F32)],
        name=name, sem=("arbitrary",),
        scratch_shapes=[pltpu.VMEM((tm + 2 * HALO, CV_W), F32), pltpu.VMEM((tm + 2 * HALO, CV_W), F32),
                        pltpu.VMEM((32 * 8, CV_W), F32)])
    return (*outs, rid)


def _loss_head(x, g, target, name):
    t = x.shape[0]
    tm = _row_tile(t, 512)

    def body(x_ref, g_ref, t_ref, dx_ref, dg_ref, l_ref):
        first = pl.program_id(0) == 0
        xv = x_ref[...]
        r = _rms_r(xv)
        e = xv * r * g_ref[...] - t_ref[...]
        dx, dgrow = _rms_bwd(e * (1.0 / D), xv, r, g_ref[...])
        dx_ref[...] = dx
        _acc_out(dg_ref, jnp.sum(dgrow, axis=0, keepdims=True), first)
        part = 0.5 * jnp.sum(jnp.mean(e * e, axis=-1, keepdims=True), axis=0, keepdims=True)
        _acc_out(l_ref, jnp.broadcast_to(part, (8, 128)), first)

    return pl.pallas_call(
        body, grid=(t // tm,),
        in_specs=[pl.BlockSpec((tm, D), lambda i: (i, 0)), _full((1, D)), pl.BlockSpec((tm, D), lambda i: (i, 0))],
        out_specs=[pl.BlockSpec((tm, D), lambda i: (i, 0)), _full((1, D)), _full((8, 128))],
        out_shape=[S((t, D), F32), S((1, D), F32), S((8, 128), F32)], name=name,
        compiler_params=_cp("arbitrary"))(x, g, target)


def _as2d(a):
    return a.reshape(-1, a.shape[-1])


def _ew_tile(rows, cols, n_arrays):
    budget = VMEM_LIMIT // 2
    tr = rows
    while tr * cols * 4 * n_arrays * 2 > budget and tr % 16 == 0:
        tr //= 2
    assert rows % tr == 0
    return tr


def _adamw(w, g, m, v, name):
    shape = w.shape
    w2, g2, m2, v2 = _as2d(w), _as2d(g), _as2d(m), _as2d(v)
    rows, cols = w2.shape
    tr = _ew_tile(rows, cols, 7)

    def body(w_ref, g_ref, m_ref, v_ref, d_ref, nm_ref, nv_ref):
        gv = g_ref[...]
        nm = ADAM_B1 * m_ref[...] + (1.0 - ADAM_B1) * gv
        nv = ADAM_B2 * v_ref[...] + (1.0 - ADAM_B2) * (gv * gv)
        m_hat = nm / (1.0 - ADAM_B1 ** ADAM_STEP)
        v_hat = nv / (1.0 - ADAM_B2 ** ADAM_STEP)
        d_ref[...] = -ADAM_LR * (m_hat / (jnp.sqrt(v_hat) + ADAM_EPS) + ADAM_WD * w_ref[...])
        nm_ref[...] = nm
        nv_ref[...] = nv

    spec = pl.BlockSpec((tr, cols), lambda i: (i, 0))
    outs = pl.pallas_call(body, grid=(rows // tr,), in_specs=[spec] * 4, out_specs=[spec] * 3,
                          out_shape=[S((rows, cols), F32)] * 3, name=name,
                          compiler_params=_cp("parallel"))(w2, g2, m2, v2)
    return tuple(o.reshape(shape) for o in outs)


BIG = (("w_in", "col"), ("w_out", "row"), ("w_ffn_in", "col"), ("w_ffn_out", "row"))
NBIG = len(BIG)


def _cast_to_gathered(w, l, me, name):
    _, r_, c_ = w.shape
    tr = _ew_tile(r_, c_, 2)

    def body(me_ref, w_ref, o_ref):
        o_ref[...] = w_ref[...].astype(o_ref.dtype)

    gs = pltpu.PrefetchScalarGridSpec(
        num_scalar_prefetch=1, grid=(r_ // tr,),
        in_specs=[pl.BlockSpec((None, tr, c_), lambda i, s: (l, i, 0))],
        out_specs=pl.BlockSpec((None, tr, c_), lambda i, s: (s[0], i, 0)))
    out = pl.pallas_call(body, grid_spec=gs, out_shape=S((N_CHIPS, r_, c_), MXU_DTYPE), name=name,
                         compiler_params=_cp("parallel"))(me.reshape(1), w)
    return out.reshape(N_CHIPS, 2, r_ // 2, c_)


def _all_gather(bufs, name):
    n = len(bufs)

    def body(*refs):
        i_refs, o_refs = refs[:n], refs[n:2 * n]
        isend, irecv, dsend, drecv = refs[2 * n:]
        pos = _mesh_pos()
        ici = _rider_copies("ici", i_refs, o_refs, isend, irecv, pos)
        d2d = _rider_copies("d2d", o_refs, o_refs, dsend, drecv, pos)
        for cp, _ in ici:
            cp.start()
        for (_, land), (fwd, _) in zip(ici, d2d):
            land.wait_recv()
            fwd.start()
        for _, land in d2d:
            land.wait_recv()
        for cp, _ in ici + d2d:
            cp.wait_send()

    return pl.pallas_call(
        body, in_specs=[ANY] * n, out_specs=[ANY] * n, out_shape=[S(a.shape, a.dtype) for a in bufs],
        input_output_aliases={w: w for w in range(n)}, name=name,
        scratch_shapes=[pltpu.SemaphoreType.DMA((n, 3))] * 4)(*bufs)


def _pair_exchange(grads, name):
    n = len(grads)

    def body(*refs):
        g_refs, theirs = refs[:n], refs[n:2 * n]
        send, recv = refs[2 * n:]
        x, y, c, *_ = _mesh_pos()
        cps = []
        for w in range(n):
            cp = pltpu.make_async_remote_copy(
                src_ref=g_refs[w].at[:, 1 - c], dst_ref=theirs[w], send_sem=send.at[w], recv_sem=recv.at[w],
                device_id=(x, y, 1 - c), device_id_type=MESH)
            cp.start()
            cps.append(cp)
        for cp in cps:
            cp.wait()

    return pl.pallas_call(
        body, in_specs=[ANY] * n, out_specs=[ANY] * n,
        out_shape=[S(a.shape[:1] + a.shape[2:], a.dtype) for a in grads], name=name,
        scratch_shapes=[pltpu.SemaphoreType.DMA((n,))] * 2)(*grads)


def _pair_sum(g, theirs, core, name):
    _, _, rh, c_ = g.shape
    tr = _ew_tile(rh, c_, 2)

    def body(s_ref, g_ref, t_ref, o_ref):
        o_ref[...] = (g_ref[...].astype(F32) + t_ref[...].astype(F32)).astype(o_ref.dtype)

    blk = pl.BlockSpec((None, tr, c_), lambda j, i, s: (j, i, 0))
    gs = pltpu.PrefetchScalarGridSpec(
        num_scalar_prefetch=1, grid=(N_CHIPS, rh // tr),
        in_specs=[pl.BlockSpec((None, None, tr, c_), lambda j, i, s: (j, s[0], i, 0)), blk], out_specs=blk)
    return pl.pallas_call(body, grid_spec=gs, out_shape=S(theirs.shape, theirs.dtype), name=name,
                          compiler_params=_cp("parallel", "parallel"))(core.reshape(1), g, theirs)


def _chip_scatter(qs, name):
    n = len(qs)

    def body(*refs):
        q_refs, got = refs[:n], refs[n:2 * n]
        send, recv = refs[2 * n:]
        copies = _rider_copies("scatter", q_refs, got, send, recv, _mesh_pos())
        for cp, _ in copies:
            cp.start()
        for cp, land in copies:
            land.wait_recv()
            cp.wait_send()

    return pl.pallas_call(
        body, in_specs=[ANY] * n, out_specs=[ANY] * n, out_shape=[_rider_out_shape("scatter", a) for a in qs],
        name=name, scratch_shapes=[pltpu.SemaphoreType.DMA((n, 3))] * 2)(*qs)


def _chip_sum(q, got, l, me, core, into, name):
    _, rh, c_ = got.shape
    tr = _ew_tile(rh, c_, 4)

    def body(s_ref, q_ref, g0_ref, g1_ref, g2_ref, o_ref):
        acc = q_ref[...].astype(F32)
        for r in (g0_ref, g1_ref, g2_ref):
            acc = acc + r[...].astype(F32)
        o_ref[...] = acc

    in_specs = [pl.BlockSpec((None, tr, c_), lambda i, s: (s[0], i, 0))] + [
        pl.BlockSpec((None, tr, c_), functools.partial(lambda k, i, s: (k, i, 0), k)) for k in range(3)]
    return _call_into(
        body, into, in_specs, [jnp.stack([me, core]), q, got, got, got], n_prefetch=1, grid=(rh // tr,),
        out_specs=pl.BlockSpec((None, None, tr, c_), lambda i, s: (l, s[1], i, 0)),
        out_shape=S((DEPTH, 2, rh, c_), F32), name=name, compiler_params=_cp("parallel"))


def _pair_gather(gs4):
    def body(*refs):
        i_refs, o_refs = refs[:NBIG], refs[NBIG:2 * NBIG]
        send, recv = refs[2 * NBIG:]
        x, y, c, *_ = _mesh_pos()
        cps = []
        for w in range(NBIG):
            cp = pltpu.make_async_remote_copy(
                src_ref=i_refs[w].at[:, c], dst_ref=o_refs[w].at[:, c], send_sem=send.at[w], recv_sem=recv.at[w],
                device_id=(x, y, 1 - c), device_id_type=MESH)
            cp.start()
            cps.append(cp)
        for cp in cps:
            cp.wait()

    outs = pl.pallas_call(
        body, in_specs=[ANY] * NBIG, out_specs=[ANY] * NBIG, out_shape=[S(a.shape, a.dtype) for a in gs4],
        input_output_aliases={w: w for w in range(NBIG)}, name="grad_pair_gather",
        scratch_shapes=[pltpu.SemaphoreType.DMA((NBIG,))] * 2)(*gs4)
    return [o.reshape(o.shape[0], 2 * o.shape[2], o.shape[3]) for o in outs]


def _all_reduce_small(p, name):
    rows = p.shape[0]

    def body(p_ref, o_ref, gath, send, recv):
        x, y, c = lax.axis_index("x"), lax.axis_index("y"), lax.axis_index("c")
        my_id = 4 * x + 2 * y + c
        gath[my_id] = p_ref[...]
        cps = []
        for r in range(1, N_DEV):
            bx, by, bc = (r >> 2) & 1, (r >> 1) & 1, r & 1
            tx, ty, tc = (1 - x if bx else x), (1 - y if by else y), (1 - c if bc else c)
            cp = pltpu.make_async_remote_copy(
                src_ref=p_ref, dst_ref=gath.at[my_id], send_sem=send.at[r - 1], recv_sem=recv.at[r - 1],
                device_id=(tx, ty, tc), device_id_type=MESH)
            cp.start()
            cps.append((cp, 4 * tx + 2 * ty + tc))
        for r, (cp, peer) in enumerate(cps):
            pltpu.make_async_remote_copy(
                src_ref=p_ref, dst_ref=gath.at[peer], send_sem=send.at[r], recv_sem=recv.at[r],
                device_id=(x, y, c), device_id_type=MESH).wait_recv()
        for cp, _ in cps:
            cp.wait_send()
        acc = gath[0]
        for s in range(1, N_DEV):
            acc = acc + gath[s]
        o_ref[...] = acc

    vm = pl.BlockSpec(memory_space=pltpu.VMEM)
    return pl.pallas_call(
        body, in_specs=[vm], out_specs=vm, out_shape=S((rows, 128), F32), name=name,
        scratch_shapes=[pltpu.VMEM((N_DEV, rows, 128), F32), pltpu.SemaphoreType.DMA((N_DEV - 1,)),
                        pltpu.SemaphoreType.DMA((N_DEV - 1,))],
        compiler_params=pltpu.CompilerParams(vmem_limit_bytes=VMEM_LIMIT))(p)


PACK_UNIT = 8 * 128


def _pack(arrs):
    parts = []
    for a in arrs:
        flat = a.reshape(-1)
        pad = (-flat.shape[0]) % PACK_UNIT
        parts.append(jnp.pad(flat, (0, pad)).reshape(-1, 128))
    return jnp.concatenate(parts, axis=0)


def _unpack(buf, shapes):
    outs, row = [], 0
    for shp in shapes:
        n = int(np.prod(shp))
        rows = -(-n // PACK_UNIT) * 8
        outs.append(buf[row:row + rows].reshape(-1)[:n].reshape(shp))
        row += rows
    return outs


SMALL = ("norm1_g", "gm_ln_g", "gm_ln_b", "gm_ws", "gm_bs", "conv_w", "conv_b", "conv_ln_g", "conv_ln_b",
         "norm2_g", "final_g")
WEIGHTS = ("norm1_g", "w_in", "gm_ln_g", "gm_ln_b", "gm_ws", "gm_bs", "conv_w", "conv_b", "conv_ln_g",
           "conv_ln_b", "w_out", "norm2_g", "w_ffn_in", "w_ffn_out", "final_g")


def kernel(x, norm1_g, w_in, gm_ln_g, gm_ln_b, gm_ws, gm_bs, conv_w, conv_b, conv_ln_g, conv_ln_b, w_out, norm2_g, w_ffn_in, w_ffn_out, final_g, loss_target, m_norm1_g, m_w_in, m_gm_ln_g, m_gm_ln_b, m_gm_ws, m_gm_bs, m_conv_w, m_conv_b, m_conv_ln_g, m_conv_ln_b, m_w_out, m_norm2_g, m_w_ffn_in, m_w_ffn_out, m_final_g, v_norm1_g, v_w_in, v_gm_ln_g, v_gm_ln_b, v_gm_ws, v_gm_bs, v_conv_w, v_conv_b, v_conv_ln_g, v_conv_ln_b, v_w_out, v_norm2_g, v_w_ffn_in, v_w_ffn_out, v_final_g):
    given = dict(locals())
    t = x.shape[1]
    xc = x.reshape(t, D)
    target = loss_target.reshape(t, D)
    me = 2 * lax.axis_index("x") + lax.axis_index("y")
    core = lax.axis_index("c")
    tb = _tables(t)

    me = me.astype(jnp.int32)
    core = core.astype(jnp.int32)
    names = [n for n, _ in BIG]
    kinds = dict(BIG)
    gathered = [{n: _cast_to_gathered(given[n], l, me, f"cast_{n}{l}") for n in names} for l in range(DEPTH)]
    gathered[0]["w_in"] = _all_gather([gathered[0]["w_in"]], "all_gather_w_in0")[0]

    def weight(l, n):
        b = gathered[l][n]
        r_, c_ = 2 * b.shape[2], b.shape[3]
        return b.reshape(N_CHIPS, r_, c_) if kinds[n] == "col" else b.reshape(N_CHIPS * r_, c_)

    cshard = CV_W // N_CHIPS
    placed = lax.dynamic_update_slice(jnp.zeros((DEPTH, KCONV, CV_W), F32),
                                      conv_w * (core == 0).astype(F32), (0, 0, me * cshard))
    conv_w_full = _unpack(_all_reduce_small(_pack([placed]), "gather_conv_w"), [(DEPTH, KCONV, CV_W)])[0]
    cw32 = jnp.pad(conv_w_full, ((0, 0), (0, 32 - KCONV), (0, 0)))

    def row(a, l):
        return a[l].reshape(1, -1)

    saved = []
    early = ["w_in", "w_out", "w_ffn_in"]
    for l in range(DEPTH):
        cur = gathered[l]
        nxt = gathered[l + 1] if l + 1 < DEPTH else None
        sv = {"x": xc}
        bias = jnp.repeat(gm_bs[l].T, GM_W // GM_HEADS, axis=1)
        first = ["w_out", "w_ffn_out"] if l == 0 else ["w_ffn_out"]
        proj, rid = _norm_mm(xc, row(norm1_g, l), weight(l, "w_in"), F32, f"in_proj{l}", 512,
                             [("ici" if l == 0 else "d2d", [cur[n] for n in first])])
        cur.update(zip(first, rid))
        y_gm, rid = _gm_fwd(proj, row(gm_ln_g, l), row(gm_ln_b, l), gm_ws[l], bias, f"gm_fwd{l}",
                            [("d2d", [cur[n] for n in first])] if l == 0 else ())
        cur.update(zip(first, rid))
        rq, rk = _rotary(proj, tb["cos2"], tb["sin2"], f"rotary{l}")
        sf, sb = _ret_scan(rk, proj, 3, tb["zf"], tb["zb"], tb["gcf"], tb["gcb"], f"ret_state{l}")
        a, y_ret, rid = _ret_out(rq, rk, proj, sf, sb, tb, f"ret_out{l}",
                                 [("ici", [cur["w_ffn_in"]])] if l == 0 else ())
        cur.update(zip(["w_ffn_in"], rid))
        y_cv, hc, rid = _conv_fwd(proj, cw32[l], row(conv_b, l), row(conv_ln_g, l), row(conv_ln_b, l),
                                  f"conv_fwd{l}", [("d2d", [cur["w_ffn_in"]])] if l == 0 else ())
        cur.update(zip(["w_ffn_in"], rid))
        x_mid = _parts_mm_res([y_gm, y_ret, y_cv], weight(l, "w_out"), xc, f"out_proj{l}")
        ff, rid = _norm_mm(x_mid, row(norm2_g, l), weight(l, "w_ffn_in"), ACT_DTYPE, f"ffn_in{l}", 512,
                           [("ici", [nxt[n] for n in early])] if nxt else ())
        if nxt:
            nxt.update(zip(early, rid))
        xc, rid = _swiglu_mm_res(ff, weight(l, "w_ffn_out"), x_mid, f"ffn_out{l}",
                                 [("d2d", [nxt[n] for n in early]), ("ici", [nxt["w_ffn_out"]])] if nxt else ())
        if nxt:
            nxt.update(zip(early + ["w_ffn_out"], rid))
        sv.update(bias=bias, proj=proj, y_gm=y_gm, rq=rq, rk=rk, sf=sf, sb=sb, a=a, y_ret=y_ret, y_cv=y_cv,
                  hc=hc, x_mid=x_mid, ff=ff)
        saved.append(sv)

    dx, d_final_g, lpart = _loss_head(xc, final_g.reshape(1, D), target, "loss_head")
    loss = lax.psum(lpart[0, 0], ("x", "y", "c"))

    small_g = {n: [None] * DEPTH for n in SMALL}
    qs = [{} for _ in range(DEPTH)]
    got = [{} for _ in range(DEPTH)]
    ffn_w, mix_w = ["w_ffn_out", "w_ffn_in"], ["w_out", "w_in"]

    def pair_reduce(l, group, big_g):
        g4 = [big_g[n].reshape(N_CHIPS, 2, given[n].shape[1] // 2, given[n].shape[2]) for n in group]
        theirs = _pair_exchange(g4, f"grad_pair_exchange_{group[0]}{l}")
        qs[l].update({n: _pair_sum(g, th, core, f"pair_sum_{n}{l}") for n, g, th in zip(group, g4, theirs)})
        return [qs[l][n] for n in group]

    riding = None
    for l in reversed(range(DEPTH)):
        sv = saved[l]
        proj = sv["proj"]
        big_g = {}
        dff = _dx_swiglu(dx, weight(l, "w_ffn_out"), sv["ff"], f"ffn_out_dx{l}")
        big_g["w_ffn_out"] = _dw_swiglu(sv["ff"], dx, f"ffn_out_dw{l}")
        dx_mid, dg2 = _dx_norm([dff], weight(l, "w_ffn_in"), sv["x_mid"], row(norm2_g, l), dx, f"ffn_in_dx{l}", 256)
        big_g["w_ffn_in"], rid = _dw_norm_cols(sv["x_mid"], row(norm2_g, l), dff, w_ffn_in.shape[2],
                                               f"ffn_in_dw{l}", [("scatter", riding)] if riding else ())
        if riding:
            got[l + 1].update(zip(mix_w, rid))
        q_ffn = pair_reduce(l, ffn_w, big_g)
        dy_gm, dy_ret, dy_cv = _dx_parts(dx_mid, weight(l, "w_out"), [GM_W, RET_W, CV_W], f"out_proj_dx{l}")
        big_g["w_out"] = _dw_parts([sv["y_gm"], sv["y_ret"], sv["y_cv"]], dx_mid, f"out_proj_dw{l}")
        d_cv, dcw, dcb, dclg, dclb, rid = _conv_bwd(proj, dy_cv, sv["hc"], cw32[l], row(conv_ln_g, l),
                                                    row(conv_ln_b, l), f"conv_bwd{l}", [("scatter", q_ffn)])
        got[l].update(zip(ffn_w, rid))
        da, d_g = _ret_bwd_pre(dy_ret, sv["a"], proj, f"ret_bwd_pre{l}")
        gb_, gf_ = _ret_scan(sv["rq"], da, 0, tb["xib"], tb["xif"], tb["gcb"], tb["gcf"], f"ret_bwd_state{l}")
        d_qkv = _ret_bwd_main(sv["rq"], sv["rk"], proj, da, sv["sf"], sv["sb"], gf_, gb_, tb, f"ret_bwd_main{l}")
        d_gm, dws, dbs, dglg, dglb = _gm_bwd(proj, dy_gm, row(gm_ln_g, l), row(gm_ln_b, l), gm_ws[l],
                                             jnp.swapaxes(gm_ws[l], 1, 2), sv["bias"], f"gm_bwd{l}")
        dparts = [d_gm, d_qkv, d_g, d_cv]
        dx, dg1 = _dx_norm(dparts, weight(l, "w_in"), sv["x"], row(norm1_g, l), dx_mid, f"in_proj_dx{l}", 512)
        big_g["w_in"] = _dw_norm_parts(sv["x"], row(norm1_g, l), dparts, w_in.shape[2], f"in_proj_dw{l}")
        for n, val in (("norm1_g", dg1[0]), ("gm_ln_g", dglg[0]), ("gm_ln_b", dglb[0]), ("gm_ws", dws),
                       ("gm_bs", dbs[:, :GM_HEADS].T), ("conv_w", dcw[:KCONV]), ("conv_b", dcb[0]),
                       ("conv_ln_g", dclg[0]), ("conv_ln_b", dclb[0]), ("norm2_g", dg2[0])):
            small_g[n][l] = val
        riding = pair_reduce(l, mix_w, big_g)
    got[0].update(zip(mix_w, _chip_scatter(riding, "grad_chip_scatter_mix0")))

    small_shapes = [given[n].shape if n != "conv_w" else (DEPTH, KCONV, CV_W) for n in SMALL]
    partials = [d_final_g[0] if n == "final_g" else jnp.stack(small_g[n]) for n in SMALL]
    reduced = dict(zip(SMALL, _unpack(_all_reduce_small(_pack(partials), "all_reduce_small_grads"), small_shapes)))
    reduced["conv_w"] = lax.dynamic_slice(reduced["conv_w"], (0, 0, me * cshard), (DEPTH, KCONV, cshard))

    halves = [None] * NBIG
    for l in reversed(range(DEPTH)):
        halves = [_chip_sum(qs[l][n], got[l][n], l, me, core, h, f"chip_sum_{n}{l}") for n, h in zip(names, halves)]
    grads = dict(zip(names, _pair_gather(halves)))
    grads.update(reduced)

    delta, new_m, new_v = {}, {}, {}
    for n, _ in BIG:
        delta[n], new_m[n], new_v[n] = _adamw(given[n], grads[n], given["m_" + n], given["v_" + n], f"adamw_{n}")
    shapes = [given[n].shape for n in SMALL]
    packed = [_pack([src[n] if src is grads else src[p + n] for n in SMALL])
              for src, p in ((given, ""), (grads, ""), (given, "m_"), (given, "v_"))]
    outs = _adamw(*packed, "adamw_small")
    for dst, buf in zip((delta, new_m, new_v), outs):
        dst.update(zip(SMALL, _unpack(buf, shapes)))

    return (loss, dx.reshape(1, t, D), *[grads[n] for n in WEIGHTS], *[delta[n] for n in WEIGHTS],
            *[new_m[n] for n in WEIGHTS], *[new_v[n] for n in WEIGHTS])
```

```python
import functools
import math

import numpy as np
import jax
import jax.numpy as jnp
from jax import lax
from jax.experimental import pallas as pl
from jax.experimental.pallas import tpu as pltpu

F32 = jnp.float32
BF16 = jnp.bfloat16
MXU_DTYPE = BF16
ACT_DTYPE = BF16
S = jax.ShapeDtypeStruct

D = 1024
DEPTH = 2
GM_W = 256
GM_HEADS = 4
RET_W = 512
HEADS = 4
HD = 128
CV_W = 256
KCONV = 31
IN_W = 2 * GM_W + 4 * RET_W + 2 * CV_W
FFN_H = 2816
CH = 128
ROPE_BASE = 10000.0
EPS = 1e-6
N_CHIPS = 4
N_DEV = 8
HALO = 16

ADAM_LR = 0.001
ADAM_B1 = 0.9
ADAM_B2 = 0.999
ADAM_EPS = 1e-08
ADAM_WD = 0.01
ADAM_STEP = 10

VMEM_LIMIT = 52 * 1024 * 1024
MESH = pl.DeviceIdType.MESH


def _cp(*sem, vmem=VMEM_LIMIT):
    return pltpu.CompilerParams(dimension_semantics=tuple(sem), vmem_limit_bytes=vmem)


def _mx(a):
    return a.astype(MXU_DTYPE)


def _dot(a, b):
    return jnp.dot(_mx(a), _mx(b), preferred_element_type=F32)


def _dot_nt(a, b):
    return lax.dot_general(_mx(a), _mx(b), (((1,), (1,)), ((), ())), preferred_element_type=F32)


def _dot_tn(a, b):
    return lax.dot_general(_mx(a), _mx(b), (((0,), (0,)), ((), ())), preferred_element_type=F32)


def _sigmoid(x):
    return 1.0 / (1.0 + jnp.exp(-x))


def _gelu(x):
    return 0.5 * x * (1.0 + lax.erf(x * (1.0 / math.sqrt(2.0))))


def _gelu_grad(x):
    return 0.5 * (1.0 + lax.erf(x * (1.0 / math.sqrt(2.0)))) + x * jnp.exp(-0.5 * x * x) * (1.0 / math.sqrt(2.0 * math.pi))


def _rms_r(x):
    return lax.rsqrt(jnp.mean(x * x, axis=-1, keepdims=True) + EPS)


def _rms_bwd(dh, x, r, g):
    u = dh * g
    dx = r * u - x * (r * r * r) * jnp.mean(u * x, axis=-1, keepdims=True)
    return dx, dh * x * r


def _standardize(a):
    mu = jnp.mean(a, axis=-1, keepdims=True)
    d = a - mu
    r = lax.rsqrt(jnp.mean(d * d, axis=-1, keepdims=True) + EPS)
    return d * r, r


def _standardize_bwd(do, o, r):
    return r * (do - jnp.mean(do, axis=-1, keepdims=True) - o * jnp.mean(do * o, axis=-1, keepdims=True))


def _acc_out(ref, val, first):
    @pl.when(first)
    def _():
        ref[...] = val

    @pl.when(jnp.logical_not(first))
    def _():
        ref[...] += val


def _row_tile(t, pref):
    tm = min(t, pref)
    assert t % tm == 0, (t, tm)
    return tm


def _segments(part_widths, shard_w):
    bounds = {0}
    off = 0
    for w in part_widths:
        off += w
        bounds.add(off)
    total = off
    for j in range(1, total // shard_w + 1):
        bounds.add(j * shard_w)
    bounds = sorted(bounds)
    starts = np.cumsum([0] + list(part_widths))
    segs = []
    for a, b in zip(bounds[:-1], bounds[1:]):
        p = int(np.searchsorted(starts, a, side="right") - 1)
        segs.append((p, a - int(starts[p]), a // shard_w, a % shard_w, b - a))
    return segs


ANY = pl.BlockSpec(memory_space=pl.ANY)


def _mesh_pos():
    x, y, c = lax.axis_index("x"), lax.axis_index("y"), lax.axis_index("c")
    chips = [(1 - x, y), (x, 1 - y), (1 - x, 1 - y)]
    return x, y, c, 2 * x + y, chips, [2 * cx + cy for cx, cy in chips]


def _rider_copies(kind, i_refs, o_refs, send, recv, pos):
    x, y, c, me, chips, cj = pos
    out = []
    for b, (i_ref, o_ref) in enumerate(zip(i_refs, o_refs)):
        for k in range(3):
            if kind == "ici":
                src, dst, land, dev = i_ref.at[me, c], o_ref.at[me, c], o_ref.at[cj[k], c], (*chips[k], c)
            elif kind == "d2d":
                src, dst, land, dev = i_ref.at[cj[k], c], o_ref.at[cj[k], c], o_ref.at[cj[k], 1 - c], (x, y, 1 - c)
            else:
                src, dst, land, dev = i_ref.at[cj[k]], o_ref.at[k], o_ref.at[k], (*chips[k], c)
            out.append(tuple(pltpu.make_async_remote_copy(
                src_ref=s_, dst_ref=d_, send_sem=send.at[b, k], recv_sem=recv.at[b, k],
                device_id=dev, device_id_type=MESH) for s_, d_ in ((src, dst), (land, land))))
    return out


def _rider_out_shape(kind, a):
    return S((3,) + a.shape[1:], a.dtype) if kind == "scatter" else S(a.shape, a.dtype)


def _pcall(body, args, riders, *, grid, in_specs, out_specs, out_shape, name, sem, scratch_shapes=()):
    outs = list(out_shape)
    if not riders:
        res = pl.pallas_call(body, grid=grid, in_specs=in_specs, out_specs=out_specs, out_shape=outs, name=name,
                             scratch_shapes=list(scratch_shapes), compiler_params=_cp(*sem))(*args)
        return res, []
    r_in = [a for _, bufs in riders for a in bufs]
    r_out = [_rider_out_shape(kind, a) for kind, bufs in riders for a in bufs]
    n_in, n_out, n_scr, n_r = len(args), len(outs), len(scratch_shapes), len(r_in)
    aliases, idx = {}, 0
    for kind, bufs in riders:
        for _ in bufs:
            if kind != "scatter":
                aliases[n_in + idx] = n_out + idx
            idx += 1
    sems = [pltpu.SemaphoreType.DMA((len(bufs), 3)) for _, bufs in riders for _ in range(2)]

    def wrapped(*refs):
        a, ri = refs[:n_in], refs[n_in:n_in + n_r]
        o, ro = refs[n_in + n_r:n_in + n_r + n_out], refs[n_in + n_r + n_out:n_in + 2 * n_r + n_out]
        scr = refs[n_in + 2 * n_r + n_out:n_in + 2 * n_r + n_out + n_scr]
        sm = refs[n_in + 2 * n_r + n_out + n_scr:]
        pos = _mesh_pos()
        copies, off = [], 0
        for r, (kind, bufs) in enumerate(riders):
            copies += _rider_copies(kind, ri[off:off + len(bufs)], ro[off:off + len(bufs)], sm[2 * r], sm[2 * r + 1], pos)
            off += len(bufs)
        ids = [pl.program_id(d) for d in range(len(grid))]
        first = functools.reduce(jnp.logical_and, [i == 0 for i in ids])
        last = functools.reduce(jnp.logical_and, [i == n - 1 for i, n in zip(ids, grid)])

        @pl.when(first)
        def _():
            for cp, _ in copies:
                cp.start()

        body(*a, *o, *scr)

        @pl.when(last)
        def _():
            for cp, land in copies:
                land.wait_recv()
                cp.wait_send()

    res = pl.pallas_call(
        wrapped, grid=grid, in_specs=list(in_specs) + [ANY] * n_r, out_specs=list(out_specs) + [ANY] * n_r,
        out_shape=outs + r_out, input_output_aliases=aliases, name=name,
        scratch_shapes=list(scratch_shapes) + sems, compiler_params=_cp(*(("arbitrary",) * len(grid))))(*args, *r_in)
    return res[:n_out], res[n_out:]


def _wcol_spec(w):
    return pl.BlockSpec(w.shape, lambda *_: (0, 0, 0))


def _wrow_spec(w):
    return pl.BlockSpec(w.shape, lambda *_: (0, 0))


def _norm_mm(x, g, w, out_dtype, name, tm_pref, riders=()):
    t = x.shape[0]
    nc = w.shape[2]
    tm = _row_tile(t, tm_pref)

    def body(x_ref, g_ref, w_ref, o_ref):
        xv = x_ref[...]
        h = _mx(xv * _rms_r(xv) * g_ref[...])
        for j in range(N_CHIPS):
            o_ref[:, j * nc:(j + 1) * nc] = jnp.dot(h, w_ref[j], preferred_element_type=F32).astype(o_ref.dtype)

    (out,), rid = _pcall(
        body, [x, g, w], riders, grid=(t // tm,),
        in_specs=[pl.BlockSpec((tm, D), lambda i: (i, 0)), pl.BlockSpec((1, D), lambda i: (0, 0)), _wcol_spec(w)],
        out_specs=[pl.BlockSpec((tm, N_CHIPS * nc), lambda i: (i, 0))],
        out_shape=[S((t, N_CHIPS * nc), out_dtype)], name=name, sem=("parallel",))
    return out, rid


def _parts_mm_res(parts, w, res, name):
    t = res.shape[0]
    tm = _row_tile(t, 512)
    widths = [p.shape[1] for p in parts]
    offs = np.cumsum([0] + widths)
    n = len(parts)

    def body(*refs):
        p_refs, w_ref, r_ref, o_ref = refs[:n], refs[n], refs[n + 1], refs[n + 2]
        acc = r_ref[...]
        for p in range(n):
            acc = acc + _dot(p_refs[p][...], w_ref[int(offs[p]):int(offs[p + 1]), :])
        o_ref[...] = acc

    return pl.pallas_call(
        body, grid=(t // tm,),
        in_specs=[pl.BlockSpec((tm, wd), lambda i: (i, 0)) for wd in widths]
        + [_wrow_spec(w), pl.BlockSpec((tm, D), lambda i: (i, 0))],
        out_specs=pl.BlockSpec((tm, D), lambda i: (i, 0)),
        out_shape=S((t, D), F32), name=name, compiler_params=_cp("parallel"))(*parts, w, res)


def _swiglu(ff):
    gate = ff[:, :FFN_H].astype(F32)
    up = ff[:, FFN_H:].astype(F32)
    return gate * _sigmoid(gate) * up


def _swiglu_mm_res(ff, w, res, name, riders=()):
    t = res.shape[0]
    tm = _row_tile(t, 512)

    def body(f_ref, w_ref, r_ref, o_ref):
        o_ref[...] = r_ref[...] + _dot(_swiglu(f_ref[...]), w_ref[...])

    (out,), rid = _pcall(
        body, [ff, w, res], riders, grid=(t // tm,),
        in_specs=[pl.BlockSpec((tm, 2 * FFN_H), lambda i: (i, 0)), _wrow_spec(w),
                  pl.BlockSpec((tm, D), lambda i: (i, 0))],
        out_specs=[pl.BlockSpec((tm, D), lambda i: (i, 0))],
        out_shape=[S((t, D), F32)], name=name, sem=("parallel",))
    return out, rid


def _dx_norm(dparts, w, x, g, dres, name, tm_pref, riders=()):
    t = x.shape[0]
    nc = w.shape[2]
    tm = _row_tile(t, tm_pref)
    widths = [p.shape[1] for p in dparts]
    segs = _segments(widths, nc)
    n = len(dparts)

    def body(*refs):
        d_refs = refs[:n]
        w_ref, x_ref, g_ref, r_ref, dx_ref, dg_ref = refs[n:]
        dh = jnp.zeros((tm, D), F32)
        for (p, po, j, jo, wd) in segs:
            dh = dh + _dot_nt(d_refs[p][:, po:po + wd], w_ref[j, :, jo:jo + wd])
        xv = x_ref[...]
        dx, dgrow = _rms_bwd(dh, xv, _rms_r(xv), g_ref[...])
        dx_ref[...] = r_ref[...] + dx
        _acc_out(dg_ref, jnp.sum(dgrow, axis=0, keepdims=True), pl.program_id(0) == 0)

    (dx, dg), rid = _pcall(
        body, [*dparts, w, x, g, dres], riders, grid=(t // tm,),
        in_specs=[pl.BlockSpec((tm, wd), lambda i: (i, 0)) for wd in widths]
        + [_wcol_spec(w), pl.BlockSpec((tm, D), lambda i: (i, 0)),
           pl.BlockSpec((1, D), lambda i: (0, 0)), pl.BlockSpec((tm, D), lambda i: (i, 0))],
        out_specs=[pl.BlockSpec((tm, D), lambda i: (i, 0)), pl.BlockSpec((1, D), lambda i: (0, 0))],
        out_shape=[S((t, D), F32), S((1, D), F32)], name=name, sem=("arbitrary",))
    return dx, dg, rid


def _dx_parts(dy, w, widths, name):
    t = dy.shape[0]
    tm = _row_tile(t, 512)
    offs = np.cumsum([0] + list(widths))
    n = len(widths)

    def body(dy_ref, w_ref, *o_refs):
        dyv = _mx(dy_ref[...])
        for p in range(n):
            o_refs[p][...] = _dot_nt(dyv, w_ref[int(offs[p]):int(offs[p + 1]), :])

    return pl.pallas_call(
        body, grid=(t // tm,),
        in_specs=[pl.BlockSpec((tm, D), lambda i: (i, 0)), _wrow_spec(w)],
        out_specs=[pl.BlockSpec((tm, wd), lambda i: (i, 0)) for wd in widths],
        out_shape=[S((t, wd), F32) for wd in widths], name=name, compiler_params=_cp("parallel"))(dy, w)


def _dx_swiglu(dy, w, ff, name):
    t = dy.shape[0]
    tm = _row_tile(t, 256)

    def body(dy_ref, w_ref, f_ref, o_ref):
        dact = _dot_nt(dy_ref[...], w_ref[...])
        gate = f_ref[:, :FFN_H].astype(F32)
        up = f_ref[:, FFN_H:].astype(F32)
        s = _sigmoid(gate)
        o_ref[:, :FFN_H] = (dact * up * (s * (1.0 + gate * (1.0 - s)))).astype(o_ref.dtype)
        o_ref[:, FFN_H:] = (dact * (gate * s)).astype(o_ref.dtype)

    return pl.pallas_call(
        body, grid=(t // tm,),
        in_specs=[pl.BlockSpec((tm, D), lambda i: (i, 0)), _wrow_spec(w),
                  pl.BlockSpec((tm, 2 * FFN_H), lambda i: (i, 0))],
        out_specs=pl.BlockSpec((tm, 2 * FFN_H), lambda i: (i, 0)),
        out_shape=S((t, 2 * FFN_H), ACT_DTYPE), name=name, compiler_params=_cp("parallel"))(dy, w, ff)


def _call_into(body, into, in_specs, args, *, n_prefetch, grid, out_specs, **kw):
    n_in = len(args)
    if into is None:
        gs = pltpu.PrefetchScalarGridSpec(num_scalar_prefetch=n_prefetch, grid=grid, in_specs=in_specs,
                                          out_specs=out_specs)
        return pl.pallas_call(body, grid_spec=gs, **kw)(*args)

    def wrapped(*refs):
        return body(*refs[:n_in], *refs[n_in + 1:])

    gs = pltpu.PrefetchScalarGridSpec(num_scalar_prefetch=n_prefetch, grid=grid,
                                      in_specs=list(in_specs) + [ANY], out_specs=out_specs)
    return pl.pallas_call(wrapped, grid_spec=gs, input_output_aliases={n_in: 0}, **kw)(*args, into)


def _dw_norm_parts(x, g, dparts, nc, name):
    t = x.shape[0]
    tk = _row_tile(t, 1024)
    widths = [p.shape[1] for p in dparts]
    segs = _segments(widths, nc)
    n = len(dparts)
    nk = t // tk

    def body(*refs):
        x_ref, g_ref = refs[0], refs[1]
        d_refs = refs[2:2 + n]
        o_ref, acc_ref = refs[2 + n], refs[3 + n]
        k = pl.program_id(0)
        xv = x_ref[...]
        h = _mx(xv * _rms_r(xv) * g_ref[...])

        @pl.when(k == 0)
        def _():
            acc_ref[...] = jnp.zeros_like(acc_ref)

        for (p, po, j, jo, wd) in segs:
            acc_ref[j, :, jo:jo + wd] += _dot_tn(h, d_refs[p][:, po:po + wd])

        @pl.when(k == nk - 1)
        def _():
            o_ref[...] = acc_ref[...].astype(o_ref.dtype)

    return pl.pallas_call(
        body, grid=(nk,),
        in_specs=[pl.BlockSpec((tk, D), lambda k: (k, 0)), pl.BlockSpec((1, D), lambda k: (0, 0))]
        + [pl.BlockSpec((tk, wd), lambda k: (k, 0)) for wd in widths],
        out_specs=pl.BlockSpec((N_CHIPS, D, nc), lambda k: (0, 0, 0)),
        out_shape=S((N_CHIPS, D, nc), MXU_DTYPE), name=name,
        scratch_shapes=[pltpu.VMEM((N_CHIPS, D, nc), F32)], compiler_params=_cp("arbitrary"))(x, g, *dparts)


def _dw_norm_cols(x, g, dy, nc, name, riders=()):
    t = x.shape[0]
    tk = _row_tile(t, 1024)
    nk = t // tk

    def body(x_ref, g_ref, dy_ref, o_ref, acc_ref):
        k = pl.program_id(1)
        xv = x_ref[...]
        h = _mx(xv * _rms_r(xv) * g_ref[...])

        @pl.when(k == 0)
        def _():
            acc_ref[...] = jnp.zeros_like(acc_ref)

        acc_ref[...] += _dot_tn(h, dy_ref[...])

        @pl.when(k == nk - 1)
        def _():
            o_ref[...] = acc_ref[...].astype(o_ref.dtype)

    (out,), rid = _pcall(
        body, [x, g, dy], riders, grid=(N_CHIPS, nk),
        in_specs=[pl.BlockSpec((tk, D), lambda j, k: (k, 0)), pl.BlockSpec((1, D), lambda j, k: (0, 0)),
                  pl.BlockSpec((tk, nc), lambda j, k: (k, j))],
        out_specs=[pl.BlockSpec((None, D, nc), lambda j, k: (j, 0, 0))],
        out_shape=[S((N_CHIPS, D, nc), MXU_DTYPE)], name=name, sem=("parallel", "arbitrary"),
        scratch_shapes=[pltpu.VMEM((D, nc), F32)])
    return out, rid


def _dw_parts(parts, dy, name):
    t = dy.shape[0]
    tk = _row_tile(t, 1024)
    widths = [p.shape[1] for p in parts]
    offs = np.cumsum([0] + widths)
    ktot = int(offs[-1])
    n = len(parts)
    nk = t // tk

    def body(*refs):
        p_refs, dy_ref, o_ref, acc_ref = refs[:n], refs[n], refs[n + 1], refs[n + 2]
        k = pl.program_id(0)

        @pl.when(k == 0)
        def _():
            acc_ref[...] = jnp.zeros_like(acc_ref)

        dyv = _mx(dy_ref[...])
        for p in range(n):
            acc_ref[int(offs[p]):int(offs[p + 1]), :] += _dot_tn(p_refs[p][...], dyv)

        @pl.when(k == nk - 1)
        def _():
            o_ref[...] = acc_ref[...].astype(o_ref.dtype)

    return pl.pallas_call(
        body, grid=(nk,),
        in_specs=[pl.BlockSpec((tk, wd), lambda k: (k, 0)) for wd in widths]
        + [pl.BlockSpec((tk, D), lambda k: (k, 0))],
        out_specs=pl.BlockSpec((ktot, D), lambda k: (0, 0)),
        out_shape=S((ktot, D), MXU_DTYPE), name=name,
        scratch_shapes=[pltpu.VMEM((ktot, D), F32)], compiler_params=_cp("arbitrary"))(*parts, dy)


def _dw_swiglu(ff, dy, name):
    t = dy.shape[0]
    tk = _row_tile(t, 512)
    nk = t // tk

    def body(f_ref, dy_ref, o_ref, acc_ref):
        k = pl.program_id(0)

        @pl.when(k == 0)
        def _():
            acc_ref[...] = jnp.zeros_like(acc_ref)

        acc_ref[...] += _dot_tn(_swiglu(f_ref[...]), dy_ref[...])

        @pl.when(k == nk - 1)
        def _():
            o_ref[...] = acc_ref[...].astype(o_ref.dtype)

    return pl.pallas_call(
        body, grid=(nk,),
        in_specs=[pl.BlockSpec((tk, 2 * FFN_H), lambda k: (k, 0)), pl.BlockSpec((tk, D), lambda k: (k, 0))],
        out_specs=pl.BlockSpec((FFN_H, D), lambda k: (0, 0)),
        out_shape=S((FFN_H, D), MXU_DTYPE), name=name,
        scratch_shapes=[pltpu.VMEM((FFN_H, D), F32)], compiler_params=_cp("arbitrary"))(ff, dy)


def _tables(t):
    pos = jnp.arange(t, dtype=F32)
    half = HD // 2
    inv_freq = ROPE_BASE ** (-jnp.arange(half, dtype=F32) / half)
    ang = pos[:, None] * inv_freq[None, :]
    cos, sin = jnp.cos(ang), jnp.sin(ang)
    tb = {"cos2": jnp.concatenate([cos, cos], axis=1), "sin2": jnp.concatenate([-sin, sin], axis=1)}
    gf = 1.0 - jnp.exp2(-5.0 - jnp.arange(HEADS, dtype=F32))
    lgf = jnp.log(gf)[:, None]
    lgb = jnp.log(gf[::-1])[:, None]
    idx = jnp.arange(CH, dtype=F32)
    diff = idx[:, None] - idx[None, :]
    dfwd = jnp.where(diff >= 0, jnp.exp(lgf[:, :, None] * jnp.where(diff >= 0, diff, 0.0)), 0.0)
    dbwd = jnp.where(diff < 0, jnp.exp(lgb[:, :, None] * jnp.where(diff < 0, -diff, 0.0)), 0.0)
    tb["dm"] = dfwd + dbwd
    tb["dmt"] = jnp.swapaxes(tb["dm"], 1, 2)

    def lanes(a):
        return jnp.repeat(a.T, HD, axis=1)

    tb["xif"] = lanes(jnp.exp(lgf * (idx + 1)))
    tb["zf"] = lanes(jnp.exp(lgf * (CH - 1 - idx)))
    tb["xib"] = lanes(jnp.exp(lgb * (CH - idx)))
    tb["zb"] = lanes(jnp.exp(lgb * idx))
    tb["gcf"] = jnp.repeat(jnp.exp(lgf * CH), HD, axis=0).reshape(1, HEADS * HD)
    tb["gcb"] = jnp.repeat(jnp.exp(lgb * CH), HD, axis=0).reshape(1, HEADS * HD)
    return tb


def _full(shape):
    nd = len(shape)
    return pl.BlockSpec(shape, lambda *_: (0,) * nd)


def _gm_mixed(vn, ws_ref, bias):
    lane = lax.broadcasted_iota(jnp.int32, (CH, 128), 1)
    halves = []
    for hf in range(2):
        vh = _mx(vn[:, hf * 128:(hf + 1) * 128])
        r0 = jnp.dot(_mx(ws_ref[2 * hf]), vh, preferred_element_type=F32)
        r1 = jnp.dot(_mx(ws_ref[2 * hf + 1]), vh, preferred_element_type=F32)
        halves.append(jnp.where(lane < 64, r0, r1))
    return jnp.concatenate(halves, axis=1) + bias


def _gm_fwd(proj, ln_g, ln_b, ws, bias, name, riders=()):
    t = proj.shape[0]
    tm = _row_tile(t, 512)

    def body(pu_ref, pv_ref, g_ref, b_ref, ws_ref, bias_ref, o_ref):
        for c in range(tm // CH):
            rows = slice(c * CH, (c + 1) * CH)
            u = _gelu(pu_ref[rows, :])
            o, _ = _standardize(_gelu(pv_ref[rows, :]))
            vn = o * g_ref[...] + b_ref[...]
            o_ref[rows, :] = (u * _gm_mixed(vn, ws_ref, bias_ref[...])).astype(o_ref.dtype)

    (out,), rid = _pcall(
        body, [proj, proj, ln_g, ln_b, ws, bias], riders, grid=(t // tm,),
        in_specs=[pl.BlockSpec((tm, GM_W), lambda i: (i, 0)), pl.BlockSpec((tm, GM_W), lambda i: (i, 1)),
                  _full((1, GM_W)), _full((1, GM_W)), _full((GM_HEADS, CH, CH)), _full((CH, GM_W))],
        out_specs=[pl.BlockSpec((tm, GM_W), lambda i: (i, 0))],
        out_shape=[S((t, GM_W), ACT_DTYPE)], name=name, sem=("parallel",))
    return out, rid


def _gm_bwd(proj, dy, ln_g, ln_b, ws, wst, bias, name):
    t = proj.shape[0]
    tm = _row_tile(t, 512)
    nb = t // tm

    def body(pu_ref, pv_ref, dy_ref, g_ref, b_ref, ws_ref, wst_ref, bias_ref,
             d_ref, dws_ref, dbs_ref, dg_ref, db_ref, dbias_ref):
        first = pl.program_id(0) == 0
        lane = lax.broadcasted_iota(jnp.int32, (CH, 128), 1)
        dws = [jnp.zeros((CH, CH), F32) for _ in range(GM_HEADS)]
        dbias = jnp.zeros((CH, GM_W), F32)
        dg = jnp.zeros((1, GM_W), F32)
        db = jnp.zeros((1, GM_W), F32)
        for c in range(tm // CH):
            rows = slice(c * CH, (c + 1) * CH)
            pu = pu_ref[rows, :]
            pv = pv_ref[rows, :]
            u = _gelu(pu)
            o, r = _standardize(_gelu(pv))
            vn = o * g_ref[...] + b_ref[...]
            mixed = _gm_mixed(vn, ws_ref, bias_ref[...])
            dyv = dy_ref[rows, :]
            d_ref[rows, :GM_W] = (dyv * mixed * _gelu_grad(pu)).astype(d_ref.dtype)
            dmixed = dyv * u
            dbias = dbias + dmixed
            dvn_halves = []
            for hf in range(2):
                dm = dmixed[:, hf * 128:(hf + 1) * 128]
                vh = vn[:, hf * 128:(hf + 1) * 128]
                dm0 = jnp.where(lane < 64, dm, 0.0)
                dm1 = dm - dm0
                dws[2 * hf] = dws[2 * hf] + _dot_nt(dm0, vh)
                dws[2 * hf + 1] = dws[2 * hf + 1] + _dot_nt(dm1, vh)
                t0 = _dot(wst_ref[2 * hf], dm)
                t1 = _dot(wst_ref[2 * hf + 1], dm)
                dvn_halves.append(jnp.where(lane < 64, t0, t1))
            dvn = jnp.concatenate(dvn_halves, axis=1)
            dg = dg + jnp.sum(dvn * o, axis=0, keepdims=True)
            db = db + jnp.sum(dvn, axis=0, keepdims=True)
            dv = _standardize_bwd(dvn * g_ref[...], o, r)
            d_ref[rows, GM_W:] = (dv * _gelu_grad(pv)).astype(d_ref.dtype)
        for h in range(GM_HEADS):
            _acc_out(dws_ref.at[h], dws[h], first)
        _acc_out(dbias_ref, dbias, first)
        _acc_out(dg_ref, dg, first)
        _acc_out(db_ref, db, first)

        @pl.when(pl.program_id(0) == nb - 1)
        def _():
            tot = dbias_ref[...]
            head = lax.broadcasted_iota(jnp.int32, (CH, GM_W), 1) // (GM_W // GM_HEADS)
            out = jnp.zeros((CH, 128), F32)
            for h in range(GM_HEADS):
                s = jnp.sum(jnp.where(head == h, tot, 0.0), axis=1, keepdims=True)
                out = jnp.where(lane == h, s, out)
            dbs_ref[...] = out

    return pl.pallas_call(
        body, grid=(nb,),
        in_specs=[pl.BlockSpec((tm, GM_W), lambda i: (i, 0)), pl.BlockSpec((tm, GM_W), lambda i: (i, 1)),
                  pl.BlockSpec((tm, GM_W), lambda i: (i, 0)),
                  _full((1, GM_W)), _full((1, GM_W)), _full((GM_HEADS, CH, CH)), _full((GM_HEADS, CH, CH)),
                  _full((CH, GM_W))],
        out_specs=[pl.BlockSpec((tm, 2 * GM_W), lambda i: (i, 0)), _full((GM_HEADS, CH, CH)), _full((CH, 128)),
                   _full((1, GM_W)), _full((1, GM_W))],
        out_shape=[S((t, 2 * GM_W), ACT_DTYPE), S((GM_HEADS, CH, CH), F32), S((CH, 128), F32),
                   S((1, GM_W), F32), S((1, GM_W), F32)],
        scratch_shapes=[pltpu.VMEM((CH, GM_W), F32)],
        name=name, compiler_params=_cp("arbitrary"))(proj, proj, dy, ln_g, ln_b, ws, wst, bias)


def _rot(x, cos2, sin2):
    return x * cos2 + pltpu.roll(x, HD // 2, 1) * sin2


def _rot_bwd(dx, cos2, sin2):
    return dx * cos2 + pltpu.roll(dx * sin2, HD // 2, 1)


def _rotary(proj, cos2, sin2, name):
    t = proj.shape[0]
    tm = _row_tile(t, 512)
    scale = HD ** -0.5

    def body(q_ref, k_ref, c_ref, s_ref, rq_ref, rk_ref):
        c, s = c_ref[...], s_ref[...]
        for h in range(HEADS):
            cols = slice(h * HD, (h + 1) * HD)
            rq_ref[:, cols] = _rot(q_ref[:, cols], c, s)
            rk_ref[:, cols] = _rot(k_ref[:, cols], c, s) * scale

    return pl.pallas_call(
        body, grid=(t // tm,),
        in_specs=[pl.BlockSpec((tm, RET_W), lambda i: (i, 1)), pl.BlockSpec((tm, RET_W), lambda i: (i, 2)),
                  pl.BlockSpec((tm, HD), lambda i: (i, 0)), pl.BlockSpec((tm, HD), lambda i: (i, 0))],
        out_specs=[pl.BlockSpec((tm, RET_W), lambda i: (i, 0))] * 2,
        out_shape=[S((t, RET_W), F32)] * 2, name=name, compiler_params=_cp("parallel"))(proj, proj, cos2, sin2)


def _ret_scan(lhs, rhs, rhs_col, lp, ls, gp, gs, name):
    t = lhs.shape[0]
    n = t // CH
    r = 4 if n % 4 == 0 else 1
    ns = n // r

    def body(lp_ref, ls_ref, gp_ref, gs_ref, l1_ref, r1_ref, l2_ref, r2_ref, pre_ref, suf_ref, sp_ref, ss_ref):
        @pl.when(pl.program_id(0) == 0)
        def _():
            sp_ref[...] = jnp.zeros_like(sp_ref)
            ss_ref[...] = jnp.zeros_like(ss_ref)

        def kv(l_ref, r_ref, scale, rows):
            lv = l_ref[rows, :] * scale
            rv = r_ref[rows, :]
            return jnp.concatenate([_dot_tn(lv[:, h * HD:(h + 1) * HD], rv[:, h * HD:(h + 1) * HD])
                                    for h in range(HEADS)], axis=1)

        for j in range(r):
            pre_ref[j] = sp_ref[...]
            sp_ref[...] = sp_ref[...] * gp_ref[...] + kv(l1_ref, r1_ref, lp_ref[...], slice(j * CH, (j + 1) * CH))
        for j in reversed(range(r)):
            suf_ref[j] = ss_ref[...]
            ss_ref[...] = ss_ref[...] * gs_ref[...] + kv(l2_ref, r2_ref, ls_ref[...], slice(j * CH, (j + 1) * CH))

    w = HEADS * HD
    return pl.pallas_call(
        body, grid=(ns,),
        in_specs=[_full((CH, w)), _full((CH, w)), _full((1, w)), _full((1, w)),
                  pl.BlockSpec((r * CH, w), lambda s: (s, 0)), pl.BlockSpec((r * CH, w), lambda s: (s, rhs_col)),
                  pl.BlockSpec((r * CH, w), lambda s: (ns - 1 - s, 0)),
                  pl.BlockSpec((r * CH, w), lambda s: (ns - 1 - s, rhs_col))],
        out_specs=[pl.BlockSpec((r, HD, w), lambda s: (s, 0, 0)), pl.BlockSpec((r, HD, w), lambda s: (ns - 1 - s, 0, 0))],
        out_shape=[S((n, HD, w), F32)] * 2, name=name,
        scratch_shapes=[pltpu.VMEM((HD, w), F32), pltpu.VMEM((HD, w), F32)],
        compiler_params=_cp("arbitrary"))(lp, ls, gp, gs, lhs, rhs, lhs, rhs)


def _ret_out(rq, rk, proj, sf, sb, tb, name, riders=()):
    t = rq.shape[0]
    r = 2 if (t // CH) % 2 == 0 else 1
    tm = r * CH
    w = HEADS * HD

    def body(rq_ref, rk_ref, v_ref, g_ref, sf_ref, sb_ref, dm_ref, xif_ref, xib_ref, a_ref, y_ref):
        for c in range(r):
            rows = slice(c * CH, (c + 1) * CH)
            for h in range(HEADS):
                cols = slice(h * HD, (h + 1) * HD)
                q = rq_ref[rows, cols]
                p = _dot_nt(q, rk_ref[rows, cols]) * dm_ref[h]
                a = (_dot(p, v_ref[rows, cols]) + _dot(q * xif_ref[:, cols], sf_ref[c, :, cols])
                     + _dot(q * xib_ref[:, cols], sb_ref[c, :, cols]))
                a_ref[rows, cols] = a
                o, _ = _standardize(a)
                gv = g_ref[rows, cols]
                y_ref[rows, cols] = (o * (gv * _sigmoid(gv))).astype(y_ref.dtype)

    (a, y), rid = _pcall(
        body, [rq, rk, proj, proj, sf, sb, tb["dm"], tb["xif"], tb["xib"]], riders, grid=(t // tm,),
        in_specs=[pl.BlockSpec((tm, w), lambda i: (i, 0)), pl.BlockSpec((tm, w), lambda i: (i, 0)),
                  pl.BlockSpec((tm, w), lambda i: (i, 3)), pl.BlockSpec((tm, w), lambda i: (i, 4)),
                  pl.BlockSpec((r, HD, w), lambda i: (i, 0, 0)), pl.BlockSpec((r, HD, w), lambda i: (i, 0, 0)),
                  _full((HEADS, CH, CH)), _full((CH, w)), _full((CH, w))],
        out_specs=[pl.BlockSpec((tm, w), lambda i: (i, 0))] * 2,
        out_shape=[S((t, w), F32), S((t, w), ACT_DTYPE)], name=name, sem=("parallel",))
    return a, y, rid


def _ret_bwd_pre(dy, a, proj, name):
    t = dy.shape[0]
    tm = _row_tile(t, 512)
    w = HEADS * HD

    def body(dy_ref, a_ref, g_ref, da_ref, dg_ref):
        for h in range(HEADS):
            cols = slice(h * HD, (h + 1) * HD)
            o, r = _standardize(a_ref[:, cols])
            gv = g_ref[:, cols]
            s = _sigmoid(gv)
            dyv = dy_ref[:, cols]
            dg_ref[:, cols] = (dyv * o * (s * (1.0 + gv * (1.0 - s)))).astype(dg_ref.dtype)
            da_ref[:, cols] = _standardize_bwd(dyv * (gv * s), o, r).astype(da_ref.dtype)

    return pl.pallas_call(
        body, grid=(t // tm,),
        in_specs=[pl.BlockSpec((tm, w), lambda i: (i, 0)), pl.BlockSpec((tm, w), lambda i: (i, 0)),
                  pl.BlockSpec((tm, w), lambda i: (i, 4))],
        out_specs=[pl.BlockSpec((tm, w), lambda i: (i, 0))] * 2,
        out_shape=[S((t, w), ACT_DTYPE)] * 2, name=name, compiler_params=_cp("parallel"))(dy, a, proj)


def _ret_bwd_main(rq, rk, proj, da, sf, sb, gf, gb, tb, name):
    t = rq.shape[0]
    r = 2 if (t // CH) % 2 == 0 else 1
    tm = r * CH
    w = HEADS * HD
    scale = HD ** -0.5

    def body(rq_ref, rk_ref, v_ref, da_ref, sf_ref, sb_ref, gf_ref, gb_ref, dm_ref, dmt_ref,
             xif_ref, xib_ref, zf_ref, zb_ref, c_ref, s_ref, o_ref):
        for c in range(r):
            rows = slice(c * CH, (c + 1) * CH)
            cos2, sin2 = c_ref[rows, :], s_ref[rows, :]
            for h in range(HEADS):
                cols = slice(h * HD, (h + 1) * HD)
                q, k, v, dav = rq_ref[rows, cols], rk_ref[rows, cols], v_ref[rows, cols], da_ref[rows, cols]
                qm, km, vm, dam = _mx(q), _mx(k), _mx(v), _mx(dav)
                dm, dmt = dm_ref[h], dmt_ref[h]
                pt = _dot_nt(km, qm) * dmt
                dp = _dot_nt(dam, vm) * dm
                dpt = _dot_nt(vm, dam) * dmt
                sfh, sbh, gfh, gbh = sf_ref[c, :, cols], sb_ref[c, :, cols], gf_ref[c, :, cols], gb_ref[c, :, cols]
                zf, zb = zf_ref[:, cols], zb_ref[:, cols]
                dv = _dot(pt, dam) + zf * _dot(km, gfh) + zb * _dot(km, gbh)
                drq = _dot(dp, km) + xif_ref[:, cols] * _dot_nt(dam, sfh) + xib_ref[:, cols] * _dot_nt(dam, sbh)
                drk = _dot(dpt, qm) + _dot_nt(zf * v, gfh) + _dot_nt(zb * v, gbh)
                o_ref[rows, h * HD:(h + 1) * HD] = _rot_bwd(drq, cos2, sin2).astype(o_ref.dtype)
                o_ref[rows, w + h * HD:w + (h + 1) * HD] = (_rot_bwd(drk, cos2, sin2) * scale).astype(o_ref.dtype)
                o_ref[rows, 2 * w + h * HD:2 * w + (h + 1) * HD] = dv.astype(o_ref.dtype)

    st = pl.BlockSpec((r, HD, w), lambda i: (i, 0, 0))
    return pl.pallas_call(
        body, grid=(t // tm,),
        in_specs=[pl.BlockSpec((tm, w), lambda i: (i, 0)), pl.BlockSpec((tm, w), lambda i: (i, 0)),
                  pl.BlockSpec((tm, w), lambda i: (i, 3)), pl.BlockSpec((tm, w), lambda i: (i, 0)),
                  st, st, st, st, _full((HEADS, CH, CH)), _full((HEADS, CH, CH)),
                  _full((CH, w)), _full((CH, w)), _full((CH, w)), _full((CH, w)),
                  pl.BlockSpec((tm, HD), lambda i: (i, 0)), pl.BlockSpec((tm, HD), lambda i: (i, 0))],
        out_specs=pl.BlockSpec((tm, 3 * w), lambda i: (i, 0)),
        out_shape=S((t, 3 * w), ACT_DTYPE), name=name,
        compiler_params=_cp("parallel"))(rq, rk, proj, da, sf, sb, gf, gb, tb["dm"], tb["dmt"],
                                         tb["xif"], tb["xib"], tb["zf"], tb["zb"], tb["cos2"], tb["sin2"])


CONV_TM = 256
CONV_SUB = 64
A_COL = (2 * GM_W + 4 * RET_W) // CV_W
G_COL = A_COL + 1


def _halo_specs(t, tm, col):
    nb16 = t // HALO
    per = tm // HALO
    return [pl.BlockSpec((tm, CV_W), lambda i: (i, col)),
            pl.BlockSpec((HALO, CV_W), lambda i: (jnp.maximum(i * per - 1, 0), col)),
            pl.BlockSpec((HALO, CV_W), lambda i: (jnp.minimum((i + 1) * per, nb16 - 1), col))]


def _fill_padded(dst_ref, prev, main, nxt, tm, i, nb):
    dst_ref[0:HALO, :] = jnp.where(i > 0, prev, 0.0)
    dst_ref[HALO:HALO + tm, :] = main
    dst_ref[HALO + tm:2 * HALO + tm, :] = jnp.where(i < nb - 1, nxt, 0.0)


SUBLANES = 8


def _fill_shifted(sh_ref, src_ref, tm):
    n = tm + 2 * HALO - SUBLANES
    for b in range(SUBLANES):
        sh_ref[b, 0:n, :] = src_ref[pl.ds(b, n), :]


def _tap(sh_ref, off, rows):
    return sh_ref[off % SUBLANES, pl.ds(off - off % SUBLANES, rows), :]


def _conv_fwd(proj, cw, cb, ln_g, ln_b, name, riders=()):
    t = proj.shape[0]
    tm = _row_tile(t, CONV_TM)
    nb = t // tm

    def body(a_ref, ap_ref, an_ref, g_ref, gp_ref, gn_ref, w_ref, b_ref, lg_ref, lb_ref, y_ref, hc_ref,
             hp_ref, sh_ref):
        i = pl.program_id(0)
        _fill_padded(hp_ref, ap_ref[...] * _sigmoid(gp_ref[...]), a_ref[...] * _sigmoid(g_ref[...]),
                     an_ref[...] * _sigmoid(gn_ref[...]), tm, i, nb)
        _fill_shifted(sh_ref, hp_ref, tm)
        for sb in range(tm // CONV_SUB):
            acc = jnp.zeros((CONV_SUB, CV_W), F32) + b_ref[...]
            for k in range(KCONV):
                acc = acc + w_ref[k:k + 1, :] * _tap(sh_ref, sb * CONV_SUB + k + 1, CONV_SUB)
            rows = slice(sb * CONV_SUB, (sb + 1) * CONV_SUB)
            hc_ref[rows, :] = acc
            o, _ = _standardize(acc)
            z = o * lg_ref[...] + lb_ref[...]
            y_ref[rows, :] = (z * _sigmoid(z)).astype(y_ref.dtype)

    (y, hc), rid = _pcall(
        body, [proj, proj, proj, proj, proj, proj, cw, cb, ln_g, ln_b], riders, grid=(nb,),
        in_specs=_halo_specs(t, tm, A_COL) + _halo_specs(t, tm, G_COL)
        + [_full((32, CV_W)), _full((1, CV_W)), _full((1, CV_W)), _full((1, CV_W))],
        out_specs=[pl.BlockSpec((tm, CV_W), lambda i: (i, 0))] * 2,
        out_shape=[S((t, CV_W), ACT_DTYPE), S((t, CV_W), F32)], name=name, sem=("parallel",),
        scratch_shapes=[pltpu.VMEM((tm + 2 * HALO, CV_W), F32), pltpu.VMEM((SUBLANES, tm + 2 * HALO, CV_W), F32)])
    return y, hc, rid


def _conv_bwd(proj, dy, hc, cw, ln_g, ln_b, name, riders=()):
    t = proj.shape[0]
    tm = _row_tile(t, CONV_TM)
    nb = t // tm

    def body(a_ref, ap_ref, an_ref, g_ref, gp_ref, gn_ref, dy_ref, dyp_ref, dyn_ref, hc_ref, hcp_ref, hcn_ref,
             w_ref, lg_ref, lb_ref, d_ref, dw_ref, dcb_ref, dlg_ref, dlb_ref, hp_ref, dhp_ref, dwacc_ref,
             sh_ref, dsh_ref):
        i = pl.program_id(0)
        first = i == 0

        def dhc_of(dyv, hcv):
            o, r = _standardize(hcv)
            z = o * lg_ref[...] + lb_ref[...]
            s = _sigmoid(z)
            dz = dyv * (s * (1.0 + z * (1.0 - s)))
            return _standardize_bwd(dz * lg_ref[...], o, r), dz, o

        dhc, dz, o = dhc_of(dy_ref[...], hc_ref[...])
        _acc_out(dlg_ref, jnp.sum(dz * o, axis=0, keepdims=True), first)
        _acc_out(dlb_ref, jnp.sum(dz, axis=0, keepdims=True), first)
        _acc_out(dcb_ref, jnp.sum(dhc, axis=0, keepdims=True), first)
        _fill_padded(dhp_ref, dhc_of(dyp_ref[...], hcp_ref[...])[0], dhc, dhc_of(dyn_ref[...], hcn_ref[...])[0],
                     tm, i, nb)
        _fill_padded(hp_ref, ap_ref[...] * _sigmoid(gp_ref[...]), a_ref[...] * _sigmoid(g_ref[...]),
                     an_ref[...] * _sigmoid(gn_ref[...]), tm, i, nb)

        _fill_shifted(sh_ref, hp_ref, tm)
        _fill_shifted(dsh_ref, dhp_ref, tm)

        @pl.when(first)
        def _():
            dwacc_ref[...] = jnp.zeros_like(dwacc_ref)

        for sb in range(tm // CONV_SUB):
            base = sb * CONV_SUB
            dmain = dhp_ref[pl.ds(HALO + base, CONV_SUB), :]
            dh = jnp.zeros((CONV_SUB, CV_W), F32)
            for k in range(KCONV):
                dh = dh + w_ref[k:k + 1, :] * _tap(dsh_ref, base + 2 * HALO - 1 - k, CONV_SUB)
                prod = dmain * _tap(sh_ref, base + k + 1, CONV_SUB)
                dwacc_ref[k * 8:(k + 1) * 8, :] += jnp.sum(prod.reshape(CONV_SUB // 8, 8, CV_W), axis=0)
            rows = slice(base, base + CONV_SUB)
            s = _sigmoid(g_ref[rows, :])
            d_ref[rows, :CV_W] = (dh * s).astype(d_ref.dtype)
            d_ref[rows, CV_W:] = (dh * a_ref[rows, :] * (s * (1.0 - s))).astype(d_ref.dtype)

        @pl.when(i == nb - 1)
        def _():
            for k in range(KCONV):
                dw_ref[k:k + 1, :] = jnp.sum(dwacc_ref[k * 8:(k + 1) * 8, :], axis=0, keepdims=True)
            dw_ref[KCONV:32, :] = jnp.zeros((32 - KCONV, CV_W), F32)

    hs = [pl.BlockSpec((tm, CV_W), lambda i: (i, 0)),
          pl.BlockSpec((HALO, CV_W), lambda i: (jnp.maximum(i * (tm // HALO) - 1, 0), 0)),
          pl.BlockSpec((HALO, CV_W), lambda i: (jnp.minimum((i + 1) * (tm // HALO), t // HALO - 1), 0))]
    outs, rid = _pcall(
        body, [proj, proj, proj, proj, proj, proj, dy, dy, dy, hc, hc, hc, cw, ln_g, ln_b], riders, grid=(nb,),
        in_specs=_halo_specs(t, tm, A_COL) + _halo_specs(t, tm, G_COL) + hs + hs
        + [_full((32, CV_W)), _full((1, CV_W)), _full((1, CV_W))],
        out_specs=[pl.BlockSpec((tm, 2 * CV_W), lambda i: (i, 0)), _full((32, CV_W)), _full((1, CV_W)),
                   _full((1, CV_W)), _full((1, CV_W))],
        out_shape=[S((t, 2 * CV_W), ACT_DTYPE), S((32, CV_W), F32), S((1, CV_W), F32), S((1, CV_W), F32),
                   S((1, CV_W), F32)],
        name=name, sem=("arbitrary",),
        scratch_shapes=[pltpu.VMEM((tm + 2 * HALO, CV_W), F32), pltpu.VMEM((tm + 2 * HALO, CV_W), F32),
                        pltpu.VMEM((32 * 8, CV_W), F32), pltpu.VMEM((SUBLANES, tm + 2 * HALO, CV_W), F32),
                        pltpu.VMEM((SUBLANES, tm + 2 * HALO, CV_W), F32)])
    return (*outs, rid)


def _loss_head(x, g, target, name):
    t = x.shape[0]
    tm = _row_tile(t, 512)

    def body(x_ref, g_ref, t_ref, dx_ref, dg_ref, l_ref):
        first = pl.program_id(0) == 0
        xv = x_ref[...]
        r = _rms_r(xv)
        e = xv * r * g_ref[...] - t_ref[...]
        dx, dgrow = _rms_bwd(e * (1.0 / D), xv, r, g_ref[...])
        dx_ref[...] = dx
        _acc_out(dg_ref, jnp.sum(dgrow, axis=0, keepdims=True), first)
        part = 0.5 * jnp.sum(jnp.mean(e * e, axis=-1, keepdims=True), axis=0, keepdims=True)
        _acc_out(l_ref, jnp.broadcast_to(part, (8, 128)), first)

    return pl.pallas_call(
        body, grid=(t // tm,),
        in_specs=[pl.BlockSpec((tm, D), lambda i: (i, 0)), _full((1, D)), pl.BlockSpec((tm, D), lambda i: (i, 0))],
        out_specs=[pl.BlockSpec((tm, D), lambda i: (i, 0)), _full((1, D)), _full((8, 128))],
        out_shape=[S((t, D), F32), S((1, D), F32), S((8, 128), F32)], name=name,
        compiler_params=_cp("arbitrary"))(x, g, target)


def _as2d(a):
    return a.reshape(-1, a.shape[-1])


def _ew_tile(rows, cols, n_arrays):
    budget = VMEM_LIMIT // 2
    tr = rows
    while tr * cols * 4 * n_arrays * 2 > budget and tr % 16 == 0:
        tr //= 2
    assert rows % tr == 0
    return tr


def _adamw(w, g, m, v, name):
    shape = w.shape
    w2, g2, m2, v2 = _as2d(w), _as2d(g), _as2d(m), _as2d(v)
    rows, cols = w2.shape
    tr = _ew_tile(rows, cols, 7)

    def body(w_ref, g_ref, m_ref, v_ref, d_ref, nm_ref, nv_ref):
        gv = g_ref[...]
        nm = ADAM_B1 * m_ref[...] + (1.0 - ADAM_B1) * gv
        nv = ADAM_B2 * v_ref[...] + (1.0 - ADAM_B2) * (gv * gv)
        m_hat = nm / (1.0 - ADAM_B1 ** ADAM_STEP)
        v_hat = nv / (1.0 - ADAM_B2 ** ADAM_STEP)
        d_ref[...] = -ADAM_LR * (m_hat / (jnp.sqrt(v_hat) + ADAM_EPS) + ADAM_WD * w_ref[...])
        nm_ref[...] = nm
        nv_ref[...] = nv

    spec = pl.BlockSpec((tr, cols), lambda i: (i, 0))
    outs = pl.pallas_call(body, grid=(rows // tr,), in_specs=[spec] * 4, out_specs=[spec] * 3,
                          out_shape=[S((rows, cols), F32)] * 3, name=name,
                          compiler_params=_cp("parallel"))(w2, g2, m2, v2)
    return tuple(o.reshape(shape) for o in outs)


BIG = (("w_in", "col"), ("w_out", "row"), ("w_ffn_in", "col"), ("w_ffn_out", "row"))
NBIG = len(BIG)


def _cast_to_gathered(w, l, me, name):
    _, r_, c_ = w.shape
    tr = _ew_tile(r_, c_, 2)

    def body(me_ref, w_ref, o_ref):
        o_ref[...] = w_ref[...].astype(o_ref.dtype)

    gs = pltpu.PrefetchScalarGridSpec(
        num_scalar_prefetch=1, grid=(r_ // tr,),
        in_specs=[pl.BlockSpec((None, tr, c_), lambda i, s: (l, i, 0))],
        out_specs=pl.BlockSpec((None, tr, c_), lambda i, s: (s[0], i, 0)))
    out = pl.pallas_call(body, grid_spec=gs, out_shape=S((N_CHIPS, r_, c_), MXU_DTYPE), name=name,
                         compiler_params=_cp("parallel"))(me.reshape(1), w)
    return out.reshape(N_CHIPS, 2, r_ // 2, c_)


def _all_gather(bufs, name):
    n = len(bufs)

    def body(*refs):
        i_refs, o_refs = refs[:n], refs[n:2 * n]
        isend, irecv, dsend, drecv = refs[2 * n:]
        pos = _mesh_pos()
        ici = _rider_copies("ici", i_refs, o_refs, isend, irecv, pos)
        d2d = _rider_copies("d2d", o_refs, o_refs, dsend, drecv, pos)
        for cp, _ in ici:
            cp.start()
        for (_, land), (fwd, _) in zip(ici, d2d):
            land.wait_recv()
            fwd.start()
        for _, land in d2d:
            land.wait_recv()
        for cp, _ in ici + d2d:
            cp.wait_send()

    return pl.pallas_call(
        body, in_specs=[ANY] * n, out_specs=[ANY] * n, out_shape=[S(a.shape, a.dtype) for a in bufs],
        input_output_aliases={w: w for w in range(n)}, name=name,
        scratch_shapes=[pltpu.SemaphoreType.DMA((n, 3))] * 4)(*bufs)


def _pair_exchange(grads, name):
    n = len(grads)

    def body(*refs):
        g_refs, theirs = refs[:n], refs[n:2 * n]
        send, recv = refs[2 * n:]
        x, y, c, *_ = _mesh_pos()
        cps = []
        for w in range(n):
            cp = pltpu.make_async_remote_copy(
                src_ref=g_refs[w].at[:, 1 - c], dst_ref=theirs[w], send_sem=send.at[w], recv_sem=recv.at[w],
                device_id=(x, y, 1 - c), device_id_type=MESH)
            cp.start()
            cps.append(cp)
        for cp in cps:
            cp.wait()

    return pl.pallas_call(
        body, in_specs=[ANY] * n, out_specs=[ANY] * n,
        out_shape=[S(a.shape[:1] + a.shape[2:], a.dtype) for a in grads], name=name,
        scratch_shapes=[pltpu.SemaphoreType.DMA((n,))] * 2)(*grads)


def _pair_sum(g, theirs, core, name):
    _, _, rh, c_ = g.shape
    tr = _ew_tile(rh, c_, 2)

    def body(s_ref, g_ref, t_ref, o_ref):
        o_ref[...] = (g_ref[...].astype(F32) + t_ref[...].astype(F32)).astype(o_ref.dtype)

    blk = pl.BlockSpec((None, tr, c_), lambda j, i, s: (j, i, 0))
    gs = pltpu.PrefetchScalarGridSpec(
        num_scalar_prefetch=1, grid=(N_CHIPS, rh // tr),
        in_specs=[pl.BlockSpec((None, None, tr, c_), lambda j, i, s: (j, s[0], i, 0)), blk], out_specs=blk)
    return pl.pallas_call(body, grid_spec=gs, out_shape=S(theirs.shape, theirs.dtype), name=name,
                          compiler_params=_cp("parallel", "parallel"))(core.reshape(1), g, theirs)


def _chip_sum(q, got, l, me, core, into, name):
    _, rh, c_ = got.shape
    tr = _ew_tile(rh, c_, 4)

    def body(s_ref, q_ref, g0_ref, g1_ref, g2_ref, o_ref):
        acc = q_ref[...].astype(F32)
        for r in (g0_ref, g1_ref, g2_ref):
            acc = acc + r[...].astype(F32)
        o_ref[...] = acc

    in_specs = [pl.BlockSpec((None, tr, c_), lambda i, s: (s[0], i, 0))] + [
        pl.BlockSpec((None, tr, c_), functools.partial(lambda k, i, s: (k, i, 0), k)) for k in range(3)]
    return _call_into(
        body, into, in_specs, [jnp.stack([me, core]), q, got, got, got], n_prefetch=1, grid=(rh // tr,),
        out_specs=pl.BlockSpec((None, None, tr, c_), lambda i, s: (l, s[1], i, 0)),
        out_shape=S((DEPTH, 2, rh, c_), F32), name=name, compiler_params=_cp("parallel"))


def _pair_gather(gs4):
    def body(*refs):
        i_refs, o_refs = refs[:NBIG], refs[NBIG:2 * NBIG]
        send, recv = refs[2 * NBIG:]
        x, y, c, *_ = _mesh_pos()
        cps = []
        for w in range(NBIG):
            cp = pltpu.make_async_remote_copy(
                src_ref=i_refs[w].at[:, c], dst_ref=o_refs[w].at[:, c], send_sem=send.at[w], recv_sem=recv.at[w],
                device_id=(x, y, 1 - c), device_id_type=MESH)
            cp.start()
            cps.append(cp)
        for cp in cps:
            cp.wait()

    outs = pl.pallas_call(
        body, in_specs=[ANY] * NBIG, out_specs=[ANY] * NBIG, out_shape=[S(a.shape, a.dtype) for a in gs4],
        input_output_aliases={w: w for w in range(NBIG)}, name="grad_pair_gather",
        scratch_shapes=[pltpu.SemaphoreType.DMA((NBIG,))] * 2)(*gs4)
    return [o.reshape(o.shape[0], 2 * o.shape[2], o.shape[3]) for o in outs]


def _all_reduce_small(p, name):
    rows = p.shape[0]

    def body(p_ref, o_ref, gath, send, recv):
        x, y, c = lax.axis_index("x"), lax.axis_index("y"), lax.axis_index("c")
        my_id = 4 * x + 2 * y + c
        gath[my_id] = p_ref[...]
        cps = []
        for r in range(1, N_DEV):
            bx, by, bc = (r >> 2) & 1, (r >> 1) & 1, r & 1
            tx, ty, tc = (1 - x if bx else x), (1 - y if by else y), (1 - c if bc else c)
            cp = pltpu.make_async_remote_copy(
                src_ref=p_ref, dst_ref=gath.at[my_id], send_sem=send.at[r - 1], recv_sem=recv.at[r - 1],
                device_id=(tx, ty, tc), device_id_type=MESH)
            cp.start()
            cps.append((cp, 4 * tx + 2 * ty + tc))
        for r, (cp, peer) in enumerate(cps):
            pltpu.make_async_remote_copy(
                src_ref=p_ref, dst_ref=gath.at[peer], send_sem=send.at[r], recv_sem=recv.at[r],
                device_id=(x, y, c), device_id_type=MESH).wait_recv()
        for cp, _ in cps:
            cp.wait_send()
        acc = gath[0]
        for s in range(1, N_DEV):
            acc = acc + gath[s]
        o_ref[...] = acc

    vm = pl.BlockSpec(memory_space=pltpu.VMEM)
    return pl.pallas_call(
        body, in_specs=[vm], out_specs=vm, out_shape=S((rows, 128), F32), name=name,
        scratch_shapes=[pltpu.VMEM((N_DEV, rows, 128), F32), pltpu.SemaphoreType.DMA((N_DEV - 1,)),
                        pltpu.SemaphoreType.DMA((N_DEV - 1,))],
        compiler_params=pltpu.CompilerParams(vmem_limit_bytes=VMEM_LIMIT))(p)


PACK_UNIT = 8 * 128


def _pack(arrs):
    parts = []
    for a in arrs:
        flat = a.reshape(-1)
        pad = (-flat.shape[0]) % PACK_UNIT
        parts.append(jnp.pad(flat, (0, pad)).reshape(-1, 128))
    return jnp.concatenate(parts, axis=0)


def _unpack(buf, shapes):
    outs, row = [], 0
    for shp in shapes:
        n = int(np.prod(shp))
        rows = -(-n // PACK_UNIT) * 8
        outs.append(buf[row:row + rows].reshape(-1)[:n].reshape(shp))
        row += rows
    return outs


SMALL = ("norm1_g", "gm_ln_g", "gm_ln_b", "gm_ws", "gm_bs", "conv_w", "conv_b", "conv_ln_g", "conv_ln_b",
         "norm2_g", "final_g")
WEIGHTS = ("norm1_g", "w_in", "gm_ln_g", "gm_ln_b", "gm_ws", "gm_bs", "conv_w", "conv_b", "conv_ln_g",
           "conv_ln_b", "w_out", "norm2_g", "w_ffn_in", "w_ffn_out", "final_g")


def kernel(x, norm1_g, w_in, gm_ln_g, gm_ln_b, gm_ws, gm_bs, conv_w, conv_b, conv_ln_g, conv_ln_b, w_out, norm2_g, w_ffn_in, w_ffn_out, final_g, loss_target, m_norm1_g, m_w_in, m_gm_ln_g, m_gm_ln_b, m_gm_ws, m_gm_bs, m_conv_w, m_conv_b, m_conv_ln_g, m_conv_ln_b, m_w_out, m_norm2_g, m_w_ffn_in, m_w_ffn_out, m_final_g, v_norm1_g, v_w_in, v_gm_ln_g, v_gm_ln_b, v_gm_ws, v_gm_bs, v_conv_w, v_conv_b, v_conv_ln_g, v_conv_ln_b, v_w_out, v_norm2_g, v_w_ffn_in, v_w_ffn_out, v_final_g):
    given = dict(locals())
    t = x.shape[1]
    xc = x.reshape(t, D)
    target = loss_target.reshape(t, D)
    me = 2 * lax.axis_index("x") + lax.axis_index("y")
    core = lax.axis_index("c")
    tb = _tables(t)

    me = me.astype(jnp.int32)
    core = core.astype(jnp.int32)
    names = [n for n, _ in BIG]
    kinds = dict(BIG)
    gathered = [{n: _cast_to_gathered(given[n], l, me, f"cast_{n}{l}") for n in names} for l in range(DEPTH)]
    gathered[0]["w_in"] = _all_gather([gathered[0]["w_in"]], "all_gather_w_in0")[0]

    def weight(l, n):
        b = gathered[l][n]
        r_, c_ = 2 * b.shape[2], b.shape[3]
        return b.reshape(N_CHIPS, r_, c_) if kinds[n] == "col" else b.reshape(N_CHIPS * r_, c_)

    cshard = CV_W // N_CHIPS
    placed = lax.dynamic_update_slice(jnp.zeros((DEPTH, KCONV, CV_W), F32),
                                      conv_w * (core == 0).astype(F32), (0, 0, me * cshard))
    conv_w_full = _unpack(_all_reduce_small(_pack([placed]), "gather_conv_w"), [(DEPTH, KCONV, CV_W)])[0]
    cw32 = jnp.pad(conv_w_full, ((0, 0), (0, 32 - KCONV), (0, 0)))

    def row(a, l):
        return a[l].reshape(1, -1)

    saved = []
    early = ["w_in", "w_out", "w_ffn_in"]
    for l in range(DEPTH):
        cur = gathered[l]
        nxt = gathered[l + 1] if l + 1 < DEPTH else None
        sv = {"x": xc}
        bias = jnp.repeat(gm_bs[l].T, GM_W // GM_HEADS, axis=1)
        first = ["w_ffn_in"] if l == 0 else ["w_ffn_out"]
        late = ["w_out", "w_ffn_out"]
        proj, rid = _norm_mm(xc, row(norm1_g, l), weight(l, "w_in"), F32, f"in_proj{l}", 512,
                             [("ici" if l == 0 else "d2d", [cur[n] for n in first])])
        cur.update(zip(first, rid))
        y_gm, rid = _gm_fwd(proj, row(gm_ln_g, l), row(gm_ln_b, l), gm_ws[l], bias, f"gm_fwd{l}",
                            [("d2d", [cur[n] for n in first])] if l == 0 else ())
        cur.update(zip(first, rid))
        rq, rk = _rotary(proj, tb["cos2"], tb["sin2"], f"rotary{l}")
        sf, sb = _ret_scan(rk, proj, 3, tb["zf"], tb["zb"], tb["gcf"], tb["gcb"], f"ret_state{l}")
        a, y_ret, rid = _ret_out(rq, rk, proj, sf, sb, tb, f"ret_out{l}",
                                 [("ici", [cur[n] for n in late])] if l == 0 else ())
        cur.update(zip(late, rid))
        y_cv, hc, rid = _conv_fwd(proj, cw32[l], row(conv_b, l), row(conv_ln_g, l), row(conv_ln_b, l),
                                  f"conv_fwd{l}", [("d2d", [cur[n] for n in late])] if l == 0 else ())
        cur.update(zip(late, rid))
        x_mid = _parts_mm_res([y_gm, y_ret, y_cv], weight(l, "w_out"), xc, f"out_proj{l}")
        ff, rid = _norm_mm(x_mid, row(norm2_g, l), weight(l, "w_ffn_in"), ACT_DTYPE, f"ffn_in{l}", 512,
                           [("ici", [nxt[n] for n in early])] if nxt else ())
        if nxt:
            nxt.update(zip(early, rid))
        xc, rid = _swiglu_mm_res(ff, weight(l, "w_ffn_out"), x_mid, f"ffn_out{l}",
                                 [("d2d", [nxt[n] for n in early]), ("ici", [nxt["w_ffn_out"]])] if nxt else ())
        if nxt:
            nxt.update(zip(early + ["w_ffn_out"], rid))
        sv.update(bias=bias, proj=proj, y_gm=y_gm, rq=rq, rk=rk, sf=sf, sb=sb, a=a, y_ret=y_ret, y_cv=y_cv,
                  hc=hc, x_mid=x_mid, ff=ff)
        saved.append(sv)

    dx, d_final_g, lpart = _loss_head(xc, final_g.reshape(1, D), target, "loss_head")
    loss = lax.psum(lpart[0, 0], ("x", "y", "c"))

    small_g = {n: [None] * DEPTH for n in SMALL}
    qs = [{} for _ in range(DEPTH)]
    got = [{} for _ in range(DEPTH)]
    ffn_w, mix_w = ["w_ffn_out", "w_ffn_in"], ["w_out", "w_in"]

    def pair_reduce(l, group, big_g):
        g4 = [big_g[n].reshape(N_CHIPS, 2, given[n].shape[1] // 2, given[n].shape[2]) for n in group]
        theirs = _pair_exchange(g4, f"grad_pair_exchange_{group[0]}{l}")
        qs[l].update({n: _pair_sum(g, th, core, f"pair_sum_{n}{l}") for n, g, th in zip(group, g4, theirs)})
        return [qs[l][n] for n in group]

    for l in reversed(range(DEPTH)):
        sv = saved[l]
        proj = sv["proj"]
        big_g = {}
        dff = _dx_swiglu(dx, weight(l, "w_ffn_out"), sv["ff"], f"ffn_out_dx{l}")
        big_g["w_ffn_out"] = _dw_swiglu(sv["ff"], dx, f"ffn_out_dw{l}")
        dx_mid, dg2, _ = _dx_norm([dff], weight(l, "w_ffn_in"), sv["x_mid"], row(norm2_g, l), dx,
                                  f"ffn_in_dx{l}", 256)
        big_g["w_ffn_in"], _ = _dw_norm_cols(sv["x_mid"], row(norm2_g, l), dff, w_ffn_in.shape[2], f"ffn_in_dw{l}")
        q_ffn = pair_reduce(l, ffn_w, big_g)
        dy_gm, dy_ret, dy_cv = _dx_parts(dx_mid, weight(l, "w_out"), [GM_W, RET_W, CV_W], f"out_proj_dx{l}")
        big_g["w_out"] = _dw_parts([sv["y_gm"], sv["y_ret"], sv["y_cv"]], dx_mid, f"out_proj_dw{l}")
        d_cv, dcw, dcb, dclg, dclb, rid = _conv_bwd(proj, dy_cv, sv["hc"], cw32[l], row(conv_ln_g, l),
                                                    row(conv_ln_b, l), f"conv_bwd{l}", [("scatter", q_ffn)])
        got[l].update(zip(ffn_w, rid))
        da, d_g = _ret_bwd_pre(dy_ret, sv["a"], proj, f"ret_bwd_pre{l}")
        gb_, gf_ = _ret_scan(sv["rq"], da, 0, tb["xib"], tb["xif"], tb["gcb"], tb["gcf"], f"ret_bwd_state{l}")
        d_qkv = _ret_bwd_main(sv["rq"], sv["rk"], proj, da, sv["sf"], sv["sb"], gf_, gb_, tb, f"ret_bwd_main{l}")
        d_gm, dws, dbs, dglg, dglb = _gm_bwd(proj, dy_gm, row(gm_ln_g, l), row(gm_ln_b, l), gm_ws[l],
                                             jnp.swapaxes(gm_ws[l], 1, 2), sv["bias"], f"gm_bwd{l}")
        dparts = [d_gm, d_qkv, d_g, d_cv]
        big_g["w_in"] = _dw_norm_parts(sv["x"], row(norm1_g, l), dparts, w_in.shape[2], f"in_proj_dw{l}")
        q_mix = pair_reduce(l, mix_w, big_g)
        dx, dg1, rid = _dx_norm(dparts, weight(l, "w_in"), sv["x"], row(norm1_g, l), dx_mid, f"in_proj_dx{l}", 512,
                                [("scatter", q_mix)])
        got[l].update(zip(mix_w, rid))
        for n, val in (("norm1_g", dg1[0]), ("gm_ln_g", dglg[0]), ("gm_ln_b", dglb[0]), ("gm_ws", dws),
                       ("gm_bs", dbs[:, :GM_HEADS].T), ("conv_w", dcw[:KCONV]), ("conv_b", dcb[0]),
                       ("conv_ln_g", dclg[0]), ("conv_ln_b", dclb[0]), ("norm2_g", dg2[0])):
            small_g[n][l] = val

    small_shapes = [given[n].shape if n != "conv_w" else (DEPTH, KCONV, CV_W) for n in SMALL]
    partials = [d_final_g[0] if n == "final_g" else jnp.stack(small_g[n]) for n in SMALL]
    reduced = dict(zip(SMALL, _unpack(_all_reduce_small(_pack(partials), "all_reduce_small_grads"), small_shapes)))
    reduced["conv_w"] = lax.dynamic_slice(reduced["conv_w"], (0, 0, me * cshard), (DEPTH, KCONV, cshard))

    halves = [None] * NBIG
    for l in reversed(range(DEPTH)):
        halves = [_chip_sum(qs[l][n], got[l][n], l, me, core, h, f"chip_sum_{n}{l}") for n, h in zip(names, halves)]
    grads = dict(zip(names, _pair_gather(halves)))
    grads.update(reduced)

    delta, new_m, new_v = {}, {}, {}
    for n, _ in BIG:
        delta[n], new_m[n], new_v[n] = _adamw(given[n], grads[n], given["m_" + n], given["v_" + n], f"adamw_{n}")
    shapes = [given[n].shape for n in SMALL]
    packed = [_pack([src[n] if src is grads else src[p + n] for n in SMALL])
              for src, p in ((given, ""), (grads, ""), (given, "m_"), (given, "v_"))]
    outs = _adamw(*packed, "adamw_small")
    for dst, buf in zip((delta, new_m, new_v), outs):
        dst.update(zip(SMALL, _unpack(buf, shapes)))

    return (loss, dx.reshape(1, t, D), *[grads[n] for n in WEIGHTS], *[delta[n] for n in WEIGHTS],
            *[new_m[n] for n in WEIGHTS], *[new_v[n] for n in WEIGHTS])
```

```python
import functools
import math

import numpy as np
import jax
import jax.numpy as jnp
from jax import lax
from jax.experimental import pallas as pl
from jax.experimental.pallas import tpu as pltpu

F32 = jnp.float32
BF16 = jnp.bfloat16
MXU_DTYPE = BF16
ACT_DTYPE = BF16
S = jax.ShapeDtypeStruct

D = 1024
DEPTH = 2
GM_W = 256
GM_HEADS = 4
RET_W = 512
HEADS = 4
HD = 128
CV_W = 256
KCONV = 31
IN_W = 2 * GM_W + 4 * RET_W + 2 * CV_W
FFN_H = 2816
CH = 128
ROPE_BASE = 10000.0
EPS = 1e-6
N_CHIPS = 4
N_DEV = 8
HALO = 16

ADAM_LR = 0.001
ADAM_B1 = 0.9
ADAM_B2 = 0.999
ADAM_EPS = 1e-08
ADAM_WD = 0.01
ADAM_STEP = 10

VMEM_LIMIT = 52 * 1024 * 1024
MESH = pl.DeviceIdType.MESH


def _cp(*sem, vmem=VMEM_LIMIT):
    return pltpu.CompilerParams(dimension_semantics=tuple(sem), vmem_limit_bytes=vmem)


def _mx(a):
    return a.astype(MXU_DTYPE)


def _dot(a, b):
    return jnp.dot(_mx(a), _mx(b), preferred_element_type=F32)


def _dot_nt(a, b):
    return lax.dot_general(_mx(a), _mx(b), (((1,), (1,)), ((), ())), preferred_element_type=F32)


def _dot_tn(a, b):
    return lax.dot_general(_mx(a), _mx(b), (((0,), (0,)), ((), ())), preferred_element_type=F32)


def _sigmoid(x):
    return 1.0 / (1.0 + jnp.exp(-x))


def _gelu(x):
    return 0.5 * x * (1.0 + lax.erf(x * (1.0 / math.sqrt(2.0))))


def _gelu_grad(x):
    return 0.5 * (1.0 + lax.erf(x * (1.0 / math.sqrt(2.0)))) + x * jnp.exp(-0.5 * x * x) * (1.0 / math.sqrt(2.0 * math.pi))


def _rms_r(x):
    return lax.rsqrt(jnp.mean(x * x, axis=-1, keepdims=True) + EPS)


def _rms_bwd(dh, x, r, g):
    u = dh * g
    dx = r * u - x * (r * r * r) * jnp.mean(u * x, axis=-1, keepdims=True)
    return dx, dh * x * r


def _standardize(a):
    mu = jnp.mean(a, axis=-1, keepdims=True)
    d = a - mu
    r = lax.rsqrt(jnp.mean(d * d, axis=-1, keepdims=True) + EPS)
    return d * r, r


def _standardize_bwd(do, o, r):
    return r * (do - jnp.mean(do, axis=-1, keepdims=True) - o * jnp.mean(do * o, axis=-1, keepdims=True))


def _acc_out(ref, val, first):
    @pl.when(first)
    def _():
        ref[...] = val

    @pl.when(jnp.logical_not(first))
    def _():
        ref[...] += val


def _row_tile(t, pref):
    tm = min(t, pref)
    assert t % tm == 0, (t, tm)
    return tm


def _segments(part_widths, shard_w):
    bounds = {0}
    off = 0
    for w in part_widths:
        off += w
        bounds.add(off)
    total = off
    for j in range(1, total // shard_w + 1):
        bounds.add(j * shard_w)
    bounds = sorted(bounds)
    starts = np.cumsum([0] + list(part_widths))
    segs = []
    for a, b in zip(bounds[:-1], bounds[1:]):
        p = int(np.searchsorted(starts, a, side="right") - 1)
        segs.append((p, a - int(starts[p]), a // shard_w, a % shard_w, b - a))
    return segs


ANY = pl.BlockSpec(memory_space=pl.ANY)


def _mesh_pos():
    x, y, c = lax.axis_index("x"), lax.axis_index("y"), lax.axis_index("c")
    chips = [(1 - x, y), (x, 1 - y), (1 - x, 1 - y)]
    return x, y, c, 2 * x + y, chips, [2 * cx + cy for cx, cy in chips]


def _rider_copies(kind, i_refs, o_refs, send, recv, pos):
    x, y, c, me, chips, cj = pos
    out = []
    for b, (i_ref, o_ref) in enumerate(zip(i_refs, o_refs)):
        for k in range(3):
            if kind == "ici":
                src, dst, land, dev = i_ref.at[me, c], o_ref.at[me, c], o_ref.at[cj[k], c], (*chips[k], c)
            elif kind == "d2d":
                src, dst, land, dev = i_ref.at[cj[k], c], o_ref.at[cj[k], c], o_ref.at[cj[k], 1 - c], (x, y, 1 - c)
            else:
                src, dst, land, dev = i_ref.at[cj[k]], o_ref.at[k], o_ref.at[k], (*chips[k], c)
            out.append(tuple(pltpu.make_async_remote_copy(
                src_ref=s_, dst_ref=d_, send_sem=send.at[b, k], recv_sem=recv.at[b, k],
                device_id=dev, device_id_type=MESH) for s_, d_ in ((src, dst), (land, land))))
    return out


def _rider_out_shape(kind, a):
    return S((3,) + a.shape[1:], a.dtype) if kind == "scatter" else S(a.shape, a.dtype)


def _pcall(body, args, riders, *, grid, in_specs, out_specs, out_shape, name, sem, scratch_shapes=()):
    outs = list(out_shape)
    if not riders:
        res = pl.pallas_call(body, grid=grid, in_specs=in_specs, out_specs=out_specs, out_shape=outs, name=name,
                             scratch_shapes=list(scratch_shapes), compiler_params=_cp(*sem))(*args)
        return res, []
    r_in = [a for _, bufs in riders for a in bufs]
    r_out = [_rider_out_shape(kind, a) for kind, bufs in riders for a in bufs]
    n_in, n_out, n_scr, n_r = len(args), len(outs), len(scratch_shapes), len(r_in)
    aliases, idx = {}, 0
    for kind, bufs in riders:
        for _ in bufs:
            if kind != "scatter":
                aliases[n_in + idx] = n_out + idx
            idx += 1
    sems = [pltpu.SemaphoreType.DMA((len(bufs), 3)) for _, bufs in riders for _ in range(2)]

    def wrapped(*refs):
        a, ri = refs[:n_in], refs[n_in:n_in + n_r]
        o, ro = refs[n_in + n_r:n_in + n_r + n_out], refs[n_in + n_r + n_out:n_in + 2 * n_r + n_out]
        scr = refs[n_in + 2 * n_r + n_out:n_in + 2 * n_r + n_out + n_scr]
        sm = refs[n_in + 2 * n_r + n_out + n_scr:]
        pos = _mesh_pos()
        copies, off = [], 0
        for r, (kind, bufs) in enumerate(riders):
            copies += _rider_copies(kind, ri[off:off + len(bufs)], ro[off:off + len(bufs)], sm[2 * r], sm[2 * r + 1], pos)
            off += len(bufs)
        ids = [pl.program_id(d) for d in range(len(grid))]
        first = functools.reduce(jnp.logical_and, [i == 0 for i in ids])
        last = functools.reduce(jnp.logical_and, [i == n - 1 for i, n in zip(ids, grid)])

        @pl.when(first)
        def _():
            for cp, _ in copies:
                cp.start()

        body(*a, *o, *scr)

        @pl.when(last)
        def _():
            for cp, land in copies:
                land.wait_recv()
                cp.wait_send()

    res = pl.pallas_call(
        wrapped, grid=grid, in_specs=list(in_specs) + [ANY] * n_r, out_specs=list(out_specs) + [ANY] * n_r,
        out_shape=outs + r_out, input_output_aliases=aliases, name=name,
        scratch_shapes=list(scratch_shapes) + sems, compiler_params=_cp(*(("arbitrary",) * len(grid))))(*args, *r_in)
    return res[:n_out], res[n_out:]


def _wcol_spec(w):
    return pl.BlockSpec(w.shape, lambda *_: (0, 0, 0))


def _wrow_spec(w):
    return pl.BlockSpec(w.shape, lambda *_: (0, 0))


def _norm_mm(x, g, w, out_dtype, name, tm_pref, riders=()):
    t = x.shape[0]
    nc = w.shape[2]
    tm = _row_tile(t, tm_pref)

    def body(x_ref, g_ref, w_ref, o_ref):
        xv = x_ref[...]
        h = _mx(xv * _rms_r(xv) * g_ref[...])
        for j in range(N_CHIPS):
            o_ref[:, j * nc:(j + 1) * nc] = jnp.dot(h, w_ref[j], preferred_element_type=F32).astype(o_ref.dtype)

    (out,), rid = _pcall(
        body, [x, g, w], riders, grid=(t // tm,),
        in_specs=[pl.BlockSpec((tm, D), lambda i: (i, 0)), pl.BlockSpec((1, D), lambda i: (0, 0)), _wcol_spec(w)],
        out_specs=[pl.BlockSpec((tm, N_CHIPS * nc), lambda i: (i, 0))],
        out_shape=[S((t, N_CHIPS * nc), out_dtype)], name=name, sem=("parallel",))
    return out, rid


def _parts_mm_res(parts, w, res, name):
    t = res.shape[0]
    tm = _row_tile(t, 512)
    widths = [p.shape[1] for p in parts]
    offs = np.cumsum([0] + widths)
    n = len(parts)

    def body(*refs):
        p_refs, w_ref, r_ref, o_ref = refs[:n], refs[n], refs[n + 1], refs[n + 2]
        acc = r_ref[...]
        for p in range(n):
            acc = acc + _dot(p_refs[p][...], w_ref[int(offs[p]):int(offs[p + 1]), :])
        o_ref[...] = acc

    return pl.pallas_call(
        body, grid=(t // tm,),
        in_specs=[pl.BlockSpec((tm, wd), lambda i: (i, 0)) for wd in widths]
        + [_wrow_spec(w), pl.BlockSpec((tm, D), lambda i: (i, 0))],
        out_specs=pl.BlockSpec((tm, D), lambda i: (i, 0)),
        out_shape=S((t, D), F32), name=name, compiler_params=_cp("parallel"))(*parts, w, res)


def _swiglu(ff):
    gate = ff[:, :FFN_H].astype(F32)
    up = ff[:, FFN_H:].astype(F32)
    return gate * _sigmoid(gate) * up


def _swiglu_mm_res(ff, w, res, name, riders=()):
    t = res.shape[0]
    tm = _row_tile(t, 512)

    def body(f_ref, w_ref, r_ref, o_ref):
        o_ref[...] = r_ref[...] + _dot(_swiglu(f_ref[...]), w_ref[...])

    (out,), rid = _pcall(
        body, [ff, w, res], riders, grid=(t // tm,),
        in_specs=[pl.BlockSpec((tm, 2 * FFN_H), lambda i: (i, 0)), _wrow_spec(w),
                  pl.BlockSpec((tm, D), lambda i: (i, 0))],
        out_specs=[pl.BlockSpec((tm, D), lambda i: (i, 0))],
        out_shape=[S((t, D), F32)], name=name, sem=("parallel",))
    return out, rid


def _dx_norm(dparts, w, x, g, dres, name, tm_pref, riders=()):
    t = x.shape[0]
    nc = w.shape[2]
    tm = _row_tile(t, tm_pref)
    widths = [p.shape[1] for p in dparts]
    segs = _segments(widths, nc)
    n = len(dparts)

    def body(*refs):
        d_refs = refs[:n]
        w_ref, x_ref, g_ref, r_ref, dx_ref, dg_ref = refs[n:]
        dh = jnp.zeros((tm, D), F32)
        for (p, po, j, jo, wd) in segs:
            dh = dh + _dot_nt(d_refs[p][:, po:po + wd], w_ref[j, :, jo:jo + wd])
        xv = x_ref[...]
        dx, dgrow = _rms_bwd(dh, xv, _rms_r(xv), g_ref[...])
        dx_ref[...] = r_ref[...] + dx
        _acc_out(dg_ref, jnp.sum(dgrow, axis=0, keepdims=True), pl.program_id(0) == 0)

    (dx, dg), rid = _pcall(
        body, [*dparts, w, x, g, dres], riders, grid=(t // tm,),
        in_specs=[pl.BlockSpec((tm, wd), lambda i: (i, 0)) for wd in widths]
        + [_wcol_spec(w), pl.BlockSpec((tm, D), lambda i: (i, 0)),
           pl.BlockSpec((1, D), lambda i: (0, 0)), pl.BlockSpec((tm, D), lambda i: (i, 0))],
        out_specs=[pl.BlockSpec((tm, D), lambda i: (i, 0)), pl.BlockSpec((1, D), lambda i: (0, 0))],
        out_shape=[S((t, D), F32), S((1, D), F32)], name=name, sem=("arbitrary",))
    return dx, dg, rid


def _dx_parts(dy, w, widths, name):
    t = dy.shape[0]
    tm = _row_tile(t, 512)
    offs = np.cumsum([0] + list(widths))
    n = len(widths)

    def body(dy_ref, w_ref, *o_refs):
        dyv = _mx(dy_ref[...])
        for p in range(n):
            o_refs[p][...] = _dot_nt(dyv, w_ref[int(offs[p]):int(offs[p + 1]), :])

    return pl.pallas_call(
        body, grid=(t // tm,),
        in_specs=[pl.BlockSpec((tm, D), lambda i: (i, 0)), _wrow_spec(w)],
        out_specs=[pl.BlockSpec((tm, wd), lambda i: (i, 0)) for wd in widths],
        out_shape=[S((t, wd), F32) for wd in widths], name=name, compiler_params=_cp("parallel"))(dy, w)


def _dx_swiglu(dy, w, ff, name):
    t = dy.shape[0]
    tm = _row_tile(t, 512)

    def body(dy_ref, w_ref, f_ref, o_ref):
        dact = _dot_nt(dy_ref[...], w_ref[...])
        gate = f_ref[:, :FFN_H].astype(F32)
        up = f_ref[:, FFN_H:].astype(F32)
        s = _sigmoid(gate)
        o_ref[:, :FFN_H] = (dact * up * (s * (1.0 + gate * (1.0 - s)))).astype(o_ref.dtype)
        o_ref[:, FFN_H:] = (dact * (gate * s)).astype(o_ref.dtype)

    return pl.pallas_call(
        body, grid=(t // tm,),
        in_specs=[pl.BlockSpec((tm, D), lambda i: (i, 0)), _wrow_spec(w),
                  pl.BlockSpec((tm, 2 * FFN_H), lambda i: (i, 0))],
        out_specs=pl.BlockSpec((tm, 2 * FFN_H), lambda i: (i, 0)),
        out_shape=S((t, 2 * FFN_H), ACT_DTYPE), name=name, compiler_params=_cp("parallel"))(dy, w, ff)


def _call_into(body, into, in_specs, args, *, n_prefetch, grid, out_specs, **kw):
    n_in = len(args)
    if into is None:
        gs = pltpu.PrefetchScalarGridSpec(num_scalar_prefetch=n_prefetch, grid=grid, in_specs=in_specs,
                                          out_specs=out_specs)
        return pl.pallas_call(body, grid_spec=gs, **kw)(*args)

    def wrapped(*refs):
        return body(*refs[:n_in], *refs[n_in + 1:])

    gs = pltpu.PrefetchScalarGridSpec(num_scalar_prefetch=n_prefetch, grid=grid,
                                      in_specs=list(in_specs) + [ANY], out_specs=out_specs)
    return pl.pallas_call(wrapped, grid_spec=gs, input_output_aliases={n_in: 0}, **kw)(*args, into)


def _dw_norm_parts(x, g, dparts, nc, name):
    t = x.shape[0]
    tk = _row_tile(t, 1024)
    widths = [p.shape[1] for p in dparts]
    segs = _segments(widths, nc)
    n = len(dparts)
    nk = t // tk

    def body(*refs):
        x_ref, g_ref = refs[0], refs[1]
        d_refs = refs[2:2 + n]
        o_ref, acc_ref = refs[2 + n], refs[3 + n]
        k = pl.program_id(0)
        xv = x_ref[...]
        h = _mx(xv * _rms_r(xv) * g_ref[...])

        @pl.when(k == 0)
        def _():
            acc_ref[...] = jnp.zeros_like(acc_ref)

        for (p, po, j, jo, wd) in segs:
            acc_ref[j, :, jo:jo + wd] += _dot_tn(h, d_refs[p][:, po:po + wd])

        @pl.when(k == nk - 1)
        def _():
            o_ref[...] = acc_ref[...].astype(o_ref.dtype)

    return pl.pallas_call(
        body, grid=(nk,),
        in_specs=[pl.BlockSpec((tk, D), lambda k: (k, 0)), pl.BlockSpec((1, D), lambda k: (0, 0))]
        + [pl.BlockSpec((tk, wd), lambda k: (k, 0)) for wd in widths],
        out_specs=pl.BlockSpec((N_CHIPS, D, nc), lambda k: (0, 0, 0)),
        out_shape=S((N_CHIPS, D, nc), MXU_DTYPE), name=name,
        scratch_shapes=[pltpu.VMEM((N_CHIPS, D, nc), F32)], compiler_params=_cp("arbitrary"))(x, g, *dparts)


def _dw_norm_cols(x, g, dy, nc, name, riders=()):
    t = x.shape[0]
    tk = _row_tile(t, 1024)
    nk = t // tk

    def body(x_ref, g_ref, dy_ref, o_ref, acc_ref):
        k = pl.program_id(1)
        xv = x_ref[...]
        h = _mx(xv * _rms_r(xv) * g_ref[...])

        @pl.when(k == 0)
        def _():
            acc_ref[...] = jnp.zeros_like(acc_ref)

        acc_ref[...] += _dot_tn(h, dy_ref[...])

        @pl.when(k == nk - 1)
        def _():
            o_ref[...] = acc_ref[...].astype(o_ref.dtype)

    (out,), rid = _pcall(
        body, [x, g, dy], riders, grid=(N_CHIPS, nk),
        in_specs=[pl.BlockSpec((tk, D), lambda j, k: (k, 0)), pl.BlockSpec((1, D), lambda j, k: (0, 0)),
                  pl.BlockSpec((tk, nc), lambda j, k: (k, j))],
        out_specs=[pl.BlockSpec((None, D, nc), lambda j, k: (j, 0, 0))],
        out_shape=[S((N_CHIPS, D, nc), MXU_DTYPE)], name=name, sem=("parallel", "arbitrary"),
        scratch_shapes=[pltpu.VMEM((D, nc), F32)])
    return out, rid


def _dw_parts(parts, dy, name):
    t = dy.shape[0]
    tk = _row_tile(t, 1024)
    widths = [p.shape[1] for p in parts]
    offs = np.cumsum([0] + widths)
    ktot = int(offs[-1])
    n = len(parts)
    nk = t // tk

    def body(*refs):
        p_refs, dy_ref, o_ref, acc_ref = refs[:n], refs[n], refs[n + 1], refs[n + 2]
        k = pl.program_id(0)

        @pl.when(k == 0)
        def _():
            acc_ref[...] = jnp.zeros_like(acc_ref)

        dyv = _mx(dy_ref[...])
        for p in range(n):
            acc_ref[int(offs[p]):int(offs[p + 1]), :] += _dot_tn(p_refs[p][...], dyv)

        @pl.when(k == nk - 1)
        def _():
            o_ref[...] = acc_ref[...].astype(o_ref.dtype)

    return pl.pallas_call(
        body, grid=(nk,),
        in_specs=[pl.BlockSpec((tk, wd), lambda k: (k, 0)) for wd in widths]
        + [pl.BlockSpec((tk, D), lambda k: (k, 0))],
        out_specs=pl.BlockSpec((ktot, D), lambda k: (0, 0)),
        out_shape=S((ktot, D), MXU_DTYPE), name=name,
        scratch_shapes=[pltpu.VMEM((ktot, D), F32)], compiler_params=_cp("arbitrary"))(*parts, dy)


def _dw_swiglu(ff, dy, name):
    t = dy.shape[0]
    tk = _row_tile(t, 512)
    nk = t // tk

    def body(f_ref, dy_ref, o_ref, acc_ref):
        k = pl.program_id(0)

        @pl.when(k == 0)
        def _():
            acc_ref[...] = jnp.zeros_like(acc_ref)

        acc_ref[...] += _dot_tn(_swiglu(f_ref[...]), dy_ref[...])

        @pl.when(k == nk - 1)
        def _():
            o_ref[...] = acc_ref[...].astype(o_ref.dtype)

    return pl.pallas_call(
        body, grid=(nk,),
        in_specs=[pl.BlockSpec((tk, 2 * FFN_H), lambda k: (k, 0)), pl.BlockSpec((tk, D), lambda k: (k, 0))],
        out_specs=pl.BlockSpec((FFN_H, D), lambda k: (0, 0)),
        out_shape=S((FFN_H, D), MXU_DTYPE), name=name,
        scratch_shapes=[pltpu.VMEM((FFN_H, D), F32)], compiler_params=_cp("arbitrary"))(ff, dy)


def _tables(t):
    pos = jnp.arange(t, dtype=F32)
    half = HD // 2
    inv_freq = ROPE_BASE ** (-jnp.arange(half, dtype=F32) / half)
    ang = pos[:, None] * inv_freq[None, :]
    cos, sin = jnp.cos(ang), jnp.sin(ang)
    tb = {"cos2": jnp.concatenate([cos, cos], axis=1), "sin2": jnp.concatenate([-sin, sin], axis=1)}
    gf = 1.0 - jnp.exp2(-5.0 - jnp.arange(HEADS, dtype=F32))
    lgf = jnp.log(gf)[:, None]
    lgb = jnp.log(gf[::-1])[:, None]
    idx = jnp.arange(CH, dtype=F32)
    diff = idx[:, None] - idx[None, :]
    dfwd = jnp.where(diff >= 0, jnp.exp(lgf[:, :, None] * jnp.where(diff >= 0, diff, 0.0)), 0.0)
    dbwd = jnp.where(diff < 0, jnp.exp(lgb[:, :, None] * jnp.where(diff < 0, -diff, 0.0)), 0.0)
    tb["dm"] = dfwd + dbwd
    tb["dmt"] = jnp.swapaxes(tb["dm"], 1, 2)

    def lanes(a):
        return jnp.repeat(a.T, HD, axis=1)

    tb["xif"] = lanes(jnp.exp(lgf * (idx + 1)))
    tb["zf"] = lanes(jnp.exp(lgf * (CH - 1 - idx)))
    tb["xib"] = lanes(jnp.exp(lgb * (CH - idx)))
    tb["zb"] = lanes(jnp.exp(lgb * idx))
    tb["gcf"] = jnp.repeat(jnp.exp(lgf * CH), HD, axis=0).reshape(1, HEADS * HD)
    tb["gcb"] = jnp.repeat(jnp.exp(lgb * CH), HD, axis=0).reshape(1, HEADS * HD)
    return tb


def _full(shape):
    nd = len(shape)
    return pl.BlockSpec(shape, lambda *_: (0,) * nd)


def _gm_mixed(vn, ws_ref, bias):
    lane = lax.broadcasted_iota(jnp.int32, (CH, 128), 1)
    halves = []
    for hf in range(2):
        vh = _mx(vn[:, hf * 128:(hf + 1) * 128])
        r0 = jnp.dot(_mx(ws_ref[2 * hf]), vh, preferred_element_type=F32)
        r1 = jnp.dot(_mx(ws_ref[2 * hf + 1]), vh, preferred_element_type=F32)
        halves.append(jnp.where(lane < 64, r0, r1))
    return jnp.concatenate(halves, axis=1) + bias


def _gm_fwd(proj, ln_g, ln_b, ws, bias, name, riders=()):
    t = proj.shape[0]
    tm = _row_tile(t, 512)

    def body(pu_ref, pv_ref, g_ref, b_ref, ws_ref, bias_ref, o_ref):
        for c in range(tm // CH):
            rows = slice(c * CH, (c + 1) * CH)
            u = _gelu(pu_ref[rows, :])
            o, _ = _standardize(_gelu(pv_ref[rows, :]))
            vn = o * g_ref[...] + b_ref[...]
            o_ref[rows, :] = (u * _gm_mixed(vn, ws_ref, bias_ref[...])).astype(o_ref.dtype)

    (out,), rid = _pcall(
        body, [proj, proj, ln_g, ln_b, ws, bias], riders, grid=(t // tm,),
        in_specs=[pl.BlockSpec((tm, GM_W), lambda i: (i, 0)), pl.BlockSpec((tm, GM_W), lambda i: (i, 1)),
                  _full((1, GM_W)), _full((1, GM_W)), _full((GM_HEADS, CH, CH)), _full((CH, GM_W))],
        out_specs=[pl.BlockSpec((tm, GM_W), lambda i: (i, 0))],
        out_shape=[S((t, GM_W), ACT_DTYPE)], name=name, sem=("parallel",))
    return out, rid


def _gm_bwd(proj, dy, ln_g, ln_b, ws, wst, bias, name):
    t = proj.shape[0]
    tm = _row_tile(t, 512)
    nb = t // tm

    def body(pu_ref, pv_ref, dy_ref, g_ref, b_ref, ws_ref, wst_ref, bias_ref,
             d_ref, dws_ref, dbs_ref, dg_ref, db_ref, dbias_ref):
        first = pl.program_id(0) == 0
        lane = lax.broadcasted_iota(jnp.int32, (CH, 128), 1)
        dws = [jnp.zeros((CH, CH), F32) for _ in range(GM_HEADS)]
        dbias = jnp.zeros((CH, GM_W), F32)
        dg = jnp.zeros((1, GM_W), F32)
        db = jnp.zeros((1, GM_W), F32)
        for c in range(tm // CH):
            rows = slice(c * CH, (c + 1) * CH)
            pu = pu_ref[rows, :]
            pv = pv_ref[rows, :]
            u = _gelu(pu)
            o, r = _standardize(_gelu(pv))
            vn = o * g_ref[...] + b_ref[...]
            mixed = _gm_mixed(vn, ws_ref, bias_ref[...])
            dyv = dy_ref[rows, :]
            d_ref[rows, :GM_W] = (dyv * mixed * _gelu_grad(pu)).astype(d_ref.dtype)
            dmixed = dyv * u
            dbias = dbias + dmixed
            dvn_halves = []
            for hf in range(2):
                dm = dmixed[:, hf * 128:(hf + 1) * 128]
                vh = vn[:, hf * 128:(hf + 1) * 128]
                dm0 = jnp.where(lane < 64, dm, 0.0)
                dm1 = dm - dm0
                dws[2 * hf] = dws[2 * hf] + _dot_nt(dm0, vh)
                dws[2 * hf + 1] = dws[2 * hf + 1] + _dot_nt(dm1, vh)
                t0 = _dot(wst_ref[2 * hf], dm)
                t1 = _dot(wst_ref[2 * hf + 1], dm)
                dvn_halves.append(jnp.where(lane < 64, t0, t1))
            dvn = jnp.concatenate(dvn_halves, axis=1)
            dg = dg + jnp.sum(dvn * o, axis=0, keepdims=True)
            db = db + jnp.sum(dvn, axis=0, keepdims=True)
            dv = _standardize_bwd(dvn * g_ref[...], o, r)
            d_ref[rows, GM_W:] = (dv * _gelu_grad(pv)).astype(d_ref.dtype)
        for h in range(GM_HEADS):
            _acc_out(dws_ref.at[h], dws[h], first)
        _acc_out(dbias_ref, dbias, first)
        _acc_out(dg_ref, dg, first)
        _acc_out(db_ref, db, first)

        @pl.when(pl.program_id(0) == nb - 1)
        def _():
            tot = dbias_ref[...]
            head = lax.broadcasted_iota(jnp.int32, (CH, GM_W), 1) // (GM_W // GM_HEADS)
            out = jnp.zeros((CH, 128), F32)
            for h in range(GM_HEADS):
                s = jnp.sum(jnp.where(head == h, tot, 0.0), axis=1, keepdims=True)
                out = jnp.where(lane == h, s, out)
            dbs_ref[...] = out

    return pl.pallas_call(
        body, grid=(nb,),
        in_specs=[pl.BlockSpec((tm, GM_W), lambda i: (i, 0)), pl.BlockSpec((tm, GM_W), lambda i: (i, 1)),
                  pl.BlockSpec((tm, GM_W), lambda i: (i, 0)),
                  _full((1, GM_W)), _full((1, GM_W)), _full((GM_HEADS, CH, CH)), _full((GM_HEADS, CH, CH)),
                  _full((CH, GM_W))],
        out_specs=[pl.BlockSpec((tm, 2 * GM_W), lambda i: (i, 0)), _full((GM_HEADS, CH, CH)), _full((CH, 128)),
                   _full((1, GM_W)), _full((1, GM_W))],
        out_shape=[S((t, 2 * GM_W), ACT_DTYPE), S((GM_HEADS, CH, CH), F32), S((CH, 128), F32),
                   S((1, GM_W), F32), S((1, GM_W), F32)],
        scratch_shapes=[pltpu.VMEM((CH, GM_W), F32)],
        name=name, compiler_params=_cp("arbitrary"))(proj, proj, dy, ln_g, ln_b, ws, wst, bias)


def _rot(x, cos2, sin2):
    return x * cos2 + pltpu.roll(x, HD // 2, 1) * sin2


def _rot_bwd(dx, cos2, sin2):
    return dx * cos2 + pltpu.roll(dx * sin2, HD // 2, 1)


def _rotary(proj, cos2, sin2, name):
    t = proj.shape[0]
    tm = _row_tile(t, 512)
    scale = HD ** -0.5

    def body(q_ref, k_ref, c_ref, s_ref, rq_ref, rk_ref):
        c, s = c_ref[...], s_ref[...]
        for h in range(HEADS):
            cols = slice(h * HD, (h + 1) * HD)
            rq_ref[:, cols] = _rot(q_ref[:, cols], c, s)
            rk_ref[:, cols] = _rot(k_ref[:, cols], c, s) * scale

    return pl.pallas_call(
        body, grid=(t // tm,),
        in_specs=[pl.BlockSpec((tm, RET_W), lambda i: (i, 1)), pl.BlockSpec((tm, RET_W), lambda i: (i, 2)),
                  pl.BlockSpec((tm, HD), lambda i: (i, 0)), pl.BlockSpec((tm, HD), lambda i: (i, 0))],
        out_specs=[pl.BlockSpec((tm, RET_W), lambda i: (i, 0))] * 2,
        out_shape=[S((t, RET_W), F32)] * 2, name=name, compiler_params=_cp("parallel"))(proj, proj, cos2, sin2)


def _ret_scan(lhs, rhs, rhs_col, lp, ls, gp, gs, name):
    t = lhs.shape[0]
    n = t // CH
    r = 4 if n % 4 == 0 else 1
    ns = n // r

    def body(lp_ref, ls_ref, gp_ref, gs_ref, l1_ref, r1_ref, l2_ref, r2_ref, pre_ref, suf_ref, sp_ref, ss_ref):
        @pl.when(pl.program_id(0) == 0)
        def _():
            sp_ref[...] = jnp.zeros_like(sp_ref)
            ss_ref[...] = jnp.zeros_like(ss_ref)

        def kv(l_ref, r_ref, scale, rows):
            lv = l_ref[rows, :] * scale
            rv = r_ref[rows, :]
            return jnp.concatenate([_dot_tn(lv[:, h * HD:(h + 1) * HD], rv[:, h * HD:(h + 1) * HD])
                                    for h in range(HEADS)], axis=1)

        for j in range(r):
            pre_ref[j] = sp_ref[...]
            sp_ref[...] = sp_ref[...] * gp_ref[...] + kv(l1_ref, r1_ref, lp_ref[...], slice(j * CH, (j + 1) * CH))
        for j in reversed(range(r)):
            suf_ref[j] = ss_ref[...]
            ss_ref[...] = ss_ref[...] * gs_ref[...] + kv(l2_ref, r2_ref, ls_ref[...], slice(j * CH, (j + 1) * CH))

    w = HEADS * HD
    return pl.pallas_call(
        body, grid=(ns,),
        in_specs=[_full((CH, w)), _full((CH, w)), _full((1, w)), _full((1, w)),
                  pl.BlockSpec((r * CH, w), lambda s: (s, 0)), pl.BlockSpec((r * CH, w), lambda s: (s, rhs_col)),
                  pl.BlockSpec((r * CH, w), lambda s: (ns - 1 - s, 0)),
                  pl.BlockSpec((r * CH, w), lambda s: (ns - 1 - s, rhs_col))],
        out_specs=[pl.BlockSpec((r, HD, w), lambda s: (s, 0, 0)), pl.BlockSpec((r, HD, w), lambda s: (ns - 1 - s, 0, 0))],
        out_shape=[S((n, HD, w), F32)] * 2, name=name,
        scratch_shapes=[pltpu.VMEM((HD, w), F32), pltpu.VMEM((HD, w), F32)],
        compiler_params=_cp("arbitrary"))(lp, ls, gp, gs, lhs, rhs, lhs, rhs)


def _ret_out(rq, rk, proj, sf, sb, tb, name, riders=()):
    t = rq.shape[0]
    r = 2 if (t // CH) % 2 == 0 else 1
    tm = r * CH
    w = HEADS * HD

    def body(rq_ref, rk_ref, v_ref, g_ref, sf_ref, sb_ref, dm_ref, xif_ref, xib_ref, a_ref, y_ref):
        for c in range(r):
            rows = slice(c * CH, (c + 1) * CH)
            for h in range(HEADS):
                cols = slice(h * HD, (h + 1) * HD)
                q = rq_ref[rows, cols]
                p = _dot_nt(q, rk_ref[rows, cols]) * dm_ref[h]
                a = (_dot(p, v_ref[rows, cols]) + _dot(q * xif_ref[:, cols], sf_ref[c, :, cols])
                     + _dot(q * xib_ref[:, cols], sb_ref[c, :, cols]))
                a_ref[rows, cols] = a
                o, _ = _standardize(a)
                gv = g_ref[rows, cols]
                y_ref[rows, cols] = (o * (gv * _sigmoid(gv))).astype(y_ref.dtype)

    (a, y), rid = _pcall(
        body, [rq, rk, proj, proj, sf, sb, tb["dm"], tb["xif"], tb["xib"]], riders, grid=(t // tm,),
        in_specs=[pl.BlockSpec((tm, w), lambda i: (i, 0)), pl.BlockSpec((tm, w), lambda i: (i, 0)),
                  pl.BlockSpec((tm, w), lambda i: (i, 3)), pl.BlockSpec((tm, w), lambda i: (i, 4)),
                  pl.BlockSpec((r, HD, w), lambda i: (i, 0, 0)), pl.BlockSpec((r, HD, w), lambda i: (i, 0, 0)),
                  _full((HEADS, CH, CH)), _full((CH, w)), _full((CH, w))],
        out_specs=[pl.BlockSpec((tm, w), lambda i: (i, 0))] * 2,
        out_shape=[S((t, w), F32), S((t, w), ACT_DTYPE)], name=name, sem=("parallel",))
    return a, y, rid


def _ret_bwd_pre(dy, a, proj, name):
    t = dy.shape[0]
    tm = _row_tile(t, 512)
    w = HEADS * HD

    def body(dy_ref, a_ref, g_ref, da_ref, dg_ref):
        for h in range(HEADS):
            cols = slice(h * HD, (h + 1) * HD)
            o, r = _standardize(a_ref[:, cols])
            gv = g_ref[:, cols]
            s = _sigmoid(gv)
            dyv = dy_ref[:, cols]
            dg_ref[:, cols] = (dyv * o * (s * (1.0 + gv * (1.0 - s)))).astype(dg_ref.dtype)
            da_ref[:, cols] = _standardize_bwd(dyv * (gv * s), o, r).astype(da_ref.dtype)

    return pl.pallas_call(
        body, grid=(t // tm,),
        in_specs=[pl.BlockSpec((tm, w), lambda i: (i, 0)), pl.BlockSpec((tm, w), lambda i: (i, 0)),
                  pl.BlockSpec((tm, w), lambda i: (i, 4))],
        out_specs=[pl.BlockSpec((tm, w), lambda i: (i, 0))] * 2,
        out_shape=[S((t, w), ACT_DTYPE)] * 2, name=name, compiler_params=_cp("parallel"))(dy, a, proj)


def _ret_bwd_main(rq, rk, proj, da, sf, sb, gf, gb, tb, name, riders=()):
    t = rq.shape[0]
    r = 2 if (t // CH) % 2 == 0 else 1
    tm = r * CH
    w = HEADS * HD
    scale = HD ** -0.5

    def body(rq_ref, rk_ref, v_ref, da_ref, sf_ref, sb_ref, gf_ref, gb_ref, dm_ref, dmt_ref,
             xif_ref, xib_ref, zf_ref, zb_ref, c_ref, s_ref, o_ref):
        for c in range(r):
            rows = slice(c * CH, (c + 1) * CH)
            cos2, sin2 = c_ref[rows, :], s_ref[rows, :]
            for h in range(HEADS):
                cols = slice(h * HD, (h + 1) * HD)
                q, k, v, dav = rq_ref[rows, cols], rk_ref[rows, cols], v_ref[rows, cols], da_ref[rows, cols]
                qm, km, vm, dam = _mx(q), _mx(k), _mx(v), _mx(dav)
                dm, dmt = dm_ref[h], dmt_ref[h]
                pt = _dot_nt(km, qm) * dmt
                dp = _dot_nt(dam, vm) * dm
                dpt = _dot_nt(vm, dam) * dmt
                sfh, sbh, gfh, gbh = sf_ref[c, :, cols], sb_ref[c, :, cols], gf_ref[c, :, cols], gb_ref[c, :, cols]
                zf, zb = zf_ref[:, cols], zb_ref[:, cols]
                dv = _dot(pt, dam) + zf * _dot(km, gfh) + zb * _dot(km, gbh)
                drq = _dot(dp, km) + xif_ref[:, cols] * _dot_nt(dam, sfh) + xib_ref[:, cols] * _dot_nt(dam, sbh)
                drk = _dot(dpt, qm) + _dot_nt(zf * v, gfh) + _dot_nt(zb * v, gbh)
                o_ref[rows, h * HD:(h + 1) * HD] = _rot_bwd(drq, cos2, sin2).astype(o_ref.dtype)
                o_ref[rows, w + h * HD:w + (h + 1) * HD] = (_rot_bwd(drk, cos2, sin2) * scale).astype(o_ref.dtype)
                o_ref[rows, 2 * w + h * HD:2 * w + (h + 1) * HD] = dv.astype(o_ref.dtype)

    st = pl.BlockSpec((r, HD, w), lambda i: (i, 0, 0))
    (out,), rid = _pcall(
        body, [rq, rk, proj, da, sf, sb, gf, gb, tb["dm"], tb["dmt"], tb["xif"], tb["xib"], tb["zf"], tb["zb"],
               tb["cos2"], tb["sin2"]], riders, grid=(t // tm,),
        in_specs=[pl.BlockSpec((tm, w), lambda i: (i, 0)), pl.BlockSpec((tm, w), lambda i: (i, 0)),
                  pl.BlockSpec((tm, w), lambda i: (i, 3)), pl.BlockSpec((tm, w), lambda i: (i, 0)),
                  st, st, st, st, _full((HEADS, CH, CH)), _full((HEADS, CH, CH)),
                  _full((CH, w)), _full((CH, w)), _full((CH, w)), _full((CH, w)),
                  pl.BlockSpec((tm, HD), lambda i: (i, 0)), pl.BlockSpec((tm, HD), lambda i: (i, 0))],
        out_specs=[pl.BlockSpec((tm, 3 * w), lambda i: (i, 0))],
        out_shape=[S((t, 3 * w), ACT_DTYPE)], name=name, sem=("parallel",))
    return out, rid


CONV_TM = 256
CONV_SUB = 64
A_COL = (2 * GM_W + 4 * RET_W) // CV_W
G_COL = A_COL + 1


def _halo_specs(t, tm, col):
    nb16 = t // HALO
    per = tm // HALO
    return [pl.BlockSpec((tm, CV_W), lambda i: (i, col)),
            pl.BlockSpec((HALO, CV_W), lambda i: (jnp.maximum(i * per - 1, 0), col)),
            pl.BlockSpec((HALO, CV_W), lambda i: (jnp.minimum((i + 1) * per, nb16 - 1), col))]


def _fill_padded(dst_ref, prev, main, nxt, tm, i, nb):
    dst_ref[0:HALO, :] = jnp.where(i > 0, prev, 0.0)
    dst_ref[HALO:HALO + tm, :] = main
    dst_ref[HALO + tm:2 * HALO + tm, :] = jnp.where(i < nb - 1, nxt, 0.0)


SUBLANES = 8


def _fill_shifted(sh_ref, src_ref, tm):
    n = tm + 2 * HALO - SUBLANES
    for b in range(SUBLANES):
        sh_ref[b, 0:n, :] = src_ref[pl.ds(b, n), :]


def _tap(sh_ref, off, rows):
    return sh_ref[off % SUBLANES, pl.ds(off - off % SUBLANES, rows), :]


def _conv_fwd(proj, cw, cb, ln_g, ln_b, name, riders=()):
    t = proj.shape[0]
    tm = _row_tile(t, CONV_TM)
    nb = t // tm

    def body(a_ref, ap_ref, an_ref, g_ref, gp_ref, gn_ref, w_ref, b_ref, lg_ref, lb_ref, y_ref, hc_ref,
             hp_ref, sh_ref):
        i = pl.program_id(0)
        _fill_padded(hp_ref, ap_ref[...] * _sigmoid(gp_ref[...]), a_ref[...] * _sigmoid(g_ref[...]),
                     an_ref[...] * _sigmoid(gn_ref[...]), tm, i, nb)
        _fill_shifted(sh_ref, hp_ref, tm)
        for sb in range(tm // CONV_SUB):
            acc = jnp.zeros((CONV_SUB, CV_W), F32) + b_ref[...]
            for k in range(KCONV):
                acc = acc + w_ref[k:k + 1, :] * _tap(sh_ref, sb * CONV_SUB + k + 1, CONV_SUB)
            rows = slice(sb * CONV_SUB, (sb + 1) * CONV_SUB)
            hc_ref[rows, :] = acc
            o, _ = _standardize(acc)
            z = o * lg_ref[...] + lb_ref[...]
            y_ref[rows, :] = (z * _sigmoid(z)).astype(y_ref.dtype)

    (y, hc), rid = _pcall(
        body, [proj, proj, proj, proj, proj, proj, cw, cb, ln_g, ln_b], riders, grid=(nb,),
        in_specs=_halo_specs(t, tm, A_COL) + _halo_specs(t, tm, G_COL)
        + [_full((32, CV_W)), _full((1, CV_W)), _full((1, CV_W)), _full((1, CV_W))],
        out_specs=[pl.BlockSpec((tm, CV_W), lambda i: (i, 0))] * 2,
        out_shape=[S((t, CV_W), ACT_DTYPE), S((t, CV_W), F32)], name=name, sem=("parallel",),
        scratch_shapes=[pltpu.VMEM((tm + 2 * HALO, CV_W), F32), pltpu.VMEM((SUBLANES, tm + 2 * HALO, CV_W), F32)])
    return y, hc, rid


def _conv_bwd(proj, dy, hc, cw, ln_g, ln_b, name, riders=()):
    t = proj.shape[0]
    tm = _row_tile(t, CONV_TM)
    nb = t // tm

    def body(a_ref, ap_ref, an_ref, g_ref, gp_ref, gn_ref, dy_ref, dyp_ref, dyn_ref, hc_ref, hcp_ref, hcn_ref,
             w_ref, lg_ref, lb_ref, d_ref, dw_ref, dcb_ref, dlg_ref, dlb_ref, hp_ref, dhp_ref, dwacc_ref,
             sh_ref, dsh_ref):
        i = pl.program_id(0)
        first = i == 0

        def dhc_of(dyv, hcv):
            o, r = _standardize(hcv)
            z = o * lg_ref[...] + lb_ref[...]
            s = _sigmoid(z)
            dz = dyv * (s * (1.0 + z * (1.0 - s)))
            return _standardize_bwd(dz * lg_ref[...], o, r), dz, o

        dhc, dz, o = dhc_of(dy_ref[...], hc_ref[...])
        _acc_out(dlg_ref, jnp.sum(dz * o, axis=0, keepdims=True), first)
        _acc_out(dlb_ref, jnp.sum(dz, axis=0, keepdims=True), first)
        _acc_out(dcb_ref, jnp.sum(dhc, axis=0, keepdims=True), first)
        _fill_padded(dhp_ref, dhc_of(dyp_ref[...], hcp_ref[...])[0], dhc, dhc_of(dyn_ref[...], hcn_ref[...])[0],
                     tm, i, nb)
        _fill_padded(hp_ref, ap_ref[...] * _sigmoid(gp_ref[...]), a_ref[...] * _sigmoid(g_ref[...]),
                     an_ref[...] * _sigmoid(gn_ref[...]), tm, i, nb)

        _fill_shifted(sh_ref, hp_ref, tm)
        _fill_shifted(dsh_ref, dhp_ref, tm)

        @pl.when(first)
        def _():
            dwacc_ref[...] = jnp.zeros_like(dwacc_ref)

        for sb in range(tm // CONV_SUB):
            base = sb * CONV_SUB
            dmain = dhp_ref[pl.ds(HALO + base, CONV_SUB), :]
            dh = jnp.zeros((CONV_SUB, CV_W), F32)
            for k in range(KCONV):
                dh = dh + w_ref[k:k + 1, :] * _tap(dsh_ref, base + 2 * HALO - 1 - k, CONV_SUB)
                prod = dmain * _tap(sh_ref, base + k + 1, CONV_SUB)
                dwacc_ref[k * 8:(k + 1) * 8, :] += jnp.sum(prod.reshape(CONV_SUB // 8, 8, CV_W), axis=0)
            rows = slice(base, base + CONV_SUB)
            s = _sigmoid(g_ref[rows, :])
            d_ref[rows, :CV_W] = (dh * s).astype(d_ref.dtype)
            d_ref[rows, CV_W:] = (dh * a_ref[rows, :] * (s * (1.0 - s))).astype(d_ref.dtype)

        @pl.when(i == nb - 1)
        def _():
            for k in range(KCONV):
                dw_ref[k:k + 1, :] = jnp.sum(dwacc_ref[k * 8:(k + 1) * 8, :], axis=0, keepdims=True)
            dw_ref[KCONV:32, :] = jnp.zeros((32 - KCONV, CV_W), F32)

    hs = [pl.BlockSpec((tm, CV_W), lambda i: (i, 0)),
          pl.BlockSpec((HALO, CV_W), lambda i: (jnp.maximum(i * (tm // HALO) - 1, 0), 0)),
          pl.BlockSpec((HALO, CV_W), lambda i: (jnp.minimum((i + 1) * (tm // HALO), t // HALO - 1), 0))]
    outs, rid = _pcall(
        body, [proj, proj, proj, proj, proj, proj, dy, dy, dy, hc, hc, hc, cw, ln_g, ln_b], riders, grid=(nb,),
        in_specs=_halo_specs(t, tm, A_COL) + _halo_specs(t, tm, G_COL) + hs + hs
        + [_full((32, CV_W)), _full((1, CV_W)), _full((1, CV_W))],
        out_specs=[pl.BlockSpec((tm, 2 * CV_W), lambda i: (i, 0)), _full((32, CV_W)), _full((1, CV_W)),
                   _full((1, CV_W)), _full((1, CV_W))],
        out_shape=[S((t, 2 * CV_W), ACT_DTYPE), S((32, CV_W), F32), S((1, CV_W), F32), S((1, CV_W), F32),
                   S((1, CV_W), F32)],
        name=name, sem=("arbitrary",),
        scratch_shapes=[pltpu.VMEM((tm + 2 * HALO, CV_W), F32), pltpu.VMEM((tm + 2 * HALO, CV_W), F32),
                        pltpu.VMEM((32 * 8, CV_W), F32), pltpu.VMEM((SUBLANES, tm + 2 * HALO, CV_W), F32),
                        pltpu.VMEM((SUBLANES, tm + 2 * HALO, CV_W), F32)])
    return (*outs, rid)


def _loss_head(x, g, target, name):
    t = x.shape[0]
    tm = _row_tile(t, 512)

    def body(x_ref, g_ref, t_ref, dx_ref, dg_ref, l_ref):
        first = pl.program_id(0) == 0
        xv = x_ref[...]
        r = _rms_r(xv)
        e = xv * r * g_ref[...] - t_ref[...]
        dx, dgrow = _rms_bwd(e * (1.0 / D), xv, r, g_ref[...])
        dx_ref[...] = dx
        _acc_out(dg_ref, jnp.sum(dgrow, axis=0, keepdims=True), first)
        part = 0.5 * jnp.sum(jnp.mean(e * e, axis=-1, keepdims=True), axis=0, keepdims=True)
        _acc_out(l_ref, jnp.broadcast_to(part, (8, 128)), first)

    return pl.pallas_call(
        body, grid=(t // tm,),
        in_specs=[pl.BlockSpec((tm, D), lambda i: (i, 0)), _full((1, D)), pl.BlockSpec((tm, D), lambda i: (i, 0))],
        out_specs=[pl.BlockSpec((tm, D), lambda i: (i, 0)), _full((1, D)), _full((8, 128))],
        out_shape=[S((t, D), F32), S((1, D), F32), S((8, 128), F32)], name=name,
        compiler_params=_cp("arbitrary"))(x, g, target)


def _as2d(a):
    return a.reshape(-1, a.shape[-1])


def _ew_tile(rows, cols, n_arrays):
    budget = VMEM_LIMIT // 2
    tr = rows
    while tr * cols * 4 * n_arrays * 2 > budget and tr % 16 == 0:
        tr //= 2
    assert rows % tr == 0
    return tr


def _adamw(w, g, m, v, name):
    shape = w.shape
    w2, g2, m2, v2 = _as2d(w), _as2d(g), _as2d(m), _as2d(v)
    rows, cols = w2.shape
    tr = _ew_tile(rows, cols, 7)

    def body(w_ref, g_ref, m_ref, v_ref, d_ref, nm_ref, nv_ref):
        gv = g_ref[...]
        nm = ADAM_B1 * m_ref[...] + (1.0 - ADAM_B1) * gv
        nv = ADAM_B2 * v_ref[...] + (1.0 - ADAM_B2) * (gv * gv)
        m_hat = nm / (1.0 - ADAM_B1 ** ADAM_STEP)
        v_hat = nv / (1.0 - ADAM_B2 ** ADAM_STEP)
        d_ref[...] = -ADAM_LR * (m_hat / (jnp.sqrt(v_hat) + ADAM_EPS) + ADAM_WD * w_ref[...])
        nm_ref[...] = nm
        nv_ref[...] = nv

    spec = pl.BlockSpec((tr, cols), lambda i: (i, 0))
    outs = pl.pallas_call(body, grid=(rows // tr,), in_specs=[spec] * 4, out_specs=[spec] * 3,
                          out_shape=[S((rows, cols), F32)] * 3, name=name,
                          compiler_params=_cp("parallel"))(w2, g2, m2, v2)
    return tuple(o.reshape(shape) for o in outs)


BIG = (("w_in", "col"), ("w_out", "row"), ("w_ffn_in", "col"), ("w_ffn_out", "row"))
NBIG = len(BIG)


def _cast_to_gathered(w, l, me, name):
    _, r_, c_ = w.shape
    tr = _ew_tile(r_, c_, 2)

    def body(me_ref, w_ref, o_ref):
        o_ref[...] = w_ref[...].astype(o_ref.dtype)

    gs = pltpu.PrefetchScalarGridSpec(
        num_scalar_prefetch=1, grid=(r_ // tr,),
        in_specs=[pl.BlockSpec((None, tr, c_), lambda i, s: (l, i, 0))],
        out_specs=pl.BlockSpec((None, tr, c_), lambda i, s: (s[0], i, 0)))
    out = pl.pallas_call(body, grid_spec=gs, out_shape=S((N_CHIPS, r_, c_), MXU_DTYPE), name=name,
                         compiler_params=_cp("parallel"))(me.reshape(1), w)
    return out.reshape(N_CHIPS, 2, r_ // 2, c_)


def _all_gather(bufs, name, per_core=False):
    n = len(bufs)

    def body(*refs):
        i_refs, o_refs = refs[:n], refs[n:2 * n]
        isend, irecv, dsend, drecv, osend, orecv = refs[2 * n:]
        pos = _mesh_pos()
        x, y, c, me, _, _ = pos
        ici = _rider_copies("ici", i_refs, o_refs, isend, irecv, pos)
        d2d = _rider_copies("d2d", o_refs, o_refs, dsend, drecv, pos)
        own = []
        if per_core:
            for b in range(n):
                own.append(tuple(pltpu.make_async_remote_copy(
                    src_ref=s_, dst_ref=d_, send_sem=osend.at[b], recv_sem=orecv.at[b],
                    device_id=(x, y, 1 - c), device_id_type=MESH)
                    for s_, d_ in ((i_refs[b].at[me, c], o_refs[b].at[me, c]),
                                   (o_refs[b].at[me, 1 - c], o_refs[b].at[me, 1 - c]))))
        for cp, _ in ici + own:
            cp.start()
        for (_, land), (fwd, _) in zip(ici, d2d):
            land.wait_recv()
            fwd.start()
        for _, land in d2d + own:
            land.wait_recv()
        for cp, _ in ici + d2d + own:
            cp.wait_send()

    return pl.pallas_call(
        body, in_specs=[ANY] * n, out_specs=[ANY] * n, out_shape=[S(a.shape, a.dtype) for a in bufs],
        input_output_aliases={w: w for w in range(n)}, name=name,
        scratch_shapes=[pltpu.SemaphoreType.DMA((n, 3))] * 4 + [pltpu.SemaphoreType.DMA((n,))] * 2)(*bufs)


def _pair_exchange(grads, name):
    n = len(grads)

    def body(*refs):
        g_refs, theirs = refs[:n], refs[n:2 * n]
        send, recv = refs[2 * n:]
        x, y, c, *_ = _mesh_pos()
        cps = []
        for w in range(n):
            cp = pltpu.make_async_remote_copy(
                src_ref=g_refs[w].at[:, 1 - c], dst_ref=theirs[w], send_sem=send.at[w], recv_sem=recv.at[w],
                device_id=(x, y, 1 - c), device_id_type=MESH)
            cp.start()
            cps.append(cp)
        for cp in cps:
            cp.wait()

    return pl.pallas_call(
        body, in_specs=[ANY] * n, out_specs=[ANY] * n,
        out_shape=[S(a.shape[:1] + a.shape[2:], a.dtype) for a in grads], name=name,
        scratch_shapes=[pltpu.SemaphoreType.DMA((n,))] * 2)(*grads)


def _pair_sum(g, theirs, core, name):
    _, _, rh, c_ = g.shape
    tr = _ew_tile(rh, c_, 2)

    def body(s_ref, g_ref, t_ref, o_ref):
        o_ref[...] = (g_ref[...].astype(F32) + t_ref[...].astype(F32)).astype(o_ref.dtype)

    blk = pl.BlockSpec((None, tr, c_), lambda j, i, s: (j, i, 0))
    gs = pltpu.PrefetchScalarGridSpec(
        num_scalar_prefetch=1, grid=(N_CHIPS, rh // tr),
        in_specs=[pl.BlockSpec((None, None, tr, c_), lambda j, i, s: (j, s[0], i, 0)), blk], out_specs=blk)
    return pl.pallas_call(body, grid_spec=gs, out_shape=S(theirs.shape, theirs.dtype), name=name,
                          compiler_params=_cp("parallel", "parallel"))(core.reshape(1), g, theirs)


def _chip_sum(q, got, l, me, core, into, name):
    _, rh, c_ = got.shape
    tr = _ew_tile(rh, c_, 4)

    def body(s_ref, q_ref, g0_ref, g1_ref, g2_ref, o_ref):
        acc = q_ref[...].astype(F32)
        for r in (g0_ref, g1_ref, g2_ref):
            acc = acc + r[...].astype(F32)
        o_ref[...] = acc

    in_specs = [pl.BlockSpec((None, tr, c_), lambda i, s: (s[0], i, 0))] + [
        pl.BlockSpec((None, tr, c_), functools.partial(lambda k, i, s: (k, i, 0), k)) for k in range(3)]
    return _call_into(
        body, into, in_specs, [jnp.stack([me, core]), q, got, got, got], n_prefetch=1, grid=(rh // tr,),
        out_specs=pl.BlockSpec((None, None, tr, c_), lambda i, s: (l, s[1], i, 0)),
        out_shape=S((DEPTH, 2, rh, c_), F32), name=name, compiler_params=_cp("parallel"))


def _pair_gather(gs4):
    def body(*refs):
        i_refs, o_refs = refs[:NBIG], refs[NBIG:2 * NBIG]
        send, recv = refs[2 * NBIG:]
        x, y, c, *_ = _mesh_pos()
        cps = []
        for w in range(NBIG):
            cp = pltpu.make_async_remote_copy(
                src_ref=i_refs[w].at[:, c], dst_ref=o_refs[w].at[:, c], send_sem=send.at[w], recv_sem=recv.at[w],
                device_id=(x, y, 1 - c), device_id_type=MESH)
            cp.start()
            cps.append(cp)
        for cp in cps:
            cp.wait()

    outs = pl.pallas_call(
        body, in_specs=[ANY] * NBIG, out_specs=[ANY] * NBIG, out_shape=[S(a.shape, a.dtype) for a in gs4],
        input_output_aliases={w: w for w in range(NBIG)}, name="grad_pair_gather",
        scratch_shapes=[pltpu.SemaphoreType.DMA((NBIG,))] * 2)(*gs4)
    return [o.reshape(o.shape[0], 2 * o.shape[2], o.shape[3]) for o in outs]


def _all_reduce_small(p, me, core, name):
    rows = p.shape[0]

    def place(s_ref, p_ref, o_ref):
        o_ref[...] = p_ref[...]

    gs = pltpu.PrefetchScalarGridSpec(
        num_scalar_prefetch=1, grid=(1,), in_specs=[pl.BlockSpec((rows, 128), lambda i, s: (0, 0))],
        out_specs=pl.BlockSpec((None, None, rows, 128), lambda i, s: (s[0], s[1], 0, 0)))
    mine = pl.pallas_call(place, grid_spec=gs, out_shape=S((N_CHIPS, 2, rows, 128), F32), name=name + "_place",
                          compiler_params=_cp("arbitrary"))(jnp.stack([me, core]), p)
    parts = _all_gather([mine], name + "_gather", per_core=True)[0]

    def total(g_ref, o_ref):
        acc = g_ref[0, 0]
        for j in range(N_CHIPS):
            for c in range(2):
                if (j, c) != (0, 0):
                    acc = acc + g_ref[j, c]
        o_ref[...] = acc

    vm = pl.BlockSpec(memory_space=pltpu.VMEM)
    return pl.pallas_call(total, in_specs=[vm], out_specs=vm, out_shape=S((rows, 128), F32), name=name + "_sum",
                          compiler_params=pltpu.CompilerParams(vmem_limit_bytes=VMEM_LIMIT))(parts)


PACK_UNIT = 8 * 128


def _pack(arrs):
    parts = []
    for a in arrs:
        flat = a.reshape(-1)
        pad = (-flat.shape[0]) % PACK_UNIT
        parts.append(jnp.pad(flat, (0, pad)).reshape(-1, 128))
    return jnp.concatenate(parts, axis=0)


def _unpack(buf, shapes):
    outs, row = [], 0
    for shp in shapes:
        n = int(np.prod(shp))
        rows = -(-n // PACK_UNIT) * 8
        outs.append(buf[row:row + rows].reshape(-1)[:n].reshape(shp))
        row += rows
    return outs


SMALL = ("norm1_g", "gm_ln_g", "gm_ln_b", "gm_ws", "gm_bs", "conv_w", "conv_b", "conv_ln_g", "conv_ln_b",
         "norm2_g", "final_g")
WEIGHTS = ("norm1_g", "w_in", "gm_ln_g", "gm_ln_b", "gm_ws", "gm_bs", "conv_w", "conv_b", "conv_ln_g",
           "conv_ln_b", "w_out", "norm2_g", "w_ffn_in", "w_ffn_out", "final_g")


def kernel(x, norm1_g, w_in, gm_ln_g, gm_ln_b, gm_ws, gm_bs, conv_w, conv_b, conv_ln_g, conv_ln_b, w_out, norm2_g, w_ffn_in, w_ffn_out, final_g, loss_target, m_norm1_g, m_w_in, m_gm_ln_g, m_gm_ln_b, m_gm_ws, m_gm_bs, m_conv_w, m_conv_b, m_conv_ln_g, m_conv_ln_b, m_w_out, m_norm2_g, m_w_ffn_in, m_w_ffn_out, m_final_g, v_norm1_g, v_w_in, v_gm_ln_g, v_gm_ln_b, v_gm_ws, v_gm_bs, v_conv_w, v_conv_b, v_conv_ln_g, v_conv_ln_b, v_w_out, v_norm2_g, v_w_ffn_in, v_w_ffn_out, v_final_g):
    given = dict(locals())
    t = x.shape[1]
    xc = x.reshape(t, D)
    target = loss_target.reshape(t, D)
    me = 2 * lax.axis_index("x") + lax.axis_index("y")
    core = lax.axis_index("c")
    tb = _tables(t)

    me = me.astype(jnp.int32)
    core = core.astype(jnp.int32)
    names = [n for n, _ in BIG]
    kinds = dict(BIG)
    gathered = [{n: _cast_to_gathered(given[n], l, me, f"cast_{n}{l}") for n in names} for l in range(DEPTH)]
    gathered[0]["w_in"] = _all_gather([gathered[0]["w_in"]], "all_gather_w_in0")[0]

    def weight(l, n):
        b = gathered[l][n]
        r_, c_ = 2 * b.shape[2], b.shape[3]
        return b.reshape(N_CHIPS, r_, c_) if kinds[n] == "col" else b.reshape(N_CHIPS * r_, c_)

    cshard = CV_W // N_CHIPS
    placed = lax.dynamic_update_slice(jnp.zeros((DEPTH, KCONV, CV_W), F32),
                                      conv_w * (core == 0).astype(F32), (0, 0, me * cshard))
    conv_w_full = _unpack(_all_reduce_small(_pack([placed]), me, core, "gather_conv_w"), [(DEPTH, KCONV, CV_W)])[0]
    cw32 = jnp.pad(conv_w_full, ((0, 0), (0, 32 - KCONV), (0, 0)))

    def row(a, l):
        return a[l].reshape(1, -1)

    saved = []
    early = ["w_in", "w_out", "w_ffn_in"]
    for l in range(DEPTH):
        cur = gathered[l]
        nxt = gathered[l + 1] if l + 1 < DEPTH else None
        sv = {"x": xc}
        bias = jnp.repeat(gm_bs[l].T, GM_W // GM_HEADS, axis=1)
        first = ["w_ffn_in"] if l == 0 else ["w_ffn_out"]
        late = ["w_out", "w_ffn_out"]
        proj, rid = _norm_mm(xc, row(norm1_g, l), weight(l, "w_in"), F32, f"in_proj{l}", 512,
                             [("ici" if l == 0 else "d2d", [cur[n] for n in first])])
        cur.update(zip(first, rid))
        y_gm, rid = _gm_fwd(proj, row(gm_ln_g, l), row(gm_ln_b, l), gm_ws[l], bias, f"gm_fwd{l}",
                            [("d2d", [cur[n] for n in first])] if l == 0 else ())
        cur.update(zip(first, rid))
        rq, rk = _rotary(proj, tb["cos2"], tb["sin2"], f"rotary{l}")
        sf, sb = _ret_scan(rk, proj, 3, tb["zf"], tb["zb"], tb["gcf"], tb["gcb"], f"ret_state{l}")
        a, y_ret, rid = _ret_out(rq, rk, proj, sf, sb, tb, f"ret_out{l}",
                                 [("ici", [cur[n] for n in late])] if l == 0 else ())
        cur.update(zip(late, rid))
        y_cv, hc, rid = _conv_fwd(proj, cw32[l], row(conv_b, l), row(conv_ln_g, l), row(conv_ln_b, l),
                                  f"conv_fwd{l}", [("d2d", [cur[n] for n in late])] if l == 0 else ())
        cur.update(zip(late, rid))
        x_mid = _parts_mm_res([y_gm, y_ret, y_cv], weight(l, "w_out"), xc, f"out_proj{l}")
        ff, rid = _norm_mm(x_mid, row(norm2_g, l), weight(l, "w_ffn_in"), ACT_DTYPE, f"ffn_in{l}", 512,
                           [("ici", [nxt[n] for n in early])] if nxt else ())
        if nxt:
            nxt.update(zip(early, rid))
        xc, rid = _swiglu_mm_res(ff, weight(l, "w_ffn_out"), x_mid, f"ffn_out{l}",
                                 [("d2d", [nxt[n] for n in early]), ("ici", [nxt["w_ffn_out"]])] if nxt else ())
        if nxt:
            nxt.update(zip(early + ["w_ffn_out"], rid))
        sv.update(bias=bias, proj=proj, y_gm=y_gm, rq=rq, rk=rk, sf=sf, sb=sb, a=a, y_ret=y_ret, y_cv=y_cv,
                  hc=hc, x_mid=x_mid, ff=ff)
        saved.append(sv)

    dx, d_final_g, lpart = _loss_head(xc, final_g.reshape(1, D), target, "loss_head")

    small_g = {n: [None] * DEPTH for n in SMALL}
    qs = [{} for _ in range(DEPTH)]
    got = [{} for _ in range(DEPTH)]
    ffn_w, mix_w = ["w_ffn_out", "w_ffn_in"], ["w_out", "w_in"]

    def pair_reduce(l, group, big_g):
        g4 = [big_g[n].reshape(N_CHIPS, 2, given[n].shape[1] // 2, given[n].shape[2]) for n in group]
        theirs = _pair_exchange(g4, f"grad_pair_exchange_{group[0]}{l}")
        qs[l].update({n: _pair_sum(g, th, core, f"pair_sum_{n}{l}") for n, g, th in zip(group, g4, theirs)})
        return [qs[l][n] for n in group]

    for l in reversed(range(DEPTH)):
        sv = saved[l]
        proj = sv["proj"]
        big_g = {}
        dff = _dx_swiglu(dx, weight(l, "w_ffn_out"), sv["ff"], f"ffn_out_dx{l}")
        big_g["w_ffn_out"] = _dw_swiglu(sv["ff"], dx, f"ffn_out_dw{l}")
        dx_mid, dg2, _ = _dx_norm([dff], weight(l, "w_ffn_in"), sv["x_mid"], row(norm2_g, l), dx,
                                  f"ffn_in_dx{l}", 512)
        big_g["w_ffn_in"], _ = _dw_norm_cols(sv["x_mid"], row(norm2_g, l), dff, w_ffn_in.shape[2], f"ffn_in_dw{l}")
        q_ffn = pair_reduce(l, ffn_w, big_g)
        dy_gm, dy_ret, dy_cv = _dx_parts(dx_mid, weight(l, "w_out"), [GM_W, RET_W, CV_W], f"out_proj_dx{l}")
        big_g["w_out"] = _dw_parts([sv["y_gm"], sv["y_ret"], sv["y_cv"]], dx_mid, f"out_proj_dw{l}")
        d_cv, dcw, dcb, dclg, dclb, rid = _conv_bwd(proj, dy_cv, sv["hc"], cw32[l], row(conv_ln_g, l),
                                                    row(conv_ln_b, l), f"conv_bwd{l}", [("scatter", q_ffn[:1])])
        got[l].update(zip(ffn_w[:1], rid))
        da, d_g = _ret_bwd_pre(dy_ret, sv["a"], proj, f"ret_bwd_pre{l}")
        gb_, gf_ = _ret_scan(sv["rq"], da, 0, tb["xib"], tb["xif"], tb["gcb"], tb["gcf"], f"ret_bwd_state{l}")
        d_qkv, rid = _ret_bwd_main(sv["rq"], sv["rk"], proj, da, sv["sf"], sv["sb"], gf_, gb_, tb,
                                   f"ret_bwd_main{l}", [("scatter", q_ffn[1:])])
        got[l].update(zip(ffn_w[1:], rid))
        d_gm, dws, dbs, dglg, dglb = _gm_bwd(proj, dy_gm, row(gm_ln_g, l), row(gm_ln_b, l), gm_ws[l],
                                             jnp.swapaxes(gm_ws[l], 1, 2), sv["bias"], f"gm_bwd{l}")
        dparts = [d_gm, d_qkv, d_g, d_cv]
        big_g["w_in"] = _dw_norm_parts(sv["x"], row(norm1_g, l), dparts, w_in.shape[2], f"in_proj_dw{l}")
        q_mix = pair_reduce(l, mix_w, big_g)
        dx, dg1, rid = _dx_norm(dparts, weight(l, "w_in"), sv["x"], row(norm1_g, l), dx_mid, f"in_proj_dx{l}", 512,
                                [("scatter", q_mix)])
        got[l].update(zip(mix_w, rid))
        for n, val in (("norm1_g", dg1[0]), ("gm_ln_g", dglg[0]), ("gm_ln_b", dglb[0]), ("gm_ws", dws),
                       ("gm_bs", dbs[:, :GM_HEADS].T), ("conv_w", dcw[:KCONV]), ("conv_b", dcb[0]),
                       ("conv_ln_g", dclg[0]), ("conv_ln_b", dclb[0]), ("norm2_g", dg2[0])):
            small_g[n][l] = val

    small_shapes = [given[n].shape if n != "conv_w" else (DEPTH, KCONV, CV_W) for n in SMALL]
    partials = [d_final_g[0] if n == "final_g" else jnp.stack(small_g[n]) for n in SMALL]
    summed = _unpack(_all_reduce_small(_pack(partials + [lpart]), me, core, "all_reduce_small_grads"),
                     small_shapes + [lpart.shape])
    loss = summed[-1][0, 0]
    reduced = dict(zip(SMALL, summed))
    reduced["conv_w"] = lax.dynamic_slice(reduced["conv_w"], (0, 0, me * cshard), (DEPTH, KCONV, cshard))

    halves = [None] * NBIG
    for l in reversed(range(DEPTH)):
        halves = [_chip_sum(qs[l][n], got[l][n], l, me, core, h, f"chip_sum_{n}{l}") for n, h in zip(names, halves)]
    grads = dict(zip(names, _pair_gather(halves)))
    grads.update(reduced)

    delta, new_m, new_v = {}, {}, {}
    for n, _ in BIG:
        delta[n], new_m[n], new_v[n] = _adamw(given[n], grads[n], given["m_" + n], given["v_" + n], f"adamw_{n}")
    shapes = [given[n].shape for n in SMALL]
    packed = [_pack([src[n] if src is grads else src[p + n] for n in SMALL])
              for src, p in ((given, ""), (grads, ""), (given, "m_"), (given, "v_"))]
    outs = _adamw(*packed, "adamw_small")
    for dst, buf in zip((delta, new_m, new_v), outs):
        dst.update(zip(SMALL, _unpack(buf, shapes)))

    return (loss, dx.reshape(1, t, D), *[grads[n] for n in WEIGHTS], *[delta[n] for n in WEIGHTS],
            *[new_m[n] for n in WEIGHTS], *[new_v[n] for n in WEIGHTS])
```

```python
import functools
import math

import numpy as np
import jax
import jax.numpy as jnp
from jax import lax
from jax.experimental import pallas as pl
from jax.experimental.pallas import tpu as pltpu

F32 = jnp.float32
BF16 = jnp.bfloat16
MXU_DTYPE = BF16
ACT_DTYPE = BF16
S = jax.ShapeDtypeStruct

D = 1024
DEPTH = 2
GM_W = 256
GM_HEADS = 4
RET_W = 512
HEADS = 4
HD = 128
CV_W = 256
KCONV = 31
IN_W = 2 * GM_W + 4 * RET_W + 2 * CV_W
FFN_H = 2816
CH = 128
ROPE_BASE = 10000.0
EPS = 1e-6
N_CHIPS = 4
N_DEV = 8
HALO = 16

ADAM_LR = 0.001
ADAM_B1 = 0.9
ADAM_B2 = 0.999
ADAM_EPS = 1e-08
ADAM_WD = 0.01
ADAM_STEP = 10

VMEM_LIMIT = 52 * 1024 * 1024
MESH = pl.DeviceIdType.MESH


def _cp(*sem, vmem=VMEM_LIMIT):
    return pltpu.CompilerParams(dimension_semantics=tuple(sem), vmem_limit_bytes=vmem)


def _mx(a):
    return a.astype(MXU_DTYPE)


def _dot(a, b):
    return jnp.dot(_mx(a), _mx(b), preferred_element_type=F32)


def _dot_nt(a, b):
    return lax.dot_general(_mx(a), _mx(b), (((1,), (1,)), ((), ())), preferred_element_type=F32)


def _dot_tn(a, b):
    return lax.dot_general(_mx(a), _mx(b), (((0,), (0,)), ((), ())), preferred_element_type=F32)


def _sigmoid(x):
    return 1.0 / (1.0 + jnp.exp(-x))


def _gelu(x):
    return 0.5 * x * (1.0 + lax.erf(x * (1.0 / math.sqrt(2.0))))


def _gelu_grad(x):
    return 0.5 * (1.0 + lax.erf(x * (1.0 / math.sqrt(2.0)))) + x * jnp.exp(-0.5 * x * x) * (1.0 / math.sqrt(2.0 * math.pi))


def _rms_r(x):
    return lax.rsqrt(jnp.mean(x * x, axis=-1, keepdims=True) + EPS)


def _rms_bwd(dh, x, r, g):
    u = dh * g
    dx = r * u - x * (r * r * r) * jnp.mean(u * x, axis=-1, keepdims=True)
    return dx, dh * x * r


def _standardize(a):
    mu = jnp.mean(a, axis=-1, keepdims=True)
    d = a - mu
    r = lax.rsqrt(jnp.mean(d * d, axis=-1, keepdims=True) + EPS)
    return d * r, r


def _standardize_bwd(do, o, r):
    return r * (do - jnp.mean(do, axis=-1, keepdims=True) - o * jnp.mean(do * o, axis=-1, keepdims=True))


def _acc_out(ref, val, first):
    @pl.when(first)
    def _():
        ref[...] = val

    @pl.when(jnp.logical_not(first))
    def _():
        ref[...] += val


def _row_tile(t, pref):
    tm = min(t, pref)
    assert t % tm == 0, (t, tm)
    return tm


def _segments(part_widths, shard_w):
    bounds = {0}
    off = 0
    for w in part_widths:
        off += w
        bounds.add(off)
    total = off
    for j in range(1, total // shard_w + 1):
        bounds.add(j * shard_w)
    bounds = sorted(bounds)
    starts = np.cumsum([0] + list(part_widths))
    segs = []
    for a, b in zip(bounds[:-1], bounds[1:]):
        p = int(np.searchsorted(starts, a, side="right") - 1)
        segs.append((p, a - int(starts[p]), a // shard_w, a % shard_w, b - a))
    return segs


ANY = pl.BlockSpec(memory_space=pl.ANY)


def _mesh_pos():
    x, y, c = lax.axis_index("x"), lax.axis_index("y"), lax.axis_index("c")
    chips = [(1 - x, y), (x, 1 - y), (1 - x, 1 - y)]
    return x, y, c, 2 * x + y, chips, [2 * cx + cy for cx, cy in chips]


def _rider_copies(kind, i_refs, o_refs, send, recv, pos):
    x, y, c, me, chips, cj = pos
    out = []
    for b, (i_ref, o_ref) in enumerate(zip(i_refs, o_refs)):
        for k in range(1 if kind == "pairx" else 3):
            if kind == "ici":
                src, dst, land, dev = i_ref.at[me, c], o_ref.at[me, c], o_ref.at[cj[k], c], (*chips[k], c)
            elif kind == "d2d":
                src, dst, land, dev = i_ref.at[cj[k], c], o_ref.at[cj[k], c], o_ref.at[cj[k], 1 - c], (x, y, 1 - c)
            elif kind == "pairx":
                src, dst, land, dev = i_ref.at[:, 1 - c], o_ref, o_ref, (x, y, 1 - c)
            else:
                src, dst, land, dev = i_ref.at[cj[k]], o_ref.at[k], o_ref.at[k], (*chips[k], c)
            out.append(tuple(pltpu.make_async_remote_copy(
                src_ref=s_, dst_ref=d_, send_sem=send.at[b, k], recv_sem=recv.at[b, k],
                device_id=dev, device_id_type=MESH) for s_, d_ in ((src, dst), (land, land))))
    return out


def _rider_out_shape(kind, a):
    if kind == "scatter":
        return S((3,) + a.shape[1:], a.dtype)
    if kind == "pairx":
        return S(a.shape[:1] + a.shape[2:], a.dtype)
    return S(a.shape, a.dtype)


def _pcall(body, args, riders, *, grid, in_specs, out_specs, out_shape, name, sem, scratch_shapes=()):
    outs = list(out_shape)
    if not riders:
        res = pl.pallas_call(body, grid=grid, in_specs=in_specs, out_specs=out_specs, out_shape=outs, name=name,
                             scratch_shapes=list(scratch_shapes), compiler_params=_cp(*sem))(*args)
        return res, []
    r_in = [a for _, bufs in riders for a in bufs]
    r_out = [_rider_out_shape(kind, a) for kind, bufs in riders for a in bufs]
    n_in, n_out, n_scr, n_r = len(args), len(outs), len(scratch_shapes), len(r_in)
    aliases, idx = {}, 0
    for kind, bufs in riders:
        for _ in bufs:
            if kind in ("ici", "d2d"):
                aliases[n_in + idx] = n_out + idx
            idx += 1
    sems = [pltpu.SemaphoreType.DMA((len(bufs), 3)) for _, bufs in riders for _ in range(2)]

    def wrapped(*refs):
        a, ri = refs[:n_in], refs[n_in:n_in + n_r]
        o, ro = refs[n_in + n_r:n_in + n_r + n_out], refs[n_in + n_r + n_out:n_in + 2 * n_r + n_out]
        scr = refs[n_in + 2 * n_r + n_out:n_in + 2 * n_r + n_out + n_scr]
        sm = refs[n_in + 2 * n_r + n_out + n_scr:]
        pos = _mesh_pos()
        copies, off = [], 0
        for r, (kind, bufs) in enumerate(riders):
            copies += _rider_copies(kind, ri[off:off + len(bufs)], ro[off:off + len(bufs)], sm[2 * r], sm[2 * r + 1], pos)
            off += len(bufs)
        ids = [pl.program_id(d) for d in range(len(grid))]
        first = functools.reduce(jnp.logical_and, [i == 0 for i in ids])
        last = functools.reduce(jnp.logical_and, [i == n - 1 for i, n in zip(ids, grid)])

        @pl.when(first)
        def _():
            for cp, _ in copies:
                cp.start()

        body(*a, *o, *scr)

        @pl.when(last)
        def _():
            for cp, land in copies:
                land.wait_recv()
                cp.wait_send()

    res = pl.pallas_call(
        wrapped, grid=grid, in_specs=list(in_specs) + [ANY] * n_r, out_specs=list(out_specs) + [ANY] * n_r,
        out_shape=outs + r_out, input_output_aliases=aliases, name=name,
        scratch_shapes=list(scratch_shapes) + sems, compiler_params=_cp(*(("arbitrary",) * len(grid))))(*args, *r_in)
    return res[:n_out], res[n_out:]


def _wcol_spec(w):
    return pl.BlockSpec(w.shape, lambda *_: (0, 0, 0))


def _wrow_spec(w):
    return pl.BlockSpec(w.shape, lambda *_: (0, 0))


def _norm_mm(x, g, w, out_dtype, name, tm_pref, riders=(), rope=None):
    t = x.shape[0]
    nc = w.shape[2]
    tm = _row_tile(t, tm_pref)
    extra = list(rope) if rope else []

    def body(x_ref, g_ref, w_ref, *rest):
        o_ref = rest[-1]
        xv = x_ref[...]
        h = _mx(xv * _rms_r(xv) * g_ref[...])
        for j in range(N_CHIPS):
            o_ref[:, j * nc:(j + 1) * nc] = jnp.dot(h, w_ref[j], preferred_element_type=F32).astype(o_ref.dtype)
        if rope:
            _rotate_qk(o_ref, rest[0][...], rest[1][...])

    (out,), rid = _pcall(
        body, [x, g, w] + extra, riders, grid=(t // tm,),
        in_specs=[pl.BlockSpec((tm, D), lambda i: (i, 0)), pl.BlockSpec((1, D), lambda i: (0, 0)), _wcol_spec(w)]
        + [pl.BlockSpec((tm, HD), lambda i: (i, 0)) for _ in extra],
        out_specs=[pl.BlockSpec((tm, N_CHIPS * nc), lambda i: (i, 0))],
        out_shape=[S((t, N_CHIPS * nc), out_dtype)], name=name, sem=("parallel",))
    return out, rid


def _parts_mm_res(parts, w, res, name):
    t = res.shape[0]
    tm = _row_tile(t, 512)
    widths = [p.shape[1] for p in parts]
    offs = np.cumsum([0] + widths)
    n = len(parts)

    def body(*refs):
        p_refs, w_ref, r_ref, o_ref = refs[:n], refs[n], refs[n + 1], refs[n + 2]
        acc = r_ref[...]
        for p in range(n):
            acc = acc + _dot(p_refs[p][...], w_ref[int(offs[p]):int(offs[p + 1]), :])
        o_ref[...] = acc

    return pl.pallas_call(
        body, grid=(t // tm,),
        in_specs=[pl.BlockSpec((tm, wd), lambda i: (i, 0)) for wd in widths]
        + [_wrow_spec(w), pl.BlockSpec((tm, D), lambda i: (i, 0))],
        out_specs=pl.BlockSpec((tm, D), lambda i: (i, 0)),
        out_shape=S((t, D), F32), name=name, compiler_params=_cp("parallel"))(*parts, w, res)


def _swiglu(ff):
    gate = ff[:, :FFN_H].astype(F32)
    up = ff[:, FFN_H:].astype(F32)
    return gate * _sigmoid(gate) * up


def _swiglu_mm_res(ff, w, res, name, riders=()):
    t = res.shape[0]
    tm = _row_tile(t, 512)

    def body(f_ref, w_ref, r_ref, o_ref):
        o_ref[...] = r_ref[...] + _dot(_swiglu(f_ref[...]), w_ref[...])

    (out,), rid = _pcall(
        body, [ff, w, res], riders, grid=(t // tm,),
        in_specs=[pl.BlockSpec((tm, 2 * FFN_H), lambda i: (i, 0)), _wrow_spec(w),
                  pl.BlockSpec((tm, D), lambda i: (i, 0))],
        out_specs=[pl.BlockSpec((tm, D), lambda i: (i, 0))],
        out_shape=[S((t, D), F32)], name=name, sem=("parallel",))
    return out, rid


def _dx_norm(dparts, w, x, g, dres, name, tm_pref, riders=()):
    t = x.shape[0]
    nc = w.shape[2]
    tm = _row_tile(t, tm_pref)
    widths = [p.shape[1] for p in dparts]
    segs = _segments(widths, nc)
    n = len(dparts)

    def body(*refs):
        d_refs = refs[:n]
        w_ref, x_ref, g_ref, r_ref, dx_ref, dg_ref = refs[n:]
        dh = jnp.zeros((tm, D), F32)
        for (p, po, j, jo, wd) in segs:
            dh = dh + _dot_nt(d_refs[p][:, po:po + wd], w_ref[j, :, jo:jo + wd])
        xv = x_ref[...]
        dx, dgrow = _rms_bwd(dh, xv, _rms_r(xv), g_ref[...])
        dx_ref[...] = r_ref[...] + dx
        _acc_out(dg_ref, jnp.sum(dgrow, axis=0, keepdims=True), pl.program_id(0) == 0)

    (dx, dg), rid = _pcall(
        body, [*dparts, w, x, g, dres], riders, grid=(t // tm,),
        in_specs=[pl.BlockSpec((tm, wd), lambda i: (i, 0)) for wd in widths]
        + [_wcol_spec(w), pl.BlockSpec((tm, D), lambda i: (i, 0)),
           pl.BlockSpec((1, D), lambda i: (0, 0)), pl.BlockSpec((tm, D), lambda i: (i, 0))],
        out_specs=[pl.BlockSpec((tm, D), lambda i: (i, 0)), pl.BlockSpec((1, D), lambda i: (0, 0))],
        out_shape=[S((t, D), F32), S((1, D), F32)], name=name, sem=("arbitrary",))
    return dx, dg, rid


def _dx_parts(dy, w, widths, name, riders=()):
    t = dy.shape[0]
    tm = _row_tile(t, 512)
    offs = np.cumsum([0] + list(widths))
    n = len(widths)

    def body(dy_ref, w_ref, *o_refs):
        dyv = _mx(dy_ref[...])
        for p in range(n):
            o_refs[p][...] = _dot_nt(dyv, w_ref[int(offs[p]):int(offs[p + 1]), :])

    return _pcall(
        body, [dy, w], riders, grid=(t // tm,),
        in_specs=[pl.BlockSpec((tm, D), lambda i: (i, 0)), _wrow_spec(w)],
        out_specs=[pl.BlockSpec((tm, wd), lambda i: (i, 0)) for wd in widths],
        out_shape=[S((t, wd), F32) for wd in widths], name=name, sem=("parallel",))


def _dx_swiglu(dy, w, ff, name):
    t = dy.shape[0]
    tm = _row_tile(t, 512)

    def body(dy_ref, w_ref, f_ref, o_ref):
        dact = _dot_nt(dy_ref[...], w_ref[...])
        gate = f_ref[:, :FFN_H].astype(F32)
        up = f_ref[:, FFN_H:].astype(F32)
        s = _sigmoid(gate)
        o_ref[:, :FFN_H] = (dact * up * (s * (1.0 + gate * (1.0 - s)))).astype(o_ref.dtype)
        o_ref[:, FFN_H:] = (dact * (gate * s)).astype(o_ref.dtype)

    return pl.pallas_call(
        body, grid=(t // tm,),
        in_specs=[pl.BlockSpec((tm, D), lambda i: (i, 0)), _wrow_spec(w),
                  pl.BlockSpec((tm, 2 * FFN_H), lambda i: (i, 0))],
        out_specs=pl.BlockSpec((tm, 2 * FFN_H), lambda i: (i, 0)),
        out_shape=S((t, 2 * FFN_H), ACT_DTYPE), name=name, compiler_params=_cp("parallel"))(dy, w, ff)


def _call_into(body, into, in_specs, args, *, n_prefetch, grid, out_specs, **kw):
    n_in = len(args)
    if into is None:
        gs = pltpu.PrefetchScalarGridSpec(num_scalar_prefetch=n_prefetch, grid=grid, in_specs=in_specs,
                                          out_specs=out_specs)
        return pl.pallas_call(body, grid_spec=gs, **kw)(*args)

    def wrapped(*refs):
        return body(*refs[:n_in], *refs[n_in + 1:])

    gs = pltpu.PrefetchScalarGridSpec(num_scalar_prefetch=n_prefetch, grid=grid,
                                      in_specs=list(in_specs) + [ANY], out_specs=out_specs)
    return pl.pallas_call(wrapped, grid_spec=gs, input_output_aliases={n_in: 0}, **kw)(*args, into)


def _dw_norm_parts(x, g, dparts, nc, name):
    t = x.shape[0]
    tk = _row_tile(t, 1024)
    widths = [p.shape[1] for p in dparts]
    segs = _segments(widths, nc)
    n = len(dparts)
    nk = t // tk

    def body(*refs):
        x_ref, g_ref = refs[0], refs[1]
        d_refs = refs[2:2 + n]
        o_ref, acc_ref = refs[2 + n], refs[3 + n]
        k = pl.program_id(0)
        xv = x_ref[...]
        h = _mx(xv * _rms_r(xv) * g_ref[...])

        @pl.when(k == 0)
        def _():
            acc_ref[...] = jnp.zeros_like(acc_ref)

        for (p, po, j, jo, wd) in segs:
            acc_ref[j, :, jo:jo + wd] += _dot_tn(h, d_refs[p][:, po:po + wd])

        @pl.when(k == nk - 1)
        def _():
            o_ref[...] = acc_ref[...].astype(o_ref.dtype)

    return pl.pallas_call(
        body, grid=(nk,),
        in_specs=[pl.BlockSpec((tk, D), lambda k: (k, 0)), pl.BlockSpec((1, D), lambda k: (0, 0))]
        + [pl.BlockSpec((tk, wd), lambda k: (k, 0)) for wd in widths],
        out_specs=pl.BlockSpec((N_CHIPS, D, nc), lambda k: (0, 0, 0)),
        out_shape=S((N_CHIPS, D, nc), MXU_DTYPE), name=name,
        scratch_shapes=[pltpu.VMEM((N_CHIPS, D, nc), F32)], compiler_params=_cp("arbitrary"))(x, g, *dparts)


def _dw_norm_cols(x, g, dy, nc, name, riders=()):
    t = x.shape[0]
    tk = _row_tile(t, 1024)
    nk = t // tk

    def body(x_ref, g_ref, dy_ref, o_ref, acc_ref):
        k = pl.program_id(1)
        xv = x_ref[...]
        h = _mx(xv * _rms_r(xv) * g_ref[...])

        @pl.when(k == 0)
        def _():
            acc_ref[...] = jnp.zeros_like(acc_ref)

        acc_ref[...] += _dot_tn(h, dy_ref[...])

        @pl.when(k == nk - 1)
        def _():
            o_ref[...] = acc_ref[...].astype(o_ref.dtype)

    (out,), rid = _pcall(
        body, [x, g, dy], riders, grid=(N_CHIPS, nk),
        in_specs=[pl.BlockSpec((tk, D), lambda j, k: (k, 0)), pl.BlockSpec((1, D), lambda j, k: (0, 0)),
                  pl.BlockSpec((tk, nc), lambda j, k: (k, j))],
        out_specs=[pl.BlockSpec((None, D, nc), lambda j, k: (j, 0, 0))],
        out_shape=[S((N_CHIPS, D, nc), MXU_DTYPE)], name=name, sem=("parallel", "arbitrary"),
        scratch_shapes=[pltpu.VMEM((D, nc), F32)])
    return out, rid


def _dw_parts(parts, dy, name):
    t = dy.shape[0]
    tk = _row_tile(t, 1024)
    widths = [p.shape[1] for p in parts]
    offs = np.cumsum([0] + widths)
    ktot = int(offs[-1])
    n = len(parts)
    nk = t // tk

    def body(*refs):
        p_refs, dy_ref, o_ref, acc_ref = refs[:n], refs[n], refs[n + 1], refs[n + 2]
        k = pl.program_id(0)

        @pl.when(k == 0)
        def _():
            acc_ref[...] = jnp.zeros_like(acc_ref)

        dyv = _mx(dy_ref[...])
        for p in range(n):
            acc_ref[int(offs[p]):int(offs[p + 1]), :] += _dot_tn(p_refs[p][...], dyv)

        @pl.when(k == nk - 1)
        def _():
            o_ref[...] = acc_ref[...].astype(o_ref.dtype)

    return pl.pallas_call(
        body, grid=(nk,),
        in_specs=[pl.BlockSpec((tk, wd), lambda k: (k, 0)) for wd in widths]
        + [pl.BlockSpec((tk, D), lambda k: (k, 0))],
        out_specs=pl.BlockSpec((ktot, D), lambda k: (0, 0)),
        out_shape=S((ktot, D), MXU_DTYPE), name=name,
        scratch_shapes=[pltpu.VMEM((ktot, D), F32)], compiler_params=_cp("arbitrary"))(*parts, dy)


def _dw_swiglu(ff, dy, name):
    t = dy.shape[0]
    tk = _row_tile(t, 512)
    nk = t // tk

    def body(f_ref, dy_ref, o_ref, acc_ref):
        k = pl.program_id(0)

        @pl.when(k == 0)
        def _():
            acc_ref[...] = jnp.zeros_like(acc_ref)

        acc_ref[...] += _dot_tn(_swiglu(f_ref[...]), dy_ref[...])

        @pl.when(k == nk - 1)
        def _():
            o_ref[...] = acc_ref[...].astype(o_ref.dtype)

    return pl.pallas_call(
        body, grid=(nk,),
        in_specs=[pl.BlockSpec((tk, 2 * FFN_H), lambda k: (k, 0)), pl.BlockSpec((tk, D), lambda k: (k, 0))],
        out_specs=pl.BlockSpec((FFN_H, D), lambda k: (0, 0)),
        out_shape=S((FFN_H, D), MXU_DTYPE), name=name,
        scratch_shapes=[pltpu.VMEM((FFN_H, D), F32)], compiler_params=_cp("arbitrary"))(ff, dy)


def _tables(t):
    pos = jnp.arange(t, dtype=F32)
    half = HD // 2
    inv_freq = ROPE_BASE ** (-jnp.arange(half, dtype=F32) / half)
    ang = pos[:, None] * inv_freq[None, :]
    cos, sin = jnp.cos(ang), jnp.sin(ang)
    tb = {"cos2": jnp.concatenate([cos, cos], axis=1), "sin2": jnp.concatenate([-sin, sin], axis=1)}
    gf = 1.0 - jnp.exp2(-5.0 - jnp.arange(HEADS, dtype=F32))
    lgf = jnp.log(gf)[:, None]
    lgb = jnp.log(gf[::-1])[:, None]
    idx = jnp.arange(CH, dtype=F32)
    diff = idx[:, None] - idx[None, :]
    dfwd = jnp.where(diff >= 0, jnp.exp(lgf[:, :, None] * jnp.where(diff >= 0, diff, 0.0)), 0.0)
    dbwd = jnp.where(diff < 0, jnp.exp(lgb[:, :, None] * jnp.where(diff < 0, -diff, 0.0)), 0.0)
    tb["dm"] = dfwd + dbwd
    tb["dmt"] = jnp.swapaxes(tb["dm"], 1, 2)

    def lanes(a):
        return jnp.repeat(a.T, HD, axis=1)

    tb["xif"] = lanes(jnp.exp(lgf * (idx + 1)))
    tb["zf"] = lanes(jnp.exp(lgf * (CH - 1 - idx)))
    tb["xib"] = lanes(jnp.exp(lgb * (CH - idx)))
    tb["zb"] = lanes(jnp.exp(lgb * idx))
    tb["gcf"] = jnp.repeat(jnp.exp(lgf * CH), HD, axis=0).reshape(1, HEADS * HD)
    tb["gcb"] = jnp.repeat(jnp.exp(lgb * CH), HD, axis=0).reshape(1, HEADS * HD)
    return tb


def _full(shape):
    nd = len(shape)
    return pl.BlockSpec(shape, lambda *_: (0,) * nd)


def _gm_mixed(vn, ws_ref, bias):
    lane = lax.broadcasted_iota(jnp.int32, (CH, 128), 1)
    halves = []
    for hf in range(2):
        vh = _mx(vn[:, hf * 128:(hf + 1) * 128])
        r0 = jnp.dot(_mx(ws_ref[2 * hf]), vh, preferred_element_type=F32)
        r1 = jnp.dot(_mx(ws_ref[2 * hf + 1]), vh, preferred_element_type=F32)
        halves.append(jnp.where(lane < 64, r0, r1))
    return jnp.concatenate(halves, axis=1) + bias


def _gm_fwd(proj, ln_g, ln_b, ws, bias, name, riders=()):
    t = proj.shape[0]
    tm = _row_tile(t, 512)

    def body(pu_ref, pv_ref, g_ref, b_ref, ws_ref, bias_ref, o_ref):
        for c in range(tm // CH):
            rows = slice(c * CH, (c + 1) * CH)
            u = _gelu(pu_ref[rows, :])
            o, _ = _standardize(_gelu(pv_ref[rows, :]))
            vn = o * g_ref[...] + b_ref[...]
            o_ref[rows, :] = (u * _gm_mixed(vn, ws_ref, bias_ref[...])).astype(o_ref.dtype)

    (out,), rid = _pcall(
        body, [proj, proj, ln_g, ln_b, ws, bias], riders, grid=(t // tm,),
        in_specs=[pl.BlockSpec((tm, GM_W), lambda i: (i, 0)), pl.BlockSpec((tm, GM_W), lambda i: (i, 1)),
                  _full((1, GM_W)), _full((1, GM_W)), _full((GM_HEADS, CH, CH)), _full((CH, GM_W))],
        out_specs=[pl.BlockSpec((tm, GM_W), lambda i: (i, 0))],
        out_shape=[S((t, GM_W), ACT_DTYPE)], name=name, sem=("parallel",))
    return out, rid


def _gm_bwd(proj, dy, ln_g, ln_b, ws, wst, bias, name):
    t = proj.shape[0]
    tm = _row_tile(t, 512)
    nb = t // tm

    def body(pu_ref, pv_ref, dy_ref, g_ref, b_ref, ws_ref, wst_ref, bias_ref,
             d_ref, dws_ref, dbs_ref, dg_ref, db_ref, dbias_ref):
        first = pl.program_id(0) == 0
        lane = lax.broadcasted_iota(jnp.int32, (CH, 128), 1)
        dws = [jnp.zeros((CH, CH), F32) for _ in range(GM_HEADS)]
        dbias = jnp.zeros((CH, GM_W), F32)
        dg = jnp.zeros((1, GM_W), F32)
        db = jnp.zeros((1, GM_W), F32)
        for c in range(tm // CH):
            rows = slice(c * CH, (c + 1) * CH)
            pu = pu_ref[rows, :]
            pv = pv_ref[rows, :]
            u = _gelu(pu)
            o, r = _standardize(_gelu(pv))
            vn = o * g_ref[...] + b_ref[...]
            mixed = _gm_mixed(vn, ws_ref, bias_ref[...])
            dyv = dy_ref[rows, :]
            d_ref[rows, :GM_W] = (dyv * mixed * _gelu_grad(pu)).astype(d_ref.dtype)
            dmixed = dyv * u
            dbias = dbias + dmixed
            dvn_halves = []
            for hf in range(2):
                dm = dmixed[:, hf * 128:(hf + 1) * 128]
                vh = vn[:, hf * 128:(hf + 1) * 128]
                dm0 = jnp.where(lane < 64, dm, 0.0)
                dm1 = dm - dm0
                dws[2 * hf] = dws[2 * hf] + _dot_nt(dm0, vh)
                dws[2 * hf + 1] = dws[2 * hf + 1] + _dot_nt(dm1, vh)
                t0 = _dot(wst_ref[2 * hf], dm)
                t1 = _dot(wst_ref[2 * hf + 1], dm)
                dvn_halves.append(jnp.where(lane < 64, t0, t1))
            dvn = jnp.concatenate(dvn_halves, axis=1)
            dg = dg + jnp.sum(dvn * o, axis=0, keepdims=True)
            db = db + jnp.sum(dvn, axis=0, keepdims=True)
            dv = _standardize_bwd(dvn * g_ref[...], o, r)
            d_ref[rows, GM_W:] = (dv * _gelu_grad(pv)).astype(d_ref.dtype)
        for h in range(GM_HEADS):
            _acc_out(dws_ref.at[h], dws[h], first)
        _acc_out(dbias_ref, dbias, first)
        _acc_out(dg_ref, dg, first)
        _acc_out(db_ref, db, first)

        @pl.when(pl.program_id(0) == nb - 1)
        def _():
            tot = dbias_ref[...]
            head = lax.broadcasted_iota(jnp.int32, (CH, GM_W), 1) // (GM_W // GM_HEADS)
            out = jnp.zeros((CH, 128), F32)
            for h in range(GM_HEADS):
                s = jnp.sum(jnp.where(head == h, tot, 0.0), axis=1, keepdims=True)
                out = jnp.where(lane == h, s, out)
            dbs_ref[...] = out

    return pl.pallas_call(
        body, grid=(nb,),
        in_specs=[pl.BlockSpec((tm, GM_W), lambda i: (i, 0)), pl.BlockSpec((tm, GM_W), lambda i: (i, 1)),
                  pl.BlockSpec((tm, GM_W), lambda i: (i, 0)),
                  _full((1, GM_W)), _full((1, GM_W)), _full((GM_HEADS, CH, CH)), _full((GM_HEADS, CH, CH)),
                  _full((CH, GM_W))],
        out_specs=[pl.BlockSpec((tm, 2 * GM_W), lambda i: (i, 0)), _full((GM_HEADS, CH, CH)), _full((CH, 128)),
                   _full((1, GM_W)), _full((1, GM_W))],
        out_shape=[S((t, 2 * GM_W), ACT_DTYPE), S((GM_HEADS, CH, CH), F32), S((CH, 128), F32),
                   S((1, GM_W), F32), S((1, GM_W), F32)],
        scratch_shapes=[pltpu.VMEM((CH, GM_W), F32)],
        name=name, compiler_params=_cp("arbitrary"))(proj, proj, dy, ln_g, ln_b, ws, wst, bias)


def _rot(x, cos2, sin2):
    return x * cos2 + pltpu.roll(x, HD // 2, 1) * sin2


def _rot_bwd(dx, cos2, sin2):
    return dx * cos2 + pltpu.roll(dx * sin2, HD // 2, 1)


Q_COL, K_COL, V_COL, GATE_COL = 1, 2, 3, 4


def _rotate_qk(o_ref, cos2, sin2):
    for col, scale in ((Q_COL, 1.0), (K_COL, HD ** -0.5)):
        for h in range(HEADS):
            cols = slice(col * RET_W + h * HD, col * RET_W + (h + 1) * HD)
            o_ref[:, cols] = _rot(o_ref[:, cols], cos2, sin2) * scale


def _ret_scan(lhs, lhs_col, rhs, rhs_col, lp, ls, gp, gs, name):
    t = lhs.shape[0]
    n = t // CH
    r = 4 if n % 4 == 0 else 1
    ns = n // r

    def body(lp_ref, ls_ref, gp_ref, gs_ref, l1_ref, r1_ref, l2_ref, r2_ref, pre_ref, suf_ref, sp_ref, ss_ref):
        @pl.when(pl.program_id(0) == 0)
        def _():
            sp_ref[...] = jnp.zeros_like(sp_ref)
            ss_ref[...] = jnp.zeros_like(ss_ref)

        def kv(l_ref, r_ref, scale, rows):
            lv = l_ref[rows, :] * scale
            rv = r_ref[rows, :]
            return jnp.concatenate([_dot_tn(lv[:, h * HD:(h + 1) * HD], rv[:, h * HD:(h + 1) * HD])
                                    for h in range(HEADS)], axis=1)

        for j in range(r):
            pre_ref[j] = sp_ref[...]
            sp_ref[...] = sp_ref[...] * gp_ref[...] + kv(l1_ref, r1_ref, lp_ref[...], slice(j * CH, (j + 1) * CH))
        for j in reversed(range(r)):
            suf_ref[j] = ss_ref[...]
            ss_ref[...] = ss_ref[...] * gs_ref[...] + kv(l2_ref, r2_ref, ls_ref[...], slice(j * CH, (j + 1) * CH))

    w = HEADS * HD
    return pl.pallas_call(
        body, grid=(ns,),
        in_specs=[_full((CH, w)), _full((CH, w)), _full((1, w)), _full((1, w)),
                  pl.BlockSpec((r * CH, w), lambda s: (s, lhs_col)), pl.BlockSpec((r * CH, w), lambda s: (s, rhs_col)),
                  pl.BlockSpec((r * CH, w), lambda s: (ns - 1 - s, lhs_col)),
                  pl.BlockSpec((r * CH, w), lambda s: (ns - 1 - s, rhs_col))],
        out_specs=[pl.BlockSpec((r, HD, w), lambda s: (s, 0, 0)), pl.BlockSpec((r, HD, w), lambda s: (ns - 1 - s, 0, 0))],
        out_shape=[S((n, HD, w), F32)] * 2, name=name,
        scratch_shapes=[pltpu.VMEM((HD, w), F32), pltpu.VMEM((HD, w), F32)],
        compiler_params=_cp("arbitrary"))(lp, ls, gp, gs, lhs, rhs, lhs, rhs)


def _ret_out(proj, sf, sb, tb, name, riders=()):
    t = proj.shape[0]
    r = 2 if (t // CH) % 2 == 0 else 1
    tm = r * CH
    w = HEADS * HD

    def body(rq_ref, rk_ref, v_ref, g_ref, sf_ref, sb_ref, dm_ref, xif_ref, xib_ref, a_ref, y_ref):
        for c in range(r):
            rows = slice(c * CH, (c + 1) * CH)
            for h in range(HEADS):
                cols = slice(h * HD, (h + 1) * HD)
                q = rq_ref[rows, cols]
                p = _dot_nt(q, rk_ref[rows, cols]) * dm_ref[h]
                a = (_dot(p, v_ref[rows, cols]) + _dot(q * xif_ref[:, cols], sf_ref[c, :, cols])
                     + _dot(q * xib_ref[:, cols], sb_ref[c, :, cols]))
                a_ref[rows, cols] = a
                o, _ = _standardize(a)
                gv = g_ref[rows, cols]
                y_ref[rows, cols] = (o * (gv * _sigmoid(gv))).astype(y_ref.dtype)

    (a, y), rid = _pcall(
        body, [proj, proj, proj, proj, sf, sb, tb["dm"], tb["xif"], tb["xib"]], riders, grid=(t // tm,),
        in_specs=[pl.BlockSpec((tm, w), lambda i: (i, Q_COL)), pl.BlockSpec((tm, w), lambda i: (i, K_COL)),
                  pl.BlockSpec((tm, w), lambda i: (i, V_COL)), pl.BlockSpec((tm, w), lambda i: (i, GATE_COL)),
                  pl.BlockSpec((r, HD, w), lambda i: (i, 0, 0)), pl.BlockSpec((r, HD, w), lambda i: (i, 0, 0)),
                  _full((HEADS, CH, CH)), _full((CH, w)), _full((CH, w))],
        out_specs=[pl.BlockSpec((tm, w), lambda i: (i, 0))] * 2,
        out_shape=[S((t, w), F32), S((t, w), ACT_DTYPE)], name=name, sem=("parallel",))
    return a, y, rid


def _ret_bwd_pre(dy, a, proj, name):
    t = dy.shape[0]
    tm = _row_tile(t, 512)
    w = HEADS * HD

    def body(dy_ref, a_ref, g_ref, da_ref, dg_ref):
        for h in range(HEADS):
            cols = slice(h * HD, (h + 1) * HD)
            o, r = _standardize(a_ref[:, cols])
            gv = g_ref[:, cols]
            s = _sigmoid(gv)
            dyv = dy_ref[:, cols]
            dg_ref[:, cols] = (dyv * o * (s * (1.0 + gv * (1.0 - s)))).astype(dg_ref.dtype)
            da_ref[:, cols] = _standardize_bwd(dyv * (gv * s), o, r).astype(da_ref.dtype)

    return pl.pallas_call(
        body, grid=(t // tm,),
        in_specs=[pl.BlockSpec((tm, w), lambda i: (i, 0)), pl.BlockSpec((tm, w), lambda i: (i, 0)),
                  pl.BlockSpec((tm, w), lambda i: (i, GATE_COL))],
        out_specs=[pl.BlockSpec((tm, w), lambda i: (i, 0))] * 2,
        out_shape=[S((t, w), ACT_DTYPE)] * 2, name=name, compiler_params=_cp("parallel"))(dy, a, proj)


def _ret_bwd_main(proj, da, sf, sb, gf, gb, tb, name, riders=()):
    t = proj.shape[0]
    r = 2 if (t // CH) % 2 == 0 else 1
    tm = r * CH
    w = HEADS * HD
    scale = HD ** -0.5

    def body(rq_ref, rk_ref, v_ref, da_ref, sf_ref, sb_ref, gf_ref, gb_ref, dm_ref, dmt_ref,
             xif_ref, xib_ref, zf_ref, zb_ref, c_ref, s_ref, o_ref):
        for c in range(r):
            rows = slice(c * CH, (c + 1) * CH)
            cos2, sin2 = c_ref[rows, :], s_ref[rows, :]
            for h in range(HEADS):
                cols = slice(h * HD, (h + 1) * HD)
                q, k, v, dav = rq_ref[rows, cols], rk_ref[rows, cols], v_ref[rows, cols], da_ref[rows, cols]
                qm, km, vm, dam = _mx(q), _mx(k), _mx(v), _mx(dav)
                dm, dmt = dm_ref[h], dmt_ref[h]
                pt = _dot_nt(km, qm) * dmt
                dp = _dot_nt(dam, vm) * dm
                dpt = _dot_nt(vm, dam) * dmt
                sfh, sbh, gfh, gbh = sf_ref[c, :, cols], sb_ref[c, :, cols], gf_ref[c, :, cols], gb_ref[c, :, cols]
                zf, zb = zf_ref[:, cols], zb_ref[:, cols]
                dv = _dot(pt, dam) + zf * _dot(km, gfh) + zb * _dot(km, gbh)
                drq = _dot(dp, km) + xif_ref[:, cols] * _dot_nt(dam, sfh) + xib_ref[:, cols] * _dot_nt(dam, sbh)
                drk = _dot(dpt, qm) + _dot_nt(zf * v, gfh) + _dot_nt(zb * v, gbh)
                o_ref[rows, h * HD:(h + 1) * HD] = _rot_bwd(drq, cos2, sin2).astype(o_ref.dtype)
                o_ref[rows, w + h * HD:w + (h + 1) * HD] = (_rot_bwd(drk, cos2, sin2) * scale).astype(o_ref.dtype)
                o_ref[rows, 2 * w + h * HD:2 * w + (h + 1) * HD] = dv.astype(o_ref.dtype)

    st = pl.BlockSpec((r, HD, w), lambda i: (i, 0, 0))
    (out,), rid = _pcall(
        body, [proj, proj, proj, da, sf, sb, gf, gb, tb["dm"], tb["dmt"], tb["xif"], tb["xib"], tb["zf"], tb["zb"],
               tb["cos2"], tb["sin2"]], riders, grid=(t // tm,),
        in_specs=[pl.BlockSpec((tm, w), lambda i: (i, Q_COL)), pl.BlockSpec((tm, w), lambda i: (i, K_COL)),
                  pl.BlockSpec((tm, w), lambda i: (i, V_COL)), pl.BlockSpec((tm, w), lambda i: (i, 0)),
                  st, st, st, st, _full((HEADS, CH, CH)), _full((HEADS, CH, CH)),
                  _full((CH, w)), _full((CH, w)), _full((CH, w)), _full((CH, w)),
                  pl.BlockSpec((tm, HD), lambda i: (i, 0)), pl.BlockSpec((tm, HD), lambda i: (i, 0))],
        out_specs=[pl.BlockSpec((tm, 3 * w), lambda i: (i, 0))],
        out_shape=[S((t, 3 * w), ACT_DTYPE)], name=name, sem=("parallel",))
    return out, rid


CONV_TM = 256
CONV_SUB = 64
A_COL = (2 * GM_W + 4 * RET_W) // CV_W
G_COL = A_COL + 1


def _halo_specs(t, tm, col):
    nb16 = t // HALO
    per = tm // HALO
    return [pl.BlockSpec((tm, CV_W), lambda i: (i, col)),
            pl.BlockSpec((HALO, CV_W), lambda i: (jnp.maximum(i * per - 1, 0), col)),
            pl.BlockSpec((HALO, CV_W), lambda i: (jnp.minimum((i + 1) * per, nb16 - 1), col))]


def _fill_padded(dst_ref, prev, main, nxt, tm, i, nb):
    dst_ref[0:HALO, :] = jnp.where(i > 0, prev, 0.0)
    dst_ref[HALO:HALO + tm, :] = main
    dst_ref[HALO + tm:2 * HALO + tm, :] = jnp.where(i < nb - 1, nxt, 0.0)


SUBLANES = 8


def _fill_shifted(sh_ref, src_ref, tm):
    n = tm + 2 * HALO - SUBLANES
    for b in range(SUBLANES):
        sh_ref[b, 0:n, :] = src_ref[pl.ds(b, n), :]


def _tap(sh_ref, off, rows):
    return sh_ref[off % SUBLANES, pl.ds(off - off % SUBLANES, rows), :]


def _conv_fwd(proj, cw, cb, ln_g, ln_b, name, riders=()):
    t = proj.shape[0]
    tm = _row_tile(t, CONV_TM)
    nb = t // tm

    def body(a_ref, ap_ref, an_ref, g_ref, gp_ref, gn_ref, w_ref, b_ref, lg_ref, lb_ref, y_ref, hc_ref,
             hp_ref, sh_ref):
        i = pl.program_id(0)
        _fill_padded(hp_ref, ap_ref[...] * _sigmoid(gp_ref[...]), a_ref[...] * _sigmoid(g_ref[...]),
                     an_ref[...] * _sigmoid(gn_ref[...]), tm, i, nb)
        _fill_shifted(sh_ref, hp_ref, tm)
        for sb in range(tm // CONV_SUB):
            acc = jnp.zeros((CONV_SUB, CV_W), F32) + b_ref[...]
            for k in range(KCONV):
                acc = acc + w_ref[k:k + 1, :] * _tap(sh_ref, sb * CONV_SUB + k + 1, CONV_SUB)
            rows = slice(sb * CONV_SUB, (sb + 1) * CONV_SUB)
            hc_ref[rows, :] = acc
            o, _ = _standardize(acc)
            z = o * lg_ref[...] + lb_ref[...]
            y_ref[rows, :] = (z * _sigmoid(z)).astype(y_ref.dtype)

    (y, hc), rid = _pcall(
        body, [proj, proj, proj, proj, proj, proj, cw, cb, ln_g, ln_b], riders, grid=(nb,),
        in_specs=_halo_specs(t, tm, A_COL) + _halo_specs(t, tm, G_COL)
        + [_full((32, CV_W)), _full((1, CV_W)), _full((1, CV_W)), _full((1, CV_W))],
        out_specs=[pl.BlockSpec((tm, CV_W), lambda i: (i, 0))] * 2,
        out_shape=[S((t, CV_W), ACT_DTYPE), S((t, CV_W), F32)], name=name, sem=("parallel",),
        scratch_shapes=[pltpu.VMEM((tm + 2 * HALO, CV_W), F32), pltpu.VMEM((SUBLANES, tm + 2 * HALO, CV_W), F32)])
    return y, hc, rid


def _conv_bwd(proj, dy, hc, cw, ln_g, ln_b, name, riders=()):
    t = proj.shape[0]
    tm = _row_tile(t, CONV_TM)
    nb = t // tm

    def body(a_ref, ap_ref, an_ref, g_ref, gp_ref, gn_ref, dy_ref, dyp_ref, dyn_ref, hc_ref, hcp_ref, hcn_ref,
             w_ref, lg_ref, lb_ref, d_ref, dw_ref, dcb_ref, dlg_ref, dlb_ref, hp_ref, dhp_ref, dwacc_ref,
             sh_ref, dsh_ref):
        i = pl.program_id(0)
        first = i == 0

        def dhc_of(dyv, hcv):
            o, r = _standardize(hcv)
            z = o * lg_ref[...] + lb_ref[...]
            s = _sigmoid(z)
            dz = dyv * (s * (1.0 + z * (1.0 - s)))
            return _standardize_bwd(dz * lg_ref[...], o, r), dz, o

        dhc, dz, o = dhc_of(dy_ref[...], hc_ref[...])
        _acc_out(dlg_ref, jnp.sum(dz * o, axis=0, keepdims=True), first)
        _acc_out(dlb_ref, jnp.sum(dz, axis=0, keepdims=True), first)
        _acc_out(dcb_ref, jnp.sum(dhc, axis=0, keepdims=True), first)
        _fill_padded(dhp_ref, dhc_of(dyp_ref[...], hcp_ref[...])[0], dhc, dhc_of(dyn_ref[...], hcn_ref[...])[0],
                     tm, i, nb)
        _fill_padded(hp_ref, ap_ref[...] * _sigmoid(gp_ref[...]), a_ref[...] * _sigmoid(g_ref[...]),
                     an_ref[...] * _sigmoid(gn_ref[...]), tm, i, nb)

        _fill_shifted(sh_ref, hp_ref, tm)
        _fill_shifted(dsh_ref, dhp_ref, tm)

        @pl.when(first)
        def _():
            dwacc_ref[...] = jnp.zeros_like(dwacc_ref)

        for sb in range(tm // CONV_SUB):
            base = sb * CONV_SUB
            dmain = dhp_ref[pl.ds(HALO + base, CONV_SUB), :]
            dh = jnp.zeros((CONV_SUB, CV_W), F32)
            for k in range(KCONV):
                dh = dh + w_ref[k:k + 1, :] * _tap(dsh_ref, base + 2 * HALO - 1 - k, CONV_SUB)
                prod = dmain * _tap(sh_ref, base + k + 1, CONV_SUB)
                dwacc_ref[k * 8:(k + 1) * 8, :] += jnp.sum(prod.reshape(CONV_SUB // 8, 8, CV_W), axis=0)
            rows = slice(base, base + CONV_SUB)
            s = _sigmoid(g_ref[rows, :])
            d_ref[rows, :CV_W] = (dh * s).astype(d_ref.dtype)
            d_ref[rows, CV_W:] = (dh * a_ref[rows, :] * (s * (1.0 - s))).astype(d_ref.dtype)

        @pl.when(i == nb - 1)
        def _():
            for k in range(KCONV):
                dw_ref[k:k + 1, :] = jnp.sum(dwacc_ref[k * 8:(k + 1) * 8, :], axis=0, keepdims=True)
            dw_ref[KCONV:32, :] = jnp.zeros((32 - KCONV, CV_W), F32)

    hs = [pl.BlockSpec((tm, CV_W), lambda i: (i, 0)),
          pl.BlockSpec((HALO, CV_W), lambda i: (jnp.maximum(i * (tm // HALO) - 1, 0), 0)),
          pl.BlockSpec((HALO, CV_W), lambda i: (jnp.minimum((i + 1) * (tm // HALO), t // HALO - 1), 0))]
    outs, rid = _pcall(
        body, [proj, proj, proj, proj, proj, proj, dy, dy, dy, hc, hc, hc, cw, ln_g, ln_b], riders, grid=(nb,),
        in_specs=_halo_specs(t, tm, A_COL) + _halo_specs(t, tm, G_COL) + hs + hs
        + [_full((32, CV_W)), _full((1, CV_W)), _full((1, CV_W))],
        out_specs=[pl.BlockSpec((tm, 2 * CV_W), lambda i: (i, 0)), _full((32, CV_W)), _full((1, CV_W)),
                   _full((1, CV_W)), _full((1, CV_W))],
        out_shape=[S((t, 2 * CV_W), ACT_DTYPE), S((32, CV_W), F32), S((1, CV_W), F32), S((1, CV_W), F32),
                   S((1, CV_W), F32)],
        name=name, sem=("arbitrary",),
        scratch_shapes=[pltpu.VMEM((tm + 2 * HALO, CV_W), F32), pltpu.VMEM((tm + 2 * HALO, CV_W), F32),
                        pltpu.VMEM((32 * 8, CV_W), F32), pltpu.VMEM((SUBLANES, tm + 2 * HALO, CV_W), F32),
                        pltpu.VMEM((SUBLANES, tm + 2 * HALO, CV_W), F32)])
    return (*outs, rid)


def _loss_head(x, g, target, name):
    t = x.shape[0]
    tm = _row_tile(t, 512)

    def body(x_ref, g_ref, t_ref, dx_ref, dg_ref, l_ref):
        first = pl.program_id(0) == 0
        xv = x_ref[...]
        r = _rms_r(xv)
        e = xv * r * g_ref[...] - t_ref[...]
        dx, dgrow = _rms_bwd(e * (1.0 / D), xv, r, g_ref[...])
        dx_ref[...] = dx
        _acc_out(dg_ref, jnp.sum(dgrow, axis=0, keepdims=True), first)
        part = 0.5 * jnp.sum(jnp.mean(e * e, axis=-1, keepdims=True), axis=0, keepdims=True)
        _acc_out(l_ref, jnp.broadcast_to(part, (8, 128)), first)

    return pl.pallas_call(
        body, grid=(t // tm,),
        in_specs=[pl.BlockSpec((tm, D), lambda i: (i, 0)), _full((1, D)), pl.BlockSpec((tm, D), lambda i: (i, 0))],
        out_specs=[pl.BlockSpec((tm, D), lambda i: (i, 0)), _full((1, D)), _full((8, 128))],
        out_shape=[S((t, D), F32), S((1, D), F32), S((8, 128), F32)], name=name,
        compiler_params=_cp("arbitrary"))(x, g, target)


def _as2d(a):
    return a.reshape(-1, a.shape[-1])


def _ew_tile(rows, cols, n_arrays):
    budget = VMEM_LIMIT // 2
    tr = rows
    while tr * cols * 4 * n_arrays * 2 > budget and tr % 16 == 0:
        tr //= 2
    assert rows % tr == 0
    return tr


def _adamw(w, g, m, v, name):
    shape = w.shape
    w2, g2, m2, v2 = _as2d(w), _as2d(g), _as2d(m), _as2d(v)
    rows, cols = w2.shape
    tr = _ew_tile(rows, cols, 7)

    def body(w_ref, g_ref, m_ref, v_ref, d_ref, nm_ref, nv_ref):
        gv = g_ref[...]
        nm = ADAM_B1 * m_ref[...] + (1.0 - ADAM_B1) * gv
        nv = ADAM_B2 * v_ref[...] + (1.0 - ADAM_B2) * (gv * gv)
        m_hat = nm / (1.0 - ADAM_B1 ** ADAM_STEP)
        v_hat = nv / (1.0 - ADAM_B2 ** ADAM_STEP)
        d_ref[...] = -ADAM_LR * (m_hat / (jnp.sqrt(v_hat) + ADAM_EPS) + ADAM_WD * w_ref[...])
        nm_ref[...] = nm
        nv_ref[...] = nv

    spec = pl.BlockSpec((tr, cols), lambda i: (i, 0))
    outs = pl.pallas_call(body, grid=(rows // tr,), in_specs=[spec] * 4, out_specs=[spec] * 3,
                          out_shape=[S((rows, cols), F32)] * 3, name=name,
                          compiler_params=_cp("parallel"))(w2, g2, m2, v2)
    return tuple(o.reshape(shape) for o in outs)


BIG = (("w_in", "col"), ("w_out", "row"), ("w_ffn_in", "col"), ("w_ffn_out", "row"))
NBIG = len(BIG)


def _cast_to_gathered(w, l, me, name):
    _, r_, c_ = w.shape
    tr = _ew_tile(r_, c_, 2)

    def body(me_ref, w_ref, o_ref):
        o_ref[...] = w_ref[...].astype(o_ref.dtype)

    gs = pltpu.PrefetchScalarGridSpec(
        num_scalar_prefetch=1, grid=(r_ // tr,),
        in_specs=[pl.BlockSpec((None, tr, c_), lambda i, s: (l, i, 0))],
        out_specs=pl.BlockSpec((None, tr, c_), lambda i, s: (s[0], i, 0)))
    out = pl.pallas_call(body, grid_spec=gs, out_shape=S((N_CHIPS, r_, c_), MXU_DTYPE), name=name,
                         compiler_params=_cp("parallel"))(me.reshape(1), w)
    return out.reshape(N_CHIPS, 2, r_ // 2, c_)


def _all_gather(bufs, name, per_core=False):
    n = len(bufs)

    def body(*refs):
        i_refs, o_refs = refs[:n], refs[n:2 * n]
        isend, irecv, dsend, drecv, osend, orecv = refs[2 * n:]
        pos = _mesh_pos()
        x, y, c, me, _, _ = pos
        ici = _rider_copies("ici", i_refs, o_refs, isend, irecv, pos)
        d2d = _rider_copies("d2d", o_refs, o_refs, dsend, drecv, pos)
        own = []
        if per_core:
            for b in range(n):
                own.append(tuple(pltpu.make_async_remote_copy(
                    src_ref=s_, dst_ref=d_, send_sem=osend.at[b], recv_sem=orecv.at[b],
                    device_id=(x, y, 1 - c), device_id_type=MESH)
                    for s_, d_ in ((i_refs[b].at[me, c], o_refs[b].at[me, c]),
                                   (o_refs[b].at[me, 1 - c], o_refs[b].at[me, 1 - c]))))
        for cp, _ in ici + own:
            cp.start()
        for (_, land), (fwd, _) in zip(ici, d2d):
            land.wait_recv()
            fwd.start()
        for _, land in d2d + own:
            land.wait_recv()
        for cp, _ in ici + d2d + own:
            cp.wait_send()

    return pl.pallas_call(
        body, in_specs=[ANY] * n, out_specs=[ANY] * n, out_shape=[S(a.shape, a.dtype) for a in bufs],
        input_output_aliases={w: w for w in range(n)}, name=name,
        scratch_shapes=[pltpu.SemaphoreType.DMA((n, 3))] * 4 + [pltpu.SemaphoreType.DMA((n,))] * 2)(*bufs)


def _pair_exchange(grads, name):
    n = len(grads)

    def body(*refs):
        g_refs, theirs = refs[:n], refs[n:2 * n]
        send, recv = refs[2 * n:]
        x, y, c, *_ = _mesh_pos()
        cps = []
        for w in range(n):
            cp = pltpu.make_async_remote_copy(
                src_ref=g_refs[w].at[:, 1 - c], dst_ref=theirs[w], send_sem=send.at[w], recv_sem=recv.at[w],
                device_id=(x, y, 1 - c), device_id_type=MESH)
            cp.start()
            cps.append(cp)
        for cp in cps:
            cp.wait()

    return pl.pallas_call(
        body, in_specs=[ANY] * n, out_specs=[ANY] * n,
        out_shape=[S(a.shape[:1] + a.shape[2:], a.dtype) for a in grads], name=name,
        scratch_shapes=[pltpu.SemaphoreType.DMA((n,))] * 2)(*grads)


def _pair_sum(g, theirs, core, name):
    _, _, rh, c_ = g.shape
    tr = _ew_tile(rh, c_, 2)

    def body(s_ref, g_ref, t_ref, o_ref):
        o_ref[...] = (g_ref[...].astype(F32) + t_ref[...].astype(F32)).astype(o_ref.dtype)

    blk = pl.BlockSpec((None, tr, c_), lambda j, i, s: (j, i, 0))
    gs = pltpu.PrefetchScalarGridSpec(
        num_scalar_prefetch=1, grid=(N_CHIPS, rh // tr),
        in_specs=[pl.BlockSpec((None, None, tr, c_), lambda j, i, s: (j, s[0], i, 0)), blk], out_specs=blk)
    return pl.pallas_call(body, grid_spec=gs, out_shape=S(theirs.shape, theirs.dtype), name=name,
                          compiler_params=_cp("parallel", "parallel"))(core.reshape(1), g, theirs)


def _chip_sum(q, got, l, me, core, into, name):
    _, rh, c_ = got.shape
    tr = _ew_tile(rh, c_, 4)

    def body(s_ref, q_ref, g0_ref, g1_ref, g2_ref, o_ref):
        acc = q_ref[...].astype(F32)
        for r in (g0_ref, g1_ref, g2_ref):
            acc = acc + r[...].astype(F32)
        o_ref[...] = acc

    in_specs = [pl.BlockSpec((None, tr, c_), lambda i, s: (s[0], i, 0))] + [
        pl.BlockSpec((None, tr, c_), functools.partial(lambda k, i, s: (k, i, 0), k)) for k in range(3)]
    return _call_into(
        body, into, in_specs, [jnp.stack([me, core]), q, got, got, got], n_prefetch=1, grid=(rh // tr,),
        out_specs=pl.BlockSpec((None, None, tr, c_), lambda i, s: (l, s[1], i, 0)),
        out_shape=S((DEPTH, 2, rh, c_), F32), name=name, compiler_params=_cp("parallel"))


def _pair_gather(gs4):
    def body(*refs):
        i_refs, o_refs = refs[:NBIG], refs[NBIG:2 * NBIG]
        send, recv = refs[2 * NBIG:]
        x, y, c, *_ = _mesh_pos()
        cps = []
        for w in range(NBIG):
            cp = pltpu.make_async_remote_copy(
                src_ref=i_refs[w].at[:, c], dst_ref=o_refs[w].at[:, c], send_sem=send.at[w], recv_sem=recv.at[w],
                device_id=(x, y, 1 - c), device_id_type=MESH)
            cp.start()
            cps.append(cp)
        for cp in cps:
            cp.wait()

    outs = pl.pallas_call(
        body, in_specs=[ANY] * NBIG, out_specs=[ANY] * NBIG, out_shape=[S(a.shape, a.dtype) for a in gs4],
        input_output_aliases={w: w for w in range(NBIG)}, name="grad_pair_gather",
        scratch_shapes=[pltpu.SemaphoreType.DMA((NBIG,))] * 2)(*gs4)
    return [o.reshape(o.shape[0], 2 * o.shape[2], o.shape[3]) for o in outs]


def _all_reduce_small(p, me, core, name):
    rows = p.shape[0]

    def place(s_ref, p_ref, o_ref):
        o_ref[...] = p_ref[...]

    gs = pltpu.PrefetchScalarGridSpec(
        num_scalar_prefetch=1, grid=(1,), in_specs=[pl.BlockSpec((rows, 128), lambda i, s: (0, 0))],
        out_specs=pl.BlockSpec((None, None, rows, 128), lambda i, s: (s[0], s[1], 0, 0)))
    mine = pl.pallas_call(place, grid_spec=gs, out_shape=S((N_CHIPS, 2, rows, 128), F32), name=name + "_place",
                          compiler_params=_cp("arbitrary"))(jnp.stack([me, core]), p)
    parts = _all_gather([mine], name + "_gather", per_core=True)[0]

    def total(g_ref, o_ref):
        acc = g_ref[0, 0]
        for j in range(N_CHIPS):
            for c in range(2):
                if (j, c) != (0, 0):
                    acc = acc + g_ref[j, c]
        o_ref[...] = acc

    vm = pl.BlockSpec(memory_space=pltpu.VMEM)
    return pl.pallas_call(total, in_specs=[vm], out_specs=vm, out_shape=S((rows, 128), F32), name=name + "_sum",
                          compiler_params=pltpu.CompilerParams(vmem_limit_bytes=VMEM_LIMIT))(parts)


PACK_UNIT = 8 * 128


def _pack(arrs):
    parts = []
    for a in arrs:
        flat = a.reshape(-1)
        pad = (-flat.shape[0]) % PACK_UNIT
        parts.append(jnp.pad(flat, (0, pad)).reshape(-1, 128))
    return jnp.concatenate(parts, axis=0)


def _unpack(buf, shapes):
    outs, row = [], 0
    for shp in shapes:
        n = int(np.prod(shp))
        rows = -(-n // PACK_UNIT) * 8
        outs.append(buf[row:row + rows].reshape(-1)[:n].reshape(shp))
        row += rows
    return outs


SMALL = ("norm1_g", "gm_ln_g", "gm_ln_b", "gm_ws", "gm_bs", "conv_w", "conv_b", "conv_ln_g", "conv_ln_b",
         "norm2_g", "final_g")
WEIGHTS = ("norm1_g", "w_in", "gm_ln_g", "gm_ln_b", "gm_ws", "gm_bs", "conv_w", "conv_b", "conv_ln_g",
           "conv_ln_b", "w_out", "norm2_g", "w_ffn_in", "w_ffn_out", "final_g")


def kernel(x, norm1_g, w_in, gm_ln_g, gm_ln_b, gm_ws, gm_bs, conv_w, conv_b, conv_ln_g, conv_ln_b, w_out, norm2_g, w_ffn_in, w_ffn_out, final_g, loss_target, m_norm1_g, m_w_in, m_gm_ln_g, m_gm_ln_b, m_gm_ws, m_gm_bs, m_conv_w, m_conv_b, m_conv_ln_g, m_conv_ln_b, m_w_out, m_norm2_g, m_w_ffn_in, m_w_ffn_out, m_final_g, v_norm1_g, v_w_in, v_gm_ln_g, v_gm_ln_b, v_gm_ws, v_gm_bs, v_conv_w, v_conv_b, v_conv_ln_g, v_conv_ln_b, v_w_out, v_norm2_g, v_w_ffn_in, v_w_ffn_out, v_final_g):
    given = dict(locals())
    t = x.shape[1]
    xc = x.reshape(t, D)
    target = loss_target.reshape(t, D)
    me = 2 * lax.axis_index("x") + lax.axis_index("y")
    core = lax.axis_index("c")
    tb = _tables(t)

    me = me.astype(jnp.int32)
    core = core.astype(jnp.int32)
    names = [n for n, _ in BIG]
    kinds = dict(BIG)
    gathered = [{n: _cast_to_gathered(given[n], l, me, f"cast_{n}{l}") for n in names} for l in range(DEPTH)]
    gathered[0]["w_in"] = _all_gather([gathered[0]["w_in"]], "all_gather_w_in0")[0]

    def weight(l, n):
        b = gathered[l][n]
        r_, c_ = 2 * b.shape[2], b.shape[3]
        return b.reshape(N_CHIPS, r_, c_) if kinds[n] == "col" else b.reshape(N_CHIPS * r_, c_)

    cshard = CV_W // N_CHIPS
    placed = lax.dynamic_update_slice(jnp.zeros((DEPTH, KCONV, CV_W), F32),
                                      conv_w * (core == 0).astype(F32), (0, 0, me * cshard))
    conv_w_full = _unpack(_all_reduce_small(_pack([placed]), me, core, "gather_conv_w"), [(DEPTH, KCONV, CV_W)])[0]
    cw32 = jnp.pad(conv_w_full, ((0, 0), (0, 32 - KCONV), (0, 0)))

    def row(a, l):
        return a[l].reshape(1, -1)

    saved = []
    early = ["w_in", "w_out", "w_ffn_in"]
    for l in range(DEPTH):
        cur = gathered[l]
        nxt = gathered[l + 1] if l + 1 < DEPTH else None
        sv = {"x": xc}
        bias = jnp.repeat(gm_bs[l].T, GM_W // GM_HEADS, axis=1)
        first = ["w_ffn_in"] if l == 0 else ["w_ffn_out"]
        late = ["w_out", "w_ffn_out"]
        proj, rid = _norm_mm(xc, row(norm1_g, l), weight(l, "w_in"), F32, f"in_proj{l}", 512,
                             [("ici" if l == 0 else "d2d", [cur[n] for n in first])], (tb["cos2"], tb["sin2"]))
        cur.update(zip(first, rid))
        y_gm, rid = _gm_fwd(proj, row(gm_ln_g, l), row(gm_ln_b, l), gm_ws[l], bias, f"gm_fwd{l}",
                            [("d2d", [cur[n] for n in first])] if l == 0 else ())
        cur.update(zip(first, rid))
        sf, sb = _ret_scan(proj, K_COL, proj, V_COL, tb["zf"], tb["zb"], tb["gcf"], tb["gcb"], f"ret_state{l}")
        a, y_ret, rid = _ret_out(proj, sf, sb, tb, f"ret_out{l}",
                                 [("ici", [cur[n] for n in late])] if l == 0 else ())
        cur.update(zip(late, rid))
        y_cv, hc, rid = _conv_fwd(proj, cw32[l], row(conv_b, l), row(conv_ln_g, l), row(conv_ln_b, l),
                                  f"conv_fwd{l}", [("d2d", [cur[n] for n in late])] if l == 0 else ())
        cur.update(zip(late, rid))
        x_mid = _parts_mm_res([y_gm, y_ret, y_cv], weight(l, "w_out"), xc, f"out_proj{l}")
        ff, rid = _norm_mm(x_mid, row(norm2_g, l), weight(l, "w_ffn_in"), ACT_DTYPE, f"ffn_in{l}", 512,
                           [("ici", [nxt[n] for n in early])] if nxt else ())
        if nxt:
            nxt.update(zip(early, rid))
        xc, rid = _swiglu_mm_res(ff, weight(l, "w_ffn_out"), x_mid, f"ffn_out{l}",
                                 [("d2d", [nxt[n] for n in early]), ("ici", [nxt["w_ffn_out"]])] if nxt else ())
        if nxt:
            nxt.update(zip(early + ["w_ffn_out"], rid))
        sv.update(bias=bias, proj=proj, y_gm=y_gm, sf=sf, sb=sb, a=a, y_ret=y_ret, y_cv=y_cv, hc=hc, x_mid=x_mid,
                  ff=ff)
        saved.append(sv)

    dx, d_final_g, lpart = _loss_head(xc, final_g.reshape(1, D), target, "loss_head")

    small_g = {n: [None] * DEPTH for n in SMALL}
    qs = [{} for _ in range(DEPTH)]
    got = [{} for _ in range(DEPTH)]
    ffn_w, mix_w = ["w_ffn_out", "w_ffn_in"], ["w_out", "w_in"]

    def halves(big_g, group):
        return [big_g[n].reshape(N_CHIPS, 2, given[n].shape[1] // 2, given[n].shape[2]) for n in group]

    def pair_sums(l, group, g4, theirs):
        qs[l].update({n: _pair_sum(g, th, core, f"pair_sum_{n}{l}") for n, g, th in zip(group, g4, theirs)})
        return [qs[l][n] for n in group]

    for l in reversed(range(DEPTH)):
        sv = saved[l]
        proj = sv["proj"]
        big_g = {}
        dff = _dx_swiglu(dx, weight(l, "w_ffn_out"), sv["ff"], f"ffn_out_dx{l}")
        big_g["w_ffn_out"] = _dw_swiglu(sv["ff"], dx, f"ffn_out_dw{l}")
        dx_mid, dg2, _ = _dx_norm([dff], weight(l, "w_ffn_in"), sv["x_mid"], row(norm2_g, l), dx,
                                  f"ffn_in_dx{l}", 512)
        big_g["w_ffn_in"], _ = _dw_norm_cols(sv["x_mid"], row(norm2_g, l), dff, w_ffn_in.shape[2], f"ffn_in_dw{l}")
        g4 = halves(big_g, ffn_w)
        (dy_gm, dy_ret, dy_cv), theirs = _dx_parts(dx_mid, weight(l, "w_out"), [GM_W, RET_W, CV_W],
                                                   f"out_proj_dx{l}", [("pairx", g4)])
        q_ffn = pair_sums(l, ffn_w, g4, theirs)
        big_g["w_out"] = _dw_parts([sv["y_gm"], sv["y_ret"], sv["y_cv"]], dx_mid, f"out_proj_dw{l}")
        d_cv, dcw, dcb, dclg, dclb, rid = _conv_bwd(proj, dy_cv, sv["hc"], cw32[l], row(conv_ln_g, l),
                                                    row(conv_ln_b, l), f"conv_bwd{l}", [("scatter", q_ffn[:1])])
        got[l].update(zip(ffn_w[:1], rid))
        da, d_g = _ret_bwd_pre(dy_ret, sv["a"], proj, f"ret_bwd_pre{l}")
        gb_, gf_ = _ret_scan(proj, Q_COL, da, 0, tb["xib"], tb["xif"], tb["gcb"], tb["gcf"], f"ret_bwd_state{l}")
        d_qkv, rid = _ret_bwd_main(proj, da, sv["sf"], sv["sb"], gf_, gb_, tb, f"ret_bwd_main{l}",
                                   [("scatter", q_ffn[1:])])
        got[l].update(zip(ffn_w[1:], rid))
        d_gm, dws, dbs, dglg, dglb = _gm_bwd(proj, dy_gm, row(gm_ln_g, l), row(gm_ln_b, l), gm_ws[l],
                                             jnp.swapaxes(gm_ws[l], 1, 2), sv["bias"], f"gm_bwd{l}")
        dparts = [d_gm, d_qkv, d_g, d_cv]
        big_g["w_in"] = _dw_norm_parts(sv["x"], row(norm1_g, l), dparts, w_in.shape[2], f"in_proj_dw{l}")
        g4 = halves(big_g, mix_w)
        q_mix = pair_sums(l, mix_w, g4, _pair_exchange(g4, f"grad_pair_exchange_mix{l}"))
        dx, dg1, rid = _dx_norm(dparts, weight(l, "w_in"), sv["x"], row(norm1_g, l), dx_mid, f"in_proj_dx{l}", 512,
                                [("scatter", q_mix)])
        got[l].update(zip(mix_w, rid))
        for n, val in (("norm1_g", dg1[0]), ("gm_ln_g", dglg[0]), ("gm_ln_b", dglb[0]), ("gm_ws", dws),
                       ("gm_bs", dbs[:, :GM_HEADS].T), ("conv_w", dcw[:KCONV]), ("conv_b", dcb[0]),
                       ("conv_ln_g", dclg[0]), ("conv_ln_b", dclb[0]), ("norm2_g", dg2[0])):
            small_g[n][l] = val

    small_shapes = [given[n].shape if n != "conv_w" else (DEPTH, KCONV, CV_W) for n in SMALL]
    partials = [d_final_g[0] if n == "final_g" else jnp.stack(small_g[n]) for n in SMALL]
    summed = _unpack(_all_reduce_small(_pack(partials + [lpart]), me, core, "all_reduce_small_grads"),
                     small_shapes + [lpart.shape])
    loss = summed[-1][0, 0]
    reduced = dict(zip(SMALL, summed))
    reduced["conv_w"] = lax.dynamic_slice(reduced["conv_w"], (0, 0, me * cshard), (DEPTH, KCONV, cshard))

    halves = [None] * NBIG
    for l in reversed(range(DEPTH)):
        halves = [_chip_sum(qs[l][n], got[l][n], l, me, core, h, f"chip_sum_{n}{l}") for n, h in zip(names, halves)]
    grads = dict(zip(names, _pair_gather(halves)))
    grads.update(reduced)

    delta, new_m, new_v = {}, {}, {}
    for n, _ in BIG:
        delta[n], new_m[n], new_v[n] = _adamw(given[n], grads[n], given["m_" + n], given["v_" + n], f"adamw_{n}")
    shapes = [given[n].shape for n in SMALL]
    packed = [_pack([src[n] if src is grads else src[p + n] for n in SMALL])
              for src, p in ((given, ""), (grads, ""), (given, "m_"), (given, "v_"))]
    outs = _adamw(*packed, "adamw_small")
    for dst, buf in zip((delta, new_m, new_v), outs):
        dst.update(zip(SMALL, _unpack(buf, shapes)))

    return (loss, dx.reshape(1, t, D), *[grads[n] for n in WEIGHTS], *[delta[n] for n in WEIGHTS],
            *[new_m[n] for n in WEIGHTS], *[new_v[n] for n in WEIGHTS])
```

```python
import functools
import math

import numpy as np
import jax
import jax.numpy as jnp
from jax import lax
from jax.experimental import pallas as pl
from jax.experimental.pallas import tpu as pltpu

F32 = jnp.float32
BF16 = jnp.bfloat16
MXU_DTYPE = BF16
ACT_DTYPE = BF16
S = jax.ShapeDtypeStruct

D = 1024
DEPTH = 2
GM_W = 256
GM_HEADS = 4
RET_W = 512
HEADS = 4
HD = 128
CV_W = 256
KCONV = 31
IN_W = 2 * GM_W + 4 * RET_W + 2 * CV_W
FFN_H = 2816
CH = 128
ROPE_BASE = 10000.0
EPS = 1e-6
N_CHIPS = 4
N_DEV = 8
HALO = 16

ADAM_LR = 0.001
ADAM_B1 = 0.9
ADAM_B2 = 0.999
ADAM_EPS = 1e-08
ADAM_WD = 0.01
ADAM_STEP = 10

VMEM_LIMIT = 52 * 1024 * 1024
MESH = pl.DeviceIdType.MESH


def _cp(*sem, vmem=VMEM_LIMIT):
    return pltpu.CompilerParams(dimension_semantics=tuple(sem), vmem_limit_bytes=vmem)


def _mx(a):
    return a.astype(MXU_DTYPE)


def _dot(a, b):
    return jnp.dot(_mx(a), _mx(b), preferred_element_type=F32)


def _dot_nt(a, b):
    return lax.dot_general(_mx(a), _mx(b), (((1,), (1,)), ((), ())), preferred_element_type=F32)


def _dot_tn(a, b):
    return lax.dot_general(_mx(a), _mx(b), (((0,), (0,)), ((), ())), preferred_element_type=F32)


def _sigmoid(x):
    return 1.0 / (1.0 + jnp.exp(-x))


def _gelu(x):
    return 0.5 * x * (1.0 + lax.erf(x * (1.0 / math.sqrt(2.0))))


def _gelu_grad(x):
    return 0.5 * (1.0 + lax.erf(x * (1.0 / math.sqrt(2.0)))) + x * jnp.exp(-0.5 * x * x) * (1.0 / math.sqrt(2.0 * math.pi))


def _rms_r(x):
    return lax.rsqrt(jnp.mean(x * x, axis=-1, keepdims=True) + EPS)


def _rms_bwd(dh, x, r, g):
    u = dh * g
    dx = r * u - x * (r * r * r) * jnp.mean(u * x, axis=-1, keepdims=True)
    return dx, dh * x * r


def _standardize(a):
    mu = jnp.mean(a, axis=-1, keepdims=True)
    d = a - mu
    r = lax.rsqrt(jnp.mean(d * d, axis=-1, keepdims=True) + EPS)
    return d * r, r


def _standardize_bwd(do, o, r):
    return r * (do - jnp.mean(do, axis=-1, keepdims=True) - o * jnp.mean(do * o, axis=-1, keepdims=True))


def _acc_out(ref, val, first):
    @pl.when(first)
    def _():
        ref[...] = val

    @pl.when(jnp.logical_not(first))
    def _():
        ref[...] += val


def _row_tile(t, pref):
    tm = min(t, pref)
    assert t % tm == 0, (t, tm)
    return tm


def _segments(part_widths, shard_w):
    bounds = {0}
    off = 0
    for w in part_widths:
        off += w
        bounds.add(off)
    total = off
    for j in range(1, total // shard_w + 1):
        bounds.add(j * shard_w)
    bounds = sorted(bounds)
    starts = np.cumsum([0] + list(part_widths))
    segs = []
    for a, b in zip(bounds[:-1], bounds[1:]):
        p = int(np.searchsorted(starts, a, side="right") - 1)
        segs.append((p, a - int(starts[p]), a // shard_w, a % shard_w, b - a))
    return segs


ANY = pl.BlockSpec(memory_space=pl.ANY)


def _mesh_pos():
    x, y, c = lax.axis_index("x"), lax.axis_index("y"), lax.axis_index("c")
    chips = [(1 - x, y), (x, 1 - y), (1 - x, 1 - y)]
    return x, y, c, 2 * x + y, chips, [2 * cx + cy for cx, cy in chips]


def _rider_copies(kind, i_refs, o_refs, send, recv, pos):
    x, y, c, me, chips, cj = pos
    out = []
    for b, (i_ref, o_ref) in enumerate(zip(i_refs, o_refs)):
        for k in range(1 if kind == "pairx" else 3):
            if kind == "ici":
                src, dst, land, dev = i_ref.at[me, c], o_ref.at[me, c], o_ref.at[cj[k], c], (*chips[k], c)
            elif kind == "d2d":
                src, dst, land, dev = i_ref.at[cj[k], c], o_ref.at[cj[k], c], o_ref.at[cj[k], 1 - c], (x, y, 1 - c)
            elif kind == "pairx":
                src, dst, land, dev = i_ref.at[:, 1 - c], o_ref, o_ref, (x, y, 1 - c)
            else:
                src, dst, land, dev = i_ref.at[cj[k]], o_ref.at[k], o_ref.at[k], (*chips[k], c)
            out.append(tuple(pltpu.make_async_remote_copy(
                src_ref=s_, dst_ref=d_, send_sem=send.at[b, k], recv_sem=recv.at[b, k],
                device_id=dev, device_id_type=MESH) for s_, d_ in ((src, dst), (land, land))))
    return out


def _rider_out_shape(kind, a):
    if kind == "scatter":
        return S((3,) + a.shape[1:], a.dtype)
    if kind == "pairx":
        return S(a.shape[:1] + a.shape[2:], a.dtype)
    return S(a.shape, a.dtype)


def _pcall(body, args, riders, *, grid, in_specs, out_specs, out_shape, name, sem, scratch_shapes=()):
    outs = list(out_shape)
    if not riders:
        res = pl.pallas_call(body, grid=grid, in_specs=in_specs, out_specs=out_specs, out_shape=outs, name=name,
                             scratch_shapes=list(scratch_shapes), compiler_params=_cp(*sem))(*args)
        return res, []
    r_in = [a for _, bufs in riders for a in bufs]
    r_out = [_rider_out_shape(kind, a) for kind, bufs in riders for a in bufs]
    n_in, n_out, n_scr, n_r = len(args), len(outs), len(scratch_shapes), len(r_in)
    aliases, idx = {}, 0
    for kind, bufs in riders:
        for _ in bufs:
            if kind in ("ici", "d2d"):
                aliases[n_in + idx] = n_out + idx
            idx += 1
    sems = [pltpu.SemaphoreType.DMA((len(bufs), 3)) for _, bufs in riders for _ in range(2)]

    def wrapped(*refs):
        a, ri = refs[:n_in], refs[n_in:n_in + n_r]
        o, ro = refs[n_in + n_r:n_in + n_r + n_out], refs[n_in + n_r + n_out:n_in + 2 * n_r + n_out]
        scr = refs[n_in + 2 * n_r + n_out:n_in + 2 * n_r + n_out + n_scr]
        sm = refs[n_in + 2 * n_r + n_out + n_scr:]
        pos = _mesh_pos()
        copies, off = [], 0
        for r, (kind, bufs) in enumerate(riders):
            copies += _rider_copies(kind, ri[off:off + len(bufs)], ro[off:off + len(bufs)], sm[2 * r], sm[2 * r + 1], pos)
            off += len(bufs)
        ids = [pl.program_id(d) for d in range(len(grid))]
        first = functools.reduce(jnp.logical_and, [i == 0 for i in ids])
        last = functools.reduce(jnp.logical_and, [i == n - 1 for i, n in zip(ids, grid)])

        @pl.when(first)
        def _():
            for cp, _ in copies:
                cp.start()

        body(*a, *o, *scr)

        @pl.when(last)
        def _():
            for cp, land in copies:
                land.wait_recv()
                cp.wait_send()

    res = pl.pallas_call(
        wrapped, grid=grid, in_specs=list(in_specs) + [ANY] * n_r, out_specs=list(out_specs) + [ANY] * n_r,
        out_shape=outs + r_out, input_output_aliases=aliases, name=name,
        scratch_shapes=list(scratch_shapes) + sems, compiler_params=_cp(*(("arbitrary",) * len(grid))))(*args, *r_in)
    return res[:n_out], res[n_out:]


def _wcol_spec(w):
    return pl.BlockSpec(w.shape, lambda *_: (0, 0, 0))


def _wrow_spec(w):
    return pl.BlockSpec(w.shape, lambda *_: (0, 0))


def _norm_mm(x, g, w, out_dtype, name, tm_pref, riders=(), rope=None):
    t = x.shape[0]
    nc = w.shape[2]
    tm = _row_tile(t, tm_pref)
    extra = list(rope) if rope else []

    def body(x_ref, g_ref, w_ref, *rest):
        o_ref = rest[-1]
        xv = x_ref[...]
        h = _mx(xv * _rms_r(xv) * g_ref[...])
        for j in range(N_CHIPS):
            o_ref[:, j * nc:(j + 1) * nc] = jnp.dot(h, w_ref[j], preferred_element_type=F32).astype(o_ref.dtype)
        if rope:
            _rotate_qk(o_ref, rest[0][...], rest[1][...])

    (out,), rid = _pcall(
        body, [x, g, w] + extra, riders, grid=(t // tm,),
        in_specs=[pl.BlockSpec((tm, D), lambda i: (i, 0)), pl.BlockSpec((1, D), lambda i: (0, 0)), _wcol_spec(w)]
        + [pl.BlockSpec((tm, HD), lambda i: (i, 0)) for _ in extra],
        out_specs=[pl.BlockSpec((tm, N_CHIPS * nc), lambda i: (i, 0))],
        out_shape=[S((t, N_CHIPS * nc), out_dtype)], name=name, sem=("parallel",))
    return out, rid


def _parts_mm_res(parts, w, res, name):
    t = res.shape[0]
    tm = _row_tile(t, 512)
    widths = [p.shape[1] for p in parts]
    offs = np.cumsum([0] + widths)
    n = len(parts)

    def body(*refs):
        p_refs, w_ref, r_ref, o_ref = refs[:n], refs[n], refs[n + 1], refs[n + 2]
        acc = r_ref[...]
        for p in range(n):
            acc = acc + _dot(p_refs[p][...], w_ref[int(offs[p]):int(offs[p + 1]), :])
        o_ref[...] = acc

    return pl.pallas_call(
        body, grid=(t // tm,),
        in_specs=[pl.BlockSpec((tm, wd), lambda i: (i, 0)) for wd in widths]
        + [_wrow_spec(w), pl.BlockSpec((tm, D), lambda i: (i, 0))],
        out_specs=pl.BlockSpec((tm, D), lambda i: (i, 0)),
        out_shape=S((t, D), F32), name=name, compiler_params=_cp("parallel"))(*parts, w, res)


def _swiglu(ff):
    gate = ff[:, :FFN_H].astype(F32)
    up = ff[:, FFN_H:].astype(F32)
    return gate * _sigmoid(gate) * up


def _swiglu_mm_res(ff, w, res, name, riders=()):
    t = res.shape[0]
    tm = _row_tile(t, 512)

    def body(f_ref, w_ref, r_ref, o_ref):
        o_ref[...] = r_ref[...] + _dot(_swiglu(f_ref[...]), w_ref[...])

    (out,), rid = _pcall(
        body, [ff, w, res], riders, grid=(t // tm,),
        in_specs=[pl.BlockSpec((tm, 2 * FFN_H), lambda i: (i, 0)), _wrow_spec(w),
                  pl.BlockSpec((tm, D), lambda i: (i, 0))],
        out_specs=[pl.BlockSpec((tm, D), lambda i: (i, 0))],
        out_shape=[S((t, D), F32)], name=name, sem=("parallel",))
    return out, rid


def _dx_norm(dparts, w, x, g, dres, name, tm_pref, riders=()):
    t = x.shape[0]
    nc = w.shape[2]
    tm = _row_tile(t, tm_pref)
    widths = [p.shape[1] for p in dparts]
    segs = _segments(widths, nc)
    n = len(dparts)

    def body(*refs):
        d_refs = refs[:n]
        w_ref, x_ref, g_ref, r_ref, dx_ref, dg_ref = refs[n:]
        dh = jnp.zeros((tm, D), F32)
        for (p, po, j, jo, wd) in segs:
            dh = dh + _dot_nt(d_refs[p][:, po:po + wd], w_ref[j, :, jo:jo + wd])
        xv = x_ref[...]
        dx, dgrow = _rms_bwd(dh, xv, _rms_r(xv), g_ref[...])
        dx_ref[...] = r_ref[...] + dx
        _acc_out(dg_ref, jnp.sum(dgrow, axis=0, keepdims=True), pl.program_id(0) == 0)

    (dx, dg), rid = _pcall(
        body, [*dparts, w, x, g, dres], riders, grid=(t // tm,),
        in_specs=[pl.BlockSpec((tm, wd), lambda i: (i, 0)) for wd in widths]
        + [_wcol_spec(w), pl.BlockSpec((tm, D), lambda i: (i, 0)),
           pl.BlockSpec((1, D), lambda i: (0, 0)), pl.BlockSpec((tm, D), lambda i: (i, 0))],
        out_specs=[pl.BlockSpec((tm, D), lambda i: (i, 0)), pl.BlockSpec((1, D), lambda i: (0, 0))],
        out_shape=[S((t, D), F32), S((1, D), F32)], name=name, sem=("arbitrary",))
    return dx, dg, rid


def _dx_parts(dy, w, widths, name, riders=()):
    t = dy.shape[0]
    tm = _row_tile(t, 512)
    offs = np.cumsum([0] + list(widths))
    n = len(widths)

    def body(dy_ref, w_ref, *o_refs):
        dyv = _mx(dy_ref[...])
        for p in range(n):
            o_refs[p][...] = _dot_nt(dyv, w_ref[int(offs[p]):int(offs[p + 1]), :])

    return _pcall(
        body, [dy, w], riders, grid=(t // tm,),
        in_specs=[pl.BlockSpec((tm, D), lambda i: (i, 0)), _wrow_spec(w)],
        out_specs=[pl.BlockSpec((tm, wd), lambda i: (i, 0)) for wd in widths],
        out_shape=[S((t, wd), F32) for wd in widths], name=name, sem=("parallel",))


def _dx_swiglu(dy, w, ff, name):
    t = dy.shape[0]
    tm = _row_tile(t, 512)

    def body(dy_ref, w_ref, f_ref, o_ref):
        dact = _dot_nt(dy_ref[...], w_ref[...])
        gate = f_ref[:, :FFN_H].astype(F32)
        up = f_ref[:, FFN_H:].astype(F32)
        s = _sigmoid(gate)
        gs = gate * s
        o_ref[:, :FFN_H] = ((dact * up) * (s + gs - gs * s)).astype(o_ref.dtype)
        o_ref[:, FFN_H:] = (dact * gs).astype(o_ref.dtype)

    return pl.pallas_call(
        body, grid=(t // tm,),
        in_specs=[pl.BlockSpec((tm, D), lambda i: (i, 0)), _wrow_spec(w),
                  pl.BlockSpec((tm, 2 * FFN_H), lambda i: (i, 0))],
        out_specs=pl.BlockSpec((tm, 2 * FFN_H), lambda i: (i, 0)),
        out_shape=S((t, 2 * FFN_H), ACT_DTYPE), name=name, compiler_params=_cp("parallel"))(dy, w, ff)


def _call_into(body, into, in_specs, args, *, n_prefetch, grid, out_specs, **kw):
    n_in = len(args)
    if into is None:
        gs = pltpu.PrefetchScalarGridSpec(num_scalar_prefetch=n_prefetch, grid=grid, in_specs=in_specs,
                                          out_specs=out_specs)
        return pl.pallas_call(body, grid_spec=gs, **kw)(*args)

    def wrapped(*refs):
        return body(*refs[:n_in], *refs[n_in + 1:])

    gs = pltpu.PrefetchScalarGridSpec(num_scalar_prefetch=n_prefetch, grid=grid,
                                      in_specs=list(in_specs) + [ANY], out_specs=out_specs)
    return pl.pallas_call(wrapped, grid_spec=gs, input_output_aliases={n_in: 0}, **kw)(*args, into)


def _dw_norm_parts(x, g, dparts, nc, name):
    t = x.shape[0]
    tk = _row_tile(t, 1024)
    widths = [p.shape[1] for p in dparts]
    segs = _segments(widths, nc)
    n = len(dparts)
    nk = t // tk

    def body(*refs):
        x_ref, g_ref = refs[0], refs[1]
        d_refs = refs[2:2 + n]
        o_ref, acc_ref = refs[2 + n], refs[3 + n]
        k = pl.program_id(0)
        xv = x_ref[...]
        h = _mx(xv * _rms_r(xv) * g_ref[...])

        @pl.when(k == 0)
        def _():
            acc_ref[...] = jnp.zeros_like(acc_ref)

        for (p, po, j, jo, wd) in segs:
            acc_ref[j, :, jo:jo + wd] += _dot_tn(h, d_refs[p][:, po:po + wd])

        @pl.when(k == nk - 1)
        def _():
            o_ref[...] = acc_ref[...].astype(o_ref.dtype)

    return pl.pallas_call(
        body, grid=(nk,),
        in_specs=[pl.BlockSpec((tk, D), lambda k: (k, 0)), pl.BlockSpec((1, D), lambda k: (0, 0))]
        + [pl.BlockSpec((tk, wd), lambda k: (k, 0)) for wd in widths],
        out_specs=pl.BlockSpec((N_CHIPS, D, nc), lambda k: (0, 0, 0)),
        out_shape=S((N_CHIPS, D, nc), MXU_DTYPE), name=name,
        scratch_shapes=[pltpu.VMEM((N_CHIPS, D, nc), F32)], compiler_params=_cp("arbitrary"))(x, g, *dparts)


def _dw_norm_cols(x, g, dy, nc, name, riders=()):
    t = x.shape[0]
    tk = _row_tile(t, 1024)
    nk = t // tk

    def body(x_ref, g_ref, dy_ref, o_ref, acc_ref):
        k = pl.program_id(1)
        xv = x_ref[...]
        h = _mx(xv * _rms_r(xv) * g_ref[...])

        @pl.when(k == 0)
        def _():
            acc_ref[...] = jnp.zeros_like(acc_ref)

        acc_ref[...] += _dot_tn(h, dy_ref[...])

        @pl.when(k == nk - 1)
        def _():
            o_ref[...] = acc_ref[...].astype(o_ref.dtype)

    (out,), rid = _pcall(
        body, [x, g, dy], riders, grid=(N_CHIPS, nk),
        in_specs=[pl.BlockSpec((tk, D), lambda j, k: (k, 0)), pl.BlockSpec((1, D), lambda j, k: (0, 0)),
                  pl.BlockSpec((tk, nc), lambda j, k: (k, j))],
        out_specs=[pl.BlockSpec((None, D, nc), lambda j, k: (j, 0, 0))],
        out_shape=[S((N_CHIPS, D, nc), MXU_DTYPE)], name=name, sem=("parallel", "arbitrary"),
        scratch_shapes=[pltpu.VMEM((D, nc), F32)])
    return out, rid


def _dw_parts(parts, dy, name):
    t = dy.shape[0]
    tk = _row_tile(t, 1024)
    widths = [p.shape[1] for p in parts]
    offs = np.cumsum([0] + widths)
    ktot = int(offs[-1])
    n = len(parts)
    nk = t // tk

    def body(*refs):
        p_refs, dy_ref, o_ref, acc_ref = refs[:n], refs[n], refs[n + 1], refs[n + 2]
        k = pl.program_id(0)

        @pl.when(k == 0)
        def _():
            acc_ref[...] = jnp.zeros_like(acc_ref)

        dyv = _mx(dy_ref[...])
        for p in range(n):
            acc_ref[int(offs[p]):int(offs[p + 1]), :] += _dot_tn(p_refs[p][...], dyv)

        @pl.when(k == nk - 1)
        def _():
            o_ref[...] = acc_ref[...].astype(o_ref.dtype)

    return pl.pallas_call(
        body, grid=(nk,),
        in_specs=[pl.BlockSpec((tk, wd), lambda k: (k, 0)) for wd in widths]
        + [pl.BlockSpec((tk, D), lambda k: (k, 0))],
        out_specs=pl.BlockSpec((ktot, D), lambda k: (0, 0)),
        out_shape=S((ktot, D), MXU_DTYPE), name=name,
        scratch_shapes=[pltpu.VMEM((ktot, D), F32)], compiler_params=_cp("arbitrary"))(*parts, dy)


def _dw_swiglu(ff, dy, name):
    t = dy.shape[0]
    tk = _row_tile(t, 512)
    nk = t // tk

    def body(f_ref, dy_ref, o_ref, acc_ref):
        k = pl.program_id(0)

        @pl.when(k == 0)
        def _():
            acc_ref[...] = jnp.zeros_like(acc_ref)

        acc_ref[...] += _dot_tn(_swiglu(f_ref[...]), dy_ref[...])

        @pl.when(k == nk - 1)
        def _():
            o_ref[...] = acc_ref[...].astype(o_ref.dtype)

    return pl.pallas_call(
        body, grid=(nk,),
        in_specs=[pl.BlockSpec((tk, 2 * FFN_H), lambda k: (k, 0)), pl.BlockSpec((tk, D), lambda k: (k, 0))],
        out_specs=pl.BlockSpec((FFN_H, D), lambda k: (0, 0)),
        out_shape=S((FFN_H, D), MXU_DTYPE), name=name,
        scratch_shapes=[pltpu.VMEM((FFN_H, D), F32)], compiler_params=_cp("arbitrary"))(ff, dy)


def _tables(t):
    half = HD // 2
    inv_freq = ROPE_BASE ** (-jnp.arange(half, dtype=F32) / half)
    base = (jnp.arange(t // CH, dtype=F32) * CH)[:, None] * inv_freq[None, :]
    off = jnp.arange(CH, dtype=F32)[:, None] * inv_freq[None, :]
    cb, sb, co, so = jnp.cos(base)[:, None], jnp.sin(base)[:, None], jnp.cos(off)[None], jnp.sin(off)[None]
    cos = (cb * co - sb * so).reshape(t, half)
    sin = (sb * co + cb * so).reshape(t, half)
    tb = {"cos2": jnp.concatenate([cos, cos], axis=1), "sin2": jnp.concatenate([-sin, sin], axis=1)}
    gf = 1.0 - jnp.exp2(-5.0 - jnp.arange(HEADS, dtype=F32))
    lgf = jnp.log(gf)[:, None]
    lgb = jnp.log(gf[::-1])[:, None]
    idx = jnp.arange(CH, dtype=F32)
    diff = idx[:, None] - idx[None, :]
    dfwd = jnp.where(diff >= 0, jnp.exp(lgf[:, :, None] * jnp.where(diff >= 0, diff, 0.0)), 0.0)
    dbwd = jnp.where(diff < 0, jnp.exp(lgb[:, :, None] * jnp.where(diff < 0, -diff, 0.0)), 0.0)
    tb["dm"] = dfwd + dbwd
    tb["dmt"] = jnp.swapaxes(tb["dm"], 1, 2)

    def lanes(a):
        return jnp.repeat(a.T, HD, axis=1)

    tb["xif"] = lanes(jnp.exp(lgf * (idx + 1)))
    tb["zf"] = lanes(jnp.exp(lgf * (CH - 1 - idx)))
    tb["xib"] = lanes(jnp.exp(lgb * (CH - idx)))
    tb["zb"] = lanes(jnp.exp(lgb * idx))
    tb["gcf"] = jnp.repeat(jnp.exp(lgf * CH), HD, axis=0).reshape(1, HEADS * HD)
    tb["gcb"] = jnp.repeat(jnp.exp(lgb * CH), HD, axis=0).reshape(1, HEADS * HD)
    return tb


def _full(shape):
    nd = len(shape)
    return pl.BlockSpec(shape, lambda *_: (0,) * nd)


def _gm_mixed(vn, ws_ref, bias):
    lane = lax.broadcasted_iota(jnp.int32, (CH, 128), 1)
    halves = []
    for hf in range(2):
        vh = _mx(vn[:, hf * 128:(hf + 1) * 128])
        r0 = jnp.dot(_mx(ws_ref[2 * hf]), vh, preferred_element_type=F32)
        r1 = jnp.dot(_mx(ws_ref[2 * hf + 1]), vh, preferred_element_type=F32)
        halves.append(jnp.where(lane < 64, r0, r1))
    return jnp.concatenate(halves, axis=1) + bias


def _gm_fwd(proj, ln_g, ln_b, ws, bias, name, riders=()):
    t = proj.shape[0]
    tm = _row_tile(t, 512)

    def body(pu_ref, pv_ref, g_ref, b_ref, ws_ref, bias_ref, o_ref):
        for c in range(tm // CH):
            rows = slice(c * CH, (c + 1) * CH)
            u = _gelu(pu_ref[rows, :])
            o, _ = _standardize(_gelu(pv_ref[rows, :]))
            vn = o * g_ref[...] + b_ref[...]
            o_ref[rows, :] = (u * _gm_mixed(vn, ws_ref, bias_ref[...])).astype(o_ref.dtype)

    (out,), rid = _pcall(
        body, [proj, proj, ln_g, ln_b, ws, bias], riders, grid=(t // tm,),
        in_specs=[pl.BlockSpec((tm, GM_W), lambda i: (i, 0)), pl.BlockSpec((tm, GM_W), lambda i: (i, 1)),
                  _full((1, GM_W)), _full((1, GM_W)), _full((GM_HEADS, CH, CH)), _full((CH, GM_W))],
        out_specs=[pl.BlockSpec((tm, GM_W), lambda i: (i, 0))],
        out_shape=[S((t, GM_W), ACT_DTYPE)], name=name, sem=("parallel",))
    return out, rid


def _gm_bwd(proj, dy, ln_g, ln_b, ws, wst, bias, name):
    t = proj.shape[0]
    tm = _row_tile(t, 512)
    nb = t // tm

    def body(pu_ref, pv_ref, dy_ref, g_ref, b_ref, ws_ref, wst_ref, bias_ref,
             d_ref, dws_ref, dbs_ref, dg_ref, db_ref, dbias_ref):
        first = pl.program_id(0) == 0
        lane = lax.broadcasted_iota(jnp.int32, (CH, 128), 1)
        dws = [jnp.zeros((CH, CH), F32) for _ in range(GM_HEADS)]
        dbias = jnp.zeros((CH, GM_W), F32)
        dg = jnp.zeros((1, GM_W), F32)
        db = jnp.zeros((1, GM_W), F32)
        for c in range(tm // CH):
            rows = slice(c * CH, (c + 1) * CH)
            pu = pu_ref[rows, :]
            pv = pv_ref[rows, :]
            u = _gelu(pu)
            o, r = _standardize(_gelu(pv))
            vn = o * g_ref[...] + b_ref[...]
            mixed = _gm_mixed(vn, ws_ref, bias_ref[...])
            dyv = dy_ref[rows, :]
            d_ref[rows, :GM_W] = (dyv * mixed * _gelu_grad(pu)).astype(d_ref.dtype)
            dmixed = dyv * u
            dbias = dbias + dmixed
            dvn_halves = []
            for hf in range(2):
                dm = dmixed[:, hf * 128:(hf + 1) * 128]
                vh = vn[:, hf * 128:(hf + 1) * 128]
                dm0 = jnp.where(lane < 64, dm, 0.0)
                dm1 = dm - dm0
                dws[2 * hf] = dws[2 * hf] + _dot_nt(dm0, vh)
                dws[2 * hf + 1] = dws[2 * hf + 1] + _dot_nt(dm1, vh)
                t0 = _dot(wst_ref[2 * hf], dm)
                t1 = _dot(wst_ref[2 * hf + 1], dm)
                dvn_halves.append(jnp.where(lane < 64, t0, t1))
            dvn = jnp.concatenate(dvn_halves, axis=1)
            dg = dg + jnp.sum(dvn * o, axis=0, keepdims=True)
            db = db + jnp.sum(dvn, axis=0, keepdims=True)
            dv = _standardize_bwd(dvn * g_ref[...], o, r)
            d_ref[rows, GM_W:] = (dv * _gelu_grad(pv)).astype(d_ref.dtype)
        for h in range(GM_HEADS):
            _acc_out(dws_ref.at[h], dws[h], first)
        _acc_out(dbias_ref, dbias, first)
        _acc_out(dg_ref, dg, first)
        _acc_out(db_ref, db, first)

        @pl.when(pl.program_id(0) == nb - 1)
        def _():
            tot = dbias_ref[...]
            head = lax.broadcasted_iota(jnp.int32, (CH, GM_W), 1) // (GM_W // GM_HEADS)
            out = jnp.zeros((CH, 128), F32)
            for h in range(GM_HEADS):
                s = jnp.sum(jnp.where(head == h, tot, 0.0), axis=1, keepdims=True)
                out = jnp.where(lane == h, s, out)
            dbs_ref[...] = out

    return pl.pallas_call(
        body, grid=(nb,),
        in_specs=[pl.BlockSpec((tm, GM_W), lambda i: (i, 0)), pl.BlockSpec((tm, GM_W), lambda i: (i, 1)),
                  pl.BlockSpec((tm, GM_W), lambda i: (i, 0)),
                  _full((1, GM_W)), _full((1, GM_W)), _full((GM_HEADS, CH, CH)), _full((GM_HEADS, CH, CH)),
                  _full((CH, GM_W))],
        out_specs=[pl.BlockSpec((tm, 2 * GM_W), lambda i: (i, 0)), _full((GM_HEADS, CH, CH)), _full((CH, 128)),
                   _full((1, GM_W)), _full((1, GM_W))],
        out_shape=[S((t, 2 * GM_W), ACT_DTYPE), S((GM_HEADS, CH, CH), F32), S((CH, 128), F32),
                   S((1, GM_W), F32), S((1, GM_W), F32)],
        scratch_shapes=[pltpu.VMEM((CH, GM_W), F32)],
        name=name, compiler_params=_cp("arbitrary"))(proj, proj, dy, ln_g, ln_b, ws, wst, bias)


def _rot(x, cos2, sin2):
    return x * cos2 + pltpu.roll(x, HD // 2, 1) * sin2


def _rot_bwd(dx, cos2, sin2):
    return dx * cos2 + pltpu.roll(dx * sin2, HD // 2, 1)


Q_COL, K_COL, V_COL, GATE_COL = 1, 2, 3, 4


def _rotate_qk(o_ref, cos2, sin2):
    for col, scale in ((Q_COL, 1.0), (K_COL, HD ** -0.5)):
        for h in range(HEADS):
            cols = slice(col * RET_W + h * HD, col * RET_W + (h + 1) * HD)
            o_ref[:, cols] = _rot(o_ref[:, cols], cos2, sin2) * scale


def _ret_scan(lhs, lhs_col, rhs, rhs_col, lp, ls, gp, gs, name):
    t = lhs.shape[0]
    n = t // CH
    r = 4 if n % 4 == 0 else 1
    ns = n // r

    def body(lp_ref, ls_ref, gp_ref, gs_ref, l1_ref, r1_ref, l2_ref, r2_ref, pre_ref, suf_ref, sp_ref, ss_ref):
        @pl.when(pl.program_id(0) == 0)
        def _():
            sp_ref[...] = jnp.zeros_like(sp_ref)
            ss_ref[...] = jnp.zeros_like(ss_ref)

        def kv(l_ref, r_ref, scale, rows):
            lv = l_ref[rows, :] * scale
            rv = r_ref[rows, :]
            return jnp.concatenate([_dot_tn(lv[:, h * HD:(h + 1) * HD], rv[:, h * HD:(h + 1) * HD])
                                    for h in range(HEADS)], axis=1)

        for j in range(r):
            pre_ref[j] = sp_ref[...]
            sp_ref[...] = sp_ref[...] * gp_ref[...] + kv(l1_ref, r1_ref, lp_ref[...], slice(j * CH, (j + 1) * CH))
        for j in reversed(range(r)):
            suf_ref[j] = ss_ref[...]
            ss_ref[...] = ss_ref[...] * gs_ref[...] + kv(l2_ref, r2_ref, ls_ref[...], slice(j * CH, (j + 1) * CH))

    w = HEADS * HD
    return pl.pallas_call(
        body, grid=(ns,),
        in_specs=[_full((CH, w)), _full((CH, w)), _full((1, w)), _full((1, w)),
                  pl.BlockSpec((r * CH, w), lambda s: (s, lhs_col)), pl.BlockSpec((r * CH, w), lambda s: (s, rhs_col)),
                  pl.BlockSpec((r * CH, w), lambda s: (ns - 1 - s, lhs_col)),
                  pl.BlockSpec((r * CH, w), lambda s: (ns - 1 - s, rhs_col))],
        out_specs=[pl.BlockSpec((r, HD, w), lambda s: (s, 0, 0)), pl.BlockSpec((r, HD, w), lambda s: (ns - 1 - s, 0, 0))],
        out_shape=[S((n, HD, w), F32)] * 2, name=name,
        scratch_shapes=[pltpu.VMEM((HD, w), F32), pltpu.VMEM((HD, w), F32)],
        compiler_params=_cp("arbitrary"))(lp, ls, gp, gs, lhs, rhs, lhs, rhs)


def _ret_out(proj, sf, sb, tb, name, riders=()):
    t = proj.shape[0]
    r = 4 if (t // CH) % 4 == 0 else 1
    tm = r * CH
    w = HEADS * HD

    def body(rq_ref, rk_ref, v_ref, g_ref, sf_ref, sb_ref, dm_ref, xif_ref, xib_ref, a_ref, y_ref):
        for c in range(r):
            rows = slice(c * CH, (c + 1) * CH)
            for h in range(HEADS):
                cols = slice(h * HD, (h + 1) * HD)
                q = rq_ref[rows, cols]
                p = _dot_nt(q, rk_ref[rows, cols]) * dm_ref[h]
                a = (_dot(p, v_ref[rows, cols]) + _dot(q * xif_ref[:, cols], sf_ref[c, :, cols])
                     + _dot(q * xib_ref[:, cols], sb_ref[c, :, cols]))
                a_ref[rows, cols] = a
                o, _ = _standardize(a)
                gv = g_ref[rows, cols]
                y_ref[rows, cols] = (o * (gv * _sigmoid(gv))).astype(y_ref.dtype)

    (a, y), rid = _pcall(
        body, [proj, proj, proj, proj, sf, sb, tb["dm"], tb["xif"], tb["xib"]], riders, grid=(t // tm,),
        in_specs=[pl.BlockSpec((tm, w), lambda i: (i, Q_COL)), pl.BlockSpec((tm, w), lambda i: (i, K_COL)),
                  pl.BlockSpec((tm, w), lambda i: (i, V_COL)), pl.BlockSpec((tm, w), lambda i: (i, GATE_COL)),
                  pl.BlockSpec((r, HD, w), lambda i: (i, 0, 0)), pl.BlockSpec((r, HD, w), lambda i: (i, 0, 0)),
                  _full((HEADS, CH, CH)), _full((CH, w)), _full((CH, w))],
        out_specs=[pl.BlockSpec((tm, w), lambda i: (i, 0))] * 2,
        out_shape=[S((t, w), F32), S((t, w), ACT_DTYPE)], name=name, sem=("parallel",))
    return a, y, rid


def _ret_bwd_pre(dy, a, proj, name):
    t = dy.shape[0]
    tm = _row_tile(t, 512)
    w = HEADS * HD

    def body(dy_ref, a_ref, g_ref, da_ref, dg_ref):
        for h in range(HEADS):
            cols = slice(h * HD, (h + 1) * HD)
            o, r = _standardize(a_ref[:, cols])
            gv = g_ref[:, cols]
            s = _sigmoid(gv)
            dyv = dy_ref[:, cols]
            dg_ref[:, cols] = (dyv * o * (s * (1.0 + gv * (1.0 - s)))).astype(dg_ref.dtype)
            da_ref[:, cols] = _standardize_bwd(dyv * (gv * s), o, r).astype(da_ref.dtype)

    return pl.pallas_call(
        body, grid=(t // tm,),
        in_specs=[pl.BlockSpec((tm, w), lambda i: (i, 0)), pl.BlockSpec((tm, w), lambda i: (i, 0)),
                  pl.BlockSpec((tm, w), lambda i: (i, GATE_COL))],
        out_specs=[pl.BlockSpec((tm, w), lambda i: (i, 0))] * 2,
        out_shape=[S((t, w), ACT_DTYPE)] * 2, name=name, compiler_params=_cp("parallel"))(dy, a, proj)


def _ret_bwd_main(proj, da, sf, sb, gf, gb, tb, name, riders=()):
    t = proj.shape[0]
    r = 4 if (t // CH) % 4 == 0 else 1
    tm = r * CH
    w = HEADS * HD
    scale = HD ** -0.5

    def body(rq_ref, rk_ref, v_ref, da_ref, sf_ref, sb_ref, gf_ref, gb_ref, dm_ref, dmt_ref,
             xif_ref, xib_ref, zf_ref, zb_ref, c_ref, s_ref, o_ref):
        for c in range(r):
            rows = slice(c * CH, (c + 1) * CH)
            cos2, sin2 = c_ref[rows, :], s_ref[rows, :]
            for h in range(HEADS):
                cols = slice(h * HD, (h + 1) * HD)
                q, k, v, dav = rq_ref[rows, cols], rk_ref[rows, cols], v_ref[rows, cols], da_ref[rows, cols]
                qm, km, vm, dam = _mx(q), _mx(k), _mx(v), _mx(dav)
                dm, dmt = dm_ref[h], dmt_ref[h]
                pt = _dot_nt(km, qm) * dmt
                dp = _dot_nt(dam, vm) * dm
                dpt = _dot_nt(vm, dam) * dmt
                sfh, sbh, gfh, gbh = sf_ref[c, :, cols], sb_ref[c, :, cols], gf_ref[c, :, cols], gb_ref[c, :, cols]
                zf, zb = zf_ref[:, cols], zb_ref[:, cols]
                dv = _dot(pt, dam) + zf * _dot(km, gfh) + zb * _dot(km, gbh)
                drq = _dot(dp, km) + xif_ref[:, cols] * _dot_nt(dam, sfh) + xib_ref[:, cols] * _dot_nt(dam, sbh)
                drk = _dot(dpt, qm) + _dot_nt(zf * v, gfh) + _dot_nt(zb * v, gbh)
                o_ref[rows, h * HD:(h + 1) * HD] = _rot_bwd(drq, cos2, sin2).astype(o_ref.dtype)
                o_ref[rows, w + h * HD:w + (h + 1) * HD] = (_rot_bwd(drk, cos2, sin2) * scale).astype(o_ref.dtype)
                o_ref[rows, 2 * w + h * HD:2 * w + (h + 1) * HD] = dv.astype(o_ref.dtype)

    st = pl.BlockSpec((r, HD, w), lambda i: (i, 0, 0))
    (out,), rid = _pcall(
        body, [proj, proj, proj, da, sf, sb, gf, gb, tb["dm"], tb["dmt"], tb["xif"], tb["xib"], tb["zf"], tb["zb"],
               tb["cos2"], tb["sin2"]], riders, grid=(t // tm,),
        in_specs=[pl.BlockSpec((tm, w), lambda i: (i, Q_COL)), pl.BlockSpec((tm, w), lambda i: (i, K_COL)),
                  pl.BlockSpec((tm, w), lambda i: (i, V_COL)), pl.BlockSpec((tm, w), lambda i: (i, 0)),
                  st, st, st, st, _full((HEADS, CH, CH)), _full((HEADS, CH, CH)),
                  _full((CH, w)), _full((CH, w)), _full((CH, w)), _full((CH, w)),
                  pl.BlockSpec((tm, HD), lambda i: (i, 0)), pl.BlockSpec((tm, HD), lambda i: (i, 0))],
        out_specs=[pl.BlockSpec((tm, 3 * w), lambda i: (i, 0))],
        out_shape=[S((t, 3 * w), ACT_DTYPE)], name=name, sem=("parallel",))
    return out, rid


CONV_TM = 256
CONV_SUB = 64
A_COL = (2 * GM_W + 4 * RET_W) // CV_W
G_COL = A_COL + 1


def _halo_specs(t, tm, col):
    nb16 = t // HALO
    per = tm // HALO
    return [pl.BlockSpec((tm, CV_W), lambda i: (i, col)),
            pl.BlockSpec((HALO, CV_W), lambda i: (jnp.maximum(i * per - 1, 0), col)),
            pl.BlockSpec((HALO, CV_W), lambda i: (jnp.minimum((i + 1) * per, nb16 - 1), col))]


def _fill_padded(dst_ref, prev, main, nxt, tm, i, nb):
    dst_ref[0:HALO, :] = jnp.where(i > 0, prev, 0.0)
    dst_ref[HALO:HALO + tm, :] = main
    dst_ref[HALO + tm:2 * HALO + tm, :] = jnp.where(i < nb - 1, nxt, 0.0)


SUBLANES = 8


def _fill_shifted(sh_ref, src_ref, tm):
    n = tm + 2 * HALO - SUBLANES
    for b in range(SUBLANES):
        sh_ref[b, 0:n, :] = src_ref[pl.ds(b, n), :]


def _tap(sh_ref, off, rows):
    return sh_ref[off % SUBLANES, pl.ds(off - off % SUBLANES, rows), :]


def _conv_fwd(proj, cw, cb, ln_g, ln_b, name, riders=()):
    t = proj.shape[0]
    tm = _row_tile(t, CONV_TM)
    nb = t // tm

    def body(a_ref, ap_ref, an_ref, g_ref, gp_ref, gn_ref, w_ref, b_ref, lg_ref, lb_ref, y_ref, hc_ref,
             hp_ref, sh_ref):
        i = pl.program_id(0)
        _fill_padded(hp_ref, ap_ref[...] * _sigmoid(gp_ref[...]), a_ref[...] * _sigmoid(g_ref[...]),
                     an_ref[...] * _sigmoid(gn_ref[...]), tm, i, nb)
        _fill_shifted(sh_ref, hp_ref, tm)
        for sb in range(tm // CONV_SUB):
            acc = jnp.zeros((CONV_SUB, CV_W), F32) + b_ref[...]
            for k in range(KCONV):
                acc = acc + w_ref[k:k + 1, :] * _tap(sh_ref, sb * CONV_SUB + k + 1, CONV_SUB)
            rows = slice(sb * CONV_SUB, (sb + 1) * CONV_SUB)
            hc_ref[rows, :] = acc
            o, _ = _standardize(acc)
            z = o * lg_ref[...] + lb_ref[...]
            y_ref[rows, :] = (z * _sigmoid(z)).astype(y_ref.dtype)

    (y, hc), rid = _pcall(
        body, [proj, proj, proj, proj, proj, proj, cw, cb, ln_g, ln_b], riders, grid=(nb,),
        in_specs=_halo_specs(t, tm, A_COL) + _halo_specs(t, tm, G_COL)
        + [_full((32, CV_W)), _full((1, CV_W)), _full((1, CV_W)), _full((1, CV_W))],
        out_specs=[pl.BlockSpec((tm, CV_W), lambda i: (i, 0))] * 2,
        out_shape=[S((t, CV_W), ACT_DTYPE), S((t, CV_W), F32)], name=name, sem=("parallel",),
        scratch_shapes=[pltpu.VMEM((tm + 2 * HALO, CV_W), F32), pltpu.VMEM((SUBLANES, tm + 2 * HALO, CV_W), F32)])
    return y, hc, rid


def _conv_bwd(proj, dy, hc, cw, ln_g, ln_b, name, riders=()):
    t = proj.shape[0]
    tm = _row_tile(t, CONV_TM)
    nb = t // tm

    def body(a_ref, ap_ref, an_ref, g_ref, gp_ref, gn_ref, dy_ref, dyp_ref, dyn_ref, hc_ref, hcp_ref, hcn_ref,
             w_ref, lg_ref, lb_ref, d_ref, dw_ref, dcb_ref, dlg_ref, dlb_ref, hp_ref, dhp_ref, dwacc_ref,
             sh_ref, dsh_ref):
        i = pl.program_id(0)
        first = i == 0

        def dhc_of(dyv, hcv):
            o, r = _standardize(hcv)
            z = o * lg_ref[...] + lb_ref[...]
            s = _sigmoid(z)
            dz = dyv * (s * (1.0 + z * (1.0 - s)))
            return _standardize_bwd(dz * lg_ref[...], o, r), dz, o

        dhc, dz, o = dhc_of(dy_ref[...], hc_ref[...])
        _acc_out(dlg_ref, jnp.sum(dz * o, axis=0, keepdims=True), first)
        _acc_out(dlb_ref, jnp.sum(dz, axis=0, keepdims=True), first)
        _acc_out(dcb_ref, jnp.sum(dhc, axis=0, keepdims=True), first)
        _fill_padded(dhp_ref, dhc_of(dyp_ref[...], hcp_ref[...])[0], dhc, dhc_of(dyn_ref[...], hcn_ref[...])[0],
                     tm, i, nb)
        _fill_padded(hp_ref, ap_ref[...] * _sigmoid(gp_ref[...]), a_ref[...] * _sigmoid(g_ref[...]),
                     an_ref[...] * _sigmoid(gn_ref[...]), tm, i, nb)

        _fill_shifted(sh_ref, hp_ref, tm)
        _fill_shifted(dsh_ref, dhp_ref, tm)

        @pl.when(first)
        def _():
            dwacc_ref[...] = jnp.zeros_like(dwacc_ref)

        for sb in range(tm // CONV_SUB):
            base = sb * CONV_SUB
            dmain = dhp_ref[pl.ds(HALO + base, CONV_SUB), :]
            dh = jnp.zeros((CONV_SUB, CV_W), F32)
            for k in range(KCONV):
                dh = dh + w_ref[k:k + 1, :] * _tap(dsh_ref, base + 2 * HALO - 1 - k, CONV_SUB)
                prod = dmain * _tap(sh_ref, base + k + 1, CONV_SUB)
                dwacc_ref[k * 8:(k + 1) * 8, :] += jnp.sum(prod.reshape(CONV_SUB // 8, 8, CV_W), axis=0)
            rows = slice(base, base + CONV_SUB)
            s = _sigmoid(g_ref[rows, :])
            d_ref[rows, :CV_W] = (dh * s).astype(d_ref.dtype)
            d_ref[rows, CV_W:] = (dh * a_ref[rows, :] * (s * (1.0 - s))).astype(d_ref.dtype)

        @pl.when(i == nb - 1)
        def _():
            for k in range(KCONV):
                dw_ref[k:k + 1, :] = jnp.sum(dwacc_ref[k * 8:(k + 1) * 8, :], axis=0, keepdims=True)
            dw_ref[KCONV:32, :] = jnp.zeros((32 - KCONV, CV_W), F32)

    hs = [pl.BlockSpec((tm, CV_W), lambda i: (i, 0)),
          pl.BlockSpec((HALO, CV_W), lambda i: (jnp.maximum(i * (tm // HALO) - 1, 0), 0)),
          pl.BlockSpec((HALO, CV_W), lambda i: (jnp.minimum((i + 1) * (tm // HALO), t // HALO - 1), 0))]
    outs, rid = _pcall(
        body, [proj, proj, proj, proj, proj, proj, dy, dy, dy, hc, hc, hc, cw, ln_g, ln_b], riders, grid=(nb,),
        in_specs=_halo_specs(t, tm, A_COL) + _halo_specs(t, tm, G_COL) + hs + hs
        + [_full((32, CV_W)), _full((1, CV_W)), _full((1, CV_W))],
        out_specs=[pl.BlockSpec((tm, 2 * CV_W), lambda i: (i, 0)), _full((32, CV_W)), _full((1, CV_W)),
                   _full((1, CV_W)), _full((1, CV_W))],
        out_shape=[S((t, 2 * CV_W), ACT_DTYPE), S((32, CV_W), F32), S((1, CV_W), F32), S((1, CV_W), F32),
                   S((1, CV_W), F32)],
        name=name, sem=("arbitrary",),
        scratch_shapes=[pltpu.VMEM((tm + 2 * HALO, CV_W), F32), pltpu.VMEM((tm + 2 * HALO, CV_W), F32),
                        pltpu.VMEM((32 * 8, CV_W), F32), pltpu.VMEM((SUBLANES, tm + 2 * HALO, CV_W), F32),
                        pltpu.VMEM((SUBLANES, tm + 2 * HALO, CV_W), F32)])
    return (*outs, rid)


def _loss_head(x, g, target, name):
    t = x.shape[0]
    tm = _row_tile(t, 512)

    def body(x_ref, g_ref, t_ref, dx_ref, dg_ref, l_ref):
        first = pl.program_id(0) == 0
        xv = x_ref[...]
        r = _rms_r(xv)
        e = xv * r * g_ref[...] - t_ref[...]
        dx, dgrow = _rms_bwd(e * (1.0 / D), xv, r, g_ref[...])
        dx_ref[...] = dx
        _acc_out(dg_ref, jnp.sum(dgrow, axis=0, keepdims=True), first)
        part = 0.5 * jnp.sum(jnp.mean(e * e, axis=-1, keepdims=True), axis=0, keepdims=True)
        _acc_out(l_ref, jnp.broadcast_to(part, (8, 128)), first)

    return pl.pallas_call(
        body, grid=(t // tm,),
        in_specs=[pl.BlockSpec((tm, D), lambda i: (i, 0)), _full((1, D)), pl.BlockSpec((tm, D), lambda i: (i, 0))],
        out_specs=[pl.BlockSpec((tm, D), lambda i: (i, 0)), _full((1, D)), _full((8, 128))],
        out_shape=[S((t, D), F32), S((1, D), F32), S((8, 128), F32)], name=name,
        compiler_params=_cp("arbitrary"))(x, g, target)


def _as2d(a):
    return a.reshape(-1, a.shape[-1])


def _ew_tile(rows, cols, n_arrays):
    budget = VMEM_LIMIT // 2
    tr = rows
    while tr * cols * 4 * n_arrays * 2 > budget and tr % 16 == 0:
        tr //= 2
    assert rows % tr == 0
    return tr


def _adamw(w, g, m, v, name):
    shape = w.shape
    w2, g2, m2, v2 = _as2d(w), _as2d(g), _as2d(m), _as2d(v)
    rows, cols = w2.shape
    tr = _ew_tile(rows, cols, 7)

    def body(w_ref, g_ref, m_ref, v_ref, d_ref, nm_ref, nv_ref):
        gv = g_ref[...]
        nm = ADAM_B1 * m_ref[...] + (1.0 - ADAM_B1) * gv
        nv = ADAM_B2 * v_ref[...] + (1.0 - ADAM_B2) * (gv * gv)
        m_hat = nm / (1.0 - ADAM_B1 ** ADAM_STEP)
        v_hat = nv / (1.0 - ADAM_B2 ** ADAM_STEP)
        d_ref[...] = -ADAM_LR * (m_hat / (jnp.sqrt(v_hat) + ADAM_EPS) + ADAM_WD * w_ref[...])
        nm_ref[...] = nm
        nv_ref[...] = nv

    spec = pl.BlockSpec((tr, cols), lambda i: (i, 0))
    outs = pl.pallas_call(body, grid=(rows // tr,), in_specs=[spec] * 4, out_specs=[spec] * 3,
                          out_shape=[S((rows, cols), F32)] * 3, name=name,
                          compiler_params=_cp("parallel"))(w2, g2, m2, v2)
    return tuple(o.reshape(shape) for o in outs)


BIG = (("w_in", "col"), ("w_out", "row"), ("w_ffn_in", "col"), ("w_ffn_out", "row"))
NBIG = len(BIG)


def _cast_to_gathered(w, l, me, name):
    _, r_, c_ = w.shape
    tr = _ew_tile(r_, c_, 2)

    def body(me_ref, w_ref, o_ref):
        o_ref[...] = w_ref[...].astype(o_ref.dtype)

    gs = pltpu.PrefetchScalarGridSpec(
        num_scalar_prefetch=1, grid=(r_ // tr,),
        in_specs=[pl.BlockSpec((None, tr, c_), lambda i, s: (l, i, 0))],
        out_specs=pl.BlockSpec((None, tr, c_), lambda i, s: (s[0], i, 0)))
    out = pl.pallas_call(body, grid_spec=gs, out_shape=S((N_CHIPS, r_, c_), MXU_DTYPE), name=name,
                         compiler_params=_cp("parallel"))(me.reshape(1), w)
    return out.reshape(N_CHIPS, 2, r_ // 2, c_)


def _all_gather(bufs, name, per_core=False):
    n = len(bufs)

    def body(*refs):
        i_refs, o_refs = refs[:n], refs[n:2 * n]
        isend, irecv, dsend, drecv, osend, orecv = refs[2 * n:]
        pos = _mesh_pos()
        x, y, c, me, _, _ = pos
        ici = _rider_copies("ici", i_refs, o_refs, isend, irecv, pos)
        d2d = _rider_copies("d2d", o_refs, o_refs, dsend, drecv, pos)
        own = []
        if per_core:
            for b in range(n):
                own.append(tuple(pltpu.make_async_remote_copy(
                    src_ref=s_, dst_ref=d_, send_sem=osend.at[b], recv_sem=orecv.at[b],
                    device_id=(x, y, 1 - c), device_id_type=MESH)
                    for s_, d_ in ((i_refs[b].at[me, c], o_refs[b].at[me, c]),
                                   (o_refs[b].at[me, 1 - c], o_refs[b].at[me, 1 - c]))))
        for cp, _ in ici + own:
            cp.start()
        for (_, land), (fwd, _) in zip(ici, d2d):
            land.wait_recv()
            fwd.start()
        for _, land in d2d + own:
            land.wait_recv()
        for cp, _ in ici + d2d + own:
            cp.wait_send()

    return pl.pallas_call(
        body, in_specs=[ANY] * n, out_specs=[ANY] * n, out_shape=[S(a.shape, a.dtype) for a in bufs],
        input_output_aliases={w: w for w in range(n)}, name=name,
        scratch_shapes=[pltpu.SemaphoreType.DMA((n, 3))] * 4 + [pltpu.SemaphoreType.DMA((n,))] * 2)(*bufs)


def _pair_exchange(grads, name):
    n = len(grads)

    def body(*refs):
        g_refs, theirs = refs[:n], refs[n:2 * n]
        send, recv = refs[2 * n:]
        x, y, c, *_ = _mesh_pos()
        cps = []
        for w in range(n):
            cp = pltpu.make_async_remote_copy(
                src_ref=g_refs[w].at[:, 1 - c], dst_ref=theirs[w], send_sem=send.at[w], recv_sem=recv.at[w],
                device_id=(x, y, 1 - c), device_id_type=MESH)
            cp.start()
            cps.append(cp)
        for cp in cps:
            cp.wait()

    return pl.pallas_call(
        body, in_specs=[ANY] * n, out_specs=[ANY] * n,
        out_shape=[S(a.shape[:1] + a.shape[2:], a.dtype) for a in grads], name=name,
        scratch_shapes=[pltpu.SemaphoreType.DMA((n,))] * 2)(*grads)


def _pair_sum(g, theirs, core, name):
    _, _, rh, c_ = g.shape
    tr = _ew_tile(rh, c_, 2)

    def body(s_ref, g_ref, t_ref, o_ref):
        o_ref[...] = (g_ref[...].astype(F32) + t_ref[...].astype(F32)).astype(o_ref.dtype)

    blk = pl.BlockSpec((None, tr, c_), lambda j, i, s: (j, i, 0))
    gs = pltpu.PrefetchScalarGridSpec(
        num_scalar_prefetch=1, grid=(N_CHIPS, rh // tr),
        in_specs=[pl.BlockSpec((None, None, tr, c_), lambda j, i, s: (j, s[0], i, 0)), blk], out_specs=blk)
    return pl.pallas_call(body, grid_spec=gs, out_shape=S(theirs.shape, theirs.dtype), name=name,
                          compiler_params=_cp("parallel", "parallel"))(core.reshape(1), g, theirs)


def _chip_sum(q, got, l, me, core, into, name):
    _, rh, c_ = got.shape
    tr = _ew_tile(rh, c_, 4)

    def body(s_ref, q_ref, g0_ref, g1_ref, g2_ref, o_ref):
        acc = q_ref[...].astype(F32)
        for r in (g0_ref, g1_ref, g2_ref):
            acc = acc + r[...].astype(F32)
        o_ref[...] = acc

    in_specs = [pl.BlockSpec((None, tr, c_), lambda i, s: (s[0], i, 0))] + [
        pl.BlockSpec((None, tr, c_), functools.partial(lambda k, i, s: (k, i, 0), k)) for k in range(3)]
    return _call_into(
        body, into, in_specs, [jnp.stack([me, core]), q, got, got, got], n_prefetch=1, grid=(rh // tr,),
        out_specs=pl.BlockSpec((None, None, tr, c_), lambda i, s: (l, s[1], i, 0)),
        out_shape=S((DEPTH, 2, rh, c_), F32), name=name, compiler_params=_cp("parallel"))


def _pair_gather(gs4):
    def body(*refs):
        i_refs, o_refs = refs[:NBIG], refs[NBIG:2 * NBIG]
        send, recv = refs[2 * NBIG:]
        x, y, c, *_ = _mesh_pos()
        cps = []
        for w in range(NBIG):
            cp = pltpu.make_async_remote_copy(
                src_ref=i_refs[w].at[:, c], dst_ref=o_refs[w].at[:, c], send_sem=send.at[w], recv_sem=recv.at[w],
                device_id=(x, y, 1 - c), device_id_type=MESH)
            cp.start()
            cps.append(cp)
        for cp in cps:
            cp.wait()

    outs = pl.pallas_call(
        body, in_specs=[ANY] * NBIG, out_specs=[ANY] * NBIG, out_shape=[S(a.shape, a.dtype) for a in gs4],
        input_output_aliases={w: w for w in range(NBIG)}, name="grad_pair_gather",
        scratch_shapes=[pltpu.SemaphoreType.DMA((NBIG,))] * 2)(*gs4)
    return [o.reshape(o.shape[0], 2 * o.shape[2], o.shape[3]) for o in outs]


def _all_reduce_small(p, me, core, name):
    rows = p.shape[0]

    def place(s_ref, p_ref, o_ref):
        o_ref[...] = p_ref[...]

    gs = pltpu.PrefetchScalarGridSpec(
        num_scalar_prefetch=1, grid=(1,), in_specs=[pl.BlockSpec((rows, 128), lambda i, s: (0, 0))],
        out_specs=pl.BlockSpec((None, None, rows, 128), lambda i, s: (s[0], s[1], 0, 0)))
    mine = pl.pallas_call(place, grid_spec=gs, out_shape=S((N_CHIPS, 2, rows, 128), F32), name=name + "_place",
                          compiler_params=_cp("arbitrary"))(jnp.stack([me, core]), p)
    parts = _all_gather([mine], name + "_gather", per_core=True)[0]

    def total(g_ref, o_ref):
        acc = g_ref[0, 0]
        for j in range(N_CHIPS):
            for c in range(2):
                if (j, c) != (0, 0):
                    acc = acc + g_ref[j, c]
        o_ref[...] = acc

    vm = pl.BlockSpec(memory_space=pltpu.VMEM)
    return pl.pallas_call(total, in_specs=[vm], out_specs=vm, out_shape=S((rows, 128), F32), name=name + "_sum",
                          compiler_params=pltpu.CompilerParams(vmem_limit_bytes=VMEM_LIMIT))(parts)


PACK_UNIT = 8 * 128


def _pack(arrs):
    parts = []
    for a in arrs:
        flat = a.reshape(-1)
        pad = (-flat.shape[0]) % PACK_UNIT
        parts.append(jnp.pad(flat, (0, pad)).reshape(-1, 128))
    return jnp.concatenate(parts, axis=0)


def _unpack(buf, shapes):
    outs, row = [], 0
    for shp in shapes:
        n = int(np.prod(shp))
        rows = -(-n // PACK_UNIT) * 8
        outs.append(buf[row:row + rows].reshape(-1)[:n].reshape(shp))
        row += rows
    return outs


SMALL = ("norm1_g", "gm_ln_g", "gm_ln_b", "gm_ws", "gm_bs", "conv_w", "conv_b", "conv_ln_g", "conv_ln_b",
         "norm2_g", "final_g")
WEIGHTS = ("norm1_g", "w_in", "gm_ln_g", "gm_ln_b", "gm_ws", "gm_bs", "conv_w", "conv_b", "conv_ln_g",
           "conv_ln_b", "w_out", "norm2_g", "w_ffn_in", "w_ffn_out", "final_g")


def kernel(x, norm1_g, w_in, gm_ln_g, gm_ln_b, gm_ws, gm_bs, conv_w, conv_b, conv_ln_g, conv_ln_b, w_out, norm2_g, w_ffn_in, w_ffn_out, final_g, loss_target, m_norm1_g, m_w_in, m_gm_ln_g, m_gm_ln_b, m_gm_ws, m_gm_bs, m_conv_w, m_conv_b, m_conv_ln_g, m_conv_ln_b, m_w_out, m_norm2_g, m_w_ffn_in, m_w_ffn_out, m_final_g, v_norm1_g, v_w_in, v_gm_ln_g, v_gm_ln_b, v_gm_ws, v_gm_bs, v_conv_w, v_conv_b, v_conv_ln_g, v_conv_ln_b, v_w_out, v_norm2_g, v_w_ffn_in, v_w_ffn_out, v_final_g):
    given = dict(locals())
    t = x.shape[1]
    xc = x.reshape(t, D)
    target = loss_target.reshape(t, D)
    me = 2 * lax.axis_index("x") + lax.axis_index("y")
    core = lax.axis_index("c")
    tb = _tables(t)

    me = me.astype(jnp.int32)
    core = core.astype(jnp.int32)
    names = [n for n, _ in BIG]
    kinds = dict(BIG)
    gathered = [{n: _cast_to_gathered(given[n], l, me, f"cast_{n}{l}") for n in names} for l in range(DEPTH)]
    gathered[0]["w_in"] = _all_gather([gathered[0]["w_in"]], "all_gather_w_in0")[0]

    def weight(l, n):
        b = gathered[l][n]
        r_, c_ = 2 * b.shape[2], b.shape[3]
        return b.reshape(N_CHIPS, r_, c_) if kinds[n] == "col" else b.reshape(N_CHIPS * r_, c_)

    cshard = CV_W // N_CHIPS
    placed = lax.dynamic_update_slice(jnp.zeros((DEPTH, KCONV, CV_W), F32),
                                      conv_w * (core == 0).astype(F32), (0, 0, me * cshard))
    conv_w_full = _unpack(_all_reduce_small(_pack([placed]), me, core, "gather_conv_w"), [(DEPTH, KCONV, CV_W)])[0]
    cw32 = jnp.pad(conv_w_full, ((0, 0), (0, 32 - KCONV), (0, 0)))

    def row(a, l):
        return a[l].reshape(1, -1)

    saved = []
    early = ["w_in", "w_out", "w_ffn_in"]
    for l in range(DEPTH):
        cur = gathered[l]
        nxt = gathered[l + 1] if l + 1 < DEPTH else None
        sv = {"x": xc}
        bias = jnp.repeat(gm_bs[l].T, GM_W // GM_HEADS, axis=1)
        first = ["w_ffn_in"] if l == 0 else ["w_ffn_out"]
        late = ["w_out", "w_ffn_out"]
        proj, rid = _norm_mm(xc, row(norm1_g, l), weight(l, "w_in"), F32, f"in_proj{l}", 512,
                             [("ici" if l == 0 else "d2d", [cur[n] for n in first])], (tb["cos2"], tb["sin2"]))
        cur.update(zip(first, rid))
        y_gm, rid = _gm_fwd(proj, row(gm_ln_g, l), row(gm_ln_b, l), gm_ws[l], bias, f"gm_fwd{l}",
                            [("d2d", [cur[n] for n in first])] if l == 0 else ())
        cur.update(zip(first, rid))
        sf, sb = _ret_scan(proj, K_COL, proj, V_COL, tb["zf"], tb["zb"], tb["gcf"], tb["gcb"], f"ret_state{l}")
        a, y_ret, rid = _ret_out(proj, sf, sb, tb, f"ret_out{l}",
                                 [("ici", [cur[n] for n in late])] if l == 0 else ())
        cur.update(zip(late, rid))
        y_cv, hc, rid = _conv_fwd(proj, cw32[l], row(conv_b, l), row(conv_ln_g, l), row(conv_ln_b, l),
                                  f"conv_fwd{l}", [("d2d", [cur[n] for n in late])] if l == 0 else ())
        cur.update(zip(late, rid))
        x_mid = _parts_mm_res([y_gm, y_ret, y_cv], weight(l, "w_out"), xc, f"out_proj{l}")
        ff, rid = _norm_mm(x_mid, row(norm2_g, l), weight(l, "w_ffn_in"), ACT_DTYPE, f"ffn_in{l}", 512,
                           [("ici", [nxt[n] for n in early])] if nxt else ())
        if nxt:
            nxt.update(zip(early, rid))
        xc, rid = _swiglu_mm_res(ff, weight(l, "w_ffn_out"), x_mid, f"ffn_out{l}",
                                 [("d2d", [nxt[n] for n in early]), ("ici", [nxt["w_ffn_out"]])] if nxt else ())
        if nxt:
            nxt.update(zip(early + ["w_ffn_out"], rid))
        sv.update(bias=bias, proj=proj, y_gm=y_gm, sf=sf, sb=sb, a=a, y_ret=y_ret, y_cv=y_cv, hc=hc, x_mid=x_mid,
                  ff=ff)
        saved.append(sv)

    dx, d_final_g, lpart = _loss_head(xc, final_g.reshape(1, D), target, "loss_head")

    small_g = {n: [None] * DEPTH for n in SMALL}
    qs = [{} for _ in range(DEPTH)]
    got = [{} for _ in range(DEPTH)]
    ffn_w, mix_w = ["w_ffn_out", "w_ffn_in"], ["w_out", "w_in"]

    def halves(big_g, group):
        return [big_g[n].reshape(N_CHIPS, 2, given[n].shape[1] // 2, given[n].shape[2]) for n in group]

    def pair_sums(l, group, g4, theirs):
        qs[l].update({n: _pair_sum(g, th, core, f"pair_sum_{n}{l}") for n, g, th in zip(group, g4, theirs)})
        return [qs[l][n] for n in group]

    for l in reversed(range(DEPTH)):
        sv = saved[l]
        proj = sv["proj"]
        big_g = {}
        dff = _dx_swiglu(dx, weight(l, "w_ffn_out"), sv["ff"], f"ffn_out_dx{l}")
        big_g["w_ffn_out"] = _dw_swiglu(sv["ff"], dx, f"ffn_out_dw{l}")
        dx_mid, dg2, _ = _dx_norm([dff], weight(l, "w_ffn_in"), sv["x_mid"], row(norm2_g, l), dx,
                                  f"ffn_in_dx{l}", 512)
        big_g["w_ffn_in"], _ = _dw_norm_cols(sv["x_mid"], row(norm2_g, l), dff, w_ffn_in.shape[2], f"ffn_in_dw{l}")
        g4 = halves(big_g, ffn_w)
        (dy_gm, dy_ret, dy_cv), theirs = _dx_parts(dx_mid, weight(l, "w_out"), [GM_W, RET_W, CV_W],
                                                   f"out_proj_dx{l}", [("pairx", g4)])
        q_ffn = pair_sums(l, ffn_w, g4, theirs)
        big_g["w_out"] = _dw_parts([sv["y_gm"], sv["y_ret"], sv["y_cv"]], dx_mid, f"out_proj_dw{l}")
        d_cv, dcw, dcb, dclg, dclb, rid = _conv_bwd(proj, dy_cv, sv["hc"], cw32[l], row(conv_ln_g, l),
                                                    row(conv_ln_b, l), f"conv_bwd{l}", [("scatter", q_ffn[:1])])
        got[l].update(zip(ffn_w[:1], rid))
        da, d_g = _ret_bwd_pre(dy_ret, sv["a"], proj, f"ret_bwd_pre{l}")
        gb_, gf_ = _ret_scan(proj, Q_COL, da, 0, tb["xib"], tb["xif"], tb["gcb"], tb["gcf"], f"ret_bwd_state{l}")
        d_qkv, rid = _ret_bwd_main(proj, da, sv["sf"], sv["sb"], gf_, gb_, tb, f"ret_bwd_main{l}",
                                   [("scatter", q_ffn[1:])])
        got[l].update(zip(ffn_w[1:], rid))
        d_gm, dws, dbs, dglg, dglb = _gm_bwd(proj, dy_gm, row(gm_ln_g, l), row(gm_ln_b, l), gm_ws[l],
                                             jnp.swapaxes(gm_ws[l], 1, 2), sv["bias"], f"gm_bwd{l}")
        dparts = [d_gm, d_qkv, d_g, d_cv]
        big_g["w_in"] = _dw_norm_parts(sv["x"], row(norm1_g, l), dparts, w_in.shape[2], f"in_proj_dw{l}")
        g4 = halves(big_g, mix_w)
        q_mix = pair_sums(l, mix_w, g4, _pair_exchange(g4, f"grad_pair_exchange_mix{l}"))
        dx, dg1, rid = _dx_norm(dparts, weight(l, "w_in"), sv["x"], row(norm1_g, l), dx_mid, f"in_proj_dx{l}", 512,
                                [("scatter", q_mix)])
        got[l].update(zip(mix_w, rid))
        for n, val in (("norm1_g", dg1[0]), ("gm_ln_g", dglg[0]), ("gm_ln_b", dglb[0]), ("gm_ws", dws),
                       ("gm_bs", dbs[:, :GM_HEADS].T), ("conv_w", dcw[:KCONV]), ("conv_b", dcb[0]),
                       ("conv_ln_g", dclg[0]), ("conv_ln_b", dclb[0]), ("norm2_g", dg2[0])):
            small_g[n][l] = val

    small_shapes = [given[n].shape if n != "conv_w" else (DEPTH, KCONV, CV_W) for n in SMALL]
    partials = [d_final_g[0] if n == "final_g" else jnp.stack(small_g[n]) for n in SMALL]
    summed = _unpack(_all_reduce_small(_pack(partials + [lpart]), me, core, "all_reduce_small_grads"),
                     small_shapes + [lpart.shape])
    loss = summed[-1][0, 0]
    reduced = dict(zip(SMALL, summed))
    reduced["conv_w"] = lax.dynamic_slice(reduced["conv_w"], (0, 0, me * cshard), (DEPTH, KCONV, cshard))

    halves = [None] * NBIG
    for l in reversed(range(DEPTH)):
        halves = [_chip_sum(qs[l][n], got[l][n], l, me, core, h, f"chip_sum_{n}{l}") for n, h in zip(names, halves)]
    grads = dict(zip(names, _pair_gather(halves)))
    grads.update(reduced)

    delta, new_m, new_v = {}, {}, {}
    for n, _ in BIG:
        delta[n], new_m[n], new_v[n] = _adamw(given[n], grads[n], given["m_" + n], given["v_" + n], f"adamw_{n}")
    shapes = [given[n].shape for n in SMALL]
    packed = [_pack([src[n] if src is grads else src[p + n] for n in SMALL])
              for src, p in ((given, ""), (grads, ""), (given, "m_"), (given, "v_"))]
    outs = _adamw(*packed, "adamw_small")
    for dst, buf in zip((delta, new_m, new_v), outs):
        dst.update(zip(SMALL, _unpack(buf, shapes)))

    return (loss, dx.reshape(1, t, D), *[grads[n] for n in WEIGHTS], *[delta[n] for n in WEIGHTS],
            *[new_m[n] for n in WEIGHTS], *[new_v[n] for n in WEIGHTS])
```

```python
import functools
import math

import numpy as np
import jax
import jax.numpy as jnp
from jax import lax
from jax.experimental import pallas as pl
from jax.experimental.pallas import tpu as pltpu

F32 = jnp.float32
BF16 = jnp.bfloat16
MXU_DTYPE = BF16
ACT_DTYPE = BF16
S = jax.ShapeDtypeStruct

D = 1024
DEPTH = 2
GM_W = 256
GM_HEADS = 4
RET_W = 512
HEADS = 4
HD = 128
CV_W = 256
KCONV = 31
IN_W = 2 * GM_W + 4 * RET_W + 2 * CV_W
FFN_H = 2816
CH = 128
ROPE_BASE = 10000.0
EPS = 1e-6
N_CHIPS = 4
N_DEV = 8
HALO = 16

ADAM_LR = 0.001
ADAM_B1 = 0.9
ADAM_B2 = 0.999
ADAM_EPS = 1e-08
ADAM_WD = 0.01
ADAM_STEP = 10

VMEM_LIMIT = 52 * 1024 * 1024
MESH = pl.DeviceIdType.MESH


def _cp(*sem, vmem=VMEM_LIMIT):
    return pltpu.CompilerParams(dimension_semantics=tuple(sem), vmem_limit_bytes=vmem)


def _mx(a):
    return a.astype(MXU_DTYPE)


def _dot(a, b):
    return jnp.dot(_mx(a), _mx(b), preferred_element_type=F32)


def _dot_nt(a, b):
    return lax.dot_general(_mx(a), _mx(b), (((1,), (1,)), ((), ())), preferred_element_type=F32)


def _dot_tn(a, b):
    return lax.dot_general(_mx(a), _mx(b), (((0,), (0,)), ((), ())), preferred_element_type=F32)


def _sigmoid(x):
    return 1.0 / (1.0 + jnp.exp(-x))


def _gelu(x):
    return 0.5 * x * (1.0 + lax.erf(x * (1.0 / math.sqrt(2.0))))


def _gelu_grad(x):
    return 0.5 * (1.0 + lax.erf(x * (1.0 / math.sqrt(2.0)))) + x * jnp.exp(-0.5 * x * x) * (1.0 / math.sqrt(2.0 * math.pi))


def _rms_r(x):
    return lax.rsqrt(jnp.mean(x * x, axis=-1, keepdims=True) + EPS)


def _rms_bwd(dh, x, r, g):
    u = dh * g
    dx = r * u - x * (r * r * r) * jnp.mean(u * x, axis=-1, keepdims=True)
    return dx, dh * x * r


def _standardize(a):
    mu = jnp.mean(a, axis=-1, keepdims=True)
    d = a - mu
    r = lax.rsqrt(jnp.mean(d * d, axis=-1, keepdims=True) + EPS)
    return d * r, r


def _standardize_bwd(do, o, r):
    return r * (do - jnp.mean(do, axis=-1, keepdims=True) - o * jnp.mean(do * o, axis=-1, keepdims=True))


def _acc_out(ref, val, first):
    @pl.when(first)
    def _():
        ref[...] = val

    @pl.when(jnp.logical_not(first))
    def _():
        ref[...] += val


def _row_tile(t, pref):
    tm = min(t, pref)
    assert t % tm == 0, (t, tm)
    return tm


def _segments(part_widths, shard_w):
    bounds = {0}
    off = 0
    for w in part_widths:
        off += w
        bounds.add(off)
    total = off
    for j in range(1, total // shard_w + 1):
        bounds.add(j * shard_w)
    bounds = sorted(bounds)
    starts = np.cumsum([0] + list(part_widths))
    segs = []
    for a, b in zip(bounds[:-1], bounds[1:]):
        p = int(np.searchsorted(starts, a, side="right") - 1)
        segs.append((p, a - int(starts[p]), a // shard_w, a % shard_w, b - a))
    return segs


ANY = pl.BlockSpec(memory_space=pl.ANY)


def _mesh_pos():
    x, y, c = lax.axis_index("x"), lax.axis_index("y"), lax.axis_index("c")
    chips = [(1 - x, y), (x, 1 - y), (1 - x, 1 - y)]
    return x, y, c, 2 * x + y, chips, [2 * cx + cy for cx, cy in chips]


def _rider_copies(kind, i_refs, o_refs, send, recv, pos):
    x, y, c, me, chips, cj = pos
    out = []
    for b, (i_ref, o_ref) in enumerate(zip(i_refs, o_refs)):
        for k in range(1 if kind == "pairx" else 3):
            if kind == "ici":
                src, dst, land, dev = i_ref.at[me, c], o_ref.at[me, c], o_ref.at[cj[k], c], (*chips[k], c)
            elif kind == "d2d":
                src, dst, land, dev = i_ref.at[cj[k], c], o_ref.at[cj[k], c], o_ref.at[cj[k], 1 - c], (x, y, 1 - c)
            elif kind == "pairx":
                src, dst, land, dev = i_ref.at[:, 1 - c], o_ref, o_ref, (x, y, 1 - c)
            else:
                src, dst, land, dev = i_ref.at[cj[k]], o_ref.at[k], o_ref.at[k], (*chips[k], c)
            out.append(tuple(pltpu.make_async_remote_copy(
                src_ref=s_, dst_ref=d_, send_sem=send.at[b, k], recv_sem=recv.at[b, k],
                device_id=dev, device_id_type=MESH) for s_, d_ in ((src, dst), (land, land))))
    return out


def _rider_out_shape(kind, a):
    if kind == "scatter":
        return S((3,) + a.shape[1:], a.dtype)
    if kind == "pairx":
        return S(a.shape[:1] + a.shape[2:], a.dtype)
    return S(a.shape, a.dtype)


def _pcall(body, args, riders, *, grid, in_specs, out_specs, out_shape, name, sem, scratch_shapes=()):
    outs = list(out_shape)
    if not riders:
        res = pl.pallas_call(body, grid=grid, in_specs=in_specs, out_specs=out_specs, out_shape=outs, name=name,
                             scratch_shapes=list(scratch_shapes), compiler_params=_cp(*sem))(*args)
        return res, []
    r_in = [a for _, bufs in riders for a in bufs]
    r_out = [_rider_out_shape(kind, a) for kind, bufs in riders for a in bufs]
    n_in, n_out, n_scr, n_r = len(args), len(outs), len(scratch_shapes), len(r_in)
    aliases, idx = {}, 0
    for kind, bufs in riders:
        for _ in bufs:
            if kind in ("ici", "d2d"):
                aliases[n_in + idx] = n_out + idx
            idx += 1
    sems = [pltpu.SemaphoreType.DMA((len(bufs), 3)) for _, bufs in riders for _ in range(2)]

    def wrapped(*refs):
        a, ri = refs[:n_in], refs[n_in:n_in + n_r]
        o, ro = refs[n_in + n_r:n_in + n_r + n_out], refs[n_in + n_r + n_out:n_in + 2 * n_r + n_out]
        scr = refs[n_in + 2 * n_r + n_out:n_in + 2 * n_r + n_out + n_scr]
        sm = refs[n_in + 2 * n_r + n_out + n_scr:]
        pos = _mesh_pos()
        copies, off = [], 0
        for r, (kind, bufs) in enumerate(riders):
            copies += _rider_copies(kind, ri[off:off + len(bufs)], ro[off:off + len(bufs)], sm[2 * r], sm[2 * r + 1], pos)
            off += len(bufs)
        ids = [pl.program_id(d) for d in range(len(grid))]
        first = functools.reduce(jnp.logical_and, [i == 0 for i in ids])
        last = functools.reduce(jnp.logical_and, [i == n - 1 for i, n in zip(ids, grid)])

        @pl.when(first)
        def _():
            for cp, _ in copies:
                cp.start()

        body(*a, *o, *scr)

        @pl.when(last)
        def _():
            for cp, land in copies:
                land.wait_recv()
                cp.wait_send()

    res = pl.pallas_call(
        wrapped, grid=grid, in_specs=list(in_specs) + [ANY] * n_r, out_specs=list(out_specs) + [ANY] * n_r,
        out_shape=outs + r_out, input_output_aliases=aliases, name=name,
        scratch_shapes=list(scratch_shapes) + sems, compiler_params=_cp(*(("arbitrary",) * len(grid))))(*args, *r_in)
    return res[:n_out], res[n_out:]


def _wcol_spec(w):
    return pl.BlockSpec(w.shape, lambda *_: (0, 0, 0))


def _wrow_spec(w):
    return pl.BlockSpec(w.shape, lambda *_: (0, 0))


def _norm_mm(x, g, w, out_dtype, name, tm_pref, riders=(), rope=None):
    t = x.shape[0]
    nc = w.shape[2]
    tm = _row_tile(t, tm_pref)
    extra = list(rope) if rope else []

    def body(x_ref, g_ref, w_ref, *rest):
        o_ref, ht_ref = rest[-2], rest[-1]
        xv = x_ref[...]
        hf = xv * _rms_r(xv) * g_ref[...]
        h = _mx(hf)
        for j in range(N_CHIPS):
            o_ref[:, j * nc:(j + 1) * nc] = jnp.dot(h, w_ref[j], preferred_element_type=F32).astype(o_ref.dtype)
        if rope:
            _rotate_qk(o_ref, rest[0][...], rest[1][...])
        ht_ref[...] = hf.T.astype(ht_ref.dtype)

    (out, ht), rid = _pcall(
        body, [x, g, w] + extra, riders, grid=(t // tm,),
        in_specs=[pl.BlockSpec((tm, D), lambda i: (i, 0)), pl.BlockSpec((1, D), lambda i: (0, 0)), _wcol_spec(w)]
        + [pl.BlockSpec((tm, HD), lambda i: (i, 0)) for _ in extra],
        out_specs=[pl.BlockSpec((tm, N_CHIPS * nc), lambda i: (i, 0)), pl.BlockSpec((D, tm), lambda i: (0, i))],
        out_shape=[S((t, N_CHIPS * nc), out_dtype), S((D, t), MXU_DTYPE)], name=name, sem=("parallel",))
    return out, ht, rid


def _parts_mm_res(parts, w, res, name):
    t = res.shape[0]
    tm = _row_tile(t, 512)
    widths = [p.shape[1] for p in parts]
    offs = np.cumsum([0] + widths)
    n = len(parts)

    def body(*refs):
        p_refs, w_ref, r_ref, o_ref = refs[:n], refs[n], refs[n + 1], refs[n + 2]
        acc = r_ref[...]
        for p in range(n):
            acc = acc + _dot(p_refs[p][...], w_ref[int(offs[p]):int(offs[p + 1]), :])
        o_ref[...] = acc

    return pl.pallas_call(
        body, grid=(t // tm,),
        in_specs=[pl.BlockSpec((tm, wd), lambda i: (i, 0)) for wd in widths]
        + [_wrow_spec(w), pl.BlockSpec((tm, D), lambda i: (i, 0))],
        out_specs=pl.BlockSpec((tm, D), lambda i: (i, 0)),
        out_shape=S((t, D), F32), name=name, compiler_params=_cp("parallel"))(*parts, w, res)


def _swiglu(ff):
    gate = ff[:, :FFN_H].astype(F32)
    up = ff[:, FFN_H:].astype(F32)
    return gate * _sigmoid(gate) * up


def _swiglu_mm_res(ff, w, res, name, riders=()):
    t = res.shape[0]
    tm = _row_tile(t, 512)

    def body(f_ref, w_ref, r_ref, o_ref):
        o_ref[...] = r_ref[...] + _dot(_swiglu(f_ref[...]), w_ref[...])

    (out,), rid = _pcall(
        body, [ff, w, res], riders, grid=(t // tm,),
        in_specs=[pl.BlockSpec((tm, 2 * FFN_H), lambda i: (i, 0)), _wrow_spec(w),
                  pl.BlockSpec((tm, D), lambda i: (i, 0))],
        out_specs=[pl.BlockSpec((tm, D), lambda i: (i, 0))],
        out_shape=[S((t, D), F32)], name=name, sem=("parallel",))
    return out, rid


def _dx_norm(dparts, w, x, g, dres, name, tm_pref, riders=()):
    t = x.shape[0]
    nc = w.shape[2]
    tm = _row_tile(t, tm_pref)
    widths = [p.shape[1] for p in dparts]
    segs = _segments(widths, nc)
    n = len(dparts)

    def body(*refs):
        d_refs = refs[:n]
        w_ref, x_ref, g_ref, r_ref, dx_ref, dg_ref = refs[n:]
        dh = jnp.zeros((tm, D), F32)
        for (p, po, j, jo, wd) in segs:
            dh = dh + _dot_nt(d_refs[p][:, po:po + wd], w_ref[j, :, jo:jo + wd])
        xv = x_ref[...]
        dx, dgrow = _rms_bwd(dh, xv, _rms_r(xv), g_ref[...])
        dx_ref[...] = r_ref[...] + dx
        _acc_out(dg_ref, jnp.sum(dgrow, axis=0, keepdims=True), pl.program_id(0) == 0)

    (dx, dg), rid = _pcall(
        body, [*dparts, w, x, g, dres], riders, grid=(t // tm,),
        in_specs=[pl.BlockSpec((tm, wd), lambda i: (i, 0)) for wd in widths]
        + [_wcol_spec(w), pl.BlockSpec((tm, D), lambda i: (i, 0)),
           pl.BlockSpec((1, D), lambda i: (0, 0)), pl.BlockSpec((tm, D), lambda i: (i, 0))],
        out_specs=[pl.BlockSpec((tm, D), lambda i: (i, 0)), pl.BlockSpec((1, D), lambda i: (0, 0))],
        out_shape=[S((t, D), F32), S((1, D), F32)], name=name, sem=("arbitrary",))
    return dx, dg, rid


def _dx_parts(dy, w, widths, name, riders=()):
    t = dy.shape[0]
    tm = _row_tile(t, 512)
    offs = np.cumsum([0] + list(widths))
    n = len(widths)

    def body(dy_ref, w_ref, *o_refs):
        dyv = _mx(dy_ref[...])
        for p in range(n):
            o_refs[p][...] = _dot_nt(dyv, w_ref[int(offs[p]):int(offs[p + 1]), :])

    return _pcall(
        body, [dy, w], riders, grid=(t // tm,),
        in_specs=[pl.BlockSpec((tm, D), lambda i: (i, 0)), _wrow_spec(w)],
        out_specs=[pl.BlockSpec((tm, wd), lambda i: (i, 0)) for wd in widths],
        out_shape=[S((t, wd), F32) for wd in widths], name=name, sem=("parallel",))


def _dx_swiglu(dy, w, ff, name):
    t = dy.shape[0]
    tm = _row_tile(t, 512)

    def body(dy_ref, w_ref, f_ref, o_ref):
        dact = _dot_nt(dy_ref[...], w_ref[...])
        gate = f_ref[:, :FFN_H].astype(F32)
        up = f_ref[:, FFN_H:].astype(F32)
        s = _sigmoid(gate)
        gs = gate * s
        o_ref[:, :FFN_H] = ((dact * up) * (s + gs - gs * s)).astype(o_ref.dtype)
        o_ref[:, FFN_H:] = (dact * gs).astype(o_ref.dtype)

    return pl.pallas_call(
        body, grid=(t // tm,),
        in_specs=[pl.BlockSpec((tm, D), lambda i: (i, 0)), _wrow_spec(w),
                  pl.BlockSpec((tm, 2 * FFN_H), lambda i: (i, 0))],
        out_specs=pl.BlockSpec((tm, 2 * FFN_H), lambda i: (i, 0)),
        out_shape=S((t, 2 * FFN_H), ACT_DTYPE), name=name, compiler_params=_cp("parallel"))(dy, w, ff)


def _call_into(body, into, in_specs, args, *, n_prefetch, grid, out_specs, **kw):
    n_in = len(args)
    if into is None:
        gs = pltpu.PrefetchScalarGridSpec(num_scalar_prefetch=n_prefetch, grid=grid, in_specs=in_specs,
                                          out_specs=out_specs)
        return pl.pallas_call(body, grid_spec=gs, **kw)(*args)

    def wrapped(*refs):
        return body(*refs[:n_in], *refs[n_in + 1:])

    gs = pltpu.PrefetchScalarGridSpec(num_scalar_prefetch=n_prefetch, grid=grid,
                                      in_specs=list(in_specs) + [ANY], out_specs=out_specs)
    return pl.pallas_call(wrapped, grid_spec=gs, input_output_aliases={n_in: 0}, **kw)(*args, into)


def _dw_norm_parts(ht, dparts, nc, name):
    t = ht.shape[1]
    tk = _row_tile(t, 1024)
    widths = [p.shape[1] for p in dparts]
    segs = _segments(widths, nc)
    n = len(dparts)
    nk = t // tk

    def body(*refs):
        h_ref, d_refs, o_ref, acc_ref = refs[0], refs[1:1 + n], refs[1 + n], refs[2 + n]
        k = pl.program_id(0)
        h = h_ref[...]

        @pl.when(k == 0)
        def _():
            acc_ref[...] = jnp.zeros_like(acc_ref)

        for (p, po, j, jo, wd) in segs:
            acc_ref[j, :, jo:jo + wd] += _dot(h, d_refs[p][:, po:po + wd])

        @pl.when(k == nk - 1)
        def _():
            o_ref[...] = acc_ref[...].astype(o_ref.dtype)

    return pl.pallas_call(
        body, grid=(nk,),
        in_specs=[pl.BlockSpec((D, tk), lambda k: (0, k))]
        + [pl.BlockSpec((tk, wd), lambda k: (k, 0)) for wd in widths],
        out_specs=pl.BlockSpec((N_CHIPS, D, nc), lambda k: (0, 0, 0)),
        out_shape=S((N_CHIPS, D, nc), MXU_DTYPE), name=name,
        scratch_shapes=[pltpu.VMEM((N_CHIPS, D, nc), F32)], compiler_params=_cp("arbitrary"))(ht, *dparts)


def _dw_norm_cols(ht, dy, nc, name):
    t = ht.shape[1]
    tk = _row_tile(t, 2048)
    nk = t // tk

    def body(h_ref, dy_ref, o_ref, acc_ref):
        k = pl.program_id(1)

        @pl.when(k == 0)
        def _():
            acc_ref[...] = jnp.zeros_like(acc_ref)

        acc_ref[...] += _dot(h_ref[...], dy_ref[...])

        @pl.when(k == nk - 1)
        def _():
            o_ref[...] = acc_ref[...].astype(o_ref.dtype)

    return pl.pallas_call(
        body, grid=(N_CHIPS, nk),
        in_specs=[pl.BlockSpec((D, tk), lambda j, k: (0, k)), pl.BlockSpec((tk, nc), lambda j, k: (k, j))],
        out_specs=pl.BlockSpec((None, D, nc), lambda j, k: (j, 0, 0)),
        out_shape=S((N_CHIPS, D, nc), MXU_DTYPE), name=name,
        scratch_shapes=[pltpu.VMEM((D, nc), F32)], compiler_params=_cp("parallel", "arbitrary"))(ht, dy)


def _dw_parts(parts, dy, name):
    t = dy.shape[0]
    tk = _row_tile(t, 1024)
    widths = [p.shape[1] for p in parts]
    offs = np.cumsum([0] + widths)
    ktot = int(offs[-1])
    n = len(parts)
    nk = t // tk

    def body(*refs):
        p_refs, dy_ref, o_ref, acc_ref = refs[:n], refs[n], refs[n + 1], refs[n + 2]
        k = pl.program_id(0)

        @pl.when(k == 0)
        def _():
            acc_ref[...] = jnp.zeros_like(acc_ref)

        dyv = _mx(dy_ref[...])
        for p in range(n):
            acc_ref[int(offs[p]):int(offs[p + 1]), :] += _dot_tn(p_refs[p][...], dyv)

        @pl.when(k == nk - 1)
        def _():
            o_ref[...] = acc_ref[...].astype(o_ref.dtype)

    return pl.pallas_call(
        body, grid=(nk,),
        in_specs=[pl.BlockSpec((tk, wd), lambda k: (k, 0)) for wd in widths]
        + [pl.BlockSpec((tk, D), lambda k: (k, 0))],
        out_specs=pl.BlockSpec((ktot, D), lambda k: (0, 0)),
        out_shape=S((ktot, D), MXU_DTYPE), name=name,
        scratch_shapes=[pltpu.VMEM((ktot, D), F32)], compiler_params=_cp("arbitrary"))(*parts, dy)


def _dw_swiglu(ff, dy, name):
    t = dy.shape[0]
    tk = _row_tile(t, 512)
    nk = t // tk

    def body(f_ref, dy_ref, o_ref, acc_ref):
        k = pl.program_id(0)

        @pl.when(k == 0)
        def _():
            acc_ref[...] = jnp.zeros_like(acc_ref)

        acc_ref[...] += _dot_tn(_swiglu(f_ref[...]), dy_ref[...])

        @pl.when(k == nk - 1)
        def _():
            o_ref[...] = acc_ref[...].astype(o_ref.dtype)

    return pl.pallas_call(
        body, grid=(nk,),
        in_specs=[pl.BlockSpec((tk, 2 * FFN_H), lambda k: (k, 0)), pl.BlockSpec((tk, D), lambda k: (k, 0))],
        out_specs=pl.BlockSpec((FFN_H, D), lambda k: (0, 0)),
        out_shape=S((FFN_H, D), MXU_DTYPE), name=name,
        scratch_shapes=[pltpu.VMEM((FFN_H, D), F32)], compiler_params=_cp("arbitrary"))(ff, dy)


def _tables(t):
    half = HD // 2
    inv_freq = ROPE_BASE ** (-jnp.arange(half, dtype=F32) / half)
    base = (jnp.arange(t // CH, dtype=F32) * CH)[:, None] * inv_freq[None, :]
    off = jnp.arange(CH, dtype=F32)[:, None] * inv_freq[None, :]
    cb, sb, co, so = jnp.cos(base)[:, None], jnp.sin(base)[:, None], jnp.cos(off)[None], jnp.sin(off)[None]
    cos = (cb * co - sb * so).reshape(t, half)
    sin = (sb * co + cb * so).reshape(t, half)
    tb = {"cos2": jnp.concatenate([cos, cos], axis=1), "sin2": jnp.concatenate([-sin, sin], axis=1)}
    gf = 1.0 - jnp.exp2(-5.0 - jnp.arange(HEADS, dtype=F32))
    lgf = jnp.log(gf)[:, None]
    lgb = jnp.log(gf[::-1])[:, None]
    idx = jnp.arange(CH, dtype=F32)
    diff = idx[:, None] - idx[None, :]
    dfwd = jnp.where(diff >= 0, jnp.exp(lgf[:, :, None] * jnp.where(diff >= 0, diff, 0.0)), 0.0)
    dbwd = jnp.where(diff < 0, jnp.exp(lgb[:, :, None] * jnp.where(diff < 0, -diff, 0.0)), 0.0)
    tb["dm"] = dfwd + dbwd
    tb["dmt"] = jnp.swapaxes(tb["dm"], 1, 2)

    def lanes(a):
        return jnp.repeat(a.T, HD, axis=1)

    tb["xif"] = lanes(jnp.exp(lgf * (idx + 1)))
    tb["zf"] = lanes(jnp.exp(lgf * (CH - 1 - idx)))
    tb["xib"] = lanes(jnp.exp(lgb * (CH - idx)))
    tb["zb"] = lanes(jnp.exp(lgb * idx))
    tb["gcf"] = jnp.repeat(jnp.exp(lgf * CH), HD, axis=0).reshape(1, HEADS * HD)
    tb["gcb"] = jnp.repeat(jnp.exp(lgb * CH), HD, axis=0).reshape(1, HEADS * HD)
    return tb


def _full(shape):
    nd = len(shape)
    return pl.BlockSpec(shape, lambda *_: (0,) * nd)


def _gm_mixed(vn, ws_ref, bias):
    lane = lax.broadcasted_iota(jnp.int32, (CH, 128), 1)
    halves = []
    for hf in range(2):
        vh = _mx(vn[:, hf * 128:(hf + 1) * 128])
        r0 = jnp.dot(_mx(ws_ref[2 * hf]), vh, preferred_element_type=F32)
        r1 = jnp.dot(_mx(ws_ref[2 * hf + 1]), vh, preferred_element_type=F32)
        halves.append(jnp.where(lane < 64, r0, r1))
    return jnp.concatenate(halves, axis=1) + bias


def _gm_fwd(proj, ln_g, ln_b, ws, bias, name, riders=()):
    t = proj.shape[0]
    tm = _row_tile(t, 512)

    def body(pu_ref, pv_ref, g_ref, b_ref, ws_ref, bias_ref, o_ref):
        for c in range(tm // CH):
            rows = slice(c * CH, (c + 1) * CH)
            u = _gelu(pu_ref[rows, :])
            o, _ = _standardize(_gelu(pv_ref[rows, :]))
            vn = o * g_ref[...] + b_ref[...]
            o_ref[rows, :] = (u * _gm_mixed(vn, ws_ref, bias_ref[...])).astype(o_ref.dtype)

    (out,), rid = _pcall(
        body, [proj, proj, ln_g, ln_b, ws, bias], riders, grid=(t // tm,),
        in_specs=[pl.BlockSpec((tm, GM_W), lambda i: (i, 0)), pl.BlockSpec((tm, GM_W), lambda i: (i, 1)),
                  _full((1, GM_W)), _full((1, GM_W)), _full((GM_HEADS, CH, CH)), _full((CH, GM_W))],
        out_specs=[pl.BlockSpec((tm, GM_W), lambda i: (i, 0))],
        out_shape=[S((t, GM_W), ACT_DTYPE)], name=name, sem=("parallel",))
    return out, rid


def _gm_bwd(proj, dy, ln_g, ln_b, ws, wst, bias, name):
    t = proj.shape[0]
    tm = _row_tile(t, 512)
    nb = t // tm

    def body(pu_ref, pv_ref, dy_ref, g_ref, b_ref, ws_ref, wst_ref, bias_ref,
             d_ref, dws_ref, dbs_ref, dg_ref, db_ref, dbias_ref):
        first = pl.program_id(0) == 0
        lane = lax.broadcasted_iota(jnp.int32, (CH, 128), 1)
        dws = [jnp.zeros((CH, CH), F32) for _ in range(GM_HEADS)]
        dbias = jnp.zeros((CH, GM_W), F32)
        dg = jnp.zeros((1, GM_W), F32)
        db = jnp.zeros((1, GM_W), F32)
        for c in range(tm // CH):
            rows = slice(c * CH, (c + 1) * CH)
            pu = pu_ref[rows, :]
            pv = pv_ref[rows, :]
            u = _gelu(pu)
            o, r = _standardize(_gelu(pv))
            vn = o * g_ref[...] + b_ref[...]
            mixed = _gm_mixed(vn, ws_ref, bias_ref[...])
            dyv = dy_ref[rows, :]
            d_ref[rows, :GM_W] = (dyv * mixed * _gelu_grad(pu)).astype(d_ref.dtype)
            dmixed = dyv * u
            dbias = dbias + dmixed
            dvn_halves = []
            for hf in range(2):
                dm = dmixed[:, hf * 128:(hf + 1) * 128]
                vh = vn[:, hf * 128:(hf + 1) * 128]
                dm0 = jnp.where(lane < 64, dm, 0.0)
                dm1 = dm - dm0
                dws[2 * hf] = dws[2 * hf] + _dot_nt(dm0, vh)
                dws[2 * hf + 1] = dws[2 * hf + 1] + _dot_nt(dm1, vh)
                t0 = _dot(wst_ref[2 * hf], dm)
                t1 = _dot(wst_ref[2 * hf + 1], dm)
                dvn_halves.append(jnp.where(lane < 64, t0, t1))
            dvn = jnp.concatenate(dvn_halves, axis=1)
            dg = dg + jnp.sum(dvn * o, axis=0, keepdims=True)
            db = db + jnp.sum(dvn, axis=0, keepdims=True)
            dv = _standardize_bwd(dvn * g_ref[...], o, r)
            d_ref[rows, GM_W:] = (dv * _gelu_grad(pv)).astype(d_ref.dtype)
        for h in range(GM_HEADS):
            _acc_out(dws_ref.at[h], dws[h], first)
        _acc_out(dbias_ref, dbias, first)
        _acc_out(dg_ref, dg, first)
        _acc_out(db_ref, db, first)

        @pl.when(pl.program_id(0) == nb - 1)
        def _():
            tot = dbias_ref[...]
            head = lax.broadcasted_iota(jnp.int32, (CH, GM_W), 1) // (GM_W // GM_HEADS)
            out = jnp.zeros((CH, 128), F32)
            for h in range(GM_HEADS):
                s = jnp.sum(jnp.where(head == h, tot, 0.0), axis=1, keepdims=True)
                out = jnp.where(lane == h, s, out)
            dbs_ref[...] = out

    return pl.pallas_call(
        body, grid=(nb,),
        in_specs=[pl.BlockSpec((tm, GM_W), lambda i: (i, 0)), pl.BlockSpec((tm, GM_W), lambda i: (i, 1)),
                  pl.BlockSpec((tm, GM_W), lambda i: (i, 0)),
                  _full((1, GM_W)), _full((1, GM_W)), _full((GM_HEADS, CH, CH)), _full((GM_HEADS, CH, CH)),
                  _full((CH, GM_W))],
        out_specs=[pl.BlockSpec((tm, 2 * GM_W), lambda i: (i, 0)), _full((GM_HEADS, CH, CH)), _full((CH, 128)),
                   _full((1, GM_W)), _full((1, GM_W))],
        out_shape=[S((t, 2 * GM_W), ACT_DTYPE), S((GM_HEADS, CH, CH), F32), S((CH, 128), F32),
                   S((1, GM_W), F32), S((1, GM_W), F32)],
        scratch_shapes=[pltpu.VMEM((CH, GM_W), F32)],
        name=name, compiler_params=_cp("arbitrary"))(proj, proj, dy, ln_g, ln_b, ws, wst, bias)


def _rot(x, cos2, sin2):
    return x * cos2 + pltpu.roll(x, HD // 2, 1) * sin2


def _rot_bwd(dx, cos2, sin2):
    return dx * cos2 + pltpu.roll(dx * sin2, HD // 2, 1)


Q_COL, K_COL, V_COL, GATE_COL = 1, 2, 3, 4


def _rotate_qk(o_ref, cos2, sin2):
    for col, scale in ((Q_COL, 1.0), (K_COL, HD ** -0.5)):
        for h in range(HEADS):
            cols = slice(col * RET_W + h * HD, col * RET_W + (h + 1) * HD)
            o_ref[:, cols] = _rot(o_ref[:, cols], cos2, sin2) * scale


def _ret_scan(lhs, lhs_col, rhs, rhs_col, lp, ls, gp, gs, name):
    t = lhs.shape[0]
    n = t // CH
    r = 4 if n % 4 == 0 else 1
    ns = n // r

    def body(lp_ref, ls_ref, gp_ref, gs_ref, l1_ref, r1_ref, l2_ref, r2_ref, pre_ref, suf_ref, sp_ref, ss_ref):
        @pl.when(pl.program_id(0) == 0)
        def _():
            sp_ref[...] = jnp.zeros_like(sp_ref)
            ss_ref[...] = jnp.zeros_like(ss_ref)

        def kv(l_ref, r_ref, scale, rows):
            lv = l_ref[rows, :] * scale
            rv = r_ref[rows, :]
            return jnp.concatenate([_dot_tn(lv[:, h * HD:(h + 1) * HD], rv[:, h * HD:(h + 1) * HD])
                                    for h in range(HEADS)], axis=1)

        for j in range(r):
            pre_ref[j] = sp_ref[...]
            sp_ref[...] = sp_ref[...] * gp_ref[...] + kv(l1_ref, r1_ref, lp_ref[...], slice(j * CH, (j + 1) * CH))
        for j in reversed(range(r)):
            suf_ref[j] = ss_ref[...]
            ss_ref[...] = ss_ref[...] * gs_ref[...] + kv(l2_ref, r2_ref, ls_ref[...], slice(j * CH, (j + 1) * CH))

    w = HEADS * HD
    return pl.pallas_call(
        body, grid=(ns,),
        in_specs=[_full((CH, w)), _full((CH, w)), _full((1, w)), _full((1, w)),
                  pl.BlockSpec((r * CH, w), lambda s: (s, lhs_col)), pl.BlockSpec((r * CH, w), lambda s: (s, rhs_col)),
                  pl.BlockSpec((r * CH, w), lambda s: (ns - 1 - s, lhs_col)),
                  pl.BlockSpec((r * CH, w), lambda s: (ns - 1 - s, rhs_col))],
        out_specs=[pl.BlockSpec((r, HD, w), lambda s: (s, 0, 0)), pl.BlockSpec((r, HD, w), lambda s: (ns - 1 - s, 0, 0))],
        out_shape=[S((n, HD, w), F32)] * 2, name=name,
        scratch_shapes=[pltpu.VMEM((HD, w), F32), pltpu.VMEM((HD, w), F32)],
        compiler_params=_cp("arbitrary"))(lp, ls, gp, gs, lhs, rhs, lhs, rhs)


def _ret_out(proj, sf, sb, tb, name, riders=()):
    t = proj.shape[0]
    r = 4 if (t // CH) % 4 == 0 else 1
    tm = r * CH
    w = HEADS * HD

    def body(rq_ref, rk_ref, v_ref, g_ref, sf_ref, sb_ref, dm_ref, xif_ref, xib_ref, a_ref, y_ref):
        for c in range(r):
            rows = slice(c * CH, (c + 1) * CH)
            for h in range(HEADS):
                cols = slice(h * HD, (h + 1) * HD)
                q = rq_ref[rows, cols]
                p = _dot_nt(q, rk_ref[rows, cols]) * dm_ref[h]
                a = (_dot(p, v_ref[rows, cols]) + _dot(q * xif_ref[:, cols], sf_ref[c, :, cols])
                     + _dot(q * xib_ref[:, cols], sb_ref[c, :, cols]))
                a_ref[rows, cols] = a
                o, _ = _standardize(a)
                gv = g_ref[rows, cols]
                y_ref[rows, cols] = (o * (gv * _sigmoid(gv))).astype(y_ref.dtype)

    (a, y), rid = _pcall(
        body, [proj, proj, proj, proj, sf, sb, tb["dm"], tb["xif"], tb["xib"]], riders, grid=(t // tm,),
        in_specs=[pl.BlockSpec((tm, w), lambda i: (i, Q_COL)), pl.BlockSpec((tm, w), lambda i: (i, K_COL)),
                  pl.BlockSpec((tm, w), lambda i: (i, V_COL)), pl.BlockSpec((tm, w), lambda i: (i, GATE_COL)),
                  pl.BlockSpec((r, HD, w), lambda i: (i, 0, 0)), pl.BlockSpec((r, HD, w), lambda i: (i, 0, 0)),
                  _full((HEADS, CH, CH)), _full((CH, w)), _full((CH, w))],
        out_specs=[pl.BlockSpec((tm, w), lambda i: (i, 0))] * 2,
        out_shape=[S((t, w), F32), S((t, w), ACT_DTYPE)], name=name, sem=("parallel",))
    return a, y, rid


def _ret_bwd_pre(dy, a, proj, name):
    t = dy.shape[0]
    tm = _row_tile(t, 512)
    w = HEADS * HD

    def body(dy_ref, a_ref, g_ref, da_ref, dg_ref):
        for h in range(HEADS):
            cols = slice(h * HD, (h + 1) * HD)
            o, r = _standardize(a_ref[:, cols])
            gv = g_ref[:, cols]
            s = _sigmoid(gv)
            dyv = dy_ref[:, cols]
            dg_ref[:, cols] = (dyv * o * (s * (1.0 + gv * (1.0 - s)))).astype(dg_ref.dtype)
            da_ref[:, cols] = _standardize_bwd(dyv * (gv * s), o, r).astype(da_ref.dtype)

    return pl.pallas_call(
        body, grid=(t // tm,),
        in_specs=[pl.BlockSpec((tm, w), lambda i: (i, 0)), pl.BlockSpec((tm, w), lambda i: (i, 0)),
                  pl.BlockSpec((tm, w), lambda i: (i, GATE_COL))],
        out_specs=[pl.BlockSpec((tm, w), lambda i: (i, 0))] * 2,
        out_shape=[S((t, w), ACT_DTYPE)] * 2, name=name, compiler_params=_cp("parallel"))(dy, a, proj)


def _ret_bwd_main(proj, da, sf, sb, gf, gb, tb, name, riders=()):
    t = proj.shape[0]
    r = 4 if (t // CH) % 4 == 0 else 1
    tm = r * CH
    w = HEADS * HD
    scale = HD ** -0.5

    def body(rq_ref, rk_ref, v_ref, da_ref, sf_ref, sb_ref, gf_ref, gb_ref, dm_ref, dmt_ref,
             xif_ref, xib_ref, zf_ref, zb_ref, c_ref, s_ref, o_ref):
        for c in range(r):
            rows = slice(c * CH, (c + 1) * CH)
            cos2, sin2 = c_ref[rows, :], s_ref[rows, :]
            for h in range(HEADS):
                cols = slice(h * HD, (h + 1) * HD)
                q, k, v, dav = rq_ref[rows, cols], rk_ref[rows, cols], v_ref[rows, cols], da_ref[rows, cols]
                qm, km, vm, dam = _mx(q), _mx(k), _mx(v), _mx(dav)
                dm, dmt = dm_ref[h], dmt_ref[h]
                pt = _dot_nt(km, qm) * dmt
                dp = _dot_nt(dam, vm) * dm
                dpt = _dot_nt(vm, dam) * dmt
                sfh, sbh, gfh, gbh = sf_ref[c, :, cols], sb_ref[c, :, cols], gf_ref[c, :, cols], gb_ref[c, :, cols]
                zf, zb = zf_ref[:, cols], zb_ref[:, cols]
                dv = _dot(pt, dam) + zf * _dot(km, gfh) + zb * _dot(km, gbh)
                drq = _dot(dp, km) + xif_ref[:, cols] * _dot_nt(dam, sfh) + xib_ref[:, cols] * _dot_nt(dam, sbh)
                drk = _dot(dpt, qm) + _dot_nt(zf * v, gfh) + _dot_nt(zb * v, gbh)
                o_ref[rows, h * HD:(h + 1) * HD] = _rot_bwd(drq, cos2, sin2).astype(o_ref.dtype)
                o_ref[rows, w + h * HD:w + (h + 1) * HD] = (_rot_bwd(drk, cos2, sin2) * scale).astype(o_ref.dtype)
                o_ref[rows, 2 * w + h * HD:2 * w + (h + 1) * HD] = dv.astype(o_ref.dtype)

    st = pl.BlockSpec((r, HD, w), lambda i: (i, 0, 0))
    (out,), rid = _pcall(
        body, [proj, proj, proj, da, sf, sb, gf, gb, tb["dm"], tb["dmt"], tb["xif"], tb["xib"], tb["zf"], tb["zb"],
               tb["cos2"], tb["sin2"]], riders, grid=(t // tm,),
        in_specs=[pl.BlockSpec((tm, w), lambda i: (i, Q_COL)), pl.BlockSpec((tm, w), lambda i: (i, K_COL)),
                  pl.BlockSpec((tm, w), lambda i: (i, V_COL)), pl.BlockSpec((tm, w), lambda i: (i, 0)),
                  st, st, st, st, _full((HEADS, CH, CH)), _full((HEADS, CH, CH)),
                  _full((CH, w)), _full((CH, w)), _full((CH, w)), _full((CH, w)),
                  pl.BlockSpec((tm, HD), lambda i: (i, 0)), pl.BlockSpec((tm, HD), lambda i: (i, 0))],
        out_specs=[pl.BlockSpec((tm, 3 * w), lambda i: (i, 0))],
        out_shape=[S((t, 3 * w), ACT_DTYPE)], name=name, sem=("parallel",))
    return out, rid


CONV_TM = 256
CONV_SUB = 64
A_COL = (2 * GM_W + 4 * RET_W) // CV_W
G_COL = A_COL + 1


def _halo_specs(t, tm, col):
    nb16 = t // HALO
    per = tm // HALO
    return [pl.BlockSpec((tm, CV_W), lambda i: (i, col)),
            pl.BlockSpec((HALO, CV_W), lambda i: (jnp.maximum(i * per - 1, 0), col)),
            pl.BlockSpec((HALO, CV_W), lambda i: (jnp.minimum((i + 1) * per, nb16 - 1), col))]


def _fill_padded(dst_ref, prev, main, nxt, tm, i, nb):
    dst_ref[0:HALO, :] = jnp.where(i > 0, prev, 0.0)
    dst_ref[HALO:HALO + tm, :] = main
    dst_ref[HALO + tm:2 * HALO + tm, :] = jnp.where(i < nb - 1, nxt, 0.0)


SUBLANES = 8


def _fill_shifted(sh_ref, src_ref, tm):
    n = tm + 2 * HALO - SUBLANES
    for b in range(SUBLANES):
        sh_ref[b, 0:n, :] = src_ref[pl.ds(b, n), :]


def _tap(sh_ref, off, rows):
    return sh_ref[off % SUBLANES, pl.ds(off - off % SUBLANES, rows), :]


def _conv_fwd(proj, cw, cb, ln_g, ln_b, name, riders=()):
    t = proj.shape[0]
    tm = _row_tile(t, CONV_TM)
    nb = t // tm

    def body(a_ref, ap_ref, an_ref, g_ref, gp_ref, gn_ref, w_ref, b_ref, lg_ref, lb_ref, y_ref, hc_ref,
             hp_ref, sh_ref):
        i = pl.program_id(0)
        _fill_padded(hp_ref, ap_ref[...] * _sigmoid(gp_ref[...]), a_ref[...] * _sigmoid(g_ref[...]),
                     an_ref[...] * _sigmoid(gn_ref[...]), tm, i, nb)
        _fill_shifted(sh_ref, hp_ref, tm)
        for sb in range(tm // CONV_SUB):
            acc = jnp.zeros((CONV_SUB, CV_W), F32) + b_ref[...]
            for k in range(KCONV):
                acc = acc + w_ref[k:k + 1, :] * _tap(sh_ref, sb * CONV_SUB + k + 1, CONV_SUB)
            rows = slice(sb * CONV_SUB, (sb + 1) * CONV_SUB)
            hc_ref[rows, :] = acc
            o, _ = _standardize(acc)
            z = o * lg_ref[...] + lb_ref[...]
            y_ref[rows, :] = (z * _sigmoid(z)).astype(y_ref.dtype)

    (y, hc), rid = _pcall(
        body, [proj, proj, proj, proj, proj, proj, cw, cb, ln_g, ln_b], riders, grid=(nb,),
        in_specs=_halo_specs(t, tm, A_COL) + _halo_specs(t, tm, G_COL)
        + [_full((32, CV_W)), _full((1, CV_W)), _full((1, CV_W)), _full((1, CV_W))],
        out_specs=[pl.BlockSpec((tm, CV_W), lambda i: (i, 0))] * 2,
        out_shape=[S((t, CV_W), ACT_DTYPE), S((t, CV_W), F32)], name=name, sem=("parallel",),
        scratch_shapes=[pltpu.VMEM((tm + 2 * HALO, CV_W), F32), pltpu.VMEM((SUBLANES, tm + 2 * HALO, CV_W), F32)])
    return y, hc, rid


def _conv_bwd(proj, dy, hc, cw, ln_g, ln_b, name, riders=()):
    t = proj.shape[0]
    tm = _row_tile(t, CONV_TM)
    nb = t // tm

    def body(a_ref, ap_ref, an_ref, g_ref, gp_ref, gn_ref, dy_ref, dyp_ref, dyn_ref, hc_ref, hcp_ref, hcn_ref,
             w_ref, lg_ref, lb_ref, d_ref, dw_ref, dcb_ref, dlg_ref, dlb_ref, hp_ref, dhp_ref, dwacc_ref,
             sh_ref, dsh_ref):
        i = pl.program_id(0)
        first = i == 0

        def dhc_of(dyv, hcv):
            o, r = _standardize(hcv)
            z = o * lg_ref[...] + lb_ref[...]
            s = _sigmoid(z)
            dz = dyv * (s * (1.0 + z * (1.0 - s)))
            return _standardize_bwd(dz * lg_ref[...], o, r), dz, o

        dhc, dz, o = dhc_of(dy_ref[...], hc_ref[...])
        _acc_out(dlg_ref, jnp.sum(dz * o, axis=0, keepdims=True), first)
        _acc_out(dlb_ref, jnp.sum(dz, axis=0, keepdims=True), first)
        _acc_out(dcb_ref, jnp.sum(dhc, axis=0, keepdims=True), first)
        _fill_padded(dhp_ref, dhc_of(dyp_ref[...], hcp_ref[...])[0], dhc, dhc_of(dyn_ref[...], hcn_ref[...])[0],
                     tm, i, nb)
        _fill_padded(hp_ref, ap_ref[...] * _sigmoid(gp_ref[...]), a_ref[...] * _sigmoid(g_ref[...]),
                     an_ref[...] * _sigmoid(gn_ref[...]), tm, i, nb)

        _fill_shifted(sh_ref, hp_ref, tm)
        _fill_shifted(dsh_ref, dhp_ref, tm)

        @pl.when(first)
        def _():
            dwacc_ref[...] = jnp.zeros_like(dwacc_ref)

        for sb in range(tm // CONV_SUB):
            base = sb * CONV_SUB
            dmain = dhp_ref[pl.ds(HALO + base, CONV_SUB), :]
            dh = jnp.zeros((CONV_SUB, CV_W), F32)
            for k in range(KCONV):
                dh = dh + w_ref[k:k + 1, :] * _tap(dsh_ref, base + 2 * HALO - 1 - k, CONV_SUB)
                prod = dmain * _tap(sh_ref, base + k + 1, CONV_SUB)
                dwacc_ref[k * 8:(k + 1) * 8, :] += jnp.sum(prod.reshape(CONV_SUB // 8, 8, CV_W), axis=0)
            rows = slice(base, base + CONV_SUB)
            s = _sigmoid(g_ref[rows, :])
            d_ref[rows, :CV_W] = (dh * s).astype(d_ref.dtype)
            d_ref[rows, CV_W:] = (dh * a_ref[rows, :] * (s * (1.0 - s))).astype(d_ref.dtype)

        @pl.when(i == nb - 1)
        def _():
            for k in range(KCONV):
                dw_ref[k:k + 1, :] = jnp.sum(dwacc_ref[k * 8:(k + 1) * 8, :], axis=0, keepdims=True)
            dw_ref[KCONV:32, :] = jnp.zeros((32 - KCONV, CV_W), F32)

    hs = [pl.BlockSpec((tm, CV_W), lambda i: (i, 0)),
          pl.BlockSpec((HALO, CV_W), lambda i: (jnp.maximum(i * (tm // HALO) - 1, 0), 0)),
          pl.BlockSpec((HALO, CV_W), lambda i: (jnp.minimum((i + 1) * (tm // HALO), t // HALO - 1), 0))]
    outs, rid = _pcall(
        body, [proj, proj, proj, proj, proj, proj, dy, dy, dy, hc, hc, hc, cw, ln_g, ln_b], riders, grid=(nb,),
        in_specs=_halo_specs(t, tm, A_COL) + _halo_specs(t, tm, G_COL) + hs + hs
        + [_full((32, CV_W)), _full((1, CV_W)), _full((1, CV_W))],
        out_specs=[pl.BlockSpec((tm, 2 * CV_W), lambda i: (i, 0)), _full((32, CV_W)), _full((1, CV_W)),
                   _full((1, CV_W)), _full((1, CV_W))],
        out_shape=[S((t, 2 * CV_W), ACT_DTYPE), S((32, CV_W), F32), S((1, CV_W), F32), S((1, CV_W), F32),
                   S((1, CV_W), F32)],
        name=name, sem=("arbitrary",),
        scratch_shapes=[pltpu.VMEM((tm + 2 * HALO, CV_W), F32), pltpu.VMEM((tm + 2 * HALO, CV_W), F32),
                        pltpu.VMEM((32 * 8, CV_W), F32), pltpu.VMEM((SUBLANES, tm + 2 * HALO, CV_W), F32),
                        pltpu.VMEM((SUBLANES, tm + 2 * HALO, CV_W), F32)])
    return (*outs, rid)


def _loss_head(x, g, target, name):
    t = x.shape[0]
    tm = _row_tile(t, 512)

    def body(x_ref, g_ref, t_ref, dx_ref, dg_ref, l_ref):
        first = pl.program_id(0) == 0
        xv = x_ref[...]
        r = _rms_r(xv)
        e = xv * r * g_ref[...] - t_ref[...]
        dx, dgrow = _rms_bwd(e * (1.0 / D), xv, r, g_ref[...])
        dx_ref[...] = dx
        _acc_out(dg_ref, jnp.sum(dgrow, axis=0, keepdims=True), first)
        part = 0.5 * jnp.sum(jnp.mean(e * e, axis=-1, keepdims=True), axis=0, keepdims=True)
        _acc_out(l_ref, jnp.broadcast_to(part, (8, 128)), first)

    return pl.pallas_call(
        body, grid=(t // tm,),
        in_specs=[pl.BlockSpec((tm, D), lambda i: (i, 0)), _full((1, D)), pl.BlockSpec((tm, D), lambda i: (i, 0))],
        out_specs=[pl.BlockSpec((tm, D), lambda i: (i, 0)), _full((1, D)), _full((8, 128))],
        out_shape=[S((t, D), F32), S((1, D), F32), S((8, 128), F32)], name=name,
        compiler_params=_cp("arbitrary"))(x, g, target)


def _as2d(a):
    return a.reshape(-1, a.shape[-1])


def _ew_tile(rows, cols, n_arrays):
    budget = VMEM_LIMIT // 2
    tr = rows
    while tr * cols * 4 * n_arrays * 2 > budget and tr % 16 == 0:
        tr //= 2
    assert rows % tr == 0
    return tr


def _adamw(w, g, m, v, name):
    shape = w.shape
    w2, g2, m2, v2 = _as2d(w), _as2d(g), _as2d(m), _as2d(v)
    rows, cols = w2.shape
    tr = _ew_tile(rows, cols, 7)

    def body(w_ref, g_ref, m_ref, v_ref, d_ref, nm_ref, nv_ref):
        gv = g_ref[...]
        nm = ADAM_B1 * m_ref[...] + (1.0 - ADAM_B1) * gv
        nv = ADAM_B2 * v_ref[...] + (1.0 - ADAM_B2) * (gv * gv)
        m_hat = nm / (1.0 - ADAM_B1 ** ADAM_STEP)
        v_hat = nv / (1.0 - ADAM_B2 ** ADAM_STEP)
        d_ref[...] = -ADAM_LR * (m_hat / (jnp.sqrt(v_hat) + ADAM_EPS) + ADAM_WD * w_ref[...])
        nm_ref[...] = nm
        nv_ref[...] = nv

    spec = pl.BlockSpec((tr, cols), lambda i: (i, 0))
    outs = pl.pallas_call(body, grid=(rows // tr,), in_specs=[spec] * 4, out_specs=[spec] * 3,
                          out_shape=[S((rows, cols), F32)] * 3, name=name,
                          compiler_params=_cp("parallel"))(w2, g2, m2, v2)
    return tuple(o.reshape(shape) for o in outs)


BIG = (("w_in", "col"), ("w_out", "row"), ("w_ffn_in", "col"), ("w_ffn_out", "row"))
NBIG = len(BIG)


def _cast_to_gathered(w, l, me, name):
    _, r_, c_ = w.shape
    tr = _ew_tile(r_, c_, 2)

    def body(me_ref, w_ref, o_ref):
        o_ref[...] = w_ref[...].astype(o_ref.dtype)

    gs = pltpu.PrefetchScalarGridSpec(
        num_scalar_prefetch=1, grid=(r_ // tr,),
        in_specs=[pl.BlockSpec((None, tr, c_), lambda i, s: (l, i, 0))],
        out_specs=pl.BlockSpec((None, tr, c_), lambda i, s: (s[0], i, 0)))
    out = pl.pallas_call(body, grid_spec=gs, out_shape=S((N_CHIPS, r_, c_), MXU_DTYPE), name=name,
                         compiler_params=_cp("parallel"))(me.reshape(1), w)
    return out.reshape(N_CHIPS, 2, r_ // 2, c_)


def _all_gather(bufs, name, per_core=False):
    n = len(bufs)

    def body(*refs):
        i_refs, o_refs = refs[:n], refs[n:2 * n]
        isend, irecv, dsend, drecv, osend, orecv = refs[2 * n:]
        pos = _mesh_pos()
        x, y, c, me, _, _ = pos
        ici = _rider_copies("ici", i_refs, o_refs, isend, irecv, pos)
        d2d = _rider_copies("d2d", o_refs, o_refs, dsend, drecv, pos)
        own = []
        if per_core:
            for b in range(n):
                own.append(tuple(pltpu.make_async_remote_copy(
                    src_ref=s_, dst_ref=d_, send_sem=osend.at[b], recv_sem=orecv.at[b],
                    device_id=(x, y, 1 - c), device_id_type=MESH)
                    for s_, d_ in ((i_refs[b].at[me, c], o_refs[b].at[me, c]),
                                   (o_refs[b].at[me, 1 - c], o_refs[b].at[me, 1 - c]))))
        for cp, _ in ici + own:
            cp.start()
        for (_, land), (fwd, _) in zip(ici, d2d):
            land.wait_recv()
            fwd.start()
        for _, land in d2d + own:
            land.wait_recv()
        for cp, _ in ici + d2d + own:
            cp.wait_send()

    return pl.pallas_call(
        body, in_specs=[ANY] * n, out_specs=[ANY] * n, out_shape=[S(a.shape, a.dtype) for a in bufs],
        input_output_aliases={w: w for w in range(n)}, name=name,
        scratch_shapes=[pltpu.SemaphoreType.DMA((n, 3))] * 4 + [pltpu.SemaphoreType.DMA((n,))] * 2)(*bufs)


def _pair_exchange(grads, name):
    n = len(grads)

    def body(*refs):
        g_refs, theirs = refs[:n], refs[n:2 * n]
        send, recv = refs[2 * n:]
        x, y, c, *_ = _mesh_pos()
        cps = []
        for w in range(n):
            cp = pltpu.make_async_remote_copy(
                src_ref=g_refs[w].at[:, 1 - c], dst_ref=theirs[w], send_sem=send.at[w], recv_sem=recv.at[w],
                device_id=(x, y, 1 - c), device_id_type=MESH)
            cp.start()
            cps.append(cp)
        for cp in cps:
            cp.wait()

    return pl.pallas_call(
        body, in_specs=[ANY] * n, out_specs=[ANY] * n,
        out_shape=[S(a.shape[:1] + a.shape[2:], a.dtype) for a in grads], name=name,
        scratch_shapes=[pltpu.SemaphoreType.DMA((n,))] * 2)(*grads)


def _pair_sum(g, theirs, core, name):
    _, _, rh, c_ = g.shape
    tr = _ew_tile(rh, c_, 2)

    def body(s_ref, g_ref, t_ref, o_ref):
        o_ref[...] = (g_ref[...].astype(F32) + t_ref[...].astype(F32)).astype(o_ref.dtype)

    blk = pl.BlockSpec((None, tr, c_), lambda j, i, s: (j, i, 0))
    gs = pltpu.PrefetchScalarGridSpec(
        num_scalar_prefetch=1, grid=(N_CHIPS, rh // tr),
        in_specs=[pl.BlockSpec((None, None, tr, c_), lambda j, i, s: (j, s[0], i, 0)), blk], out_specs=blk)
    return pl.pallas_call(body, grid_spec=gs, out_shape=S(theirs.shape, theirs.dtype), name=name,
                          compiler_params=_cp("parallel", "parallel"))(core.reshape(1), g, theirs)


def _chip_sum(q, got, l, me, core, into, name):
    _, rh, c_ = got.shape
    tr = _ew_tile(rh, c_, 4)

    def body(s_ref, q_ref, g0_ref, g1_ref, g2_ref, o_ref):
        acc = q_ref[...].astype(F32)
        for r in (g0_ref, g1_ref, g2_ref):
            acc = acc + r[...].astype(F32)
        o_ref[...] = acc

    in_specs = [pl.BlockSpec((None, tr, c_), lambda i, s: (s[0], i, 0))] + [
        pl.BlockSpec((None, tr, c_), functools.partial(lambda k, i, s: (k, i, 0), k)) for k in range(3)]
    return _call_into(
        body, into, in_specs, [jnp.stack([me, core]), q, got, got, got], n_prefetch=1, grid=(rh // tr,),
        out_specs=pl.BlockSpec((None, None, tr, c_), lambda i, s: (l, s[1], i, 0)),
        out_shape=S((DEPTH, 2, rh, c_), F32), name=name, compiler_params=_cp("parallel"))


def _pair_gather(gs4):
    def body(*refs):
        i_refs, o_refs = refs[:NBIG], refs[NBIG:2 * NBIG]
        send, recv = refs[2 * NBIG:]
        x, y, c, *_ = _mesh_pos()
        cps = []
        for w in range(NBIG):
            cp = pltpu.make_async_remote_copy(
                src_ref=i_refs[w].at[:, c], dst_ref=o_refs[w].at[:, c], send_sem=send.at[w], recv_sem=recv.at[w],
                device_id=(x, y, 1 - c), device_id_type=MESH)
            cp.start()
            cps.append(cp)
        for cp in cps:
            cp.wait()

    outs = pl.pallas_call(
        body, in_specs=[ANY] * NBIG, out_specs=[ANY] * NBIG, out_shape=[S(a.shape, a.dtype) for a in gs4],
        input_output_aliases={w: w for w in range(NBIG)}, name="grad_pair_gather",
        scratch_shapes=[pltpu.SemaphoreType.DMA((NBIG,))] * 2)(*gs4)
    return [o.reshape(o.shape[0], 2 * o.shape[2], o.shape[3]) for o in outs]


def _all_reduce_small(p, me, core, name):
    rows = p.shape[0]

    def place(s_ref, p_ref, o_ref):
        o_ref[...] = p_ref[...]

    gs = pltpu.PrefetchScalarGridSpec(
        num_scalar_prefetch=1, grid=(1,), in_specs=[pl.BlockSpec((rows, 128), lambda i, s: (0, 0))],
        out_specs=pl.BlockSpec((None, None, rows, 128), lambda i, s: (s[0], s[1], 0, 0)))
    mine = pl.pallas_call(place, grid_spec=gs, out_shape=S((N_CHIPS, 2, rows, 128), F32), name=name + "_place",
                          compiler_params=_cp("arbitrary"))(jnp.stack([me, core]), p)
    parts = _all_gather([mine], name + "_gather", per_core=True)[0]

    def total(g_ref, o_ref):
        acc = g_ref[0, 0]
        for j in range(N_CHIPS):
            for c in range(2):
                if (j, c) != (0, 0):
                    acc = acc + g_ref[j, c]
        o_ref[...] = acc

    vm = pl.BlockSpec(memory_space=pltpu.VMEM)
    return pl.pallas_call(total, in_specs=[vm], out_specs=vm, out_shape=S((rows, 128), F32), name=name + "_sum",
                          compiler_params=pltpu.CompilerParams(vmem_limit_bytes=VMEM_LIMIT))(parts)


PACK_UNIT = 8 * 128


def _pack(arrs):
    parts = []
    for a in arrs:
        flat = a.reshape(-1)
        pad = (-flat.shape[0]) % PACK_UNIT
        parts.append(jnp.pad(flat, (0, pad)).reshape(-1, 128))
    return jnp.concatenate(parts, axis=0)


def _unpack(buf, shapes):
    outs, row = [], 0
    for shp in shapes:
        n = int(np.prod(shp))
        rows = -(-n // PACK_UNIT) * 8
        outs.append(buf[row:row + rows].reshape(-1)[:n].reshape(shp))
        row += rows
    return outs


SMALL = ("norm1_g", "gm_ln_g", "gm_ln_b", "gm_ws", "gm_bs", "conv_w", "conv_b", "conv_ln_g", "conv_ln_b",
         "norm2_g", "final_g")
WEIGHTS = ("norm1_g", "w_in", "gm_ln_g", "gm_ln_b", "gm_ws", "gm_bs", "conv_w", "conv_b", "conv_ln_g",
           "conv_ln_b", "w_out", "norm2_g", "w_ffn_in", "w_ffn_out", "final_g")


def kernel(x, norm1_g, w_in, gm_ln_g, gm_ln_b, gm_ws, gm_bs, conv_w, conv_b, conv_ln_g, conv_ln_b, w_out, norm2_g, w_ffn_in, w_ffn_out, final_g, loss_target, m_norm1_g, m_w_in, m_gm_ln_g, m_gm_ln_b, m_gm_ws, m_gm_bs, m_conv_w, m_conv_b, m_conv_ln_g, m_conv_ln_b, m_w_out, m_norm2_g, m_w_ffn_in, m_w_ffn_out, m_final_g, v_norm1_g, v_w_in, v_gm_ln_g, v_gm_ln_b, v_gm_ws, v_gm_bs, v_conv_w, v_conv_b, v_conv_ln_g, v_conv_ln_b, v_w_out, v_norm2_g, v_w_ffn_in, v_w_ffn_out, v_final_g):
    given = dict(locals())
    t = x.shape[1]
    xc = x.reshape(t, D)
    target = loss_target.reshape(t, D)
    me = 2 * lax.axis_index("x") + lax.axis_index("y")
    core = lax.axis_index("c")
    tb = _tables(t)

    me = me.astype(jnp.int32)
    core = core.astype(jnp.int32)
    names = [n for n, _ in BIG]
    kinds = dict(BIG)
    gathered = [{n: _cast_to_gathered(given[n], l, me, f"cast_{n}{l}") for n in names} for l in range(DEPTH)]
    gathered[0]["w_in"] = _all_gather([gathered[0]["w_in"]], "all_gather_w_in0")[0]

    def weight(l, n):
        b = gathered[l][n]
        r_, c_ = 2 * b.shape[2], b.shape[3]
        return b.reshape(N_CHIPS, r_, c_) if kinds[n] == "col" else b.reshape(N_CHIPS * r_, c_)

    cshard = CV_W // N_CHIPS
    placed = lax.dynamic_update_slice(jnp.zeros((DEPTH, KCONV, CV_W), F32),
                                      conv_w * (core == 0).astype(F32), (0, 0, me * cshard))
    conv_w_full = _unpack(_all_reduce_small(_pack([placed]), me, core, "gather_conv_w"), [(DEPTH, KCONV, CV_W)])[0]
    cw32 = jnp.pad(conv_w_full, ((0, 0), (0, 32 - KCONV), (0, 0)))

    def row(a, l):
        return a[l].reshape(1, -1)

    saved = []
    early = ["w_in", "w_out", "w_ffn_in"]
    for l in range(DEPTH):
        cur = gathered[l]
        nxt = gathered[l + 1] if l + 1 < DEPTH else None
        sv = {"x": xc}
        bias = jnp.repeat(gm_bs[l].T, GM_W // GM_HEADS, axis=1)
        first = ["w_ffn_in"] if l == 0 else ["w_ffn_out"]
        late = ["w_out", "w_ffn_out"]
        proj, h1t, rid = _norm_mm(xc, row(norm1_g, l), weight(l, "w_in"), F32, f"in_proj{l}", 512,
                             [("ici" if l == 0 else "d2d", [cur[n] for n in first])], (tb["cos2"], tb["sin2"]))
        cur.update(zip(first, rid))
        y_gm, rid = _gm_fwd(proj, row(gm_ln_g, l), row(gm_ln_b, l), gm_ws[l], bias, f"gm_fwd{l}",
                            [("d2d", [cur[n] for n in first])] if l == 0 else ())
        cur.update(zip(first, rid))
        sf, sb = _ret_scan(proj, K_COL, proj, V_COL, tb["zf"], tb["zb"], tb["gcf"], tb["gcb"], f"ret_state{l}")
        a, y_ret, rid = _ret_out(proj, sf, sb, tb, f"ret_out{l}",
                                 [("ici", [cur[n] for n in late])] if l == 0 else ())
        cur.update(zip(late, rid))
        y_cv, hc, rid = _conv_fwd(proj, cw32[l], row(conv_b, l), row(conv_ln_g, l), row(conv_ln_b, l),
                                  f"conv_fwd{l}", [("d2d", [cur[n] for n in late])] if l == 0 else ())
        cur.update(zip(late, rid))
        x_mid = _parts_mm_res([y_gm, y_ret, y_cv], weight(l, "w_out"), xc, f"out_proj{l}")
        ff, h2t, rid = _norm_mm(x_mid, row(norm2_g, l), weight(l, "w_ffn_in"), ACT_DTYPE, f"ffn_in{l}", 512,
                           [("ici", [nxt[n] for n in early])] if nxt else ())
        if nxt:
            nxt.update(zip(early, rid))
        xc, rid = _swiglu_mm_res(ff, weight(l, "w_ffn_out"), x_mid, f"ffn_out{l}",
                                 [("d2d", [nxt[n] for n in early]), ("ici", [nxt["w_ffn_out"]])] if nxt else ())
        if nxt:
            nxt.update(zip(early + ["w_ffn_out"], rid))
        sv.update(bias=bias, proj=proj, h1t=h1t, h2t=h2t, y_gm=y_gm, sf=sf, sb=sb, a=a, y_ret=y_ret, y_cv=y_cv,
                  hc=hc, x_mid=x_mid,
                  ff=ff)
        saved.append(sv)

    dx, d_final_g, lpart = _loss_head(xc, final_g.reshape(1, D), target, "loss_head")

    small_g = {n: [None] * DEPTH for n in SMALL}
    qs = [{} for _ in range(DEPTH)]
    got = [{} for _ in range(DEPTH)]
    ffn_w, mix_w = ["w_ffn_out", "w_ffn_in"], ["w_out", "w_in"]

    def halves(big_g, group):
        return [big_g[n].reshape(N_CHIPS, 2, given[n].shape[1] // 2, given[n].shape[2]) for n in group]

    def pair_sums(l, group, g4, theirs):
        qs[l].update({n: _pair_sum(g, th, core, f"pair_sum_{n}{l}") for n, g, th in zip(group, g4, theirs)})
        return [qs[l][n] for n in group]

    for l in reversed(range(DEPTH)):
        sv = saved[l]
        proj = sv["proj"]
        big_g = {}
        dff = _dx_swiglu(dx, weight(l, "w_ffn_out"), sv["ff"], f"ffn_out_dx{l}")
        big_g["w_ffn_out"] = _dw_swiglu(sv["ff"], dx, f"ffn_out_dw{l}")
        dx_mid, dg2, _ = _dx_norm([dff], weight(l, "w_ffn_in"), sv["x_mid"], row(norm2_g, l), dx,
                                  f"ffn_in_dx{l}", 512)
        big_g["w_ffn_in"] = _dw_norm_cols(sv["h2t"], dff, w_ffn_in.shape[2], f"ffn_in_dw{l}")
        g4 = halves(big_g, ffn_w)
        (dy_gm, dy_ret, dy_cv), theirs = _dx_parts(dx_mid, weight(l, "w_out"), [GM_W, RET_W, CV_W],
                                                   f"out_proj_dx{l}", [("pairx", g4)])
        q_ffn = pair_sums(l, ffn_w, g4, theirs)
        big_g["w_out"] = _dw_parts([sv["y_gm"], sv["y_ret"], sv["y_cv"]], dx_mid, f"out_proj_dw{l}")
        d_cv, dcw, dcb, dclg, dclb, rid = _conv_bwd(proj, dy_cv, sv["hc"], cw32[l], row(conv_ln_g, l),
                                                    row(conv_ln_b, l), f"conv_bwd{l}", [("scatter", q_ffn[:1])])
        got[l].update(zip(ffn_w[:1], rid))
        da, d_g = _ret_bwd_pre(dy_ret, sv["a"], proj, f"ret_bwd_pre{l}")
        gb_, gf_ = _ret_scan(proj, Q_COL, da, 0, tb["xib"], tb["xif"], tb["gcb"], tb["gcf"], f"ret_bwd_state{l}")
        d_qkv, rid = _ret_bwd_main(proj, da, sv["sf"], sv["sb"], gf_, gb_, tb, f"ret_bwd_main{l}",
                                   [("scatter", q_ffn[1:])])
        got[l].update(zip(ffn_w[1:], rid))
        d_gm, dws, dbs, dglg, dglb = _gm_bwd(proj, dy_gm, row(gm_ln_g, l), row(gm_ln_b, l), gm_ws[l],
                                             jnp.swapaxes(gm_ws[l], 1, 2), sv["bias"], f"gm_bwd{l}")
        dparts = [d_gm, d_qkv, d_g, d_cv]
        big_g["w_in"] = _dw_norm_parts(sv["h1t"], dparts, w_in.shape[2], f"in_proj_dw{l}")
        g4 = halves(big_g, mix_w)
        q_mix = pair_sums(l, mix_w, g4, _pair_exchange(g4, f"grad_pair_exchange_mix{l}"))
        dx, dg1, rid = _dx_norm(dparts, weight(l, "w_in"), sv["x"], row(norm1_g, l), dx_mid, f"in_proj_dx{l}", 512,
                                [("scatter", q_mix)])
        got[l].update(zip(mix_w, rid))
        for n, val in (("norm1_g", dg1[0]), ("gm_ln_g", dglg[0]), ("gm_ln_b", dglb[0]), ("gm_ws", dws),
                       ("gm_bs", dbs[:, :GM_HEADS].T), ("conv_w", dcw[:KCONV]), ("conv_b", dcb[0]),
                       ("conv_ln_g", dclg[0]), ("conv_ln_b", dclb[0]), ("norm2_g", dg2[0])):
            small_g[n][l] = val

    small_shapes = [given[n].shape if n != "conv_w" else (DEPTH, KCONV, CV_W) for n in SMALL]
    partials = [d_final_g[0] if n == "final_g" else jnp.stack(small_g[n]) for n in SMALL]
    summed = _unpack(_all_reduce_small(_pack(partials + [lpart]), me, core, "all_reduce_small_grads"),
                     small_shapes + [lpart.shape])
    loss = summed[-1][0, 0]
    reduced = dict(zip(SMALL, summed))
    reduced["conv_w"] = lax.dynamic_slice(reduced["conv_w"], (0, 0, me * cshard), (DEPTH, KCONV, cshard))

    halves = [None] * NBIG
    for l in reversed(range(DEPTH)):
        halves = [_chip_sum(qs[l][n], got[l][n], l, me, core, h, f"chip_sum_{n}{l}") for n, h in zip(names, halves)]
    grads = dict(zip(names, _pair_gather(halves)))
    grads.update(reduced)

    delta, new_m, new_v = {}, {}, {}
    for n, _ in BIG:
        delta[n], new_m[n], new_v[n] = _adamw(given[n], grads[n], given["m_" + n], given["v_" + n], f"adamw_{n}")
    shapes = [given[n].shape for n in SMALL]
    packed = [_pack([src[n] if src is grads else src[p + n] for n in SMALL])
              for src, p in ((given, ""), (grads, ""), (given, "m_"), (given, "v_"))]
    outs = _adamw(*packed, "adamw_small")
    for dst, buf in zip((delta, new_m, new_v), outs):
        dst.update(zip(SMALL, _unpack(buf, shapes)))

    return (loss, dx.reshape(1, t, D), *[grads[n] for n in WEIGHTS], *[delta[n] for n in WEIGHTS],
            *[new_m[n] for n in WEIGHTS], *[new_v[n] for n in WEIGHTS])
```

```python
import functools
import math

import numpy as np
import jax
import jax.numpy as jnp
from jax import lax
from jax.experimental import pallas as pl
from jax.experimental.pallas import tpu as pltpu

F32 = jnp.float32
BF16 = jnp.bfloat16
MXU_DTYPE = BF16
ACT_DTYPE = BF16
S = jax.ShapeDtypeStruct

D = 1024
DEPTH = 2
GM_W = 256
GM_HEADS = 4
RET_W = 512
HEADS = 4
HD = 128
CV_W = 256
KCONV = 31
IN_W = 2 * GM_W + 4 * RET_W + 2 * CV_W
FFN_H = 2816
CH = 128
ROPE_BASE = 10000.0
EPS = 1e-6
N_CHIPS = 4
N_DEV = 8
HALO = 16

ADAM_LR = 0.001
ADAM_B1 = 0.9
ADAM_B2 = 0.999
ADAM_EPS = 1e-08
ADAM_WD = 0.01
ADAM_STEP = 10

VMEM_LIMIT = 52 * 1024 * 1024
MESH = pl.DeviceIdType.MESH


def _cp(*sem, vmem=VMEM_LIMIT):
    return pltpu.CompilerParams(dimension_semantics=tuple(sem), vmem_limit_bytes=vmem)


def _mx(a):
    return a.astype(MXU_DTYPE)


def _dot(a, b):
    return jnp.dot(_mx(a), _mx(b), preferred_element_type=F32)


def _dot_nt(a, b):
    return lax.dot_general(_mx(a), _mx(b), (((1,), (1,)), ((), ())), preferred_element_type=F32)


def _dot_tn(a, b):
    return lax.dot_general(_mx(a), _mx(b), (((0,), (0,)), ((), ())), preferred_element_type=F32)


def _sigmoid(x):
    return 1.0 / (1.0 + jnp.exp(-x))


def _gelu(x):
    return 0.5 * x * (1.0 + lax.erf(x * (1.0 / math.sqrt(2.0))))


def _gelu_grad(x):
    return 0.5 * (1.0 + lax.erf(x * (1.0 / math.sqrt(2.0)))) + x * jnp.exp(-0.5 * x * x) * (1.0 / math.sqrt(2.0 * math.pi))


def _rms_r(x):
    return lax.rsqrt(jnp.mean(x * x, axis=-1, keepdims=True) + EPS)


def _rms_bwd(dh, x, r, g):
    u = dh * g
    dx = r * u - x * (r * r * r) * jnp.mean(u * x, axis=-1, keepdims=True)
    return dx, dh * x * r


def _standardize(a):
    mu = jnp.mean(a, axis=-1, keepdims=True)
    d = a - mu
    r = lax.rsqrt(jnp.mean(d * d, axis=-1, keepdims=True) + EPS)
    return d * r, r


def _standardize_bwd(do, o, r):
    return r * (do - jnp.mean(do, axis=-1, keepdims=True) - o * jnp.mean(do * o, axis=-1, keepdims=True))


def _acc_out(ref, val, first):
    @pl.when(first)
    def _():
        ref[...] = val

    @pl.when(jnp.logical_not(first))
    def _():
        ref[...] += val


def _row_tile(t, pref):
    tm = min(t, pref)
    assert t % tm == 0, (t, tm)
    return tm


def _segments(part_widths, shard_w):
    bounds = {0}
    off = 0
    for w in part_widths:
        off += w
        bounds.add(off)
    total = off
    for j in range(1, total // shard_w + 1):
        bounds.add(j * shard_w)
    bounds = sorted(bounds)
    starts = np.cumsum([0] + list(part_widths))
    segs = []
    for a, b in zip(bounds[:-1], bounds[1:]):
        p = int(np.searchsorted(starts, a, side="right") - 1)
        segs.append((p, a - int(starts[p]), a // shard_w, a % shard_w, b - a))
    return segs


ANY = pl.BlockSpec(memory_space=pl.ANY)


def _mesh_pos():
    x, y, c = lax.axis_index("x"), lax.axis_index("y"), lax.axis_index("c")
    chips = [(1 - x, y), (x, 1 - y), (1 - x, 1 - y)]
    return x, y, c, 2 * x + y, chips, [2 * cx + cy for cx, cy in chips]


def _rider_copies(kind, i_refs, o_refs, send, recv, pos):
    x, y, c, me, chips, cj = pos
    out = []
    for b, (i_ref, o_ref) in enumerate(zip(i_refs, o_refs)):
        for k in range(1 if kind == "pairx" else 3):
            if kind == "ici":
                src, dst, land, dev = i_ref.at[me, c], o_ref.at[me, c], o_ref.at[cj[k], c], (*chips[k], c)
            elif kind == "d2d":
                src, dst, land, dev = i_ref.at[cj[k], c], o_ref.at[cj[k], c], o_ref.at[cj[k], 1 - c], (x, y, 1 - c)
            elif kind == "pairx":
                src, dst, land, dev = i_ref.at[:, 1 - c], o_ref, o_ref, (x, y, 1 - c)
            else:
                src, dst, land, dev = i_ref.at[cj[k]], o_ref.at[k], o_ref.at[k], (*chips[k], c)
            out.append(tuple(pltpu.make_async_remote_copy(
                src_ref=s_, dst_ref=d_, send_sem=send.at[b, k], recv_sem=recv.at[b, k],
                device_id=dev, device_id_type=MESH) for s_, d_ in ((src, dst), (land, land))))
    return out


def _rider_out_shape(kind, a):
    if kind == "scatter":
        return S((3,) + a.shape[1:], a.dtype)
    if kind == "pairx":
        return S(a.shape[:1] + a.shape[2:], a.dtype)
    return S(a.shape, a.dtype)


def _pcall(body, args, riders, *, grid, in_specs, out_specs, out_shape, name, sem, scratch_shapes=()):
    outs = list(out_shape)
    if not riders:
        res = pl.pallas_call(body, grid=grid, in_specs=in_specs, out_specs=out_specs, out_shape=outs, name=name,
                             scratch_shapes=list(scratch_shapes), compiler_params=_cp(*sem))(*args)
        return res, []
    r_in = [a for _, bufs in riders for a in bufs]
    r_out = [_rider_out_shape(kind, a) for kind, bufs in riders for a in bufs]
    n_in, n_out, n_scr, n_r = len(args), len(outs), len(scratch_shapes), len(r_in)
    aliases, idx = {}, 0
    for kind, bufs in riders:
        for _ in bufs:
            if kind in ("ici", "d2d"):
                aliases[n_in + idx] = n_out + idx
            idx += 1
    sems = [pltpu.SemaphoreType.DMA((len(bufs), 3)) for _, bufs in riders for _ in range(2)]

    def wrapped(*refs):
        a, ri = refs[:n_in], refs[n_in:n_in + n_r]
        o, ro = refs[n_in + n_r:n_in + n_r + n_out], refs[n_in + n_r + n_out:n_in + 2 * n_r + n_out]
        scr = refs[n_in + 2 * n_r + n_out:n_in + 2 * n_r + n_out + n_scr]
        sm = refs[n_in + 2 * n_r + n_out + n_scr:]
        pos = _mesh_pos()
        copies, off = [], 0
        for r, (kind, bufs) in enumerate(riders):
            copies += _rider_copies(kind, ri[off:off + len(bufs)], ro[off:off + len(bufs)], sm[2 * r], sm[2 * r + 1], pos)
            off += len(bufs)
        ids = [pl.program_id(d) for d in range(len(grid))]
        first = functools.reduce(jnp.logical_and, [i == 0 for i in ids])
        last = functools.reduce(jnp.logical_and, [i == n - 1 for i, n in zip(ids, grid)])

        @pl.when(first)
        def _():
            for cp, _ in copies:
                cp.start()

        body(*a, *o, *scr)

        @pl.when(last)
        def _():
            for cp, land in copies:
                land.wait_recv()
                cp.wait_send()

    res = pl.pallas_call(
        wrapped, grid=grid, in_specs=list(in_specs) + [ANY] * n_r, out_specs=list(out_specs) + [ANY] * n_r,
        out_shape=outs + r_out, input_output_aliases=aliases, name=name,
        scratch_shapes=list(scratch_shapes) + sems, compiler_params=_cp(*(("arbitrary",) * len(grid))))(*args, *r_in)
    return res[:n_out], res[n_out:]


def _wcol_spec(w):
    return pl.BlockSpec(w.shape, lambda *_: (0, 0, 0))


def _wrow_spec(w):
    return pl.BlockSpec(w.shape, lambda *_: (0, 0))


def _norm_mm(x, g, w, out_dtype, name, tm_pref, riders=(), rope=None):
    t = x.shape[0]
    nc = w.shape[2]
    tm = _row_tile(t, tm_pref)
    extra = list(rope) if rope else []

    def body(x_ref, g_ref, w_ref, *rest):
        o_ref, ht_ref = rest[-2], rest[-1]
        xv = x_ref[...]
        hf = xv * _rms_r(xv) * g_ref[...]
        h = _mx(hf)
        for j in range(N_CHIPS):
            o_ref[:, j * nc:(j + 1) * nc] = jnp.dot(h, w_ref[j], preferred_element_type=F32).astype(o_ref.dtype)
        if rope:
            _rotate_qk(o_ref, rest[0][...], rest[1][...])
        ht_ref[...] = hf.T.astype(ht_ref.dtype)

    (out, ht), rid = _pcall(
        body, [x, g, w] + extra, riders, grid=(t // tm,),
        in_specs=[pl.BlockSpec((tm, D), lambda i: (i, 0)), pl.BlockSpec((1, D), lambda i: (0, 0)), _wcol_spec(w)]
        + [pl.BlockSpec((tm, HD), lambda i: (i, 0)) for _ in extra],
        out_specs=[pl.BlockSpec((tm, N_CHIPS * nc), lambda i: (i, 0)), pl.BlockSpec((D, tm), lambda i: (0, i))],
        out_shape=[S((t, N_CHIPS * nc), out_dtype), S((D, t), MXU_DTYPE)], name=name, sem=("parallel",))
    return out, ht, rid


def _parts_mm_res(parts, w, res, name):
    t = res.shape[0]
    tm = _row_tile(t, 512)
    widths = [p.shape[1] for p in parts]
    offs = np.cumsum([0] + widths)
    n = len(parts)

    def body(*refs):
        p_refs, w_ref, r_ref, o_ref = refs[:n], refs[n], refs[n + 1], refs[n + 2]
        acc = r_ref[...]
        for p in range(n):
            acc = acc + _dot(p_refs[p][...], w_ref[int(offs[p]):int(offs[p + 1]), :])
        o_ref[...] = acc

    return pl.pallas_call(
        body, grid=(t // tm,),
        in_specs=[pl.BlockSpec((tm, wd), lambda i: (i, 0)) for wd in widths]
        + [_wrow_spec(w), pl.BlockSpec((tm, D), lambda i: (i, 0))],
        out_specs=pl.BlockSpec((tm, D), lambda i: (i, 0)),
        out_shape=S((t, D), F32), name=name, compiler_params=_cp("parallel"))(*parts, w, res)


def _swiglu(ff):
    gate = ff[:, :FFN_H].astype(F32)
    up = ff[:, FFN_H:].astype(F32)
    return gate * _sigmoid(gate) * up


def _swiglu_mm_res(ff, w, res, name, riders=()):
    t = res.shape[0]
    tm = _row_tile(t, 512)

    def body(f_ref, w_ref, r_ref, o_ref, at_ref):
        act = _swiglu(f_ref[...])
        o_ref[...] = r_ref[...] + _dot(act, w_ref[...])
        at_ref[...] = act.T.astype(at_ref.dtype)

    (out, at), rid = _pcall(
        body, [ff, w, res], riders, grid=(t // tm,),
        in_specs=[pl.BlockSpec((tm, 2 * FFN_H), lambda i: (i, 0)), _wrow_spec(w),
                  pl.BlockSpec((tm, D), lambda i: (i, 0))],
        out_specs=[pl.BlockSpec((tm, D), lambda i: (i, 0)), pl.BlockSpec((FFN_H, tm), lambda i: (0, i))],
        out_shape=[S((t, D), F32), S((FFN_H, t), MXU_DTYPE)], name=name, sem=("parallel",))
    return out, at, rid


def _dx_norm(dparts, w, x, g, dres, name, tm_pref, riders=()):
    t = x.shape[0]
    nc = w.shape[2]
    tm = _row_tile(t, tm_pref)
    widths = [p.shape[1] for p in dparts]
    segs = _segments(widths, nc)
    n = len(dparts)

    def body(*refs):
        d_refs = refs[:n]
        w_ref, x_ref, g_ref, r_ref, dx_ref, dg_ref = refs[n:]
        dh = jnp.zeros((tm, D), F32)
        for (p, po, j, jo, wd) in segs:
            dh = dh + _dot_nt(d_refs[p][:, po:po + wd], w_ref[j, :, jo:jo + wd])
        xv = x_ref[...]
        dx, dgrow = _rms_bwd(dh, xv, _rms_r(xv), g_ref[...])
        dx_ref[...] = r_ref[...] + dx
        _acc_out(dg_ref, jnp.sum(dgrow, axis=0, keepdims=True), pl.program_id(0) == 0)

    (dx, dg), rid = _pcall(
        body, [*dparts, w, x, g, dres], riders, grid=(t // tm,),
        in_specs=[pl.BlockSpec((tm, wd), lambda i: (i, 0)) for wd in widths]
        + [_wcol_spec(w), pl.BlockSpec((tm, D), lambda i: (i, 0)),
           pl.BlockSpec((1, D), lambda i: (0, 0)), pl.BlockSpec((tm, D), lambda i: (i, 0))],
        out_specs=[pl.BlockSpec((tm, D), lambda i: (i, 0)), pl.BlockSpec((1, D), lambda i: (0, 0))],
        out_shape=[S((t, D), F32), S((1, D), F32)], name=name, sem=("arbitrary",))
    return dx, dg, rid


def _dx_parts(dy, w, widths, name, riders=()):
    t = dy.shape[0]
    tm = _row_tile(t, 512)
    offs = np.cumsum([0] + list(widths))
    n = len(widths)

    def body(dy_ref, w_ref, *o_refs):
        dyv = _mx(dy_ref[...])
        for p in range(n):
            o_refs[p][...] = _dot_nt(dyv, w_ref[int(offs[p]):int(offs[p + 1]), :])

    return _pcall(
        body, [dy, w], riders, grid=(t // tm,),
        in_specs=[pl.BlockSpec((tm, D), lambda i: (i, 0)), _wrow_spec(w)],
        out_specs=[pl.BlockSpec((tm, wd), lambda i: (i, 0)) for wd in widths],
        out_shape=[S((t, wd), F32) for wd in widths], name=name, sem=("parallel",))


def _dx_swiglu(dy, w, ff, name):
    t = dy.shape[0]
    tm = _row_tile(t, 512)

    def body(dy_ref, w_ref, f_ref, o_ref):
        dact = _dot_nt(dy_ref[...], w_ref[...])
        gate = f_ref[:, :FFN_H].astype(F32)
        up = f_ref[:, FFN_H:].astype(F32)
        s = _sigmoid(gate)
        gs = gate * s
        o_ref[:, :FFN_H] = ((dact * up) * (s + gs - gs * s)).astype(o_ref.dtype)
        o_ref[:, FFN_H:] = (dact * gs).astype(o_ref.dtype)

    return pl.pallas_call(
        body, grid=(t // tm,),
        in_specs=[pl.BlockSpec((tm, D), lambda i: (i, 0)), _wrow_spec(w),
                  pl.BlockSpec((tm, 2 * FFN_H), lambda i: (i, 0))],
        out_specs=pl.BlockSpec((tm, 2 * FFN_H), lambda i: (i, 0)),
        out_shape=S((t, 2 * FFN_H), ACT_DTYPE), name=name, compiler_params=_cp("parallel"))(dy, w, ff)


def _call_into(body, into, in_specs, args, *, n_prefetch, grid, out_specs, **kw):
    n_in = len(args)
    if into is None:
        gs = pltpu.PrefetchScalarGridSpec(num_scalar_prefetch=n_prefetch, grid=grid, in_specs=in_specs,
                                          out_specs=out_specs)
        return pl.pallas_call(body, grid_spec=gs, **kw)(*args)

    def wrapped(*refs):
        return body(*refs[:n_in], *refs[n_in + 1:])

    gs = pltpu.PrefetchScalarGridSpec(num_scalar_prefetch=n_prefetch, grid=grid,
                                      in_specs=list(in_specs) + [ANY], out_specs=out_specs)
    return pl.pallas_call(wrapped, grid_spec=gs, input_output_aliases={n_in: 0}, **kw)(*args, into)


def _dw_norm_parts(ht, dparts, nc, name):
    t = ht.shape[1]
    tk = _row_tile(t, 1024)
    widths = [p.shape[1] for p in dparts]
    segs = _segments(widths, nc)
    n = len(dparts)
    nk = t // tk

    def body(*refs):
        h_ref, d_refs, o_ref, acc_ref = refs[0], refs[1:1 + n], refs[1 + n], refs[2 + n]
        k = pl.program_id(0)
        h = h_ref[...]

        @pl.when(k == 0)
        def _():
            acc_ref[...] = jnp.zeros_like(acc_ref)

        for (p, po, j, jo, wd) in segs:
            acc_ref[j, :, jo:jo + wd] += _dot(h, d_refs[p][:, po:po + wd])

        @pl.when(k == nk - 1)
        def _():
            o_ref[...] = acc_ref[...].astype(o_ref.dtype)

    return pl.pallas_call(
        body, grid=(nk,),
        in_specs=[pl.BlockSpec((D, tk), lambda k: (0, k))]
        + [pl.BlockSpec((tk, wd), lambda k: (k, 0)) for wd in widths],
        out_specs=pl.BlockSpec((N_CHIPS, D, nc), lambda k: (0, 0, 0)),
        out_shape=S((N_CHIPS, D, nc), MXU_DTYPE), name=name,
        scratch_shapes=[pltpu.VMEM((N_CHIPS, D, nc), F32)], compiler_params=_cp("arbitrary"))(ht, *dparts)


def _dw_norm_cols(ht, dy, nc, name):
    t = ht.shape[1]
    tk = _row_tile(t, 2048)
    nk = t // tk

    def body(h_ref, dy_ref, o_ref, acc_ref):
        k = pl.program_id(1)

        @pl.when(k == 0)
        def _():
            acc_ref[...] = jnp.zeros_like(acc_ref)

        acc_ref[...] += _dot(h_ref[...], dy_ref[...])

        @pl.when(k == nk - 1)
        def _():
            o_ref[...] = acc_ref[...].astype(o_ref.dtype)

    return pl.pallas_call(
        body, grid=(N_CHIPS, nk),
        in_specs=[pl.BlockSpec((D, tk), lambda j, k: (0, k)), pl.BlockSpec((tk, nc), lambda j, k: (k, j))],
        out_specs=pl.BlockSpec((None, D, nc), lambda j, k: (j, 0, 0)),
        out_shape=S((N_CHIPS, D, nc), MXU_DTYPE), name=name,
        scratch_shapes=[pltpu.VMEM((D, nc), F32)], compiler_params=_cp("parallel", "arbitrary"))(ht, dy)


def _dw_parts(parts, dy, name):
    t = dy.shape[0]
    tk = _row_tile(t, 1024)
    widths = [p.shape[1] for p in parts]
    offs = np.cumsum([0] + widths)
    ktot = int(offs[-1])
    n = len(parts)
    nk = t // tk

    def body(*refs):
        p_refs, dy_ref, o_ref, acc_ref = refs[:n], refs[n], refs[n + 1], refs[n + 2]
        k = pl.program_id(0)

        @pl.when(k == 0)
        def _():
            acc_ref[...] = jnp.zeros_like(acc_ref)

        dyv = _mx(dy_ref[...])
        for p in range(n):
            acc_ref[int(offs[p]):int(offs[p + 1]), :] += _dot_tn(p_refs[p][...], dyv)

        @pl.when(k == nk - 1)
        def _():
            o_ref[...] = acc_ref[...].astype(o_ref.dtype)

    return pl.pallas_call(
        body, grid=(nk,),
        in_specs=[pl.BlockSpec((tk, wd), lambda k: (k, 0)) for wd in widths]
        + [pl.BlockSpec((tk, D), lambda k: (k, 0))],
        out_specs=pl.BlockSpec((ktot, D), lambda k: (0, 0)),
        out_shape=S((ktot, D), MXU_DTYPE), name=name,
        scratch_shapes=[pltpu.VMEM((ktot, D), F32)], compiler_params=_cp("arbitrary"))(*parts, dy)


def _dw_swiglu(at, dy, name):
    t = dy.shape[0]
    tk = _row_tile(t, 1024)
    nk = t // tk

    def body(a_ref, dy_ref, o_ref, acc_ref):
        k = pl.program_id(0)

        @pl.when(k == 0)
        def _():
            acc_ref[...] = jnp.zeros_like(acc_ref)

        acc_ref[...] += _dot(a_ref[...], dy_ref[...])

        @pl.when(k == nk - 1)
        def _():
            o_ref[...] = acc_ref[...].astype(o_ref.dtype)

    return pl.pallas_call(
        body, grid=(nk,),
        in_specs=[pl.BlockSpec((FFN_H, tk), lambda k: (0, k)), pl.BlockSpec((tk, D), lambda k: (k, 0))],
        out_specs=pl.BlockSpec((FFN_H, D), lambda k: (0, 0)),
        out_shape=S((FFN_H, D), MXU_DTYPE), name=name,
        scratch_shapes=[pltpu.VMEM((FFN_H, D), F32)], compiler_params=_cp("arbitrary"))(at, dy)


def _tables(t):
    half = HD // 2
    inv_freq = ROPE_BASE ** (-jnp.arange(half, dtype=F32) / half)
    base = (jnp.arange(t // CH, dtype=F32) * CH)[:, None] * inv_freq[None, :]
    off = jnp.arange(CH, dtype=F32)[:, None] * inv_freq[None, :]
    cb, sb, co, so = jnp.cos(base)[:, None], jnp.sin(base)[:, None], jnp.cos(off)[None], jnp.sin(off)[None]
    cos = (cb * co - sb * so).reshape(t, half)
    sin = (sb * co + cb * so).reshape(t, half)
    tb = {"cos2": jnp.concatenate([cos, cos], axis=1), "sin2": jnp.concatenate([-sin, sin], axis=1)}
    gf = 1.0 - jnp.exp2(-5.0 - jnp.arange(HEADS, dtype=F32))
    lgf = jnp.log(gf)[:, None]
    lgb = jnp.log(gf[::-1])[:, None]
    idx = jnp.arange(CH, dtype=F32)
    diff = idx[:, None] - idx[None, :]
    dfwd = jnp.where(diff >= 0, jnp.exp(lgf[:, :, None] * jnp.where(diff >= 0, diff, 0.0)), 0.0)
    dbwd = jnp.where(diff < 0, jnp.exp(lgb[:, :, None] * jnp.where(diff < 0, -diff, 0.0)), 0.0)
    tb["dm"] = dfwd + dbwd
    tb["dmt"] = jnp.swapaxes(tb["dm"], 1, 2)

    def lanes(a):
        return jnp.repeat(a.T, HD, axis=1)

    tb["xif"] = lanes(jnp.exp(lgf * (idx + 1)))
    tb["zf"] = lanes(jnp.exp(lgf * (CH - 1 - idx)))
    tb["xib"] = lanes(jnp.exp(lgb * (CH - idx)))
    tb["zb"] = lanes(jnp.exp(lgb * idx))
    tb["gcf"] = jnp.repeat(jnp.exp(lgf * CH), HD, axis=0).reshape(1, HEADS * HD)
    tb["gcb"] = jnp.repeat(jnp.exp(lgb * CH), HD, axis=0).reshape(1, HEADS * HD)
    return tb


def _full(shape):
    nd = len(shape)
    return pl.BlockSpec(shape, lambda *_: (0,) * nd)


def _gm_mixed(vn, ws_ref, bias):
    lane = lax.broadcasted_iota(jnp.int32, (CH, 128), 1)
    halves = []
    for hf in range(2):
        vh = _mx(vn[:, hf * 128:(hf + 1) * 128])
        r0 = jnp.dot(_mx(ws_ref[2 * hf]), vh, preferred_element_type=F32)
        r1 = jnp.dot(_mx(ws_ref[2 * hf + 1]), vh, preferred_element_type=F32)
        halves.append(jnp.where(lane < 64, r0, r1))
    return jnp.concatenate(halves, axis=1) + bias


def _gm_fwd(proj, ln_g, ln_b, ws, bias, name, riders=()):
    t = proj.shape[0]
    tm = _row_tile(t, 512)

    def body(pu_ref, pv_ref, g_ref, b_ref, ws_ref, bias_ref, o_ref):
        for c in range(tm // CH):
            rows = slice(c * CH, (c + 1) * CH)
            u = _gelu(pu_ref[rows, :])
            o, _ = _standardize(_gelu(pv_ref[rows, :]))
            vn = o * g_ref[...] + b_ref[...]
            o_ref[rows, :] = (u * _gm_mixed(vn, ws_ref, bias_ref[...])).astype(o_ref.dtype)

    (out,), rid = _pcall(
        body, [proj, proj, ln_g, ln_b, ws, bias], riders, grid=(t // tm,),
        in_specs=[pl.BlockSpec((tm, GM_W), lambda i: (i, 0)), pl.BlockSpec((tm, GM_W), lambda i: (i, 1)),
                  _full((1, GM_W)), _full((1, GM_W)), _full((GM_HEADS, CH, CH)), _full((CH, GM_W))],
        out_specs=[pl.BlockSpec((tm, GM_W), lambda i: (i, 0))],
        out_shape=[S((t, GM_W), ACT_DTYPE)], name=name, sem=("parallel",))
    return out, rid


def _gm_bwd(proj, dy, ln_g, ln_b, ws, wst, bias, name):
    t = proj.shape[0]
    tm = _row_tile(t, 512)
    nb = t // tm

    def body(pu_ref, pv_ref, dy_ref, g_ref, b_ref, ws_ref, wst_ref, bias_ref,
             d_ref, dws_ref, dbs_ref, dg_ref, db_ref, dbias_ref):
        first = pl.program_id(0) == 0
        lane = lax.broadcasted_iota(jnp.int32, (CH, 128), 1)
        dws = [jnp.zeros((CH, CH), F32) for _ in range(GM_HEADS)]
        dbias = jnp.zeros((CH, GM_W), F32)
        dg = jnp.zeros((1, GM_W), F32)
        db = jnp.zeros((1, GM_W), F32)
        for c in range(tm // CH):
            rows = slice(c * CH, (c + 1) * CH)
            pu = pu_ref[rows, :]
            pv = pv_ref[rows, :]
            u = _gelu(pu)
            o, r = _standardize(_gelu(pv))
            vn = o * g_ref[...] + b_ref[...]
            mixed = _gm_mixed(vn, ws_ref, bias_ref[...])
            dyv = dy_ref[rows, :]
            d_ref[rows, :GM_W] = (dyv * mixed * _gelu_grad(pu)).astype(d_ref.dtype)
            dmixed = dyv * u
            dbias = dbias + dmixed
            dvn_halves = []
            for hf in range(2):
                dm = dmixed[:, hf * 128:(hf + 1) * 128]
                vh = vn[:, hf * 128:(hf + 1) * 128]
                dm0 = jnp.where(lane < 64, dm, 0.0)
                dm1 = dm - dm0
                dws[2 * hf] = dws[2 * hf] + _dot_nt(dm0, vh)
                dws[2 * hf + 1] = dws[2 * hf + 1] + _dot_nt(dm1, vh)
                t0 = _dot(wst_ref[2 * hf], dm)
                t1 = _dot(wst_ref[2 * hf + 1], dm)
                dvn_halves.append(jnp.where(lane < 64, t0, t1))
            dvn = jnp.concatenate(dvn_halves, axis=1)
            dg = dg + jnp.sum(dvn * o, axis=0, keepdims=True)
            db = db + jnp.sum(dvn, axis=0, keepdims=True)
            dv = _standardize_bwd(dvn * g_ref[...], o, r)
            d_ref[rows, GM_W:] = (dv * _gelu_grad(pv)).astype(d_ref.dtype)
        for h in range(GM_HEADS):
            _acc_out(dws_ref.at[h], dws[h], first)
        _acc_out(dbias_ref, dbias, first)
        _acc_out(dg_ref, dg, first)
        _acc_out(db_ref, db, first)

        @pl.when(pl.program_id(0) == nb - 1)
        def _():
            tot = dbias_ref[...]
            head = lax.broadcasted_iota(jnp.int32, (CH, GM_W), 1) // (GM_W // GM_HEADS)
            out = jnp.zeros((CH, 128), F32)
            for h in range(GM_HEADS):
                s = jnp.sum(jnp.where(head == h, tot, 0.0), axis=1, keepdims=True)
                out = jnp.where(lane == h, s, out)
            dbs_ref[...] = out

    return pl.pallas_call(
        body, grid=(nb,),
        in_specs=[pl.BlockSpec((tm, GM_W), lambda i: (i, 0)), pl.BlockSpec((tm, GM_W), lambda i: (i, 1)),
                  pl.BlockSpec((tm, GM_W), lambda i: (i, 0)),
                  _full((1, GM_W)), _full((1, GM_W)), _full((GM_HEADS, CH, CH)), _full((GM_HEADS, CH, CH)),
                  _full((CH, GM_W))],
        out_specs=[pl.BlockSpec((tm, 2 * GM_W), lambda i: (i, 0)), _full((GM_HEADS, CH, CH)), _full((CH, 128)),
                   _full((1, GM_W)), _full((1, GM_W))],
        out_shape=[S((t, 2 * GM_W), ACT_DTYPE), S((GM_HEADS, CH, CH), F32), S((CH, 128), F32),
                   S((1, GM_W), F32), S((1, GM_W), F32)],
        scratch_shapes=[pltpu.VMEM((CH, GM_W), F32)],
        name=name, compiler_params=_cp("arbitrary"))(proj, proj, dy, ln_g, ln_b, ws, wst, bias)


def _rot(x, cos2, sin2):
    return x * cos2 + pltpu.roll(x, HD // 2, 1) * sin2


def _rot_bwd(dx, cos2, sin2):
    return dx * cos2 + pltpu.roll(dx * sin2, HD // 2, 1)


Q_COL, K_COL, V_COL, GATE_COL = 1, 2, 3, 4


def _rotate_qk(o_ref, cos2, sin2):
    for col, scale in ((Q_COL, 1.0), (K_COL, HD ** -0.5)):
        for h in range(HEADS):
            cols = slice(col * RET_W + h * HD, col * RET_W + (h + 1) * HD)
            o_ref[:, cols] = _rot(o_ref[:, cols], cos2, sin2) * scale


def _ret_scan(lhs, lhs_col, rhs, rhs_col, lp, ls, gp, gs, name):
    t = lhs.shape[0]
    n = t // CH
    r = 4 if n % 4 == 0 else 1
    ns = n // r

    def body(lp_ref, ls_ref, gp_ref, gs_ref, l1_ref, r1_ref, l2_ref, r2_ref, pre_ref, suf_ref, sp_ref, ss_ref):
        @pl.when(pl.program_id(0) == 0)
        def _():
            sp_ref[...] = jnp.zeros_like(sp_ref)
            ss_ref[...] = jnp.zeros_like(ss_ref)

        def kv(l_ref, r_ref, scale, rows):
            lv = l_ref[rows, :] * scale
            rv = r_ref[rows, :]
            return jnp.concatenate([_dot_tn(lv[:, h * HD:(h + 1) * HD], rv[:, h * HD:(h + 1) * HD])
                                    for h in range(HEADS)], axis=1)

        for j in range(r):
            pre_ref[j] = sp_ref[...]
            sp_ref[...] = sp_ref[...] * gp_ref[...] + kv(l1_ref, r1_ref, lp_ref[...], slice(j * CH, (j + 1) * CH))
        for j in reversed(range(r)):
            suf_ref[j] = ss_ref[...]
            ss_ref[...] = ss_ref[...] * gs_ref[...] + kv(l2_ref, r2_ref, ls_ref[...], slice(j * CH, (j + 1) * CH))

    w = HEADS * HD
    return pl.pallas_call(
        body, grid=(ns,),
        in_specs=[_full((CH, w)), _full((CH, w)), _full((1, w)), _full((1, w)),
                  pl.BlockSpec((r * CH, w), lambda s: (s, lhs_col)), pl.BlockSpec((r * CH, w), lambda s: (s, rhs_col)),
                  pl.BlockSpec((r * CH, w), lambda s: (ns - 1 - s, lhs_col)),
                  pl.BlockSpec((r * CH, w), lambda s: (ns - 1 - s, rhs_col))],
        out_specs=[pl.BlockSpec((r, HD, w), lambda s: (s, 0, 0)), pl.BlockSpec((r, HD, w), lambda s: (ns - 1 - s, 0, 0))],
        out_shape=[S((n, HD, w), F32)] * 2, name=name,
        scratch_shapes=[pltpu.VMEM((HD, w), F32), pltpu.VMEM((HD, w), F32)],
        compiler_params=_cp("arbitrary"))(lp, ls, gp, gs, lhs, rhs, lhs, rhs)


def _ret_out(proj, sf, sb, tb, name, riders=()):
    t = proj.shape[0]
    r = 4 if (t // CH) % 4 == 0 else 1
    tm = r * CH
    w = HEADS * HD

    def body(rq_ref, rk_ref, v_ref, g_ref, sf_ref, sb_ref, dm_ref, xif_ref, xib_ref, a_ref, y_ref):
        for c in range(r):
            rows = slice(c * CH, (c + 1) * CH)
            for h in range(HEADS):
                cols = slice(h * HD, (h + 1) * HD)
                q = rq_ref[rows, cols]
                p = _dot_nt(q, rk_ref[rows, cols]) * dm_ref[h]
                a = (_dot(p, v_ref[rows, cols]) + _dot(q * xif_ref[:, cols], sf_ref[c, :, cols])
                     + _dot(q * xib_ref[:, cols], sb_ref[c, :, cols]))
                a_ref[rows, cols] = a
                o, _ = _standardize(a)
                gv = g_ref[rows, cols]
                y_ref[rows, cols] = (o * (gv * _sigmoid(gv))).astype(y_ref.dtype)

    (a, y), rid = _pcall(
        body, [proj, proj, proj, proj, sf, sb, tb["dm"], tb["xif"], tb["xib"]], riders, grid=(t // tm,),
        in_specs=[pl.BlockSpec((tm, w), lambda i: (i, Q_COL)), pl.BlockSpec((tm, w), lambda i: (i, K_COL)),
                  pl.BlockSpec((tm, w), lambda i: (i, V_COL)), pl.BlockSpec((tm, w), lambda i: (i, GATE_COL)),
                  pl.BlockSpec((r, HD, w), lambda i: (i, 0, 0)), pl.BlockSpec((r, HD, w), lambda i: (i, 0, 0)),
                  _full((HEADS, CH, CH)), _full((CH, w)), _full((CH, w))],
        out_specs=[pl.BlockSpec((tm, w), lambda i: (i, 0))] * 2,
        out_shape=[S((t, w), F32), S((t, w), ACT_DTYPE)], name=name, sem=("parallel",))
    return a, y, rid


def _ret_bwd_pre(dy, a, proj, name):
    t = dy.shape[0]
    tm = _row_tile(t, 512)
    w = HEADS * HD

    def body(dy_ref, a_ref, g_ref, da_ref, dg_ref):
        for h in range(HEADS):
            cols = slice(h * HD, (h + 1) * HD)
            o, r = _standardize(a_ref[:, cols])
            gv = g_ref[:, cols]
            s = _sigmoid(gv)
            dyv = dy_ref[:, cols]
            dg_ref[:, cols] = (dyv * o * (s * (1.0 + gv * (1.0 - s)))).astype(dg_ref.dtype)
            da_ref[:, cols] = _standardize_bwd(dyv * (gv * s), o, r).astype(da_ref.dtype)

    return pl.pallas_call(
        body, grid=(t // tm,),
        in_specs=[pl.BlockSpec((tm, w), lambda i: (i, 0)), pl.BlockSpec((tm, w), lambda i: (i, 0)),
                  pl.BlockSpec((tm, w), lambda i: (i, GATE_COL))],
        out_specs=[pl.BlockSpec((tm, w), lambda i: (i, 0))] * 2,
        out_shape=[S((t, w), ACT_DTYPE)] * 2, name=name, compiler_params=_cp("parallel"))(dy, a, proj)


def _ret_bwd_main(proj, da, sf, sb, gf, gb, tb, name, riders=()):
    t = proj.shape[0]
    r = 4 if (t // CH) % 4 == 0 else 1
    tm = r * CH
    w = HEADS * HD
    scale = HD ** -0.5

    def body(rq_ref, rk_ref, v_ref, da_ref, sf_ref, sb_ref, gf_ref, gb_ref, dm_ref, dmt_ref,
             xif_ref, xib_ref, zf_ref, zb_ref, c_ref, s_ref, o_ref):
        for c in range(r):
            rows = slice(c * CH, (c + 1) * CH)
            cos2, sin2 = c_ref[rows, :], s_ref[rows, :]
            for h in range(HEADS):
                cols = slice(h * HD, (h + 1) * HD)
                q, k, v, dav = rq_ref[rows, cols], rk_ref[rows, cols], v_ref[rows, cols], da_ref[rows, cols]
                qm, km, vm, dam = _mx(q), _mx(k), _mx(v), _mx(dav)
                dm, dmt = dm_ref[h], dmt_ref[h]
                pt = _dot_nt(km, qm) * dmt
                dp = _dot_nt(dam, vm) * dm
                dpt = _dot_nt(vm, dam) * dmt
                sfh, sbh, gfh, gbh = sf_ref[c, :, cols], sb_ref[c, :, cols], gf_ref[c, :, cols], gb_ref[c, :, cols]
                zf, zb = zf_ref[:, cols], zb_ref[:, cols]
                dv = _dot(pt, dam) + zf * _dot(km, gfh) + zb * _dot(km, gbh)
                drq = _dot(dp, km) + xif_ref[:, cols] * _dot_nt(dam, sfh) + xib_ref[:, cols] * _dot_nt(dam, sbh)
                drk = _dot(dpt, qm) + _dot_nt(zf * v, gfh) + _dot_nt(zb * v, gbh)
                o_ref[rows, h * HD:(h + 1) * HD] = _rot_bwd(drq, cos2, sin2).astype(o_ref.dtype)
                o_ref[rows, w + h * HD:w + (h + 1) * HD] = (_rot_bwd(drk, cos2, sin2) * scale).astype(o_ref.dtype)
                o_ref[rows, 2 * w + h * HD:2 * w + (h + 1) * HD] = dv.astype(o_ref.dtype)

    st = pl.BlockSpec((r, HD, w), lambda i: (i, 0, 0))
    (out,), rid = _pcall(
        body, [proj, proj, proj, da, sf, sb, gf, gb, tb["dm"], tb["dmt"], tb["xif"], tb["xib"], tb["zf"], tb["zb"],
               tb["cos2"], tb["sin2"]], riders, grid=(t // tm,),
        in_specs=[pl.BlockSpec((tm, w), lambda i: (i, Q_COL)), pl.BlockSpec((tm, w), lambda i: (i, K_COL)),
                  pl.BlockSpec((tm, w), lambda i: (i, V_COL)), pl.BlockSpec((tm, w), lambda i: (i, 0)),
                  st, st, st, st, _full((HEADS, CH, CH)), _full((HEADS, CH, CH)),
                  _full((CH, w)), _full((CH, w)), _full((CH, w)), _full((CH, w)),
                  pl.BlockSpec((tm, HD), lambda i: (i, 0)), pl.BlockSpec((tm, HD), lambda i: (i, 0))],
        out_specs=[pl.BlockSpec((tm, 3 * w), lambda i: (i, 0))],
        out_shape=[S((t, 3 * w), ACT_DTYPE)], name=name, sem=("parallel",))
    return out, rid


CONV_TM = 256
CONV_SUB = 64
A_COL = (2 * GM_W + 4 * RET_W) // CV_W
G_COL = A_COL + 1


def _halo_specs(t, tm, col):
    nb16 = t // HALO
    per = tm // HALO
    return [pl.BlockSpec((tm, CV_W), lambda i: (i, col)),
            pl.BlockSpec((HALO, CV_W), lambda i: (jnp.maximum(i * per - 1, 0), col)),
            pl.BlockSpec((HALO, CV_W), lambda i: (jnp.minimum((i + 1) * per, nb16 - 1), col))]


def _fill_padded(dst_ref, prev, main, nxt, tm, i, nb):
    dst_ref[0:HALO, :] = jnp.where(i > 0, prev, 0.0)
    dst_ref[HALO:HALO + tm, :] = main
    dst_ref[HALO + tm:2 * HALO + tm, :] = jnp.where(i < nb - 1, nxt, 0.0)


SUBLANES = 8


def _fill_shifted(sh_ref, src_ref, tm):
    n = tm + 2 * HALO - SUBLANES
    for b in range(SUBLANES):
        sh_ref[b, 0:n, :] = src_ref[pl.ds(b, n), :]


def _tap(sh_ref, off, rows):
    return sh_ref[off % SUBLANES, pl.ds(off - off % SUBLANES, rows), :]


def _conv_fwd(proj, cw, cb, ln_g, ln_b, name, riders=()):
    t = proj.shape[0]
    tm = _row_tile(t, CONV_TM)
    nb = t // tm

    def body(a_ref, ap_ref, an_ref, g_ref, gp_ref, gn_ref, w_ref, b_ref, lg_ref, lb_ref, y_ref, hc_ref,
             hp_ref, sh_ref):
        i = pl.program_id(0)
        _fill_padded(hp_ref, ap_ref[...] * _sigmoid(gp_ref[...]), a_ref[...] * _sigmoid(g_ref[...]),
                     an_ref[...] * _sigmoid(gn_ref[...]), tm, i, nb)
        _fill_shifted(sh_ref, hp_ref, tm)
        for sb in range(tm // CONV_SUB):
            acc = jnp.zeros((CONV_SUB, CV_W), F32) + b_ref[...]
            for k in range(KCONV):
                acc = acc + w_ref[k:k + 1, :] * _tap(sh_ref, sb * CONV_SUB + k + 1, CONV_SUB)
            rows = slice(sb * CONV_SUB, (sb + 1) * CONV_SUB)
            hc_ref[rows, :] = acc
            o, _ = _standardize(acc)
            z = o * lg_ref[...] + lb_ref[...]
            y_ref[rows, :] = (z * _sigmoid(z)).astype(y_ref.dtype)

    (y, hc), rid = _pcall(
        body, [proj, proj, proj, proj, proj, proj, cw, cb, ln_g, ln_b], riders, grid=(nb,),
        in_specs=_halo_specs(t, tm, A_COL) + _halo_specs(t, tm, G_COL)
        + [_full((32, CV_W)), _full((1, CV_W)), _full((1, CV_W)), _full((1, CV_W))],
        out_specs=[pl.BlockSpec((tm, CV_W), lambda i: (i, 0))] * 2,
        out_shape=[S((t, CV_W), ACT_DTYPE), S((t, CV_W), F32)], name=name, sem=("parallel",),
        scratch_shapes=[pltpu.VMEM((tm + 2 * HALO, CV_W), F32), pltpu.VMEM((SUBLANES, tm + 2 * HALO, CV_W), F32)])
    return y, hc, rid


def _conv_bwd(proj, dy, hc, cw, ln_g, ln_b, name, riders=()):
    t = proj.shape[0]
    tm = _row_tile(t, CONV_TM)
    nb = t // tm

    def body(a_ref, ap_ref, an_ref, g_ref, gp_ref, gn_ref, dy_ref, dyp_ref, dyn_ref, hc_ref, hcp_ref, hcn_ref,
             w_ref, lg_ref, lb_ref, d_ref, dw_ref, dcb_ref, dlg_ref, dlb_ref, hp_ref, dhp_ref, dwacc_ref,
             sh_ref, dsh_ref):
        i = pl.program_id(0)
        first = i == 0

        def dhc_of(dyv, hcv):
            o, r = _standardize(hcv)
            z = o * lg_ref[...] + lb_ref[...]
            s = _sigmoid(z)
            dz = dyv * (s * (1.0 + z * (1.0 - s)))
            return _standardize_bwd(dz * lg_ref[...], o, r), dz, o

        dhc, dz, o = dhc_of(dy_ref[...], hc_ref[...])
        _acc_out(dlg_ref, jnp.sum(dz * o, axis=0, keepdims=True), first)
        _acc_out(dlb_ref, jnp.sum(dz, axis=0, keepdims=True), first)
        _acc_out(dcb_ref, jnp.sum(dhc, axis=0, keepdims=True), first)
        _fill_padded(dhp_ref, dhc_of(dyp_ref[...], hcp_ref[...])[0], dhc, dhc_of(dyn_ref[...], hcn_ref[...])[0],
                     tm, i, nb)
        _fill_padded(hp_ref, ap_ref[...] * _sigmoid(gp_ref[...]), a_ref[...] * _sigmoid(g_ref[...]),
                     an_ref[...] * _sigmoid(gn_ref[...]), tm, i, nb)

        _fill_shifted(sh_ref, hp_ref, tm)
        _fill_shifted(dsh_ref, dhp_ref, tm)

        @pl.when(first)
        def _():
            dwacc_ref[...] = jnp.zeros_like(dwacc_ref)

        for sb in range(tm // CONV_SUB):
            base = sb * CONV_SUB
            dmain = dhp_ref[pl.ds(HALO + base, CONV_SUB), :]
            dh = jnp.zeros((CONV_SUB, CV_W), F32)
            for k in range(KCONV):
                dh = dh + w_ref[k:k + 1, :] * _tap(dsh_ref, base + 2 * HALO - 1 - k, CONV_SUB)
                prod = dmain * _tap(sh_ref, base + k + 1, CONV_SUB)
                dwacc_ref[k * 8:(k + 1) * 8, :] += jnp.sum(prod.reshape(CONV_SUB // 8, 8, CV_W), axis=0)
            rows = slice(base, base + CONV_SUB)
            s = _sigmoid(g_ref[rows, :])
            d_ref[rows, :CV_W] = (dh * s).astype(d_ref.dtype)
            d_ref[rows, CV_W:] = (dh * a_ref[rows, :] * (s * (1.0 - s))).astype(d_ref.dtype)

        @pl.when(i == nb - 1)
        def _():
            for k in range(KCONV):
                dw_ref[k:k + 1, :] = jnp.sum(dwacc_ref[k * 8:(k + 1) * 8, :], axis=0, keepdims=True)
            dw_ref[KCONV:32, :] = jnp.zeros((32 - KCONV, CV_W), F32)

    hs = [pl.BlockSpec((tm, CV_W), lambda i: (i, 0)),
          pl.BlockSpec((HALO, CV_W), lambda i: (jnp.maximum(i * (tm // HALO) - 1, 0), 0)),
          pl.BlockSpec((HALO, CV_W), lambda i: (jnp.minimum((i + 1) * (tm // HALO), t // HALO - 1), 0))]
    outs, rid = _pcall(
        body, [proj, proj, proj, proj, proj, proj, dy, dy, dy, hc, hc, hc, cw, ln_g, ln_b], riders, grid=(nb,),
        in_specs=_halo_specs(t, tm, A_COL) + _halo_specs(t, tm, G_COL) + hs + hs
        + [_full((32, CV_W)), _full((1, CV_W)), _full((1, CV_W))],
        out_specs=[pl.BlockSpec((tm, 2 * CV_W), lambda i: (i, 0)), _full((32, CV_W)), _full((1, CV_W)),
                   _full((1, CV_W)), _full((1, CV_W))],
        out_shape=[S((t, 2 * CV_W), ACT_DTYPE), S((32, CV_W), F32), S((1, CV_W), F32), S((1, CV_W), F32),
                   S((1, CV_W), F32)],
        name=name, sem=("arbitrary",),
        scratch_shapes=[pltpu.VMEM((tm + 2 * HALO, CV_W), F32), pltpu.VMEM((tm + 2 * HALO, CV_W), F32),
                        pltpu.VMEM((32 * 8, CV_W), F32), pltpu.VMEM((SUBLANES, tm + 2 * HALO, CV_W), F32),
                        pltpu.VMEM((SUBLANES, tm + 2 * HALO, CV_W), F32)])
    return (*outs, rid)


def _loss_head(x, g, target, name):
    t = x.shape[0]
    tm = _row_tile(t, 512)

    def body(x_ref, g_ref, t_ref, dx_ref, dg_ref, l_ref):
        first = pl.program_id(0) == 0
        xv = x_ref[...]
        r = _rms_r(xv)
        e = xv * r * g_ref[...] - t_ref[...]
        dx, dgrow = _rms_bwd(e * (1.0 / D), xv, r, g_ref[...])
        dx_ref[...] = dx
        _acc_out(dg_ref, jnp.sum(dgrow, axis=0, keepdims=True), first)
        part = 0.5 * jnp.sum(jnp.mean(e * e, axis=-1, keepdims=True), axis=0, keepdims=True)
        _acc_out(l_ref, jnp.broadcast_to(part, (8, 128)), first)

    return pl.pallas_call(
        body, grid=(t // tm,),
        in_specs=[pl.BlockSpec((tm, D), lambda i: (i, 0)), _full((1, D)), pl.BlockSpec((tm, D), lambda i: (i, 0))],
        out_specs=[pl.BlockSpec((tm, D), lambda i: (i, 0)), _full((1, D)), _full((8, 128))],
        out_shape=[S((t, D), F32), S((1, D), F32), S((8, 128), F32)], name=name,
        compiler_params=_cp("arbitrary"))(x, g, target)


def _as2d(a):
    return a.reshape(-1, a.shape[-1])


def _ew_tile(rows, cols, n_arrays):
    budget = VMEM_LIMIT // 2
    tr = rows
    while tr * cols * 4 * n_arrays * 2 > budget and tr % 16 == 0:
        tr //= 2
    assert rows % tr == 0
    return tr


def _adamw(w, g, m, v, name):
    shape = w.shape
    w2, g2, m2, v2 = _as2d(w), _as2d(g), _as2d(m), _as2d(v)
    rows, cols = w2.shape
    tr = _ew_tile(rows, cols, 7)

    def body(w_ref, g_ref, m_ref, v_ref, d_ref, nm_ref, nv_ref):
        gv = g_ref[...]
        nm = ADAM_B1 * m_ref[...] + (1.0 - ADAM_B1) * gv
        nv = ADAM_B2 * v_ref[...] + (1.0 - ADAM_B2) * (gv * gv)
        m_hat = nm / (1.0 - ADAM_B1 ** ADAM_STEP)
        v_hat = nv / (1.0 - ADAM_B2 ** ADAM_STEP)
        d_ref[...] = -ADAM_LR * (m_hat / (jnp.sqrt(v_hat) + ADAM_EPS) + ADAM_WD * w_ref[...])
        nm_ref[...] = nm
        nv_ref[...] = nv

    spec = pl.BlockSpec((tr, cols), lambda i: (i, 0))
    outs = pl.pallas_call(body, grid=(rows // tr,), in_specs=[spec] * 4, out_specs=[spec] * 3,
                          out_shape=[S((rows, cols), F32)] * 3, name=name,
                          compiler_params=_cp("parallel"))(w2, g2, m2, v2)
    return tuple(o.reshape(shape) for o in outs)


BIG = (("w_in", "col"), ("w_out", "row"), ("w_ffn_in", "col"), ("w_ffn_out", "row"))
NBIG = len(BIG)


def _cast_to_gathered(w, l, me, name):
    _, r_, c_ = w.shape
    tr = _ew_tile(r_, c_, 2)

    def body(me_ref, w_ref, o_ref):
        o_ref[...] = w_ref[...].astype(o_ref.dtype)

    gs = pltpu.PrefetchScalarGridSpec(
        num_scalar_prefetch=1, grid=(r_ // tr,),
        in_specs=[pl.BlockSpec((None, tr, c_), lambda i, s: (l, i, 0))],
        out_specs=pl.BlockSpec((None, tr, c_), lambda i, s: (s[0], i, 0)))
    out = pl.pallas_call(body, grid_spec=gs, out_shape=S((N_CHIPS, r_, c_), MXU_DTYPE), name=name,
                         compiler_params=_cp("parallel"))(me.reshape(1), w)
    return out.reshape(N_CHIPS, 2, r_ // 2, c_)


def _all_gather(bufs, name, per_core=False):
    n = len(bufs)

    def body(*refs):
        i_refs, o_refs = refs[:n], refs[n:2 * n]
        isend, irecv, dsend, drecv, osend, orecv = refs[2 * n:]
        pos = _mesh_pos()
        x, y, c, me, _, _ = pos
        ici = _rider_copies("ici", i_refs, o_refs, isend, irecv, pos)
        d2d = _rider_copies("d2d", o_refs, o_refs, dsend, drecv, pos)
        own = []
        if per_core:
            for b in range(n):
                own.append(tuple(pltpu.make_async_remote_copy(
                    src_ref=s_, dst_ref=d_, send_sem=osend.at[b], recv_sem=orecv.at[b],
                    device_id=(x, y, 1 - c), device_id_type=MESH)
                    for s_, d_ in ((i_refs[b].at[me, c], o_refs[b].at[me, c]),
                                   (o_refs[b].at[me, 1 - c], o_refs[b].at[me, 1 - c]))))
        for cp, _ in ici + own:
            cp.start()
        for (_, land), (fwd, _) in zip(ici, d2d):
            land.wait_recv()
            fwd.start()
        for _, land in d2d + own:
            land.wait_recv()
        for cp, _ in ici + d2d + own:
            cp.wait_send()

    return pl.pallas_call(
        body, in_specs=[ANY] * n, out_specs=[ANY] * n, out_shape=[S(a.shape, a.dtype) for a in bufs],
        input_output_aliases={w: w for w in range(n)}, name=name,
        scratch_shapes=[pltpu.SemaphoreType.DMA((n, 3))] * 4 + [pltpu.SemaphoreType.DMA((n,))] * 2)(*bufs)


def _pair_exchange(grads, name):
    n = len(grads)

    def body(*refs):
        g_refs, theirs = refs[:n], refs[n:2 * n]
        send, recv = refs[2 * n:]
        x, y, c, *_ = _mesh_pos()
        cps = []
        for w in range(n):
            cp = pltpu.make_async_remote_copy(
                src_ref=g_refs[w].at[:, 1 - c], dst_ref=theirs[w], send_sem=send.at[w], recv_sem=recv.at[w],
                device_id=(x, y, 1 - c), device_id_type=MESH)
            cp.start()
            cps.append(cp)
        for cp in cps:
            cp.wait()

    return pl.pallas_call(
        body, in_specs=[ANY] * n, out_specs=[ANY] * n,
        out_shape=[S(a.shape[:1] + a.shape[2:], a.dtype) for a in grads], name=name,
        scratch_shapes=[pltpu.SemaphoreType.DMA((n,))] * 2)(*grads)


def _pair_sum(g, theirs, core, name):
    _, _, rh, c_ = g.shape
    tr = _ew_tile(rh, c_, 2)

    def body(s_ref, g_ref, t_ref, o_ref):
        o_ref[...] = (g_ref[...].astype(F32) + t_ref[...].astype(F32)).astype(o_ref.dtype)

    blk = pl.BlockSpec((None, tr, c_), lambda j, i, s: (j, i, 0))
    gs = pltpu.PrefetchScalarGridSpec(
        num_scalar_prefetch=1, grid=(N_CHIPS, rh // tr),
        in_specs=[pl.BlockSpec((None, None, tr, c_), lambda j, i, s: (j, s[0], i, 0)), blk], out_specs=blk)
    return pl.pallas_call(body, grid_spec=gs, out_shape=S(theirs.shape, theirs.dtype), name=name,
                          compiler_params=_cp("parallel", "parallel"))(core.reshape(1), g, theirs)


def _chip_sum(q, got, l, me, core, into, name):
    _, rh, c_ = got.shape
    tr = _ew_tile(rh, c_, 4)

    def body(s_ref, q_ref, g0_ref, g1_ref, g2_ref, o_ref):
        acc = q_ref[...].astype(F32)
        for r in (g0_ref, g1_ref, g2_ref):
            acc = acc + r[...].astype(F32)
        o_ref[...] = acc

    in_specs = [pl.BlockSpec((None, tr, c_), lambda i, s: (s[0], i, 0))] + [
        pl.BlockSpec((None, tr, c_), functools.partial(lambda k, i, s: (k, i, 0), k)) for k in range(3)]
    return _call_into(
        body, into, in_specs, [jnp.stack([me, core]), q, got, got, got], n_prefetch=1, grid=(rh // tr,),
        out_specs=pl.BlockSpec((None, None, tr, c_), lambda i, s: (l, s[1], i, 0)),
        out_shape=S((DEPTH, 2, rh, c_), F32), name=name, compiler_params=_cp("parallel"))


def _pair_gather(gs4):
    def body(*refs):
        i_refs, o_refs = refs[:NBIG], refs[NBIG:2 * NBIG]
        send, recv = refs[2 * NBIG:]
        x, y, c, *_ = _mesh_pos()
        cps = []
        for w in range(NBIG):
            cp = pltpu.make_async_remote_copy(
                src_ref=i_refs[w].at[:, c], dst_ref=o_refs[w].at[:, c], send_sem=send.at[w], recv_sem=recv.at[w],
                device_id=(x, y, 1 - c), device_id_type=MESH)
            cp.start()
            cps.append(cp)
        for cp in cps:
            cp.wait()

    outs = pl.pallas_call(
        body, in_specs=[ANY] * NBIG, out_specs=[ANY] * NBIG, out_shape=[S(a.shape, a.dtype) for a in gs4],
        input_output_aliases={w: w for w in range(NBIG)}, name="grad_pair_gather",
        scratch_shapes=[pltpu.SemaphoreType.DMA((NBIG,))] * 2)(*gs4)
    return [o.reshape(o.shape[0], 2 * o.shape[2], o.shape[3]) for o in outs]


def _all_reduce_small(p, me, core, name):
    rows = p.shape[0]

    def place(s_ref, p_ref, o_ref):
        o_ref[...] = p_ref[...]

    gs = pltpu.PrefetchScalarGridSpec(
        num_scalar_prefetch=1, grid=(1,), in_specs=[pl.BlockSpec((rows, 128), lambda i, s: (0, 0))],
        out_specs=pl.BlockSpec((None, None, rows, 128), lambda i, s: (s[0], s[1], 0, 0)))
    mine = pl.pallas_call(place, grid_spec=gs, out_shape=S((N_CHIPS, 2, rows, 128), F32), name=name + "_place",
                          compiler_params=_cp("arbitrary"))(jnp.stack([me, core]), p)
    parts = _all_gather([mine], name + "_gather", per_core=True)[0]

    def total(g_ref, o_ref):
        acc = g_ref[0, 0]
        for j in range(N_CHIPS):
            for c in range(2):
                if (j, c) != (0, 0):
                    acc = acc + g_ref[j, c]
        o_ref[...] = acc

    vm = pl.BlockSpec(memory_space=pltpu.VMEM)
    return pl.pallas_call(total, in_specs=[vm], out_specs=vm, out_shape=S((rows, 128), F32), name=name + "_sum",
                          compiler_params=pltpu.CompilerParams(vmem_limit_bytes=VMEM_LIMIT))(parts)


PACK_UNIT = 8 * 128


def _pack(arrs):
    parts = []
    for a in arrs:
        flat = a.reshape(-1)
        pad = (-flat.shape[0]) % PACK_UNIT
        parts.append(jnp.pad(flat, (0, pad)).reshape(-1, 128))
    return jnp.concatenate(parts, axis=0)


def _unpack(buf, shapes):
    outs, row = [], 0
    for shp in shapes:
        n = int(np.prod(shp))
        rows = -(-n // PACK_UNIT) * 8
        outs.append(buf[row:row + rows].reshape(-1)[:n].reshape(shp))
        row += rows
    return outs


SMALL = ("norm1_g", "gm_ln_g", "gm_ln_b", "gm_ws", "gm_bs", "conv_w", "conv_b", "conv_ln_g", "conv_ln_b",
         "norm2_g", "final_g")
WEIGHTS = ("norm1_g", "w_in", "gm_ln_g", "gm_ln_b", "gm_ws", "gm_bs", "conv_w", "conv_b", "conv_ln_g",
           "conv_ln_b", "w_out", "norm2_g", "w_ffn_in", "w_ffn_out", "final_g")


def kernel(x, norm1_g, w_in, gm_ln_g, gm_ln_b, gm_ws, gm_bs, conv_w, conv_b, conv_ln_g, conv_ln_b, w_out, norm2_g, w_ffn_in, w_ffn_out, final_g, loss_target, m_norm1_g, m_w_in, m_gm_ln_g, m_gm_ln_b, m_gm_ws, m_gm_bs, m_conv_w, m_conv_b, m_conv_ln_g, m_conv_ln_b, m_w_out, m_norm2_g, m_w_ffn_in, m_w_ffn_out, m_final_g, v_norm1_g, v_w_in, v_gm_ln_g, v_gm_ln_b, v_gm_ws, v_gm_bs, v_conv_w, v_conv_b, v_conv_ln_g, v_conv_ln_b, v_w_out, v_norm2_g, v_w_ffn_in, v_w_ffn_out, v_final_g):
    given = dict(locals())
    t = x.shape[1]
    xc = x.reshape(t, D)
    target = loss_target.reshape(t, D)
    me = 2 * lax.axis_index("x") + lax.axis_index("y")
    core = lax.axis_index("c")
    tb = _tables(t)

    me = me.astype(jnp.int32)
    core = core.astype(jnp.int32)
    names = [n for n, _ in BIG]
    kinds = dict(BIG)
    gathered = [{n: _cast_to_gathered(given[n], l, me, f"cast_{n}{l}") for n in names} for l in range(DEPTH)]
    gathered[0]["w_in"] = _all_gather([gathered[0]["w_in"]], "all_gather_w_in0")[0]

    def weight(l, n):
        b = gathered[l][n]
        r_, c_ = 2 * b.shape[2], b.shape[3]
        return b.reshape(N_CHIPS, r_, c_) if kinds[n] == "col" else b.reshape(N_CHIPS * r_, c_)

    cshard = CV_W // N_CHIPS
    placed = lax.dynamic_update_slice(jnp.zeros((DEPTH, KCONV, CV_W), F32),
                                      conv_w * (core == 0).astype(F32), (0, 0, me * cshard))
    conv_w_full = _unpack(_all_reduce_small(_pack([placed]), me, core, "gather_conv_w"), [(DEPTH, KCONV, CV_W)])[0]
    cw32 = jnp.pad(conv_w_full, ((0, 0), (0, 32 - KCONV), (0, 0)))

    def row(a, l):
        return a[l].reshape(1, -1)

    saved = []
    early = ["w_in", "w_out", "w_ffn_in"]
    for l in range(DEPTH):
        cur = gathered[l]
        nxt = gathered[l + 1] if l + 1 < DEPTH else None
        sv = {"x": xc}
        bias = jnp.repeat(gm_bs[l].T, GM_W // GM_HEADS, axis=1)
        first = ["w_ffn_in"] if l == 0 else ["w_ffn_out"]
        late = ["w_out", "w_ffn_out"]
        proj, h1t, rid = _norm_mm(xc, row(norm1_g, l), weight(l, "w_in"), F32, f"in_proj{l}", 512,
                             [("ici" if l == 0 else "d2d", [cur[n] for n in first])], (tb["cos2"], tb["sin2"]))
        cur.update(zip(first, rid))
        y_gm, rid = _gm_fwd(proj, row(gm_ln_g, l), row(gm_ln_b, l), gm_ws[l], bias, f"gm_fwd{l}",
                            [("d2d", [cur[n] for n in first])] if l == 0 else ())
        cur.update(zip(first, rid))
        sf, sb = _ret_scan(proj, K_COL, proj, V_COL, tb["zf"], tb["zb"], tb["gcf"], tb["gcb"], f"ret_state{l}")
        a, y_ret, rid = _ret_out(proj, sf, sb, tb, f"ret_out{l}",
                                 [("ici", [cur[n] for n in late])] if l == 0 else ())
        cur.update(zip(late, rid))
        y_cv, hc, rid = _conv_fwd(proj, cw32[l], row(conv_b, l), row(conv_ln_g, l), row(conv_ln_b, l),
                                  f"conv_fwd{l}", [("d2d", [cur[n] for n in late])] if l == 0 else ())
        cur.update(zip(late, rid))
        x_mid = _parts_mm_res([y_gm, y_ret, y_cv], weight(l, "w_out"), xc, f"out_proj{l}")
        ff, h2t, rid = _norm_mm(x_mid, row(norm2_g, l), weight(l, "w_ffn_in"), ACT_DTYPE, f"ffn_in{l}", 512,
                           [("ici", [nxt[n] for n in early])] if nxt else ())
        if nxt:
            nxt.update(zip(early, rid))
        xc, act_t, rid = _swiglu_mm_res(ff, weight(l, "w_ffn_out"), x_mid, f"ffn_out{l}",
                                 [("d2d", [nxt[n] for n in early]), ("ici", [nxt["w_ffn_out"]])] if nxt else ())
        if nxt:
            nxt.update(zip(early + ["w_ffn_out"], rid))
        sv.update(bias=bias, proj=proj, h1t=h1t, h2t=h2t, y_gm=y_gm, sf=sf, sb=sb, a=a, y_ret=y_ret, y_cv=y_cv,
                  hc=hc, x_mid=x_mid,
                  ff=ff, act_t=act_t)
        saved.append(sv)

    dx, d_final_g, lpart = _loss_head(xc, final_g.reshape(1, D), target, "loss_head")

    small_g = {n: [None] * DEPTH for n in SMALL}
    qs = [{} for _ in range(DEPTH)]
    got = [{} for _ in range(DEPTH)]
    ffn_w, mix_w = ["w_ffn_out", "w_ffn_in"], ["w_out", "w_in"]

    def halves(big_g, group):
        return [big_g[n].reshape(N_CHIPS, 2, given[n].shape[1] // 2, given[n].shape[2]) for n in group]

    def pair_sums(l, group, g4, theirs):
        qs[l].update({n: _pair_sum(g, th, core, f"pair_sum_{n}{l}") for n, g, th in zip(group, g4, theirs)})
        return [qs[l][n] for n in group]

    for l in reversed(range(DEPTH)):
        sv = saved[l]
        proj = sv["proj"]
        big_g = {}
        dff = _dx_swiglu(dx, weight(l, "w_ffn_out"), sv["ff"], f"ffn_out_dx{l}")
        big_g["w_ffn_out"] = _dw_swiglu(sv["act_t"], dx, f"ffn_out_dw{l}")
        dx_mid, dg2, _ = _dx_norm([dff], weight(l, "w_ffn_in"), sv["x_mid"], row(norm2_g, l), dx,
                                  f"ffn_in_dx{l}", 512)
        big_g["w_ffn_in"] = _dw_norm_cols(sv["h2t"], dff, w_ffn_in.shape[2], f"ffn_in_dw{l}")
        g4 = halves(big_g, ffn_w)
        (dy_gm, dy_ret, dy_cv), theirs = _dx_parts(dx_mid, weight(l, "w_out"), [GM_W, RET_W, CV_W],
                                                   f"out_proj_dx{l}", [("pairx", g4)])
        q_ffn = pair_sums(l, ffn_w, g4, theirs)
        big_g["w_out"] = _dw_parts([sv["y_gm"], sv["y_ret"], sv["y_cv"]], dx_mid, f"out_proj_dw{l}")
        d_cv, dcw, dcb, dclg, dclb, rid = _conv_bwd(proj, dy_cv, sv["hc"], cw32[l], row(conv_ln_g, l),
                                                    row(conv_ln_b, l), f"conv_bwd{l}", [("scatter", q_ffn[:1])])
        got[l].update(zip(ffn_w[:1], rid))
        da, d_g = _ret_bwd_pre(dy_ret, sv["a"], proj, f"ret_bwd_pre{l}")
        gb_, gf_ = _ret_scan(proj, Q_COL, da, 0, tb["xib"], tb["xif"], tb["gcb"], tb["gcf"], f"ret_bwd_state{l}")
        d_qkv, rid = _ret_bwd_main(proj, da, sv["sf"], sv["sb"], gf_, gb_, tb, f"ret_bwd_main{l}",
                                   [("scatter", q_ffn[1:])])
        got[l].update(zip(ffn_w[1:], rid))
        d_gm, dws, dbs, dglg, dglb = _gm_bwd(proj, dy_gm, row(gm_ln_g, l), row(gm_ln_b, l), gm_ws[l],
                                             jnp.swapaxes(gm_ws[l], 1, 2), sv["bias"], f"gm_bwd{l}")
        dparts = [d_gm, d_qkv, d_g, d_cv]
        big_g["w_in"] = _dw_norm_parts(sv["h1t"], dparts, w_in.shape[2], f"in_proj_dw{l}")
        g4 = halves(big_g, mix_w)
        q_mix = pair_sums(l, mix_w, g4, _pair_exchange(g4, f"grad_pair_exchange_mix{l}"))
        dx, dg1, rid = _dx_norm(dparts, weight(l, "w_in"), sv["x"], row(norm1_g, l), dx_mid, f"in_proj_dx{l}", 512,
                                [("scatter", q_mix)])
        got[l].update(zip(mix_w, rid))
        for n, val in (("norm1_g", dg1[0]), ("gm_ln_g", dglg[0]), ("gm_ln_b", dglb[0]), ("gm_ws", dws),
                       ("gm_bs", dbs[:, :GM_HEADS].T), ("conv_w", dcw[:KCONV]), ("conv_b", dcb[0]),
                       ("conv_ln_g", dclg[0]), ("conv_ln_b", dclb[0]), ("norm2_g", dg2[0])):
            small_g[n][l] = val

    small_shapes = [given[n].shape if n != "conv_w" else (DEPTH, KCONV, CV_W) for n in SMALL]
    partials = [d_final_g[0] if n == "final_g" else jnp.stack(small_g[n]) for n in SMALL]
    summed = _unpack(_all_reduce_small(_pack(partials + [lpart]), me, core, "all_reduce_small_grads"),
                     small_shapes + [lpart.shape])
    loss = summed[-1][0, 0]
    reduced = dict(zip(SMALL, summed))
    reduced["conv_w"] = lax.dynamic_slice(reduced["conv_w"], (0, 0, me * cshard), (DEPTH, KCONV, cshard))

    halves = [None] * NBIG
    for l in reversed(range(DEPTH)):
        halves = [_chip_sum(qs[l][n], got[l][n], l, me, core, h, f"chip_sum_{n}{l}") for n, h in zip(names, halves)]
    grads = dict(zip(names, _pair_gather(halves)))
    grads.update(reduced)

    delta, new_m, new_v = {}, {}, {}
    for n, _ in BIG:
        delta[n], new_m[n], new_v[n] = _adamw(given[n], grads[n], given["m_" + n], given["v_" + n], f"adamw_{n}")
    shapes = [given[n].shape for n in SMALL]
    packed = [_pack([src[n] if src is grads else src[p + n] for n in SMALL])
              for src, p in ((given, ""), (grads, ""), (given, "m_"), (given, "v_"))]
    outs = _adamw(*packed, "adamw_small")
    for dst, buf in zip((delta, new_m, new_v), outs):
        dst.update(zip(SMALL, _unpack(buf, shapes)))

    return (loss, dx.reshape(1, t, D), *[grads[n] for n in WEIGHTS], *[delta[n] for n in WEIGHTS],
            *[new_m[n] for n in WEIGHTS], *[new_v[n] for n in WEIGHTS])
```

```python
import functools
import math

import numpy as np
import jax
import jax.numpy as jnp
from jax import lax
from jax.experimental import pallas as pl
from jax.experimental.pallas import tpu as pltpu

F32 = jnp.float32
BF16 = jnp.bfloat16
MXU_DTYPE = BF16
ACT_DTYPE = BF16
S = jax.ShapeDtypeStruct

D = 1024
DEPTH = 2
GM_W = 256
GM_HEADS = 4
RET_W = 512
HEADS = 4
HD = 128
CV_W = 256
KCONV = 31
IN_W = 2 * GM_W + 4 * RET_W + 2 * CV_W
FFN_H = 2816
CH = 128
ROPE_BASE = 10000.0
EPS = 1e-6
N_CHIPS = 4
N_DEV = 8
HALO = 16

ADAM_LR = 0.001
ADAM_B1 = 0.9
ADAM_B2 = 0.999
ADAM_EPS = 1e-08
ADAM_WD = 0.01
ADAM_STEP = 10

VMEM_LIMIT = 52 * 1024 * 1024
MESH = pl.DeviceIdType.MESH


def _cp(*sem, vmem=VMEM_LIMIT):
    return pltpu.CompilerParams(dimension_semantics=tuple(sem), vmem_limit_bytes=vmem)


def _mx(a):
    return a.astype(MXU_DTYPE)


def _dot(a, b):
    return jnp.dot(_mx(a), _mx(b), preferred_element_type=F32)


def _dot_nt(a, b):
    return lax.dot_general(_mx(a), _mx(b), (((1,), (1,)), ((), ())), preferred_element_type=F32)


def _dot_tn(a, b):
    return lax.dot_general(_mx(a), _mx(b), (((0,), (0,)), ((), ())), preferred_element_type=F32)


def _sigmoid(x):
    return 1.0 / (1.0 + jnp.exp(-x))


def _gelu(x):
    return 0.5 * x * (1.0 + lax.erf(x * (1.0 / math.sqrt(2.0))))


def _gelu_grad(x):
    return 0.5 * (1.0 + lax.erf(x * (1.0 / math.sqrt(2.0)))) + x * jnp.exp(-0.5 * x * x) * (1.0 / math.sqrt(2.0 * math.pi))


def _rms_r(x):
    return lax.rsqrt(jnp.mean(x * x, axis=-1, keepdims=True) + EPS)


def _rms_bwd(dh, x, r, g):
    u = dh * g
    dx = r * u - x * (r * r * r) * jnp.mean(u * x, axis=-1, keepdims=True)
    return dx, dh * x * r


def _standardize(a):
    mu = jnp.mean(a, axis=-1, keepdims=True)
    d = a - mu
    r = lax.rsqrt(jnp.mean(d * d, axis=-1, keepdims=True) + EPS)
    return d * r, r


def _standardize_bwd(do, o, r):
    return r * (do - jnp.mean(do, axis=-1, keepdims=True) - o * jnp.mean(do * o, axis=-1, keepdims=True))


def _acc_out(ref, val, first):
    @pl.when(first)
    def _():
        ref[...] = val

    @pl.when(jnp.logical_not(first))
    def _():
        ref[...] += val


def _row_tile(t, pref):
    tm = min(t, pref)
    assert t % tm == 0, (t, tm)
    return tm


def _segments(part_widths, shard_w):
    bounds = {0}
    off = 0
    for w in part_widths:
        off += w
        bounds.add(off)
    total = off
    for j in range(1, total // shard_w + 1):
        bounds.add(j * shard_w)
    bounds = sorted(bounds)
    starts = np.cumsum([0] + list(part_widths))
    segs = []
    for a, b in zip(bounds[:-1], bounds[1:]):
        p = int(np.searchsorted(starts, a, side="right") - 1)
        segs.append((p, a - int(starts[p]), a // shard_w, a % shard_w, b - a))
    return segs


ANY = pl.BlockSpec(memory_space=pl.ANY)


def _mesh_pos():
    x, y, c = lax.axis_index("x"), lax.axis_index("y"), lax.axis_index("c")
    chips = [(1 - x, y), (x, 1 - y), (1 - x, 1 - y)]
    return x, y, c, 2 * x + y, chips, [2 * cx + cy for cx, cy in chips]


def _rider_copies(kind, i_refs, o_refs, send, recv, pos):
    x, y, c, me, chips, cj = pos
    out = []
    for b, (i_ref, o_ref) in enumerate(zip(i_refs, o_refs)):
        for k in range(1 if kind == "pairx" else 3):
            if kind == "ici":
                src, dst, land, dev = i_ref.at[me, c], o_ref.at[me, c], o_ref.at[cj[k], c], (*chips[k], c)
            elif kind == "d2d":
                src, dst, land, dev = i_ref.at[cj[k], c], o_ref.at[cj[k], c], o_ref.at[cj[k], 1 - c], (x, y, 1 - c)
            elif kind == "pairx":
                src, dst, land, dev = i_ref.at[:, 1 - c], o_ref, o_ref, (x, y, 1 - c)
            else:
                src, dst, land, dev = i_ref.at[cj[k]], o_ref.at[k], o_ref.at[k], (*chips[k], c)
            out.append(tuple(pltpu.make_async_remote_copy(
                src_ref=s_, dst_ref=d_, send_sem=send.at[b, k], recv_sem=recv.at[b, k],
                device_id=dev, device_id_type=MESH) for s_, d_ in ((src, dst), (land, land))))
    return out


def _rider_out_shape(kind, a):
    if kind == "scatter":
        return S((3,) + a.shape[1:], a.dtype)
    if kind == "pairx":
        return S(a.shape[:1] + a.shape[2:], a.dtype)
    return S(a.shape, a.dtype)


def _pcall(body, args, riders, *, grid, in_specs, out_specs, out_shape, name, sem, scratch_shapes=()):
    outs = list(out_shape)
    if not riders:
        res = pl.pallas_call(body, grid=grid, in_specs=in_specs, out_specs=out_specs, out_shape=outs, name=name,
                             scratch_shapes=list(scratch_shapes), compiler_params=_cp(*sem))(*args)
        return res, []
    r_in = [a for _, bufs in riders for a in bufs]
    r_out = [_rider_out_shape(kind, a) for kind, bufs in riders for a in bufs]
    n_in, n_out, n_scr, n_r = len(args), len(outs), len(scratch_shapes), len(r_in)
    aliases, idx = {}, 0
    for kind, bufs in riders:
        for _ in bufs:
            if kind in ("ici", "d2d"):
                aliases[n_in + idx] = n_out + idx
            idx += 1
    sems = [pltpu.SemaphoreType.DMA((len(bufs), 3)) for _, bufs in riders for _ in range(2)]

    def wrapped(*refs):
        a, ri = refs[:n_in], refs[n_in:n_in + n_r]
        o, ro = refs[n_in + n_r:n_in + n_r + n_out], refs[n_in + n_r + n_out:n_in + 2 * n_r + n_out]
        scr = refs[n_in + 2 * n_r + n_out:n_in + 2 * n_r + n_out + n_scr]
        sm = refs[n_in + 2 * n_r + n_out + n_scr:]
        pos = _mesh_pos()
        copies, off = [], 0
        for r, (kind, bufs) in enumerate(riders):
            copies += _rider_copies(kind, ri[off:off + len(bufs)], ro[off:off + len(bufs)], sm[2 * r], sm[2 * r + 1], pos)
            off += len(bufs)
        ids = [pl.program_id(d) for d in range(len(grid))]
        first = functools.reduce(jnp.logical_and, [i == 0 for i in ids])
        last = functools.reduce(jnp.logical_and, [i == n - 1 for i, n in zip(ids, grid)])

        @pl.when(first)
        def _():
            for cp, _ in copies:
                cp.start()

        body(*a, *o, *scr)

        @pl.when(last)
        def _():
            for cp, land in copies:
                land.wait_recv()
                cp.wait_send()

    res = pl.pallas_call(
        wrapped, grid=grid, in_specs=list(in_specs) + [ANY] * n_r, out_specs=list(out_specs) + [ANY] * n_r,
        out_shape=outs + r_out, input_output_aliases=aliases, name=name,
        scratch_shapes=list(scratch_shapes) + sems, compiler_params=_cp(*(("arbitrary",) * len(grid))))(*args, *r_in)
    return res[:n_out], res[n_out:]


def _wcol_spec(w):
    return pl.BlockSpec(w.shape, lambda *_: (0, 0, 0))


def _wrow_spec(w):
    return pl.BlockSpec(w.shape, lambda *_: (0, 0))


def _norm_mm(x, g, w, out_dtype, name, tm_pref, riders=(), rope=None):
    t = x.shape[0]
    nc = w.shape[2]
    tm = _row_tile(t, tm_pref)
    extra = list(rope) if rope else []

    def body(x_ref, g_ref, w_ref, *rest):
        o_ref, ht_ref = rest[-2], rest[-1]
        xv = x_ref[...]
        hf = xv * _rms_r(xv) * g_ref[...]
        h = _mx(hf)
        for j in range(N_CHIPS):
            o_ref[:, j * nc:(j + 1) * nc] = jnp.dot(h, w_ref[j], preferred_element_type=F32).astype(o_ref.dtype)
        if rope:
            _rotate_qk(o_ref, rest[0][...], rest[1][...])
        ht_ref[...] = hf.T.astype(ht_ref.dtype)

    (out, ht), rid = _pcall(
        body, [x, g, w] + extra, riders, grid=(t // tm,),
        in_specs=[pl.BlockSpec((tm, D), lambda i: (i, 0)), pl.BlockSpec((1, D), lambda i: (0, 0)), _wcol_spec(w)]
        + [pl.BlockSpec((tm, HD), lambda i: (i, 0)) for _ in extra],
        out_specs=[pl.BlockSpec((tm, N_CHIPS * nc), lambda i: (i, 0)), pl.BlockSpec((D, tm), lambda i: (0, i))],
        out_shape=[S((t, N_CHIPS * nc), out_dtype), S((D, t), MXU_DTYPE)], name=name, sem=("parallel",))
    return out, ht, rid


def _parts_mm_res(parts, w, res, name):
    t = res.shape[0]
    tm = _row_tile(t, 512)
    widths = [p.shape[1] for p in parts]
    offs = np.cumsum([0] + widths)
    n = len(parts)

    def body(*refs):
        p_refs, w_ref, r_ref, o_ref = refs[:n], refs[n], refs[n + 1], refs[n + 2]
        acc = r_ref[...]
        for p in range(n):
            acc = acc + _dot(p_refs[p][...], w_ref[int(offs[p]):int(offs[p + 1]), :])
        o_ref[...] = acc

    return pl.pallas_call(
        body, grid=(t // tm,),
        in_specs=[pl.BlockSpec((tm, wd), lambda i: (i, 0)) for wd in widths]
        + [_wrow_spec(w), pl.BlockSpec((tm, D), lambda i: (i, 0))],
        out_specs=pl.BlockSpec((tm, D), lambda i: (i, 0)),
        out_shape=S((t, D), F32), name=name, compiler_params=_cp("parallel"))(*parts, w, res)


def _swiglu(ff):
    gate = ff[:, :FFN_H].astype(F32)
    up = ff[:, FFN_H:].astype(F32)
    return gate * _sigmoid(gate) * up


def _swiglu_mm_res(ff, w, res, name, riders=()):
    t = res.shape[0]
    tm = _row_tile(t, 512)

    def body(f_ref, w_ref, r_ref, o_ref):
        o_ref[...] = r_ref[...] + _dot(_swiglu(f_ref[...]), w_ref[...])

    (out,), rid = _pcall(
        body, [ff, w, res], riders, grid=(t // tm,),
        in_specs=[pl.BlockSpec((tm, 2 * FFN_H), lambda i: (i, 0)), _wrow_spec(w),
                  pl.BlockSpec((tm, D), lambda i: (i, 0))],
        out_specs=[pl.BlockSpec((tm, D), lambda i: (i, 0))],
        out_shape=[S((t, D), F32)], name=name, sem=("parallel",))
    return out, rid


def _dx_norm(dparts, w, x, g, dres, name, tm_pref, riders=()):
    t = x.shape[0]
    nc = w.shape[2]
    tm = _row_tile(t, tm_pref)
    widths = [p.shape[1] for p in dparts]
    segs = _segments(widths, nc)
    n = len(dparts)

    def body(*refs):
        d_refs = refs[:n]
        w_ref, x_ref, g_ref, r_ref, dx_ref, dg_ref = refs[n:]
        dh = jnp.zeros((tm, D), F32)
        for (p, po, j, jo, wd) in segs:
            dh = dh + _dot_nt(d_refs[p][:, po:po + wd], w_ref[j, :, jo:jo + wd])
        xv = x_ref[...]
        dx, dgrow = _rms_bwd(dh, xv, _rms_r(xv), g_ref[...])
        dx_ref[...] = r_ref[...] + dx
        _acc_out(dg_ref, jnp.sum(dgrow, axis=0, keepdims=True), pl.program_id(0) == 0)

    (dx, dg), rid = _pcall(
        body, [*dparts, w, x, g, dres], riders, grid=(t // tm,),
        in_specs=[pl.BlockSpec((tm, wd), lambda i: (i, 0)) for wd in widths]
        + [_wcol_spec(w), pl.BlockSpec((tm, D), lambda i: (i, 0)),
           pl.BlockSpec((1, D), lambda i: (0, 0)), pl.BlockSpec((tm, D), lambda i: (i, 0))],
        out_specs=[pl.BlockSpec((tm, D), lambda i: (i, 0)), pl.BlockSpec((1, D), lambda i: (0, 0))],
        out_shape=[S((t, D), F32), S((1, D), F32)], name=name, sem=("arbitrary",))
    return dx, dg, rid


def _out_proj_dx(dy, w, a, proj, name, riders=()):
    t = dy.shape[0]
    tm = _row_tile(t, 512)
    wr = HEADS * HD

    def body(dy_ref, w_ref, a_ref, g_ref, dgm_ref, da_ref, dg_ref, dcv_ref):
        dyv = _mx(dy_ref[...])
        dgm_ref[...] = _dot_nt(dyv, w_ref[0:GM_W, :])
        dcv_ref[...] = _dot_nt(dyv, w_ref[GM_W + RET_W:, :])
        for h in range(HEADS):
            cols = slice(h * HD, (h + 1) * HD)
            dyr = _dot_nt(dyv, w_ref[GM_W + h * HD:GM_W + (h + 1) * HD, :])
            o, r = _standardize(a_ref[:, cols])
            gv = g_ref[:, cols]
            s = _sigmoid(gv)
            dg_ref[:, cols] = (dyr * o * (s * (1.0 + gv * (1.0 - s)))).astype(dg_ref.dtype)
            da_ref[:, cols] = _standardize_bwd(dyr * (gv * s), o, r).astype(da_ref.dtype)

    return _pcall(
        body, [dy, w, a, proj], riders, grid=(t // tm,),
        in_specs=[pl.BlockSpec((tm, D), lambda i: (i, 0)), _wrow_spec(w), pl.BlockSpec((tm, wr), lambda i: (i, 0)),
                  pl.BlockSpec((tm, wr), lambda i: (i, GATE_COL))],
        out_specs=[pl.BlockSpec((tm, GM_W), lambda i: (i, 0)), pl.BlockSpec((tm, wr), lambda i: (i, 0)),
                   pl.BlockSpec((tm, wr), lambda i: (i, 0)), pl.BlockSpec((tm, CV_W), lambda i: (i, 0))],
        out_shape=[S((t, GM_W), F32), S((t, wr), ACT_DTYPE), S((t, wr), ACT_DTYPE), S((t, CV_W), F32)],
        name=name, sem=("parallel",))


def _dx_swiglu(dy, w, ff, name):
    t = dy.shape[0]
    tm = _row_tile(t, 512)

    def body(dy_ref, w_ref, f_ref, o_ref):
        dact = _dot_nt(dy_ref[...], w_ref[...])
        gate = f_ref[:, :FFN_H].astype(F32)
        up = f_ref[:, FFN_H:].astype(F32)
        s = _sigmoid(gate)
        gs = gate * s
        o_ref[:, :FFN_H] = ((dact * up) * (s + gs - gs * s)).astype(o_ref.dtype)
        o_ref[:, FFN_H:] = (dact * gs).astype(o_ref.dtype)

    return pl.pallas_call(
        body, grid=(t // tm,),
        in_specs=[pl.BlockSpec((tm, D), lambda i: (i, 0)), _wrow_spec(w),
                  pl.BlockSpec((tm, 2 * FFN_H), lambda i: (i, 0))],
        out_specs=pl.BlockSpec((tm, 2 * FFN_H), lambda i: (i, 0)),
        out_shape=S((t, 2 * FFN_H), ACT_DTYPE), name=name, compiler_params=_cp("parallel"))(dy, w, ff)


def _call_into(body, into, in_specs, args, *, n_prefetch, grid, out_specs, **kw):
    n_in = len(args)
    if into is None:
        gs = pltpu.PrefetchScalarGridSpec(num_scalar_prefetch=n_prefetch, grid=grid, in_specs=in_specs,
                                          out_specs=out_specs)
        return pl.pallas_call(body, grid_spec=gs, **kw)(*args)

    def wrapped(*refs):
        return body(*refs[:n_in], *refs[n_in + 1:])

    gs = pltpu.PrefetchScalarGridSpec(num_scalar_prefetch=n_prefetch, grid=grid,
                                      in_specs=list(in_specs) + [ANY], out_specs=out_specs)
    return pl.pallas_call(wrapped, grid_spec=gs, input_output_aliases={n_in: 0}, **kw)(*args, into)


def _dw_norm_parts(ht, dparts, nc, name):
    t = ht.shape[1]
    tk = _row_tile(t, 1024)
    widths = [p.shape[1] for p in dparts]
    segs = _segments(widths, nc)
    n = len(dparts)
    nk = t // tk

    def body(*refs):
        h_ref, d_refs, o_ref, acc_ref = refs[0], refs[1:1 + n], refs[1 + n], refs[2 + n]
        k = pl.program_id(0)
        h = h_ref[...]

        @pl.when(k == 0)
        def _():
            acc_ref[...] = jnp.zeros_like(acc_ref)

        for (p, po, j, jo, wd) in segs:
            acc_ref[j, :, jo:jo + wd] += _dot(h, d_refs[p][:, po:po + wd])

        @pl.when(k == nk - 1)
        def _():
            o_ref[...] = acc_ref[...].astype(o_ref.dtype)

    return pl.pallas_call(
        body, grid=(nk,),
        in_specs=[pl.BlockSpec((D, tk), lambda k: (0, k))]
        + [pl.BlockSpec((tk, wd), lambda k: (k, 0)) for wd in widths],
        out_specs=pl.BlockSpec((N_CHIPS, D, nc), lambda k: (0, 0, 0)),
        out_shape=S((N_CHIPS, D, nc), MXU_DTYPE), name=name,
        scratch_shapes=[pltpu.VMEM((N_CHIPS, D, nc), F32)], compiler_params=_cp("arbitrary"))(ht, *dparts)


def _dw_norm_cols(ht, dy, nc, name):
    t = ht.shape[1]
    tk = _row_tile(t, 2048)
    nk = t // tk

    def body(h_ref, dy_ref, o_ref, acc_ref):
        k = pl.program_id(1)

        @pl.when(k == 0)
        def _():
            acc_ref[...] = jnp.zeros_like(acc_ref)

        acc_ref[...] += _dot(h_ref[...], dy_ref[...])

        @pl.when(k == nk - 1)
        def _():
            o_ref[...] = acc_ref[...].astype(o_ref.dtype)

    return pl.pallas_call(
        body, grid=(N_CHIPS, nk),
        in_specs=[pl.BlockSpec((D, tk), lambda j, k: (0, k)), pl.BlockSpec((tk, nc), lambda j, k: (k, j))],
        out_specs=pl.BlockSpec((None, D, nc), lambda j, k: (j, 0, 0)),
        out_shape=S((N_CHIPS, D, nc), MXU_DTYPE), name=name,
        scratch_shapes=[pltpu.VMEM((D, nc), F32)], compiler_params=_cp("parallel", "arbitrary"))(ht, dy)


def _dw_parts(parts, dy, name):
    t = dy.shape[0]
    tk = _row_tile(t, 1024)
    widths = [p.shape[1] for p in parts]
    offs = np.cumsum([0] + widths)
    ktot = int(offs[-1])
    n = len(parts)
    nk = t // tk

    def body(*refs):
        p_refs, dy_ref, o_ref, acc_ref = refs[:n], refs[n], refs[n + 1], refs[n + 2]
        k = pl.program_id(0)

        @pl.when(k == 0)
        def _():
            acc_ref[...] = jnp.zeros_like(acc_ref)

        dyv = _mx(dy_ref[...])
        for p in range(n):
            acc_ref[int(offs[p]):int(offs[p + 1]), :] += _dot_tn(p_refs[p][...], dyv)

        @pl.when(k == nk - 1)
        def _():
            o_ref[...] = acc_ref[...].astype(o_ref.dtype)

    return pl.pallas_call(
        body, grid=(nk,),
        in_specs=[pl.BlockSpec((tk, wd), lambda k: (k, 0)) for wd in widths]
        + [pl.BlockSpec((tk, D), lambda k: (k, 0))],
        out_specs=pl.BlockSpec((ktot, D), lambda k: (0, 0)),
        out_shape=S((ktot, D), MXU_DTYPE), name=name,
        scratch_shapes=[pltpu.VMEM((ktot, D), F32)], compiler_params=_cp("arbitrary"))(*parts, dy)


def _dw_swiglu(ff, dy, name):
    t = dy.shape[0]
    tk = _row_tile(t, 512)
    nk = t // tk

    def body(f_ref, dy_ref, o_ref, acc_ref):
        k = pl.program_id(0)

        @pl.when(k == 0)
        def _():
            acc_ref[...] = jnp.zeros_like(acc_ref)

        acc_ref[...] += _dot_tn(_swiglu(f_ref[...]), dy_ref[...])

        @pl.when(k == nk - 1)
        def _():
            o_ref[...] = acc_ref[...].astype(o_ref.dtype)

    return pl.pallas_call(
        body, grid=(nk,),
        in_specs=[pl.BlockSpec((tk, 2 * FFN_H), lambda k: (k, 0)), pl.BlockSpec((tk, D), lambda k: (k, 0))],
        out_specs=pl.BlockSpec((FFN_H, D), lambda k: (0, 0)),
        out_shape=S((FFN_H, D), MXU_DTYPE), name=name,
        scratch_shapes=[pltpu.VMEM((FFN_H, D), F32)], compiler_params=_cp("arbitrary"))(ff, dy)


def _tables(t):
    half = HD // 2
    inv_freq = ROPE_BASE ** (-jnp.arange(half, dtype=F32) / half)
    base = (jnp.arange(t // CH, dtype=F32) * CH)[:, None] * inv_freq[None, :]
    off = jnp.arange(CH, dtype=F32)[:, None] * inv_freq[None, :]
    cb, sb, co, so = jnp.cos(base)[:, None], jnp.sin(base)[:, None], jnp.cos(off)[None], jnp.sin(off)[None]
    cos = (cb * co - sb * so).reshape(t, half)
    sin = (sb * co + cb * so).reshape(t, half)
    tb = {"cos2": jnp.concatenate([cos, cos], axis=1), "sin2": jnp.concatenate([-sin, sin], axis=1)}
    gf = 1.0 - jnp.exp2(-5.0 - jnp.arange(HEADS, dtype=F32))
    lgf = jnp.log(gf)[:, None]
    lgb = jnp.log(gf[::-1])[:, None]
    idx = jnp.arange(CH, dtype=F32)
    diff = idx[:, None] - idx[None, :]
    dfwd = jnp.where(diff >= 0, jnp.exp(lgf[:, :, None] * jnp.where(diff >= 0, diff, 0.0)), 0.0)
    dbwd = jnp.where(diff < 0, jnp.exp(lgb[:, :, None] * jnp.where(diff < 0, -diff, 0.0)), 0.0)
    tb["dm"] = dfwd + dbwd
    tb["dmt"] = jnp.swapaxes(tb["dm"], 1, 2)

    def lanes(a):
        return jnp.repeat(a.T, HD, axis=1)

    tb["xif"] = lanes(jnp.exp(lgf * (idx + 1)))
    tb["zf"] = lanes(jnp.exp(lgf * (CH - 1 - idx)))
    tb["xib"] = lanes(jnp.exp(lgb * (CH - idx)))
    tb["zb"] = lanes(jnp.exp(lgb * idx))
    tb["gcf"] = jnp.repeat(jnp.exp(lgf * CH), HD, axis=0).reshape(1, HEADS * HD)
    tb["gcb"] = jnp.repeat(jnp.exp(lgb * CH), HD, axis=0).reshape(1, HEADS * HD)
    return tb


def _full(shape):
    nd = len(shape)
    return pl.BlockSpec(shape, lambda *_: (0,) * nd)


def _gm_mixed(vn, ws_ref, bias):
    lane = lax.broadcasted_iota(jnp.int32, (CH, 128), 1)
    halves = []
    for hf in range(2):
        vh = _mx(vn[:, hf * 128:(hf + 1) * 128])
        r0 = jnp.dot(_mx(ws_ref[2 * hf]), vh, preferred_element_type=F32)
        r1 = jnp.dot(_mx(ws_ref[2 * hf + 1]), vh, preferred_element_type=F32)
        halves.append(jnp.where(lane < 64, r0, r1))
    return jnp.concatenate(halves, axis=1) + bias


def _gm_fwd(proj, ln_g, ln_b, ws, bias, name, riders=()):
    t = proj.shape[0]
    tm = _row_tile(t, 512)

    def body(pu_ref, pv_ref, g_ref, b_ref, ws_ref, bias_ref, o_ref):
        for c in range(tm // CH):
            rows = slice(c * CH, (c + 1) * CH)
            u = _gelu(pu_ref[rows, :])
            o, _ = _standardize(_gelu(pv_ref[rows, :]))
            vn = o * g_ref[...] + b_ref[...]
            o_ref[rows, :] = (u * _gm_mixed(vn, ws_ref, bias_ref[...])).astype(o_ref.dtype)

    (out,), rid = _pcall(
        body, [proj, proj, ln_g, ln_b, ws, bias], riders, grid=(t // tm,),
        in_specs=[pl.BlockSpec((tm, GM_W), lambda i: (i, 0)), pl.BlockSpec((tm, GM_W), lambda i: (i, 1)),
                  _full((1, GM_W)), _full((1, GM_W)), _full((GM_HEADS, CH, CH)), _full((CH, GM_W))],
        out_specs=[pl.BlockSpec((tm, GM_W), lambda i: (i, 0))],
        out_shape=[S((t, GM_W), ACT_DTYPE)], name=name, sem=("parallel",))
    return out, rid


def _gm_bwd(proj, dy, ln_g, ln_b, ws, wst, bias, name):
    t = proj.shape[0]
    tm = _row_tile(t, 512)
    nb = t // tm

    def body(pu_ref, pv_ref, dy_ref, g_ref, b_ref, ws_ref, wst_ref, bias_ref,
             d_ref, dws_ref, dbs_ref, dg_ref, db_ref, dbias_ref):
        first = pl.program_id(0) == 0
        lane = lax.broadcasted_iota(jnp.int32, (CH, 128), 1)
        dws = [jnp.zeros((CH, CH), F32) for _ in range(GM_HEADS)]
        dbias = jnp.zeros((CH, GM_W), F32)
        dg = jnp.zeros((1, GM_W), F32)
        db = jnp.zeros((1, GM_W), F32)
        for c in range(tm // CH):
            rows = slice(c * CH, (c + 1) * CH)
            pu = pu_ref[rows, :]
            pv = pv_ref[rows, :]
            u = _gelu(pu)
            o, r = _standardize(_gelu(pv))
            vn = o * g_ref[...] + b_ref[...]
            mixed = _gm_mixed(vn, ws_ref, bias_ref[...])
            dyv = dy_ref[rows, :]
            d_ref[rows, :GM_W] = (dyv * mixed * _gelu_grad(pu)).astype(d_ref.dtype)
            dmixed = dyv * u
            dbias = dbias + dmixed
            dvn_halves = []
            for hf in range(2):
                dm = dmixed[:, hf * 128:(hf + 1) * 128]
                vh = vn[:, hf * 128:(hf + 1) * 128]
                dm0 = jnp.where(lane < 64, dm, 0.0)
                dm1 = dm - dm0
                dws[2 * hf] = dws[2 * hf] + _dot_nt(dm0, vh)
                dws[2 * hf + 1] = dws[2 * hf + 1] + _dot_nt(dm1, vh)
                t0 = _dot(wst_ref[2 * hf], dm)
                t1 = _dot(wst_ref[2 * hf + 1], dm)
                dvn_halves.append(jnp.where(lane < 64, t0, t1))
            dvn = jnp.concatenate(dvn_halves, axis=1)
            dg = dg + jnp.sum(dvn * o, axis=0, keepdims=True)
            db = db + jnp.sum(dvn, axis=0, keepdims=True)
            dv = _standardize_bwd(dvn * g_ref[...], o, r)
            d_ref[rows, GM_W:] = (dv * _gelu_grad(pv)).astype(d_ref.dtype)
        for h in range(GM_HEADS):
            _acc_out(dws_ref.at[h], dws[h], first)
        _acc_out(dbias_ref, dbias, first)
        _acc_out(dg_ref, dg, first)
        _acc_out(db_ref, db, first)

        @pl.when(pl.program_id(0) == nb - 1)
        def _():
            tot = dbias_ref[...]
            head = lax.broadcasted_iota(jnp.int32, (CH, GM_W), 1) // (GM_W // GM_HEADS)
            out = jnp.zeros((CH, 128), F32)
            for h in range(GM_HEADS):
                s = jnp.sum(jnp.where(head == h, tot, 0.0), axis=1, keepdims=True)
                out = jnp.where(lane == h, s, out)
            dbs_ref[...] = out

    return pl.pallas_call(
        body, grid=(nb,),
        in_specs=[pl.BlockSpec((tm, GM_W), lambda i: (i, 0)), pl.BlockSpec((tm, GM_W), lambda i: (i, 1)),
                  pl.BlockSpec((tm, GM_W), lambda i: (i, 0)),
                  _full((1, GM_W)), _full((1, GM_W)), _full((GM_HEADS, CH, CH)), _full((GM_HEADS, CH, CH)),
                  _full((CH, GM_W))],
        out_specs=[pl.BlockSpec((tm, 2 * GM_W), lambda i: (i, 0)), _full((GM_HEADS, CH, CH)), _full((CH, 128)),
                   _full((1, GM_W)), _full((1, GM_W))],
        out_shape=[S((t, 2 * GM_W), ACT_DTYPE), S((GM_HEADS, CH, CH), F32), S((CH, 128), F32),
                   S((1, GM_W), F32), S((1, GM_W), F32)],
        scratch_shapes=[pltpu.VMEM((CH, GM_W), F32)],
        name=name, compiler_params=_cp("arbitrary"))(proj, proj, dy, ln_g, ln_b, ws, wst, bias)


def _rot(x, cos2, sin2):
    return x * cos2 + pltpu.roll(x, HD // 2, 1) * sin2


def _rot_bwd(dx, cos2, sin2):
    return dx * cos2 + pltpu.roll(dx * sin2, HD // 2, 1)


Q_COL, K_COL, V_COL, GATE_COL = 1, 2, 3, 4


def _rotate_qk(o_ref, cos2, sin2):
    for col, scale in ((Q_COL, 1.0), (K_COL, HD ** -0.5)):
        for h in range(HEADS):
            cols = slice(col * RET_W + h * HD, col * RET_W + (h + 1) * HD)
            o_ref[:, cols] = _rot(o_ref[:, cols], cos2, sin2) * scale


def _ret_scan(lhs, lhs_col, rhs, rhs_col, lp, ls, gp, gs, name):
    t = lhs.shape[0]
    n = t // CH
    r = 4 if n % 4 == 0 else 1
    ns = n // r

    def body(lp_ref, ls_ref, gp_ref, gs_ref, l1_ref, r1_ref, l2_ref, r2_ref, pre_ref, suf_ref, sp_ref, ss_ref):
        @pl.when(pl.program_id(0) == 0)
        def _():
            sp_ref[...] = jnp.zeros_like(sp_ref)
            ss_ref[...] = jnp.zeros_like(ss_ref)

        def kv(l_ref, r_ref, scale, rows):
            lv = l_ref[rows, :] * scale
            rv = r_ref[rows, :]
            return jnp.concatenate([_dot_tn(lv[:, h * HD:(h + 1) * HD], rv[:, h * HD:(h + 1) * HD])
                                    for h in range(HEADS)], axis=1)

        for j in range(r):
            pre_ref[j] = sp_ref[...]
            sp_ref[...] = sp_ref[...] * gp_ref[...] + kv(l1_ref, r1_ref, lp_ref[...], slice(j * CH, (j + 1) * CH))
        for j in reversed(range(r)):
            suf_ref[j] = ss_ref[...]
            ss_ref[...] = ss_ref[...] * gs_ref[...] + kv(l2_ref, r2_ref, ls_ref[...], slice(j * CH, (j + 1) * CH))

    w = HEADS * HD
    return pl.pallas_call(
        body, grid=(ns,),
        in_specs=[_full((CH, w)), _full((CH, w)), _full((1, w)), _full((1, w)),
                  pl.BlockSpec((r * CH, w), lambda s: (s, lhs_col)), pl.BlockSpec((r * CH, w), lambda s: (s, rhs_col)),
                  pl.BlockSpec((r * CH, w), lambda s: (ns - 1 - s, lhs_col)),
                  pl.BlockSpec((r * CH, w), lambda s: (ns - 1 - s, rhs_col))],
        out_specs=[pl.BlockSpec((r, HD, w), lambda s: (s, 0, 0)), pl.BlockSpec((r, HD, w), lambda s: (ns - 1 - s, 0, 0))],
        out_shape=[S((n, HD, w), F32)] * 2, name=name,
        scratch_shapes=[pltpu.VMEM((HD, w), F32), pltpu.VMEM((HD, w), F32)],
        compiler_params=_cp("arbitrary"))(lp, ls, gp, gs, lhs, rhs, lhs, rhs)


def _ret_out(proj, sf, sb, tb, name, riders=()):
    t = proj.shape[0]
    r = 4 if (t // CH) % 4 == 0 else 1
    tm = r * CH
    w = HEADS * HD

    def body(rq_ref, rk_ref, v_ref, g_ref, sf_ref, sb_ref, dm_ref, xif_ref, xib_ref, a_ref, y_ref):
        for c in range(r):
            rows = slice(c * CH, (c + 1) * CH)
            for h in range(HEADS):
                cols = slice(h * HD, (h + 1) * HD)
                q = rq_ref[rows, cols]
                p = _dot_nt(q, rk_ref[rows, cols]) * dm_ref[h]
                a = (_dot(p, v_ref[rows, cols]) + _dot(q * xif_ref[:, cols], sf_ref[c, :, cols])
                     + _dot(q * xib_ref[:, cols], sb_ref[c, :, cols]))
                a_ref[rows, cols] = a
                o, _ = _standardize(a)
                gv = g_ref[rows, cols]
                y_ref[rows, cols] = (o * (gv * _sigmoid(gv))).astype(y_ref.dtype)

    (a, y), rid = _pcall(
        body, [proj, proj, proj, proj, sf, sb, tb["dm"], tb["xif"], tb["xib"]], riders, grid=(t // tm,),
        in_specs=[pl.BlockSpec((tm, w), lambda i: (i, Q_COL)), pl.BlockSpec((tm, w), lambda i: (i, K_COL)),
                  pl.BlockSpec((tm, w), lambda i: (i, V_COL)), pl.BlockSpec((tm, w), lambda i: (i, GATE_COL)),
                  pl.BlockSpec((r, HD, w), lambda i: (i, 0, 0)), pl.BlockSpec((r, HD, w), lambda i: (i, 0, 0)),
                  _full((HEADS, CH, CH)), _full((CH, w)), _full((CH, w))],
        out_specs=[pl.BlockSpec((tm, w), lambda i: (i, 0))] * 2,
        out_shape=[S((t, w), F32), S((t, w), ACT_DTYPE)], name=name, sem=("parallel",))
    return a, y, rid


def _ret_bwd_main(proj, da, sf, sb, gf, gb, tb, name, riders=()):
    t = proj.shape[0]
    r = 4 if (t // CH) % 4 == 0 else 1
    tm = r * CH
    w = HEADS * HD
    scale = HD ** -0.5

    def body(rq_ref, rk_ref, v_ref, da_ref, sf_ref, sb_ref, gf_ref, gb_ref, dm_ref, dmt_ref,
             xif_ref, xib_ref, zf_ref, zb_ref, c_ref, s_ref, o_ref):
        for c in range(r):
            rows = slice(c * CH, (c + 1) * CH)
            cos2, sin2 = c_ref[rows, :], s_ref[rows, :]
            for h in range(HEADS):
                cols = slice(h * HD, (h + 1) * HD)
                q, k, v, dav = rq_ref[rows, cols], rk_ref[rows, cols], v_ref[rows, cols], da_ref[rows, cols]
                qm, km, vm, dam = _mx(q), _mx(k), _mx(v), _mx(dav)
                dm, dmt = dm_ref[h], dmt_ref[h]
                pt = _dot_nt(km, qm) * dmt
                dp = _dot_nt(dam, vm) * dm
                dpt = _dot_nt(vm, dam) * dmt
                sfh, sbh, gfh, gbh = sf_ref[c, :, cols], sb_ref[c, :, cols], gf_ref[c, :, cols], gb_ref[c, :, cols]
                zf, zb = zf_ref[:, cols], zb_ref[:, cols]
                dv = _dot(pt, dam) + zf * _dot(km, gfh) + zb * _dot(km, gbh)
                drq = _dot(dp, km) + xif_ref[:, cols] * _dot_nt(dam, sfh) + xib_ref[:, cols] * _dot_nt(dam, sbh)
                drk = _dot(dpt, qm) + _dot_nt(zf * v, gfh) + _dot_nt(zb * v, gbh)
                o_ref[rows, h * HD:(h + 1) * HD] = _rot_bwd(drq, cos2, sin2).astype(o_ref.dtype)
                o_ref[rows, w + h * HD:w + (h + 1) * HD] = (_rot_bwd(drk, cos2, sin2) * scale).astype(o_ref.dtype)
                o_ref[rows, 2 * w + h * HD:2 * w + (h + 1) * HD] = dv.astype(o_ref.dtype)

    st = pl.BlockSpec((r, HD, w), lambda i: (i, 0, 0))
    (out,), rid = _pcall(
        body, [proj, proj, proj, da, sf, sb, gf, gb, tb["dm"], tb["dmt"], tb["xif"], tb["xib"], tb["zf"], tb["zb"],
               tb["cos2"], tb["sin2"]], riders, grid=(t // tm,),
        in_specs=[pl.BlockSpec((tm, w), lambda i: (i, Q_COL)), pl.BlockSpec((tm, w), lambda i: (i, K_COL)),
                  pl.BlockSpec((tm, w), lambda i: (i, V_COL)), pl.BlockSpec((tm, w), lambda i: (i, 0)),
                  st, st, st, st, _full((HEADS, CH, CH)), _full((HEADS, CH, CH)),
                  _full((CH, w)), _full((CH, w)), _full((CH, w)), _full((CH, w)),
                  pl.BlockSpec((tm, HD), lambda i: (i, 0)), pl.BlockSpec((tm, HD), lambda i: (i, 0))],
        out_specs=[pl.BlockSpec((tm, 3 * w), lambda i: (i, 0))],
        out_shape=[S((t, 3 * w), ACT_DTYPE)], name=name, sem=("parallel",))
    return out, rid


CONV_TM = 256
CONV_SUB = 64
A_COL = (2 * GM_W + 4 * RET_W) // CV_W
G_COL = A_COL + 1


def _halo_specs(t, tm, col):
    nb16 = t // HALO
    per = tm // HALO
    return [pl.BlockSpec((tm, CV_W), lambda i: (i, col)),
            pl.BlockSpec((HALO, CV_W), lambda i: (jnp.maximum(i * per - 1, 0), col)),
            pl.BlockSpec((HALO, CV_W), lambda i: (jnp.minimum((i + 1) * per, nb16 - 1), col))]


def _fill_padded(dst_ref, prev, main, nxt, tm, i, nb):
    dst_ref[0:HALO, :] = jnp.where(i > 0, prev, 0.0)
    dst_ref[HALO:HALO + tm, :] = main
    dst_ref[HALO + tm:2 * HALO + tm, :] = jnp.where(i < nb - 1, nxt, 0.0)


SUBLANES = 8


def _fill_shifted(sh_ref, src_ref, tm):
    n = tm + 2 * HALO - SUBLANES
    for b in range(SUBLANES):
        sh_ref[b, 0:n, :] = src_ref[pl.ds(b, n), :]


def _tap(sh_ref, off, rows):
    return sh_ref[off % SUBLANES, pl.ds(off - off % SUBLANES, rows), :]


def _conv_fwd(proj, cw, cb, ln_g, ln_b, name, riders=()):
    t = proj.shape[0]
    tm = _row_tile(t, CONV_TM)
    nb = t // tm

    def body(a_ref, ap_ref, an_ref, g_ref, gp_ref, gn_ref, w_ref, b_ref, lg_ref, lb_ref, y_ref, hc_ref,
             hp_ref, sh_ref):
        i = pl.program_id(0)
        _fill_padded(hp_ref, ap_ref[...] * _sigmoid(gp_ref[...]), a_ref[...] * _sigmoid(g_ref[...]),
                     an_ref[...] * _sigmoid(gn_ref[...]), tm, i, nb)
        _fill_shifted(sh_ref, hp_ref, tm)
        for sb in range(tm // CONV_SUB):
            acc = jnp.zeros((CONV_SUB, CV_W), F32) + b_ref[...]
            for k in range(KCONV):
                acc = acc + w_ref[k:k + 1, :] * _tap(sh_ref, sb * CONV_SUB + k + 1, CONV_SUB)
            rows = slice(sb * CONV_SUB, (sb + 1) * CONV_SUB)
            hc_ref[rows, :] = acc
            o, _ = _standardize(acc)
            z = o * lg_ref[...] + lb_ref[...]
            y_ref[rows, :] = (z * _sigmoid(z)).astype(y_ref.dtype)

    (y, hc), rid = _pcall(
        body, [proj, proj, proj, proj, proj, proj, cw, cb, ln_g, ln_b], riders, grid=(nb,),
        in_specs=_halo_specs(t, tm, A_COL) + _halo_specs(t, tm, G_COL)
        + [_full((32, CV_W)), _full((1, CV_W)), _full((1, CV_W)), _full((1, CV_W))],
        out_specs=[pl.BlockSpec((tm, CV_W), lambda i: (i, 0))] * 2,
        out_shape=[S((t, CV_W), ACT_DTYPE), S((t, CV_W), F32)], name=name, sem=("parallel",),
        scratch_shapes=[pltpu.VMEM((tm + 2 * HALO, CV_W), F32), pltpu.VMEM((SUBLANES, tm + 2 * HALO, CV_W), F32)])
    return y, hc, rid


def _conv_bwd(proj, dy, hc, cw, ln_g, ln_b, name, riders=()):
    t = proj.shape[0]
    tm = _row_tile(t, CONV_TM)
    nb = t // tm

    def body(a_ref, ap_ref, an_ref, g_ref, gp_ref, gn_ref, dy_ref, dyp_ref, dyn_ref, hc_ref, hcp_ref, hcn_ref,
             w_ref, lg_ref, lb_ref, d_ref, dw_ref, dcb_ref, dlg_ref, dlb_ref, hp_ref, dhp_ref, dwacc_ref,
             sh_ref, dsh_ref):
        i = pl.program_id(0)
        first = i == 0

        def dhc_of(dyv, hcv):
            o, r = _standardize(hcv)
            z = o * lg_ref[...] + lb_ref[...]
            s = _sigmoid(z)
            dz = dyv * (s * (1.0 + z * (1.0 - s)))
            return _standardize_bwd(dz * lg_ref[...], o, r), dz, o

        dhc, dz, o = dhc_of(dy_ref[...], hc_ref[...])
        _acc_out(dlg_ref, jnp.sum(dz * o, axis=0, keepdims=True), first)
        _acc_out(dlb_ref, jnp.sum(dz, axis=0, keepdims=True), first)
        _acc_out(dcb_ref, jnp.sum(dhc, axis=0, keepdims=True), first)
        _fill_padded(dhp_ref, dhc_of(dyp_ref[...], hcp_ref[...])[0], dhc, dhc_of(dyn_ref[...], hcn_ref[...])[0],
                     tm, i, nb)
        _fill_padded(hp_ref, ap_ref[...] * _sigmoid(gp_ref[...]), a_ref[...] * _sigmoid(g_ref[...]),
                     an_ref[...] * _sigmoid(gn_ref[...]), tm, i, nb)

        _fill_shifted(sh_ref, hp_ref, tm)
        _fill_shifted(dsh_ref, dhp_ref, tm)

        @pl.when(first)
        def _():
            dwacc_ref[...] = jnp.zeros_like(dwacc_ref)

        for sb in range(tm // CONV_SUB):
            base = sb * CONV_SUB
            dmain = dhp_ref[pl.ds(HALO + base, CONV_SUB), :]
            dh = jnp.zeros((CONV_SUB, CV_W), F32)
            for k in range(KCONV):
                dh = dh + w_ref[k:k + 1, :] * _tap(dsh_ref, base + 2 * HALO - 1 - k, CONV_SUB)
                prod = dmain * _tap(sh_ref, base + k + 1, CONV_SUB)
                dwacc_ref[k * 8:(k + 1) * 8, :] += jnp.sum(prod.reshape(CONV_SUB // 8, 8, CV_W), axis=0)
            rows = slice(base, base + CONV_SUB)
            s = _sigmoid(g_ref[rows, :])
            d_ref[rows, :CV_W] = (dh * s).astype(d_ref.dtype)
            d_ref[rows, CV_W:] = (dh * a_ref[rows, :] * (s * (1.0 - s))).astype(d_ref.dtype)

        @pl.when(i == nb - 1)
        def _():
            for k in range(KCONV):
                dw_ref[k:k + 1, :] = jnp.sum(dwacc_ref[k * 8:(k + 1) * 8, :], axis=0, keepdims=True)
            dw_ref[KCONV:32, :] = jnp.zeros((32 - KCONV, CV_W), F32)

    hs = [pl.BlockSpec((tm, CV_W), lambda i: (i, 0)),
          pl.BlockSpec((HALO, CV_W), lambda i: (jnp.maximum(i * (tm // HALO) - 1, 0), 0)),
          pl.BlockSpec((HALO, CV_W), lambda i: (jnp.minimum((i + 1) * (tm // HALO), t // HALO - 1), 0))]
    outs, rid = _pcall(
        body, [proj, proj, proj, proj, proj, proj, dy, dy, dy, hc, hc, hc, cw, ln_g, ln_b], riders, grid=(nb,),
        in_specs=_halo_specs(t, tm, A_COL) + _halo_specs(t, tm, G_COL) + hs + hs
        + [_full((32, CV_W)), _full((1, CV_W)), _full((1, CV_W))],
        out_specs=[pl.BlockSpec((tm, 2 * CV_W), lambda i: (i, 0)), _full((32, CV_W)), _full((1, CV_W)),
                   _full((1, CV_W)), _full((1, CV_W))],
        out_shape=[S((t, 2 * CV_W), ACT_DTYPE), S((32, CV_W), F32), S((1, CV_W), F32), S((1, CV_W), F32),
                   S((1, CV_W), F32)],
        name=name, sem=("arbitrary",),
        scratch_shapes=[pltpu.VMEM((tm + 2 * HALO, CV_W), F32), pltpu.VMEM((tm + 2 * HALO, CV_W), F32),
                        pltpu.VMEM((32 * 8, CV_W), F32), pltpu.VMEM((SUBLANES, tm + 2 * HALO, CV_W), F32),
                        pltpu.VMEM((SUBLANES, tm + 2 * HALO, CV_W), F32)])
    return (*outs, rid)


def _loss_head(x, g, target, name):
    t = x.shape[0]
    tm = _row_tile(t, 512)

    def body(x_ref, g_ref, t_ref, dx_ref, dg_ref, l_ref):
        first = pl.program_id(0) == 0
        xv = x_ref[...]
        r = _rms_r(xv)
        e = xv * r * g_ref[...] - t_ref[...]
        dx, dgrow = _rms_bwd(e * (1.0 / D), xv, r, g_ref[...])
        dx_ref[...] = dx
        _acc_out(dg_ref, jnp.sum(dgrow, axis=0, keepdims=True), first)
        part = 0.5 * jnp.sum(jnp.mean(e * e, axis=-1, keepdims=True), axis=0, keepdims=True)
        _acc_out(l_ref, jnp.broadcast_to(part, (8, 128)), first)

    return pl.pallas_call(
        body, grid=(t // tm,),
        in_specs=[pl.BlockSpec((tm, D), lambda i: (i, 0)), _full((1, D)), pl.BlockSpec((tm, D), lambda i: (i, 0))],
        out_specs=[pl.BlockSpec((tm, D), lambda i: (i, 0)), _full((1, D)), _full((8, 128))],
        out_shape=[S((t, D), F32), S((1, D), F32), S((8, 128), F32)], name=name,
        compiler_params=_cp("arbitrary"))(x, g, target)


def _as2d(a):
    return a.reshape(-1, a.shape[-1])


def _ew_tile(rows, cols, n_arrays):
    budget = VMEM_LIMIT // 2
    tr = rows
    while tr * cols * 4 * n_arrays * 2 > budget and tr % 16 == 0:
        tr //= 2
    assert rows % tr == 0
    return tr


def _adamw(w, g, m, v, name):
    shape = w.shape
    w2, g2, m2, v2 = _as2d(w), _as2d(g), _as2d(m), _as2d(v)
    rows, cols = w2.shape
    tr = _ew_tile(rows, cols, 7)

    def body(w_ref, g_ref, m_ref, v_ref, d_ref, nm_ref, nv_ref):
        gv = g_ref[...]
        nm = ADAM_B1 * m_ref[...] + (1.0 - ADAM_B1) * gv
        nv = ADAM_B2 * v_ref[...] + (1.0 - ADAM_B2) * (gv * gv)
        m_hat = nm / (1.0 - ADAM_B1 ** ADAM_STEP)
        v_hat = nv / (1.0 - ADAM_B2 ** ADAM_STEP)
        d_ref[...] = -ADAM_LR * (m_hat / (jnp.sqrt(v_hat) + ADAM_EPS) + ADAM_WD * w_ref[...])
        nm_ref[...] = nm
        nv_ref[...] = nv

    spec = pl.BlockSpec((tr, cols), lambda i: (i, 0))
    outs = pl.pallas_call(body, grid=(rows // tr,), in_specs=[spec] * 4, out_specs=[spec] * 3,
                          out_shape=[S((rows, cols), F32)] * 3, name=name,
                          compiler_params=_cp("parallel"))(w2, g2, m2, v2)
    return tuple(o.reshape(shape) for o in outs)


BIG = (("w_in", "col"), ("w_out", "row"), ("w_ffn_in", "col"), ("w_ffn_out", "row"))
NBIG = len(BIG)


def _cast_to_gathered(w, l, me, name):
    _, r_, c_ = w.shape
    tr = _ew_tile(r_, c_, 2)

    def body(me_ref, w_ref, o_ref):
        o_ref[...] = w_ref[...].astype(o_ref.dtype)

    gs = pltpu.PrefetchScalarGridSpec(
        num_scalar_prefetch=1, grid=(r_ // tr,),
        in_specs=[pl.BlockSpec((None, tr, c_), lambda i, s: (l, i, 0))],
        out_specs=pl.BlockSpec((None, tr, c_), lambda i, s: (s[0], i, 0)))
    out = pl.pallas_call(body, grid_spec=gs, out_shape=S((N_CHIPS, r_, c_), MXU_DTYPE), name=name,
                         compiler_params=_cp("parallel"))(me.reshape(1), w)
    return out.reshape(N_CHIPS, 2, r_ // 2, c_)


def _all_gather(bufs, name, per_core=False):
    n = len(bufs)

    def body(*refs):
        i_refs, o_refs = refs[:n], refs[n:2 * n]
        isend, irecv, dsend, drecv, osend, orecv = refs[2 * n:]
        pos = _mesh_pos()
        x, y, c, me, _, _ = pos
        ici = _rider_copies("ici", i_refs, o_refs, isend, irecv, pos)
        d2d = _rider_copies("d2d", o_refs, o_refs, dsend, drecv, pos)
        own = []
        if per_core:
            for b in range(n):
                own.append(tuple(pltpu.make_async_remote_copy(
                    src_ref=s_, dst_ref=d_, send_sem=osend.at[b], recv_sem=orecv.at[b],
                    device_id=(x, y, 1 - c), device_id_type=MESH)
                    for s_, d_ in ((i_refs[b].at[me, c], o_refs[b].at[me, c]),
                                   (o_refs[b].at[me, 1 - c], o_refs[b].at[me, 1 - c]))))
        for cp, _ in ici + own:
            cp.start()
        for (_, land), (fwd, _) in zip(ici, d2d):
            land.wait_recv()
            fwd.start()
        for _, land in d2d + own:
            land.wait_recv()
        for cp, _ in ici + d2d + own:
            cp.wait_send()

    return pl.pallas_call(
        body, in_specs=[ANY] * n, out_specs=[ANY] * n, out_shape=[S(a.shape, a.dtype) for a in bufs],
        input_output_aliases={w: w for w in range(n)}, name=name,
        scratch_shapes=[pltpu.SemaphoreType.DMA((n, 3))] * 4 + [pltpu.SemaphoreType.DMA((n,))] * 2)(*bufs)


def _pair_exchange(grads, name):
    n = len(grads)

    def body(*refs):
        g_refs, theirs = refs[:n], refs[n:2 * n]
        send, recv = refs[2 * n:]
        x, y, c, *_ = _mesh_pos()
        cps = []
        for w in range(n):
            cp = pltpu.make_async_remote_copy(
                src_ref=g_refs[w].at[:, 1 - c], dst_ref=theirs[w], send_sem=send.at[w], recv_sem=recv.at[w],
                device_id=(x, y, 1 - c), device_id_type=MESH)
            cp.start()
            cps.append(cp)
        for cp in cps:
            cp.wait()

    return pl.pallas_call(
        body, in_specs=[ANY] * n, out_specs=[ANY] * n,
        out_shape=[S(a.shape[:1] + a.shape[2:], a.dtype) for a in grads], name=name,
        scratch_shapes=[pltpu.SemaphoreType.DMA((n,))] * 2)(*grads)


def _pair_sum(g, theirs, core, name):
    _, _, rh, c_ = g.shape
    tr = _ew_tile(rh, c_, 2)

    def body(s_ref, g_ref, t_ref, o_ref):
        o_ref[...] = (g_ref[...].astype(F32) + t_ref[...].astype(F32)).astype(o_ref.dtype)

    blk = pl.BlockSpec((None, tr, c_), lambda j, i, s: (j, i, 0))
    gs = pltpu.PrefetchScalarGridSpec(
        num_scalar_prefetch=1, grid=(N_CHIPS, rh // tr),
        in_specs=[pl.BlockSpec((None, None, tr, c_), lambda j, i, s: (j, s[0], i, 0)), blk], out_specs=blk)
    return pl.pallas_call(body, grid_spec=gs, out_shape=S(theirs.shape, theirs.dtype), name=name,
                          compiler_params=_cp("parallel", "parallel"))(core.reshape(1), g, theirs)


def _chip_sum(q, got, l, me, core, into, name):
    _, rh, c_ = got.shape
    tr = _ew_tile(rh, c_, 4)

    def body(s_ref, q_ref, g0_ref, g1_ref, g2_ref, o_ref):
        acc = q_ref[...].astype(F32)
        for r in (g0_ref, g1_ref, g2_ref):
            acc = acc + r[...].astype(F32)
        o_ref[...] = acc

    in_specs = [pl.BlockSpec((None, tr, c_), lambda i, s: (s[0], i, 0))] + [
        pl.BlockSpec((None, tr, c_), functools.partial(lambda k, i, s: (k, i, 0), k)) for k in range(3)]
    return _call_into(
        body, into, in_specs, [jnp.stack([me, core]), q, got, got, got], n_prefetch=1, grid=(rh // tr,),
        out_specs=pl.BlockSpec((None, None, tr, c_), lambda i, s: (l, s[1], i, 0)),
        out_shape=S((DEPTH, 2, rh, c_), F32), name=name, compiler_params=_cp("parallel"))


def _pair_gather(gs4):
    def body(*refs):
        i_refs, o_refs = refs[:NBIG], refs[NBIG:2 * NBIG]
        send, recv = refs[2 * NBIG:]
        x, y, c, *_ = _mesh_pos()
        cps = []
        for w in range(NBIG):
            cp = pltpu.make_async_remote_copy(
                src_ref=i_refs[w].at[:, c], dst_ref=o_refs[w].at[:, c], send_sem=send.at[w], recv_sem=recv.at[w],
                device_id=(x, y, 1 - c), device_id_type=MESH)
            cp.start()
            cps.append(cp)
        for cp in cps:
            cp.wait()

    outs = pl.pallas_call(
        body, in_specs=[ANY] * NBIG, out_specs=[ANY] * NBIG, out_shape=[S(a.shape, a.dtype) for a in gs4],
        input_output_aliases={w: w for w in range(NBIG)}, name="grad_pair_gather",
        scratch_shapes=[pltpu.SemaphoreType.DMA((NBIG,))] * 2)(*gs4)
    return [o.reshape(o.shape[0], 2 * o.shape[2], o.shape[3]) for o in outs]


def _all_reduce_small(p, me, core, name):
    rows = p.shape[0]

    def place(s_ref, p_ref, o_ref):
        o_ref[...] = p_ref[...]

    gs = pltpu.PrefetchScalarGridSpec(
        num_scalar_prefetch=1, grid=(1,), in_specs=[pl.BlockSpec((rows, 128), lambda i, s: (0, 0))],
        out_specs=pl.BlockSpec((None, None, rows, 128), lambda i, s: (s[0], s[1], 0, 0)))
    mine = pl.pallas_call(place, grid_spec=gs, out_shape=S((N_CHIPS, 2, rows, 128), F32), name=name + "_place",
                          compiler_params=_cp("arbitrary"))(jnp.stack([me, core]), p)
    parts = _all_gather([mine], name + "_gather", per_core=True)[0]

    def total(g_ref, o_ref):
        acc = g_ref[0, 0]
        for j in range(N_CHIPS):
            for c in range(2):
                if (j, c) != (0, 0):
                    acc = acc + g_ref[j, c]
        o_ref[...] = acc

    vm = pl.BlockSpec(memory_space=pltpu.VMEM)
    return pl.pallas_call(total, in_specs=[vm], out_specs=vm, out_shape=S((rows, 128), F32), name=name + "_sum",
                          compiler_params=pltpu.CompilerParams(vmem_limit_bytes=VMEM_LIMIT))(parts)


PACK_UNIT = 8 * 128


def _pack(arrs):
    parts = []
    for a in arrs:
        flat = a.reshape(-1)
        pad = (-flat.shape[0]) % PACK_UNIT
        parts.append(jnp.pad(flat, (0, pad)).reshape(-1, 128))
    return jnp.concatenate(parts, axis=0)


def _unpack(buf, shapes):
    outs, row = [], 0
    for shp in shapes:
        n = int(np.prod(shp))
        rows = -(-n // PACK_UNIT) * 8
        outs.append(buf[row:row + rows].reshape(-1)[:n].reshape(shp))
        row += rows
    return outs


SMALL = ("norm1_g", "gm_ln_g", "gm_ln_b", "gm_ws", "gm_bs", "conv_w", "conv_b", "conv_ln_g", "conv_ln_b",
         "norm2_g", "final_g")
WEIGHTS = ("norm1_g", "w_in", "gm_ln_g", "gm_ln_b", "gm_ws", "gm_bs", "conv_w", "conv_b", "conv_ln_g",
           "conv_ln_b", "w_out", "norm2_g", "w_ffn_in", "w_ffn_out", "final_g")


def kernel(x, norm1_g, w_in, gm_ln_g, gm_ln_b, gm_ws, gm_bs, conv_w, conv_b, conv_ln_g, conv_ln_b, w_out, norm2_g, w_ffn_in, w_ffn_out, final_g, loss_target, m_norm1_g, m_w_in, m_gm_ln_g, m_gm_ln_b, m_gm_ws, m_gm_bs, m_conv_w, m_conv_b, m_conv_ln_g, m_conv_ln_b, m_w_out, m_norm2_g, m_w_ffn_in, m_w_ffn_out, m_final_g, v_norm1_g, v_w_in, v_gm_ln_g, v_gm_ln_b, v_gm_ws, v_gm_bs, v_conv_w, v_conv_b, v_conv_ln_g, v_conv_ln_b, v_w_out, v_norm2_g, v_w_ffn_in, v_w_ffn_out, v_final_g):
    given = dict(locals())
    t = x.shape[1]
    xc = x.reshape(t, D)
    target = loss_target.reshape(t, D)
    me = 2 * lax.axis_index("x") + lax.axis_index("y")
    core = lax.axis_index("c")
    tb = _tables(t)

    me = me.astype(jnp.int32)
    core = core.astype(jnp.int32)
    names = [n for n, _ in BIG]
    kinds = dict(BIG)
    gathered = [{n: _cast_to_gathered(given[n], l, me, f"cast_{n}{l}") for n in names} for l in range(DEPTH)]
    gathered[0]["w_in"] = _all_gather([gathered[0]["w_in"]], "all_gather_w_in0")[0]

    def weight(l, n):
        b = gathered[l][n]
        r_, c_ = 2 * b.shape[2], b.shape[3]
        return b.reshape(N_CHIPS, r_, c_) if kinds[n] == "col" else b.reshape(N_CHIPS * r_, c_)

    cshard = CV_W // N_CHIPS
    placed = lax.dynamic_update_slice(jnp.zeros((DEPTH, KCONV, CV_W), F32),
                                      conv_w * (core == 0).astype(F32), (0, 0, me * cshard))
    conv_w_full = _unpack(_all_reduce_small(_pack([placed]), me, core, "gather_conv_w"), [(DEPTH, KCONV, CV_W)])[0]
    cw32 = jnp.pad(conv_w_full, ((0, 0), (0, 32 - KCONV), (0, 0)))

    def row(a, l):
        return a[l].reshape(1, -1)

    saved = []
    early = ["w_in", "w_out", "w_ffn_in"]
    for l in range(DEPTH):
        cur = gathered[l]
        nxt = gathered[l + 1] if l + 1 < DEPTH else None
        sv = {"x": xc}
        bias = jnp.repeat(gm_bs[l].T, GM_W // GM_HEADS, axis=1)
        first = ["w_ffn_in"] if l == 0 else ["w_ffn_out"]
        late = ["w_out", "w_ffn_out"]
        proj, h1t, rid = _norm_mm(xc, row(norm1_g, l), weight(l, "w_in"), F32, f"in_proj{l}", 512,
                             [("ici" if l == 0 else "d2d", [cur[n] for n in first])], (tb["cos2"], tb["sin2"]))
        cur.update(zip(first, rid))
        y_gm, rid = _gm_fwd(proj, row(gm_ln_g, l), row(gm_ln_b, l), gm_ws[l], bias, f"gm_fwd{l}",
                            [("d2d", [cur[n] for n in first])] if l == 0 else ())
        cur.update(zip(first, rid))
        sf, sb = _ret_scan(proj, K_COL, proj, V_COL, tb["zf"], tb["zb"], tb["gcf"], tb["gcb"], f"ret_state{l}")
        a, y_ret, rid = _ret_out(proj, sf, sb, tb, f"ret_out{l}",
                                 [("ici", [cur[n] for n in late])] if l == 0 else ())
        cur.update(zip(late, rid))
        y_cv, hc, rid = _conv_fwd(proj, cw32[l], row(conv_b, l), row(conv_ln_g, l), row(conv_ln_b, l),
                                  f"conv_fwd{l}", [("d2d", [cur[n] for n in late])] if l == 0 else ())
        cur.update(zip(late, rid))
        x_mid = _parts_mm_res([y_gm, y_ret, y_cv], weight(l, "w_out"), xc, f"out_proj{l}")
        ff, h2t, rid = _norm_mm(x_mid, row(norm2_g, l), weight(l, "w_ffn_in"), ACT_DTYPE, f"ffn_in{l}", 512,
                           [("ici", [nxt[n] for n in early])] if nxt else ())
        if nxt:
            nxt.update(zip(early, rid))
        xc, rid = _swiglu_mm_res(ff, weight(l, "w_ffn_out"), x_mid, f"ffn_out{l}",
                                 [("d2d", [nxt[n] for n in early]), ("ici", [nxt["w_ffn_out"]])] if nxt else ())
        if nxt:
            nxt.update(zip(early + ["w_ffn_out"], rid))
        sv.update(bias=bias, proj=proj, h1t=h1t, h2t=h2t, y_gm=y_gm, sf=sf, sb=sb, a=a, y_ret=y_ret, y_cv=y_cv,
                  hc=hc, x_mid=x_mid,
                  ff=ff)
        saved.append(sv)

    dx, d_final_g, lpart = _loss_head(xc, final_g.reshape(1, D), target, "loss_head")

    small_g = {n: [None] * DEPTH for n in SMALL}
    qs = [{} for _ in range(DEPTH)]
    got = [{} for _ in range(DEPTH)]
    ffn_w, mix_w = ["w_ffn_out", "w_ffn_in"], ["w_out", "w_in"]

    def halves(big_g, group):
        return [big_g[n].reshape(N_CHIPS, 2, given[n].shape[1] // 2, given[n].shape[2]) for n in group]

    def pair_sums(l, group, g4, theirs):
        qs[l].update({n: _pair_sum(g, th, core, f"pair_sum_{n}{l}") for n, g, th in zip(group, g4, theirs)})
        return [qs[l][n] for n in group]

    for l in reversed(range(DEPTH)):
        sv = saved[l]
        proj = sv["proj"]
        big_g = {}
        dff = _dx_swiglu(dx, weight(l, "w_ffn_out"), sv["ff"], f"ffn_out_dx{l}")
        big_g["w_ffn_out"] = _dw_swiglu(sv["ff"], dx, f"ffn_out_dw{l}")
        dx_mid, dg2, _ = _dx_norm([dff], weight(l, "w_ffn_in"), sv["x_mid"], row(norm2_g, l), dx,
                                  f"ffn_in_dx{l}", 512)
        big_g["w_ffn_in"] = _dw_norm_cols(sv["h2t"], dff, w_ffn_in.shape[2], f"ffn_in_dw{l}")
        g4 = halves(big_g, ffn_w)
        (dy_gm, da, d_g, dy_cv), theirs = _out_proj_dx(dx_mid, weight(l, "w_out"), sv["a"], proj,
                                                       f"out_proj_dx{l}", [("pairx", g4)])
        q_ffn = pair_sums(l, ffn_w, g4, theirs)
        big_g["w_out"] = _dw_parts([sv["y_gm"], sv["y_ret"], sv["y_cv"]], dx_mid, f"out_proj_dw{l}")
        d_cv, dcw, dcb, dclg, dclb, rid = _conv_bwd(proj, dy_cv, sv["hc"], cw32[l], row(conv_ln_g, l),
                                                    row(conv_ln_b, l), f"conv_bwd{l}", [("scatter", q_ffn[:1])])
        got[l].update(zip(ffn_w[:1], rid))
        gb_, gf_ = _ret_scan(proj, Q_COL, da, 0, tb["xib"], tb["xif"], tb["gcb"], tb["gcf"], f"ret_bwd_state{l}")
        d_qkv, rid = _ret_bwd_main(proj, da, sv["sf"], sv["sb"], gf_, gb_, tb, f"ret_bwd_main{l}",
                                   [("scatter", q_ffn[1:])])
        got[l].update(zip(ffn_w[1:], rid))
        d_gm, dws, dbs, dglg, dglb = _gm_bwd(proj, dy_gm, row(gm_ln_g, l), row(gm_ln_b, l), gm_ws[l],
                                             jnp.swapaxes(gm_ws[l], 1, 2), sv["bias"], f"gm_bwd{l}")
        dparts = [d_gm, d_qkv, d_g, d_cv]
        big_g["w_in"] = _dw_norm_parts(sv["h1t"], dparts, w_in.shape[2], f"in_proj_dw{l}")
        g4 = halves(big_g, mix_w)
        q_mix = pair_sums(l, mix_w, g4, _pair_exchange(g4, f"grad_pair_exchange_mix{l}"))
        dx, dg1, rid = _dx_norm(dparts, weight(l, "w_in"), sv["x"], row(norm1_g, l), dx_mid, f"in_proj_dx{l}", 512,
                                [("scatter", q_mix)])
        got[l].update(zip(mix_w, rid))
        for n, val in (("norm1_g", dg1[0]), ("gm_ln_g", dglg[0]), ("gm_ln_b", dglb[0]), ("gm_ws", dws),
                       ("gm_bs", dbs[:, :GM_HEADS].T), ("conv_w", dcw[:KCONV]), ("conv_b", dcb[0]),
                       ("conv_ln_g", dclg[0]), ("conv_ln_b", dclb[0]), ("norm2_g", dg2[0])):
            small_g[n][l] = val

    small_shapes = [given[n].shape if n != "conv_w" else (DEPTH, KCONV, CV_W) for n in SMALL]
    partials = [d_final_g[0] if n == "final_g" else jnp.stack(small_g[n]) for n in SMALL]
    summed = _unpack(_all_reduce_small(_pack(partials + [lpart]), me, core, "all_reduce_small_grads"),
                     small_shapes + [lpart.shape])
    loss = summed[-1][0, 0]
    reduced = dict(zip(SMALL, summed))
    reduced["conv_w"] = lax.dynamic_slice(reduced["conv_w"], (0, 0, me * cshard), (DEPTH, KCONV, cshard))

    halves = [None] * NBIG
    for l in reversed(range(DEPTH)):
        halves = [_chip_sum(qs[l][n], got[l][n], l, me, core, h, f"chip_sum_{n}{l}") for n, h in zip(names, halves)]
    grads = dict(zip(names, _pair_gather(halves)))
    grads.update(reduced)

    delta, new_m, new_v = {}, {}, {}
    for n, _ in BIG:
        delta[n], new_m[n], new_v[n] = _adamw(given[n], grads[n], given["m_" + n], given["v_" + n], f"adamw_{n}")
    shapes = [given[n].shape for n in SMALL]
    packed = [_pack([src[n] if src is grads else src[p + n] for n in SMALL])
              for src, p in ((given, ""), (grads, ""), (given, "m_"), (given, "v_"))]
    outs = _adamw(*packed, "adamw_small")
    for dst, buf in zip((delta, new_m, new_v), outs):
        dst.update(zip(SMALL, _unpack(buf, shapes)))

    return (loss, dx.reshape(1, t, D), *[grads[n] for n in WEIGHTS], *[delta[n] for n in WEIGHTS],
            *[new_m[n] for n in WEIGHTS], *[new_v[n] for n in WEIGHTS])
```

```python
import functools
import math

import numpy as np
import jax
import jax.numpy as jnp
from jax import lax
from jax.experimental import pallas as pl
from jax.experimental.pallas import tpu as pltpu

F32 = jnp.float32
BF16 = jnp.bfloat16
MXU_DTYPE = BF16
ACT_DTYPE = BF16
S = jax.ShapeDtypeStruct

D = 1024
DEPTH = 2
GM_W = 256
GM_HEADS = 4
RET_W = 512
HEADS = 4
HD = 128
CV_W = 256
KCONV = 31
IN_W = 2 * GM_W + 4 * RET_W + 2 * CV_W
FFN_H = 2816
CH = 128
ROPE_BASE = 10000.0
EPS = 1e-6
N_CHIPS = 4
N_DEV = 8
HALO = 16

ADAM_LR = 0.001
ADAM_B1 = 0.9
ADAM_B2 = 0.999
ADAM_EPS = 1e-08
ADAM_WD = 0.01
ADAM_STEP = 10

VMEM_LIMIT = 52 * 1024 * 1024
MESH = pl.DeviceIdType.MESH


def _cp(*sem, vmem=VMEM_LIMIT):
    return pltpu.CompilerParams(dimension_semantics=tuple(sem), vmem_limit_bytes=vmem)


def _mx(a):
    return a.astype(MXU_DTYPE)


def _dot(a, b):
    return jnp.dot(_mx(a), _mx(b), preferred_element_type=F32)


def _dot_nt(a, b):
    return lax.dot_general(_mx(a), _mx(b), (((1,), (1,)), ((), ())), preferred_element_type=F32)


def _dot_tn(a, b):
    return lax.dot_general(_mx(a), _mx(b), (((0,), (0,)), ((), ())), preferred_element_type=F32)


def _sigmoid(x):
    return 1.0 / (1.0 + jnp.exp(-x))


def _gelu(x):
    return 0.5 * x * (1.0 + lax.erf(x * (1.0 / math.sqrt(2.0))))


def _gelu_grad(x):
    return 0.5 * (1.0 + lax.erf(x * (1.0 / math.sqrt(2.0)))) + x * jnp.exp(-0.5 * x * x) * (1.0 / math.sqrt(2.0 * math.pi))


def _rms_r(x):
    return lax.rsqrt(jnp.mean(x * x, axis=-1, keepdims=True) + EPS)


def _rms_bwd(dh, x, r, g):
    u = dh * g
    dx = r * u - x * (r * r * r) * jnp.mean(u * x, axis=-1, keepdims=True)
    return dx, dh * x * r


def _standardize(a):
    mu = jnp.mean(a, axis=-1, keepdims=True)
    d = a - mu
    r = lax.rsqrt(jnp.mean(d * d, axis=-1, keepdims=True) + EPS)
    return d * r, r


def _standardize_bwd(do, o, r):
    return r * (do - jnp.mean(do, axis=-1, keepdims=True) - o * jnp.mean(do * o, axis=-1, keepdims=True))


def _acc_out(ref, val, first):
    @pl.when(first)
    def _():
        ref[...] = val

    @pl.when(jnp.logical_not(first))
    def _():
        ref[...] += val


def _row_tile(t, pref):
    tm = min(t, pref)
    assert t % tm == 0, (t, tm)
    return tm


def _segments(part_widths, shard_w):
    bounds = {0}
    off = 0
    for w in part_widths:
        off += w
        bounds.add(off)
    total = off
    for j in range(1, total // shard_w + 1):
        bounds.add(j * shard_w)
    bounds = sorted(bounds)
    starts = np.cumsum([0] + list(part_widths))
    segs = []
    for a, b in zip(bounds[:-1], bounds[1:]):
        p = int(np.searchsorted(starts, a, side="right") - 1)
        segs.append((p, a - int(starts[p]), a // shard_w, a % shard_w, b - a))
    return segs


ANY = pl.BlockSpec(memory_space=pl.ANY)


def _mesh_pos():
    x, y, c = lax.axis_index("x"), lax.axis_index("y"), lax.axis_index("c")
    chips = [(1 - x, y), (x, 1 - y), (1 - x, 1 - y)]
    return x, y, c, 2 * x + y, chips, [2 * cx + cy for cx, cy in chips]


def _rider_copies(kind, i_refs, o_refs, send, recv, pos):
    x, y, c, me, chips, cj = pos
    out = []
    for b, (i_ref, o_ref) in enumerate(zip(i_refs, o_refs)):
        for k in range(1 if kind == "pairx" else 3):
            if kind == "ici":
                src, dst, land, dev = i_ref.at[me, c], o_ref.at[me, c], o_ref.at[cj[k], c], (*chips[k], c)
            elif kind == "d2d":
                src, dst, land, dev = i_ref.at[cj[k], c], o_ref.at[cj[k], c], o_ref.at[cj[k], 1 - c], (x, y, 1 - c)
            elif kind == "pairx":
                src, dst, land, dev = i_ref.at[:, 1 - c], o_ref, o_ref, (x, y, 1 - c)
            else:
                src, dst, land, dev = i_ref.at[cj[k]], o_ref.at[k], o_ref.at[k], (*chips[k], c)
            out.append(tuple(pltpu.make_async_remote_copy(
                src_ref=s_, dst_ref=d_, send_sem=send.at[b, k], recv_sem=recv.at[b, k],
                device_id=dev, device_id_type=MESH) for s_, d_ in ((src, dst), (land, land))))
    return out


def _rider_out_shape(kind, a):
    if kind == "scatter":
        return S((3,) + a.shape[1:], a.dtype)
    if kind == "pairx":
        return S(a.shape[:1] + a.shape[2:], a.dtype)
    return S(a.shape, a.dtype)


def _pcall(body, args, riders, *, grid, in_specs, out_specs, out_shape, name, sem, scratch_shapes=()):
    outs = list(out_shape)
    if not riders:
        res = pl.pallas_call(body, grid=grid, in_specs=in_specs, out_specs=out_specs, out_shape=outs, name=name,
                             scratch_shapes=list(scratch_shapes), compiler_params=_cp(*sem))(*args)
        return res, []
    r_in = [a for _, bufs in riders for a in bufs]
    r_out = [_rider_out_shape(kind, a) for kind, bufs in riders for a in bufs]
    n_in, n_out, n_scr, n_r = len(args), len(outs), len(scratch_shapes), len(r_in)
    aliases, idx = {}, 0
    for kind, bufs in riders:
        for _ in bufs:
            if kind in ("ici", "d2d"):
                aliases[n_in + idx] = n_out + idx
            idx += 1
    sems = [pltpu.SemaphoreType.DMA((len(bufs), 3)) for _, bufs in riders for _ in range(2)]

    def wrapped(*refs):
        a, ri = refs[:n_in], refs[n_in:n_in + n_r]
        o, ro = refs[n_in + n_r:n_in + n_r + n_out], refs[n_in + n_r + n_out:n_in + 2 * n_r + n_out]
        scr = refs[n_in + 2 * n_r + n_out:n_in + 2 * n_r + n_out + n_scr]
        sm = refs[n_in + 2 * n_r + n_out + n_scr:]
        pos = _mesh_pos()
        copies, off = [], 0
        for r, (kind, bufs) in enumerate(riders):
            copies += _rider_copies(kind, ri[off:off + len(bufs)], ro[off:off + len(bufs)], sm[2 * r], sm[2 * r + 1], pos)
            off += len(bufs)
        ids = [pl.program_id(d) for d in range(len(grid))]
        first = functools.reduce(jnp.logical_and, [i == 0 for i in ids])
        last = functools.reduce(jnp.logical_and, [i == n - 1 for i, n in zip(ids, grid)])

        @pl.when(first)
        def _():
            for cp, _ in copies:
                cp.start()

        body(*a, *o, *scr)

        @pl.when(last)
        def _():
            for cp, land in copies:
                land.wait_recv()
                cp.wait_send()

    res = pl.pallas_call(
        wrapped, grid=grid, in_specs=list(in_specs) + [ANY] * n_r, out_specs=list(out_specs) + [ANY] * n_r,
        out_shape=outs + r_out, input_output_aliases=aliases, name=name,
        scratch_shapes=list(scratch_shapes) + sems, compiler_params=_cp(*(("arbitrary",) * len(grid))))(*args, *r_in)
    return res[:n_out], res[n_out:]


def _wcol_spec(w):
    return pl.BlockSpec(w.shape, lambda *_: (0, 0, 0))


def _wrow_spec(w):
    return pl.BlockSpec(w.shape, lambda *_: (0, 0))


def _norm_mm(x, g, w, out_dtype, name, tm_pref, riders=(), rope=None):
    t = x.shape[0]
    nc = w.shape[2]
    tm = _row_tile(t, tm_pref)
    extra = list(rope) if rope else []

    def body(x_ref, g_ref, w_ref, *rest):
        o_ref, ht_ref = rest[-2], rest[-1]
        xv = x_ref[...]
        hf = xv * _rms_r(xv) * g_ref[...]
        h = _mx(hf)
        for j in range(N_CHIPS):
            o_ref[:, j * nc:(j + 1) * nc] = jnp.dot(h, w_ref[j], preferred_element_type=F32).astype(o_ref.dtype)
        if rope:
            _rotate_qk(o_ref, rest[0][...], rest[1][...])
        ht_ref[...] = hf.T.astype(ht_ref.dtype)

    (out, ht), rid = _pcall(
        body, [x, g, w] + extra, riders, grid=(t // tm,),
        in_specs=[pl.BlockSpec((tm, D), lambda i: (i, 0)), pl.BlockSpec((1, D), lambda i: (0, 0)), _wcol_spec(w)]
        + [pl.BlockSpec((tm, HD), lambda i: (i, 0)) for _ in extra],
        out_specs=[pl.BlockSpec((tm, N_CHIPS * nc), lambda i: (i, 0)), pl.BlockSpec((D, tm), lambda i: (0, i))],
        out_shape=[S((t, N_CHIPS * nc), out_dtype), S((D, t), MXU_DTYPE)], name=name, sem=("parallel",))
    return out, ht, rid


def _parts_mm_res(parts, w, res, name):
    t = res.shape[0]
    tm = _row_tile(t, 512)
    widths = [p.shape[1] for p in parts]
    offs = np.cumsum([0] + widths)
    n = len(parts)

    def body(*refs):
        p_refs, w_ref, r_ref, o_ref = refs[:n], refs[n], refs[n + 1], refs[n + 2]
        acc = r_ref[...]
        for p in range(n):
            acc = acc + _dot(p_refs[p][...], w_ref[int(offs[p]):int(offs[p + 1]), :])
        o_ref[...] = acc

    return pl.pallas_call(
        body, grid=(t // tm,),
        in_specs=[pl.BlockSpec((tm, wd), lambda i: (i, 0)) for wd in widths]
        + [_wrow_spec(w), pl.BlockSpec((tm, D), lambda i: (i, 0))],
        out_specs=pl.BlockSpec((tm, D), lambda i: (i, 0)),
        out_shape=S((t, D), F32), name=name, compiler_params=_cp("parallel"))(*parts, w, res)


def _swiglu(ff):
    gate = ff[:, :FFN_H].astype(F32)
    up = ff[:, FFN_H:].astype(F32)
    return gate * _sigmoid(gate) * up


def _swiglu_mm_res(ff, w, res, name, riders=()):
    t = res.shape[0]
    tm = _row_tile(t, 512)

    def body(f_ref, w_ref, r_ref, o_ref):
        o_ref[...] = r_ref[...] + _dot(_swiglu(f_ref[...]), w_ref[...])

    (out,), rid = _pcall(
        body, [ff, w, res], riders, grid=(t // tm,),
        in_specs=[pl.BlockSpec((tm, 2 * FFN_H), lambda i: (i, 0)), _wrow_spec(w),
                  pl.BlockSpec((tm, D), lambda i: (i, 0))],
        out_specs=[pl.BlockSpec((tm, D), lambda i: (i, 0))],
        out_shape=[S((t, D), F32)], name=name, sem=("parallel",))
    return out, rid


def _dx_norm(dparts, w, x, g, dres, name, tm_pref, riders=()):
    t = x.shape[0]
    nc = w.shape[2]
    tm = _row_tile(t, tm_pref)
    widths = [p.shape[1] for p in dparts]
    segs = _segments(widths, nc)
    n = len(dparts)

    def body(*refs):
        d_refs = refs[:n]
        w_ref, x_ref, g_ref, r_ref, dx_ref, dg_ref = refs[n:]
        dh = jnp.zeros((tm, D), F32)
        for (p, po, j, jo, wd) in segs:
            dh = dh + _dot_nt(d_refs[p][:, po:po + wd], w_ref[j, :, jo:jo + wd])
        xv = x_ref[...]
        dx, dgrow = _rms_bwd(dh, xv, _rms_r(xv), g_ref[...])
        dx_ref[...] = r_ref[...] + dx
        _acc_out(dg_ref, jnp.sum(dgrow, axis=0, keepdims=True), pl.program_id(0) == 0)

    (dx, dg), rid = _pcall(
        body, [*dparts, w, x, g, dres], riders, grid=(t // tm,),
        in_specs=[pl.BlockSpec((tm, wd), lambda i: (i, 0)) for wd in widths]
        + [_wcol_spec(w), pl.BlockSpec((tm, D), lambda i: (i, 0)),
           pl.BlockSpec((1, D), lambda i: (0, 0)), pl.BlockSpec((tm, D), lambda i: (i, 0))],
        out_specs=[pl.BlockSpec((tm, D), lambda i: (i, 0)), pl.BlockSpec((1, D), lambda i: (0, 0))],
        out_shape=[S((t, D), F32), S((1, D), F32)], name=name, sem=("arbitrary",))
    return dx, dg, rid


def _out_proj_dx(dy, w, a, proj, name, riders=()):
    t = dy.shape[0]
    tm = _row_tile(t, 512)
    wr = HEADS * HD

    def body(dy_ref, w_ref, a_ref, g_ref, dgm_ref, da_ref, dg_ref, dcv_ref):
        dyv = _mx(dy_ref[...])
        dgm_ref[...] = _dot_nt(dyv, w_ref[0:GM_W, :])
        dcv_ref[...] = _dot_nt(dyv, w_ref[GM_W + RET_W:, :])
        for h in range(HEADS):
            cols = slice(h * HD, (h + 1) * HD)
            dyr = _dot_nt(dyv, w_ref[GM_W + h * HD:GM_W + (h + 1) * HD, :])
            o, r = _standardize(a_ref[:, cols])
            gv = g_ref[:, cols]
            s = _sigmoid(gv)
            dg_ref[:, cols] = (dyr * o * (s * (1.0 + gv * (1.0 - s)))).astype(dg_ref.dtype)
            da_ref[:, cols] = _standardize_bwd(dyr * (gv * s), o, r).astype(da_ref.dtype)

    return _pcall(
        body, [dy, w, a, proj], riders, grid=(t // tm,),
        in_specs=[pl.BlockSpec((tm, D), lambda i: (i, 0)), _wrow_spec(w), pl.BlockSpec((tm, wr), lambda i: (i, 0)),
                  pl.BlockSpec((tm, wr), lambda i: (i, GATE_COL))],
        out_specs=[pl.BlockSpec((tm, GM_W), lambda i: (i, 0)), pl.BlockSpec((tm, wr), lambda i: (i, 0)),
                   pl.BlockSpec((tm, wr), lambda i: (i, 0)), pl.BlockSpec((tm, CV_W), lambda i: (i, 0))],
        out_shape=[S((t, GM_W), F32), S((t, wr), ACT_DTYPE), S((t, wr), ACT_DTYPE), S((t, CV_W), F32)],
        name=name, sem=("parallel",))


def _dx_swiglu(dy, w, ff, name):
    t = dy.shape[0]
    tm = _row_tile(t, 512)

    def body(dy_ref, w_ref, f_ref, o_ref):
        dact = _dot_nt(dy_ref[...], w_ref[...])
        gate = f_ref[:, :FFN_H].astype(F32)
        up = f_ref[:, FFN_H:].astype(F32)
        s = _sigmoid(gate)
        gs = gate * s
        o_ref[:, :FFN_H] = ((dact * up) * (s + gs - gs * s)).astype(o_ref.dtype)
        o_ref[:, FFN_H:] = (dact * gs).astype(o_ref.dtype)

    return pl.pallas_call(
        body, grid=(t // tm,),
        in_specs=[pl.BlockSpec((tm, D), lambda i: (i, 0)), _wrow_spec(w),
                  pl.BlockSpec((tm, 2 * FFN_H), lambda i: (i, 0))],
        out_specs=pl.BlockSpec((tm, 2 * FFN_H), lambda i: (i, 0)),
        out_shape=S((t, 2 * FFN_H), ACT_DTYPE), name=name, compiler_params=_cp("parallel"))(dy, w, ff)


def _call_into(body, into, in_specs, args, *, n_prefetch, grid, out_specs, **kw):
    n_in = len(args)
    if into is None:
        gs = pltpu.PrefetchScalarGridSpec(num_scalar_prefetch=n_prefetch, grid=grid, in_specs=in_specs,
                                          out_specs=out_specs)
        return pl.pallas_call(body, grid_spec=gs, **kw)(*args)

    def wrapped(*refs):
        return body(*refs[:n_in], *refs[n_in + 1:])

    gs = pltpu.PrefetchScalarGridSpec(num_scalar_prefetch=n_prefetch, grid=grid,
                                      in_specs=list(in_specs) + [ANY], out_specs=out_specs)
    return pl.pallas_call(wrapped, grid_spec=gs, input_output_aliases={n_in: 0}, **kw)(*args, into)


def _dw_norm_parts(ht, dparts, nc, name):
    t = ht.shape[1]
    tk = _row_tile(t, 1024)
    widths = [p.shape[1] for p in dparts]
    segs = _segments(widths, nc)
    n = len(dparts)
    nk = t // tk

    def body(*refs):
        h_ref, d_refs, o_ref, acc_ref = refs[0], refs[1:1 + n], refs[1 + n], refs[2 + n]
        k = pl.program_id(0)
        h = h_ref[...]

        @pl.when(k == 0)
        def _():
            acc_ref[...] = jnp.zeros_like(acc_ref)

        for (p, po, j, jo, wd) in segs:
            acc_ref[j, :, jo:jo + wd] += _dot(h, d_refs[p][:, po:po + wd])

        @pl.when(k == nk - 1)
        def _():
            o_ref[...] = acc_ref[...].astype(o_ref.dtype)

    return pl.pallas_call(
        body, grid=(nk,),
        in_specs=[pl.BlockSpec((D, tk), lambda k: (0, k))]
        + [pl.BlockSpec((tk, wd), lambda k: (k, 0)) for wd in widths],
        out_specs=pl.BlockSpec((N_CHIPS, D, nc), lambda k: (0, 0, 0)),
        out_shape=S((N_CHIPS, D, nc), MXU_DTYPE), name=name,
        scratch_shapes=[pltpu.VMEM((N_CHIPS, D, nc), F32)], compiler_params=_cp("arbitrary"))(ht, *dparts)


def _dw_norm_cols(ht, dy, nc, name):
    t = ht.shape[1]
    tk = _row_tile(t, 2048)
    nk = t // tk

    def body(h_ref, dy_ref, o_ref, acc_ref):
        k = pl.program_id(1)

        @pl.when(k == 0)
        def _():
            acc_ref[...] = jnp.zeros_like(acc_ref)

        acc_ref[...] += _dot(h_ref[...], dy_ref[...])

        @pl.when(k == nk - 1)
        def _():
            o_ref[...] = acc_ref[...].astype(o_ref.dtype)

    return pl.pallas_call(
        body, grid=(N_CHIPS, nk),
        in_specs=[pl.BlockSpec((D, tk), lambda j, k: (0, k)), pl.BlockSpec((tk, nc), lambda j, k: (k, j))],
        out_specs=pl.BlockSpec((None, D, nc), lambda j, k: (j, 0, 0)),
        out_shape=S((N_CHIPS, D, nc), MXU_DTYPE), name=name,
        scratch_shapes=[pltpu.VMEM((D, nc), F32)], compiler_params=_cp("parallel", "arbitrary"))(ht, dy)


def _dw_parts(parts, dy, name):
    t = dy.shape[0]
    tk = _row_tile(t, 1024)
    widths = [p.shape[1] for p in parts]
    offs = np.cumsum([0] + widths)
    ktot = int(offs[-1])
    n = len(parts)
    nk = t // tk

    def body(*refs):
        p_refs, dy_ref, o_ref, acc_ref = refs[:n], refs[n], refs[n + 1], refs[n + 2]
        k = pl.program_id(0)

        @pl.when(k == 0)
        def _():
            acc_ref[...] = jnp.zeros_like(acc_ref)

        dyv = _mx(dy_ref[...])
        for p in range(n):
            acc_ref[int(offs[p]):int(offs[p + 1]), :] += _dot_tn(p_refs[p][...], dyv)

        @pl.when(k == nk - 1)
        def _():
            o_ref[...] = acc_ref[...].astype(o_ref.dtype)

    return pl.pallas_call(
        body, grid=(nk,),
        in_specs=[pl.BlockSpec((tk, wd), lambda k: (k, 0)) for wd in widths]
        + [pl.BlockSpec((tk, D), lambda k: (k, 0))],
        out_specs=pl.BlockSpec((ktot, D), lambda k: (0, 0)),
        out_shape=S((ktot, D), MXU_DTYPE), name=name,
        scratch_shapes=[pltpu.VMEM((ktot, D), F32)], compiler_params=_cp("arbitrary"))(*parts, dy)


def _dw_swiglu(ff, dy, name):
    t = dy.shape[0]
    tk = _row_tile(t, 512)
    nk = t // tk

    def body(f_ref, dy_ref, o_ref, acc_ref):
        k = pl.program_id(0)

        @pl.when(k == 0)
        def _():
            acc_ref[...] = jnp.zeros_like(acc_ref)

        acc_ref[...] += _dot_tn(_swiglu(f_ref[...]), dy_ref[...])

        @pl.when(k == nk - 1)
        def _():
            o_ref[...] = acc_ref[...].astype(o_ref.dtype)

    return pl.pallas_call(
        body, grid=(nk,),
        in_specs=[pl.BlockSpec((tk, 2 * FFN_H), lambda k: (k, 0)), pl.BlockSpec((tk, D), lambda k: (k, 0))],
        out_specs=pl.BlockSpec((FFN_H, D), lambda k: (0, 0)),
        out_shape=S((FFN_H, D), MXU_DTYPE), name=name,
        scratch_shapes=[pltpu.VMEM((FFN_H, D), F32)], compiler_params=_cp("arbitrary"))(ff, dy)


def _tables(t):
    half = HD // 2
    inv_freq = ROPE_BASE ** (-jnp.arange(half, dtype=F32) / half)
    base = (jnp.arange(t // CH, dtype=F32) * CH)[:, None] * inv_freq[None, :]
    off = jnp.arange(CH, dtype=F32)[:, None] * inv_freq[None, :]
    cb, sb, co, so = jnp.cos(base)[:, None], jnp.sin(base)[:, None], jnp.cos(off)[None], jnp.sin(off)[None]
    cos = (cb * co - sb * so).reshape(t, half)
    sin = (sb * co + cb * so).reshape(t, half)
    tb = {"cos2": jnp.concatenate([cos, cos], axis=1), "sin2": jnp.concatenate([-sin, sin], axis=1)}
    gf = 1.0 - jnp.exp2(-5.0 - jnp.arange(HEADS, dtype=F32))
    lgf = jnp.log(gf)[:, None]
    lgb = jnp.log(gf[::-1])[:, None]
    idx = jnp.arange(CH, dtype=F32)
    diff = idx[:, None] - idx[None, :]
    dfwd = jnp.where(diff >= 0, jnp.exp(lgf[:, :, None] * jnp.where(diff >= 0, diff, 0.0)), 0.0)
    dbwd = jnp.where(diff < 0, jnp.exp(lgb[:, :, None] * jnp.where(diff < 0, -diff, 0.0)), 0.0)
    tb["dm"] = dfwd + dbwd
    tb["dmt"] = jnp.swapaxes(tb["dm"], 1, 2)

    def lanes(a):
        return jnp.repeat(a.T, HD, axis=1)

    tb["xif"] = lanes(jnp.exp(lgf * (idx + 1)))
    tb["zf"] = lanes(jnp.exp(lgf * (CH - 1 - idx)))
    tb["xib"] = lanes(jnp.exp(lgb * (CH - idx)))
    tb["zb"] = lanes(jnp.exp(lgb * idx))
    tb["gcf"] = jnp.repeat(jnp.exp(lgf * CH), HD, axis=0).reshape(1, HEADS * HD)
    tb["gcb"] = jnp.repeat(jnp.exp(lgb * CH), HD, axis=0).reshape(1, HEADS * HD)
    return tb


def _full(shape):
    nd = len(shape)
    return pl.BlockSpec(shape, lambda *_: (0,) * nd)


def _gm_mixed(vn, ws_ref, bias):
    lane = lax.broadcasted_iota(jnp.int32, (CH, 128), 1)
    halves = []
    for hf in range(2):
        vh = _mx(vn[:, hf * 128:(hf + 1) * 128])
        r0 = jnp.dot(_mx(ws_ref[2 * hf]), vh, preferred_element_type=F32)
        r1 = jnp.dot(_mx(ws_ref[2 * hf + 1]), vh, preferred_element_type=F32)
        halves.append(jnp.where(lane < 64, r0, r1))
    return jnp.concatenate(halves, axis=1) + bias


def _gm_fwd(proj, ln_g, ln_b, ws, bias, name, riders=()):
    t = proj.shape[0]
    tm = _row_tile(t, 512)

    def body(pu_ref, pv_ref, g_ref, b_ref, ws_ref, bias_ref, o_ref):
        for c in range(tm // CH):
            rows = slice(c * CH, (c + 1) * CH)
            u = _gelu(pu_ref[rows, :])
            o, _ = _standardize(_gelu(pv_ref[rows, :]))
            vn = o * g_ref[...] + b_ref[...]
            o_ref[rows, :] = (u * _gm_mixed(vn, ws_ref, bias_ref[...])).astype(o_ref.dtype)

    (out,), rid = _pcall(
        body, [proj, proj, ln_g, ln_b, ws, bias], riders, grid=(t // tm,),
        in_specs=[pl.BlockSpec((tm, GM_W), lambda i: (i, 0)), pl.BlockSpec((tm, GM_W), lambda i: (i, 1)),
                  _full((1, GM_W)), _full((1, GM_W)), _full((GM_HEADS, CH, CH)), _full((CH, GM_W))],
        out_specs=[pl.BlockSpec((tm, GM_W), lambda i: (i, 0))],
        out_shape=[S((t, GM_W), ACT_DTYPE)], name=name, sem=("parallel",))
    return out, rid


def _gm_bwd(proj, dy, ln_g, ln_b, ws, wst, bias, name):
    t = proj.shape[0]
    tm = _row_tile(t, 512)
    nb = t // tm

    def body(pu_ref, pv_ref, dy_ref, g_ref, b_ref, ws_ref, wst_ref, bias_ref,
             d_ref, dws_ref, dbs_ref, dg_ref, db_ref, dbias_ref):
        first = pl.program_id(0) == 0
        lane = lax.broadcasted_iota(jnp.int32, (CH, 128), 1)
        dws = [jnp.zeros((CH, CH), F32) for _ in range(GM_HEADS)]
        dbias = jnp.zeros((CH, GM_W), F32)
        dg = jnp.zeros((1, GM_W), F32)
        db = jnp.zeros((1, GM_W), F32)
        for c in range(tm // CH):
            rows = slice(c * CH, (c + 1) * CH)
            pu = pu_ref[rows, :]
            pv = pv_ref[rows, :]
            u = _gelu(pu)
            o, r = _standardize(_gelu(pv))
            vn = o * g_ref[...] + b_ref[...]
            mixed = _gm_mixed(vn, ws_ref, bias_ref[...])
            dyv = dy_ref[rows, :]
            d_ref[rows, :GM_W] = (dyv * mixed * _gelu_grad(pu)).astype(d_ref.dtype)
            dmixed = dyv * u
            dbias = dbias + dmixed
            dvn_halves = []
            for hf in range(2):
                dm = dmixed[:, hf * 128:(hf + 1) * 128]
                vh = vn[:, hf * 128:(hf + 1) * 128]
                dm0 = jnp.where(lane < 64, dm, 0.0)
                dm1 = dm - dm0
                dws[2 * hf] = dws[2 * hf] + _dot_nt(dm0, vh)
                dws[2 * hf + 1] = dws[2 * hf + 1] + _dot_nt(dm1, vh)
                t0 = _dot(wst_ref[2 * hf], dm)
                t1 = _dot(wst_ref[2 * hf + 1], dm)
                dvn_halves.append(jnp.where(lane < 64, t0, t1))
            dvn = jnp.concatenate(dvn_halves, axis=1)
            dg = dg + jnp.sum(dvn * o, axis=0, keepdims=True)
            db = db + jnp.sum(dvn, axis=0, keepdims=True)
            dv = _standardize_bwd(dvn * g_ref[...], o, r)
            d_ref[rows, GM_W:] = (dv * _gelu_grad(pv)).astype(d_ref.dtype)
        for h in range(GM_HEADS):
            _acc_out(dws_ref.at[h], dws[h], first)
        _acc_out(dbias_ref, dbias, first)
        _acc_out(dg_ref, dg, first)
        _acc_out(db_ref, db, first)

        @pl.when(pl.program_id(0) == nb - 1)
        def _():
            tot = dbias_ref[...]
            head = lax.broadcasted_iota(jnp.int32, (CH, GM_W), 1) // (GM_W // GM_HEADS)
            out = jnp.zeros((CH, 128), F32)
            for h in range(GM_HEADS):
                s = jnp.sum(jnp.where(head == h, tot, 0.0), axis=1, keepdims=True)
                out = jnp.where(lane == h, s, out)
            dbs_ref[...] = out

    return pl.pallas_call(
        body, grid=(nb,),
        in_specs=[pl.BlockSpec((tm, GM_W), lambda i: (i, 0)), pl.BlockSpec((tm, GM_W), lambda i: (i, 1)),
                  pl.BlockSpec((tm, GM_W), lambda i: (i, 0)),
                  _full((1, GM_W)), _full((1, GM_W)), _full((GM_HEADS, CH, CH)), _full((GM_HEADS, CH, CH)),
                  _full((CH, GM_W))],
        out_specs=[pl.BlockSpec((tm, 2 * GM_W), lambda i: (i, 0)), _full((GM_HEADS, CH, CH)), _full((CH, 128)),
                   _full((1, GM_W)), _full((1, GM_W))],
        out_shape=[S((t, 2 * GM_W), ACT_DTYPE), S((GM_HEADS, CH, CH), F32), S((CH, 128), F32),
                   S((1, GM_W), F32), S((1, GM_W), F32)],
        scratch_shapes=[pltpu.VMEM((CH, GM_W), F32)],
        name=name, compiler_params=_cp("arbitrary"))(proj, proj, dy, ln_g, ln_b, ws, wst, bias)


def _rot(x, cos2, sin2):
    return x * cos2 + pltpu.roll(x, HD // 2, 1) * sin2


def _rot_bwd(dx, cos2, sin2):
    return dx * cos2 + pltpu.roll(dx * sin2, HD // 2, 1)


Q_COL, K_COL, V_COL, GATE_COL = 1, 2, 3, 4


def _rotate_qk(o_ref, cos2, sin2):
    for col, scale in ((Q_COL, 1.0), (K_COL, HD ** -0.5)):
        for h in range(HEADS):
            cols = slice(col * RET_W + h * HD, col * RET_W + (h + 1) * HD)
            o_ref[:, cols] = _rot(o_ref[:, cols], cos2, sin2) * scale


def _ret_scan(lhs, lhs_col, rhs, rhs_col, lp, ls, gp, gs, name):
    t = lhs.shape[0]
    n = t // CH
    r = 4 if n % 4 == 0 else 1
    ns = n // r

    def body(lp_ref, ls_ref, gp_ref, gs_ref, l1_ref, r1_ref, l2_ref, r2_ref, pre_ref, suf_ref, sp_ref, ss_ref):
        @pl.when(pl.program_id(0) == 0)
        def _():
            sp_ref[...] = jnp.zeros_like(sp_ref)
            ss_ref[...] = jnp.zeros_like(ss_ref)

        def kv(l_ref, r_ref, scale, rows):
            lv = l_ref[rows, :] * scale
            rv = r_ref[rows, :]
            return jnp.concatenate([_dot_tn(lv[:, h * HD:(h + 1) * HD], rv[:, h * HD:(h + 1) * HD])
                                    for h in range(HEADS)], axis=1)

        for j in range(r):
            pre_ref[j] = sp_ref[...].astype(pre_ref.dtype)
            sp_ref[...] = sp_ref[...] * gp_ref[...] + kv(l1_ref, r1_ref, lp_ref[...], slice(j * CH, (j + 1) * CH))
        for j in reversed(range(r)):
            suf_ref[j] = ss_ref[...].astype(suf_ref.dtype)
            ss_ref[...] = ss_ref[...] * gs_ref[...] + kv(l2_ref, r2_ref, ls_ref[...], slice(j * CH, (j + 1) * CH))

    w = HEADS * HD
    return pl.pallas_call(
        body, grid=(ns,),
        in_specs=[_full((CH, w)), _full((CH, w)), _full((1, w)), _full((1, w)),
                  pl.BlockSpec((r * CH, w), lambda s: (s, lhs_col)), pl.BlockSpec((r * CH, w), lambda s: (s, rhs_col)),
                  pl.BlockSpec((r * CH, w), lambda s: (ns - 1 - s, lhs_col)),
                  pl.BlockSpec((r * CH, w), lambda s: (ns - 1 - s, rhs_col))],
        out_specs=[pl.BlockSpec((r, HD, w), lambda s: (s, 0, 0)), pl.BlockSpec((r, HD, w), lambda s: (ns - 1 - s, 0, 0))],
        out_shape=[S((n, HD, w), MXU_DTYPE)] * 2, name=name,
        scratch_shapes=[pltpu.VMEM((HD, w), F32), pltpu.VMEM((HD, w), F32)],
        compiler_params=_cp("arbitrary"))(lp, ls, gp, gs, lhs, rhs, lhs, rhs)


def _ret_out(proj, sf, sb, tb, name, riders=()):
    t = proj.shape[0]
    r = 4 if (t // CH) % 4 == 0 else 1
    tm = r * CH
    w = HEADS * HD

    def body(rq_ref, rk_ref, v_ref, g_ref, sf_ref, sb_ref, dm_ref, xif_ref, xib_ref, a_ref, y_ref):
        for c in range(r):
            rows = slice(c * CH, (c + 1) * CH)
            for h in range(HEADS):
                cols = slice(h * HD, (h + 1) * HD)
                q = rq_ref[rows, cols]
                p = _dot_nt(q, rk_ref[rows, cols]) * dm_ref[h]
                a = (_dot(p, v_ref[rows, cols]) + _dot(q * xif_ref[:, cols], sf_ref[c, :, cols])
                     + _dot(q * xib_ref[:, cols], sb_ref[c, :, cols]))
                a_ref[rows, cols] = a
                o, _ = _standardize(a)
                gv = g_ref[rows, cols]
                y_ref[rows, cols] = (o * (gv * _sigmoid(gv))).astype(y_ref.dtype)

    (a, y), rid = _pcall(
        body, [proj, proj, proj, proj, sf, sb, tb["dm"], tb["xif"], tb["xib"]], riders, grid=(t // tm,),
        in_specs=[pl.BlockSpec((tm, w), lambda i: (i, Q_COL)), pl.BlockSpec((tm, w), lambda i: (i, K_COL)),
                  pl.BlockSpec((tm, w), lambda i: (i, V_COL)), pl.BlockSpec((tm, w), lambda i: (i, GATE_COL)),
                  pl.BlockSpec((r, HD, w), lambda i: (i, 0, 0)), pl.BlockSpec((r, HD, w), lambda i: (i, 0, 0)),
                  _full((HEADS, CH, CH)), _full((CH, w)), _full((CH, w))],
        out_specs=[pl.BlockSpec((tm, w), lambda i: (i, 0))] * 2,
        out_shape=[S((t, w), F32), S((t, w), ACT_DTYPE)], name=name, sem=("parallel",))
    return a, y, rid


def _ret_bwd_main(proj, da, sf, sb, gf, gb, tb, name, riders=()):
    t = proj.shape[0]
    r = 4 if (t // CH) % 4 == 0 else 1
    tm = r * CH
    w = HEADS * HD
    scale = HD ** -0.5

    def body(rq_ref, rk_ref, v_ref, da_ref, sf_ref, sb_ref, gf_ref, gb_ref, dm_ref, dmt_ref,
             xif_ref, xib_ref, zf_ref, zb_ref, c_ref, s_ref, o_ref):
        for c in range(r):
            rows = slice(c * CH, (c + 1) * CH)
            cos2, sin2 = c_ref[rows, :], s_ref[rows, :]
            for h in range(HEADS):
                cols = slice(h * HD, (h + 1) * HD)
                q, k, v, dav = rq_ref[rows, cols], rk_ref[rows, cols], v_ref[rows, cols], da_ref[rows, cols]
                qm, km, vm, dam = _mx(q), _mx(k), _mx(v), _mx(dav)
                dm, dmt = dm_ref[h], dmt_ref[h]
                pt = _dot_nt(km, qm) * dmt
                dp = _dot_nt(dam, vm) * dm
                dpt = _dot_nt(vm, dam) * dmt
                sfh, sbh, gfh, gbh = sf_ref[c, :, cols], sb_ref[c, :, cols], gf_ref[c, :, cols], gb_ref[c, :, cols]
                zf, zb = zf_ref[:, cols], zb_ref[:, cols]
                dv = _dot(pt, dam) + zf * _dot(km, gfh) + zb * _dot(km, gbh)
                drq = _dot(dp, km) + xif_ref[:, cols] * _dot_nt(dam, sfh) + xib_ref[:, cols] * _dot_nt(dam, sbh)
                drk = _dot(dpt, qm) + _dot_nt(zf * v, gfh) + _dot_nt(zb * v, gbh)
                o_ref[rows, h * HD:(h + 1) * HD] = _rot_bwd(drq, cos2, sin2).astype(o_ref.dtype)
                o_ref[rows, w + h * HD:w + (h + 1) * HD] = (_rot_bwd(drk, cos2, sin2) * scale).astype(o_ref.dtype)
                o_ref[rows, 2 * w + h * HD:2 * w + (h + 1) * HD] = dv.astype(o_ref.dtype)

    st = pl.BlockSpec((r, HD, w), lambda i: (i, 0, 0))
    (out,), rid = _pcall(
        body, [proj, proj, proj, da, sf, sb, gf, gb, tb["dm"], tb["dmt"], tb["xif"], tb["xib"], tb["zf"], tb["zb"],
               tb["cos2"], tb["sin2"]], riders, grid=(t // tm,),
        in_specs=[pl.BlockSpec((tm, w), lambda i: (i, Q_COL)), pl.BlockSpec((tm, w), lambda i: (i, K_COL)),
                  pl.BlockSpec((tm, w), lambda i: (i, V_COL)), pl.BlockSpec((tm, w), lambda i: (i, 0)),
                  st, st, st, st, _full((HEADS, CH, CH)), _full((HEADS, CH, CH)),
                  _full((CH, w)), _full((CH, w)), _full((CH, w)), _full((CH, w)),
                  pl.BlockSpec((tm, HD), lambda i: (i, 0)), pl.BlockSpec((tm, HD), lambda i: (i, 0))],
        out_specs=[pl.BlockSpec((tm, 3 * w), lambda i: (i, 0))],
        out_shape=[S((t, 3 * w), ACT_DTYPE)], name=name, sem=("parallel",))
    return out, rid


CONV_TM = 256
CONV_SUB = 64
A_COL = (2 * GM_W + 4 * RET_W) // CV_W
G_COL = A_COL + 1


def _halo_specs(t, tm, col):
    nb16 = t // HALO
    per = tm // HALO
    return [pl.BlockSpec((tm, CV_W), lambda i: (i, col)),
            pl.BlockSpec((HALO, CV_W), lambda i: (jnp.maximum(i * per - 1, 0), col)),
            pl.BlockSpec((HALO, CV_W), lambda i: (jnp.minimum((i + 1) * per, nb16 - 1), col))]


def _fill_padded(dst_ref, prev, main, nxt, tm, i, nb):
    dst_ref[0:HALO, :] = jnp.where(i > 0, prev, 0.0)
    dst_ref[HALO:HALO + tm, :] = main
    dst_ref[HALO + tm:2 * HALO + tm, :] = jnp.where(i < nb - 1, nxt, 0.0)


SUBLANES = 8


def _fill_shifted(sh_ref, src_ref, tm):
    n = tm + 2 * HALO - SUBLANES
    for b in range(SUBLANES):
        sh_ref[b, 0:n, :] = src_ref[pl.ds(b, n), :]


def _tap(sh_ref, off, rows):
    return sh_ref[off % SUBLANES, pl.ds(off - off % SUBLANES, rows), :]


def _conv_fwd(proj, cw, cb, ln_g, ln_b, name, riders=()):
    t = proj.shape[0]
    tm = _row_tile(t, CONV_TM)
    nb = t // tm

    def body(a_ref, ap_ref, an_ref, g_ref, gp_ref, gn_ref, w_ref, b_ref, lg_ref, lb_ref, y_ref, hc_ref,
             hp_ref, sh_ref):
        i = pl.program_id(0)
        _fill_padded(hp_ref, ap_ref[...] * _sigmoid(gp_ref[...]), a_ref[...] * _sigmoid(g_ref[...]),
                     an_ref[...] * _sigmoid(gn_ref[...]), tm, i, nb)
        _fill_shifted(sh_ref, hp_ref, tm)
        for sb in range(tm // CONV_SUB):
            acc = jnp.zeros((CONV_SUB, CV_W), F32) + b_ref[...]
            for k in range(KCONV):
                acc = acc + w_ref[k:k + 1, :] * _tap(sh_ref, sb * CONV_SUB + k + 1, CONV_SUB)
            rows = slice(sb * CONV_SUB, (sb + 1) * CONV_SUB)
            hc_ref[rows, :] = acc
            o, _ = _standardize(acc)
            z = o * lg_ref[...] + lb_ref[...]
            y_ref[rows, :] = (z * _sigmoid(z)).astype(y_ref.dtype)

    (y, hc), rid = _pcall(
        body, [proj, proj, proj, proj, proj, proj, cw, cb, ln_g, ln_b], riders, grid=(nb,),
        in_specs=_halo_specs(t, tm, A_COL) + _halo_specs(t, tm, G_COL)
        + [_full((32, CV_W)), _full((1, CV_W)), _full((1, CV_W)), _full((1, CV_W))],
        out_specs=[pl.BlockSpec((tm, CV_W), lambda i: (i, 0))] * 2,
        out_shape=[S((t, CV_W), ACT_DTYPE), S((t, CV_W), F32)], name=name, sem=("parallel",),
        scratch_shapes=[pltpu.VMEM((tm + 2 * HALO, CV_W), F32), pltpu.VMEM((SUBLANES, tm + 2 * HALO, CV_W), F32)])
    return y, hc, rid


def _conv_bwd(proj, dy, hc, cw, ln_g, ln_b, name, riders=()):
    t = proj.shape[0]
    tm = _row_tile(t, CONV_TM)
    nb = t // tm

    def body(a_ref, ap_ref, an_ref, g_ref, gp_ref, gn_ref, dy_ref, dyp_ref, dyn_ref, hc_ref, hcp_ref, hcn_ref,
             w_ref, lg_ref, lb_ref, d_ref, dw_ref, dcb_ref, dlg_ref, dlb_ref, hp_ref, dhp_ref, dwacc_ref,
             sh_ref, dsh_ref):
        i = pl.program_id(0)
        first = i == 0

        def dhc_of(dyv, hcv):
            o, r = _standardize(hcv)
            z = o * lg_ref[...] + lb_ref[...]
            s = _sigmoid(z)
            dz = dyv * (s * (1.0 + z * (1.0 - s)))
            return _standardize_bwd(dz * lg_ref[...], o, r), dz, o

        dhc, dz, o = dhc_of(dy_ref[...], hc_ref[...])
        _acc_out(dlg_ref, jnp.sum(dz * o, axis=0, keepdims=True), first)
        _acc_out(dlb_ref, jnp.sum(dz, axis=0, keepdims=True), first)
        _acc_out(dcb_ref, jnp.sum(dhc, axis=0, keepdims=True), first)
        _fill_padded(dhp_ref, dhc_of(dyp_ref[...], hcp_ref[...])[0], dhc, dhc_of(dyn_ref[...], hcn_ref[...])[0],
                     tm, i, nb)
        _fill_padded(hp_ref, ap_ref[...] * _sigmoid(gp_ref[...]), a_ref[...] * _sigmoid(g_ref[...]),
                     an_ref[...] * _sigmoid(gn_ref[...]), tm, i, nb)

        _fill_shifted(sh_ref, hp_ref, tm)
        _fill_shifted(dsh_ref, dhp_ref, tm)

        @pl.when(first)
        def _():
            dwacc_ref[...] = jnp.zeros_like(dwacc_ref)

        for sb in range(tm // CONV_SUB):
            base = sb * CONV_SUB
            dmain = dhp_ref[pl.ds(HALO + base, CONV_SUB), :]
            dh = jnp.zeros((CONV_SUB, CV_W), F32)
            for k in range(KCONV):
                dh = dh + w_ref[k:k + 1, :] * _tap(dsh_ref, base + 2 * HALO - 1 - k, CONV_SUB)
                prod = dmain * _tap(sh_ref, base + k + 1, CONV_SUB)
                dwacc_ref[k * 8:(k + 1) * 8, :] += jnp.sum(prod.reshape(CONV_SUB // 8, 8, CV_W), axis=0)
            rows = slice(base, base + CONV_SUB)
            s = _sigmoid(g_ref[rows, :])
            d_ref[rows, :CV_W] = (dh * s).astype(d_ref.dtype)
            d_ref[rows, CV_W:] = (dh * a_ref[rows, :] * (s * (1.0 - s))).astype(d_ref.dtype)

        @pl.when(i == nb - 1)
        def _():
            for k in range(KCONV):
                dw_ref[k:k + 1, :] = jnp.sum(dwacc_ref[k * 8:(k + 1) * 8, :], axis=0, keepdims=True)
            dw_ref[KCONV:32, :] = jnp.zeros((32 - KCONV, CV_W), F32)

    hs = [pl.BlockSpec((tm, CV_W), lambda i: (i, 0)),
          pl.BlockSpec((HALO, CV_W), lambda i: (jnp.maximum(i * (tm // HALO) - 1, 0), 0)),
          pl.BlockSpec((HALO, CV_W), lambda i: (jnp.minimum((i + 1) * (tm // HALO), t // HALO - 1), 0))]
    outs, rid = _pcall(
        body, [proj, proj, proj, proj, proj, proj, dy, dy, dy, hc, hc, hc, cw, ln_g, ln_b], riders, grid=(nb,),
        in_specs=_halo_specs(t, tm, A_COL) + _halo_specs(t, tm, G_COL) + hs + hs
        + [_full((32, CV_W)), _full((1, CV_W)), _full((1, CV_W))],
        out_specs=[pl.BlockSpec((tm, 2 * CV_W), lambda i: (i, 0)), _full((32, CV_W)), _full((1, CV_W)),
                   _full((1, CV_W)), _full((1, CV_W))],
        out_shape=[S((t, 2 * CV_W), ACT_DTYPE), S((32, CV_W), F32), S((1, CV_W), F32), S((1, CV_W), F32),
                   S((1, CV_W), F32)],
        name=name, sem=("arbitrary",),
        scratch_shapes=[pltpu.VMEM((tm + 2 * HALO, CV_W), F32), pltpu.VMEM((tm + 2 * HALO, CV_W), F32),
                        pltpu.VMEM((32 * 8, CV_W), F32), pltpu.VMEM((SUBLANES, tm + 2 * HALO, CV_W), F32),
                        pltpu.VMEM((SUBLANES, tm + 2 * HALO, CV_W), F32)])
    return (*outs, rid)


def _loss_head(x, g, target, name):
    t = x.shape[0]
    tm = _row_tile(t, 512)

    def body(x_ref, g_ref, t_ref, dx_ref, dg_ref, l_ref):
        first = pl.program_id(0) == 0
        xv = x_ref[...]
        r = _rms_r(xv)
        e = xv * r * g_ref[...] - t_ref[...]
        dx, dgrow = _rms_bwd(e * (1.0 / D), xv, r, g_ref[...])
        dx_ref[...] = dx
        _acc_out(dg_ref, jnp.sum(dgrow, axis=0, keepdims=True), first)
        part = 0.5 * jnp.sum(jnp.mean(e * e, axis=-1, keepdims=True), axis=0, keepdims=True)
        _acc_out(l_ref, jnp.broadcast_to(part, (8, 128)), first)

    return pl.pallas_call(
        body, grid=(t // tm,),
        in_specs=[pl.BlockSpec((tm, D), lambda i: (i, 0)), _full((1, D)), pl.BlockSpec((tm, D), lambda i: (i, 0))],
        out_specs=[pl.BlockSpec((tm, D), lambda i: (i, 0)), _full((1, D)), _full((8, 128))],
        out_shape=[S((t, D), F32), S((1, D), F32), S((8, 128), F32)], name=name,
        compiler_params=_cp("arbitrary"))(x, g, target)


def _as2d(a):
    return a.reshape(-1, a.shape[-1])


def _ew_tile(rows, cols, n_arrays):
    budget = VMEM_LIMIT // 2
    tr = rows
    while tr * cols * 4 * n_arrays * 2 > budget and tr % 16 == 0:
        tr //= 2
    assert rows % tr == 0
    return tr


def _adamw(w, g, m, v, name):
    shape = w.shape
    w2, g2, m2, v2 = _as2d(w), _as2d(g), _as2d(m), _as2d(v)
    rows, cols = w2.shape
    tr = _ew_tile(rows, cols, 7)

    def body(w_ref, g_ref, m_ref, v_ref, d_ref, nm_ref, nv_ref):
        gv = g_ref[...]
        nm = ADAM_B1 * m_ref[...] + (1.0 - ADAM_B1) * gv
        nv = ADAM_B2 * v_ref[...] + (1.0 - ADAM_B2) * (gv * gv)
        m_hat = nm / (1.0 - ADAM_B1 ** ADAM_STEP)
        v_hat = nv / (1.0 - ADAM_B2 ** ADAM_STEP)
        d_ref[...] = -ADAM_LR * (m_hat / (jnp.sqrt(v_hat) + ADAM_EPS) + ADAM_WD * w_ref[...])
        nm_ref[...] = nm
        nv_ref[...] = nv

    spec = pl.BlockSpec((tr, cols), lambda i: (i, 0))
    outs = pl.pallas_call(body, grid=(rows // tr,), in_specs=[spec] * 4, out_specs=[spec] * 3,
                          out_shape=[S((rows, cols), F32)] * 3, name=name,
                          compiler_params=_cp("parallel"))(w2, g2, m2, v2)
    return tuple(o.reshape(shape) for o in outs)


BIG = (("w_in", "col"), ("w_out", "row"), ("w_ffn_in", "col"), ("w_ffn_out", "row"))
NBIG = len(BIG)


def _cast_to_gathered(w, l, me, name):
    _, r_, c_ = w.shape
    tr = _ew_tile(r_, c_, 2)

    def body(me_ref, w_ref, o_ref):
        o_ref[...] = w_ref[...].astype(o_ref.dtype)

    gs = pltpu.PrefetchScalarGridSpec(
        num_scalar_prefetch=1, grid=(r_ // tr,),
        in_specs=[pl.BlockSpec((None, tr, c_), lambda i, s: (l, i, 0))],
        out_specs=pl.BlockSpec((None, tr, c_), lambda i, s: (s[0], i, 0)))
    out = pl.pallas_call(body, grid_spec=gs, out_shape=S((N_CHIPS, r_, c_), MXU_DTYPE), name=name,
                         compiler_params=_cp("parallel"))(me.reshape(1), w)
    return out.reshape(N_CHIPS, 2, r_ // 2, c_)


def _all_gather(bufs, name, per_core=False):
    n = len(bufs)

    def body(*refs):
        i_refs, o_refs = refs[:n], refs[n:2 * n]
        isend, irecv, dsend, drecv, osend, orecv = refs[2 * n:]
        pos = _mesh_pos()
        x, y, c, me, _, _ = pos
        ici = _rider_copies("ici", i_refs, o_refs, isend, irecv, pos)
        d2d = _rider_copies("d2d", o_refs, o_refs, dsend, drecv, pos)
        own = []
        if per_core:
            for b in range(n):
                own.append(tuple(pltpu.make_async_remote_copy(
                    src_ref=s_, dst_ref=d_, send_sem=osend.at[b], recv_sem=orecv.at[b],
                    device_id=(x, y, 1 - c), device_id_type=MESH)
                    for s_, d_ in ((i_refs[b].at[me, c], o_refs[b].at[me, c]),
                                   (o_refs[b].at[me, 1 - c], o_refs[b].at[me, 1 - c]))))
        for cp, _ in ici + own:
            cp.start()
        for (_, land), (fwd, _) in zip(ici, d2d):
            land.wait_recv()
            fwd.start()
        for _, land in d2d + own:
            land.wait_recv()
        for cp, _ in ici + d2d + own:
            cp.wait_send()

    return pl.pallas_call(
        body, in_specs=[ANY] * n, out_specs=[ANY] * n, out_shape=[S(a.shape, a.dtype) for a in bufs],
        input_output_aliases={w: w for w in range(n)}, name=name,
        scratch_shapes=[pltpu.SemaphoreType.DMA((n, 3))] * 4 + [pltpu.SemaphoreType.DMA((n,))] * 2)(*bufs)


def _pair_exchange(grads, name):
    n = len(grads)

    def body(*refs):
        g_refs, theirs = refs[:n], refs[n:2 * n]
        send, recv = refs[2 * n:]
        x, y, c, *_ = _mesh_pos()
        cps = []
        for w in range(n):
            cp = pltpu.make_async_remote_copy(
                src_ref=g_refs[w].at[:, 1 - c], dst_ref=theirs[w], send_sem=send.at[w], recv_sem=recv.at[w],
                device_id=(x, y, 1 - c), device_id_type=MESH)
            cp.start()
            cps.append(cp)
        for cp in cps:
            cp.wait()

    return pl.pallas_call(
        body, in_specs=[ANY] * n, out_specs=[ANY] * n,
        out_shape=[S(a.shape[:1] + a.shape[2:], a.dtype) for a in grads], name=name,
        scratch_shapes=[pltpu.SemaphoreType.DMA((n,))] * 2)(*grads)


def _pair_sum(g, theirs, core, name):
    _, _, rh, c_ = g.shape
    tr = _ew_tile(rh, c_, 2)

    def body(s_ref, g_ref, t_ref, o_ref):
        o_ref[...] = (g_ref[...].astype(F32) + t_ref[...].astype(F32)).astype(o_ref.dtype)

    blk = pl.BlockSpec((None, tr, c_), lambda j, i, s: (j, i, 0))
    gs = pltpu.PrefetchScalarGridSpec(
        num_scalar_prefetch=1, grid=(N_CHIPS, rh // tr),
        in_specs=[pl.BlockSpec((None, None, tr, c_), lambda j, i, s: (j, s[0], i, 0)), blk], out_specs=blk)
    return pl.pallas_call(body, grid_spec=gs, out_shape=S(theirs.shape, theirs.dtype), name=name,
                          compiler_params=_cp("parallel", "parallel"))(core.reshape(1), g, theirs)


def _chip_sum(q, got, l, me, core, into, name):
    _, rh, c_ = got.shape
    tr = _ew_tile(rh, c_, 4)

    def body(s_ref, q_ref, g0_ref, g1_ref, g2_ref, o_ref):
        acc = q_ref[...].astype(F32)
        for r in (g0_ref, g1_ref, g2_ref):
            acc = acc + r[...].astype(F32)
        o_ref[...] = acc

    in_specs = [pl.BlockSpec((None, tr, c_), lambda i, s: (s[0], i, 0))] + [
        pl.BlockSpec((None, tr, c_), functools.partial(lambda k, i, s: (k, i, 0), k)) for k in range(3)]
    return _call_into(
        body, into, in_specs, [jnp.stack([me, core]), q, got, got, got], n_prefetch=1, grid=(rh // tr,),
        out_specs=pl.BlockSpec((None, None, tr, c_), lambda i, s: (l, s[1], i, 0)),
        out_shape=S((DEPTH, 2, rh, c_), F32), name=name, compiler_params=_cp("parallel"))


def _pair_gather(gs4):
    def body(*refs):
        i_refs, o_refs = refs[:NBIG], refs[NBIG:2 * NBIG]
        send, recv = refs[2 * NBIG:]
        x, y, c, *_ = _mesh_pos()
        cps = []
        for w in range(NBIG):
            cp = pltpu.make_async_remote_copy(
                src_ref=i_refs[w].at[:, c], dst_ref=o_refs[w].at[:, c], send_sem=send.at[w], recv_sem=recv.at[w],
                device_id=(x, y, 1 - c), device_id_type=MESH)
            cp.start()
            cps.append(cp)
        for cp in cps:
            cp.wait()

    outs = pl.pallas_call(
        body, in_specs=[ANY] * NBIG, out_specs=[ANY] * NBIG, out_shape=[S(a.shape, a.dtype) for a in gs4],
        input_output_aliases={w: w for w in range(NBIG)}, name="grad_pair_gather",
        scratch_shapes=[pltpu.SemaphoreType.DMA((NBIG,))] * 2)(*gs4)
    return [o.reshape(o.shape[0], 2 * o.shape[2], o.shape[3]) for o in outs]


def _all_reduce_small(p, me, core, name):
    rows = p.shape[0]

    def place(s_ref, p_ref, o_ref):
        o_ref[...] = p_ref[...]

    gs = pltpu.PrefetchScalarGridSpec(
        num_scalar_prefetch=1, grid=(1,), in_specs=[pl.BlockSpec((rows, 128), lambda i, s: (0, 0))],
        out_specs=pl.BlockSpec((None, None, rows, 128), lambda i, s: (s[0], s[1], 0, 0)))
    mine = pl.pallas_call(place, grid_spec=gs, out_shape=S((N_CHIPS, 2, rows, 128), F32), name=name + "_place",
                          compiler_params=_cp("arbitrary"))(jnp.stack([me, core]), p)
    parts = _all_gather([mine], name + "_gather", per_core=True)[0]

    def total(g_ref, o_ref):
        acc = g_ref[0, 0]
        for j in range(N_CHIPS):
            for c in range(2):
                if (j, c) != (0, 0):
                    acc = acc + g_ref[j, c]
        o_ref[...] = acc

    vm = pl.BlockSpec(memory_space=pltpu.VMEM)
    return pl.pallas_call(total, in_specs=[vm], out_specs=vm, out_shape=S((rows, 128), F32), name=name + "_sum",
                          compiler_params=pltpu.CompilerParams(vmem_limit_bytes=VMEM_LIMIT))(parts)


PACK_UNIT = 8 * 128


def _pack(arrs):
    parts = []
    for a in arrs:
        flat = a.reshape(-1)
        pad = (-flat.shape[0]) % PACK_UNIT
        parts.append(jnp.pad(flat, (0, pad)).reshape(-1, 128))
    return jnp.concatenate(parts, axis=0)


def _unpack(buf, shapes):
    outs, row = [], 0
    for shp in shapes:
        n = int(np.prod(shp))
        rows = -(-n // PACK_UNIT) * 8
        outs.append(buf[row:row + rows].reshape(-1)[:n].reshape(shp))
        row += rows
    return outs


SMALL = ("norm1_g", "gm_ln_g", "gm_ln_b", "gm_ws", "gm_bs", "conv_w", "conv_b", "conv_ln_g", "conv_ln_b",
         "norm2_g", "final_g")
WEIGHTS = ("norm1_g", "w_in", "gm_ln_g", "gm_ln_b", "gm_ws", "gm_bs", "conv_w", "conv_b", "conv_ln_g",
           "conv_ln_b", "w_out", "norm2_g", "w_ffn_in", "w_ffn_out", "final_g")


def kernel(x, norm1_g, w_in, gm_ln_g, gm_ln_b, gm_ws, gm_bs, conv_w, conv_b, conv_ln_g, conv_ln_b, w_out, norm2_g, w_ffn_in, w_ffn_out, final_g, loss_target, m_norm1_g, m_w_in, m_gm_ln_g, m_gm_ln_b, m_gm_ws, m_gm_bs, m_conv_w, m_conv_b, m_conv_ln_g, m_conv_ln_b, m_w_out, m_norm2_g, m_w_ffn_in, m_w_ffn_out, m_final_g, v_norm1_g, v_w_in, v_gm_ln_g, v_gm_ln_b, v_gm_ws, v_gm_bs, v_conv_w, v_conv_b, v_conv_ln_g, v_conv_ln_b, v_w_out, v_norm2_g, v_w_ffn_in, v_w_ffn_out, v_final_g):
    given = dict(locals())
    t = x.shape[1]
    xc = x.reshape(t, D)
    target = loss_target.reshape(t, D)
    me = 2 * lax.axis_index("x") + lax.axis_index("y")
    core = lax.axis_index("c")
    tb = _tables(t)

    me = me.astype(jnp.int32)
    core = core.astype(jnp.int32)
    names = [n for n, _ in BIG]
    kinds = dict(BIG)
    gathered = [{n: _cast_to_gathered(given[n], l, me, f"cast_{n}{l}") for n in names} for l in range(DEPTH)]
    gathered[0]["w_in"] = _all_gather([gathered[0]["w_in"]], "all_gather_w_in0")[0]

    def weight(l, n):
        b = gathered[l][n]
        r_, c_ = 2 * b.shape[2], b.shape[3]
        return b.reshape(N_CHIPS, r_, c_) if kinds[n] == "col" else b.reshape(N_CHIPS * r_, c_)

    cshard = CV_W // N_CHIPS
    placed = lax.dynamic_update_slice(jnp.zeros((DEPTH, KCONV, CV_W), F32),
                                      conv_w * (core == 0).astype(F32), (0, 0, me * cshard))
    conv_w_full = _unpack(_all_reduce_small(_pack([placed]), me, core, "gather_conv_w"), [(DEPTH, KCONV, CV_W)])[0]
    cw32 = jnp.pad(conv_w_full, ((0, 0), (0, 32 - KCONV), (0, 0)))

    def row(a, l):
        return a[l].reshape(1, -1)

    saved = []
    early = ["w_in", "w_out", "w_ffn_in"]
    for l in range(DEPTH):
        cur = gathered[l]
        nxt = gathered[l + 1] if l + 1 < DEPTH else None
        sv = {"x": xc}
        bias = jnp.repeat(gm_bs[l].T, GM_W // GM_HEADS, axis=1)
        first = ["w_ffn_in"] if l == 0 else ["w_ffn_out"]
        late = ["w_out", "w_ffn_out"]
        proj, h1t, rid = _norm_mm(xc, row(norm1_g, l), weight(l, "w_in"), F32, f"in_proj{l}", 512,
                             [("ici" if l == 0 else "d2d", [cur[n] for n in first])], (tb["cos2"], tb["sin2"]))
        cur.update(zip(first, rid))
        y_gm, rid = _gm_fwd(proj, row(gm_ln_g, l), row(gm_ln_b, l), gm_ws[l], bias, f"gm_fwd{l}",
                            [("d2d", [cur[n] for n in first])] if l == 0 else ())
        cur.update(zip(first, rid))
        sf, sb = _ret_scan(proj, K_COL, proj, V_COL, tb["zf"], tb["zb"], tb["gcf"], tb["gcb"], f"ret_state{l}")
        a, y_ret, rid = _ret_out(proj, sf, sb, tb, f"ret_out{l}",
                                 [("ici", [cur[n] for n in late])] if l == 0 else ())
        cur.update(zip(late, rid))
        y_cv, hc, rid = _conv_fwd(proj, cw32[l], row(conv_b, l), row(conv_ln_g, l), row(conv_ln_b, l),
                                  f"conv_fwd{l}", [("d2d", [cur[n] for n in late])] if l == 0 else ())
        cur.update(zip(late, rid))
        x_mid = _parts_mm_res([y_gm, y_ret, y_cv], weight(l, "w_out"), xc, f"out_proj{l}")
        ff, h2t, rid = _norm_mm(x_mid, row(norm2_g, l), weight(l, "w_ffn_in"), ACT_DTYPE, f"ffn_in{l}", 512,
                           [("ici", [nxt[n] for n in early])] if nxt else ())
        if nxt:
            nxt.update(zip(early, rid))
        xc, rid = _swiglu_mm_res(ff, weight(l, "w_ffn_out"), x_mid, f"ffn_out{l}",
                                 [("d2d", [nxt[n] for n in early]), ("ici", [nxt["w_ffn_out"]])] if nxt else ())
        if nxt:
            nxt.update(zip(early + ["w_ffn_out"], rid))
        sv.update(bias=bias, proj=proj, h1t=h1t, h2t=h2t, y_gm=y_gm, sf=sf, sb=sb, a=a, y_ret=y_ret, y_cv=y_cv,
                  hc=hc, x_mid=x_mid,
                  ff=ff)
        saved.append(sv)

    dx, d_final_g, lpart = _loss_head(xc, final_g.reshape(1, D), target, "loss_head")

    small_g = {n: [None] * DEPTH for n in SMALL}
    qs = [{} for _ in range(DEPTH)]
    got = [{} for _ in range(DEPTH)]
    ffn_w, mix_w = ["w_ffn_out", "w_ffn_in"], ["w_out", "w_in"]

    def halves(big_g, group):
        return [big_g[n].reshape(N_CHIPS, 2, given[n].shape[1] // 2, given[n].shape[2]) for n in group]

    def pair_sums(l, group, g4, theirs):
        qs[l].update({n: _pair_sum(g, th, core, f"pair_sum_{n}{l}") for n, g, th in zip(group, g4, theirs)})
        return [qs[l][n] for n in group]

    for l in reversed(range(DEPTH)):
        sv = saved[l]
        proj = sv["proj"]
        big_g = {}
        dff = _dx_swiglu(dx, weight(l, "w_ffn_out"), sv["ff"], f"ffn_out_dx{l}")
        big_g["w_ffn_out"] = _dw_swiglu(sv["ff"], dx, f"ffn_out_dw{l}")
        dx_mid, dg2, _ = _dx_norm([dff], weight(l, "w_ffn_in"), sv["x_mid"], row(norm2_g, l), dx,
                                  f"ffn_in_dx{l}", 512)
        big_g["w_ffn_in"] = _dw_norm_cols(sv["h2t"], dff, w_ffn_in.shape[2], f"ffn_in_dw{l}")
        g4 = halves(big_g, ffn_w)
        (dy_gm, da, d_g, dy_cv), theirs = _out_proj_dx(dx_mid, weight(l, "w_out"), sv["a"], proj,
                                                       f"out_proj_dx{l}", [("pairx", g4)])
        q_ffn = pair_sums(l, ffn_w, g4, theirs)
        big_g["w_out"] = _dw_parts([sv["y_gm"], sv["y_ret"], sv["y_cv"]], dx_mid, f"out_proj_dw{l}")
        d_cv, dcw, dcb, dclg, dclb, rid = _conv_bwd(proj, dy_cv, sv["hc"], cw32[l], row(conv_ln_g, l),
                                                    row(conv_ln_b, l), f"conv_bwd{l}", [("scatter", q_ffn[:1])])
        got[l].update(zip(ffn_w[:1], rid))
        gb_, gf_ = _ret_scan(proj, Q_COL, da, 0, tb["xib"], tb["xif"], tb["gcb"], tb["gcf"], f"ret_bwd_state{l}")
        d_qkv, rid = _ret_bwd_main(proj, da, sv["sf"], sv["sb"], gf_, gb_, tb, f"ret_bwd_main{l}",
                                   [("scatter", q_ffn[1:])])
        got[l].update(zip(ffn_w[1:], rid))
        d_gm, dws, dbs, dglg, dglb = _gm_bwd(proj, dy_gm, row(gm_ln_g, l), row(gm_ln_b, l), gm_ws[l],
                                             jnp.swapaxes(gm_ws[l], 1, 2), sv["bias"], f"gm_bwd{l}")
        dparts = [d_gm, d_qkv, d_g, d_cv]
        big_g["w_in"] = _dw_norm_parts(sv["h1t"], dparts, w_in.shape[2], f"in_proj_dw{l}")
        g4 = halves(big_g, mix_w)
        q_mix = pair_sums(l, mix_w, g4, _pair_exchange(g4, f"grad_pair_exchange_mix{l}"))
        dx, dg1, rid = _dx_norm(dparts, weight(l, "w_in"), sv["x"], row(norm1_g, l), dx_mid, f"in_proj_dx{l}", 512,
                                [("scatter", q_mix)])
        got[l].update(zip(mix_w, rid))
        for n, val in (("norm1_g", dg1[0]), ("gm_ln_g", dglg[0]), ("gm_ln_b", dglb[0]), ("gm_ws", dws),
                       ("gm_bs", dbs[:, :GM_HEADS].T), ("conv_w", dcw[:KCONV]), ("conv_b", dcb[0]),
                       ("conv_ln_g", dclg[0]), ("conv_ln_b", dclb[0]), ("norm2_g", dg2[0])):
            small_g[n][l] = val

    small_shapes = [given[n].shape if n != "conv_w" else (DEPTH, KCONV, CV_W) for n in SMALL]
    partials = [d_final_g[0] if n == "final_g" else jnp.stack(small_g[n]) for n in SMALL]
    summed = _unpack(_all_reduce_small(_pack(partials + [lpart]), me, core, "all_reduce_small_grads"),
                     small_shapes + [lpart.shape])
    loss = summed[-1][0, 0]
    reduced = dict(zip(SMALL, summed))
    reduced["conv_w"] = lax.dynamic_slice(reduced["conv_w"], (0, 0, me * cshard), (DEPTH, KCONV, cshard))

    halves = [None] * NBIG
    for l in reversed(range(DEPTH)):
        halves = [_chip_sum(qs[l][n], got[l][n], l, me, core, h, f"chip_sum_{n}{l}") for n, h in zip(names, halves)]
    grads = dict(zip(names, _pair_gather(halves)))
    grads.update(reduced)

    delta, new_m, new_v = {}, {}, {}
    for n, _ in BIG:
        delta[n], new_m[n], new_v[n] = _adamw(given[n], grads[n], given["m_" + n], given["v_" + n], f"adamw_{n}")
    shapes = [given[n].shape for n in SMALL]
    packed = [_pack([src[n] if src is grads else src[p + n] for n in SMALL])
              for src, p in ((given, ""), (grads, ""), (given, "m_"), (given, "v_"))]
    outs = _adamw(*packed, "adamw_small")
    for dst, buf in zip((delta, new_m, new_v), outs):
        dst.update(zip(SMALL, _unpack(buf, shapes)))

    return (loss, dx.reshape(1, t, D), *[grads[n] for n in WEIGHTS], *[delta[n] for n in WEIGHTS],
            *[new_m[n] for n in WEIGHTS], *[new_v[n] for n in WEIGHTS])
```

```python
import functools
import math

import numpy as np
import jax
import jax.numpy as jnp
from jax import lax
from jax.experimental import pallas as pl
from jax.experimental.pallas import tpu as pltpu

F32 = jnp.float32
BF16 = jnp.bfloat16
MXU_DTYPE = BF16
ACT_DTYPE = BF16
S = jax.ShapeDtypeStruct

D = 1024
DEPTH = 2
GM_W = 256
GM_HEADS = 4
RET_W = 512
HEADS = 4
HD = 128
CV_W = 256
KCONV = 31
IN_W = 2 * GM_W + 4 * RET_W + 2 * CV_W
FFN_H = 2816
CH = 128
ROPE_BASE = 10000.0
EPS = 1e-6
N_CHIPS = 4
N_DEV = 8
HALO = 16

ADAM_LR = 0.001
ADAM_B1 = 0.9
ADAM_B2 = 0.999
ADAM_EPS = 1e-08
ADAM_WD = 0.01
ADAM_STEP = 10

VMEM_LIMIT = 52 * 1024 * 1024
MESH = pl.DeviceIdType.MESH


def _cp(*sem, vmem=VMEM_LIMIT):
    return pltpu.CompilerParams(dimension_semantics=tuple(sem), vmem_limit_bytes=vmem)


def _mx(a):
    return a.astype(MXU_DTYPE)


def _dot(a, b):
    return jnp.dot(_mx(a), _mx(b), preferred_element_type=F32)


def _dot_nt(a, b):
    return lax.dot_general(_mx(a), _mx(b), (((1,), (1,)), ((), ())), preferred_element_type=F32)


def _dot_tn(a, b):
    return lax.dot_general(_mx(a), _mx(b), (((0,), (0,)), ((), ())), preferred_element_type=F32)


def _sigmoid(x):
    return 1.0 / (1.0 + jnp.exp(-x))


def _gelu(x):
    return 0.5 * x * (1.0 + lax.erf(x * (1.0 / math.sqrt(2.0))))


def _gelu_grad(x):
    return 0.5 * (1.0 + lax.erf(x * (1.0 / math.sqrt(2.0)))) + x * jnp.exp(-0.5 * x * x) * (1.0 / math.sqrt(2.0 * math.pi))


def _rms_r(x):
    return lax.rsqrt(jnp.mean(x * x, axis=-1, keepdims=True) + EPS)


def _rms_bwd(dh, x, r, g):
    u = dh * g
    dx = r * u - x * (r * r * r) * jnp.mean(u * x, axis=-1, keepdims=True)
    return dx, dh * x * r


def _standardize(a):
    mu = jnp.mean(a, axis=-1, keepdims=True)
    d = a - mu
    r = lax.rsqrt(jnp.mean(d * d, axis=-1, keepdims=True) + EPS)
    return d * r, r


def _standardize_bwd(do, o, r):
    return r * (do - jnp.mean(do, axis=-1, keepdims=True) - o * jnp.mean(do * o, axis=-1, keepdims=True))


def _acc_out(ref, val, first):
    @pl.when(first)
    def _():
        ref[...] = val

    @pl.when(jnp.logical_not(first))
    def _():
        ref[...] += val


def _row_tile(t, pref):
    tm = min(t, pref)
    assert t % tm == 0, (t, tm)
    return tm


def _segments(part_widths, shard_w):
    bounds = {0}
    off = 0
    for w in part_widths:
        off += w
        bounds.add(off)
    total = off
    for j in range(1, total // shard_w + 1):
        bounds.add(j * shard_w)
    bounds = sorted(bounds)
    starts = np.cumsum([0] + list(part_widths))
    segs = []
    for a, b in zip(bounds[:-1], bounds[1:]):
        p = int(np.searchsorted(starts, a, side="right") - 1)
        segs.append((p, a - int(starts[p]), a // shard_w, a % shard_w, b - a))
    return segs


ANY = pl.BlockSpec(memory_space=pl.ANY)


def _mesh_pos():
    x, y, c = lax.axis_index("x"), lax.axis_index("y"), lax.axis_index("c")
    chips = [(1 - x, y), (x, 1 - y), (1 - x, 1 - y)]
    return x, y, c, 2 * x + y, chips, [2 * cx + cy for cx, cy in chips]


def _rider_copies(kind, i_refs, o_refs, send, recv, pos):
    x, y, c, me, chips, cj = pos
    out = []
    for b, (i_ref, o_ref) in enumerate(zip(i_refs, o_refs)):
        for k in range(1 if kind == "pairx" else 3):
            if kind == "ici":
                src, dst, land, dev = i_ref.at[me, c], o_ref.at[me, c], o_ref.at[cj[k], c], (*chips[k], c)
            elif kind == "d2d":
                src, dst, land, dev = i_ref.at[cj[k], c], o_ref.at[cj[k], c], o_ref.at[cj[k], 1 - c], (x, y, 1 - c)
            elif kind == "pairx":
                src, dst, land, dev = i_ref.at[:, 1 - c], o_ref, o_ref, (x, y, 1 - c)
            else:
                src, dst, land, dev = i_ref.at[cj[k]], o_ref.at[k], o_ref.at[k], (*chips[k], c)
            out.append(tuple(pltpu.make_async_remote_copy(
                src_ref=s_, dst_ref=d_, send_sem=send.at[b, k], recv_sem=recv.at[b, k],
                device_id=dev, device_id_type=MESH) for s_, d_ in ((src, dst), (land, land))))
    return out


def _rider_out_shape(kind, a):
    if kind == "scatter":
        return S((3,) + a.shape[1:], a.dtype)
    if kind == "pairx":
        return S(a.shape[:1] + a.shape[2:], a.dtype)
    return S(a.shape, a.dtype)


def _pcall(body, args, riders, *, grid, in_specs, out_specs, out_shape, name, sem, scratch_shapes=()):
    outs = list(out_shape)
    if not riders:
        res = pl.pallas_call(body, grid=grid, in_specs=in_specs, out_specs=out_specs, out_shape=outs, name=name,
                             scratch_shapes=list(scratch_shapes), compiler_params=_cp(*sem))(*args)
        return res, []
    r_in = [a for _, bufs in riders for a in bufs]
    r_out = [_rider_out_shape(kind, a) for kind, bufs in riders for a in bufs]
    n_in, n_out, n_scr, n_r = len(args), len(outs), len(scratch_shapes), len(r_in)
    aliases, idx = {}, 0
    for kind, bufs in riders:
        for _ in bufs:
            if kind in ("ici", "d2d"):
                aliases[n_in + idx] = n_out + idx
            idx += 1
    sems = [pltpu.SemaphoreType.DMA((len(bufs), 3)) for _, bufs in riders for _ in range(2)]

    def wrapped(*refs):
        a, ri = refs[:n_in], refs[n_in:n_in + n_r]
        o, ro = refs[n_in + n_r:n_in + n_r + n_out], refs[n_in + n_r + n_out:n_in + 2 * n_r + n_out]
        scr = refs[n_in + 2 * n_r + n_out:n_in + 2 * n_r + n_out + n_scr]
        sm = refs[n_in + 2 * n_r + n_out + n_scr:]
        pos = _mesh_pos()
        copies, off = [], 0
        for r, (kind, bufs) in enumerate(riders):
            copies += _rider_copies(kind, ri[off:off + len(bufs)], ro[off:off + len(bufs)], sm[2 * r], sm[2 * r + 1], pos)
            off += len(bufs)
        ids = [pl.program_id(d) for d in range(len(grid))]
        first = functools.reduce(jnp.logical_and, [i == 0 for i in ids])
        last = functools.reduce(jnp.logical_and, [i == n - 1 for i, n in zip(ids, grid)])

        @pl.when(first)
        def _():
            for cp, _ in copies:
                cp.start()

        body(*a, *o, *scr)

        @pl.when(last)
        def _():
            for cp, land in copies:
                land.wait_recv()
                cp.wait_send()

    res = pl.pallas_call(
        wrapped, grid=grid, in_specs=list(in_specs) + [ANY] * n_r, out_specs=list(out_specs) + [ANY] * n_r,
        out_shape=outs + r_out, input_output_aliases=aliases, name=name,
        scratch_shapes=list(scratch_shapes) + sems, compiler_params=_cp(*(("arbitrary",) * len(grid))))(*args, *r_in)
    return res[:n_out], res[n_out:]


def _wcol_spec(w):
    return pl.BlockSpec(w.shape, lambda *_: (0, 0, 0))


def _wrow_spec(w):
    return pl.BlockSpec(w.shape, lambda *_: (0, 0))


def _norm_mm(x, g, w, out_dtype, name, tm_pref, riders=(), rope=None):
    t = x.shape[0]
    nc = w.shape[2]
    tm = _row_tile(t, tm_pref)
    extra = list(rope) if rope else []

    qkv_w = 3 * RET_W

    def body(x_ref, g_ref, w_ref, *rest):
        o_ref, ht_ref = rest[len(extra)], rest[len(extra) + 1]
        xv = x_ref[...]
        hf = xv * _rms_r(xv) * g_ref[...]
        h = _mx(hf)
        for j in range(N_CHIPS):
            o_ref[:, j * nc:(j + 1) * nc] = jnp.dot(h, w_ref[j], preferred_element_type=F32).astype(o_ref.dtype)
        if rope:
            _rotate_qk(o_ref, rest[0][...], rest[1][...])
            rest[-1][...] = o_ref[:, Q_COL * RET_W:Q_COL * RET_W + qkv_w].astype(rest[-1].dtype)
        ht_ref[...] = hf.T.astype(ht_ref.dtype)

    outs, rid = _pcall(
        body, [x, g, w] + extra, riders, grid=(t // tm,),
        in_specs=[pl.BlockSpec((tm, D), lambda i: (i, 0)), pl.BlockSpec((1, D), lambda i: (0, 0)), _wcol_spec(w)]
        + [pl.BlockSpec((tm, HD), lambda i: (i, 0)) for _ in extra],
        out_specs=[pl.BlockSpec((tm, N_CHIPS * nc), lambda i: (i, 0)), pl.BlockSpec((D, tm), lambda i: (0, i))]
        + ([pl.BlockSpec((tm, qkv_w), lambda i: (i, 0))] if rope else []),
        out_shape=[S((t, N_CHIPS * nc), out_dtype), S((D, t), MXU_DTYPE)]
        + ([S((t, qkv_w), ACT_DTYPE)] if rope else []), name=name, sem=("parallel",))
    return (*outs, rid) if rope else (*outs, None, rid)


def _parts_mm_res(parts, w, res, name):
    t = res.shape[0]
    tm = _row_tile(t, 512)
    widths = [p.shape[1] for p in parts]
    offs = np.cumsum([0] + widths)
    n = len(parts)

    def body(*refs):
        p_refs, w_ref, r_ref, o_ref = refs[:n], refs[n], refs[n + 1], refs[n + 2]
        acc = r_ref[...]
        for p in range(n):
            acc = acc + _dot(p_refs[p][...], w_ref[int(offs[p]):int(offs[p + 1]), :])
        o_ref[...] = acc

    return pl.pallas_call(
        body, grid=(t // tm,),
        in_specs=[pl.BlockSpec((tm, wd), lambda i: (i, 0)) for wd in widths]
        + [_wrow_spec(w), pl.BlockSpec((tm, D), lambda i: (i, 0))],
        out_specs=pl.BlockSpec((tm, D), lambda i: (i, 0)),
        out_shape=S((t, D), F32), name=name, compiler_params=_cp("parallel"))(*parts, w, res)


def _swiglu(ff):
    gate = ff[:, :FFN_H].astype(F32)
    up = ff[:, FFN_H:].astype(F32)
    return gate * _sigmoid(gate) * up


def _swiglu_mm_res(ff, w, res, name, riders=()):
    t = res.shape[0]
    tm = _row_tile(t, 512)

    def body(f_ref, w_ref, r_ref, o_ref):
        o_ref[...] = r_ref[...] + _dot(_swiglu(f_ref[...]), w_ref[...])

    (out,), rid = _pcall(
        body, [ff, w, res], riders, grid=(t // tm,),
        in_specs=[pl.BlockSpec((tm, 2 * FFN_H), lambda i: (i, 0)), _wrow_spec(w),
                  pl.BlockSpec((tm, D), lambda i: (i, 0))],
        out_specs=[pl.BlockSpec((tm, D), lambda i: (i, 0))],
        out_shape=[S((t, D), F32)], name=name, sem=("parallel",))
    return out, rid


def _dx_norm(dparts, w, x, g, dres, name, tm_pref, riders=()):
    t = x.shape[0]
    nc = w.shape[2]
    tm = _row_tile(t, tm_pref)
    widths = [p.shape[1] for p in dparts]
    segs = _segments(widths, nc)
    n = len(dparts)

    def body(*refs):
        d_refs = refs[:n]
        w_ref, x_ref, g_ref, r_ref, dx_ref, dg_ref = refs[n:]
        dh = jnp.zeros((tm, D), F32)
        for (p, po, j, jo, wd) in segs:
            dh = dh + _dot_nt(d_refs[p][:, po:po + wd], w_ref[j, :, jo:jo + wd])
        xv = x_ref[...]
        dx, dgrow = _rms_bwd(dh, xv, _rms_r(xv), g_ref[...])
        dx_ref[...] = r_ref[...] + dx
        _acc_out(dg_ref, jnp.sum(dgrow, axis=0, keepdims=True), pl.program_id(0) == 0)

    (dx, dg), rid = _pcall(
        body, [*dparts, w, x, g, dres], riders, grid=(t // tm,),
        in_specs=[pl.BlockSpec((tm, wd), lambda i: (i, 0)) for wd in widths]
        + [_wcol_spec(w), pl.BlockSpec((tm, D), lambda i: (i, 0)),
           pl.BlockSpec((1, D), lambda i: (0, 0)), pl.BlockSpec((tm, D), lambda i: (i, 0))],
        out_specs=[pl.BlockSpec((tm, D), lambda i: (i, 0)), pl.BlockSpec((1, D), lambda i: (0, 0))],
        out_shape=[S((t, D), F32), S((1, D), F32)], name=name, sem=("arbitrary",))
    return dx, dg, rid


def _out_proj_dx(dy, w, a, proj, name, riders=()):
    t = dy.shape[0]
    tm = _row_tile(t, 512)
    wr = HEADS * HD

    def body(dy_ref, w_ref, a_ref, g_ref, dgm_ref, da_ref, dg_ref, dcv_ref):
        dyv = _mx(dy_ref[...])
        dgm_ref[...] = _dot_nt(dyv, w_ref[0:GM_W, :])
        dcv_ref[...] = _dot_nt(dyv, w_ref[GM_W + RET_W:, :])
        for h in range(HEADS):
            cols = slice(h * HD, (h + 1) * HD)
            dyr = _dot_nt(dyv, w_ref[GM_W + h * HD:GM_W + (h + 1) * HD, :])
            o, r = _standardize(a_ref[:, cols])
            gv = g_ref[:, cols]
            s = _sigmoid(gv)
            dg_ref[:, cols] = (dyr * o * (s * (1.0 + gv * (1.0 - s)))).astype(dg_ref.dtype)
            da_ref[:, cols] = _standardize_bwd(dyr * (gv * s), o, r).astype(da_ref.dtype)

    return _pcall(
        body, [dy, w, a, proj], riders, grid=(t // tm,),
        in_specs=[pl.BlockSpec((tm, D), lambda i: (i, 0)), _wrow_spec(w), pl.BlockSpec((tm, wr), lambda i: (i, 0)),
                  pl.BlockSpec((tm, wr), lambda i: (i, GATE_COL))],
        out_specs=[pl.BlockSpec((tm, GM_W), lambda i: (i, 0)), pl.BlockSpec((tm, wr), lambda i: (i, 0)),
                   pl.BlockSpec((tm, wr), lambda i: (i, 0)), pl.BlockSpec((tm, CV_W), lambda i: (i, 0))],
        out_shape=[S((t, GM_W), F32), S((t, wr), ACT_DTYPE), S((t, wr), ACT_DTYPE), S((t, CV_W), F32)],
        name=name, sem=("parallel",))


def _dx_swiglu(dy, w, ff, name):
    t = dy.shape[0]
    tm = _row_tile(t, 512)

    def body(dy_ref, w_ref, f_ref, o_ref):
        dact = _dot_nt(dy_ref[...], w_ref[...])
        gate = f_ref[:, :FFN_H].astype(F32)
        up = f_ref[:, FFN_H:].astype(F32)
        s = _sigmoid(gate)
        gs = gate * s
        o_ref[:, :FFN_H] = ((dact * up) * (s + gs - gs * s)).astype(o_ref.dtype)
        o_ref[:, FFN_H:] = (dact * gs).astype(o_ref.dtype)

    return pl.pallas_call(
        body, grid=(t // tm,),
        in_specs=[pl.BlockSpec((tm, D), lambda i: (i, 0)), _wrow_spec(w),
                  pl.BlockSpec((tm, 2 * FFN_H), lambda i: (i, 0))],
        out_specs=pl.BlockSpec((tm, 2 * FFN_H), lambda i: (i, 0)),
        out_shape=S((t, 2 * FFN_H), ACT_DTYPE), name=name, compiler_params=_cp("parallel"))(dy, w, ff)


def _call_into(body, into, in_specs, args, *, n_prefetch, grid, out_specs, **kw):
    n_in = len(args)
    if into is None:
        gs = pltpu.PrefetchScalarGridSpec(num_scalar_prefetch=n_prefetch, grid=grid, in_specs=in_specs,
                                          out_specs=out_specs)
        return pl.pallas_call(body, grid_spec=gs, **kw)(*args)

    def wrapped(*refs):
        return body(*refs[:n_in], *refs[n_in + 1:])

    gs = pltpu.PrefetchScalarGridSpec(num_scalar_prefetch=n_prefetch, grid=grid,
                                      in_specs=list(in_specs) + [ANY], out_specs=out_specs)
    return pl.pallas_call(wrapped, grid_spec=gs, input_output_aliases={n_in: 0}, **kw)(*args, into)


def _dw_norm_parts(ht, dparts, nc, name):
    t = ht.shape[1]
    tk = _row_tile(t, 1024)
    widths = [p.shape[1] for p in dparts]
    segs = _segments(widths, nc)
    n = len(dparts)
    nk = t // tk

    def body(*refs):
        h_ref, d_refs, o_ref, acc_ref = refs[0], refs[1:1 + n], refs[1 + n], refs[2 + n]
        k = pl.program_id(0)
        h = h_ref[...]

        @pl.when(k == 0)
        def _():
            acc_ref[...] = jnp.zeros_like(acc_ref)

        for (p, po, j, jo, wd) in segs:
            acc_ref[j, :, jo:jo + wd] += _dot(h, d_refs[p][:, po:po + wd])

        @pl.when(k == nk - 1)
        def _():
            o_ref[...] = acc_ref[...].astype(o_ref.dtype)

    return pl.pallas_call(
        body, grid=(nk,),
        in_specs=[pl.BlockSpec((D, tk), lambda k: (0, k))]
        + [pl.BlockSpec((tk, wd), lambda k: (k, 0)) for wd in widths],
        out_specs=pl.BlockSpec((N_CHIPS, D, nc), lambda k: (0, 0, 0)),
        out_shape=S((N_CHIPS, D, nc), MXU_DTYPE), name=name,
        scratch_shapes=[pltpu.VMEM((N_CHIPS, D, nc), F32)], compiler_params=_cp("arbitrary"))(ht, *dparts)


def _dw_norm_cols(ht, dy, nc, name):
    t = ht.shape[1]
    tk = _row_tile(t, 2048)
    nk = t // tk

    def body(h_ref, dy_ref, o_ref, acc_ref):
        k = pl.program_id(1)

        @pl.when(k == 0)
        def _():
            acc_ref[...] = jnp.zeros_like(acc_ref)

        acc_ref[...] += _dot(h_ref[...], dy_ref[...])

        @pl.when(k == nk - 1)
        def _():
            o_ref[...] = acc_ref[...].astype(o_ref.dtype)

    return pl.pallas_call(
        body, grid=(N_CHIPS, nk),
        in_specs=[pl.BlockSpec((D, tk), lambda j, k: (0, k)), pl.BlockSpec((tk, nc), lambda j, k: (k, j))],
        out_specs=pl.BlockSpec((None, D, nc), lambda j, k: (j, 0, 0)),
        out_shape=S((N_CHIPS, D, nc), MXU_DTYPE), name=name,
        scratch_shapes=[pltpu.VMEM((D, nc), F32)], compiler_params=_cp("parallel", "arbitrary"))(ht, dy)


def _dw_parts(parts, dy, name):
    t = dy.shape[0]
    tk = _row_tile(t, 1024)
    widths = [p.shape[1] for p in parts]
    offs = np.cumsum([0] + widths)
    ktot = int(offs[-1])
    n = len(parts)
    nk = t // tk

    def body(*refs):
        p_refs, dy_ref, o_ref, acc_ref = refs[:n], refs[n], refs[n + 1], refs[n + 2]
        k = pl.program_id(0)

        @pl.when(k == 0)
        def _():
            acc_ref[...] = jnp.zeros_like(acc_ref)

        dyv = _mx(dy_ref[...])
        for p in range(n):
            acc_ref[int(offs[p]):int(offs[p + 1]), :] += _dot_tn(p_refs[p][...], dyv)

        @pl.when(k == nk - 1)
        def _():
            o_ref[...] = acc_ref[...].astype(o_ref.dtype)

    return pl.pallas_call(
        body, grid=(nk,),
        in_specs=[pl.BlockSpec((tk, wd), lambda k: (k, 0)) for wd in widths]
        + [pl.BlockSpec((tk, D), lambda k: (k, 0))],
        out_specs=pl.BlockSpec((ktot, D), lambda k: (0, 0)),
        out_shape=S((ktot, D), MXU_DTYPE), name=name,
        scratch_shapes=[pltpu.VMEM((ktot, D), F32)], compiler_params=_cp("arbitrary"))(*parts, dy)


def _dw_swiglu(ff, dy, name):
    t = dy.shape[0]
    tk = _row_tile(t, 512)
    nk = t // tk

    def body(f_ref, dy_ref, o_ref, acc_ref):
        k = pl.program_id(0)

        @pl.when(k == 0)
        def _():
            acc_ref[...] = jnp.zeros_like(acc_ref)

        acc_ref[...] += _dot_tn(_swiglu(f_ref[...]), dy_ref[...])

        @pl.when(k == nk - 1)
        def _():
            o_ref[...] = acc_ref[...].astype(o_ref.dtype)

    return pl.pallas_call(
        body, grid=(nk,),
        in_specs=[pl.BlockSpec((tk, 2 * FFN_H), lambda k: (k, 0)), pl.BlockSpec((tk, D), lambda k: (k, 0))],
        out_specs=pl.BlockSpec((FFN_H, D), lambda k: (0, 0)),
        out_shape=S((FFN_H, D), MXU_DTYPE), name=name,
        scratch_shapes=[pltpu.VMEM((FFN_H, D), F32)], compiler_params=_cp("arbitrary"))(ff, dy)


def _tables(t):
    half = HD // 2
    inv_freq = ROPE_BASE ** (-jnp.arange(half, dtype=F32) / half)
    base = (jnp.arange(t // CH, dtype=F32) * CH)[:, None] * inv_freq[None, :]
    off = jnp.arange(CH, dtype=F32)[:, None] * inv_freq[None, :]
    cb, sb, co, so = jnp.cos(base)[:, None], jnp.sin(base)[:, None], jnp.cos(off)[None], jnp.sin(off)[None]
    cos = (cb * co - sb * so).reshape(t, half)
    sin = (sb * co + cb * so).reshape(t, half)
    tb = {"cos2": jnp.concatenate([cos, cos], axis=1), "sin2": jnp.concatenate([-sin, sin], axis=1)}
    gf = 1.0 - jnp.exp2(-5.0 - jnp.arange(HEADS, dtype=F32))
    lgf = jnp.log(gf)[:, None]
    lgb = jnp.log(gf[::-1])[:, None]
    idx = jnp.arange(CH, dtype=F32)
    diff = idx[:, None] - idx[None, :]
    dfwd = jnp.where(diff >= 0, jnp.exp(lgf[:, :, None] * jnp.where(diff >= 0, diff, 0.0)), 0.0)
    dbwd = jnp.where(diff < 0, jnp.exp(lgb[:, :, None] * jnp.where(diff < 0, -diff, 0.0)), 0.0)
    tb["dm"] = dfwd + dbwd
    tb["dmt"] = jnp.swapaxes(tb["dm"], 1, 2)

    def lanes(a):
        return jnp.repeat(a.T, HD, axis=1)

    tb["xif"] = lanes(jnp.exp(lgf * (idx + 1)))
    tb["zf"] = lanes(jnp.exp(lgf * (CH - 1 - idx)))
    tb["xib"] = lanes(jnp.exp(lgb * (CH - idx)))
    tb["zb"] = lanes(jnp.exp(lgb * idx))
    tb["gcf"] = jnp.repeat(jnp.exp(lgf * CH), HD, axis=0).reshape(1, HEADS * HD)
    tb["gcb"] = jnp.repeat(jnp.exp(lgb * CH), HD, axis=0).reshape(1, HEADS * HD)
    return tb


def _full(shape):
    nd = len(shape)
    return pl.BlockSpec(shape, lambda *_: (0,) * nd)


def _gm_mixed(vn, ws_ref, bias):
    lane = lax.broadcasted_iota(jnp.int32, (CH, 128), 1)
    halves = []
    for hf in range(2):
        vh = _mx(vn[:, hf * 128:(hf + 1) * 128])
        r0 = jnp.dot(_mx(ws_ref[2 * hf]), vh, preferred_element_type=F32)
        r1 = jnp.dot(_mx(ws_ref[2 * hf + 1]), vh, preferred_element_type=F32)
        halves.append(jnp.where(lane < 64, r0, r1))
    return jnp.concatenate(halves, axis=1) + bias


def _gm_fwd(proj, ln_g, ln_b, ws, bias, name, riders=()):
    t = proj.shape[0]
    tm = _row_tile(t, 512)

    def body(pu_ref, pv_ref, g_ref, b_ref, ws_ref, bias_ref, o_ref):
        for c in range(tm // CH):
            rows = slice(c * CH, (c + 1) * CH)
            u = _gelu(pu_ref[rows, :])
            o, _ = _standardize(_gelu(pv_ref[rows, :]))
            vn = o * g_ref[...] + b_ref[...]
            o_ref[rows, :] = (u * _gm_mixed(vn, ws_ref, bias_ref[...])).astype(o_ref.dtype)

    (out,), rid = _pcall(
        body, [proj, proj, ln_g, ln_b, ws, bias], riders, grid=(t // tm,),
        in_specs=[pl.BlockSpec((tm, GM_W), lambda i: (i, 0)), pl.BlockSpec((tm, GM_W), lambda i: (i, 1)),
                  _full((1, GM_W)), _full((1, GM_W)), _full((GM_HEADS, CH, CH)), _full((CH, GM_W))],
        out_specs=[pl.BlockSpec((tm, GM_W), lambda i: (i, 0))],
        out_shape=[S((t, GM_W), ACT_DTYPE)], name=name, sem=("parallel",))
    return out, rid


def _gm_bwd(proj, dy, ln_g, ln_b, ws, wst, bias, name):
    t = proj.shape[0]
    tm = _row_tile(t, 512)
    nb = t // tm

    def body(pu_ref, pv_ref, dy_ref, g_ref, b_ref, ws_ref, wst_ref, bias_ref,
             d_ref, dws_ref, dbs_ref, dg_ref, db_ref, dbias_ref):
        first = pl.program_id(0) == 0
        lane = lax.broadcasted_iota(jnp.int32, (CH, 128), 1)
        dws = [jnp.zeros((CH, CH), F32) for _ in range(GM_HEADS)]
        dbias = jnp.zeros((CH, GM_W), F32)
        dg = jnp.zeros((1, GM_W), F32)
        db = jnp.zeros((1, GM_W), F32)
        for c in range(tm // CH):
            rows = slice(c * CH, (c + 1) * CH)
            pu = pu_ref[rows, :]
            pv = pv_ref[rows, :]
            u = _gelu(pu)
            o, r = _standardize(_gelu(pv))
            vn = o * g_ref[...] + b_ref[...]
            mixed = _gm_mixed(vn, ws_ref, bias_ref[...])
            dyv = dy_ref[rows, :]
            d_ref[rows, :GM_W] = (dyv * mixed * _gelu_grad(pu)).astype(d_ref.dtype)
            dmixed = dyv * u
            dbias = dbias + dmixed
            dvn_halves = []
            for hf in range(2):
                dm = dmixed[:, hf * 128:(hf + 1) * 128]
                vh = vn[:, hf * 128:(hf + 1) * 128]
                dm0 = jnp.where(lane < 64, dm, 0.0)
                dm1 = dm - dm0
                dws[2 * hf] = dws[2 * hf] + _dot_nt(dm0, vh)
                dws[2 * hf + 1] = dws[2 * hf + 1] + _dot_nt(dm1, vh)
                t0 = _dot(wst_ref[2 * hf], dm)
                t1 = _dot(wst_ref[2 * hf + 1], dm)
                dvn_halves.append(jnp.where(lane < 64, t0, t1))
            dvn = jnp.concatenate(dvn_halves, axis=1)
            dg = dg + jnp.sum(dvn * o, axis=0, keepdims=True)
            db = db + jnp.sum(dvn, axis=0, keepdims=True)
            dv = _standardize_bwd(dvn * g_ref[...], o, r)
            d_ref[rows, GM_W:] = (dv * _gelu_grad(pv)).astype(d_ref.dtype)
        for h in range(GM_HEADS):
            _acc_out(dws_ref.at[h], dws[h], first)
        _acc_out(dbias_ref, dbias, first)
        _acc_out(dg_ref, dg, first)
        _acc_out(db_ref, db, first)

        @pl.when(pl.program_id(0) == nb - 1)
        def _():
            tot = dbias_ref[...]
            head = lax.broadcasted_iota(jnp.int32, (CH, GM_W), 1) // (GM_W // GM_HEADS)
            out = jnp.zeros((CH, 128), F32)
            for h in range(GM_HEADS):
                s = jnp.sum(jnp.where(head == h, tot, 0.0), axis=1, keepdims=True)
                out = jnp.where(lane == h, s, out)
            dbs_ref[...] = out

    return pl.pallas_call(
        body, grid=(nb,),
        in_specs=[pl.BlockSpec((tm, GM_W), lambda i: (i, 0)), pl.BlockSpec((tm, GM_W), lambda i: (i, 1)),
                  pl.BlockSpec((tm, GM_W), lambda i: (i, 0)),
                  _full((1, GM_W)), _full((1, GM_W)), _full((GM_HEADS, CH, CH)), _full((GM_HEADS, CH, CH)),
                  _full((CH, GM_W))],
        out_specs=[pl.BlockSpec((tm, 2 * GM_W), lambda i: (i, 0)), _full((GM_HEADS, CH, CH)), _full((CH, 128)),
                   _full((1, GM_W)), _full((1, GM_W))],
        out_shape=[S((t, 2 * GM_W), ACT_DTYPE), S((GM_HEADS, CH, CH), F32), S((CH, 128), F32),
                   S((1, GM_W), F32), S((1, GM_W), F32)],
        scratch_shapes=[pltpu.VMEM((CH, GM_W), F32)],
        name=name, compiler_params=_cp("arbitrary"))(proj, proj, dy, ln_g, ln_b, ws, wst, bias)


def _rot(x, cos2, sin2):
    return x * cos2 + pltpu.roll(x, HD // 2, 1) * sin2


def _rot_bwd(dx, cos2, sin2):
    return dx * cos2 + pltpu.roll(dx * sin2, HD // 2, 1)


Q_COL, K_COL, V_COL, GATE_COL = 1, 2, 3, 4


def _rotate_qk(o_ref, cos2, sin2):
    for col, scale in ((Q_COL, 1.0), (K_COL, HD ** -0.5)):
        for h in range(HEADS):
            cols = slice(col * RET_W + h * HD, col * RET_W + (h + 1) * HD)
            o_ref[:, cols] = _rot(o_ref[:, cols], cos2, sin2) * scale


def _ret_scan(lhs, lhs_col, rhs, rhs_col, lp, ls, gp, gs, name):
    t = lhs.shape[0]
    n = t // CH
    r = 4 if n % 4 == 0 else 1
    ns = n // r

    def body(lp_ref, ls_ref, gp_ref, gs_ref, l1_ref, r1_ref, l2_ref, r2_ref, pre_ref, suf_ref, sp_ref, ss_ref):
        @pl.when(pl.program_id(0) == 0)
        def _():
            sp_ref[...] = jnp.zeros_like(sp_ref)
            ss_ref[...] = jnp.zeros_like(ss_ref)

        def kv(l_ref, r_ref, scale, rows):
            lv = l_ref[rows, :] * scale
            rv = r_ref[rows, :]
            return jnp.concatenate([_dot_tn(lv[:, h * HD:(h + 1) * HD], rv[:, h * HD:(h + 1) * HD])
                                    for h in range(HEADS)], axis=1)

        for j in range(r):
            pre_ref[j] = sp_ref[...].astype(pre_ref.dtype)
            sp_ref[...] = sp_ref[...] * gp_ref[...] + kv(l1_ref, r1_ref, lp_ref[...], slice(j * CH, (j + 1) * CH))
        for j in reversed(range(r)):
            suf_ref[j] = ss_ref[...].astype(suf_ref.dtype)
            ss_ref[...] = ss_ref[...] * gs_ref[...] + kv(l2_ref, r2_ref, ls_ref[...], slice(j * CH, (j + 1) * CH))

    w = HEADS * HD
    return pl.pallas_call(
        body, grid=(ns,),
        in_specs=[_full((CH, w)), _full((CH, w)), _full((1, w)), _full((1, w)),
                  pl.BlockSpec((r * CH, w), lambda s: (s, lhs_col)), pl.BlockSpec((r * CH, w), lambda s: (s, rhs_col)),
                  pl.BlockSpec((r * CH, w), lambda s: (ns - 1 - s, lhs_col)),
                  pl.BlockSpec((r * CH, w), lambda s: (ns - 1 - s, rhs_col))],
        out_specs=[pl.BlockSpec((r, HD, w), lambda s: (s, 0, 0)), pl.BlockSpec((r, HD, w), lambda s: (ns - 1 - s, 0, 0))],
        out_shape=[S((n, HD, w), MXU_DTYPE)] * 2, name=name,
        scratch_shapes=[pltpu.VMEM((HD, w), F32), pltpu.VMEM((HD, w), F32)],
        compiler_params=_cp("arbitrary"))(lp, ls, gp, gs, lhs, rhs, lhs, rhs)


def _ret_out(proj, qkv, sf, sb, tb, name, riders=()):
    t = proj.shape[0]
    r = 4 if (t // CH) % 4 == 0 else 1
    tm = r * CH
    w = HEADS * HD

    def body(rq_ref, rk_ref, v_ref, g_ref, sf_ref, sb_ref, dm_ref, xif_ref, xib_ref, a_ref, y_ref):
        for c in range(r):
            rows = slice(c * CH, (c + 1) * CH)
            for h in range(HEADS):
                cols = slice(h * HD, (h + 1) * HD)
                q = rq_ref[rows, cols]
                p = _dot_nt(q, rk_ref[rows, cols]) * dm_ref[h]
                a = (_dot(p, v_ref[rows, cols]) + _dot(q * xif_ref[:, cols], sf_ref[c, :, cols])
                     + _dot(q * xib_ref[:, cols], sb_ref[c, :, cols]))
                a_ref[rows, cols] = a
                o, _ = _standardize(a)
                gv = g_ref[rows, cols]
                y_ref[rows, cols] = (o * (gv * _sigmoid(gv))).astype(y_ref.dtype)

    (a, y), rid = _pcall(
        body, [qkv, qkv, qkv, proj, sf, sb, tb["dm"], tb["xif"], tb["xib"]], riders, grid=(t // tm,),
        in_specs=[pl.BlockSpec((tm, w), lambda i: (i, 0)), pl.BlockSpec((tm, w), lambda i: (i, 1)),
                  pl.BlockSpec((tm, w), lambda i: (i, 2)), pl.BlockSpec((tm, w), lambda i: (i, GATE_COL)),
                  pl.BlockSpec((r, HD, w), lambda i: (i, 0, 0)), pl.BlockSpec((r, HD, w), lambda i: (i, 0, 0)),
                  _full((HEADS, CH, CH)), _full((CH, w)), _full((CH, w))],
        out_specs=[pl.BlockSpec((tm, w), lambda i: (i, 0))] * 2,
        out_shape=[S((t, w), F32), S((t, w), ACT_DTYPE)], name=name, sem=("parallel",))
    return a, y, rid


def _ret_bwd_main(qkv, da, sf, sb, gf, gb, tb, name, riders=()):
    t = qkv.shape[0]
    r = 4 if (t // CH) % 4 == 0 else 1
    tm = r * CH
    w = HEADS * HD
    scale = HD ** -0.5

    def body(rq_ref, rk_ref, v_ref, da_ref, sf_ref, sb_ref, gf_ref, gb_ref, dm_ref, dmt_ref,
             xif_ref, xib_ref, zf_ref, zb_ref, c_ref, s_ref, o_ref):
        for c in range(r):
            rows = slice(c * CH, (c + 1) * CH)
            cos2, sin2 = c_ref[rows, :], s_ref[rows, :]
            for h in range(HEADS):
                cols = slice(h * HD, (h + 1) * HD)
                q, k, v, dav = rq_ref[rows, cols], rk_ref[rows, cols], v_ref[rows, cols], da_ref[rows, cols]
                qm, km, vm, dam = _mx(q), _mx(k), _mx(v), _mx(dav)
                dm, dmt = dm_ref[h], dmt_ref[h]
                pt = _dot_nt(km, qm) * dmt
                dp = _dot_nt(dam, vm) * dm
                dpt = _dot_nt(vm, dam) * dmt
                sfh, sbh, gfh, gbh = sf_ref[c, :, cols], sb_ref[c, :, cols], gf_ref[c, :, cols], gb_ref[c, :, cols]
                zf, zb = zf_ref[:, cols], zb_ref[:, cols]
                dv = _dot(pt, dam) + zf * _dot(km, gfh) + zb * _dot(km, gbh)
                drq = _dot(dp, km) + xif_ref[:, cols] * _dot_nt(dam, sfh) + xib_ref[:, cols] * _dot_nt(dam, sbh)
                drk = _dot(dpt, qm) + _dot_nt(zf * v, gfh) + _dot_nt(zb * v, gbh)
                o_ref[rows, h * HD:(h + 1) * HD] = _rot_bwd(drq, cos2, sin2).astype(o_ref.dtype)
                o_ref[rows, w + h * HD:w + (h + 1) * HD] = (_rot_bwd(drk, cos2, sin2) * scale).astype(o_ref.dtype)
                o_ref[rows, 2 * w + h * HD:2 * w + (h + 1) * HD] = dv.astype(o_ref.dtype)

    st = pl.BlockSpec((r, HD, w), lambda i: (i, 0, 0))
    (out,), rid = _pcall(
        body, [qkv, qkv, qkv, da, sf, sb, gf, gb, tb["dm"], tb["dmt"], tb["xif"], tb["xib"], tb["zf"], tb["zb"],
               tb["cos2"], tb["sin2"]], riders, grid=(t // tm,),
        in_specs=[pl.BlockSpec((tm, w), lambda i: (i, 0)), pl.BlockSpec((tm, w), lambda i: (i, 1)),
                  pl.BlockSpec((tm, w), lambda i: (i, 2)), pl.BlockSpec((tm, w), lambda i: (i, 0)),
                  st, st, st, st, _full((HEADS, CH, CH)), _full((HEADS, CH, CH)),
                  _full((CH, w)), _full((CH, w)), _full((CH, w)), _full((CH, w)),
                  pl.BlockSpec((tm, HD), lambda i: (i, 0)), pl.BlockSpec((tm, HD), lambda i: (i, 0))],
        out_specs=[pl.BlockSpec((tm, 3 * w), lambda i: (i, 0))],
        out_shape=[S((t, 3 * w), ACT_DTYPE)], name=name, sem=("parallel",))
    return out, rid


CONV_TM = 256
CONV_SUB = 64
A_COL = (2 * GM_W + 4 * RET_W) // CV_W
G_COL = A_COL + 1


def _halo_specs(t, tm, col):
    nb16 = t // HALO
    per = tm // HALO
    return [pl.BlockSpec((tm, CV_W), lambda i: (i, col)),
            pl.BlockSpec((HALO, CV_W), lambda i: (jnp.maximum(i * per - 1, 0), col)),
            pl.BlockSpec((HALO, CV_W), lambda i: (jnp.minimum((i + 1) * per, nb16 - 1), col))]


def _fill_padded(dst_ref, prev, main, nxt, tm, i, nb):
    dst_ref[0:HALO, :] = jnp.where(i > 0, prev, 0.0)
    dst_ref[HALO:HALO + tm, :] = main
    dst_ref[HALO + tm:2 * HALO + tm, :] = jnp.where(i < nb - 1, nxt, 0.0)


SUBLANES = 8


def _fill_shifted(sh_ref, src_ref, tm):
    n = tm + 2 * HALO - SUBLANES
    for b in range(SUBLANES):
        sh_ref[b, 0:n, :] = src_ref[pl.ds(b, n), :]


def _tap(sh_ref, off, rows):
    return sh_ref[off % SUBLANES, pl.ds(off - off % SUBLANES, rows), :]


def _conv_fwd(proj, cw, cb, ln_g, ln_b, name, riders=()):
    t = proj.shape[0]
    tm = _row_tile(t, CONV_TM)
    nb = t // tm

    def body(a_ref, ap_ref, an_ref, g_ref, gp_ref, gn_ref, w_ref, b_ref, lg_ref, lb_ref, y_ref, hc_ref,
             hp_ref, sh_ref):
        i = pl.program_id(0)
        _fill_padded(hp_ref, ap_ref[...] * _sigmoid(gp_ref[...]), a_ref[...] * _sigmoid(g_ref[...]),
                     an_ref[...] * _sigmoid(gn_ref[...]), tm, i, nb)
        _fill_shifted(sh_ref, hp_ref, tm)
        for sb in range(tm // CONV_SUB):
            acc = jnp.zeros((CONV_SUB, CV_W), F32) + b_ref[...]
            for k in range(KCONV):
                acc = acc + w_ref[k:k + 1, :] * _tap(sh_ref, sb * CONV_SUB + k + 1, CONV_SUB)
            rows = slice(sb * CONV_SUB, (sb + 1) * CONV_SUB)
            hc_ref[rows, :] = acc
            o, _ = _standardize(acc)
            z = o * lg_ref[...] + lb_ref[...]
            y_ref[rows, :] = (z * _sigmoid(z)).astype(y_ref.dtype)

    (y, hc), rid = _pcall(
        body, [proj, proj, proj, proj, proj, proj, cw, cb, ln_g, ln_b], riders, grid=(nb,),
        in_specs=_halo_specs(t, tm, A_COL) + _halo_specs(t, tm, G_COL)
        + [_full((32, CV_W)), _full((1, CV_W)), _full((1, CV_W)), _full((1, CV_W))],
        out_specs=[pl.BlockSpec((tm, CV_W), lambda i: (i, 0))] * 2,
        out_shape=[S((t, CV_W), ACT_DTYPE), S((t, CV_W), F32)], name=name, sem=("parallel",),
        scratch_shapes=[pltpu.VMEM((tm + 2 * HALO, CV_W), F32), pltpu.VMEM((SUBLANES, tm + 2 * HALO, CV_W), F32)])
    return y, hc, rid


def _conv_bwd(proj, dy, hc, cw, ln_g, ln_b, name, riders=()):
    t = proj.shape[0]
    tm = _row_tile(t, CONV_TM)
    nb = t // tm

    def body(a_ref, ap_ref, an_ref, g_ref, gp_ref, gn_ref, dy_ref, dyp_ref, dyn_ref, hc_ref, hcp_ref, hcn_ref,
             w_ref, lg_ref, lb_ref, d_ref, dw_ref, dcb_ref, dlg_ref, dlb_ref, hp_ref, dhp_ref, dwacc_ref,
             sh_ref, dsh_ref):
        i = pl.program_id(0)
        first = i == 0

        def dhc_of(dyv, hcv):
            o, r = _standardize(hcv)
            z = o * lg_ref[...] + lb_ref[...]
            s = _sigmoid(z)
            dz = dyv * (s * (1.0 + z * (1.0 - s)))
            return _standardize_bwd(dz * lg_ref[...], o, r), dz, o

        dhc, dz, o = dhc_of(dy_ref[...], hc_ref[...])
        _acc_out(dlg_ref, jnp.sum(dz * o, axis=0, keepdims=True), first)
        _acc_out(dlb_ref, jnp.sum(dz, axis=0, keepdims=True), first)
        _acc_out(dcb_ref, jnp.sum(dhc, axis=0, keepdims=True), first)
        _fill_padded(dhp_ref, dhc_of(dyp_ref[...], hcp_ref[...])[0], dhc, dhc_of(dyn_ref[...], hcn_ref[...])[0],
                     tm, i, nb)
        _fill_padded(hp_ref, ap_ref[...] * _sigmoid(gp_ref[...]), a_ref[...] * _sigmoid(g_ref[...]),
                     an_ref[...] * _sigmoid(gn_ref[...]), tm, i, nb)

        _fill_shifted(sh_ref, hp_ref, tm)
        _fill_shifted(dsh_ref, dhp_ref, tm)

        @pl.when(first)
        def _():
            dwacc_ref[...] = jnp.zeros_like(dwacc_ref)

        for sb in range(tm // CONV_SUB):
            base = sb * CONV_SUB
            dmain = dhp_ref[pl.ds(HALO + base, CONV_SUB), :]
            dh = jnp.zeros((CONV_SUB, CV_W), F32)
            for k in range(KCONV):
                dh = dh + w_ref[k:k + 1, :] * _tap(dsh_ref, base + 2 * HALO - 1 - k, CONV_SUB)
                prod = dmain * _tap(sh_ref, base + k + 1, CONV_SUB)
                dwacc_ref[k * 8:(k + 1) * 8, :] += jnp.sum(prod.reshape(CONV_SUB // 8, 8, CV_W), axis=0)
            rows = slice(base, base + CONV_SUB)
            s = _sigmoid(g_ref[rows, :])
            d_ref[rows, :CV_W] = (dh * s).astype(d_ref.dtype)
            d_ref[rows, CV_W:] = (dh * a_ref[rows, :] * (s * (1.0 - s))).astype(d_ref.dtype)

        @pl.when(i == nb - 1)
        def _():
            for k in range(KCONV):
                dw_ref[k:k + 1, :] = jnp.sum(dwacc_ref[k * 8:(k + 1) * 8, :], axis=0, keepdims=True)
            dw_ref[KCONV:32, :] = jnp.zeros((32 - KCONV, CV_W), F32)

    hs = [pl.BlockSpec((tm, CV_W), lambda i: (i, 0)),
          pl.BlockSpec((HALO, CV_W), lambda i: (jnp.maximum(i * (tm // HALO) - 1, 0), 0)),
          pl.BlockSpec((HALO, CV_W), lambda i: (jnp.minimum((i + 1) * (tm // HALO), t // HALO - 1), 0))]
    outs, rid = _pcall(
        body, [proj, proj, proj, proj, proj, proj, dy, dy, dy, hc, hc, hc, cw, ln_g, ln_b], riders, grid=(nb,),
        in_specs=_halo_specs(t, tm, A_COL) + _halo_specs(t, tm, G_COL) + hs + hs
        + [_full((32, CV_W)), _full((1, CV_W)), _full((1, CV_W))],
        out_specs=[pl.BlockSpec((tm, 2 * CV_W), lambda i: (i, 0)), _full((32, CV_W)), _full((1, CV_W)),
                   _full((1, CV_W)), _full((1, CV_W))],
        out_shape=[S((t, 2 * CV_W), ACT_DTYPE), S((32, CV_W), F32), S((1, CV_W), F32), S((1, CV_W), F32),
                   S((1, CV_W), F32)],
        name=name, sem=("arbitrary",),
        scratch_shapes=[pltpu.VMEM((tm + 2 * HALO, CV_W), F32), pltpu.VMEM((tm + 2 * HALO, CV_W), F32),
                        pltpu.VMEM((32 * 8, CV_W), F32), pltpu.VMEM((SUBLANES, tm + 2 * HALO, CV_W), F32),
                        pltpu.VMEM((SUBLANES, tm + 2 * HALO, CV_W), F32)])
    return (*outs, rid)


def _loss_head(x, g, target, name):
    t = x.shape[0]
    tm = _row_tile(t, 512)

    def body(x_ref, g_ref, t_ref, dx_ref, dg_ref, l_ref):
        first = pl.program_id(0) == 0
        xv = x_ref[...]
        r = _rms_r(xv)
        e = xv * r * g_ref[...] - t_ref[...]
        dx, dgrow = _rms_bwd(e * (1.0 / D), xv, r, g_ref[...])
        dx_ref[...] = dx
        _acc_out(dg_ref, jnp.sum(dgrow, axis=0, keepdims=True), first)
        part = 0.5 * jnp.sum(jnp.mean(e * e, axis=-1, keepdims=True), axis=0, keepdims=True)
        _acc_out(l_ref, jnp.broadcast_to(part, (8, 128)), first)

    return pl.pallas_call(
        body, grid=(t // tm,),
        in_specs=[pl.BlockSpec((tm, D), lambda i: (i, 0)), _full((1, D)), pl.BlockSpec((tm, D), lambda i: (i, 0))],
        out_specs=[pl.BlockSpec((tm, D), lambda i: (i, 0)), _full((1, D)), _full((8, 128))],
        out_shape=[S((t, D), F32), S((1, D), F32), S((8, 128), F32)], name=name,
        compiler_params=_cp("arbitrary"))(x, g, target)


def _as2d(a):
    return a.reshape(-1, a.shape[-1])


def _ew_tile(rows, cols, n_arrays):
    budget = VMEM_LIMIT // 2
    tr = rows
    while tr * cols * 4 * n_arrays * 2 > budget and tr % 16 == 0:
        tr //= 2
    assert rows % tr == 0
    return tr


def _adamw(w, g, m, v, name):
    shape = w.shape
    w2, g2, m2, v2 = _as2d(w), _as2d(g), _as2d(m), _as2d(v)
    rows, cols = w2.shape
    tr = _ew_tile(rows, cols, 7)

    def body(w_ref, g_ref, m_ref, v_ref, d_ref, nm_ref, nv_ref):
        gv = g_ref[...]
        nm = ADAM_B1 * m_ref[...] + (1.0 - ADAM_B1) * gv
        nv = ADAM_B2 * v_ref[...] + (1.0 - ADAM_B2) * (gv * gv)
        m_hat = nm / (1.0 - ADAM_B1 ** ADAM_STEP)
        v_hat = nv / (1.0 - ADAM_B2 ** ADAM_STEP)
        d_ref[...] = -ADAM_LR * (m_hat / (jnp.sqrt(v_hat) + ADAM_EPS) + ADAM_WD * w_ref[...])
        nm_ref[...] = nm
        nv_ref[...] = nv

    spec = pl.BlockSpec((tr, cols), lambda i: (i, 0))
    outs = pl.pallas_call(body, grid=(rows // tr,), in_specs=[spec] * 4, out_specs=[spec] * 3,
                          out_shape=[S((rows, cols), F32)] * 3, name=name,
                          compiler_params=_cp("parallel"))(w2, g2, m2, v2)
    return tuple(o.reshape(shape) for o in outs)


BIG = (("w_in", "col"), ("w_out", "row"), ("w_ffn_in", "col"), ("w_ffn_out", "row"))
NBIG = len(BIG)


def _cast_to_gathered(w, l, me, name):
    _, r_, c_ = w.shape
    tr = _ew_tile(r_, c_, 2)

    def body(me_ref, w_ref, o_ref):
        o_ref[...] = w_ref[...].astype(o_ref.dtype)

    gs = pltpu.PrefetchScalarGridSpec(
        num_scalar_prefetch=1, grid=(r_ // tr,),
        in_specs=[pl.BlockSpec((None, tr, c_), lambda i, s: (l, i, 0))],
        out_specs=pl.BlockSpec((None, tr, c_), lambda i, s: (s[0], i, 0)))
    out = pl.pallas_call(body, grid_spec=gs, out_shape=S((N_CHIPS, r_, c_), MXU_DTYPE), name=name,
                         compiler_params=_cp("parallel"))(me.reshape(1), w)
    return out.reshape(N_CHIPS, 2, r_ // 2, c_)


def _all_gather(bufs, name, per_core=False):
    n = len(bufs)

    def body(*refs):
        i_refs, o_refs = refs[:n], refs[n:2 * n]
        isend, irecv, dsend, drecv, osend, orecv = refs[2 * n:]
        pos = _mesh_pos()
        x, y, c, me, _, _ = pos
        ici = _rider_copies("ici", i_refs, o_refs, isend, irecv, pos)
        d2d = _rider_copies("d2d", o_refs, o_refs, dsend, drecv, pos)
        own = []
        if per_core:
            for b in range(n):
                own.append(tuple(pltpu.make_async_remote_copy(
                    src_ref=s_, dst_ref=d_, send_sem=osend.at[b], recv_sem=orecv.at[b],
                    device_id=(x, y, 1 - c), device_id_type=MESH)
                    for s_, d_ in ((i_refs[b].at[me, c], o_refs[b].at[me, c]),
                                   (o_refs[b].at[me, 1 - c], o_refs[b].at[me, 1 - c]))))
        for cp, _ in ici + own:
            cp.start()
        for (_, land), (fwd, _) in zip(ici, d2d):
            land.wait_recv()
            fwd.start()
        for _, land in d2d + own:
            land.wait_recv()
        for cp, _ in ici + d2d + own:
            cp.wait_send()

    return pl.pallas_call(
        body, in_specs=[ANY] * n, out_specs=[ANY] * n, out_shape=[S(a.shape, a.dtype) for a in bufs],
        input_output_aliases={w: w for w in range(n)}, name=name,
        scratch_shapes=[pltpu.SemaphoreType.DMA((n, 3))] * 4 + [pltpu.SemaphoreType.DMA((n,))] * 2)(*bufs)


def _pair_exchange(grads, name):
    n = len(grads)

    def body(*refs):
        g_refs, theirs = refs[:n], refs[n:2 * n]
        send, recv = refs[2 * n:]
        x, y, c, *_ = _mesh_pos()
        cps = []
        for w in range(n):
            cp = pltpu.make_async_remote_copy(
                src_ref=g_refs[w].at[:, 1 - c], dst_ref=theirs[w], send_sem=send.at[w], recv_sem=recv.at[w],
                device_id=(x, y, 1 - c), device_id_type=MESH)
            cp.start()
            cps.append(cp)
        for cp in cps:
            cp.wait()

    return pl.pallas_call(
        body, in_specs=[ANY] * n, out_specs=[ANY] * n,
        out_shape=[S(a.shape[:1] + a.shape[2:], a.dtype) for a in grads], name=name,
        scratch_shapes=[pltpu.SemaphoreType.DMA((n,))] * 2)(*grads)


def _pair_sum(g, theirs, core, name):
    _, _, rh, c_ = g.shape
    tr = _ew_tile(rh, c_, 2)

    def body(s_ref, g_ref, t_ref, o_ref):
        o_ref[...] = (g_ref[...].astype(F32) + t_ref[...].astype(F32)).astype(o_ref.dtype)

    blk = pl.BlockSpec((None, tr, c_), lambda j, i, s: (j, i, 0))
    gs = pltpu.PrefetchScalarGridSpec(
        num_scalar_prefetch=1, grid=(N_CHIPS, rh // tr),
        in_specs=[pl.BlockSpec((None, None, tr, c_), lambda j, i, s: (j, s[0], i, 0)), blk], out_specs=blk)
    return pl.pallas_call(body, grid_spec=gs, out_shape=S(theirs.shape, theirs.dtype), name=name,
                          compiler_params=_cp("parallel", "parallel"))(core.reshape(1), g, theirs)


def _chip_sum(q, got, l, me, core, into, name):
    _, rh, c_ = got.shape
    tr = _ew_tile(rh, c_, 4)

    def body(s_ref, q_ref, g0_ref, g1_ref, g2_ref, o_ref):
        acc = q_ref[...].astype(F32)
        for r in (g0_ref, g1_ref, g2_ref):
            acc = acc + r[...].astype(F32)
        o_ref[...] = acc

    in_specs = [pl.BlockSpec((None, tr, c_), lambda i, s: (s[0], i, 0))] + [
        pl.BlockSpec((None, tr, c_), functools.partial(lambda k, i, s: (k, i, 0), k)) for k in range(3)]
    return _call_into(
        body, into, in_specs, [jnp.stack([me, core]), q, got, got, got], n_prefetch=1, grid=(rh // tr,),
        out_specs=pl.BlockSpec((None, None, tr, c_), lambda i, s: (l, s[1], i, 0)),
        out_shape=S((DEPTH, 2, rh, c_), F32), name=name, compiler_params=_cp("parallel"))


def _pair_gather(gs4):
    def body(*refs):
        i_refs, o_refs = refs[:NBIG], refs[NBIG:2 * NBIG]
        send, recv = refs[2 * NBIG:]
        x, y, c, *_ = _mesh_pos()
        cps = []
        for w in range(NBIG):
            cp = pltpu.make_async_remote_copy(
                src_ref=i_refs[w].at[:, c], dst_ref=o_refs[w].at[:, c], send_sem=send.at[w], recv_sem=recv.at[w],
                device_id=(x, y, 1 - c), device_id_type=MESH)
            cp.start()
            cps.append(cp)
        for cp in cps:
            cp.wait()

    outs = pl.pallas_call(
        body, in_specs=[ANY] * NBIG, out_specs=[ANY] * NBIG, out_shape=[S(a.shape, a.dtype) for a in gs4],
        input_output_aliases={w: w for w in range(NBIG)}, name="grad_pair_gather",
        scratch_shapes=[pltpu.SemaphoreType.DMA((NBIG,))] * 2)(*gs4)
    return [o.reshape(o.shape[0], 2 * o.shape[2], o.shape[3]) for o in outs]


def _all_reduce_small(p, me, core, name):
    rows = p.shape[0]

    def place(s_ref, p_ref, o_ref):
        o_ref[...] = p_ref[...]

    gs = pltpu.PrefetchScalarGridSpec(
        num_scalar_prefetch=1, grid=(1,), in_specs=[pl.BlockSpec((rows, 128), lambda i, s: (0, 0))],
        out_specs=pl.BlockSpec((None, None, rows, 128), lambda i, s: (s[0], s[1], 0, 0)))
    mine = pl.pallas_call(place, grid_spec=gs, out_shape=S((N_CHIPS, 2, rows, 128), F32), name=name + "_place",
                          compiler_params=_cp("arbitrary"))(jnp.stack([me, core]), p)
    parts = _all_gather([mine], name + "_gather", per_core=True)[0]

    def total(g_ref, o_ref):
        acc = g_ref[0, 0]
        for j in range(N_CHIPS):
            for c in range(2):
                if (j, c) != (0, 0):
                    acc = acc + g_ref[j, c]
        o_ref[...] = acc

    vm = pl.BlockSpec(memory_space=pltpu.VMEM)
    return pl.pallas_call(total, in_specs=[vm], out_specs=vm, out_shape=S((rows, 128), F32), name=name + "_sum",
                          compiler_params=pltpu.CompilerParams(vmem_limit_bytes=VMEM_LIMIT))(parts)


PACK_UNIT = 8 * 128


def _pack(arrs):
    parts = []
    for a in arrs:
        flat = a.reshape(-1)
        pad = (-flat.shape[0]) % PACK_UNIT
        parts.append(jnp.pad(flat, (0, pad)).reshape(-1, 128))
    return jnp.concatenate(parts, axis=0)


def _unpack(buf, shapes):
    outs, row = [], 0
    for shp in shapes:
        n = int(np.prod(shp))
        rows = -(-n // PACK_UNIT) * 8
        outs.append(buf[row:row + rows].reshape(-1)[:n].reshape(shp))
        row += rows
    return outs


SMALL = ("norm1_g", "gm_ln_g", "gm_ln_b", "gm_ws", "gm_bs", "conv_w", "conv_b", "conv_ln_g", "conv_ln_b",
         "norm2_g", "final_g")
WEIGHTS = ("norm1_g", "w_in", "gm_ln_g", "gm_ln_b", "gm_ws", "gm_bs", "conv_w", "conv_b", "conv_ln_g",
           "conv_ln_b", "w_out", "norm2_g", "w_ffn_in", "w_ffn_out", "final_g")


def kernel(x, norm1_g, w_in, gm_ln_g, gm_ln_b, gm_ws, gm_bs, conv_w, conv_b, conv_ln_g, conv_ln_b, w_out, norm2_g, w_ffn_in, w_ffn_out, final_g, loss_target, m_norm1_g, m_w_in, m_gm_ln_g, m_gm_ln_b, m_gm_ws, m_gm_bs, m_conv_w, m_conv_b, m_conv_ln_g, m_conv_ln_b, m_w_out, m_norm2_g, m_w_ffn_in, m_w_ffn_out, m_final_g, v_norm1_g, v_w_in, v_gm_ln_g, v_gm_ln_b, v_gm_ws, v_gm_bs, v_conv_w, v_conv_b, v_conv_ln_g, v_conv_ln_b, v_w_out, v_norm2_g, v_w_ffn_in, v_w_ffn_out, v_final_g):
    given = dict(locals())
    t = x.shape[1]
    xc = x.reshape(t, D)
    target = loss_target.reshape(t, D)
    me = 2 * lax.axis_index("x") + lax.axis_index("y")
    core = lax.axis_index("c")
    tb = _tables(t)

    me = me.astype(jnp.int32)
    core = core.astype(jnp.int32)
    names = [n for n, _ in BIG]
    kinds = dict(BIG)
    gathered = [{n: _cast_to_gathered(given[n], l, me, f"cast_{n}{l}") for n in names} for l in range(DEPTH)]
    gathered[0]["w_in"] = _all_gather([gathered[0]["w_in"]], "all_gather_w_in0")[0]

    def weight(l, n):
        b = gathered[l][n]
        r_, c_ = 2 * b.shape[2], b.shape[3]
        return b.reshape(N_CHIPS, r_, c_) if kinds[n] == "col" else b.reshape(N_CHIPS * r_, c_)

    cshard = CV_W // N_CHIPS
    placed = lax.dynamic_update_slice(jnp.zeros((DEPTH, KCONV, CV_W), F32),
                                      conv_w * (core == 0).astype(F32), (0, 0, me * cshard))
    conv_w_full = _unpack(_all_reduce_small(_pack([placed]), me, core, "gather_conv_w"), [(DEPTH, KCONV, CV_W)])[0]
    cw32 = jnp.pad(conv_w_full, ((0, 0), (0, 32 - KCONV), (0, 0)))

    def row(a, l):
        return a[l].reshape(1, -1)

    saved = []
    early = ["w_in", "w_out", "w_ffn_in"]
    for l in range(DEPTH):
        cur = gathered[l]
        nxt = gathered[l + 1] if l + 1 < DEPTH else None
        sv = {"x": xc}
        bias = jnp.repeat(gm_bs[l].T, GM_W // GM_HEADS, axis=1)
        first = ["w_ffn_in"] if l == 0 else ["w_ffn_out"]
        late = ["w_out", "w_ffn_out"]
        proj, h1t, qkv, rid = _norm_mm(xc, row(norm1_g, l), weight(l, "w_in"), F32, f"in_proj{l}", 512,
                                       [("ici" if l == 0 else "d2d", [cur[n] for n in first])],
                                       (tb["cos2"], tb["sin2"]))
        cur.update(zip(first, rid))
        y_gm, rid = _gm_fwd(proj, row(gm_ln_g, l), row(gm_ln_b, l), gm_ws[l], bias, f"gm_fwd{l}",
                            [("d2d", [cur[n] for n in first])] if l == 0 else ())
        cur.update(zip(first, rid))
        sf, sb = _ret_scan(qkv, 1, qkv, 2, tb["zf"], tb["zb"], tb["gcf"], tb["gcb"], f"ret_state{l}")
        a, y_ret, rid = _ret_out(proj, qkv, sf, sb, tb, f"ret_out{l}",
                                 [("ici", [cur[n] for n in late])] if l == 0 else ())
        cur.update(zip(late, rid))
        y_cv, hc, rid = _conv_fwd(proj, cw32[l], row(conv_b, l), row(conv_ln_g, l), row(conv_ln_b, l),
                                  f"conv_fwd{l}", [("d2d", [cur[n] for n in late])] if l == 0 else ())
        cur.update(zip(late, rid))
        x_mid = _parts_mm_res([y_gm, y_ret, y_cv], weight(l, "w_out"), xc, f"out_proj{l}")
        ff, h2t, _, rid = _norm_mm(x_mid, row(norm2_g, l), weight(l, "w_ffn_in"), ACT_DTYPE, f"ffn_in{l}", 512,
                                   [("ici", [nxt[n] for n in early])] if nxt else ())
        if nxt:
            nxt.update(zip(early, rid))
        xc, rid = _swiglu_mm_res(ff, weight(l, "w_ffn_out"), x_mid, f"ffn_out{l}",
                                 [("d2d", [nxt[n] for n in early]), ("ici", [nxt["w_ffn_out"]])] if nxt else ())
        if nxt:
            nxt.update(zip(early + ["w_ffn_out"], rid))
        sv.update(bias=bias, proj=proj, qkv=qkv, h1t=h1t, h2t=h2t, y_gm=y_gm, sf=sf, sb=sb, a=a, y_ret=y_ret,
                  y_cv=y_cv,
                  hc=hc, x_mid=x_mid,
                  ff=ff)
        saved.append(sv)

    dx, d_final_g, lpart = _loss_head(xc, final_g.reshape(1, D), target, "loss_head")

    small_g = {n: [None] * DEPTH for n in SMALL}
    qs = [{} for _ in range(DEPTH)]
    got = [{} for _ in range(DEPTH)]
    ffn_w, mix_w = ["w_ffn_out", "w_ffn_in"], ["w_out", "w_in"]

    def halves(big_g, group):
        return [big_g[n].reshape(N_CHIPS, 2, given[n].shape[1] // 2, given[n].shape[2]) for n in group]

    def pair_sums(l, group, g4, theirs):
        qs[l].update({n: _pair_sum(g, th, core, f"pair_sum_{n}{l}") for n, g, th in zip(group, g4, theirs)})
        return [qs[l][n] for n in group]

    for l in reversed(range(DEPTH)):
        sv = saved[l]
        proj = sv["proj"]
        big_g = {}
        dff = _dx_swiglu(dx, weight(l, "w_ffn_out"), sv["ff"], f"ffn_out_dx{l}")
        big_g["w_ffn_out"] = _dw_swiglu(sv["ff"], dx, f"ffn_out_dw{l}")
        dx_mid, dg2, _ = _dx_norm([dff], weight(l, "w_ffn_in"), sv["x_mid"], row(norm2_g, l), dx,
                                  f"ffn_in_dx{l}", 512)
        big_g["w_ffn_in"] = _dw_norm_cols(sv["h2t"], dff, w_ffn_in.shape[2], f"ffn_in_dw{l}")
        g4 = halves(big_g, ffn_w)
        (dy_gm, da, d_g, dy_cv), theirs = _out_proj_dx(dx_mid, weight(l, "w_out"), sv["a"], proj,
                                                       f"out_proj_dx{l}", [("pairx", g4)])
        q_ffn = pair_sums(l, ffn_w, g4, theirs)
        big_g["w_out"] = _dw_parts([sv["y_gm"], sv["y_ret"], sv["y_cv"]], dx_mid, f"out_proj_dw{l}")
        d_cv, dcw, dcb, dclg, dclb, rid = _conv_bwd(proj, dy_cv, sv["hc"], cw32[l], row(conv_ln_g, l),
                                                    row(conv_ln_b, l), f"conv_bwd{l}", [("scatter", q_ffn[:1])])
        got[l].update(zip(ffn_w[:1], rid))
        gb_, gf_ = _ret_scan(sv["qkv"], 0, da, 0, tb["xib"], tb["xif"], tb["gcb"], tb["gcf"], f"ret_bwd_state{l}")
        d_qkv, rid = _ret_bwd_main(sv["qkv"], da, sv["sf"], sv["sb"], gf_, gb_, tb, f"ret_bwd_main{l}",
                                   [("scatter", q_ffn[1:])])
        got[l].update(zip(ffn_w[1:], rid))
        d_gm, dws, dbs, dglg, dglb = _gm_bwd(proj, dy_gm, row(gm_ln_g, l), row(gm_ln_b, l), gm_ws[l],
                                             jnp.swapaxes(gm_ws[l], 1, 2), sv["bias"], f"gm_bwd{l}")
        dparts = [d_gm, d_qkv, d_g, d_cv]
        big_g["w_in"] = _dw_norm_parts(sv["h1t"], dparts, w_in.shape[2], f"in_proj_dw{l}")
        g4 = halves(big_g, mix_w)
        q_mix = pair_sums(l, mix_w, g4, _pair_exchange(g4, f"grad_pair_exchange_mix{l}"))
        dx, dg1, rid = _dx_norm(dparts, weight(l, "w_in"), sv["x"], row(norm1_g, l), dx_mid, f"in_proj_dx{l}", 512,
                                [("scatter", q_mix)])
        got[l].update(zip(mix_w, rid))
        for n, val in (("norm1_g", dg1[0]), ("gm_ln_g", dglg[0]), ("gm_ln_b", dglb[0]), ("gm_ws", dws),
                       ("gm_bs", dbs[:, :GM_HEADS].T), ("conv_w", dcw[:KCONV]), ("conv_b", dcb[0]),
                       ("conv_ln_g", dclg[0]), ("conv_ln_b", dclb[0]), ("norm2_g", dg2[0])):
            small_g[n][l] = val

    small_shapes = [given[n].shape if n != "conv_w" else (DEPTH, KCONV, CV_W) for n in SMALL]
    partials = [d_final_g[0] if n == "final_g" else jnp.stack(small_g[n]) for n in SMALL]
    summed = _unpack(_all_reduce_small(_pack(partials + [lpart]), me, core, "all_reduce_small_grads"),
                     small_shapes + [lpart.shape])
    loss = summed[-1][0, 0]
    reduced = dict(zip(SMALL, summed))
    reduced["conv_w"] = lax.dynamic_slice(reduced["conv_w"], (0, 0, me * cshard), (DEPTH, KCONV, cshard))

    halves = [None] * NBIG
    for l in reversed(range(DEPTH)):
        halves = [_chip_sum(qs[l][n], got[l][n], l, me, core, h, f"chip_sum_{n}{l}") for n, h in zip(names, halves)]
    grads = dict(zip(names, _pair_gather(halves)))
    grads.update(reduced)

    delta, new_m, new_v = {}, {}, {}
    for n, _ in BIG:
        delta[n], new_m[n], new_v[n] = _adamw(given[n], grads[n], given["m_" + n], given["v_" + n], f"adamw_{n}")
    shapes = [given[n].shape for n in SMALL]
    packed = [_pack([src[n] if src is grads else src[p + n] for n in SMALL])
              for src, p in ((given, ""), (grads, ""), (given, "m_"), (given, "v_"))]
    outs = _adamw(*packed, "adamw_small")
    for dst, buf in zip((delta, new_m, new_v), outs):
        dst.update(zip(SMALL, _unpack(buf, shapes)))

    return (loss, dx.reshape(1, t, D), *[grads[n] for n in WEIGHTS], *[delta[n] for n in WEIGHTS],
            *[new_m[n] for n in WEIGHTS], *[new_v[n] for n in WEIGHTS])
```

```python
import functools
import math

import numpy as np
import jax
import jax.numpy as jnp
from jax import lax
from jax.experimental import pallas as pl
from jax.experimental.pallas import tpu as pltpu

F32 = jnp.float32
BF16 = jnp.bfloat16
MXU_DTYPE = BF16
ACT_DTYPE = BF16
S = jax.ShapeDtypeStruct

D = 1024
DEPTH = 2
GM_W = 256
GM_HEADS = 4
RET_W = 512
HEADS = 4
HD = 128
CV_W = 256
KCONV = 31
IN_W = 2 * GM_W + 4 * RET_W + 2 * CV_W
FFN_H = 2816
CH = 128
ROPE_BASE = 10000.0
EPS = 1e-6
N_CHIPS = 4
N_DEV = 8
HALO = 16

ADAM_LR = 0.001
ADAM_B1 = 0.9
ADAM_B2 = 0.999
ADAM_EPS = 1e-08
ADAM_WD = 0.01
ADAM_STEP = 10

VMEM_LIMIT = 52 * 1024 * 1024
MESH = pl.DeviceIdType.MESH


def _cp(*sem, vmem=VMEM_LIMIT):
    return pltpu.CompilerParams(dimension_semantics=tuple(sem), vmem_limit_bytes=vmem)


def _mx(a):
    return a.astype(MXU_DTYPE)


def _dot(a, b):
    return jnp.dot(_mx(a), _mx(b), preferred_element_type=F32)


def _dot_nt(a, b):
    return lax.dot_general(_mx(a), _mx(b), (((1,), (1,)), ((), ())), preferred_element_type=F32)


def _dot_tn(a, b):
    return lax.dot_general(_mx(a), _mx(b), (((0,), (0,)), ((), ())), preferred_element_type=F32)


def _sigmoid(x):
    return 1.0 / (1.0 + jnp.exp(-x))


def _gelu(x):
    return 0.5 * x * (1.0 + lax.erf(x * (1.0 / math.sqrt(2.0))))


def _gelu_grad(x):
    return 0.5 * (1.0 + lax.erf(x * (1.0 / math.sqrt(2.0)))) + x * jnp.exp(-0.5 * x * x) * (1.0 / math.sqrt(2.0 * math.pi))


def _rms_r(x):
    return lax.rsqrt(jnp.mean(x * x, axis=-1, keepdims=True) + EPS)


def _rms_bwd(dh, x, r, g):
    u = dh * g
    dx = r * u - x * (r * r * r) * jnp.mean(u * x, axis=-1, keepdims=True)
    return dx, dh * x * r


def _standardize(a):
    mu = jnp.mean(a, axis=-1, keepdims=True)
    d = a - mu
    r = lax.rsqrt(jnp.mean(d * d, axis=-1, keepdims=True) + EPS)
    return d * r, r


def _standardize_bwd(do, o, r):
    return r * (do - jnp.mean(do, axis=-1, keepdims=True) - o * jnp.mean(do * o, axis=-1, keepdims=True))


def _acc_out(ref, val, first):
    @pl.when(first)
    def _():
        ref[...] = val

    @pl.when(jnp.logical_not(first))
    def _():
        ref[...] += val


def _row_tile(t, pref):
    tm = min(t, pref)
    assert t % tm == 0, (t, tm)
    return tm


def _segments(part_widths, shard_w):
    bounds = {0}
    off = 0
    for w in part_widths:
        off += w
        bounds.add(off)
    total = off
    for j in range(1, total // shard_w + 1):
        bounds.add(j * shard_w)
    bounds = sorted(bounds)
    starts = np.cumsum([0] + list(part_widths))
    segs = []
    for a, b in zip(bounds[:-1], bounds[1:]):
        p = int(np.searchsorted(starts, a, side="right") - 1)
        segs.append((p, a - int(starts[p]), a // shard_w, a % shard_w, b - a))
    return segs


ANY = pl.BlockSpec(memory_space=pl.ANY)


def _mesh_pos():
    x, y, c = lax.axis_index("x"), lax.axis_index("y"), lax.axis_index("c")
    chips = [(1 - x, y), (x, 1 - y), (1 - x, 1 - y)]
    return x, y, c, 2 * x + y, chips, [2 * cx + cy for cx, cy in chips]


def _rider_copies(kind, i_refs, o_refs, send, recv, pos):
    x, y, c, me, chips, cj = pos
    out = []
    for b, (i_ref, o_ref) in enumerate(zip(i_refs, o_refs)):
        for k in range(1 if kind == "pairx" else 3):
            if kind == "ici":
                src, dst, land, dev = i_ref.at[me, c], o_ref.at[me, c], o_ref.at[cj[k], c], (*chips[k], c)
            elif kind == "d2d":
                src, dst, land, dev = i_ref.at[cj[k], c], o_ref.at[cj[k], c], o_ref.at[cj[k], 1 - c], (x, y, 1 - c)
            elif kind == "pairx":
                src, dst, land, dev = i_ref.at[:, 1 - c], o_ref, o_ref, (x, y, 1 - c)
            else:
                src, dst, land, dev = i_ref.at[cj[k]], o_ref.at[k], o_ref.at[k], (*chips[k], c)
            out.append(tuple(pltpu.make_async_remote_copy(
                src_ref=s_, dst_ref=d_, send_sem=send.at[b, k], recv_sem=recv.at[b, k],
                device_id=dev, device_id_type=MESH) for s_, d_ in ((src, dst), (land, land))))
    return out


def _rider_out_shape(kind, a):
    if kind == "scatter":
        return S((3,) + a.shape[1:], a.dtype)
    if kind == "pairx":
        return S(a.shape[:1] + a.shape[2:], a.dtype)
    return S(a.shape, a.dtype)


def _pcall(body, args, riders, *, grid, in_specs, out_specs, out_shape, name, sem, scratch_shapes=()):
    outs = list(out_shape)
    if not riders:
        res = pl.pallas_call(body, grid=grid, in_specs=in_specs, out_specs=out_specs, out_shape=outs, name=name,
                             scratch_shapes=list(scratch_shapes), compiler_params=_cp(*sem))(*args)
        return res, []
    r_in = [a for _, bufs in riders for a in bufs]
    r_out = [_rider_out_shape(kind, a) for kind, bufs in riders for a in bufs]
    n_in, n_out, n_scr, n_r = len(args), len(outs), len(scratch_shapes), len(r_in)
    aliases, idx = {}, 0
    for kind, bufs in riders:
        for _ in bufs:
            if kind in ("ici", "d2d"):
                aliases[n_in + idx] = n_out + idx
            idx += 1
    sems = [pltpu.SemaphoreType.DMA((len(bufs), 3)) for _, bufs in riders for _ in range(2)]

    def wrapped(*refs):
        a, ri = refs[:n_in], refs[n_in:n_in + n_r]
        o, ro = refs[n_in + n_r:n_in + n_r + n_out], refs[n_in + n_r + n_out:n_in + 2 * n_r + n_out]
        scr = refs[n_in + 2 * n_r + n_out:n_in + 2 * n_r + n_out + n_scr]
        sm = refs[n_in + 2 * n_r + n_out + n_scr:]
        pos = _mesh_pos()
        copies, off = [], 0
        for r, (kind, bufs) in enumerate(riders):
            copies += _rider_copies(kind, ri[off:off + len(bufs)], ro[off:off + len(bufs)], sm[2 * r], sm[2 * r + 1], pos)
            off += len(bufs)
        ids = [pl.program_id(d) for d in range(len(grid))]
        first = functools.reduce(jnp.logical_and, [i == 0 for i in ids])
        last = functools.reduce(jnp.logical_and, [i == n - 1 for i, n in zip(ids, grid)])

        @pl.when(first)
        def _():
            for cp, _ in copies:
                cp.start()

        body(*a, *o, *scr)

        @pl.when(last)
        def _():
            for cp, land in copies:
                land.wait_recv()
                cp.wait_send()

    res = pl.pallas_call(
        wrapped, grid=grid, in_specs=list(in_specs) + [ANY] * n_r, out_specs=list(out_specs) + [ANY] * n_r,
        out_shape=outs + r_out, input_output_aliases=aliases, name=name,
        scratch_shapes=list(scratch_shapes) + sems, compiler_params=_cp(*(("arbitrary",) * len(grid))))(*args, *r_in)
    return res[:n_out], res[n_out:]


def _wcol_spec(w):
    return pl.BlockSpec(w.shape, lambda *_: (0, 0, 0))


def _wrow_spec(w):
    return pl.BlockSpec(w.shape, lambda *_: (0, 0))


def _norm_mm(x, g, w, out_dtype, name, tm_pref, riders=(), rope=None):
    t = x.shape[0]
    nc = w.shape[2]
    tm = _row_tile(t, tm_pref)
    extra = list(rope) if rope else []

    qkv_w = 3 * RET_W

    def body(x_ref, g_ref, w_ref, *rest):
        o_ref, ht_ref = rest[len(extra)], rest[len(extra) + 1]
        xv = x_ref[...]
        hf = xv * _rms_r(xv) * g_ref[...]
        h = _mx(hf)
        for j in range(N_CHIPS):
            o_ref[:, j * nc:(j + 1) * nc] = jnp.dot(h, w_ref[j], preferred_element_type=F32).astype(o_ref.dtype)
        if rope:
            _rotate_qk(o_ref, rest[0][...], rest[1][...])
            rest[-1][...] = o_ref[:, Q_COL * RET_W:Q_COL * RET_W + qkv_w].astype(rest[-1].dtype)
        ht_ref[...] = hf.T.astype(ht_ref.dtype)

    outs, rid = _pcall(
        body, [x, g, w] + extra, riders, grid=(t // tm,),
        in_specs=[pl.BlockSpec((tm, D), lambda i: (i, 0)), pl.BlockSpec((1, D), lambda i: (0, 0)), _wcol_spec(w)]
        + [pl.BlockSpec((tm, HD), lambda i: (i, 0)) for _ in extra],
        out_specs=[pl.BlockSpec((tm, N_CHIPS * nc), lambda i: (i, 0)), pl.BlockSpec((D, tm), lambda i: (0, i))]
        + ([pl.BlockSpec((tm, qkv_w), lambda i: (i, 0))] if rope else []),
        out_shape=[S((t, N_CHIPS * nc), out_dtype), S((D, t), MXU_DTYPE)]
        + ([S((t, qkv_w), ACT_DTYPE)] if rope else []), name=name, sem=("parallel",))
    return (*outs, rid) if rope else (*outs, None, rid)


def _parts_mm_res(parts, w, res, name):
    t = res.shape[0]
    tm = _row_tile(t, 512)
    widths = [p.shape[1] for p in parts]
    offs = np.cumsum([0] + widths)
    n = len(parts)

    def body(*refs):
        p_refs, w_ref, r_ref, o_ref = refs[:n], refs[n], refs[n + 1], refs[n + 2]
        acc = r_ref[...]
        for p in range(n):
            acc = acc + _dot(p_refs[p][...], w_ref[int(offs[p]):int(offs[p + 1]), :])
        o_ref[...] = acc

    return pl.pallas_call(
        body, grid=(t // tm,),
        in_specs=[pl.BlockSpec((tm, wd), lambda i: (i, 0)) for wd in widths]
        + [_wrow_spec(w), pl.BlockSpec((tm, D), lambda i: (i, 0))],
        out_specs=pl.BlockSpec((tm, D), lambda i: (i, 0)),
        out_shape=S((t, D), F32), name=name, compiler_params=_cp("parallel"))(*parts, w, res)


def _swiglu(ff):
    gate = ff[:, :FFN_H].astype(F32)
    up = ff[:, FFN_H:].astype(F32)
    return gate * _sigmoid(gate) * up


def _swiglu_mm_res(ff, w, res, name, riders=()):
    t = res.shape[0]
    tm = _row_tile(t, 512)

    def body(f_ref, w_ref, r_ref, o_ref):
        o_ref[...] = r_ref[...] + _dot(_swiglu(f_ref[...]), w_ref[...])

    (out,), rid = _pcall(
        body, [ff, w, res], riders, grid=(t // tm,),
        in_specs=[pl.BlockSpec((tm, 2 * FFN_H), lambda i: (i, 0)), _wrow_spec(w),
                  pl.BlockSpec((tm, D), lambda i: (i, 0))],
        out_specs=[pl.BlockSpec((tm, D), lambda i: (i, 0))],
        out_shape=[S((t, D), F32)], name=name, sem=("parallel",))
    return out, rid


def _dx_norm(dparts, w, x, g, dres, name, tm_pref, riders=()):
    t = x.shape[0]
    nc = w.shape[2]
    tm = _row_tile(t, tm_pref)
    widths = [p.shape[1] for p in dparts]
    segs = _segments(widths, nc)
    n = len(dparts)

    def body(*refs):
        d_refs = refs[:n]
        w_ref, x_ref, g_ref, r_ref, dx_ref, dg_ref = refs[n:]
        dh = jnp.zeros((tm, D), F32)
        for (p, po, j, jo, wd) in segs:
            dh = dh + _dot_nt(d_refs[p][:, po:po + wd], w_ref[j, :, jo:jo + wd])
        xv = x_ref[...]
        dx, dgrow = _rms_bwd(dh, xv, _rms_r(xv), g_ref[...])
        dx_ref[...] = r_ref[...] + dx
        _acc_out(dg_ref, jnp.sum(dgrow, axis=0, keepdims=True), pl.program_id(0) == 0)

    (dx, dg), rid = _pcall(
        body, [*dparts, w, x, g, dres], riders, grid=(t // tm,),
        in_specs=[pl.BlockSpec((tm, wd), lambda i: (i, 0)) for wd in widths]
        + [_wcol_spec(w), pl.BlockSpec((tm, D), lambda i: (i, 0)),
           pl.BlockSpec((1, D), lambda i: (0, 0)), pl.BlockSpec((tm, D), lambda i: (i, 0))],
        out_specs=[pl.BlockSpec((tm, D), lambda i: (i, 0)), pl.BlockSpec((1, D), lambda i: (0, 0))],
        out_shape=[S((t, D), F32), S((1, D), F32)], name=name, sem=("arbitrary",))
    return dx, dg, rid


def _out_proj_dx(dy, w, a, proj, name, riders=()):
    t = dy.shape[0]
    tm = _row_tile(t, 512)
    wr = HEADS * HD

    def body(dy_ref, w_ref, a_ref, g_ref, dgm_ref, da_ref, dg_ref, dcv_ref):
        dyv = _mx(dy_ref[...])
        dgm_ref[...] = _dot_nt(dyv, w_ref[0:GM_W, :])
        dcv_ref[...] = _dot_nt(dyv, w_ref[GM_W + RET_W:, :])
        for h in range(HEADS):
            cols = slice(h * HD, (h + 1) * HD)
            dyr = _dot_nt(dyv, w_ref[GM_W + h * HD:GM_W + (h + 1) * HD, :])
            o, r = _standardize(a_ref[:, cols])
            gv = g_ref[:, cols]
            s = _sigmoid(gv)
            dg_ref[:, cols] = (dyr * o * (s * (1.0 + gv * (1.0 - s)))).astype(dg_ref.dtype)
            da_ref[:, cols] = _standardize_bwd(dyr * (gv * s), o, r).astype(da_ref.dtype)

    return _pcall(
        body, [dy, w, a, proj], riders, grid=(t // tm,),
        in_specs=[pl.BlockSpec((tm, D), lambda i: (i, 0)), _wrow_spec(w), pl.BlockSpec((tm, wr), lambda i: (i, 0)),
                  pl.BlockSpec((tm, wr), lambda i: (i, GATE_COL))],
        out_specs=[pl.BlockSpec((tm, GM_W), lambda i: (i, 0)), pl.BlockSpec((tm, wr), lambda i: (i, 0)),
                   pl.BlockSpec((tm, wr), lambda i: (i, 0)), pl.BlockSpec((tm, CV_W), lambda i: (i, 0))],
        out_shape=[S((t, GM_W), F32), S((t, wr), ACT_DTYPE), S((t, wr), ACT_DTYPE), S((t, CV_W), F32)],
        name=name, sem=("parallel",))


def _dx_swiglu(dy, w, ff, name):
    t = dy.shape[0]
    tm = _row_tile(t, 512)

    def body(dy_ref, w_ref, f_ref, o_ref):
        dact = _dot_nt(dy_ref[...], w_ref[...])
        gate = f_ref[:, :FFN_H].astype(F32)
        up = f_ref[:, FFN_H:].astype(F32)
        s = _sigmoid(gate)
        gs = gate * s
        o_ref[:, :FFN_H] = ((dact * up) * (s + gs - gs * s)).astype(o_ref.dtype)
        o_ref[:, FFN_H:] = (dact * gs).astype(o_ref.dtype)

    return pl.pallas_call(
        body, grid=(t // tm,),
        in_specs=[pl.BlockSpec((tm, D), lambda i: (i, 0)), _wrow_spec(w),
                  pl.BlockSpec((tm, 2 * FFN_H), lambda i: (i, 0))],
        out_specs=pl.BlockSpec((tm, 2 * FFN_H), lambda i: (i, 0)),
        out_shape=S((t, 2 * FFN_H), ACT_DTYPE), name=name, compiler_params=_cp("parallel"))(dy, w, ff)


def _call_into(body, into, in_specs, args, *, n_prefetch, grid, out_specs, **kw):
    n_in = len(args)
    if into is None:
        gs = pltpu.PrefetchScalarGridSpec(num_scalar_prefetch=n_prefetch, grid=grid, in_specs=in_specs,
                                          out_specs=out_specs)
        return pl.pallas_call(body, grid_spec=gs, **kw)(*args)

    def wrapped(*refs):
        return body(*refs[:n_in], *refs[n_in + 1:])

    gs = pltpu.PrefetchScalarGridSpec(num_scalar_prefetch=n_prefetch, grid=grid,
                                      in_specs=list(in_specs) + [ANY], out_specs=out_specs)
    return pl.pallas_call(wrapped, grid_spec=gs, input_output_aliases={n_in: 0}, **kw)(*args, into)


def _dw_norm_parts(ht, dparts, nc, name):
    t = ht.shape[1]
    tk = _row_tile(t, 1024)
    widths = [p.shape[1] for p in dparts]
    segs = _segments(widths, nc)
    n = len(dparts)
    nk = t // tk

    def body(*refs):
        h_ref, d_refs, o_ref, acc_ref = refs[0], refs[1:1 + n], refs[1 + n], refs[2 + n]
        k = pl.program_id(0)
        h = h_ref[...]

        @pl.when(k == 0)
        def _():
            acc_ref[...] = jnp.zeros_like(acc_ref)

        for (p, po, j, jo, wd) in segs:
            acc_ref[j, :, jo:jo + wd] += _dot(h, d_refs[p][:, po:po + wd])

        @pl.when(k == nk - 1)
        def _():
            o_ref[...] = acc_ref[...].astype(o_ref.dtype)

    return pl.pallas_call(
        body, grid=(nk,),
        in_specs=[pl.BlockSpec((D, tk), lambda k: (0, k))]
        + [pl.BlockSpec((tk, wd), lambda k: (k, 0)) for wd in widths],
        out_specs=pl.BlockSpec((N_CHIPS, D, nc), lambda k: (0, 0, 0)),
        out_shape=S((N_CHIPS, D, nc), MXU_DTYPE), name=name,
        scratch_shapes=[pltpu.VMEM((N_CHIPS, D, nc), F32)], compiler_params=_cp("arbitrary"))(ht, *dparts)


def _dw_norm_cols(ht, dy, nc, name):
    t = ht.shape[1]
    tk = _row_tile(t, 2048)
    nk = t // tk

    def body(h_ref, dy_ref, o_ref, acc_ref):
        k = pl.program_id(1)

        @pl.when(k == 0)
        def _():
            acc_ref[...] = jnp.zeros_like(acc_ref)

        acc_ref[...] += _dot(h_ref[...], dy_ref[...])

        @pl.when(k == nk - 1)
        def _():
            o_ref[...] = acc_ref[...].astype(o_ref.dtype)

    return pl.pallas_call(
        body, grid=(N_CHIPS, nk),
        in_specs=[pl.BlockSpec((D, tk), lambda j, k: (0, k)), pl.BlockSpec((tk, nc), lambda j, k: (k, j))],
        out_specs=pl.BlockSpec((None, D, nc), lambda j, k: (j, 0, 0)),
        out_shape=S((N_CHIPS, D, nc), MXU_DTYPE), name=name,
        scratch_shapes=[pltpu.VMEM((D, nc), F32)], compiler_params=_cp("parallel", "arbitrary"))(ht, dy)


def _dw_parts(parts, dy, name):
    t = dy.shape[0]
    tk = _row_tile(t, 1024)
    widths = [p.shape[1] for p in parts]
    offs = np.cumsum([0] + widths)
    ktot = int(offs[-1])
    n = len(parts)
    nk = t // tk

    def body(*refs):
        p_refs, dy_ref, o_ref, acc_ref = refs[:n], refs[n], refs[n + 1], refs[n + 2]
        k = pl.program_id(0)

        @pl.when(k == 0)
        def _():
            acc_ref[...] = jnp.zeros_like(acc_ref)

        dyv = _mx(dy_ref[...])
        for p in range(n):
            acc_ref[int(offs[p]):int(offs[p + 1]), :] += _dot_tn(p_refs[p][...], dyv)

        @pl.when(k == nk - 1)
        def _():
            o_ref[...] = acc_ref[...].astype(o_ref.dtype)

    return pl.pallas_call(
        body, grid=(nk,),
        in_specs=[pl.BlockSpec((tk, wd), lambda k: (k, 0)) for wd in widths]
        + [pl.BlockSpec((tk, D), lambda k: (k, 0))],
        out_specs=pl.BlockSpec((ktot, D), lambda k: (0, 0)),
        out_shape=S((ktot, D), MXU_DTYPE), name=name,
        scratch_shapes=[pltpu.VMEM((ktot, D), F32)], compiler_params=_cp("arbitrary"))(*parts, dy)


def _dw_swiglu(ff, dy, name):
    t = dy.shape[0]
    tk = _row_tile(t, 512)
    nk = t // tk

    def body(f_ref, dy_ref, o_ref, acc_ref):
        k = pl.program_id(0)

        @pl.when(k == 0)
        def _():
            acc_ref[...] = jnp.zeros_like(acc_ref)

        acc_ref[...] += _dot_tn(_swiglu(f_ref[...]), dy_ref[...])

        @pl.when(k == nk - 1)
        def _():
            o_ref[...] = acc_ref[...].astype(o_ref.dtype)

    return pl.pallas_call(
        body, grid=(nk,),
        in_specs=[pl.BlockSpec((tk, 2 * FFN_H), lambda k: (k, 0)), pl.BlockSpec((tk, D), lambda k: (k, 0))],
        out_specs=pl.BlockSpec((FFN_H, D), lambda k: (0, 0)),
        out_shape=S((FFN_H, D), MXU_DTYPE), name=name,
        scratch_shapes=[pltpu.VMEM((FFN_H, D), F32)], compiler_params=_cp("arbitrary"))(ff, dy)


def _tables(t):
    half = HD // 2
    inv_freq = ROPE_BASE ** (-jnp.arange(half, dtype=F32) / half)
    base = (jnp.arange(t // CH, dtype=F32) * CH)[:, None] * inv_freq[None, :]
    off = jnp.arange(CH, dtype=F32)[:, None] * inv_freq[None, :]
    cb, sb, co, so = jnp.cos(base)[:, None], jnp.sin(base)[:, None], jnp.cos(off)[None], jnp.sin(off)[None]
    cos = (cb * co - sb * so).reshape(t, half)
    sin = (sb * co + cb * so).reshape(t, half)
    tb = {"cos2": jnp.concatenate([cos, cos], axis=1), "sin2": jnp.concatenate([-sin, sin], axis=1)}
    gf = 1.0 - jnp.exp2(-5.0 - jnp.arange(HEADS, dtype=F32))
    lgf = jnp.log(gf)[:, None]
    lgb = jnp.log(gf[::-1])[:, None]
    idx = jnp.arange(CH, dtype=F32)
    diff = idx[:, None] - idx[None, :]
    dfwd = jnp.where(diff >= 0, jnp.exp(lgf[:, :, None] * jnp.where(diff >= 0, diff, 0.0)), 0.0)
    dbwd = jnp.where(diff < 0, jnp.exp(lgb[:, :, None] * jnp.where(diff < 0, -diff, 0.0)), 0.0)
    tb["dm"] = dfwd + dbwd
    tb["dmt"] = jnp.swapaxes(tb["dm"], 1, 2)

    def lanes(a):
        return jnp.repeat(a.T, HD, axis=1)

    tb["xif"] = lanes(jnp.exp(lgf * (idx + 1)))
    tb["zf"] = lanes(jnp.exp(lgf * (CH - 1 - idx)))
    tb["xib"] = lanes(jnp.exp(lgb * (CH - idx)))
    tb["zb"] = lanes(jnp.exp(lgb * idx))
    tb["gcf"] = jnp.repeat(jnp.exp(lgf * CH), HD, axis=0).reshape(1, HEADS * HD)
    tb["gcb"] = jnp.repeat(jnp.exp(lgb * CH), HD, axis=0).reshape(1, HEADS * HD)
    return tb


def _full(shape):
    nd = len(shape)
    return pl.BlockSpec(shape, lambda *_: (0,) * nd)


def _gm_mixed(vn, ws_ref, bias):
    lane = lax.broadcasted_iota(jnp.int32, (CH, 128), 1)
    halves = []
    for hf in range(2):
        vh = _mx(vn[:, hf * 128:(hf + 1) * 128])
        r0 = jnp.dot(_mx(ws_ref[2 * hf]), vh, preferred_element_type=F32)
        r1 = jnp.dot(_mx(ws_ref[2 * hf + 1]), vh, preferred_element_type=F32)
        halves.append(jnp.where(lane < 64, r0, r1))
    return jnp.concatenate(halves, axis=1) + bias


def _gm_fwd(proj, ln_g, ln_b, ws, bias, name, riders=()):
    t = proj.shape[0]
    tm = _row_tile(t, 512)

    def body(pu_ref, pv_ref, g_ref, b_ref, ws_ref, bias_ref, o_ref):
        for c in range(tm // CH):
            rows = slice(c * CH, (c + 1) * CH)
            u = _gelu(pu_ref[rows, :])
            o, _ = _standardize(_gelu(pv_ref[rows, :]))
            vn = o * g_ref[...] + b_ref[...]
            o_ref[rows, :] = (u * _gm_mixed(vn, ws_ref, bias_ref[...])).astype(o_ref.dtype)

    (out,), rid = _pcall(
        body, [proj, proj, ln_g, ln_b, ws, bias], riders, grid=(t // tm,),
        in_specs=[pl.BlockSpec((tm, GM_W), lambda i: (i, 0)), pl.BlockSpec((tm, GM_W), lambda i: (i, 1)),
                  _full((1, GM_W)), _full((1, GM_W)), _full((GM_HEADS, CH, CH)), _full((CH, GM_W))],
        out_specs=[pl.BlockSpec((tm, GM_W), lambda i: (i, 0))],
        out_shape=[S((t, GM_W), ACT_DTYPE)], name=name, sem=("parallel",))
    return out, rid


def _gm_bwd(proj, dy, ln_g, ln_b, ws, wst, bias, name):
    t = proj.shape[0]
    tm = _row_tile(t, 512)
    nb = t // tm

    def body(pu_ref, pv_ref, dy_ref, g_ref, b_ref, ws_ref, wst_ref, bias_ref,
             d_ref, dws_ref, dbs_ref, dg_ref, db_ref, dbias_ref):
        first = pl.program_id(0) == 0
        lane = lax.broadcasted_iota(jnp.int32, (CH, 128), 1)
        dws = [jnp.zeros((CH, CH), F32) for _ in range(GM_HEADS)]
        dbias = jnp.zeros((CH, GM_W), F32)
        dg = jnp.zeros((1, GM_W), F32)
        db = jnp.zeros((1, GM_W), F32)
        for c in range(tm // CH):
            rows = slice(c * CH, (c + 1) * CH)
            pu = pu_ref[rows, :]
            pv = pv_ref[rows, :]
            u = _gelu(pu)
            o, r = _standardize(_gelu(pv))
            vn = o * g_ref[...] + b_ref[...]
            mixed = _gm_mixed(vn, ws_ref, bias_ref[...])
            dyv = dy_ref[rows, :]
            d_ref[rows, :GM_W] = (dyv * mixed * _gelu_grad(pu)).astype(d_ref.dtype)
            dmixed = dyv * u
            dbias = dbias + dmixed
            dvn_halves = []
            for hf in range(2):
                dm = dmixed[:, hf * 128:(hf + 1) * 128]
                vh = vn[:, hf * 128:(hf + 1) * 128]
                dm0 = jnp.where(lane < 64, dm, 0.0)
                dm1 = dm - dm0
                dws[2 * hf] = dws[2 * hf] + _dot_nt(dm0, vh)
                dws[2 * hf + 1] = dws[2 * hf + 1] + _dot_nt(dm1, vh)
                t0 = _dot(wst_ref[2 * hf], dm)
                t1 = _dot(wst_ref[2 * hf + 1], dm)
                dvn_halves.append(jnp.where(lane < 64, t0, t1))
            dvn = jnp.concatenate(dvn_halves, axis=1)
            dg = dg + jnp.sum(dvn * o, axis=0, keepdims=True)
            db = db + jnp.sum(dvn, axis=0, keepdims=True)
            dv = _standardize_bwd(dvn * g_ref[...], o, r)
            d_ref[rows, GM_W:] = (dv * _gelu_grad(pv)).astype(d_ref.dtype)
        for h in range(GM_HEADS):
            _acc_out(dws_ref.at[h], dws[h], first)
        _acc_out(dbias_ref, dbias, first)
        _acc_out(dg_ref, dg, first)
        _acc_out(db_ref, db, first)

        @pl.when(pl.program_id(0) == nb - 1)
        def _():
            tot = dbias_ref[...]
            head = lax.broadcasted_iota(jnp.int32, (CH, GM_W), 1) // (GM_W // GM_HEADS)
            out = jnp.zeros((CH, 128), F32)
            for h in range(GM_HEADS):
                s = jnp.sum(jnp.where(head == h, tot, 0.0), axis=1, keepdims=True)
                out = jnp.where(lane == h, s, out)
            dbs_ref[...] = out

    return pl.pallas_call(
        body, grid=(nb,),
        in_specs=[pl.BlockSpec((tm, GM_W), lambda i: (i, 0)), pl.BlockSpec((tm, GM_W), lambda i: (i, 1)),
                  pl.BlockSpec((tm, GM_W), lambda i: (i, 0)),
                  _full((1, GM_W)), _full((1, GM_W)), _full((GM_HEADS, CH, CH)), _full((GM_HEADS, CH, CH)),
                  _full((CH, GM_W))],
        out_specs=[pl.BlockSpec((tm, 2 * GM_W), lambda i: (i, 0)), _full((GM_HEADS, CH, CH)), _full((CH, 128)),
                   _full((1, GM_W)), _full((1, GM_W))],
        out_shape=[S((t, 2 * GM_W), ACT_DTYPE), S((GM_HEADS, CH, CH), F32), S((CH, 128), F32),
                   S((1, GM_W), F32), S((1, GM_W), F32)],
        scratch_shapes=[pltpu.VMEM((CH, GM_W), F32)],
        name=name, compiler_params=_cp("arbitrary"))(proj, proj, dy, ln_g, ln_b, ws, wst, bias)


def _rot(x, cos2, sin2):
    return x * cos2 + pltpu.roll(x, HD // 2, 1) * sin2


def _rot_bwd(dx, cos2, sin2):
    return dx * cos2 + pltpu.roll(dx * sin2, HD // 2, 1)


Q_COL, K_COL, V_COL, GATE_COL = 1, 2, 3, 4


def _rotate_qk(o_ref, cos2, sin2):
    for col, scale in ((Q_COL, 1.0), (K_COL, HD ** -0.5)):
        for h in range(HEADS):
            cols = slice(col * RET_W + h * HD, col * RET_W + (h + 1) * HD)
            o_ref[:, cols] = _rot(o_ref[:, cols], cos2, sin2) * scale


def _ret_scan(lhs, lhs_col, rhs, rhs_col, lp, ls, gp, gs, name):
    t = lhs.shape[0]
    n = t // CH
    r = 4 if n % 4 == 0 else 1
    ns = n // r

    def body(lp_ref, ls_ref, gp_ref, gs_ref, l1_ref, r1_ref, l2_ref, r2_ref, pre_ref, suf_ref, sp_ref, ss_ref):
        @pl.when(pl.program_id(0) == 0)
        def _():
            sp_ref[...] = jnp.zeros_like(sp_ref)
            ss_ref[...] = jnp.zeros_like(ss_ref)

        def kv(l_ref, r_ref, scale, rows):
            lv = l_ref[rows, :] * scale
            rv = r_ref[rows, :]
            return jnp.concatenate([_dot_tn(lv[:, h * HD:(h + 1) * HD], rv[:, h * HD:(h + 1) * HD])
                                    for h in range(HEADS)], axis=1)

        for j in range(r):
            pre_ref[j] = sp_ref[...].astype(pre_ref.dtype)
            sp_ref[...] = sp_ref[...] * gp_ref[...] + kv(l1_ref, r1_ref, lp_ref[...], slice(j * CH, (j + 1) * CH))
        for j in reversed(range(r)):
            suf_ref[j] = ss_ref[...].astype(suf_ref.dtype)
            ss_ref[...] = ss_ref[...] * gs_ref[...] + kv(l2_ref, r2_ref, ls_ref[...], slice(j * CH, (j + 1) * CH))

    w = HEADS * HD
    return pl.pallas_call(
        body, grid=(ns,),
        in_specs=[_full((CH, w)), _full((CH, w)), _full((1, w)), _full((1, w)),
                  pl.BlockSpec((r * CH, w), lambda s: (s, lhs_col)), pl.BlockSpec((r * CH, w), lambda s: (s, rhs_col)),
                  pl.BlockSpec((r * CH, w), lambda s: (ns - 1 - s, lhs_col)),
                  pl.BlockSpec((r * CH, w), lambda s: (ns - 1 - s, rhs_col))],
        out_specs=[pl.BlockSpec((r, HD, w), lambda s: (s, 0, 0)), pl.BlockSpec((r, HD, w), lambda s: (ns - 1 - s, 0, 0))],
        out_shape=[S((n, HD, w), MXU_DTYPE)] * 2, name=name,
        scratch_shapes=[pltpu.VMEM((HD, w), F32), pltpu.VMEM((HD, w), F32)],
        compiler_params=_cp("arbitrary"))(lp, ls, gp, gs, lhs, rhs, lhs, rhs)


def _ret_out(proj, qkv, sf, sb, tb, name, riders=()):
    t = proj.shape[0]
    r = 4 if (t // CH) % 4 == 0 else 1
    tm = r * CH
    w = HEADS * HD

    def body(rq_ref, rk_ref, v_ref, g_ref, sf_ref, sb_ref, dm_ref, xif_ref, xib_ref, a_ref, y_ref):
        for c in range(r):
            rows = slice(c * CH, (c + 1) * CH)
            for h in range(HEADS):
                cols = slice(h * HD, (h + 1) * HD)
                q = rq_ref[rows, cols]
                p = _dot_nt(q, rk_ref[rows, cols]) * dm_ref[h]
                a = (_dot(p, v_ref[rows, cols]) + _dot(q * xif_ref[:, cols], sf_ref[c, :, cols])
                     + _dot(q * xib_ref[:, cols], sb_ref[c, :, cols]))
                a_ref[rows, cols] = a
                o, _ = _standardize(a)
                gv = g_ref[rows, cols]
                y_ref[rows, cols] = (o * (gv * _sigmoid(gv))).astype(y_ref.dtype)

    (a, y), rid = _pcall(
        body, [qkv, qkv, qkv, proj, sf, sb, tb["dm"], tb["xif"], tb["xib"]], riders, grid=(t // tm,),
        in_specs=[pl.BlockSpec((tm, w), lambda i: (i, 0)), pl.BlockSpec((tm, w), lambda i: (i, 1)),
                  pl.BlockSpec((tm, w), lambda i: (i, 2)), pl.BlockSpec((tm, w), lambda i: (i, GATE_COL)),
                  pl.BlockSpec((r, HD, w), lambda i: (i, 0, 0)), pl.BlockSpec((r, HD, w), lambda i: (i, 0, 0)),
                  _full((HEADS, CH, CH)), _full((CH, w)), _full((CH, w))],
        out_specs=[pl.BlockSpec((tm, w), lambda i: (i, 0))] * 2,
        out_shape=[S((t, w), F32), S((t, w), ACT_DTYPE)], name=name, sem=("parallel",))
    return a, y, rid


def _ret_bwd_main(qkv, da, sf, sb, gf, gb, tb, name, riders=()):
    t = qkv.shape[0]
    r = 4 if (t // CH) % 4 == 0 else 1
    tm = r * CH
    w = HEADS * HD
    scale = HD ** -0.5

    def body(rq_ref, rk_ref, v_ref, da_ref, sf_ref, sb_ref, gf_ref, gb_ref, dm_ref, dmt_ref,
             xif_ref, xib_ref, zf_ref, zb_ref, c_ref, s_ref, o_ref):
        for c in range(r):
            rows = slice(c * CH, (c + 1) * CH)
            cos2, sin2 = c_ref[rows, :], s_ref[rows, :]
            for h in range(HEADS):
                cols = slice(h * HD, (h + 1) * HD)
                q, k, v, dav = rq_ref[rows, cols], rk_ref[rows, cols], v_ref[rows, cols], da_ref[rows, cols]
                qm, km, vm, dam = _mx(q), _mx(k), _mx(v), _mx(dav)
                dm, dmt = dm_ref[h], dmt_ref[h]
                pt = _dot_nt(km, qm) * dmt
                dp = _dot_nt(dam, vm) * dm
                dpt = _dot_nt(vm, dam) * dmt
                sfh, sbh, gfh, gbh = sf_ref[c, :, cols], sb_ref[c, :, cols], gf_ref[c, :, cols], gb_ref[c, :, cols]
                zf, zb = zf_ref[:, cols], zb_ref[:, cols]
                dv = _dot(pt, dam) + zf * _dot(km, gfh) + zb * _dot(km, gbh)
                drq = _dot(dp, km) + xif_ref[:, cols] * _dot_nt(dam, sfh) + xib_ref[:, cols] * _dot_nt(dam, sbh)
                drk = _dot(dpt, qm) + _dot_nt(zf * v, gfh) + _dot_nt(zb * v, gbh)
                o_ref[rows, h * HD:(h + 1) * HD] = _rot_bwd(drq, cos2, sin2).astype(o_ref.dtype)
                o_ref[rows, w + h * HD:w + (h + 1) * HD] = (_rot_bwd(drk, cos2, sin2) * scale).astype(o_ref.dtype)
                o_ref[rows, 2 * w + h * HD:2 * w + (h + 1) * HD] = dv.astype(o_ref.dtype)

    st = pl.BlockSpec((r, HD, w), lambda i: (i, 0, 0))
    (out,), rid = _pcall(
        body, [qkv, qkv, qkv, da, sf, sb, gf, gb, tb["dm"], tb["dmt"], tb["xif"], tb["xib"], tb["zf"], tb["zb"],
               tb["cos2"], tb["sin2"]], riders, grid=(t // tm,),
        in_specs=[pl.BlockSpec((tm, w), lambda i: (i, 0)), pl.BlockSpec((tm, w), lambda i: (i, 1)),
                  pl.BlockSpec((tm, w), lambda i: (i, 2)), pl.BlockSpec((tm, w), lambda i: (i, 0)),
                  st, st, st, st, _full((HEADS, CH, CH)), _full((HEADS, CH, CH)),
                  _full((CH, w)), _full((CH, w)), _full((CH, w)), _full((CH, w)),
                  pl.BlockSpec((tm, HD), lambda i: (i, 0)), pl.BlockSpec((tm, HD), lambda i: (i, 0))],
        out_specs=[pl.BlockSpec((tm, 3 * w), lambda i: (i, 0))],
        out_shape=[S((t, 3 * w), ACT_DTYPE)], name=name, sem=("parallel",))
    return out, rid


CONV_TM = 256
CONV_SUB = 64
A_COL = (2 * GM_W + 4 * RET_W) // CV_W
G_COL = A_COL + 1


def _halo_specs(t, tm, col):
    nb16 = t // HALO
    per = tm // HALO
    return [pl.BlockSpec((tm, CV_W), lambda i: (i, col)),
            pl.BlockSpec((HALO, CV_W), lambda i: (jnp.maximum(i * per - 1, 0), col)),
            pl.BlockSpec((HALO, CV_W), lambda i: (jnp.minimum((i + 1) * per, nb16 - 1), col))]


def _fill_padded(dst_ref, prev, main, nxt, tm, i, nb):
    dst_ref[0:HALO, :] = jnp.where(i > 0, prev, 0.0)
    dst_ref[HALO:HALO + tm, :] = main
    dst_ref[HALO + tm:2 * HALO + tm, :] = jnp.where(i < nb - 1, nxt, 0.0)


SUBLANES = 8


def _fill_shifted(sh_ref, src_ref, tm):
    n = tm + 2 * HALO - SUBLANES
    for b in range(SUBLANES):
        sh_ref[b, 0:n, :] = src_ref[pl.ds(b, n), :]


def _tap(sh_ref, off, rows):
    return sh_ref[off % SUBLANES, pl.ds(off - off % SUBLANES, rows), :]


def _conv_fwd(proj, cw, cb, ln_g, ln_b, name, riders=()):
    t = proj.shape[0]
    tm = _row_tile(t, CONV_TM)
    nb = t // tm

    def body(a_ref, ap_ref, an_ref, g_ref, gp_ref, gn_ref, w_ref, b_ref, lg_ref, lb_ref, y_ref, hc_ref,
             hp_ref, sh_ref):
        i = pl.program_id(0)
        _fill_padded(hp_ref, ap_ref[...] * _sigmoid(gp_ref[...]), a_ref[...] * _sigmoid(g_ref[...]),
                     an_ref[...] * _sigmoid(gn_ref[...]), tm, i, nb)
        _fill_shifted(sh_ref, hp_ref, tm)
        for sb in range(tm // CONV_SUB):
            acc = jnp.zeros((CONV_SUB, CV_W), F32) + b_ref[...]
            for k in range(KCONV):
                acc = acc + w_ref[k:k + 1, :] * _tap(sh_ref, sb * CONV_SUB + k + 1, CONV_SUB)
            rows = slice(sb * CONV_SUB, (sb + 1) * CONV_SUB)
            hc_ref[rows, :] = acc
            o, _ = _standardize(acc)
            z = o * lg_ref[...] + lb_ref[...]
            y_ref[rows, :] = (z * _sigmoid(z)).astype(y_ref.dtype)

    (y, hc), rid = _pcall(
        body, [proj, proj, proj, proj, proj, proj, cw, cb, ln_g, ln_b], riders, grid=(nb,),
        in_specs=_halo_specs(t, tm, A_COL) + _halo_specs(t, tm, G_COL)
        + [_full((32, CV_W)), _full((1, CV_W)), _full((1, CV_W)), _full((1, CV_W))],
        out_specs=[pl.BlockSpec((tm, CV_W), lambda i: (i, 0))] * 2,
        out_shape=[S((t, CV_W), ACT_DTYPE), S((t, CV_W), F32)], name=name, sem=("parallel",),
        scratch_shapes=[pltpu.VMEM((tm + 2 * HALO, CV_W), F32), pltpu.VMEM((SUBLANES, tm + 2 * HALO, CV_W), F32)])
    return y, hc, rid


def _conv_bwd(proj, dy, hc, cw, ln_g, ln_b, name, riders=()):
    t = proj.shape[0]
    tm = _row_tile(t, CONV_TM)
    nb = t // tm

    def body(a_ref, ap_ref, an_ref, g_ref, gp_ref, gn_ref, dy_ref, dyp_ref, dyn_ref, hc_ref, hcp_ref, hcn_ref,
             w_ref, lg_ref, lb_ref, d_ref, dw_ref, dcb_ref, dlg_ref, dlb_ref, hp_ref, dhp_ref, dwacc_ref,
             sh_ref, dsh_ref):
        i = pl.program_id(0)
        first = i == 0

        def dhc_of(dyv, hcv):
            o, r = _standardize(hcv)
            z = o * lg_ref[...] + lb_ref[...]
            s = _sigmoid(z)
            dz = dyv * (s * (1.0 + z * (1.0 - s)))
            return _standardize_bwd(dz * lg_ref[...], o, r), dz, o

        dhc, dz, o = dhc_of(dy_ref[...], hc_ref[...])
        _acc_out(dlg_ref, jnp.sum(dz * o, axis=0, keepdims=True), first)
        _acc_out(dlb_ref, jnp.sum(dz, axis=0, keepdims=True), first)
        _acc_out(dcb_ref, jnp.sum(dhc, axis=0, keepdims=True), first)
        _fill_padded(dhp_ref, dhc_of(dyp_ref[...], hcp_ref[...])[0], dhc, dhc_of(dyn_ref[...], hcn_ref[...])[0],
                     tm, i, nb)
        _fill_padded(hp_ref, ap_ref[...] * _sigmoid(gp_ref[...]), a_ref[...] * _sigmoid(g_ref[...]),
                     an_ref[...] * _sigmoid(gn_ref[...]), tm, i, nb)

        _fill_shifted(sh_ref, hp_ref, tm)
        _fill_shifted(dsh_ref, dhp_ref, tm)

        @pl.when(first)
        def _():
            dwacc_ref[...] = jnp.zeros_like(dwacc_ref)

        for sb in range(tm // CONV_SUB):
            base = sb * CONV_SUB
            dmain = dhp_ref[pl.ds(HALO + base, CONV_SUB), :]
            dh = jnp.zeros((CONV_SUB, CV_W), F32)
            for k in range(KCONV):
                dh = dh + w_ref[k:k + 1, :] * _tap(dsh_ref, base + 2 * HALO - 1 - k, CONV_SUB)
                prod = dmain * _tap(sh_ref, base + k + 1, CONV_SUB)
                dwacc_ref[k * 8:(k + 1) * 8, :] += jnp.sum(prod.reshape(CONV_SUB // 8, 8, CV_W), axis=0)
            rows = slice(base, base + CONV_SUB)
            s = _sigmoid(g_ref[rows, :])
            d_ref[rows, :CV_W] = (dh * s).astype(d_ref.dtype)
            d_ref[rows, CV_W:] = (dh * a_ref[rows, :] * (s * (1.0 - s))).astype(d_ref.dtype)

        @pl.when(i == nb - 1)
        def _():
            for k in range(KCONV):
                dw_ref[k:k + 1, :] = jnp.sum(dwacc_ref[k * 8:(k + 1) * 8, :], axis=0, keepdims=True)
            dw_ref[KCONV:32, :] = jnp.zeros((32 - KCONV, CV_W), F32)

    hs = [pl.BlockSpec((tm, CV_W), lambda i: (i, 0)),
          pl.BlockSpec((HALO, CV_W), lambda i: (jnp.maximum(i * (tm // HALO) - 1, 0), 0)),
          pl.BlockSpec((HALO, CV_W), lambda i: (jnp.minimum((i + 1) * (tm // HALO), t // HALO - 1), 0))]
    outs, rid = _pcall(
        body, [proj, proj, proj, proj, proj, proj, dy, dy, dy, hc, hc, hc, cw, ln_g, ln_b], riders, grid=(nb,),
        in_specs=_halo_specs(t, tm, A_COL) + _halo_specs(t, tm, G_COL) + hs + hs
        + [_full((32, CV_W)), _full((1, CV_W)), _full((1, CV_W))],
        out_specs=[pl.BlockSpec((tm, 2 * CV_W), lambda i: (i, 0)), _full((32, CV_W)), _full((1, CV_W)),
                   _full((1, CV_W)), _full((1, CV_W))],
        out_shape=[S((t, 2 * CV_W), ACT_DTYPE), S((32, CV_W), F32), S((1, CV_W), F32), S((1, CV_W), F32),
                   S((1, CV_W), F32)],
        name=name, sem=("arbitrary",),
        scratch_shapes=[pltpu.VMEM((tm + 2 * HALO, CV_W), F32), pltpu.VMEM((tm + 2 * HALO, CV_W), F32),
                        pltpu.VMEM((32 * 8, CV_W), F32), pltpu.VMEM((SUBLANES, tm + 2 * HALO, CV_W), F32),
                        pltpu.VMEM((SUBLANES, tm + 2 * HALO, CV_W), F32)])
    return (*outs, rid)


def _loss_head(x, g, target, name):
    t = x.shape[0]
    tm = _row_tile(t, 512)

    def body(x_ref, g_ref, t_ref, dx_ref, dg_ref, l_ref):
        first = pl.program_id(0) == 0
        xv = x_ref[...]
        r = _rms_r(xv)
        e = xv * r * g_ref[...] - t_ref[...]
        dx, dgrow = _rms_bwd(e * (1.0 / D), xv, r, g_ref[...])
        dx_ref[...] = dx
        _acc_out(dg_ref, jnp.sum(dgrow, axis=0, keepdims=True), first)
        part = 0.5 * jnp.sum(jnp.mean(e * e, axis=-1, keepdims=True), axis=0, keepdims=True)
        _acc_out(l_ref, jnp.broadcast_to(part, (8, 128)), first)

    return pl.pallas_call(
        body, grid=(t // tm,),
        in_specs=[pl.BlockSpec((tm, D), lambda i: (i, 0)), _full((1, D)), pl.BlockSpec((tm, D), lambda i: (i, 0))],
        out_specs=[pl.BlockSpec((tm, D), lambda i: (i, 0)), _full((1, D)), _full((8, 128))],
        out_shape=[S((t, D), F32), S((1, D), F32), S((8, 128), F32)], name=name,
        compiler_params=_cp("arbitrary"))(x, g, target)


def _as2d(a):
    return a.reshape(-1, a.shape[-1])


def _ew_tile(rows, cols, n_arrays):
    budget = VMEM_LIMIT // 2
    tr = rows
    while tr * cols * 4 * n_arrays * 2 > budget and tr % 16 == 0:
        tr //= 2
    assert rows % tr == 0
    return tr


def _adamw(w, g, m, v, name):
    shape = w.shape
    w2, g2, m2, v2 = _as2d(w), _as2d(g), _as2d(m), _as2d(v)
    rows, cols = w2.shape
    tr = _ew_tile(rows, cols, 7)

    def body(w_ref, g_ref, m_ref, v_ref, d_ref, nm_ref, nv_ref):
        gv = g_ref[...]
        nm = ADAM_B1 * m_ref[...] + (1.0 - ADAM_B1) * gv
        nv = ADAM_B2 * v_ref[...] + (1.0 - ADAM_B2) * (gv * gv)
        m_hat = nm / (1.0 - ADAM_B1 ** ADAM_STEP)
        v_hat = nv / (1.0 - ADAM_B2 ** ADAM_STEP)
        d_ref[...] = -ADAM_LR * (m_hat / (jnp.sqrt(v_hat) + ADAM_EPS) + ADAM_WD * w_ref[...])
        nm_ref[...] = nm
        nv_ref[...] = nv

    spec = pl.BlockSpec((tr, cols), lambda i: (i, 0))
    outs = pl.pallas_call(body, grid=(rows // tr,), in_specs=[spec] * 4, out_specs=[spec] * 3,
                          out_shape=[S((rows, cols), F32)] * 3, name=name,
                          compiler_params=_cp("parallel"))(w2, g2, m2, v2)
    return tuple(o.reshape(shape) for o in outs)


BIG = (("w_in", "col"), ("w_out", "row"), ("w_ffn_in", "col"), ("w_ffn_out", "row"))
NBIG = len(BIG)


def _cast_to_gathered(w, l, me, name):
    _, r_, c_ = w.shape
    tr = _ew_tile(r_, c_, 2)

    def body(me_ref, w_ref, o_ref):
        o_ref[...] = w_ref[...].astype(o_ref.dtype)

    gs = pltpu.PrefetchScalarGridSpec(
        num_scalar_prefetch=1, grid=(r_ // tr,),
        in_specs=[pl.BlockSpec((None, tr, c_), lambda i, s: (l, i, 0))],
        out_specs=pl.BlockSpec((None, tr, c_), lambda i, s: (s[0], i, 0)))
    out = pl.pallas_call(body, grid_spec=gs, out_shape=S((N_CHIPS, r_, c_), MXU_DTYPE), name=name,
                         compiler_params=_cp("parallel"))(me.reshape(1), w)
    return out.reshape(N_CHIPS, 2, r_ // 2, c_)


def _all_gather(bufs, name, per_core=False):
    n = len(bufs)

    def body(*refs):
        i_refs, o_refs = refs[:n], refs[n:2 * n]
        isend, irecv, dsend, drecv, osend, orecv = refs[2 * n:]
        pos = _mesh_pos()
        x, y, c, me, _, _ = pos
        ici = _rider_copies("ici", i_refs, o_refs, isend, irecv, pos)
        d2d = _rider_copies("d2d", o_refs, o_refs, dsend, drecv, pos)
        own = []
        if per_core:
            for b in range(n):
                own.append(tuple(pltpu.make_async_remote_copy(
                    src_ref=s_, dst_ref=d_, send_sem=osend.at[b], recv_sem=orecv.at[b],
                    device_id=(x, y, 1 - c), device_id_type=MESH)
                    for s_, d_ in ((i_refs[b].at[me, c], o_refs[b].at[me, c]),
                                   (o_refs[b].at[me, 1 - c], o_refs[b].at[me, 1 - c]))))
        for cp, _ in ici + own:
            cp.start()
        for (_, land), (fwd, _) in zip(ici, d2d):
            land.wait_recv()
            fwd.start()
        for _, land in d2d + own:
            land.wait_recv()
        for cp, _ in ici + d2d + own:
            cp.wait_send()

    return pl.pallas_call(
        body, in_specs=[ANY] * n, out_specs=[ANY] * n, out_shape=[S(a.shape, a.dtype) for a in bufs],
        input_output_aliases={w: w for w in range(n)}, name=name,
        scratch_shapes=[pltpu.SemaphoreType.DMA((n, 3))] * 4 + [pltpu.SemaphoreType.DMA((n,))] * 2)(*bufs)


def _pair_exchange(grads, name):
    n = len(grads)

    def body(*refs):
        g_refs, theirs = refs[:n], refs[n:2 * n]
        send, recv = refs[2 * n:]
        x, y, c, *_ = _mesh_pos()
        cps = []
        for w in range(n):
            cp = pltpu.make_async_remote_copy(
                src_ref=g_refs[w].at[:, 1 - c], dst_ref=theirs[w], send_sem=send.at[w], recv_sem=recv.at[w],
                device_id=(x, y, 1 - c), device_id_type=MESH)
            cp.start()
            cps.append(cp)
        for cp in cps:
            cp.wait()

    return pl.pallas_call(
        body, in_specs=[ANY] * n, out_specs=[ANY] * n,
        out_shape=[S(a.shape[:1] + a.shape[2:], a.dtype) for a in grads], name=name,
        scratch_shapes=[pltpu.SemaphoreType.DMA((n,))] * 2)(*grads)


def _pair_sum(g, theirs, core, name):
    _, _, rh, c_ = g.shape
    tr = _ew_tile(rh, c_, 2)

    def body(s_ref, g_ref, t_ref, o_ref):
        o_ref[...] = (g_ref[...].astype(F32) + t_ref[...].astype(F32)).astype(o_ref.dtype)

    blk = pl.BlockSpec((None, tr, c_), lambda j, i, s: (j, i, 0))
    gs = pltpu.PrefetchScalarGridSpec(
        num_scalar_prefetch=1, grid=(N_CHIPS, rh // tr),
        in_specs=[pl.BlockSpec((None, None, tr, c_), lambda j, i, s: (j, s[0], i, 0)), blk], out_specs=blk)
    return pl.pallas_call(body, grid_spec=gs, out_shape=S(theirs.shape, theirs.dtype), name=name,
                          compiler_params=_cp("parallel", "parallel"))(core.reshape(1), g, theirs)


def _chip_sum(q, got, l, me, core, into, name):
    _, rh, c_ = got.shape
    tr = _ew_tile(rh, c_, 4)

    def body(s_ref, q_ref, g0_ref, g1_ref, g2_ref, o_ref):
        acc = q_ref[...].astype(F32)
        for r in (g0_ref, g1_ref, g2_ref):
            acc = acc + r[...].astype(F32)
        o_ref[...] = acc

    in_specs = [pl.BlockSpec((None, tr, c_), lambda i, s: (s[0], i, 0))] + [
        pl.BlockSpec((None, tr, c_), functools.partial(lambda k, i, s: (k, i, 0), k)) for k in range(3)]
    return _call_into(
        body, into, in_specs, [jnp.stack([me, core]), q, got, got, got], n_prefetch=1, grid=(rh // tr,),
        out_specs=pl.BlockSpec((None, None, tr, c_), lambda i, s: (l, s[1], i, 0)),
        out_shape=S((DEPTH, 2, rh, c_), F32), name=name, compiler_params=_cp("parallel"))


def _pair_gather(gs4):
    def body(*refs):
        i_refs, o_refs = refs[:NBIG], refs[NBIG:2 * NBIG]
        send, recv = refs[2 * NBIG:]
        x, y, c, *_ = _mesh_pos()
        cps = []
        for w in range(NBIG):
            cp = pltpu.make_async_remote_copy(
                src_ref=i_refs[w].at[:, c], dst_ref=o_refs[w].at[:, c], send_sem=send.at[w], recv_sem=recv.at[w],
                device_id=(x, y, 1 - c), device_id_type=MESH)
            cp.start()
            cps.append(cp)
        for cp in cps:
            cp.wait()

    outs = pl.pallas_call(
        body, in_specs=[ANY] * NBIG, out_specs=[ANY] * NBIG, out_shape=[S(a.shape, a.dtype) for a in gs4],
        input_output_aliases={w: w for w in range(NBIG)}, name="grad_pair_gather",
        scratch_shapes=[pltpu.SemaphoreType.DMA((NBIG,))] * 2)(*gs4)
    return [o.reshape(o.shape[0], 2 * o.shape[2], o.shape[3]) for o in outs]


def _all_reduce_small(p, me, core, name):
    rows = p.shape[0]

    def place(s_ref, p_ref, o_ref):
        o_ref[...] = p_ref[...]

    gs = pltpu.PrefetchScalarGridSpec(
        num_scalar_prefetch=1, grid=(1,), in_specs=[pl.BlockSpec((rows, 128), lambda i, s: (0, 0))],
        out_specs=pl.BlockSpec((None, None, rows, 128), lambda i, s: (s[0], s[1], 0, 0)))
    mine = pl.pallas_call(place, grid_spec=gs, out_shape=S((N_CHIPS, 2, rows, 128), F32), name=name + "_place",
                          compiler_params=_cp("arbitrary"))(jnp.stack([me, core]), p)
    parts = _all_gather([mine], name + "_gather", per_core=True)[0]

    def total(g_ref, o_ref):
        acc = g_ref[0, 0]
        for j in range(N_CHIPS):
            for c in range(2):
                if (j, c) != (0, 0):
                    acc = acc + g_ref[j, c]
        o_ref[...] = acc

    vm = pl.BlockSpec(memory_space=pltpu.VMEM)
    return pl.pallas_call(total, in_specs=[vm], out_specs=vm, out_shape=S((rows, 128), F32), name=name + "_sum",
                          compiler_params=pltpu.CompilerParams(vmem_limit_bytes=VMEM_LIMIT))(parts)


PACK_UNIT = 8 * 128


def _pack(arrs):
    parts = []
    for a in arrs:
        flat = a.reshape(-1)
        pad = (-flat.shape[0]) % PACK_UNIT
        parts.append(jnp.pad(flat, (0, pad)).reshape(-1, 128))
    return jnp.concatenate(parts, axis=0)


def _unpack(buf, shapes):
    outs, row = [], 0
    for shp in shapes:
        n = int(np.prod(shp))
        rows = -(-n // PACK_UNIT) * 8
        outs.append(buf[row:row + rows].reshape(-1)[:n].reshape(shp))
        row += rows
    return outs


SMALL = ("norm1_g", "gm_ln_g", "gm_ln_b", "gm_ws", "gm_bs", "conv_w", "conv_b", "conv_ln_g", "conv_ln_b",
         "norm2_g", "final_g")
WEIGHTS = ("norm1_g", "w_in", "gm_ln_g", "gm_ln_b", "gm_ws", "gm_bs", "conv_w", "conv_b", "conv_ln_g",
           "conv_ln_b", "w_out", "norm2_g", "w_ffn_in", "w_ffn_out", "final_g")


def kernel(x, norm1_g, w_in, gm_ln_g, gm_ln_b, gm_ws, gm_bs, conv_w, conv_b, conv_ln_g, conv_ln_b, w_out, norm2_g, w_ffn_in, w_ffn_out, final_g, loss_target, m_norm1_g, m_w_in, m_gm_ln_g, m_gm_ln_b, m_gm_ws, m_gm_bs, m_conv_w, m_conv_b, m_conv_ln_g, m_conv_ln_b, m_w_out, m_norm2_g, m_w_ffn_in, m_w_ffn_out, m_final_g, v_norm1_g, v_w_in, v_gm_ln_g, v_gm_ln_b, v_gm_ws, v_gm_bs, v_conv_w, v_conv_b, v_conv_ln_g, v_conv_ln_b, v_w_out, v_norm2_g, v_w_ffn_in, v_w_ffn_out, v_final_g):
    given = dict(locals())
    t = x.shape[1]
    xc = x.reshape(t, D)
    target = loss_target.reshape(t, D)
    me = 2 * lax.axis_index("x") + lax.axis_index("y")
    core = lax.axis_index("c")
    tb = _tables(t)

    me = me.astype(jnp.int32)
    core = core.astype(jnp.int32)
    names = [n for n, _ in BIG]
    kinds = dict(BIG)
    gathered = [{n: _cast_to_gathered(given[n], l, me, f"cast_{n}{l}") for n in names} for l in range(DEPTH)]
    gathered[0]["w_in"] = _all_gather([gathered[0]["w_in"]], "all_gather_w_in0")[0]

    def weight(l, n):
        b = gathered[l][n]
        r_, c_ = 2 * b.shape[2], b.shape[3]
        return b.reshape(N_CHIPS, r_, c_) if kinds[n] == "col" else b.reshape(N_CHIPS * r_, c_)

    cshard = CV_W // N_CHIPS
    placed = lax.dynamic_update_slice(jnp.zeros((DEPTH, KCONV, CV_W), F32),
                                      conv_w * (core == 0).astype(F32), (0, 0, me * cshard))
    conv_w_full = _unpack(_all_reduce_small(_pack([placed]), me, core, "gather_conv_w"), [(DEPTH, KCONV, CV_W)])[0]
    cw32 = jnp.pad(conv_w_full, ((0, 0), (0, 32 - KCONV), (0, 0)))

    def row(a, l):
        return a[l].reshape(1, -1)

    saved = []
    early = ["w_in", "w_out", "w_ffn_in"]
    for l in range(DEPTH):
        cur = gathered[l]
        nxt = gathered[l + 1] if l + 1 < DEPTH else None
        sv = {"x": xc}
        bias = jnp.repeat(gm_bs[l].T, GM_W // GM_HEADS, axis=1)
        first = ["w_ffn_in"] if l == 0 else ["w_ffn_out"]
        late = ["w_out", "w_ffn_out"]
        proj, h1t, qkv, rid = _norm_mm(xc, row(norm1_g, l), weight(l, "w_in"), F32, f"in_proj{l}", 512,
                                       [("ici" if l == 0 else "d2d", [cur[n] for n in first])],
                                       (tb["cos2"], tb["sin2"]))
        cur.update(zip(first, rid))
        y_gm, rid = _gm_fwd(proj, row(gm_ln_g, l), row(gm_ln_b, l), gm_ws[l], bias, f"gm_fwd{l}",
                            [("d2d", [cur[n] for n in first]), ("ici", [cur[late[0]]])] if l == 0 else ())
        cur.update(zip(first + late[:1], rid))
        sf, sb = _ret_scan(qkv, 1, qkv, 2, tb["zf"], tb["zb"], tb["gcf"], tb["gcb"], f"ret_state{l}")
        a, y_ret, rid = _ret_out(proj, qkv, sf, sb, tb, f"ret_out{l}",
                                 [("d2d", [cur[late[0]]]), ("ici", [cur[late[1]]])] if l == 0 else ())
        cur.update(zip(late, rid))
        y_cv, hc, rid = _conv_fwd(proj, cw32[l], row(conv_b, l), row(conv_ln_g, l), row(conv_ln_b, l),
                                  f"conv_fwd{l}", [("d2d", [cur[late[1]]])] if l == 0 else ())
        cur.update(zip(late[1:], rid))
        x_mid = _parts_mm_res([y_gm, y_ret, y_cv], weight(l, "w_out"), xc, f"out_proj{l}")
        ff, h2t, _, rid = _norm_mm(x_mid, row(norm2_g, l), weight(l, "w_ffn_in"), ACT_DTYPE, f"ffn_in{l}", 512,
                                   [("ici", [nxt[n] for n in early])] if nxt else ())
        if nxt:
            nxt.update(zip(early, rid))
        xc, rid = _swiglu_mm_res(ff, weight(l, "w_ffn_out"), x_mid, f"ffn_out{l}",
                                 [("d2d", [nxt[n] for n in early]), ("ici", [nxt["w_ffn_out"]])] if nxt else ())
        if nxt:
            nxt.update(zip(early + ["w_ffn_out"], rid))
        sv.update(bias=bias, proj=proj, qkv=qkv, h1t=h1t, h2t=h2t, y_gm=y_gm, sf=sf, sb=sb, a=a, y_ret=y_ret,
                  y_cv=y_cv,
                  hc=hc, x_mid=x_mid,
                  ff=ff)
        saved.append(sv)

    dx, d_final_g, lpart = _loss_head(xc, final_g.reshape(1, D), target, "loss_head")

    small_g = {n: [None] * DEPTH for n in SMALL}
    qs = [{} for _ in range(DEPTH)]
    got = [{} for _ in range(DEPTH)]
    ffn_w, mix_w = ["w_ffn_out", "w_ffn_in"], ["w_out", "w_in"]

    def halves(big_g, group):
        return [big_g[n].reshape(N_CHIPS, 2, given[n].shape[1] // 2, given[n].shape[2]) for n in group]

    def pair_sums(l, group, g4, theirs):
        qs[l].update({n: _pair_sum(g, th, core, f"pair_sum_{n}{l}") for n, g, th in zip(group, g4, theirs)})
        return [qs[l][n] for n in group]

    for l in reversed(range(DEPTH)):
        sv = saved[l]
        proj = sv["proj"]
        big_g = {}
        dff = _dx_swiglu(dx, weight(l, "w_ffn_out"), sv["ff"], f"ffn_out_dx{l}")
        big_g["w_ffn_out"] = _dw_swiglu(sv["ff"], dx, f"ffn_out_dw{l}")
        dx_mid, dg2, _ = _dx_norm([dff], weight(l, "w_ffn_in"), sv["x_mid"], row(norm2_g, l), dx,
                                  f"ffn_in_dx{l}", 512)
        big_g["w_ffn_in"] = _dw_norm_cols(sv["h2t"], dff, w_ffn_in.shape[2], f"ffn_in_dw{l}")
        g4 = halves(big_g, ffn_w)
        (dy_gm, da, d_g, dy_cv), theirs = _out_proj_dx(dx_mid, weight(l, "w_out"), sv["a"], proj,
                                                       f"out_proj_dx{l}", [("pairx", g4)])
        q_ffn = pair_sums(l, ffn_w, g4, theirs)
        big_g["w_out"] = _dw_parts([sv["y_gm"], sv["y_ret"], sv["y_cv"]], dx_mid, f"out_proj_dw{l}")
        d_cv, dcw, dcb, dclg, dclb, rid = _conv_bwd(proj, dy_cv, sv["hc"], cw32[l], row(conv_ln_g, l),
                                                    row(conv_ln_b, l), f"conv_bwd{l}", [("scatter", q_ffn[:1])])
        got[l].update(zip(ffn_w[:1], rid))
        gb_, gf_ = _ret_scan(sv["qkv"], 0, da, 0, tb["xib"], tb["xif"], tb["gcb"], tb["gcf"], f"ret_bwd_state{l}")
        d_qkv, rid = _ret_bwd_main(sv["qkv"], da, sv["sf"], sv["sb"], gf_, gb_, tb, f"ret_bwd_main{l}",
                                   [("scatter", q_ffn[1:])])
        got[l].update(zip(ffn_w[1:], rid))
        d_gm, dws, dbs, dglg, dglb = _gm_bwd(proj, dy_gm, row(gm_ln_g, l), row(gm_ln_b, l), gm_ws[l],
                                             jnp.swapaxes(gm_ws[l], 1, 2), sv["bias"], f"gm_bwd{l}")
        dparts = [d_gm, d_qkv, d_g, d_cv]
        big_g["w_in"] = _dw_norm_parts(sv["h1t"], dparts, w_in.shape[2], f"in_proj_dw{l}")
        g4 = halves(big_g, mix_w)
        q_mix = pair_sums(l, mix_w, g4, _pair_exchange(g4, f"grad_pair_exchange_mix{l}"))
        dx, dg1, rid = _dx_norm(dparts, weight(l, "w_in"), sv["x"], row(norm1_g, l), dx_mid, f"in_proj_dx{l}", 512,
                                [("scatter", q_mix)])
        got[l].update(zip(mix_w, rid))
        for n, val in (("norm1_g", dg1[0]), ("gm_ln_g", dglg[0]), ("gm_ln_b", dglb[0]), ("gm_ws", dws),
                       ("gm_bs", dbs[:, :GM_HEADS].T), ("conv_w", dcw[:KCONV]), ("conv_b", dcb[0]),
                       ("conv_ln_g", dclg[0]), ("conv_ln_b", dclb[0]), ("norm2_g", dg2[0])):
            small_g[n][l] = val

    small_shapes = [given[n].shape if n != "conv_w" else (DEPTH, KCONV, CV_W) for n in SMALL]
    partials = [d_final_g[0] if n == "final_g" else jnp.stack(small_g[n]) for n in SMALL]
    summed = _unpack(_all_reduce_small(_pack(partials + [lpart]), me, core, "all_reduce_small_grads"),
                     small_shapes + [lpart.shape])
    loss = summed[-1][0, 0]
    reduced = dict(zip(SMALL, summed))
    reduced["conv_w"] = lax.dynamic_slice(reduced["conv_w"], (0, 0, me * cshard), (DEPTH, KCONV, cshard))

    halves = [None] * NBIG
    for l in reversed(range(DEPTH)):
        halves = [_chip_sum(qs[l][n], got[l][n], l, me, core, h, f"chip_sum_{n}{l}") for n, h in zip(names, halves)]
    grads = dict(zip(names, _pair_gather(halves)))
    grads.update(reduced)

    delta, new_m, new_v = {}, {}, {}
    for n, _ in BIG:
        delta[n], new_m[n], new_v[n] = _adamw(given[n], grads[n], given["m_" + n], given["v_" + n], f"adamw_{n}")
    shapes = [given[n].shape for n in SMALL]
    packed = [_pack([src[n] if src is grads else src[p + n] for n in SMALL])
              for src, p in ((given, ""), (grads, ""), (given, "m_"), (given, "v_"))]
    outs = _adamw(*packed, "adamw_small")
    for dst, buf in zip((delta, new_m, new_v), outs):
        dst.update(zip(SMALL, _unpack(buf, shapes)))

    return (loss, dx.reshape(1, t, D), *[grads[n] for n in WEIGHTS], *[delta[n] for n in WEIGHTS],
            *[new_m[n] for n in WEIGHTS], *[new_v[n] for n in WEIGHTS])
```

```python
import functools
import math

import numpy as np
import jax
import jax.numpy as jnp
from jax import lax
from jax.experimental import pallas as pl
from jax.experimental.pallas import tpu as pltpu

F32 = jnp.float32
BF16 = jnp.bfloat16
MXU_DTYPE = BF16
ACT_DTYPE = BF16
S = jax.ShapeDtypeStruct

D = 1024
DEPTH = 2
GM_W = 256
GM_HEADS = 4
RET_W = 512
HEADS = 4
HD = 128
CV_W = 256
KCONV = 31
IN_W = 2 * GM_W + 4 * RET_W + 2 * CV_W
FFN_H = 2816
CH = 128
ROPE_BASE = 10000.0
EPS = 1e-6
N_CHIPS = 4
N_DEV = 8
HALO = 16

ADAM_LR = 0.001
ADAM_B1 = 0.9
ADAM_B2 = 0.999
ADAM_EPS = 1e-08
ADAM_WD = 0.01
ADAM_STEP = 10

VMEM_LIMIT = 52 * 1024 * 1024
MESH = pl.DeviceIdType.MESH


def _cp(*sem, vmem=VMEM_LIMIT):
    return pltpu.CompilerParams(dimension_semantics=tuple(sem), vmem_limit_bytes=vmem)


def _mx(a):
    return a.astype(MXU_DTYPE)


def _dot(a, b):
    return jnp.dot(_mx(a), _mx(b), preferred_element_type=F32)


def _dot_nt(a, b):
    return lax.dot_general(_mx(a), _mx(b), (((1,), (1,)), ((), ())), preferred_element_type=F32)


def _dot_tn(a, b):
    return lax.dot_general(_mx(a), _mx(b), (((0,), (0,)), ((), ())), preferred_element_type=F32)


def _sigmoid(x):
    return 1.0 / (1.0 + jnp.exp(-x))


def _gelu(x):
    return 0.5 * x * (1.0 + lax.erf(x * (1.0 / math.sqrt(2.0))))


def _gelu_grad(x):
    return 0.5 * (1.0 + lax.erf(x * (1.0 / math.sqrt(2.0)))) + x * jnp.exp(-0.5 * x * x) * (1.0 / math.sqrt(2.0 * math.pi))


def _rms_r(x):
    return lax.rsqrt(jnp.mean(x * x, axis=-1, keepdims=True) + EPS)


def _rms_bwd(dh, x, r, g):
    u = dh * g
    dx = r * u - x * (r * r * r) * jnp.mean(u * x, axis=-1, keepdims=True)
    return dx, dh * x * r


def _standardize(a):
    mu = jnp.mean(a, axis=-1, keepdims=True)
    d = a - mu
    r = lax.rsqrt(jnp.mean(d * d, axis=-1, keepdims=True) + EPS)
    return d * r, r


def _standardize_bwd(do, o, r):
    return r * (do - jnp.mean(do, axis=-1, keepdims=True) - o * jnp.mean(do * o, axis=-1, keepdims=True))


def _acc_out(ref, val, first):
    @pl.when(first)
    def _():
        ref[...] = val

    @pl.when(jnp.logical_not(first))
    def _():
        ref[...] += val


def _row_tile(t, pref):
    tm = min(t, pref)
    assert t % tm == 0, (t, tm)
    return tm


def _segments(part_widths, shard_w):
    bounds = {0}
    off = 0
    for w in part_widths:
        off += w
        bounds.add(off)
    total = off
    for j in range(1, total // shard_w + 1):
        bounds.add(j * shard_w)
    bounds = sorted(bounds)
    starts = np.cumsum([0] + list(part_widths))
    segs = []
    for a, b in zip(bounds[:-1], bounds[1:]):
        p = int(np.searchsorted(starts, a, side="right") - 1)
        segs.append((p, a - int(starts[p]), a // shard_w, a % shard_w, b - a))
    return segs


ANY = pl.BlockSpec(memory_space=pl.ANY)


def _mesh_pos():
    x, y, c = lax.axis_index("x"), lax.axis_index("y"), lax.axis_index("c")
    chips = [(1 - x, y), (x, 1 - y), (1 - x, 1 - y)]
    return x, y, c, 2 * x + y, chips, [2 * cx + cy for cx, cy in chips]


def _rider_copies(kind, i_refs, o_refs, send, recv, pos):
    x, y, c, me, chips, cj = pos
    out = []
    for b, (i_ref, o_ref) in enumerate(zip(i_refs, o_refs)):
        for k in range(1 if kind == "pairx" else 3):
            if kind == "ici":
                src, dst, land, dev = i_ref.at[me, c], o_ref.at[me, c], o_ref.at[cj[k], c], (*chips[k], c)
            elif kind == "d2d":
                src, dst, land, dev = i_ref.at[cj[k], c], o_ref.at[cj[k], c], o_ref.at[cj[k], 1 - c], (x, y, 1 - c)
            elif kind == "pairx":
                src, dst, land, dev = i_ref.at[:, 1 - c], o_ref, o_ref, (x, y, 1 - c)
            else:
                src, dst, land, dev = i_ref.at[cj[k]], o_ref.at[k], o_ref.at[k], (*chips[k], c)
            out.append(tuple(pltpu.make_async_remote_copy(
                src_ref=s_, dst_ref=d_, send_sem=send.at[b, k], recv_sem=recv.at[b, k],
                device_id=dev, device_id_type=MESH) for s_, d_ in ((src, dst), (land, land))))
    return out


def _rider_out_shape(kind, a):
    if kind == "scatter":
        return S((3,) + a.shape[1:], a.dtype)
    if kind == "pairx":
        return S(a.shape[:1] + a.shape[2:], a.dtype)
    return S(a.shape, a.dtype)


def _pcall(body, args, riders, *, grid, in_specs, out_specs, out_shape, name, sem, scratch_shapes=()):
    outs = list(out_shape)
    if not riders:
        res = pl.pallas_call(body, grid=grid, in_specs=in_specs, out_specs=out_specs, out_shape=outs, name=name,
                             scratch_shapes=list(scratch_shapes), compiler_params=_cp(*sem))(*args)
        return res, []
    r_in = [a for _, bufs in riders for a in bufs]
    r_out = [_rider_out_shape(kind, a) for kind, bufs in riders for a in bufs]
    n_in, n_out, n_scr, n_r = len(args), len(outs), len(scratch_shapes), len(r_in)
    aliases, idx = {}, 0
    for kind, bufs in riders:
        for _ in bufs:
            if kind in ("ici", "d2d"):
                aliases[n_in + idx] = n_out + idx
            idx += 1
    sems = [pltpu.SemaphoreType.DMA((len(bufs), 3)) for _, bufs in riders for _ in range(2)]

    def wrapped(*refs):
        a, ri = refs[:n_in], refs[n_in:n_in + n_r]
        o, ro = refs[n_in + n_r:n_in + n_r + n_out], refs[n_in + n_r + n_out:n_in + 2 * n_r + n_out]
        scr = refs[n_in + 2 * n_r + n_out:n_in + 2 * n_r + n_out + n_scr]
        sm = refs[n_in + 2 * n_r + n_out + n_scr:]
        pos = _mesh_pos()
        copies, off = [], 0
        for r, (kind, bufs) in enumerate(riders):
            copies += _rider_copies(kind, ri[off:off + len(bufs)], ro[off:off + len(bufs)], sm[2 * r], sm[2 * r + 1], pos)
            off += len(bufs)
        ids = [pl.program_id(d) for d in range(len(grid))]
        first = functools.reduce(jnp.logical_and, [i == 0 for i in ids])
        last = functools.reduce(jnp.logical_and, [i == n - 1 for i, n in zip(ids, grid)])

        @pl.when(first)
        def _():
            for cp, _ in copies:
                cp.start()

        body(*a, *o, *scr)

        @pl.when(last)
        def _():
            for cp, land in copies:
                land.wait_recv()
                cp.wait_send()

    res = pl.pallas_call(
        wrapped, grid=grid, in_specs=list(in_specs) + [ANY] * n_r, out_specs=list(out_specs) + [ANY] * n_r,
        out_shape=outs + r_out, input_output_aliases=aliases, name=name,
        scratch_shapes=list(scratch_shapes) + sems, compiler_params=_cp(*(("arbitrary",) * len(grid))))(*args, *r_in)
    return res[:n_out], res[n_out:]


def _wcol_spec(w):
    return pl.BlockSpec(w.shape, lambda *_: (0, 0, 0))


def _wrow_spec(w):
    return pl.BlockSpec(w.shape, lambda *_: (0, 0))


def _norm_mm(x, g, w, out_dtype, name, tm_pref, riders=(), rope=None):
    t = x.shape[0]
    nc = w.shape[2]
    tm = _row_tile(t, tm_pref)
    extra = list(rope) if rope else []

    qkv_w = 3 * RET_W

    def body(x_ref, g_ref, w_ref, *rest):
        o_ref, ht_ref = rest[len(extra)], rest[len(extra) + 1]
        xv = x_ref[...]
        hf = xv * _rms_r(xv) * g_ref[...]
        h = _mx(hf)
        for j in range(N_CHIPS):
            o_ref[:, j * nc:(j + 1) * nc] = jnp.dot(h, w_ref[j], preferred_element_type=F32).astype(o_ref.dtype)
        if rope:
            _rotate_qk(o_ref, rest[0][...], rest[1][...])
            rest[-1][...] = o_ref[:, Q_COL * RET_W:Q_COL * RET_W + qkv_w].astype(rest[-1].dtype)
        ht_ref[...] = hf.T.astype(ht_ref.dtype)

    outs, rid = _pcall(
        body, [x, g, w] + extra, riders, grid=(t // tm,),
        in_specs=[pl.BlockSpec((tm, D), lambda i: (i, 0)), pl.BlockSpec((1, D), lambda i: (0, 0)), _wcol_spec(w)]
        + [pl.BlockSpec((tm, HD), lambda i: (i, 0)) for _ in extra],
        out_specs=[pl.BlockSpec((tm, N_CHIPS * nc), lambda i: (i, 0)), pl.BlockSpec((D, tm), lambda i: (0, i))]
        + ([pl.BlockSpec((tm, qkv_w), lambda i: (i, 0))] if rope else []),
        out_shape=[S((t, N_CHIPS * nc), out_dtype), S((D, t), MXU_DTYPE)]
        + ([S((t, qkv_w), ACT_DTYPE)] if rope else []), name=name, sem=("parallel",))
    return (*outs, rid) if rope else (*outs, None, rid)


def _parts_mm_res(parts, w, res, name):
    t = res.shape[0]
    tm = _row_tile(t, 512)
    widths = [p.shape[1] for p in parts]
    offs = np.cumsum([0] + widths)
    n = len(parts)

    def body(*refs):
        p_refs, w_ref, r_ref, o_ref = refs[:n], refs[n], refs[n + 1], refs[n + 2]
        acc = r_ref[...]
        for p in range(n):
            acc = acc + _dot(p_refs[p][...], w_ref[int(offs[p]):int(offs[p + 1]), :])
        o_ref[...] = acc

    return pl.pallas_call(
        body, grid=(t // tm,),
        in_specs=[pl.BlockSpec((tm, wd), lambda i: (i, 0)) for wd in widths]
        + [_wrow_spec(w), pl.BlockSpec((tm, D), lambda i: (i, 0))],
        out_specs=pl.BlockSpec((tm, D), lambda i: (i, 0)),
        out_shape=S((t, D), F32), name=name, compiler_params=_cp("parallel"))(*parts, w, res)


def _swiglu(ff):
    gate = ff[:, :FFN_H].astype(F32)
    up = ff[:, FFN_H:].astype(F32)
    return gate * _sigmoid(gate) * up


def _swiglu_mm_res(ff, w, res, name, riders=()):
    t = res.shape[0]
    tm = _row_tile(t, 512)

    def body(f_ref, w_ref, r_ref, o_ref):
        o_ref[...] = r_ref[...] + _dot(_swiglu(f_ref[...]), w_ref[...])

    (out,), rid = _pcall(
        body, [ff, w, res], riders, grid=(t // tm,),
        in_specs=[pl.BlockSpec((tm, 2 * FFN_H), lambda i: (i, 0)), _wrow_spec(w),
                  pl.BlockSpec((tm, D), lambda i: (i, 0))],
        out_specs=[pl.BlockSpec((tm, D), lambda i: (i, 0))],
        out_shape=[S((t, D), F32)], name=name, sem=("parallel",))
    return out, rid


def _dx_norm(dparts, w, x, g, dres, name, tm_pref, riders=()):
    t = x.shape[0]
    nc = w.shape[2]
    tm = _row_tile(t, tm_pref)
    widths = [p.shape[1] for p in dparts]
    segs = _segments(widths, nc)
    n = len(dparts)

    def body(*refs):
        d_refs = refs[:n]
        w_ref, x_ref, g_ref, r_ref, dx_ref, dg_ref = refs[n:]
        dh = jnp.zeros((tm, D), F32)
        for (p, po, j, jo, wd) in segs:
            dh = dh + _dot_nt(d_refs[p][:, po:po + wd], w_ref[j, :, jo:jo + wd])
        xv = x_ref[...]
        dx, dgrow = _rms_bwd(dh, xv, _rms_r(xv), g_ref[...])
        dx_ref[...] = r_ref[...] + dx
        _acc_out(dg_ref, jnp.sum(dgrow, axis=0, keepdims=True), pl.program_id(0) == 0)

    (dx, dg), rid = _pcall(
        body, [*dparts, w, x, g, dres], riders, grid=(t // tm,),
        in_specs=[pl.BlockSpec((tm, wd), lambda i: (i, 0)) for wd in widths]
        + [_wcol_spec(w), pl.BlockSpec((tm, D), lambda i: (i, 0)),
           pl.BlockSpec((1, D), lambda i: (0, 0)), pl.BlockSpec((tm, D), lambda i: (i, 0))],
        out_specs=[pl.BlockSpec((tm, D), lambda i: (i, 0)), pl.BlockSpec((1, D), lambda i: (0, 0))],
        out_shape=[S((t, D), F32), S((1, D), F32)], name=name, sem=("arbitrary",))
    return dx, dg, rid


def _out_proj_dx(dy, w, a, proj, name, riders=()):
    t = dy.shape[0]
    tm = _row_tile(t, 512)
    wr = HEADS * HD

    def body(dy_ref, w_ref, a_ref, g_ref, dgm_ref, da_ref, dg_ref, dcv_ref):
        dyv = _mx(dy_ref[...])
        dgm_ref[...] = _dot_nt(dyv, w_ref[0:GM_W, :])
        dcv_ref[...] = _dot_nt(dyv, w_ref[GM_W + RET_W:, :])
        for h in range(HEADS):
            cols = slice(h * HD, (h + 1) * HD)
            dyr = _dot_nt(dyv, w_ref[GM_W + h * HD:GM_W + (h + 1) * HD, :])
            o, r = _standardize(a_ref[:, cols])
            gv = g_ref[:, cols]
            s = _sigmoid(gv)
            dg_ref[:, cols] = (dyr * o * (s * (1.0 + gv * (1.0 - s)))).astype(dg_ref.dtype)
            da_ref[:, cols] = _standardize_bwd(dyr * (gv * s), o, r).astype(da_ref.dtype)

    return _pcall(
        body, [dy, w, a, proj], riders, grid=(t // tm,),
        in_specs=[pl.BlockSpec((tm, D), lambda i: (i, 0)), _wrow_spec(w), pl.BlockSpec((tm, wr), lambda i: (i, 0)),
                  pl.BlockSpec((tm, wr), lambda i: (i, GATE_COL))],
        out_specs=[pl.BlockSpec((tm, GM_W), lambda i: (i, 0)), pl.BlockSpec((tm, wr), lambda i: (i, 0)),
                   pl.BlockSpec((tm, wr), lambda i: (i, 0)), pl.BlockSpec((tm, CV_W), lambda i: (i, 0))],
        out_shape=[S((t, GM_W), F32), S((t, wr), ACT_DTYPE), S((t, wr), ACT_DTYPE), S((t, CV_W), F32)],
        name=name, sem=("parallel",))


def _dx_swiglu(dy, w, ff, name):
    t = dy.shape[0]
    tm = _row_tile(t, 512)

    def body(dy_ref, w_ref, f_ref, o_ref):
        dact = _dot_nt(dy_ref[...], w_ref[...])
        gate = f_ref[:, :FFN_H].astype(F32)
        up = f_ref[:, FFN_H:].astype(F32)
        s = _sigmoid(gate)
        gs = gate * s
        o_ref[:, :FFN_H] = ((dact * up) * (s + gs - gs * s)).astype(o_ref.dtype)
        o_ref[:, FFN_H:] = (dact * gs).astype(o_ref.dtype)

    return pl.pallas_call(
        body, grid=(t // tm,),
        in_specs=[pl.BlockSpec((tm, D), lambda i: (i, 0)), _wrow_spec(w),
                  pl.BlockSpec((tm, 2 * FFN_H), lambda i: (i, 0))],
        out_specs=pl.BlockSpec((tm, 2 * FFN_H), lambda i: (i, 0)),
        out_shape=S((t, 2 * FFN_H), ACT_DTYPE), name=name, compiler_params=_cp("parallel"))(dy, w, ff)


def _call_into(body, into, in_specs, args, *, n_prefetch, grid, out_specs, **kw):
    n_in = len(args)
    if into is None:
        gs = pltpu.PrefetchScalarGridSpec(num_scalar_prefetch=n_prefetch, grid=grid, in_specs=in_specs,
                                          out_specs=out_specs)
        return pl.pallas_call(body, grid_spec=gs, **kw)(*args)

    def wrapped(*refs):
        return body(*refs[:n_in], *refs[n_in + 1:])

    gs = pltpu.PrefetchScalarGridSpec(num_scalar_prefetch=n_prefetch, grid=grid,
                                      in_specs=list(in_specs) + [ANY], out_specs=out_specs)
    return pl.pallas_call(wrapped, grid_spec=gs, input_output_aliases={n_in: 0}, **kw)(*args, into)


def _dw_norm_parts(ht, dparts, nc, name):
    t = ht.shape[1]
    tk = _row_tile(t, 1024)
    widths = [p.shape[1] for p in dparts]
    segs = _segments(widths, nc)
    n = len(dparts)
    nk = t // tk

    def body(*refs):
        h_ref, d_refs, o_ref, acc_ref = refs[0], refs[1:1 + n], refs[1 + n], refs[2 + n]
        k = pl.program_id(0)
        h = h_ref[...]

        @pl.when(k == 0)
        def _():
            acc_ref[...] = jnp.zeros_like(acc_ref)

        for (p, po, j, jo, wd) in segs:
            acc_ref[j, :, jo:jo + wd] += _dot(h, d_refs[p][:, po:po + wd])

        @pl.when(k == nk - 1)
        def _():
            o_ref[...] = acc_ref[...].astype(o_ref.dtype)

    return pl.pallas_call(
        body, grid=(nk,),
        in_specs=[pl.BlockSpec((D, tk), lambda k: (0, k))]
        + [pl.BlockSpec((tk, wd), lambda k: (k, 0)) for wd in widths],
        out_specs=pl.BlockSpec((N_CHIPS, D, nc), lambda k: (0, 0, 0)),
        out_shape=S((N_CHIPS, D, nc), MXU_DTYPE), name=name,
        scratch_shapes=[pltpu.VMEM((N_CHIPS, D, nc), F32)], compiler_params=_cp("arbitrary"))(ht, *dparts)


def _dw_norm_cols(ht, dy, nc, name):
    t = ht.shape[1]
    tk = _row_tile(t, 2048)
    nk = t // tk

    def body(h_ref, dy_ref, o_ref, acc_ref):
        k = pl.program_id(1)

        @pl.when(k == 0)
        def _():
            acc_ref[...] = jnp.zeros_like(acc_ref)

        acc_ref[...] += _dot(h_ref[...], dy_ref[...])

        @pl.when(k == nk - 1)
        def _():
            o_ref[...] = acc_ref[...].astype(o_ref.dtype)

    return pl.pallas_call(
        body, grid=(N_CHIPS, nk),
        in_specs=[pl.BlockSpec((D, tk), lambda j, k: (0, k)), pl.BlockSpec((tk, nc), lambda j, k: (k, j))],
        out_specs=pl.BlockSpec((None, D, nc), lambda j, k: (j, 0, 0)),
        out_shape=S((N_CHIPS, D, nc), MXU_DTYPE), name=name,
        scratch_shapes=[pltpu.VMEM((D, nc), F32)], compiler_params=_cp("parallel", "arbitrary"))(ht, dy)


def _dw_parts(parts, dy, name):
    t = dy.shape[0]
    tk = _row_tile(t, 1024)
    widths = [p.shape[1] for p in parts]
    offs = np.cumsum([0] + widths)
    ktot = int(offs[-1])
    n = len(parts)
    nk = t // tk

    def body(*refs):
        p_refs, dy_ref, o_ref, acc_ref = refs[:n], refs[n], refs[n + 1], refs[n + 2]
        k = pl.program_id(0)

        @pl.when(k == 0)
        def _():
            acc_ref[...] = jnp.zeros_like(acc_ref)

        dyv = _mx(dy_ref[...])
        for p in range(n):
            acc_ref[int(offs[p]):int(offs[p + 1]), :] += _dot_tn(p_refs[p][...], dyv)

        @pl.when(k == nk - 1)
        def _():
            o_ref[...] = acc_ref[...].astype(o_ref.dtype)

    return pl.pallas_call(
        body, grid=(nk,),
        in_specs=[pl.BlockSpec((tk, wd), lambda k: (k, 0)) for wd in widths]
        + [pl.BlockSpec((tk, D), lambda k: (k, 0))],
        out_specs=pl.BlockSpec((ktot, D), lambda k: (0, 0)),
        out_shape=S((ktot, D), MXU_DTYPE), name=name,
        scratch_shapes=[pltpu.VMEM((ktot, D), F32)], compiler_params=_cp("arbitrary"))(*parts, dy)


def _dw_swiglu(ff, dy, name):
    t = dy.shape[0]
    tk = _row_tile(t, 512)
    nk = t // tk

    def body(f_ref, dy_ref, o_ref, acc_ref):
        k = pl.program_id(0)

        @pl.when(k == 0)
        def _():
            acc_ref[...] = jnp.zeros_like(acc_ref)

        acc_ref[...] += _dot_tn(_swiglu(f_ref[...]), dy_ref[...])

        @pl.when(k == nk - 1)
        def _():
            o_ref[...] = acc_ref[...].astype(o_ref.dtype)

    return pl.pallas_call(
        body, grid=(nk,),
        in_specs=[pl.BlockSpec((tk, 2 * FFN_H), lambda k: (k, 0)), pl.BlockSpec((tk, D), lambda k: (k, 0))],
        out_specs=pl.BlockSpec((FFN_H, D), lambda k: (0, 0)),
        out_shape=S((FFN_H, D), MXU_DTYPE), name=name,
        scratch_shapes=[pltpu.VMEM((FFN_H, D), F32)], compiler_params=_cp("arbitrary"))(ff, dy)


def _tables(t):
    half = HD // 2
    inv_freq = ROPE_BASE ** (-jnp.arange(half, dtype=F32) / half)
    base = (jnp.arange(t // CH, dtype=F32) * CH)[:, None] * inv_freq[None, :]
    off = jnp.arange(CH, dtype=F32)[:, None] * inv_freq[None, :]
    cb, sb, co, so = jnp.cos(base)[:, None], jnp.sin(base)[:, None], jnp.cos(off)[None], jnp.sin(off)[None]
    cos = (cb * co - sb * so).reshape(t, half)
    sin = (sb * co + cb * so).reshape(t, half)
    tb = {"cos2": jnp.concatenate([cos, cos], axis=1), "sin2": jnp.concatenate([-sin, sin], axis=1)}
    gf = 1.0 - jnp.exp2(-5.0 - jnp.arange(HEADS, dtype=F32))
    lgf = jnp.log(gf)[:, None]
    lgb = jnp.log(gf[::-1])[:, None]
    idx = jnp.arange(CH, dtype=F32)
    diff = idx[:, None] - idx[None, :]
    dfwd = jnp.where(diff >= 0, jnp.exp(lgf[:, :, None] * jnp.where(diff >= 0, diff, 0.0)), 0.0)
    dbwd = jnp.where(diff < 0, jnp.exp(lgb[:, :, None] * jnp.where(diff < 0, -diff, 0.0)), 0.0)
    tb["dm"] = dfwd + dbwd
    tb["dmt"] = jnp.swapaxes(tb["dm"], 1, 2)

    def lanes(a):
        return jnp.repeat(a.T, HD, axis=1)

    tb["xif"] = lanes(jnp.exp(lgf * (idx + 1)))
    tb["zf"] = lanes(jnp.exp(lgf * (CH - 1 - idx)))
    tb["xib"] = lanes(jnp.exp(lgb * (CH - idx)))
    tb["zb"] = lanes(jnp.exp(lgb * idx))
    tb["gcf"] = jnp.repeat(jnp.exp(lgf * CH), HD, axis=0).reshape(1, HEADS * HD)
    tb["gcb"] = jnp.repeat(jnp.exp(lgb * CH), HD, axis=0).reshape(1, HEADS * HD)
    return tb


def _full(shape):
    nd = len(shape)
    return pl.BlockSpec(shape, lambda *_: (0,) * nd)


def _gm_mixed(vn, ws_ref, bias):
    lane = lax.broadcasted_iota(jnp.int32, (CH, 128), 1)
    halves = []
    for hf in range(2):
        vh = _mx(vn[:, hf * 128:(hf + 1) * 128])
        r0 = jnp.dot(_mx(ws_ref[2 * hf]), vh, preferred_element_type=F32)
        r1 = jnp.dot(_mx(ws_ref[2 * hf + 1]), vh, preferred_element_type=F32)
        halves.append(jnp.where(lane < 64, r0, r1))
    return jnp.concatenate(halves, axis=1) + bias


def _gm_fwd(proj, ln_g, ln_b, ws, bias, name, riders=()):
    t = proj.shape[0]
    tm = _row_tile(t, 512)

    def body(pu_ref, pv_ref, g_ref, b_ref, ws_ref, bias_ref, o_ref):
        for c in range(tm // CH):
            rows = slice(c * CH, (c + 1) * CH)
            u = _gelu(pu_ref[rows, :])
            o, _ = _standardize(_gelu(pv_ref[rows, :]))
            vn = o * g_ref[...] + b_ref[...]
            o_ref[rows, :] = (u * _gm_mixed(vn, ws_ref, bias_ref[...])).astype(o_ref.dtype)

    (out,), rid = _pcall(
        body, [proj, proj, ln_g, ln_b, ws, bias], riders, grid=(t // tm,),
        in_specs=[pl.BlockSpec((tm, GM_W), lambda i: (i, 0)), pl.BlockSpec((tm, GM_W), lambda i: (i, 1)),
                  _full((1, GM_W)), _full((1, GM_W)), _full((GM_HEADS, CH, CH)), _full((CH, GM_W))],
        out_specs=[pl.BlockSpec((tm, GM_W), lambda i: (i, 0))],
        out_shape=[S((t, GM_W), ACT_DTYPE)], name=name, sem=("parallel",))
    return out, rid


def _gm_bwd(proj, dy, ln_g, ln_b, ws, wst, bias, name):
    t = proj.shape[0]
    tm = _row_tile(t, 512)
    nb = t // tm

    def body(pu_ref, pv_ref, dy_ref, g_ref, b_ref, ws_ref, wst_ref, bias_ref,
             d_ref, dws_ref, dbs_ref, dg_ref, db_ref, dbias_ref):
        first = pl.program_id(0) == 0
        lane = lax.broadcasted_iota(jnp.int32, (CH, 128), 1)
        dws = [jnp.zeros((CH, CH), F32) for _ in range(GM_HEADS)]
        dbias = jnp.zeros((CH, GM_W), F32)
        dg = jnp.zeros((1, GM_W), F32)
        db = jnp.zeros((1, GM_W), F32)
        for c in range(tm // CH):
            rows = slice(c * CH, (c + 1) * CH)
            pu = pu_ref[rows, :]
            pv = pv_ref[rows, :]
            u = _gelu(pu)
            o, r = _standardize(_gelu(pv))
            vn = o * g_ref[...] + b_ref[...]
            mixed = _gm_mixed(vn, ws_ref, bias_ref[...])
            dyv = dy_ref[rows, :]
            d_ref[rows, :GM_W] = (dyv * mixed * _gelu_grad(pu)).astype(d_ref.dtype)
            dmixed = dyv * u
            dbias = dbias + dmixed
            dvn_halves = []
            for hf in range(2):
                dm = dmixed[:, hf * 128:(hf + 1) * 128]
                vh = vn[:, hf * 128:(hf + 1) * 128]
                dm0 = jnp.where(lane < 64, dm, 0.0)
                dm1 = dm - dm0
                dws[2 * hf] = dws[2 * hf] + _dot_nt(dm0, vh)
                dws[2 * hf + 1] = dws[2 * hf + 1] + _dot_nt(dm1, vh)
                t0 = _dot(wst_ref[2 * hf], dm)
                t1 = _dot(wst_ref[2 * hf + 1], dm)
                dvn_halves.append(jnp.where(lane < 64, t0, t1))
            dvn = jnp.concatenate(dvn_halves, axis=1)
            dg = dg + jnp.sum(dvn * o, axis=0, keepdims=True)
            db = db + jnp.sum(dvn, axis=0, keepdims=True)
            dv = _standardize_bwd(dvn * g_ref[...], o, r)
            d_ref[rows, GM_W:] = (dv * _gelu_grad(pv)).astype(d_ref.dtype)
        for h in range(GM_HEADS):
            _acc_out(dws_ref.at[h], dws[h], first)
        _acc_out(dbias_ref, dbias, first)
        _acc_out(dg_ref, dg, first)
        _acc_out(db_ref, db, first)

        @pl.when(pl.program_id(0) == nb - 1)
        def _():
            tot = dbias_ref[...]
            head = lax.broadcasted_iota(jnp.int32, (CH, GM_W), 1) // (GM_W // GM_HEADS)
            out = jnp.zeros((CH, 128), F32)
            for h in range(GM_HEADS):
                s = jnp.sum(jnp.where(head == h, tot, 0.0), axis=1, keepdims=True)
                out = jnp.where(lane == h, s, out)
            dbs_ref[...] = out

    return pl.pallas_call(
        body, grid=(nb,),
        in_specs=[pl.BlockSpec((tm, GM_W), lambda i: (i, 0)), pl.BlockSpec((tm, GM_W), lambda i: (i, 1)),
                  pl.BlockSpec((tm, GM_W), lambda i: (i, 0)),
                  _full((1, GM_W)), _full((1, GM_W)), _full((GM_HEADS, CH, CH)), _full((GM_HEADS, CH, CH)),
                  _full((CH, GM_W))],
        out_specs=[pl.BlockSpec((tm, 2 * GM_W), lambda i: (i, 0)), _full((GM_HEADS, CH, CH)), _full((CH, 128)),
                   _full((1, GM_W)), _full((1, GM_W))],
        out_shape=[S((t, 2 * GM_W), ACT_DTYPE), S((GM_HEADS, CH, CH), F32), S((CH, 128), F32),
                   S((1, GM_W), F32), S((1, GM_W), F32)],
        scratch_shapes=[pltpu.VMEM((CH, GM_W), F32)],
        name=name, compiler_params=_cp("arbitrary"))(proj, proj, dy, ln_g, ln_b, ws, wst, bias)


def _rot(x, cos2, sin2):
    return x * cos2 + pltpu.roll(x, HD // 2, 1) * sin2


def _rot_bwd(dx, cos2, sin2):
    return dx * cos2 + pltpu.roll(dx * sin2, HD // 2, 1)


Q_COL, K_COL, V_COL, GATE_COL = 1, 2, 3, 4


def _rotate_qk(o_ref, cos2, sin2):
    for col, scale in ((Q_COL, 1.0), (K_COL, HD ** -0.5)):
        for h in range(HEADS):
            cols = slice(col * RET_W + h * HD, col * RET_W + (h + 1) * HD)
            o_ref[:, cols] = _rot(o_ref[:, cols], cos2, sin2) * scale


def _ret_scan(lhs, lhs_col, rhs, rhs_col, lp, ls, gp, gs, name):
    t = lhs.shape[0]
    n = t // CH
    r = 4 if n % 4 == 0 else 1
    ns = n // r

    def body(lp_ref, ls_ref, gp_ref, gs_ref, l1_ref, r1_ref, l2_ref, r2_ref, pre_ref, suf_ref, sp_ref, ss_ref):
        @pl.when(pl.program_id(0) == 0)
        def _():
            sp_ref[...] = jnp.zeros_like(sp_ref)
            ss_ref[...] = jnp.zeros_like(ss_ref)

        def kv(l_ref, r_ref, scale, rows):
            lv = l_ref[rows, :] * scale
            rv = r_ref[rows, :]
            return jnp.concatenate([_dot_tn(lv[:, h * HD:(h + 1) * HD], rv[:, h * HD:(h + 1) * HD])
                                    for h in range(HEADS)], axis=1)

        for j in range(r):
            pre_ref[j] = sp_ref[...].astype(pre_ref.dtype)
            sp_ref[...] = sp_ref[...] * gp_ref[...] + kv(l1_ref, r1_ref, lp_ref[...], slice(j * CH, (j + 1) * CH))
        for j in reversed(range(r)):
            suf_ref[j] = ss_ref[...].astype(suf_ref.dtype)
            ss_ref[...] = ss_ref[...] * gs_ref[...] + kv(l2_ref, r2_ref, ls_ref[...], slice(j * CH, (j + 1) * CH))

    w = HEADS * HD
    return pl.pallas_call(
        body, grid=(ns,),
        in_specs=[_full((CH, w)), _full((CH, w)), _full((1, w)), _full((1, w)),
                  pl.BlockSpec((r * CH, w), lambda s: (s, lhs_col)), pl.BlockSpec((r * CH, w), lambda s: (s, rhs_col)),
                  pl.BlockSpec((r * CH, w), lambda s: (ns - 1 - s, lhs_col)),
                  pl.BlockSpec((r * CH, w), lambda s: (ns - 1 - s, rhs_col))],
        out_specs=[pl.BlockSpec((r, HD, w), lambda s: (s, 0, 0)), pl.BlockSpec((r, HD, w), lambda s: (ns - 1 - s, 0, 0))],
        out_shape=[S((n, HD, w), MXU_DTYPE)] * 2, name=name,
        scratch_shapes=[pltpu.VMEM((HD, w), F32), pltpu.VMEM((HD, w), F32)],
        compiler_params=_cp("arbitrary"))(lp, ls, gp, gs, lhs, rhs, lhs, rhs)


def _ret_out(proj, qkv, sf, sb, tb, name, riders=()):
    t = proj.shape[0]
    r = 4 if (t // CH) % 4 == 0 else 1
    tm = r * CH
    w = HEADS * HD

    def body(rq_ref, rk_ref, v_ref, g_ref, sf_ref, sb_ref, dm_ref, xif_ref, xib_ref, a_ref, y_ref):
        for c in range(r):
            rows = slice(c * CH, (c + 1) * CH)
            for h in range(HEADS):
                cols = slice(h * HD, (h + 1) * HD)
                q = rq_ref[rows, cols]
                p = _dot_nt(q, rk_ref[rows, cols]) * dm_ref[h]
                a = (_dot(p, v_ref[rows, cols]) + _dot(q * xif_ref[:, cols], sf_ref[c, :, cols])
                     + _dot(q * xib_ref[:, cols], sb_ref[c, :, cols]))
                a_ref[rows, cols] = a
                o, _ = _standardize(a)
                gv = g_ref[rows, cols]
                y_ref[rows, cols] = (o * (gv * _sigmoid(gv))).astype(y_ref.dtype)

    (a, y), rid = _pcall(
        body, [qkv, qkv, qkv, proj, sf, sb, tb["dm"], tb["xif"], tb["xib"]], riders, grid=(t // tm,),
        in_specs=[pl.BlockSpec((tm, w), lambda i: (i, 0)), pl.BlockSpec((tm, w), lambda i: (i, 1)),
                  pl.BlockSpec((tm, w), lambda i: (i, 2)), pl.BlockSpec((tm, w), lambda i: (i, GATE_COL)),
                  pl.BlockSpec((r, HD, w), lambda i: (i, 0, 0)), pl.BlockSpec((r, HD, w), lambda i: (i, 0, 0)),
                  _full((HEADS, CH, CH)), _full((CH, w)), _full((CH, w))],
        out_specs=[pl.BlockSpec((tm, w), lambda i: (i, 0))] * 2,
        out_shape=[S((t, w), F32), S((t, w), ACT_DTYPE)], name=name, sem=("parallel",))
    return a, y, rid


def _ret_bwd_main(qkv, da, sf, sb, gf, gb, tb, name, riders=()):
    t = qkv.shape[0]
    r = 4 if (t // CH) % 4 == 0 else 1
    tm = r * CH
    w = HEADS * HD
    scale = HD ** -0.5

    def body(rq_ref, rk_ref, v_ref, da_ref, sf_ref, sb_ref, gf_ref, gb_ref, dm_ref, dmt_ref,
             xif_ref, xib_ref, zf_ref, zb_ref, c_ref, s_ref, o_ref):
        for c in range(r):
            rows = slice(c * CH, (c + 1) * CH)
            cos2, sin2 = c_ref[rows, :], s_ref[rows, :]
            for h in range(HEADS):
                cols = slice(h * HD, (h + 1) * HD)
                q, k, v, dav = rq_ref[rows, cols], rk_ref[rows, cols], v_ref[rows, cols], da_ref[rows, cols]
                qm, km, vm, dam = _mx(q), _mx(k), _mx(v), _mx(dav)
                dm, dmt = dm_ref[h], dmt_ref[h]
                pt = _dot_nt(km, qm) * dmt
                dp = _dot_nt(dam, vm) * dm
                dpt = _dot_nt(vm, dam) * dmt
                sfh, sbh, gfh, gbh = sf_ref[c, :, cols], sb_ref[c, :, cols], gf_ref[c, :, cols], gb_ref[c, :, cols]
                zf, zb = zf_ref[:, cols], zb_ref[:, cols]
                dv = _dot(pt, dam) + zf * _dot(km, gfh) + zb * _dot(km, gbh)
                drq = _dot(dp, km) + xif_ref[:, cols] * _dot_nt(dam, sfh) + xib_ref[:, cols] * _dot_nt(dam, sbh)
                drk = _dot(dpt, qm) + _dot_nt(zf * v, gfh) + _dot_nt(zb * v, gbh)
                o_ref[rows, h * HD:(h + 1) * HD] = _rot_bwd(drq, cos2, sin2).astype(o_ref.dtype)
                o_ref[rows, w + h * HD:w + (h + 1) * HD] = (_rot_bwd(drk, cos2, sin2) * scale).astype(o_ref.dtype)
                o_ref[rows, 2 * w + h * HD:2 * w + (h + 1) * HD] = dv.astype(o_ref.dtype)

    st = pl.BlockSpec((r, HD, w), lambda i: (i, 0, 0))
    (out,), rid = _pcall(
        body, [qkv, qkv, qkv, da, sf, sb, gf, gb, tb["dm"], tb["dmt"], tb["xif"], tb["xib"], tb["zf"], tb["zb"],
               tb["cos2"], tb["sin2"]], riders, grid=(t // tm,),
        in_specs=[pl.BlockSpec((tm, w), lambda i: (i, 0)), pl.BlockSpec((tm, w), lambda i: (i, 1)),
                  pl.BlockSpec((tm, w), lambda i: (i, 2)), pl.BlockSpec((tm, w), lambda i: (i, 0)),
                  st, st, st, st, _full((HEADS, CH, CH)), _full((HEADS, CH, CH)),
                  _full((CH, w)), _full((CH, w)), _full((CH, w)), _full((CH, w)),
                  pl.BlockSpec((tm, HD), lambda i: (i, 0)), pl.BlockSpec((tm, HD), lambda i: (i, 0))],
        out_specs=[pl.BlockSpec((tm, 3 * w), lambda i: (i, 0))],
        out_shape=[S((t, 3 * w), ACT_DTYPE)], name=name, sem=("parallel",))
    return out, rid


CONV_TM = 256
CONV_SUB = 64
A_COL = (2 * GM_W + 4 * RET_W) // CV_W
G_COL = A_COL + 1


def _halo_specs(t, tm, col):
    nb16 = t // HALO
    per = tm // HALO
    return [pl.BlockSpec((tm, CV_W), lambda i: (i, col)),
            pl.BlockSpec((HALO, CV_W), lambda i: (jnp.maximum(i * per - 1, 0), col)),
            pl.BlockSpec((HALO, CV_W), lambda i: (jnp.minimum((i + 1) * per, nb16 - 1), col))]


def _fill_padded(dst_ref, prev, main, nxt, tm, i, nb):
    dst_ref[0:HALO, :] = jnp.where(i > 0, prev, 0.0)
    dst_ref[HALO:HALO + tm, :] = main
    dst_ref[HALO + tm:2 * HALO + tm, :] = jnp.where(i < nb - 1, nxt, 0.0)


SUBLANES = 8


def _fill_shifted(sh_ref, src_ref, tm):
    n = tm + 2 * HALO - SUBLANES
    for b in range(SUBLANES):
        sh_ref[b, 0:n, :] = src_ref[pl.ds(b, n), :]


def _tap(sh_ref, off, rows):
    return sh_ref[off % SUBLANES, pl.ds(off - off % SUBLANES, rows), :]


def _conv_fwd(proj, cw, cb, ln_g, ln_b, name, riders=()):
    t = proj.shape[0]
    tm = _row_tile(t, CONV_TM)
    nb = t // tm

    def body(a_ref, ap_ref, an_ref, g_ref, gp_ref, gn_ref, w_ref, b_ref, lg_ref, lb_ref, y_ref, hc_ref,
             hp_ref, sh_ref):
        i = pl.program_id(0)
        _fill_padded(hp_ref, ap_ref[...] * _sigmoid(gp_ref[...]), a_ref[...] * _sigmoid(g_ref[...]),
                     an_ref[...] * _sigmoid(gn_ref[...]), tm, i, nb)
        _fill_shifted(sh_ref, hp_ref, tm)
        for sb in range(tm // CONV_SUB):
            acc = jnp.zeros((CONV_SUB, CV_W), F32) + b_ref[...]
            for k in range(KCONV):
                acc = acc + w_ref[k:k + 1, :] * _tap(sh_ref, sb * CONV_SUB + k + 1, CONV_SUB)
            rows = slice(sb * CONV_SUB, (sb + 1) * CONV_SUB)
            hc_ref[rows, :] = acc
            o, _ = _standardize(acc)
            z = o * lg_ref[...] + lb_ref[...]
            y_ref[rows, :] = (z * _sigmoid(z)).astype(y_ref.dtype)

    (y, hc), rid = _pcall(
        body, [proj, proj, proj, proj, proj, proj, cw, cb, ln_g, ln_b], riders, grid=(nb,),
        in_specs=_halo_specs(t, tm, A_COL) + _halo_specs(t, tm, G_COL)
        + [_full((32, CV_W)), _full((1, CV_W)), _full((1, CV_W)), _full((1, CV_W))],
        out_specs=[pl.BlockSpec((tm, CV_W), lambda i: (i, 0))] * 2,
        out_shape=[S((t, CV_W), ACT_DTYPE), S((t, CV_W), F32)], name=name, sem=("parallel",),
        scratch_shapes=[pltpu.VMEM((tm + 2 * HALO, CV_W), F32), pltpu.VMEM((SUBLANES, tm + 2 * HALO, CV_W), F32)])
    return y, hc, rid


def _conv_bwd(proj, dy, hc, cw, ln_g, ln_b, name, riders=()):
    t = proj.shape[0]
    tm = _row_tile(t, CONV_TM)
    nb = t // tm

    def body(a_ref, ap_ref, an_ref, g_ref, gp_ref, gn_ref, dy_ref, dyp_ref, dyn_ref, hc_ref, hcp_ref, hcn_ref,
             w_ref, lg_ref, lb_ref, d_ref, dw_ref, dcb_ref, dlg_ref, dlb_ref, hp_ref, dhp_ref, dwacc_ref,
             sh_ref, dsh_ref):
        i = pl.program_id(0)
        first = i == 0

        def dhc_of(dyv, hcv):
            o, r = _standardize(hcv)
            z = o * lg_ref[...] + lb_ref[...]
            s = _sigmoid(z)
            dz = dyv * (s * (1.0 + z * (1.0 - s)))
            return _standardize_bwd(dz * lg_ref[...], o, r), dz, o

        dhc, dz, o = dhc_of(dy_ref[...], hc_ref[...])
        _acc_out(dlg_ref, jnp.sum(dz * o, axis=0, keepdims=True), first)
        _acc_out(dlb_ref, jnp.sum(dz, axis=0, keepdims=True), first)
        _acc_out(dcb_ref, jnp.sum(dhc, axis=0, keepdims=True), first)
        _fill_padded(dhp_ref, dhc_of(dyp_ref[...], hcp_ref[...])[0], dhc, dhc_of(dyn_ref[...], hcn_ref[...])[0],
                     tm, i, nb)
        _fill_padded(hp_ref, ap_ref[...] * _sigmoid(gp_ref[...]), a_ref[...] * _sigmoid(g_ref[...]),
                     an_ref[...] * _sigmoid(gn_ref[...]), tm, i, nb)

        _fill_shifted(sh_ref, hp_ref, tm)
        _fill_shifted(dsh_ref, dhp_ref, tm)

        @pl.when(first)
        def _():
            dwacc_ref[...] = jnp.zeros_like(dwacc_ref)

        for sb in range(tm // CONV_SUB):
            base = sb * CONV_SUB
            dmain = dhp_ref[pl.ds(HALO + base, CONV_SUB), :]
            dh = jnp.zeros((CONV_SUB, CV_W), F32)
            for k in range(KCONV):
                dh = dh + w_ref[k:k + 1, :] * _tap(dsh_ref, base + 2 * HALO - 1 - k, CONV_SUB)
                prod = dmain * _tap(sh_ref, base + k + 1, CONV_SUB)
                dwacc_ref[k * 8:(k + 1) * 8, :] += jnp.sum(prod.reshape(CONV_SUB // 8, 8, CV_W), axis=0)
            rows = slice(base, base + CONV_SUB)
            s = _sigmoid(g_ref[rows, :])
            d_ref[rows, :CV_W] = (dh * s).astype(d_ref.dtype)
            d_ref[rows, CV_W:] = (dh * a_ref[rows, :] * (s * (1.0 - s))).astype(d_ref.dtype)

        @pl.when(i == nb - 1)
        def _():
            for k in range(KCONV):
                dw_ref[k:k + 1, :] = jnp.sum(dwacc_ref[k * 8:(k + 1) * 8, :], axis=0, keepdims=True)
            dw_ref[KCONV:32, :] = jnp.zeros((32 - KCONV, CV_W), F32)

    hs = [pl.BlockSpec((tm, CV_W), lambda i: (i, 0)),
          pl.BlockSpec((HALO, CV_W), lambda i: (jnp.maximum(i * (tm // HALO) - 1, 0), 0)),
          pl.BlockSpec((HALO, CV_W), lambda i: (jnp.minimum((i + 1) * (tm // HALO), t // HALO - 1), 0))]
    outs, rid = _pcall(
        body, [proj, proj, proj, proj, proj, proj, dy, dy, dy, hc, hc, hc, cw, ln_g, ln_b], riders, grid=(nb,),
        in_specs=_halo_specs(t, tm, A_COL) + _halo_specs(t, tm, G_COL) + hs + hs
        + [_full((32, CV_W)), _full((1, CV_W)), _full((1, CV_W))],
        out_specs=[pl.BlockSpec((tm, 2 * CV_W), lambda i: (i, 0)), _full((32, CV_W)), _full((1, CV_W)),
                   _full((1, CV_W)), _full((1, CV_W))],
        out_shape=[S((t, 2 * CV_W), ACT_DTYPE), S((32, CV_W), F32), S((1, CV_W), F32), S((1, CV_W), F32),
                   S((1, CV_W), F32)],
        name=name, sem=("arbitrary",),
        scratch_shapes=[pltpu.VMEM((tm + 2 * HALO, CV_W), F32), pltpu.VMEM((tm + 2 * HALO, CV_W), F32),
                        pltpu.VMEM((32 * 8, CV_W), F32), pltpu.VMEM((SUBLANES, tm + 2 * HALO, CV_W), F32),
                        pltpu.VMEM((SUBLANES, tm + 2 * HALO, CV_W), F32)])
    return (*outs, rid)


def _loss_head(x, g, target, name):
    t = x.shape[0]
    tm = _row_tile(t, 512)

    def body(x_ref, g_ref, t_ref, dx_ref, dg_ref, l_ref):
        first = pl.program_id(0) == 0
        xv = x_ref[...]
        r = _rms_r(xv)
        e = xv * r * g_ref[...] - t_ref[...]
        dx, dgrow = _rms_bwd(e * (1.0 / D), xv, r, g_ref[...])
        dx_ref[...] = dx
        _acc_out(dg_ref, jnp.sum(dgrow, axis=0, keepdims=True), first)
        part = 0.5 * jnp.sum(jnp.mean(e * e, axis=-1, keepdims=True), axis=0, keepdims=True)
        _acc_out(l_ref, jnp.broadcast_to(part, (8, 128)), first)

    return pl.pallas_call(
        body, grid=(t // tm,),
        in_specs=[pl.BlockSpec((tm, D), lambda i: (i, 0)), _full((1, D)), pl.BlockSpec((tm, D), lambda i: (i, 0))],
        out_specs=[pl.BlockSpec((tm, D), lambda i: (i, 0)), _full((1, D)), _full((8, 128))],
        out_shape=[S((t, D), F32), S((1, D), F32), S((8, 128), F32)], name=name,
        compiler_params=_cp("arbitrary"))(x, g, target)


def _as2d(a):
    return a.reshape(-1, a.shape[-1])


def _ew_tile(rows, cols, n_arrays):
    budget = VMEM_LIMIT // 2
    tr = rows
    while tr * cols * 4 * n_arrays * 2 > budget and tr % 16 == 0:
        tr //= 2
    assert rows % tr == 0
    return tr


def _adamw(w, g, m, v, name, pass_g=False):
    shape = w.shape
    w2, g2, m2, v2 = _as2d(w), _as2d(g), _as2d(m), _as2d(v)
    rows, cols = w2.shape
    tr = _ew_tile(rows, cols, 7 + pass_g)

    def body(w_ref, g_ref, m_ref, v_ref, d_ref, nm_ref, nv_ref, *g_out):
        gv = g_ref[...]
        for r in g_out:
            r[...] = gv
        nm = ADAM_B1 * m_ref[...] + (1.0 - ADAM_B1) * gv
        nv = ADAM_B2 * v_ref[...] + (1.0 - ADAM_B2) * (gv * gv)
        m_hat = nm / (1.0 - ADAM_B1 ** ADAM_STEP)
        v_hat = nv / (1.0 - ADAM_B2 ** ADAM_STEP)
        d_ref[...] = -ADAM_LR * (m_hat / (jnp.sqrt(v_hat) + ADAM_EPS) + ADAM_WD * w_ref[...])
        nm_ref[...] = nm
        nv_ref[...] = nv

    spec = pl.BlockSpec((tr, cols), lambda i: (i, 0))
    outs = pl.pallas_call(body, grid=(rows // tr,), in_specs=[spec] * 4, out_specs=[spec] * (3 + pass_g),
                          out_shape=[S((rows, cols), F32)] * (3 + pass_g), name=name,
                          compiler_params=_cp("parallel"))(w2, g2, m2, v2)
    return tuple(o.reshape(shape) for o in outs)


BIG = (("w_in", "col"), ("w_out", "row"), ("w_ffn_in", "col"), ("w_ffn_out", "row"))
NBIG = len(BIG)


def _cast_to_gathered(w, l, me, name):
    _, r_, c_ = w.shape
    tr = _ew_tile(r_, c_, 2)

    def body(me_ref, w_ref, o_ref):
        o_ref[...] = w_ref[...].astype(o_ref.dtype)

    gs = pltpu.PrefetchScalarGridSpec(
        num_scalar_prefetch=1, grid=(r_ // tr,),
        in_specs=[pl.BlockSpec((None, tr, c_), lambda i, s: (l, i, 0))],
        out_specs=pl.BlockSpec((None, tr, c_), lambda i, s: (s[0], i, 0)))
    out = pl.pallas_call(body, grid_spec=gs, out_shape=S((N_CHIPS, r_, c_), MXU_DTYPE), name=name,
                         compiler_params=_cp("parallel"))(me.reshape(1), w)
    return out.reshape(N_CHIPS, 2, r_ // 2, c_)


def _all_gather(bufs, name, per_core=False):
    n = len(bufs)

    def body(*refs):
        i_refs, o_refs = refs[:n], refs[n:2 * n]
        isend, irecv, dsend, drecv, osend, orecv = refs[2 * n:]
        pos = _mesh_pos()
        x, y, c, me, _, _ = pos
        ici = _rider_copies("ici", i_refs, o_refs, isend, irecv, pos)
        d2d = _rider_copies("d2d", o_refs, o_refs, dsend, drecv, pos)
        own = []
        if per_core:
            for b in range(n):
                own.append(tuple(pltpu.make_async_remote_copy(
                    src_ref=s_, dst_ref=d_, send_sem=osend.at[b], recv_sem=orecv.at[b],
                    device_id=(x, y, 1 - c), device_id_type=MESH)
                    for s_, d_ in ((i_refs[b].at[me, c], o_refs[b].at[me, c]),
                                   (o_refs[b].at[me, 1 - c], o_refs[b].at[me, 1 - c]))))
        for cp, _ in ici + own:
            cp.start()
        for (_, land), (fwd, _) in zip(ici, d2d):
            land.wait_recv()
            fwd.start()
        for _, land in d2d + own:
            land.wait_recv()
        for cp, _ in ici + d2d + own:
            cp.wait_send()

    return pl.pallas_call(
        body, in_specs=[ANY] * n, out_specs=[ANY] * n, out_shape=[S(a.shape, a.dtype) for a in bufs],
        input_output_aliases={w: w for w in range(n)}, name=name,
        scratch_shapes=[pltpu.SemaphoreType.DMA((n, 3))] * 4 + [pltpu.SemaphoreType.DMA((n,))] * 2)(*bufs)


def _pair_exchange(grads, name):
    n = len(grads)

    def body(*refs):
        g_refs, theirs = refs[:n], refs[n:2 * n]
        send, recv = refs[2 * n:]
        x, y, c, *_ = _mesh_pos()
        cps = []
        for w in range(n):
            cp = pltpu.make_async_remote_copy(
                src_ref=g_refs[w].at[:, 1 - c], dst_ref=theirs[w], send_sem=send.at[w], recv_sem=recv.at[w],
                device_id=(x, y, 1 - c), device_id_type=MESH)
            cp.start()
            cps.append(cp)
        for cp in cps:
            cp.wait()

    return pl.pallas_call(
        body, in_specs=[ANY] * n, out_specs=[ANY] * n,
        out_shape=[S(a.shape[:1] + a.shape[2:], a.dtype) for a in grads], name=name,
        scratch_shapes=[pltpu.SemaphoreType.DMA((n,))] * 2)(*grads)


def _pair_sum(g, theirs, core, name):
    _, _, rh, c_ = g.shape
    tr = _ew_tile(rh, c_, 2)

    def body(s_ref, g_ref, t_ref, o_ref):
        o_ref[...] = (g_ref[...].astype(F32) + t_ref[...].astype(F32)).astype(o_ref.dtype)

    blk = pl.BlockSpec((None, tr, c_), lambda j, i, s: (j, i, 0))
    gs = pltpu.PrefetchScalarGridSpec(
        num_scalar_prefetch=1, grid=(N_CHIPS, rh // tr),
        in_specs=[pl.BlockSpec((None, None, tr, c_), lambda j, i, s: (j, s[0], i, 0)), blk], out_specs=blk)
    return pl.pallas_call(body, grid_spec=gs, out_shape=S(theirs.shape, theirs.dtype), name=name,
                          compiler_params=_cp("parallel", "parallel"))(core.reshape(1), g, theirs)


def _chip_sum(q, got, l, me, core, into, name):
    _, rh, c_ = got.shape
    tr = _ew_tile(rh, c_, 4)

    def body(s_ref, q_ref, g0_ref, g1_ref, g2_ref, o_ref):
        acc = q_ref[...].astype(F32)
        for r in (g0_ref, g1_ref, g2_ref):
            acc = acc + r[...].astype(F32)
        o_ref[...] = acc

    in_specs = [pl.BlockSpec((None, tr, c_), lambda i, s: (s[0], i, 0))] + [
        pl.BlockSpec((None, tr, c_), functools.partial(lambda k, i, s: (k, i, 0), k)) for k in range(3)]
    return _call_into(
        body, into, in_specs, [jnp.stack([me, core]), q, got, got, got], n_prefetch=1, grid=(rh // tr,),
        out_specs=pl.BlockSpec((None, None, tr, c_), lambda i, s: (l, s[1], i, 0)),
        out_shape=S((DEPTH, 2, rh, c_), F32), name=name, compiler_params=_cp("parallel"))


def _pair_gather(gs4):
    def body(*refs):
        i_refs, o_refs = refs[:NBIG], refs[NBIG:2 * NBIG]
        send, recv = refs[2 * NBIG:]
        x, y, c, *_ = _mesh_pos()
        cps = []
        for w in range(NBIG):
            cp = pltpu.make_async_remote_copy(
                src_ref=i_refs[w].at[:, c], dst_ref=o_refs[w].at[:, c], send_sem=send.at[w], recv_sem=recv.at[w],
                device_id=(x, y, 1 - c), device_id_type=MESH)
            cp.start()
            cps.append(cp)
        for cp in cps:
            cp.wait()

    outs = pl.pallas_call(
        body, in_specs=[ANY] * NBIG, out_specs=[ANY] * NBIG, out_shape=[S(a.shape, a.dtype) for a in gs4],
        input_output_aliases={w: w for w in range(NBIG)}, name="grad_pair_gather",
        scratch_shapes=[pltpu.SemaphoreType.DMA((NBIG,))] * 2)(*gs4)
    return [o.reshape(o.shape[0], 2 * o.shape[2], o.shape[3]) for o in outs]


def _all_reduce_small(p, me, core, name):
    rows = p.shape[0]

    def place(s_ref, p_ref, o_ref):
        o_ref[...] = p_ref[...]

    gs = pltpu.PrefetchScalarGridSpec(
        num_scalar_prefetch=1, grid=(1,), in_specs=[pl.BlockSpec((rows, 128), lambda i, s: (0, 0))],
        out_specs=pl.BlockSpec((None, None, rows, 128), lambda i, s: (s[0], s[1], 0, 0)))
    mine = pl.pallas_call(place, grid_spec=gs, out_shape=S((N_CHIPS, 2, rows, 128), F32), name=name + "_place",
                          compiler_params=_cp("arbitrary"))(jnp.stack([me, core]), p)
    parts = _all_gather([mine], name + "_gather", per_core=True)[0]

    def total(g_ref, o_ref):
        acc = g_ref[0, 0]
        for j in range(N_CHIPS):
            for c in range(2):
                if (j, c) != (0, 0):
                    acc = acc + g_ref[j, c]
        o_ref[...] = acc

    vm = pl.BlockSpec(memory_space=pltpu.VMEM)
    return pl.pallas_call(total, in_specs=[vm], out_specs=vm, out_shape=S((rows, 128), F32), name=name + "_sum",
                          compiler_params=pltpu.CompilerParams(vmem_limit_bytes=VMEM_LIMIT))(parts)


PACK_UNIT = 8 * 128


def _pack(arrs):
    parts = []
    for a in arrs:
        flat = a.reshape(-1)
        pad = (-flat.shape[0]) % PACK_UNIT
        parts.append(jnp.pad(flat, (0, pad)).reshape(-1, 128))
    return jnp.concatenate(parts, axis=0)


def _unpack(buf, shapes):
    outs, row = [], 0
    for shp in shapes:
        n = int(np.prod(shp))
        rows = -(-n // PACK_UNIT) * 8
        outs.append(buf[row:row + rows].reshape(-1)[:n].reshape(shp))
        row += rows
    return outs


SMALL = ("norm1_g", "gm_ln_g", "gm_ln_b", "gm_ws", "gm_bs", "conv_w", "conv_b", "conv_ln_g", "conv_ln_b",
         "norm2_g", "final_g")
WEIGHTS = ("norm1_g", "w_in", "gm_ln_g", "gm_ln_b", "gm_ws", "gm_bs", "conv_w", "conv_b", "conv_ln_g",
           "conv_ln_b", "w_out", "norm2_g", "w_ffn_in", "w_ffn_out", "final_g")


def kernel(x, norm1_g, w_in, gm_ln_g, gm_ln_b, gm_ws, gm_bs, conv_w, conv_b, conv_ln_g, conv_ln_b, w_out, norm2_g, w_ffn_in, w_ffn_out, final_g, loss_target, m_norm1_g, m_w_in, m_gm_ln_g, m_gm_ln_b, m_gm_ws, m_gm_bs, m_conv_w, m_conv_b, m_conv_ln_g, m_conv_ln_b, m_w_out, m_norm2_g, m_w_ffn_in, m_w_ffn_out, m_final_g, v_norm1_g, v_w_in, v_gm_ln_g, v_gm_ln_b, v_gm_ws, v_gm_bs, v_conv_w, v_conv_b, v_conv_ln_g, v_conv_ln_b, v_w_out, v_norm2_g, v_w_ffn_in, v_w_ffn_out, v_final_g):
    given = dict(locals())
    t = x.shape[1]
    xc = x.reshape(t, D)
    target = loss_target.reshape(t, D)
    me = 2 * lax.axis_index("x") + lax.axis_index("y")
    core = lax.axis_index("c")
    tb = _tables(t)

    me = me.astype(jnp.int32)
    core = core.astype(jnp.int32)
    names = [n for n, _ in BIG]
    kinds = dict(BIG)
    gathered = [{n: _cast_to_gathered(given[n], l, me, f"cast_{n}{l}") for n in names} for l in range(DEPTH)]
    gathered[0]["w_in"] = _all_gather([gathered[0]["w_in"]], "all_gather_w_in0")[0]

    def weight(l, n):
        b = gathered[l][n]
        r_, c_ = 2 * b.shape[2], b.shape[3]
        return b.reshape(N_CHIPS, r_, c_) if kinds[n] == "col" else b.reshape(N_CHIPS * r_, c_)

    cshard = CV_W // N_CHIPS
    placed = lax.dynamic_update_slice(jnp.zeros((DEPTH, KCONV, CV_W), F32),
                                      conv_w * (core == 0).astype(F32), (0, 0, me * cshard))
    conv_w_full = _unpack(_all_reduce_small(_pack([placed]), me, core, "gather_conv_w"), [(DEPTH, KCONV, CV_W)])[0]
    cw32 = jnp.pad(conv_w_full, ((0, 0), (0, 32 - KCONV), (0, 0)))

    def row(a, l):
        return a[l].reshape(1, -1)

    saved = []
    early = ["w_in", "w_out", "w_ffn_in"]
    for l in range(DEPTH):
        cur = gathered[l]
        nxt = gathered[l + 1] if l + 1 < DEPTH else None
        sv = {"x": xc}
        bias = jnp.repeat(gm_bs[l].T, GM_W // GM_HEADS, axis=1)
        first = ["w_ffn_in"] if l == 0 else ["w_ffn_out"]
        late = ["w_out", "w_ffn_out"]
        proj, h1t, qkv, rid = _norm_mm(xc, row(norm1_g, l), weight(l, "w_in"), F32, f"in_proj{l}", 512,
                                       [("ici" if l == 0 else "d2d", [cur[n] for n in first])],
                                       (tb["cos2"], tb["sin2"]))
        cur.update(zip(first, rid))
        y_gm, rid = _gm_fwd(proj, row(gm_ln_g, l), row(gm_ln_b, l), gm_ws[l], bias, f"gm_fwd{l}",
                            [("d2d", [cur[n] for n in first]), ("ici", [cur[late[0]]])] if l == 0 else ())
        cur.update(zip(first + late[:1], rid))
        sf, sb = _ret_scan(qkv, 1, qkv, 2, tb["zf"], tb["zb"], tb["gcf"], tb["gcb"], f"ret_state{l}")
        a, y_ret, rid = _ret_out(proj, qkv, sf, sb, tb, f"ret_out{l}",
                                 [("d2d", [cur[late[0]]]), ("ici", [cur[late[1]]])] if l == 0 else ())
        cur.update(zip(late, rid))
        y_cv, hc, rid = _conv_fwd(proj, cw32[l], row(conv_b, l), row(conv_ln_g, l), row(conv_ln_b, l),
                                  f"conv_fwd{l}", [("d2d", [cur[late[1]]])] if l == 0 else ())
        cur.update(zip(late[1:], rid))
        x_mid = _parts_mm_res([y_gm, y_ret, y_cv], weight(l, "w_out"), xc, f"out_proj{l}")
        ff, h2t, _, rid = _norm_mm(x_mid, row(norm2_g, l), weight(l, "w_ffn_in"), ACT_DTYPE, f"ffn_in{l}", 512,
                                   [("ici", [nxt[n] for n in early])] if nxt else ())
        if nxt:
            nxt.update(zip(early, rid))
        xc, rid = _swiglu_mm_res(ff, weight(l, "w_ffn_out"), x_mid, f"ffn_out{l}",
                                 [("d2d", [nxt[n] for n in early]), ("ici", [nxt["w_ffn_out"]])] if nxt else ())
        if nxt:
            nxt.update(zip(early + ["w_ffn_out"], rid))
        sv.update(bias=bias, proj=proj, qkv=qkv, h1t=h1t, h2t=h2t, y_gm=y_gm, sf=sf, sb=sb, a=a, y_ret=y_ret,
                  y_cv=y_cv,
                  hc=hc, x_mid=x_mid,
                  ff=ff)
        saved.append(sv)

    dx, d_final_g, lpart = _loss_head(xc, final_g.reshape(1, D), target, "loss_head")

    small_g = {n: [None] * DEPTH for n in SMALL}
    qs = [{} for _ in range(DEPTH)]
    got = [{} for _ in range(DEPTH)]
    ffn_w, mix_w = ["w_ffn_out", "w_ffn_in"], ["w_out", "w_in"]

    def halves(big_g, group):
        return [big_g[n].reshape(N_CHIPS, 2, given[n].shape[1] // 2, given[n].shape[2]) for n in group]

    def pair_sums(l, group, g4, theirs):
        qs[l].update({n: _pair_sum(g, th, core, f"pair_sum_{n}{l}") for n, g, th in zip(group, g4, theirs)})
        return [qs[l][n] for n in group]

    for l in reversed(range(DEPTH)):
        sv = saved[l]
        proj = sv["proj"]
        big_g = {}
        dff = _dx_swiglu(dx, weight(l, "w_ffn_out"), sv["ff"], f"ffn_out_dx{l}")
        big_g["w_ffn_out"] = _dw_swiglu(sv["ff"], dx, f"ffn_out_dw{l}")
        dx_mid, dg2, _ = _dx_norm([dff], weight(l, "w_ffn_in"), sv["x_mid"], row(norm2_g, l), dx,
                                  f"ffn_in_dx{l}", 512)
        big_g["w_ffn_in"] = _dw_norm_cols(sv["h2t"], dff, w_ffn_in.shape[2], f"ffn_in_dw{l}")
        g4 = halves(big_g, ffn_w)
        (dy_gm, da, d_g, dy_cv), theirs = _out_proj_dx(dx_mid, weight(l, "w_out"), sv["a"], proj,
                                                       f"out_proj_dx{l}", [("pairx", g4)])
        q_ffn = pair_sums(l, ffn_w, g4, theirs)
        big_g["w_out"] = _dw_parts([sv["y_gm"], sv["y_ret"], sv["y_cv"]], dx_mid, f"out_proj_dw{l}")
        d_cv, dcw, dcb, dclg, dclb, rid = _conv_bwd(proj, dy_cv, sv["hc"], cw32[l], row(conv_ln_g, l),
                                                    row(conv_ln_b, l), f"conv_bwd{l}", [("scatter", q_ffn[:1])])
        got[l].update(zip(ffn_w[:1], rid))
        gb_, gf_ = _ret_scan(sv["qkv"], 0, da, 0, tb["xib"], tb["xif"], tb["gcb"], tb["gcf"], f"ret_bwd_state{l}")
        d_qkv, rid = _ret_bwd_main(sv["qkv"], da, sv["sf"], sv["sb"], gf_, gb_, tb, f"ret_bwd_main{l}",
                                   [("scatter", q_ffn[1:])])
        got[l].update(zip(ffn_w[1:], rid))
        d_gm, dws, dbs, dglg, dglb = _gm_bwd(proj, dy_gm, row(gm_ln_g, l), row(gm_ln_b, l), gm_ws[l],
                                             jnp.swapaxes(gm_ws[l], 1, 2), sv["bias"], f"gm_bwd{l}")
        dparts = [d_gm, d_qkv, d_g, d_cv]
        big_g["w_in"] = _dw_norm_parts(sv["h1t"], dparts, w_in.shape[2], f"in_proj_dw{l}")
        g4 = halves(big_g, mix_w)
        q_mix = pair_sums(l, mix_w, g4, _pair_exchange(g4, f"grad_pair_exchange_mix{l}"))
        dx, dg1, rid = _dx_norm(dparts, weight(l, "w_in"), sv["x"], row(norm1_g, l), dx_mid, f"in_proj_dx{l}", 512,
                                [("scatter", q_mix)])
        got[l].update(zip(mix_w, rid))
        for n, val in (("norm1_g", dg1[0]), ("gm_ln_g", dglg[0]), ("gm_ln_b", dglb[0]), ("gm_ws", dws),
                       ("gm_bs", dbs[:, :GM_HEADS].T), ("conv_w", dcw[:KCONV]), ("conv_b", dcb[0]),
                       ("conv_ln_g", dclg[0]), ("conv_ln_b", dclb[0]), ("norm2_g", dg2[0])):
            small_g[n][l] = val

    small_shapes = [given[n].shape if n != "conv_w" else (DEPTH, KCONV, CV_W) for n in SMALL]
    partials = [d_final_g[0] if n == "final_g" else jnp.stack(small_g[n]) for n in SMALL]
    summed = _unpack(_all_reduce_small(_pack(partials + [lpart]), me, core, "all_reduce_small_grads"),
                     small_shapes + [lpart.shape])
    loss = summed[-1][0, 0]
    reduced = dict(zip(SMALL, summed))
    reduced["conv_w"] = lax.dynamic_slice(reduced["conv_w"], (0, 0, me * cshard), (DEPTH, KCONV, cshard))

    halves = [None] * NBIG
    for l in reversed(range(DEPTH)):
        halves = [_chip_sum(qs[l][n], got[l][n], l, me, core, h, f"chip_sum_{n}{l}") for n, h in zip(names, halves)]
    grads = dict(zip(names, _pair_gather(halves)))
    grads.update(reduced)

    delta, new_m, new_v = {}, {}, {}
    for n, _ in BIG:
        delta[n], new_m[n], new_v[n], grads[n] = _adamw(given[n], grads[n], given["m_" + n], given["v_" + n],
                                                        f"adamw_{n}", pass_g=True)
    shapes = [given[n].shape for n in SMALL]
    packed = [_pack([src[n] if src is grads else src[p + n] for n in SMALL])
              for src, p in ((given, ""), (grads, ""), (given, "m_"), (given, "v_"))]
    outs = _adamw(*packed, "adamw_small")
    for dst, buf in zip((delta, new_m, new_v), outs):
        dst.update(zip(SMALL, _unpack(buf, shapes)))

    return (loss, dx.reshape(1, t, D), *[grads[n] for n in WEIGHTS], *[delta[n] for n in WEIGHTS],
            *[new_m[n] for n in WEIGHTS], *[new_v[n] for n in WEIGHTS])
```

```python
import functools
import math

import numpy as np
import jax
import jax.numpy as jnp
from jax import lax
from jax.experimental import pallas as pl
from jax.experimental.pallas import tpu as pltpu

F32 = jnp.float32
BF16 = jnp.bfloat16
MXU_DTYPE = BF16
ACT_DTYPE = BF16
S = jax.ShapeDtypeStruct

D = 1024
DEPTH = 2
GM_W = 256
GM_HEADS = 4
RET_W = 512
HEADS = 4
HD = 128
CV_W = 256
KCONV = 31
IN_W = 2 * GM_W + 4 * RET_W + 2 * CV_W
FFN_H = 2816
CH = 128
ROPE_BASE = 10000.0
EPS = 1e-6
N_CHIPS = 4
N_DEV = 8
HALO = 16

ADAM_LR = 0.001
ADAM_B1 = 0.9
ADAM_B2 = 0.999
ADAM_EPS = 1e-08
ADAM_WD = 0.01
ADAM_STEP = 10

VMEM_LIMIT = 52 * 1024 * 1024
MESH = pl.DeviceIdType.MESH


def _cp(*sem, vmem=VMEM_LIMIT):
    return pltpu.CompilerParams(dimension_semantics=tuple(sem), vmem_limit_bytes=vmem)


def _mx(a):
    return a.astype(MXU_DTYPE)


def _dot(a, b):
    return jnp.dot(_mx(a), _mx(b), preferred_element_type=F32)


def _dot_nt(a, b):
    return lax.dot_general(_mx(a), _mx(b), (((1,), (1,)), ((), ())), preferred_element_type=F32)


def _dot_tn(a, b):
    return lax.dot_general(_mx(a), _mx(b), (((0,), (0,)), ((), ())), preferred_element_type=F32)


def _sigmoid(x):
    return 1.0 / (1.0 + jnp.exp(-x))


def _gelu(x):
    return 0.5 * x * (1.0 + lax.erf(x * (1.0 / math.sqrt(2.0))))


def _gelu_grad(x):
    return 0.5 * (1.0 + lax.erf(x * (1.0 / math.sqrt(2.0)))) + x * jnp.exp(-0.5 * x * x) * (1.0 / math.sqrt(2.0 * math.pi))


def _rms_r(x):
    return lax.rsqrt(jnp.mean(x * x, axis=-1, keepdims=True) + EPS)


def _rms_bwd(dh, x, r, g):
    u = dh * g
    dx = r * u - x * (r * r * r) * jnp.mean(u * x, axis=-1, keepdims=True)
    return dx, dh * x * r


def _standardize(a):
    mu = jnp.mean(a, axis=-1, keepdims=True)
    d = a - mu
    r = lax.rsqrt(jnp.mean(d * d, axis=-1, keepdims=True) + EPS)
    return d * r, r


def _standardize_bwd(do, o, r):
    return r * (do - jnp.mean(do, axis=-1, keepdims=True) - o * jnp.mean(do * o, axis=-1, keepdims=True))


def _acc_out(ref, val, first):
    @pl.when(first)
    def _():
        ref[...] = val

    @pl.when(jnp.logical_not(first))
    def _():
        ref[...] += val


def _row_tile(t, pref):
    tm = min(t, pref)
    assert t % tm == 0, (t, tm)
    return tm


def _segments(part_widths, shard_w):
    bounds = {0}
    off = 0
    for w in part_widths:
        off += w
        bounds.add(off)
    total = off
    for j in range(1, total // shard_w + 1):
        bounds.add(j * shard_w)
    bounds = sorted(bounds)
    starts = np.cumsum([0] + list(part_widths))
    segs = []
    for a, b in zip(bounds[:-1], bounds[1:]):
        p = int(np.searchsorted(starts, a, side="right") - 1)
        segs.append((p, a - int(starts[p]), a // shard_w, a % shard_w, b - a))
    return segs


ANY = pl.BlockSpec(memory_space=pl.ANY)


def _mesh_pos():
    x, y, c = lax.axis_index("x"), lax.axis_index("y"), lax.axis_index("c")
    chips = [(1 - x, y), (x, 1 - y), (1 - x, 1 - y)]
    return x, y, c, 2 * x + y, chips, [2 * cx + cy for cx, cy in chips]


def _rider_copies(kind, i_refs, o_refs, send, recv, pos):
    x, y, c, me, chips, cj = pos
    out = []
    for b, (i_ref, o_ref) in enumerate(zip(i_refs, o_refs)):
        for k in range(1 if kind == "pairx" else 3):
            if kind == "ici":
                src, dst, land, dev = i_ref.at[me, c], o_ref.at[me, c], o_ref.at[cj[k], c], (*chips[k], c)
            elif kind == "d2d":
                src, dst, land, dev = i_ref.at[cj[k], c], o_ref.at[cj[k], c], o_ref.at[cj[k], 1 - c], (x, y, 1 - c)
            elif kind == "pairx":
                src, dst, land, dev = i_ref.at[:, 1 - c], o_ref, o_ref, (x, y, 1 - c)
            else:
                src, dst, land, dev = i_ref.at[cj[k]], o_ref.at[k], o_ref.at[k], (*chips[k], c)
            out.append(tuple(pltpu.make_async_remote_copy(
                src_ref=s_, dst_ref=d_, send_sem=send.at[b, k], recv_sem=recv.at[b, k],
                device_id=dev, device_id_type=MESH) for s_, d_ in ((src, dst), (land, land))))
    return out


def _rider_out_shape(kind, a):
    if kind == "scatter":
        return S((3,) + a.shape[1:], a.dtype)
    if kind == "pairx":
        return S(a.shape[:1] + a.shape[2:], a.dtype)
    return S(a.shape, a.dtype)


def _pcall(body, args, riders, *, grid, in_specs, out_specs, out_shape, name, sem, scratch_shapes=()):
    outs = list(out_shape)
    if not riders:
        res = pl.pallas_call(body, grid=grid, in_specs=in_specs, out_specs=out_specs, out_shape=outs, name=name,
                             scratch_shapes=list(scratch_shapes), compiler_params=_cp(*sem))(*args)
        return res, []
    r_in = [a for _, bufs in riders for a in bufs]
    r_out = [_rider_out_shape(kind, a) for kind, bufs in riders for a in bufs]
    n_in, n_out, n_scr, n_r = len(args), len(outs), len(scratch_shapes), len(r_in)
    aliases, idx = {}, 0
    for kind, bufs in riders:
        for _ in bufs:
            if kind in ("ici", "d2d"):
                aliases[n_in + idx] = n_out + idx
            idx += 1
    sems = [pltpu.SemaphoreType.DMA((len(bufs), 3)) for _, bufs in riders for _ in range(2)]

    def wrapped(*refs):
        a, ri = refs[:n_in], refs[n_in:n_in + n_r]
        o, ro = refs[n_in + n_r:n_in + n_r + n_out], refs[n_in + n_r + n_out:n_in + 2 * n_r + n_out]
        scr = refs[n_in + 2 * n_r + n_out:n_in + 2 * n_r + n_out + n_scr]
        sm = refs[n_in + 2 * n_r + n_out + n_scr:]
        pos = _mesh_pos()
        copies, off = [], 0
        for r, (kind, bufs) in enumerate(riders):
            copies += _rider_copies(kind, ri[off:off + len(bufs)], ro[off:off + len(bufs)], sm[2 * r], sm[2 * r + 1], pos)
            off += len(bufs)
        ids = [pl.program_id(d) for d in range(len(grid))]
        first = functools.reduce(jnp.logical_and, [i == 0 for i in ids])
        last = functools.reduce(jnp.logical_and, [i == n - 1 for i, n in zip(ids, grid)])

        @pl.when(first)
        def _():
            for cp, _ in copies:
                cp.start()

        body(*a, *o, *scr)

        @pl.when(last)
        def _():
            for cp, land in copies:
                land.wait_recv()
                cp.wait_send()

    res = pl.pallas_call(
        wrapped, grid=grid, in_specs=list(in_specs) + [ANY] * n_r, out_specs=list(out_specs) + [ANY] * n_r,
        out_shape=outs + r_out, input_output_aliases=aliases, name=name,
        scratch_shapes=list(scratch_shapes) + sems, compiler_params=_cp(*(("arbitrary",) * len(grid))))(*args, *r_in)
    return res[:n_out], res[n_out:]


def _wcol_spec(w):
    return pl.BlockSpec(w.shape, lambda *_: (0, 0, 0))


def _wrow_spec(w):
    return pl.BlockSpec(w.shape, lambda *_: (0, 0))


def _norm_mm(x, g, w, out_dtype, name, tm_pref, riders=(), rope=None):
    t = x.shape[0]
    nc = w.shape[2]
    tm = _row_tile(t, tm_pref)
    extra = list(rope) if rope else []

    qkv_w = 3 * RET_W

    def body(x_ref, g_ref, w_ref, *rest):
        o_ref, ht_ref = rest[len(extra)], rest[len(extra) + 1]
        xv = x_ref[...]
        hf = xv * _rms_r(xv) * g_ref[...]
        h = _mx(hf)
        for j in range(N_CHIPS):
            o_ref[:, j * nc:(j + 1) * nc] = jnp.dot(h, w_ref[j], preferred_element_type=F32).astype(o_ref.dtype)
        if rope:
            _rotate_qk(o_ref, rest[0][...], rest[1][...])
            rest[-1][...] = o_ref[:, Q_COL * RET_W:Q_COL * RET_W + qkv_w].astype(rest[-1].dtype)
        ht_ref[...] = hf.T.astype(ht_ref.dtype)

    outs, rid = _pcall(
        body, [x, g, w] + extra, riders, grid=(t // tm,),
        in_specs=[pl.BlockSpec((tm, D), lambda i: (i, 0)), pl.BlockSpec((1, D), lambda i: (0, 0)), _wcol_spec(w)]
        + [pl.BlockSpec((tm, HD), lambda i: (i, 0)) for _ in extra],
        out_specs=[pl.BlockSpec((tm, N_CHIPS * nc), lambda i: (i, 0)), pl.BlockSpec((D, tm), lambda i: (0, i))]
        + ([pl.BlockSpec((tm, qkv_w), lambda i: (i, 0))] if rope else []),
        out_shape=[S((t, N_CHIPS * nc), out_dtype), S((D, t), MXU_DTYPE)]
        + ([S((t, qkv_w), ACT_DTYPE)] if rope else []), name=name, sem=("parallel",))
    return (*outs, rid) if rope else (*outs, None, rid)


def _parts_mm_res(parts, w, res, name):
    t = res.shape[0]
    tm = _row_tile(t, 512)
    widths = [p.shape[1] for p in parts]
    offs = np.cumsum([0] + widths)
    n = len(parts)

    def body(*refs):
        p_refs, w_ref, r_ref, o_ref = refs[:n], refs[n], refs[n + 1], refs[n + 2]
        acc = r_ref[...]
        for p in range(n):
            acc = acc + _dot(p_refs[p][...], w_ref[int(offs[p]):int(offs[p + 1]), :])
        o_ref[...] = acc

    return pl.pallas_call(
        body, grid=(t // tm,),
        in_specs=[pl.BlockSpec((tm, wd), lambda i: (i, 0)) for wd in widths]
        + [_wrow_spec(w), pl.BlockSpec((tm, D), lambda i: (i, 0))],
        out_specs=pl.BlockSpec((tm, D), lambda i: (i, 0)),
        out_shape=S((t, D), F32), name=name, compiler_params=_cp("parallel"))(*parts, w, res)


def _swiglu(ff):
    gate = ff[:, :FFN_H].astype(F32)
    up = ff[:, FFN_H:].astype(F32)
    return gate * _sigmoid(gate) * up


def _swiglu_mm_res(ff, w, res, name, riders=()):
    t = res.shape[0]
    tm = _row_tile(t, 512)

    def body(f_ref, w_ref, r_ref, o_ref):
        o_ref[...] = r_ref[...] + _dot(_swiglu(f_ref[...]), w_ref[...])

    (out,), rid = _pcall(
        body, [ff, w, res], riders, grid=(t // tm,),
        in_specs=[pl.BlockSpec((tm, 2 * FFN_H), lambda i: (i, 0)), _wrow_spec(w),
                  pl.BlockSpec((tm, D), lambda i: (i, 0))],
        out_specs=[pl.BlockSpec((tm, D), lambda i: (i, 0))],
        out_shape=[S((t, D), F32)], name=name, sem=("parallel",))
    return out, rid


def _dx_norm(dparts, w, x, g, dres, name, tm_pref, riders=()):
    t = x.shape[0]
    nc = w.shape[2]
    tm = _row_tile(t, tm_pref)
    widths = [p.shape[1] for p in dparts]
    segs = _segments(widths, nc)
    n = len(dparts)

    def body(*refs):
        d_refs = refs[:n]
        w_ref, x_ref, g_ref, r_ref, dx_ref, dg_ref = refs[n:]
        dh = jnp.zeros((tm, D), F32)
        for (p, po, j, jo, wd) in segs:
            dh = dh + _dot_nt(d_refs[p][:, po:po + wd], w_ref[j, :, jo:jo + wd])
        xv = x_ref[...]
        dx, dgrow = _rms_bwd(dh, xv, _rms_r(xv), g_ref[...])
        dx_ref[...] = r_ref[...] + dx
        _acc_out(dg_ref, jnp.sum(dgrow, axis=0, keepdims=True), pl.program_id(0) == 0)

    (dx, dg), rid = _pcall(
        body, [*dparts, w, x, g, dres], riders, grid=(t // tm,),
        in_specs=[pl.BlockSpec((tm, wd), lambda i: (i, 0)) for wd in widths]
        + [_wcol_spec(w), pl.BlockSpec((tm, D), lambda i: (i, 0)),
           pl.BlockSpec((1, D), lambda i: (0, 0)), pl.BlockSpec((tm, D), lambda i: (i, 0))],
        out_specs=[pl.BlockSpec((tm, D), lambda i: (i, 0)), pl.BlockSpec((1, D), lambda i: (0, 0))],
        out_shape=[S((t, D), F32), S((1, D), F32)], name=name, sem=("arbitrary",))
    return dx, dg, rid


def _out_proj_dx(dy, w, a, proj, name, riders=()):
    t = dy.shape[0]
    tm = _row_tile(t, 512)
    wr = HEADS * HD

    def body(dy_ref, w_ref, a_ref, g_ref, dgm_ref, da_ref, dg_ref, dcv_ref):
        dyv = _mx(dy_ref[...])
        dgm_ref[...] = _dot_nt(dyv, w_ref[0:GM_W, :])
        dcv_ref[...] = _dot_nt(dyv, w_ref[GM_W + RET_W:, :])
        for h in range(HEADS):
            cols = slice(h * HD, (h + 1) * HD)
            dyr = _dot_nt(dyv, w_ref[GM_W + h * HD:GM_W + (h + 1) * HD, :])
            o, r = _standardize(a_ref[:, cols])
            gv = g_ref[:, cols]
            s = _sigmoid(gv)
            dg_ref[:, cols] = (dyr * o * (s * (1.0 + gv * (1.0 - s)))).astype(dg_ref.dtype)
            da_ref[:, cols] = _standardize_bwd(dyr * (gv * s), o, r).astype(da_ref.dtype)

    return _pcall(
        body, [dy, w, a, proj], riders, grid=(t // tm,),
        in_specs=[pl.BlockSpec((tm, D), lambda i: (i, 0)), _wrow_spec(w), pl.BlockSpec((tm, wr), lambda i: (i, 0)),
                  pl.BlockSpec((tm, wr), lambda i: (i, GATE_COL))],
        out_specs=[pl.BlockSpec((tm, GM_W), lambda i: (i, 0)), pl.BlockSpec((tm, wr), lambda i: (i, 0)),
                   pl.BlockSpec((tm, wr), lambda i: (i, 0)), pl.BlockSpec((tm, CV_W), lambda i: (i, 0))],
        out_shape=[S((t, GM_W), F32), S((t, wr), ACT_DTYPE), S((t, wr), ACT_DTYPE), S((t, CV_W), F32)],
        name=name, sem=("parallel",))


def _dx_swiglu(dy, w, ff, name):
    t = dy.shape[0]
    tm = _row_tile(t, 512)

    def body(dy_ref, w_ref, f_ref, o_ref):
        dact = _dot_nt(dy_ref[...], w_ref[...])
        gate = f_ref[:, :FFN_H].astype(F32)
        up = f_ref[:, FFN_H:].astype(F32)
        s = _sigmoid(gate)
        gs = gate * s
        o_ref[:, :FFN_H] = ((dact * up) * (s + gs - gs * s)).astype(o_ref.dtype)
        o_ref[:, FFN_H:] = (dact * gs).astype(o_ref.dtype)

    return pl.pallas_call(
        body, grid=(t // tm,),
        in_specs=[pl.BlockSpec((tm, D), lambda i: (i, 0)), _wrow_spec(w),
                  pl.BlockSpec((tm, 2 * FFN_H), lambda i: (i, 0))],
        out_specs=pl.BlockSpec((tm, 2 * FFN_H), lambda i: (i, 0)),
        out_shape=S((t, 2 * FFN_H), ACT_DTYPE), name=name, compiler_params=_cp("parallel"))(dy, w, ff)


def _call_into(body, into, in_specs, args, *, n_prefetch, grid, out_specs, **kw):
    n_in = len(args)
    if into is None:
        gs = pltpu.PrefetchScalarGridSpec(num_scalar_prefetch=n_prefetch, grid=grid, in_specs=in_specs,
                                          out_specs=out_specs)
        return pl.pallas_call(body, grid_spec=gs, **kw)(*args)

    def wrapped(*refs):
        return body(*refs[:n_in], *refs[n_in + 1:])

    gs = pltpu.PrefetchScalarGridSpec(num_scalar_prefetch=n_prefetch, grid=grid,
                                      in_specs=list(in_specs) + [ANY], out_specs=out_specs)
    return pl.pallas_call(wrapped, grid_spec=gs, input_output_aliases={n_in: 0}, **kw)(*args, into)


def _dw_norm_parts(ht, dparts, nc, name):
    t = ht.shape[1]
    tk = _row_tile(t, 1024)
    widths = [p.shape[1] for p in dparts]
    segs = _segments(widths, nc)
    n = len(dparts)
    nk = t // tk

    def body(*refs):
        h_ref, d_refs, o_ref, acc_ref = refs[0], refs[1:1 + n], refs[1 + n], refs[2 + n]
        k = pl.program_id(0)
        h = h_ref[...]

        @pl.when(k == 0)
        def _():
            acc_ref[...] = jnp.zeros_like(acc_ref)

        for (p, po, j, jo, wd) in segs:
            acc_ref[j, :, jo:jo + wd] += _dot(h, d_refs[p][:, po:po + wd])

        @pl.when(k == nk - 1)
        def _():
            o_ref[...] = acc_ref[...].astype(o_ref.dtype)

    return pl.pallas_call(
        body, grid=(nk,),
        in_specs=[pl.BlockSpec((D, tk), lambda k: (0, k))]
        + [pl.BlockSpec((tk, wd), lambda k: (k, 0)) for wd in widths],
        out_specs=pl.BlockSpec((N_CHIPS, D, nc), lambda k: (0, 0, 0)),
        out_shape=S((N_CHIPS, D, nc), MXU_DTYPE), name=name,
        scratch_shapes=[pltpu.VMEM((N_CHIPS, D, nc), F32)], compiler_params=_cp("arbitrary"))(ht, *dparts)


def _dw_norm_cols(ht, dy, nc, name):
    t = ht.shape[1]
    tk = _row_tile(t, 2048)
    nk = t // tk

    def body(h_ref, dy_ref, o_ref, acc_ref):
        k = pl.program_id(1)

        @pl.when(k == 0)
        def _():
            acc_ref[...] = jnp.zeros_like(acc_ref)

        acc_ref[...] += _dot(h_ref[...], dy_ref[...])

        @pl.when(k == nk - 1)
        def _():
            o_ref[...] = acc_ref[...].astype(o_ref.dtype)

    return pl.pallas_call(
        body, grid=(N_CHIPS, nk),
        in_specs=[pl.BlockSpec((D, tk), lambda j, k: (0, k)), pl.BlockSpec((tk, nc), lambda j, k: (k, j))],
        out_specs=pl.BlockSpec((None, D, nc), lambda j, k: (j, 0, 0)),
        out_shape=S((N_CHIPS, D, nc), MXU_DTYPE), name=name,
        scratch_shapes=[pltpu.VMEM((D, nc), F32)], compiler_params=_cp("parallel", "arbitrary"))(ht, dy)


def _dw_parts(parts, dy, name):
    t = dy.shape[0]
    tk = _row_tile(t, 1024)
    widths = [p.shape[1] for p in parts]
    offs = np.cumsum([0] + widths)
    ktot = int(offs[-1])
    n = len(parts)
    nk = t // tk

    def body(*refs):
        p_refs, dy_ref, o_ref, acc_ref = refs[:n], refs[n], refs[n + 1], refs[n + 2]
        k = pl.program_id(0)

        @pl.when(k == 0)
        def _():
            acc_ref[...] = jnp.zeros_like(acc_ref)

        dyv = _mx(dy_ref[...])
        for p in range(n):
            acc_ref[int(offs[p]):int(offs[p + 1]), :] += _dot_tn(p_refs[p][...], dyv)

        @pl.when(k == nk - 1)
        def _():
            o_ref[...] = acc_ref[...].astype(o_ref.dtype)

    return pl.pallas_call(
        body, grid=(nk,),
        in_specs=[pl.BlockSpec((tk, wd), lambda k: (k, 0)) for wd in widths]
        + [pl.BlockSpec((tk, D), lambda k: (k, 0))],
        out_specs=pl.BlockSpec((ktot, D), lambda k: (0, 0)),
        out_shape=S((ktot, D), MXU_DTYPE), name=name,
        scratch_shapes=[pltpu.VMEM((ktot, D), F32)], compiler_params=_cp("arbitrary"))(*parts, dy)


def _dw_swiglu(ff, dy, name):
    t = dy.shape[0]
    tk = _row_tile(t, 512)
    nk = t // tk

    def body(f_ref, dy_ref, o_ref, acc_ref):
        k = pl.program_id(0)

        @pl.when(k == 0)
        def _():
            acc_ref[...] = jnp.zeros_like(acc_ref)

        acc_ref[...] += _dot_tn(_swiglu(f_ref[...]), dy_ref[...])

        @pl.when(k == nk - 1)
        def _():
            o_ref[...] = acc_ref[...].astype(o_ref.dtype)

    return pl.pallas_call(
        body, grid=(nk,),
        in_specs=[pl.BlockSpec((tk, 2 * FFN_H), lambda k: (k, 0)), pl.BlockSpec((tk, D), lambda k: (k, 0))],
        out_specs=pl.BlockSpec((FFN_H, D), lambda k: (0, 0)),
        out_shape=S((FFN_H, D), MXU_DTYPE), name=name,
        scratch_shapes=[pltpu.VMEM((FFN_H, D), F32)], compiler_params=_cp("arbitrary"))(ff, dy)


def _tables(t):
    half = HD // 2
    inv_freq = ROPE_BASE ** (-jnp.arange(half, dtype=F32) / half)
    base = (jnp.arange(t // CH, dtype=F32) * CH)[:, None] * inv_freq[None, :]
    off = jnp.arange(CH, dtype=F32)[:, None] * inv_freq[None, :]
    cb, sb, co, so = jnp.cos(base)[:, None], jnp.sin(base)[:, None], jnp.cos(off)[None], jnp.sin(off)[None]
    cos = (cb * co - sb * so).reshape(t, half)
    sin = (sb * co + cb * so).reshape(t, half)
    tb = {"cos2": jnp.concatenate([cos, cos], axis=1), "sin2": jnp.concatenate([-sin, sin], axis=1)}
    gf = 1.0 - jnp.exp2(-5.0 - jnp.arange(HEADS, dtype=F32))
    lgf = jnp.log(gf)[:, None]
    lgb = jnp.log(gf[::-1])[:, None]
    idx = jnp.arange(CH, dtype=F32)
    diff = idx[:, None] - idx[None, :]
    dfwd = jnp.where(diff >= 0, jnp.exp(lgf[:, :, None] * jnp.where(diff >= 0, diff, 0.0)), 0.0)
    dbwd = jnp.where(diff < 0, jnp.exp(lgb[:, :, None] * jnp.where(diff < 0, -diff, 0.0)), 0.0)
    tb["dm"] = dfwd + dbwd
    tb["dmt"] = jnp.swapaxes(tb["dm"], 1, 2)

    def lanes(a):
        return jnp.repeat(a.T, HD, axis=1)

    tb["xif"] = lanes(jnp.exp(lgf * (idx + 1)))
    tb["zf"] = lanes(jnp.exp(lgf * (CH - 1 - idx)))
    tb["xib"] = lanes(jnp.exp(lgb * (CH - idx)))
    tb["zb"] = lanes(jnp.exp(lgb * idx))
    tb["gcf"] = jnp.repeat(jnp.exp(lgf * CH), HD, axis=0).reshape(1, HEADS * HD)
    tb["gcb"] = jnp.repeat(jnp.exp(lgb * CH), HD, axis=0).reshape(1, HEADS * HD)
    return tb


def _full(shape):
    nd = len(shape)
    return pl.BlockSpec(shape, lambda *_: (0,) * nd)


def _gm_mixed(vn, ws_ref, bias):
    lane = lax.broadcasted_iota(jnp.int32, (CH, 128), 1)
    halves = []
    for hf in range(2):
        vh = _mx(vn[:, hf * 128:(hf + 1) * 128])
        r0 = jnp.dot(_mx(ws_ref[2 * hf]), vh, preferred_element_type=F32)
        r1 = jnp.dot(_mx(ws_ref[2 * hf + 1]), vh, preferred_element_type=F32)
        halves.append(jnp.where(lane < 64, r0, r1))
    return jnp.concatenate(halves, axis=1) + bias


def _gm_fwd(proj, ln_g, ln_b, ws, bias, name, riders=()):
    t = proj.shape[0]
    tm = _row_tile(t, 512)

    def body(pu_ref, pv_ref, g_ref, b_ref, ws_ref, bias_ref, o_ref):
        for c in range(tm // CH):
            rows = slice(c * CH, (c + 1) * CH)
            u = _gelu(pu_ref[rows, :])
            o, _ = _standardize(_gelu(pv_ref[rows, :]))
            vn = o * g_ref[...] + b_ref[...]
            o_ref[rows, :] = (u * _gm_mixed(vn, ws_ref, bias_ref[...])).astype(o_ref.dtype)

    (out,), rid = _pcall(
        body, [proj, proj, ln_g, ln_b, ws, bias], riders, grid=(t // tm,),
        in_specs=[pl.BlockSpec((tm, GM_W), lambda i: (i, 0)), pl.BlockSpec((tm, GM_W), lambda i: (i, 1)),
                  _full((1, GM_W)), _full((1, GM_W)), _full((GM_HEADS, CH, CH)), _full((CH, GM_W))],
        out_specs=[pl.BlockSpec((tm, GM_W), lambda i: (i, 0))],
        out_shape=[S((t, GM_W), ACT_DTYPE)], name=name, sem=("parallel",))
    return out, rid


def _gm_bwd(proj, dy, ln_g, ln_b, ws, wst, bias, name):
    t = proj.shape[0]
    tm = _row_tile(t, 512)
    nb = t // tm

    def body(pu_ref, pv_ref, dy_ref, g_ref, b_ref, ws_ref, wst_ref, bias_ref,
             d_ref, dws_ref, dbs_ref, dg_ref, db_ref, dbias_ref):
        first = pl.program_id(0) == 0
        lane = lax.broadcasted_iota(jnp.int32, (CH, 128), 1)
        dws = [jnp.zeros((CH, CH), F32) for _ in range(GM_HEADS)]
        dbias = jnp.zeros((CH, GM_W), F32)
        dg = jnp.zeros((1, GM_W), F32)
        db = jnp.zeros((1, GM_W), F32)
        for c in range(tm // CH):
            rows = slice(c * CH, (c + 1) * CH)
            pu = pu_ref[rows, :]
            pv = pv_ref[rows, :]
            u = _gelu(pu)
            o, r = _standardize(_gelu(pv))
            vn = o * g_ref[...] + b_ref[...]
            mixed = _gm_mixed(vn, ws_ref, bias_ref[...])
            dyv = dy_ref[rows, :]
            d_ref[rows, :GM_W] = (dyv * mixed * _gelu_grad(pu)).astype(d_ref.dtype)
            dmixed = dyv * u
            dbias = dbias + dmixed
            dvn_halves = []
            for hf in range(2):
                dm = dmixed[:, hf * 128:(hf + 1) * 128]
                vh = vn[:, hf * 128:(hf + 1) * 128]
                dm0 = jnp.where(lane < 64, dm, 0.0)
                dm1 = dm - dm0
                dws[2 * hf] = dws[2 * hf] + _dot_nt(dm0, vh)
                dws[2 * hf + 1] = dws[2 * hf + 1] + _dot_nt(dm1, vh)
                t0 = _dot(wst_ref[2 * hf], dm)
                t1 = _dot(wst_ref[2 * hf + 1], dm)
                dvn_halves.append(jnp.where(lane < 64, t0, t1))
            dvn = jnp.concatenate(dvn_halves, axis=1)
            dg = dg + jnp.sum(dvn * o, axis=0, keepdims=True)
            db = db + jnp.sum(dvn, axis=0, keepdims=True)
            dv = _standardize_bwd(dvn * g_ref[...], o, r)
            d_ref[rows, GM_W:] = (dv * _gelu_grad(pv)).astype(d_ref.dtype)
        for h in range(GM_HEADS):
            _acc_out(dws_ref.at[h], dws[h], first)
        _acc_out(dbias_ref, dbias, first)
        _acc_out(dg_ref, dg, first)
        _acc_out(db_ref, db, first)

        @pl.when(pl.program_id(0) == nb - 1)
        def _():
            tot = dbias_ref[...]
            head = lax.broadcasted_iota(jnp.int32, (CH, GM_W), 1) // (GM_W // GM_HEADS)
            out = jnp.zeros((CH, 128), F32)
            for h in range(GM_HEADS):
                s = jnp.sum(jnp.where(head == h, tot, 0.0), axis=1, keepdims=True)
                out = jnp.where(lane == h, s, out)
            dbs_ref[...] = out

    return pl.pallas_call(
        body, grid=(nb,),
        in_specs=[pl.BlockSpec((tm, GM_W), lambda i: (i, 0)), pl.BlockSpec((tm, GM_W), lambda i: (i, 1)),
                  pl.BlockSpec((tm, GM_W), lambda i: (i, 0)),
                  _full((1, GM_W)), _full((1, GM_W)), _full((GM_HEADS, CH, CH)), _full((GM_HEADS, CH, CH)),
                  _full((CH, GM_W))],
        out_specs=[pl.BlockSpec((tm, 2 * GM_W), lambda i: (i, 0)), _full((GM_HEADS, CH, CH)), _full((CH, 128)),
                   _full((1, GM_W)), _full((1, GM_W))],
        out_shape=[S((t, 2 * GM_W), ACT_DTYPE), S((GM_HEADS, CH, CH), F32), S((CH, 128), F32),
                   S((1, GM_W), F32), S((1, GM_W), F32)],
        scratch_shapes=[pltpu.VMEM((CH, GM_W), F32)],
        name=name, compiler_params=_cp("arbitrary"))(proj, proj, dy, ln_g, ln_b, ws, wst, bias)


def _rot(x, cos2, sin2):
    return x * cos2 + pltpu.roll(x, HD // 2, 1) * sin2


def _rot_bwd(dx, cos2, sin2):
    return dx * cos2 + pltpu.roll(dx * sin2, HD // 2, 1)


Q_COL, K_COL, V_COL, GATE_COL = 1, 2, 3, 4


def _rotate_qk(o_ref, cos2, sin2):
    for col, scale in ((Q_COL, 1.0), (K_COL, HD ** -0.5)):
        for h in range(HEADS):
            cols = slice(col * RET_W + h * HD, col * RET_W + (h + 1) * HD)
            o_ref[:, cols] = _rot(o_ref[:, cols], cos2, sin2) * scale


def _ret_scan(lhs, lhs_col, rhs, rhs_col, lp, ls, gp, gs, name):
    t = lhs.shape[0]
    n = t // CH
    r = 8 if n % 8 == 0 else (4 if n % 4 == 0 else 1)
    ns = n // r

    def body(lp_ref, ls_ref, gp_ref, gs_ref, l1_ref, r1_ref, l2_ref, r2_ref, pre_ref, suf_ref, sp_ref, ss_ref):
        @pl.when(pl.program_id(0) == 0)
        def _():
            sp_ref[...] = jnp.zeros_like(sp_ref)
            ss_ref[...] = jnp.zeros_like(ss_ref)

        def kv(l_ref, r_ref, scale, rows):
            lv = l_ref[rows, :] * scale
            rv = r_ref[rows, :]
            return jnp.concatenate([_dot_tn(lv[:, h * HD:(h + 1) * HD], rv[:, h * HD:(h + 1) * HD])
                                    for h in range(HEADS)], axis=1)

        for j in range(r):
            pre_ref[j] = sp_ref[...].astype(pre_ref.dtype)
            sp_ref[...] = sp_ref[...] * gp_ref[...] + kv(l1_ref, r1_ref, lp_ref[...], slice(j * CH, (j + 1) * CH))
        for j in reversed(range(r)):
            suf_ref[j] = ss_ref[...].astype(suf_ref.dtype)
            ss_ref[...] = ss_ref[...] * gs_ref[...] + kv(l2_ref, r2_ref, ls_ref[...], slice(j * CH, (j + 1) * CH))

    w = HEADS * HD
    return pl.pallas_call(
        body, grid=(ns,),
        in_specs=[_full((CH, w)), _full((CH, w)), _full((1, w)), _full((1, w)),
                  pl.BlockSpec((r * CH, w), lambda s: (s, lhs_col)), pl.BlockSpec((r * CH, w), lambda s: (s, rhs_col)),
                  pl.BlockSpec((r * CH, w), lambda s: (ns - 1 - s, lhs_col)),
                  pl.BlockSpec((r * CH, w), lambda s: (ns - 1 - s, rhs_col))],
        out_specs=[pl.BlockSpec((r, HD, w), lambda s: (s, 0, 0)), pl.BlockSpec((r, HD, w), lambda s: (ns - 1 - s, 0, 0))],
        out_shape=[S((n, HD, w), MXU_DTYPE)] * 2, name=name,
        scratch_shapes=[pltpu.VMEM((HD, w), F32), pltpu.VMEM((HD, w), F32)],
        compiler_params=_cp("arbitrary"))(lp, ls, gp, gs, lhs, rhs, lhs, rhs)


def _ret_out(proj, qkv, sf, sb, tb, name, riders=()):
    t = proj.shape[0]
    r = 4 if (t // CH) % 4 == 0 else 1
    tm = r * CH
    w = HEADS * HD

    def body(rq_ref, rk_ref, v_ref, g_ref, sf_ref, sb_ref, dm_ref, xif_ref, xib_ref, a_ref, y_ref):
        for c in range(r):
            rows = slice(c * CH, (c + 1) * CH)
            for h in range(HEADS):
                cols = slice(h * HD, (h + 1) * HD)
                q = rq_ref[rows, cols]
                p = _dot_nt(q, rk_ref[rows, cols]) * dm_ref[h]
                a = (_dot(p, v_ref[rows, cols]) + _dot(q * xif_ref[:, cols], sf_ref[c, :, cols])
                     + _dot(q * xib_ref[:, cols], sb_ref[c, :, cols]))
                a_ref[rows, cols] = a
                o, _ = _standardize(a)
                gv = g_ref[rows, cols]
                y_ref[rows, cols] = (o * (gv * _sigmoid(gv))).astype(y_ref.dtype)

    (a, y), rid = _pcall(
        body, [qkv, qkv, qkv, proj, sf, sb, tb["dm"], tb["xif"], tb["xib"]], riders, grid=(t // tm,),
        in_specs=[pl.BlockSpec((tm, w), lambda i: (i, 0)), pl.BlockSpec((tm, w), lambda i: (i, 1)),
                  pl.BlockSpec((tm, w), lambda i: (i, 2)), pl.BlockSpec((tm, w), lambda i: (i, GATE_COL)),
                  pl.BlockSpec((r, HD, w), lambda i: (i, 0, 0)), pl.BlockSpec((r, HD, w), lambda i: (i, 0, 0)),
                  _full((HEADS, CH, CH)), _full((CH, w)), _full((CH, w))],
        out_specs=[pl.BlockSpec((tm, w), lambda i: (i, 0))] * 2,
        out_shape=[S((t, w), F32), S((t, w), ACT_DTYPE)], name=name, sem=("parallel",))
    return a, y, rid


def _ret_bwd_main(qkv, da, sf, sb, gf, gb, tb, name, riders=()):
    t = qkv.shape[0]
    r = 4 if (t // CH) % 4 == 0 else 1
    tm = r * CH
    w = HEADS * HD
    scale = HD ** -0.5

    def body(rq_ref, rk_ref, v_ref, da_ref, sf_ref, sb_ref, gf_ref, gb_ref, dm_ref, dmt_ref,
             xif_ref, xib_ref, zf_ref, zb_ref, c_ref, s_ref, o_ref):
        for c in range(r):
            rows = slice(c * CH, (c + 1) * CH)
            cos2, sin2 = c_ref[rows, :], s_ref[rows, :]
            for h in range(HEADS):
                cols = slice(h * HD, (h + 1) * HD)
                q, k, v, dav = rq_ref[rows, cols], rk_ref[rows, cols], v_ref[rows, cols], da_ref[rows, cols]
                qm, km, vm, dam = _mx(q), _mx(k), _mx(v), _mx(dav)
                dm, dmt = dm_ref[h], dmt_ref[h]
                pt = _dot_nt(km, qm) * dmt
                dp = _dot_nt(dam, vm) * dm
                dpt = _dot_nt(vm, dam) * dmt
                sfh, sbh, gfh, gbh = sf_ref[c, :, cols], sb_ref[c, :, cols], gf_ref[c, :, cols], gb_ref[c, :, cols]
                zf, zb = zf_ref[:, cols], zb_ref[:, cols]
                dv = _dot(pt, dam) + zf * _dot(km, gfh) + zb * _dot(km, gbh)
                drq = _dot(dp, km) + xif_ref[:, cols] * _dot_nt(dam, sfh) + xib_ref[:, cols] * _dot_nt(dam, sbh)
                drk = _dot(dpt, qm) + _dot_nt(zf * v, gfh) + _dot_nt(zb * v, gbh)
                o_ref[rows, h * HD:(h + 1) * HD] = _rot_bwd(drq, cos2, sin2).astype(o_ref.dtype)
                o_ref[rows, w + h * HD:w + (h + 1) * HD] = (_rot_bwd(drk, cos2, sin2) * scale).astype(o_ref.dtype)
                o_ref[rows, 2 * w + h * HD:2 * w + (h + 1) * HD] = dv.astype(o_ref.dtype)

    st = pl.BlockSpec((r, HD, w), lambda i: (i, 0, 0))
    (out,), rid = _pcall(
        body, [qkv, qkv, qkv, da, sf, sb, gf, gb, tb["dm"], tb["dmt"], tb["xif"], tb["xib"], tb["zf"], tb["zb"],
               tb["cos2"], tb["sin2"]], riders, grid=(t // tm,),
        in_specs=[pl.BlockSpec((tm, w), lambda i: (i, 0)), pl.BlockSpec((tm, w), lambda i: (i, 1)),
                  pl.BlockSpec((tm, w), lambda i: (i, 2)), pl.BlockSpec((tm, w), lambda i: (i, 0)),
                  st, st, st, st, _full((HEADS, CH, CH)), _full((HEADS, CH, CH)),
                  _full((CH, w)), _full((CH, w)), _full((CH, w)), _full((CH, w)),
                  pl.BlockSpec((tm, HD), lambda i: (i, 0)), pl.BlockSpec((tm, HD), lambda i: (i, 0))],
        out_specs=[pl.BlockSpec((tm, 3 * w), lambda i: (i, 0))],
        out_shape=[S((t, 3 * w), ACT_DTYPE)], name=name, sem=("parallel",))
    return out, rid


CONV_TM = 256
CONV_SUB = 64
A_COL = (2 * GM_W + 4 * RET_W) // CV_W
G_COL = A_COL + 1


def _halo_specs(t, tm, col):
    nb16 = t // HALO
    per = tm // HALO
    return [pl.BlockSpec((tm, CV_W), lambda i: (i, col)),
            pl.BlockSpec((HALO, CV_W), lambda i: (jnp.maximum(i * per - 1, 0), col)),
            pl.BlockSpec((HALO, CV_W), lambda i: (jnp.minimum((i + 1) * per, nb16 - 1), col))]


def _fill_padded(dst_ref, prev, main, nxt, tm, i, nb):
    dst_ref[0:HALO, :] = jnp.where(i > 0, prev, 0.0)
    dst_ref[HALO:HALO + tm, :] = main
    dst_ref[HALO + tm:2 * HALO + tm, :] = jnp.where(i < nb - 1, nxt, 0.0)


SUBLANES = 8


def _fill_shifted(sh_ref, src_ref, tm):
    n = tm + 2 * HALO - SUBLANES
    for b in range(SUBLANES):
        sh_ref[b, 0:n, :] = src_ref[pl.ds(b, n), :]


def _tap(sh_ref, off, rows):
    return sh_ref[off % SUBLANES, pl.ds(off - off % SUBLANES, rows), :]


def _conv_fwd(proj, cw, cb, ln_g, ln_b, name, riders=()):
    t = proj.shape[0]
    tm = _row_tile(t, CONV_TM)
    nb = t // tm

    def body(a_ref, ap_ref, an_ref, g_ref, gp_ref, gn_ref, w_ref, b_ref, lg_ref, lb_ref, y_ref, hc_ref,
             hp_ref, sh_ref):
        i = pl.program_id(0)
        _fill_padded(hp_ref, ap_ref[...] * _sigmoid(gp_ref[...]), a_ref[...] * _sigmoid(g_ref[...]),
                     an_ref[...] * _sigmoid(gn_ref[...]), tm, i, nb)
        _fill_shifted(sh_ref, hp_ref, tm)
        for sb in range(tm // CONV_SUB):
            acc = jnp.zeros((CONV_SUB, CV_W), F32) + b_ref[...]
            for k in range(KCONV):
                acc = acc + w_ref[k:k + 1, :] * _tap(sh_ref, sb * CONV_SUB + k + 1, CONV_SUB)
            rows = slice(sb * CONV_SUB, (sb + 1) * CONV_SUB)
            hc_ref[rows, :] = acc
            o, _ = _standardize(acc)
            z = o * lg_ref[...] + lb_ref[...]
            y_ref[rows, :] = (z * _sigmoid(z)).astype(y_ref.dtype)

    (y, hc), rid = _pcall(
        body, [proj, proj, proj, proj, proj, proj, cw, cb, ln_g, ln_b], riders, grid=(nb,),
        in_specs=_halo_specs(t, tm, A_COL) + _halo_specs(t, tm, G_COL)
        + [_full((32, CV_W)), _full((1, CV_W)), _full((1, CV_W)), _full((1, CV_W))],
        out_specs=[pl.BlockSpec((tm, CV_W), lambda i: (i, 0))] * 2,
        out_shape=[S((t, CV_W), ACT_DTYPE), S((t, CV_W), F32)], name=name, sem=("parallel",),
        scratch_shapes=[pltpu.VMEM((tm + 2 * HALO, CV_W), F32), pltpu.VMEM((SUBLANES, tm + 2 * HALO, CV_W), F32)])
    return y, hc, rid


def _conv_bwd(proj, dy, hc, cw, ln_g, ln_b, name, riders=()):
    t = proj.shape[0]
    tm = _row_tile(t, CONV_TM)
    nb = t // tm

    def body(a_ref, ap_ref, an_ref, g_ref, gp_ref, gn_ref, dy_ref, dyp_ref, dyn_ref, hc_ref, hcp_ref, hcn_ref,
             w_ref, lg_ref, lb_ref, d_ref, dw_ref, dcb_ref, dlg_ref, dlb_ref, hp_ref, dhp_ref, dwacc_ref,
             sh_ref, dsh_ref):
        i = pl.program_id(0)
        first = i == 0

        def dhc_of(dyv, hcv):
            o, r = _standardize(hcv)
            z = o * lg_ref[...] + lb_ref[...]
            s = _sigmoid(z)
            dz = dyv * (s * (1.0 + z * (1.0 - s)))
            return _standardize_bwd(dz * lg_ref[...], o, r), dz, o

        dhc, dz, o = dhc_of(dy_ref[...], hc_ref[...])
        _acc_out(dlg_ref, jnp.sum(dz * o, axis=0, keepdims=True), first)
        _acc_out(dlb_ref, jnp.sum(dz, axis=0, keepdims=True), first)
        _acc_out(dcb_ref, jnp.sum(dhc, axis=0, keepdims=True), first)
        _fill_padded(dhp_ref, dhc_of(dyp_ref[...], hcp_ref[...])[0], dhc, dhc_of(dyn_ref[...], hcn_ref[...])[0],
                     tm, i, nb)
        _fill_padded(hp_ref, ap_ref[...] * _sigmoid(gp_ref[...]), a_ref[...] * _sigmoid(g_ref[...]),
                     an_ref[...] * _sigmoid(gn_ref[...]), tm, i, nb)

        _fill_shifted(sh_ref, hp_ref, tm)
        _fill_shifted(dsh_ref, dhp_ref, tm)

        @pl.when(first)
        def _():
            dwacc_ref[...] = jnp.zeros_like(dwacc_ref)

        for sb in range(tm // CONV_SUB):
            base = sb * CONV_SUB
            dmain = dhp_ref[pl.ds(HALO + base, CONV_SUB), :]
            dh = jnp.zeros((CONV_SUB, CV_W), F32)
            for k in range(KCONV):
                dh = dh + w_ref[k:k + 1, :] * _tap(dsh_ref, base + 2 * HALO - 1 - k, CONV_SUB)
                prod = dmain * _tap(sh_ref, base + k + 1, CONV_SUB)
                dwacc_ref[k * 8:(k + 1) * 8, :] += jnp.sum(prod.reshape(CONV_SUB // 8, 8, CV_W), axis=0)
            rows = slice(base, base + CONV_SUB)
            s = _sigmoid(g_ref[rows, :])
            d_ref[rows, :CV_W] = (dh * s).astype(d_ref.dtype)
            d_ref[rows, CV_W:] = (dh * a_ref[rows, :] * (s * (1.0 - s))).astype(d_ref.dtype)

        @pl.when(i == nb - 1)
        def _():
            for k in range(KCONV):
                dw_ref[k:k + 1, :] = jnp.sum(dwacc_ref[k * 8:(k + 1) * 8, :], axis=0, keepdims=True)
            dw_ref[KCONV:32, :] = jnp.zeros((32 - KCONV, CV_W), F32)

    hs = [pl.BlockSpec((tm, CV_W), lambda i: (i, 0)),
          pl.BlockSpec((HALO, CV_W), lambda i: (jnp.maximum(i * (tm // HALO) - 1, 0), 0)),
          pl.BlockSpec((HALO, CV_W), lambda i: (jnp.minimum((i + 1) * (tm // HALO), t // HALO - 1), 0))]
    outs, rid = _pcall(
        body, [proj, proj, proj, proj, proj, proj, dy, dy, dy, hc, hc, hc, cw, ln_g, ln_b], riders, grid=(nb,),
        in_specs=_halo_specs(t, tm, A_COL) + _halo_specs(t, tm, G_COL) + hs + hs
        + [_full((32, CV_W)), _full((1, CV_W)), _full((1, CV_W))],
        out_specs=[pl.BlockSpec((tm, 2 * CV_W), lambda i: (i, 0)), _full((32, CV_W)), _full((1, CV_W)),
                   _full((1, CV_W)), _full((1, CV_W))],
        out_shape=[S((t, 2 * CV_W), ACT_DTYPE), S((32, CV_W), F32), S((1, CV_W), F32), S((1, CV_W), F32),
                   S((1, CV_W), F32)],
        name=name, sem=("arbitrary",),
        scratch_shapes=[pltpu.VMEM((tm + 2 * HALO, CV_W), F32), pltpu.VMEM((tm + 2 * HALO, CV_W), F32),
                        pltpu.VMEM((32 * 8, CV_W), F32), pltpu.VMEM((SUBLANES, tm + 2 * HALO, CV_W), F32),
                        pltpu.VMEM((SUBLANES, tm + 2 * HALO, CV_W), F32)])
    return (*outs, rid)


def _loss_head(x, g, target, name):
    t = x.shape[0]
    tm = _row_tile(t, 512)

    def body(x_ref, g_ref, t_ref, dx_ref, dg_ref, l_ref):
        first = pl.program_id(0) == 0
        xv = x_ref[...]
        r = _rms_r(xv)
        e = xv * r * g_ref[...] - t_ref[...]
        dx, dgrow = _rms_bwd(e * (1.0 / D), xv, r, g_ref[...])
        dx_ref[...] = dx
        _acc_out(dg_ref, jnp.sum(dgrow, axis=0, keepdims=True), first)
        part = 0.5 * jnp.sum(jnp.mean(e * e, axis=-1, keepdims=True), axis=0, keepdims=True)
        _acc_out(l_ref, jnp.broadcast_to(part, (8, 128)), first)

    return pl.pallas_call(
        body, grid=(t // tm,),
        in_specs=[pl.BlockSpec((tm, D), lambda i: (i, 0)), _full((1, D)), pl.BlockSpec((tm, D), lambda i: (i, 0))],
        out_specs=[pl.BlockSpec((tm, D), lambda i: (i, 0)), _full((1, D)), _full((8, 128))],
        out_shape=[S((t, D), F32), S((1, D), F32), S((8, 128), F32)], name=name,
        compiler_params=_cp("arbitrary"))(x, g, target)


def _as2d(a):
    return a.reshape(-1, a.shape[-1])


def _ew_tile(rows, cols, n_arrays):
    budget = VMEM_LIMIT // 2
    tr = rows
    while tr * cols * 4 * n_arrays * 2 > budget and tr % 16 == 0:
        tr //= 2
    assert rows % tr == 0
    return tr


def _adamw(w, g, m, v, name, pass_g=False):
    shape = w.shape
    w2, g2, m2, v2 = _as2d(w), _as2d(g), _as2d(m), _as2d(v)
    rows, cols = w2.shape
    tr = _ew_tile(rows, cols, 7 + pass_g)

    def body(w_ref, g_ref, m_ref, v_ref, d_ref, nm_ref, nv_ref, *g_out):
        gv = g_ref[...]
        for r in g_out:
            r[...] = gv
        nm = ADAM_B1 * m_ref[...] + (1.0 - ADAM_B1) * gv
        nv = ADAM_B2 * v_ref[...] + (1.0 - ADAM_B2) * (gv * gv)
        m_hat = nm / (1.0 - ADAM_B1 ** ADAM_STEP)
        v_hat = nv / (1.0 - ADAM_B2 ** ADAM_STEP)
        d_ref[...] = -ADAM_LR * (m_hat / (jnp.sqrt(v_hat) + ADAM_EPS) + ADAM_WD * w_ref[...])
        nm_ref[...] = nm
        nv_ref[...] = nv

    spec = pl.BlockSpec((tr, cols), lambda i: (i, 0))
    outs = pl.pallas_call(body, grid=(rows // tr,), in_specs=[spec] * 4, out_specs=[spec] * (3 + pass_g),
                          out_shape=[S((rows, cols), F32)] * (3 + pass_g), name=name,
                          compiler_params=_cp("parallel"))(w2, g2, m2, v2)
    return tuple(o.reshape(shape) for o in outs)


BIG = (("w_in", "col"), ("w_out", "row"), ("w_ffn_in", "col"), ("w_ffn_out", "row"))
NBIG = len(BIG)


def _cast_to_gathered(w, l, me, name):
    _, r_, c_ = w.shape
    tr = _ew_tile(r_, c_, 2)

    def body(me_ref, w_ref, o_ref):
        o_ref[...] = w_ref[...].astype(o_ref.dtype)

    gs = pltpu.PrefetchScalarGridSpec(
        num_scalar_prefetch=1, grid=(r_ // tr,),
        in_specs=[pl.BlockSpec((None, tr, c_), lambda i, s: (l, i, 0))],
        out_specs=pl.BlockSpec((None, tr, c_), lambda i, s: (s[0], i, 0)))
    out = pl.pallas_call(body, grid_spec=gs, out_shape=S((N_CHIPS, r_, c_), MXU_DTYPE), name=name,
                         compiler_params=_cp("parallel"))(me.reshape(1), w)
    return out.reshape(N_CHIPS, 2, r_ // 2, c_)


def _all_gather(bufs, name, per_core=False):
    n = len(bufs)

    def body(*refs):
        i_refs, o_refs = refs[:n], refs[n:2 * n]
        isend, irecv, dsend, drecv, osend, orecv = refs[2 * n:]
        pos = _mesh_pos()
        x, y, c, me, _, _ = pos
        ici = _rider_copies("ici", i_refs, o_refs, isend, irecv, pos)
        d2d = _rider_copies("d2d", o_refs, o_refs, dsend, drecv, pos)
        own = []
        if per_core:
            for b in range(n):
                own.append(tuple(pltpu.make_async_remote_copy(
                    src_ref=s_, dst_ref=d_, send_sem=osend.at[b], recv_sem=orecv.at[b],
                    device_id=(x, y, 1 - c), device_id_type=MESH)
                    for s_, d_ in ((i_refs[b].at[me, c], o_refs[b].at[me, c]),
                                   (o_refs[b].at[me, 1 - c], o_refs[b].at[me, 1 - c]))))
        for cp, _ in ici + own:
            cp.start()
        for (_, land), (fwd, _) in zip(ici, d2d):
            land.wait_recv()
            fwd.start()
        for _, land in d2d + own:
            land.wait_recv()
        for cp, _ in ici + d2d + own:
            cp.wait_send()

    return pl.pallas_call(
        body, in_specs=[ANY] * n, out_specs=[ANY] * n, out_shape=[S(a.shape, a.dtype) for a in bufs],
        input_output_aliases={w: w for w in range(n)}, name=name,
        scratch_shapes=[pltpu.SemaphoreType.DMA((n, 3))] * 4 + [pltpu.SemaphoreType.DMA((n,))] * 2)(*bufs)


def _pair_exchange(grads, name):
    n = len(grads)

    def body(*refs):
        g_refs, theirs = refs[:n], refs[n:2 * n]
        send, recv = refs[2 * n:]
        x, y, c, *_ = _mesh_pos()
        cps = []
        for w in range(n):
            cp = pltpu.make_async_remote_copy(
                src_ref=g_refs[w].at[:, 1 - c], dst_ref=theirs[w], send_sem=send.at[w], recv_sem=recv.at[w],
                device_id=(x, y, 1 - c), device_id_type=MESH)
            cp.start()
            cps.append(cp)
        for cp in cps:
            cp.wait()

    return pl.pallas_call(
        body, in_specs=[ANY] * n, out_specs=[ANY] * n,
        out_shape=[S(a.shape[:1] + a.shape[2:], a.dtype) for a in grads], name=name,
        scratch_shapes=[pltpu.SemaphoreType.DMA((n,))] * 2)(*grads)


def _pair_sum(g, theirs, core, name):
    _, _, rh, c_ = g.shape
    tr = _ew_tile(rh, c_, 2)

    def body(s_ref, g_ref, t_ref, o_ref):
        o_ref[...] = (g_ref[...].astype(F32) + t_ref[...].astype(F32)).astype(o_ref.dtype)

    blk = pl.BlockSpec((None, tr, c_), lambda j, i, s: (j, i, 0))
    gs = pltpu.PrefetchScalarGridSpec(
        num_scalar_prefetch=1, grid=(N_CHIPS, rh // tr),
        in_specs=[pl.BlockSpec((None, None, tr, c_), lambda j, i, s: (j, s[0], i, 0)), blk], out_specs=blk)
    return pl.pallas_call(body, grid_spec=gs, out_shape=S(theirs.shape, theirs.dtype), name=name,
                          compiler_params=_cp("parallel", "parallel"))(core.reshape(1), g, theirs)


def _chip_sum(q, got, l, me, core, into, name):
    _, rh, c_ = got.shape
    tr = _ew_tile(rh, c_, 4)

    def body(s_ref, q_ref, g0_ref, g1_ref, g2_ref, o_ref):
        acc = q_ref[...].astype(F32)
        for r in (g0_ref, g1_ref, g2_ref):
            acc = acc + r[...].astype(F32)
        o_ref[...] = acc

    in_specs = [pl.BlockSpec((None, tr, c_), lambda i, s: (s[0], i, 0))] + [
        pl.BlockSpec((None, tr, c_), functools.partial(lambda k, i, s: (k, i, 0), k)) for k in range(3)]
    return _call_into(
        body, into, in_specs, [jnp.stack([me, core]), q, got, got, got], n_prefetch=1, grid=(rh // tr,),
        out_specs=pl.BlockSpec((None, None, tr, c_), lambda i, s: (l, s[1], i, 0)),
        out_shape=S((DEPTH, 2, rh, c_), F32), name=name, compiler_params=_cp("parallel"))


def _pair_gather(gs4):
    def body(*refs):
        i_refs, o_refs = refs[:NBIG], refs[NBIG:2 * NBIG]
        send, recv = refs[2 * NBIG:]
        x, y, c, *_ = _mesh_pos()
        cps = []
        for w in range(NBIG):
            cp = pltpu.make_async_remote_copy(
                src_ref=i_refs[w].at[:, c], dst_ref=o_refs[w].at[:, c], send_sem=send.at[w], recv_sem=recv.at[w],
                device_id=(x, y, 1 - c), device_id_type=MESH)
            cp.start()
            cps.append(cp)
        for cp in cps:
            cp.wait()

    outs = pl.pallas_call(
        body, in_specs=[ANY] * NBIG, out_specs=[ANY] * NBIG, out_shape=[S(a.shape, a.dtype) for a in gs4],
        input_output_aliases={w: w for w in range(NBIG)}, name="grad_pair_gather",
        scratch_shapes=[pltpu.SemaphoreType.DMA((NBIG,))] * 2)(*gs4)
    return [o.reshape(o.shape[0], 2 * o.shape[2], o.shape[3]) for o in outs]


def _all_reduce_small(p, me, core, name):
    rows = p.shape[0]

    def place(s_ref, p_ref, o_ref):
        o_ref[...] = p_ref[...]

    gs = pltpu.PrefetchScalarGridSpec(
        num_scalar_prefetch=1, grid=(1,), in_specs=[pl.BlockSpec((rows, 128), lambda i, s: (0, 0))],
        out_specs=pl.BlockSpec((None, None, rows, 128), lambda i, s: (s[0], s[1], 0, 0)))
    mine = pl.pallas_call(place, grid_spec=gs, out_shape=S((N_CHIPS, 2, rows, 128), F32), name=name + "_place",
                          compiler_params=_cp("arbitrary"))(jnp.stack([me, core]), p)
    parts = _all_gather([mine], name + "_gather", per_core=True)[0]

    def total(g_ref, o_ref):
        acc = g_ref[0, 0]
        for j in range(N_CHIPS):
            for c in range(2):
                if (j, c) != (0, 0):
                    acc = acc + g_ref[j, c]
        o_ref[...] = acc

    vm = pl.BlockSpec(memory_space=pltpu.VMEM)
    return pl.pallas_call(total, in_specs=[vm], out_specs=vm, out_shape=S((rows, 128), F32), name=name + "_sum",
                          compiler_params=pltpu.CompilerParams(vmem_limit_bytes=VMEM_LIMIT))(parts)


PACK_UNIT = 8 * 128


def _pack(arrs):
    parts = []
    for a in arrs:
        flat = a.reshape(-1)
        pad = (-flat.shape[0]) % PACK_UNIT
        parts.append(jnp.pad(flat, (0, pad)).reshape(-1, 128))
    return jnp.concatenate(parts, axis=0)


def _unpack(buf, shapes):
    outs, row = [], 0
    for shp in shapes:
        n = int(np.prod(shp))
        rows = -(-n // PACK_UNIT) * 8
        outs.append(buf[row:row + rows].reshape(-1)[:n].reshape(shp))
        row += rows
    return outs


SMALL = ("norm1_g", "gm_ln_g", "gm_ln_b", "gm_ws", "gm_bs", "conv_w", "conv_b", "conv_ln_g", "conv_ln_b",
         "norm2_g", "final_g")
WEIGHTS = ("norm1_g", "w_in", "gm_ln_g", "gm_ln_b", "gm_ws", "gm_bs", "conv_w", "conv_b", "conv_ln_g",
           "conv_ln_b", "w_out", "norm2_g", "w_ffn_in", "w_ffn_out", "final_g")


def kernel(x, norm1_g, w_in, gm_ln_g, gm_ln_b, gm_ws, gm_bs, conv_w, conv_b, conv_ln_g, conv_ln_b, w_out, norm2_g, w_ffn_in, w_ffn_out, final_g, loss_target, m_norm1_g, m_w_in, m_gm_ln_g, m_gm_ln_b, m_gm_ws, m_gm_bs, m_conv_w, m_conv_b, m_conv_ln_g, m_conv_ln_b, m_w_out, m_norm2_g, m_w_ffn_in, m_w_ffn_out, m_final_g, v_norm1_g, v_w_in, v_gm_ln_g, v_gm_ln_b, v_gm_ws, v_gm_bs, v_conv_w, v_conv_b, v_conv_ln_g, v_conv_ln_b, v_w_out, v_norm2_g, v_w_ffn_in, v_w_ffn_out, v_final_g):
    given = dict(locals())
    t = x.shape[1]
    xc = x.reshape(t, D)
    target = loss_target.reshape(t, D)
    me = 2 * lax.axis_index("x") + lax.axis_index("y")
    core = lax.axis_index("c")
    tb = _tables(t)

    me = me.astype(jnp.int32)
    core = core.astype(jnp.int32)
    names = [n for n, _ in BIG]
    kinds = dict(BIG)
    gathered = [{n: _cast_to_gathered(given[n], l, me, f"cast_{n}{l}") for n in names} for l in range(DEPTH)]
    gathered[0]["w_in"] = _all_gather([gathered[0]["w_in"]], "all_gather_w_in0")[0]

    def weight(l, n):
        b = gathered[l][n]
        r_, c_ = 2 * b.shape[2], b.shape[3]
        return b.reshape(N_CHIPS, r_, c_) if kinds[n] == "col" else b.reshape(N_CHIPS * r_, c_)

    cshard = CV_W // N_CHIPS
    placed = lax.dynamic_update_slice(jnp.zeros((DEPTH, KCONV, CV_W), F32),
                                      conv_w * (core == 0).astype(F32), (0, 0, me * cshard))
    conv_w_full = _unpack(_all_reduce_small(_pack([placed]), me, core, "gather_conv_w"), [(DEPTH, KCONV, CV_W)])[0]
    cw32 = jnp.pad(conv_w_full, ((0, 0), (0, 32 - KCONV), (0, 0)))

    def row(a, l):
        return a[l].reshape(1, -1)

    saved = []
    early = ["w_in", "w_out", "w_ffn_in"]
    for l in range(DEPTH):
        cur = gathered[l]
        nxt = gathered[l + 1] if l + 1 < DEPTH else None
        sv = {"x": xc}
        bias = jnp.repeat(gm_bs[l].T, GM_W // GM_HEADS, axis=1)
        first = ["w_ffn_in"] if l == 0 else ["w_ffn_out"]
        late = ["w_out", "w_ffn_out"]
        proj, h1t, qkv, rid = _norm_mm(xc, row(norm1_g, l), weight(l, "w_in"), F32, f"in_proj{l}", 512,
                                       [("ici" if l == 0 else "d2d", [cur[n] for n in first])],
                                       (tb["cos2"], tb["sin2"]))
        cur.update(zip(first, rid))
        y_gm, rid = _gm_fwd(proj, row(gm_ln_g, l), row(gm_ln_b, l), gm_ws[l], bias, f"gm_fwd{l}",
                            [("d2d", [cur[n] for n in first]), ("ici", [cur[late[0]]])] if l == 0 else ())
        cur.update(zip(first + late[:1], rid))
        sf, sb = _ret_scan(qkv, 1, qkv, 2, tb["zf"], tb["zb"], tb["gcf"], tb["gcb"], f"ret_state{l}")
        a, y_ret, rid = _ret_out(proj, qkv, sf, sb, tb, f"ret_out{l}",
                                 [("d2d", [cur[late[0]]]), ("ici", [cur[late[1]]])] if l == 0 else ())
        cur.update(zip(late, rid))
        y_cv, hc, rid = _conv_fwd(proj, cw32[l], row(conv_b, l), row(conv_ln_g, l), row(conv_ln_b, l),
                                  f"conv_fwd{l}", [("d2d", [cur[late[1]]])] if l == 0 else ())
        cur.update(zip(late[1:], rid))
        x_mid = _parts_mm_res([y_gm, y_ret, y_cv], weight(l, "w_out"), xc, f"out_proj{l}")
        ff, h2t, _, rid = _norm_mm(x_mid, row(norm2_g, l), weight(l, "w_ffn_in"), ACT_DTYPE, f"ffn_in{l}", 512,
                                   [("ici", [nxt[n] for n in early])] if nxt else ())
        if nxt:
            nxt.update(zip(early, rid))
        xc, rid = _swiglu_mm_res(ff, weight(l, "w_ffn_out"), x_mid, f"ffn_out{l}",
                                 [("d2d", [nxt[n] for n in early]), ("ici", [nxt["w_ffn_out"]])] if nxt else ())
        if nxt:
            nxt.update(zip(early + ["w_ffn_out"], rid))
        sv.update(bias=bias, proj=proj, qkv=qkv, h1t=h1t, h2t=h2t, y_gm=y_gm, sf=sf, sb=sb, a=a, y_ret=y_ret,
                  y_cv=y_cv,
                  hc=hc, x_mid=x_mid,
                  ff=ff)
        saved.append(sv)

    dx, d_final_g, lpart = _loss_head(xc, final_g.reshape(1, D), target, "loss_head")

    small_g = {n: [None] * DEPTH for n in SMALL}
    qs = [{} for _ in range(DEPTH)]
    got = [{} for _ in range(DEPTH)]
    ffn_w, mix_w = ["w_ffn_out", "w_ffn_in"], ["w_out", "w_in"]

    def halves(big_g, group):
        return [big_g[n].reshape(N_CHIPS, 2, given[n].shape[1] // 2, given[n].shape[2]) for n in group]

    def pair_sums(l, group, g4, theirs):
        qs[l].update({n: _pair_sum(g, th, core, f"pair_sum_{n}{l}") for n, g, th in zip(group, g4, theirs)})
        return [qs[l][n] for n in group]

    for l in reversed(range(DEPTH)):
        sv = saved[l]
        proj = sv["proj"]
        big_g = {}
        dff = _dx_swiglu(dx, weight(l, "w_ffn_out"), sv["ff"], f"ffn_out_dx{l}")
        big_g["w_ffn_out"] = _dw_swiglu(sv["ff"], dx, f"ffn_out_dw{l}")
        dx_mid, dg2, _ = _dx_norm([dff], weight(l, "w_ffn_in"), sv["x_mid"], row(norm2_g, l), dx,
                                  f"ffn_in_dx{l}", 512)
        big_g["w_ffn_in"] = _dw_norm_cols(sv["h2t"], dff, w_ffn_in.shape[2], f"ffn_in_dw{l}")
        g4 = halves(big_g, ffn_w)
        (dy_gm, da, d_g, dy_cv), theirs = _out_proj_dx(dx_mid, weight(l, "w_out"), sv["a"], proj,
                                                       f"out_proj_dx{l}", [("pairx", g4)])
        q_ffn = pair_sums(l, ffn_w, g4, theirs)
        big_g["w_out"] = _dw_parts([sv["y_gm"], sv["y_ret"], sv["y_cv"]], dx_mid, f"out_proj_dw{l}")
        d_cv, dcw, dcb, dclg, dclb, rid = _conv_bwd(proj, dy_cv, sv["hc"], cw32[l], row(conv_ln_g, l),
                                                    row(conv_ln_b, l), f"conv_bwd{l}", [("scatter", q_ffn[:1])])
        got[l].update(zip(ffn_w[:1], rid))
        gb_, gf_ = _ret_scan(sv["qkv"], 0, da, 0, tb["xib"], tb["xif"], tb["gcb"], tb["gcf"], f"ret_bwd_state{l}")
        d_qkv, rid = _ret_bwd_main(sv["qkv"], da, sv["sf"], sv["sb"], gf_, gb_, tb, f"ret_bwd_main{l}",
                                   [("scatter", q_ffn[1:])])
        got[l].update(zip(ffn_w[1:], rid))
        d_gm, dws, dbs, dglg, dglb = _gm_bwd(proj, dy_gm, row(gm_ln_g, l), row(gm_ln_b, l), gm_ws[l],
                                             jnp.swapaxes(gm_ws[l], 1, 2), sv["bias"], f"gm_bwd{l}")
        dparts = [d_gm, d_qkv, d_g, d_cv]
        big_g["w_in"] = _dw_norm_parts(sv["h1t"], dparts, w_in.shape[2], f"in_proj_dw{l}")
        g4 = halves(big_g, mix_w)
        q_mix = pair_sums(l, mix_w, g4, _pair_exchange(g4, f"grad_pair_exchange_mix{l}"))
        dx, dg1, rid = _dx_norm(dparts, weight(l, "w_in"), sv["x"], row(norm1_g, l), dx_mid, f"in_proj_dx{l}", 512,
                                [("scatter", q_mix)])
        got[l].update(zip(mix_w, rid))
        for n, val in (("norm1_g", dg1[0]), ("gm_ln_g", dglg[0]), ("gm_ln_b", dglb[0]), ("gm_ws", dws),
                       ("gm_bs", dbs[:, :GM_HEADS].T), ("conv_w", dcw[:KCONV]), ("conv_b", dcb[0]),
                       ("conv_ln_g", dclg[0]), ("conv_ln_b", dclb[0]), ("norm2_g", dg2[0])):
            small_g[n][l] = val

    small_shapes = [given[n].shape if n != "conv_w" else (DEPTH, KCONV, CV_W) for n in SMALL]
    partials = [d_final_g[0] if n == "final_g" else jnp.stack(small_g[n]) for n in SMALL]
    summed = _unpack(_all_reduce_small(_pack(partials + [lpart]), me, core, "all_reduce_small_grads"),
                     small_shapes + [lpart.shape])
    loss = summed[-1][0, 0]
    reduced = dict(zip(SMALL, summed))
    reduced["conv_w"] = lax.dynamic_slice(reduced["conv_w"], (0, 0, me * cshard), (DEPTH, KCONV, cshard))

    halves = [None] * NBIG
    for l in reversed(range(DEPTH)):
        halves = [_chip_sum(qs[l][n], got[l][n], l, me, core, h, f"chip_sum_{n}{l}") for n, h in zip(names, halves)]
    grads = dict(zip(names, _pair_gather(halves)))
    grads.update(reduced)

    delta, new_m, new_v = {}, {}, {}
    for n, _ in BIG:
        delta[n], new_m[n], new_v[n], grads[n] = _adamw(given[n], grads[n], given["m_" + n], given["v_" + n],
                                                        f"adamw_{n}", pass_g=True)
    shapes = [given[n].shape for n in SMALL]
    packed = [_pack([src[n] if src is grads else src[p + n] for n in SMALL])
              for src, p in ((given, ""), (grads, ""), (given, "m_"), (given, "v_"))]
    outs = _adamw(*packed, "adamw_small")
    for dst, buf in zip((delta, new_m, new_v), outs):
        dst.update(zip(SMALL, _unpack(buf, shapes)))

    return (loss, dx.reshape(1, t, D), *[grads[n] for n in WEIGHTS], *[delta[n] for n in WEIGHTS],
            *[new_m[n] for n in WEIGHTS], *[new_v[n] for n in WEIGHTS])
```
